```python
import math
import jax, jax.numpy as jnp
from jax import lax
import numpy as np

D_MODEL = 1024
BATCH = 8
SEQ = 2048
DEPTH = 4

CHUNK = 64
N_MIXERS = 3
N_LAYERS_A = len(range(0, DEPTH, N_MIXERS))
N_LAYERS_B = len(range(1, DEPTH, N_MIXERS))
N_LAYERS_C = len(range(2, DEPTH, N_MIXERS))
RMS_EPS = 1e-6
D_FF = 4 * D_MODEL
CONV_W = 4

GDN_DK = 128
GDN_DV = 128
GDN_HEADS = D_MODEL // GDN_DK
GDN_QK = GDN_HEADS * GDN_DK
GDN_V = GDN_HEADS * GDN_DV
GDN_CONV_CH = 2 * GDN_QK + GDN_V
GDN_IN = GDN_CONV_CH + GDN_V + 2 * GDN_HEADS

S5_GROUP = 16
S5_GROUPS = D_MODEL // S5_GROUP
S5_STATE = 64

M2_INNER = 2 * D_MODEL
M2_HEAD_DIM = 64
M2_HEADS = M2_INNER // M2_HEAD_DIM
M2_GROUPS = 8
M2_HPG = M2_HEADS // M2_GROUPS
M2_STATE = 128
M2_BC = M2_GROUPS * M2_STATE
M2_CONV_CH = M2_INNER + 2 * M2_BC
M2_IN = M2_INNER + M2_CONV_CH + M2_HEADS

kernel_name = "hybrid_gdn_s5_ssd_trunk"


def rmsnorm(x, g):
    xf = x.astype(jnp.float32)
    y = xf * lax.rsqrt(jnp.mean(xf * xf, axis=-1, keepdims=True) + RMS_EPS)
    return (y * g.astype(jnp.float32)).astype(x.dtype)


def causal_dwconv(x, w):
    return lax.conv_general_dilated(
        x, w[:, None, :], window_strides=(1,), padding=[(w.shape[0] - 1, 0)],
        dimension_numbers=("NWC", "WIO", "NWC"), feature_group_count=x.shape[-1])


def l2norm(t):
    return t * lax.rsqrt(jnp.sum(t * t, axis=-1, keepdims=True) + 1e-6)


def gated_deltanet(h, w_in, conv_w, a_log, dt_bias, o_norm_g, w_out):
    bsz, L, _ = h.shape
    nc = L // CHUNK
    f32 = jnp.float32
    proj = h @ w_in
    qkv, gate, a_raw, b_raw = jnp.split(
        proj, [GDN_CONV_CH, GDN_CONV_CH + GDN_V, GDN_CONV_CH + GDN_V + GDN_HEADS], axis=-1)
    qkv = jax.nn.silu(causal_dwconv(qkv, conv_w)).astype(f32)
    q, k, v = jnp.split(qkv, [GDN_QK, 2 * GDN_QK], axis=-1)
    q = l2norm(q.reshape(bsz, L, GDN_HEADS, GDN_DK)) * (GDN_DK ** -0.5)
    k = l2norm(k.reshape(bsz, L, GDN_HEADS, GDN_DK))
    v = v.reshape(bsz, L, GDN_HEADS, GDN_DV)
    g = -jnp.exp(a_log.astype(f32)) * jax.nn.softplus(a_raw.astype(f32) + dt_bias.astype(f32))
    beta = jax.nn.sigmoid(b_raw.astype(f32))

    def to_chunks(t):
        return t.reshape(bsz, nc, CHUNK, GDN_HEADS, -1).transpose(0, 3, 1, 2, 4)

    qc, kc, vc = to_chunks(q), to_chunks(k), to_chunks(v)
    gc = g.reshape(bsz, nc, CHUNK, GDN_HEADS).transpose(0, 3, 1, 2)
    bc = beta.reshape(bsz, nc, CHUNK, GDN_HEADS).transpose(0, 3, 1, 2)
    G = jnp.cumsum(gc, axis=-1)
    idx = jnp.arange(CHUNK)
    causal = idx[:, None] >= idx[None, :]
    strict = idx[:, None] > idx[None, :]
    decay = jnp.exp(jnp.where(causal, G[..., :, None] - G[..., None, :], -jnp.inf))
    kk = jnp.einsum('bhcid,bhcjd->bhcij', kc, kc)
    tri = jnp.where(strict, bc[..., :, None] * kk * decay, 0.0) + jnp.eye(CHUNK, dtype=f32)
    rhs = jnp.concatenate([vc * bc[..., None], kc * (bc * jnp.exp(G))[..., None]], axis=-1)
    sol = lax.linalg.triangular_solve(tri, rhs, left_side=True, lower=True, unit_diagonal=True)
    u, w = sol[..., :GDN_DV], sol[..., GDN_DV:]
    qk = jnp.einsum('bhcid,bhcjd->bhcij', qc, kc) * decay
    q_dec = qc * jnp.exp(G)[..., None]
    k_dec = kc * jnp.exp(G[..., -1:] - G)[..., None]
    chunk_decay = jnp.exp(G[..., -1])
    xs = tuple(jnp.moveaxis(t, 2, 0) for t in (u, w, qk, q_dec, k_dec, chunk_decay))

    def step(S, inp):
        u_c, w_c, qk_c, qd_c, kd_c, cd_c = inp
        v_new = u_c - jnp.einsum('bhid,bhde->bhie', w_c, S)
        o = jnp.einsum('bhid,bhde->bhie', qd_c, S) + jnp.einsum('bhij,bhje->bhie', qk_c, v_new)
        S = cd_c[..., None, None] * S + jnp.einsum('bhid,bhie->bhde', kd_c, v_new)
        return S, o

    S0 = jnp.zeros((bsz, GDN_HEADS, GDN_DK, GDN_DV), f32)
    _, o = lax.scan(step, S0, xs)
    o = o.transpose(1, 0, 3, 2, 4).reshape(bsz, L, GDN_HEADS, GDN_DV)
    o = rmsnorm(o, o_norm_g) * jax.nn.silu(gate.astype(f32).reshape(bsz, L, GDN_HEADS, GDN_DV))
    return o.reshape(bsz, L, GDN_V).astype(h.dtype) @ w_out


def s5_mixer(h, w_in, lam_re, lam_im, log_dt, b_re, b_im, c_re, c_im, d_skip, w_out):
    bsz, L, _ = h.shape
    f32 = jnp.float32
    u = (h @ w_in).astype(f32)
    ug = u.reshape(bsz, L, S5_GROUPS, S5_GROUP).transpose(1, 0, 2, 3)
    lam = lax.complex(lam_re.astype(f32), lam_im.astype(f32))
    dt = jnp.exp(log_dt.astype(f32))[:, None]
    lam_bar = jnp.exp(lam * dt)
    b_bar = ((lam_bar - 1.0) / lam)[..., None] * lax.complex(b_re.astype(f32), b_im.astype(f32))
    bu = jnp.einsum('gpk,lbgk->lbgp', b_bar, ug.astype(jnp.complex64))
    a = jnp.broadcast_to(lam_bar[None, None], (L, 1, S5_GROUPS, S5_STATE))

    def combine(e1, e2):
        a1, b1 = e1
        a2, b2 = e2
        return a1 * a2, a2 * b1 + b2

    _, states = lax.associative_scan(combine, (a, bu), axis=0)
    c = lax.complex(c_re.astype(f32), c_im.astype(f32))
    y = jnp.real(jnp.einsum('gkp,lbgp->lbgk', c, states)) \
        + d_skip.astype(f32).reshape(S5_GROUPS, S5_GROUP) * ug
    y = jax.nn.gelu(y.transpose(1, 0, 2, 3).reshape(bsz, L, D_MODEL)).astype(h.dtype)
    ag = y @ w_out
    val, gt = jnp.split(ag, 2, axis=-1)
    return val * jax.nn.sigmoid(gt)


def mamba2_mixer(h, w_in, conv_w, conv_b, dt_bias, a_log, d_skip, norm_g, w_out):
    bsz, L, _ = h.shape
    nc = L // CHUNK
    f32 = jnp.float32
    proj = h @ w_in
    z, xbc, dt_raw = jnp.split(proj, [M2_INNER, M2_INNER + M2_CONV_CH], axis=-1)
    xbc = jax.nn.silu(causal_dwconv(xbc, conv_w) + conv_b).astype(f32)
    xs, Bm, Cm = jnp.split(xbc, [M2_INNER, M2_INNER + M2_BC], axis=-1)
    x = xs.reshape(bsz, L, M2_HEADS, M2_HEAD_DIM)
    dt = jax.nn.softplus(dt_raw.astype(f32) + dt_bias.astype(f32))
    dA = dt * (-jnp.exp(a_log.astype(f32)))
    xdt = (x * dt[..., None]).reshape(bsz, nc, CHUNK, M2_GROUPS, M2_HPG, M2_HEAD_DIM)
    Bc = Bm.reshape(bsz, nc, CHUNK, M2_GROUPS, M2_STATE)
    Cc = Cm.reshape(bsz, nc, CHUNK, M2_GROUPS, M2_STATE)
    cum = jnp.cumsum(dA.reshape(bsz, nc, CHUNK, M2_GROUPS, M2_HPG), axis=2)
    idx = jnp.arange(CHUNK)
    causal = (idx[:, None] >= idx[None, :])[:, :, None, None]
    seg = cum[:, :, :, None] - cum[:, :, None, :]
    Lmat = jnp.exp(jnp.where(causal, seg, -jnp.inf))
    cb = jnp.einsum('bclgn,bcsgn->bclsg', Cc, Bc)
    y_diag = jnp.einsum('bclsgr,bcsgrp->bclgrp', cb[..., None] * Lmat, xdt)
    decay_states = jnp.exp(cum[:, :, -1:] - cum)
    states = jnp.einsum('bclgn,bclgrp->bcgrpn', Bc, xdt * decay_states[..., None])
    chunk_decay = jnp.exp(cum[:, :, -1])

    def step(S, inp):
        cd, st = inp
        return cd[..., None, None] * S + st, S

    S0 = jnp.zeros((bsz, M2_GROUPS, M2_HPG, M2_HEAD_DIM, M2_STATE), f32)
    _, S_prev = lax.scan(step, S0, (jnp.moveaxis(chunk_decay, 1, 0), jnp.moveaxis(states, 1, 0)))
    S_prev = jnp.moveaxis(S_prev, 0, 1)
    y_off = jnp.einsum('bclgn,bcgrpn->bclgrp', Cc, S_prev) * jnp.exp(cum)[..., None]
    y = (y_diag + y_off).reshape(bsz, L, M2_HEADS, M2_HEAD_DIM) + d_skip.astype(f32)[:, None] * x
    y = y.reshape(bsz, L, M2_INNER) * jax.nn.silu(z.astype(f32))
    y = rmsnorm(y.reshape(bsz, L, M2_GROUPS, M2_INNER // M2_GROUPS),
                norm_g.reshape(M2_GROUPS, M2_INNER // M2_GROUPS))
    return y.reshape(bsz, L, M2_INNER).astype(h.dtype) @ w_out


def sq_relu_mlp(h, w1, w2):
    return jnp.square(jax.nn.relu(h @ w1)) @ w2


def _inv_softplus_dt(key, shape):
    dt = jnp.exp(jax.random.uniform(key, shape, minval=math.log(1e-3), maxval=math.log(1e-1)))
    return dt + jnp.log(-jnp.expm1(-dt))


def _fwd_setup_inputs(seed: int = 0) -> dict:
    key = jax.random.key(seed)
    ks = jax.random.split(key, 32)
    nrm = jax.random.normal
    f32 = jnp.float32
    nA, nB, nC = N_LAYERS_A, N_LAYERS_B, N_LAYERS_C
    return {
        "x": nrm(ks[0], (BATCH, SEQ, D_MODEL), f32),
        "norm_mix_g": 1.0 + 0.02 * nrm(ks[1], (DEPTH, D_MODEL), f32),
        "norm_mlp_g": 1.0 + 0.02 * nrm(ks[2], (DEPTH, D_MODEL), f32),
        "mlp_w1": nrm(ks[3], (DEPTH, D_MODEL, D_FF), f32) * D_MODEL ** -0.5,
        "mlp_w2": nrm(ks[4], (DEPTH, D_FF, D_MODEL), f32) * D_FF ** -0.5,
        "gdn_w_in": nrm(ks[5], (nA, D_MODEL, GDN_IN), f32) * D_MODEL ** -0.5,
        "gdn_conv_w": nrm(ks[6], (nA, CONV_W, GDN_CONV_CH), f32) * CONV_W ** -0.5,
        "gdn_a_log": jnp.log(jax.random.uniform(ks[7], (nA, GDN_HEADS), minval=1.0, maxval=16.0)),
        "gdn_dt_bias": _inv_softplus_dt(ks[8], (nA, GDN_HEADS)),
        "gdn_o_norm_g": 1.0 + 0.02 * nrm(ks[9], (nA, GDN_DV), f32),
        "gdn_w_out": nrm(ks[10], (nA, GDN_V, D_MODEL), f32) * GDN_V ** -0.5,
        "s5_w_in": nrm(ks[11], (nB, D_MODEL, D_MODEL), f32) * D_MODEL ** -0.5,
        "s5_lam_re": -0.5 + 0.01 * nrm(ks[12], (nB, S5_GROUPS, S5_STATE), f32),
        "s5_lam_im": math.pi * jnp.arange(S5_STATE, dtype=f32) + 0.01 * nrm(ks[13], (nB, S5_GROUPS, S5_STATE), f32),
        "s5_log_dt": jax.random.uniform(ks[14], (nB, S5_GROUPS), minval=math.log(1e-3), maxval=math.log(1e-1)),
        "s5_b_re": nrm(ks[15], (nB, S5_GROUPS, S5_STATE, S5_GROUP), f32) * (2 * S5_GROUP) ** -0.5,
        "s5_b_im": nrm(ks[16], (nB, S5_GROUPS, S5_STATE, S5_GROUP), f32) * (2 * S5_GROUP) ** -0.5,
        "s5_c_re": nrm(ks[17], (nB, S5_GROUPS, S5_GROUP, S5_STATE), f32) * (2 * S5_STATE) ** -0.5 * 4.0,
        "s5_c_im": nrm(ks[18], (nB, S5_GROUPS, S5_GROUP, S5_STATE), f32) * (2 * S5_STATE) ** -0.5 * 4.0,
        "s5_d": nrm(ks[19], (nB, D_MODEL), f32),
        "s5_w_out": nrm(ks[20], (nB, D_MODEL, 2 * D_MODEL), f32) * D_MODEL ** -0.5,
        "m2_w_in": nrm(ks[21], (nC, D_MODEL, M2_IN), f32) * D_MODEL ** -0.5,
        "m2_conv_w": nrm(ks[22], (nC, CONV_W, M2_CONV_CH), f32) * CONV_W ** -0.5,
        "m2_conv_b": 0.02 * nrm(ks[23], (nC, M2_CONV_CH), f32),
        "m2_dt_bias": _inv_softplus_dt(ks[24], (nC, M2_HEADS)),
        "m2_a_log": jnp.log(jax.random.uniform(ks[25], (nC, M2_HEADS), minval=1.0, maxval=16.0)),
        "m2_d": 1.0 + 0.02 * nrm(ks[26], (nC, M2_HEADS), f32),
        "m2_norm_g": 1.0 + 0.02 * nrm(ks[27], (nC, M2_INNER), f32),
        "m2_w_out": nrm(ks[28], (nC, M2_INNER, D_MODEL), f32) * M2_INNER ** -0.5,
        "final_norm_g": 1.0 + 0.02 * nrm(ks[29], (D_MODEL,), f32),
    }


def _fwd_reference(x, norm_mix_g, norm_mlp_g, mlp_w1, mlp_w2,
              gdn_w_in, gdn_conv_w, gdn_a_log, gdn_dt_bias, gdn_o_norm_g, gdn_w_out,
              s5_w_in, s5_lam_re, s5_lam_im, s5_log_dt, s5_b_re, s5_b_im, s5_c_re, s5_c_im, s5_d, s5_w_out,
              m2_w_in, m2_conv_w, m2_conv_b, m2_dt_bias, m2_a_log, m2_d, m2_norm_g, m2_w_out,
              final_norm_g):
    h = x
    for i in range(DEPTH):
        kind, j = i % N_MIXERS, i // N_MIXERS
        hn = rmsnorm(h, norm_mix_g[i])
        if kind == 0:
            m = gated_deltanet(hn, gdn_w_in[j], gdn_conv_w[j], gdn_a_log[j], gdn_dt_bias[j],
                               gdn_o_norm_g[j], gdn_w_out[j])
        elif kind == 1:
            m = s5_mixer(hn, s5_w_in[j], s5_lam_re[j], s5_lam_im[j], s5_log_dt[j], s5_b_re[j],
                         s5_b_im[j], s5_c_re[j], s5_c_im[j], s5_d[j], s5_w_out[j])
        else:
            m = mamba2_mixer(hn, m2_w_in[j], m2_conv_w[j], m2_conv_b[j], m2_dt_bias[j], m2_a_log[j],
                             m2_d[j], m2_norm_g[j], m2_w_out[j])
        h = h + m.astype(h.dtype)
        h = h + sq_relu_mlp(rmsnorm(h, norm_mlp_g[i]), mlp_w1[i], mlp_w2[i]).astype(h.dtype)
    return rmsnorm(h, final_norm_g)


import jax as _jax
import jax.numpy as _jnp

TWIN_FORMAT = 'train_step'
FWD_PARAMS = ['x', 'norm_mix_g', 'norm_mlp_g', 'mlp_w1', 'mlp_w2', 'gdn_w_in', 'gdn_conv_w', 'gdn_a_log', 'gdn_dt_bias', 'gdn_o_norm_g', 'gdn_w_out', 's5_w_in', 's5_lam_re', 's5_lam_im', 's5_log_dt', 's5_b_re', 's5_b_im', 's5_c_re', 's5_c_im', 's5_d', 's5_w_out', 'm2_w_in', 'm2_conv_w', 'm2_conv_b', 'm2_dt_bias', 'm2_a_log', 'm2_d', 'm2_norm_g', 'm2_w_out', 'final_norm_g']
TWIN_WEIGHTS = ['norm_mix_g', 'norm_mlp_g', 'mlp_w1', 'mlp_w2', 'gdn_w_in', 'gdn_conv_w', 'gdn_a_log', 'gdn_dt_bias', 'gdn_o_norm_g', 'gdn_w_out', 's5_w_in', 's5_lam_re', 's5_lam_im', 's5_log_dt', 's5_b_re', 's5_b_im', 's5_c_re', 's5_c_im', 's5_d', 's5_w_out', 'm2_w_in', 'm2_conv_w', 'm2_conv_b', 'm2_dt_bias', 'm2_a_log', 'm2_d', 'm2_norm_g', 'm2_w_out', 'final_norm_g']
TWIN_DIFF_INPUT = 'x'
TWIN_INPUTS = ['x', 'norm_mix_g', 'norm_mlp_g', 'mlp_w1', 'mlp_w2', 'gdn_w_in', 'gdn_conv_w', 'gdn_a_log', 'gdn_dt_bias', 'gdn_o_norm_g', 'gdn_w_out', 's5_w_in', 's5_lam_re', 's5_lam_im', 's5_log_dt', 's5_b_re', 's5_b_im', 's5_c_re', 's5_c_im', 's5_d', 's5_w_out', 'm2_w_in', 'm2_conv_w', 'm2_conv_b', 'm2_dt_bias', 'm2_a_log', 'm2_d', 'm2_norm_g', 'm2_w_out', 'final_norm_g', 'loss_target', 'm_norm_mix_g', 'm_norm_mlp_g', 'm_mlp_w1', 'm_mlp_w2', 'm_gdn_w_in', 'm_gdn_conv_w', 'm_gdn_a_log', 'm_gdn_dt_bias', 'm_gdn_o_norm_g', 'm_gdn_w_out', 'm_s5_w_in', 'm_s5_lam_re', 'm_s5_lam_im', 'm_s5_log_dt', 'm_s5_b_re', 'm_s5_b_im', 'm_s5_c_re', 'm_s5_c_im', 'm_s5_d', 'm_s5_w_out', 'm_m2_w_in', 'm_m2_conv_w', 'm_m2_conv_b', 'm_m2_dt_bias', 'm_m2_a_log', 'm_m2_d', 'm_m2_norm_g', 'm_m2_w_out', 'm_final_norm_g', 'v_norm_mix_g', 'v_norm_mlp_g', 'v_mlp_w1', 'v_mlp_w2', 'v_gdn_w_in', 'v_gdn_conv_w', 'v_gdn_a_log', 'v_gdn_dt_bias', 'v_gdn_o_norm_g', 'v_gdn_w_out', 'v_s5_w_in', 'v_s5_lam_re', 'v_s5_lam_im', 'v_s5_log_dt', 'v_s5_b_re', 'v_s5_b_im', 'v_s5_c_re', 'v_s5_c_im', 'v_s5_d', 'v_s5_w_out', 'v_m2_w_in', 'v_m2_conv_w', 'v_m2_conv_b', 'v_m2_dt_bias', 'v_m2_a_log', 'v_m2_d', 'v_m2_norm_g', 'v_m2_w_out', 'v_final_norm_g']
TWIN_OUTPUTS = ['loss', 'grad_x', 'grad_norm_mix_g', 'grad_norm_mlp_g', 'grad_mlp_w1', 'grad_mlp_w2', 'grad_gdn_w_in', 'grad_gdn_conv_w', 'grad_gdn_a_log', 'grad_gdn_dt_bias', 'grad_gdn_o_norm_g', 'grad_gdn_w_out', 'grad_s5_w_in', 'grad_s5_lam_re', 'grad_s5_lam_im', 'grad_s5_log_dt', 'grad_s5_b_re', 'grad_s5_b_im', 'grad_s5_c_re', 'grad_s5_c_im', 'grad_s5_d', 'grad_s5_w_out', 'grad_m2_w_in', 'grad_m2_conv_w', 'grad_m2_conv_b', 'grad_m2_dt_bias', 'grad_m2_a_log', 'grad_m2_d', 'grad_m2_norm_g', 'grad_m2_w_out', 'grad_final_norm_g', 'delta_norm_mix_g', 'delta_norm_mlp_g', 'delta_mlp_w1', 'delta_mlp_w2', 'delta_gdn_w_in', 'delta_gdn_conv_w', 'delta_gdn_a_log', 'delta_gdn_dt_bias', 'delta_gdn_o_norm_g', 'delta_gdn_w_out', 'delta_s5_w_in', 'delta_s5_lam_re', 'delta_s5_lam_im', 'delta_s5_log_dt', 'delta_s5_b_re', 'delta_s5_b_im', 'delta_s5_c_re', 'delta_s5_c_im', 'delta_s5_d', 'delta_s5_w_out', 'delta_m2_w_in', 'delta_m2_conv_w', 'delta_m2_conv_b', 'delta_m2_dt_bias', 'delta_m2_a_log', 'delta_m2_d', 'delta_m2_norm_g', 'delta_m2_w_out', 'delta_final_norm_g', 'new_m_norm_mix_g', 'new_m_norm_mlp_g', 'new_m_mlp_w1', 'new_m_mlp_w2', 'new_m_gdn_w_in', 'new_m_gdn_conv_w', 'new_m_gdn_a_log', 'new_m_gdn_dt_bias', 'new_m_gdn_o_norm_g', 'new_m_gdn_w_out', 'new_m_s5_w_in', 'new_m_s5_lam_re', 'new_m_s5_lam_im', 'new_m_s5_log_dt', 'new_m_s5_b_re', 'new_m_s5_b_im', 'new_m_s5_c_re', 'new_m_s5_c_im', 'new_m_s5_d', 'new_m_s5_w_out', 'new_m_m2_w_in', 'new_m_m2_conv_w', 'new_m_m2_conv_b', 'new_m_m2_dt_bias', 'new_m_m2_a_log', 'new_m_m2_d', 'new_m_m2_norm_g', 'new_m_m2_w_out', 'new_m_final_norm_g', 'new_v_norm_mix_g', 'new_v_norm_mlp_g', 'new_v_mlp_w1', 'new_v_mlp_w2', 'new_v_gdn_w_in', 'new_v_gdn_conv_w', 'new_v_gdn_a_log', 'new_v_gdn_dt_bias', 'new_v_gdn_o_norm_g', 'new_v_gdn_w_out', 'new_v_s5_w_in', 'new_v_s5_lam_re', 'new_v_s5_lam_im', 'new_v_s5_log_dt', 'new_v_s5_b_re', 'new_v_s5_b_im', 'new_v_s5_c_re', 'new_v_s5_c_im', 'new_v_s5_d', 'new_v_s5_w_out', 'new_v_m2_w_in', 'new_v_m2_conv_w', 'new_v_m2_conv_b', 'new_v_m2_dt_bias', 'new_v_m2_a_log', 'new_v_m2_d', 'new_v_m2_norm_g', 'new_v_m2_w_out', 'new_v_final_norm_g']
TWIN_LEAF_KINDS = {'loss': 'loss', 'grad_x': 'grad_x', 'grad_norm_mix_g': 'grad_w', 'grad_norm_mlp_g': 'grad_w', 'grad_mlp_w1': 'grad_w', 'grad_mlp_w2': 'grad_w', 'grad_gdn_w_in': 'grad_w', 'grad_gdn_conv_w': 'grad_w', 'grad_gdn_a_log': 'grad_w', 'grad_gdn_dt_bias': 'grad_w', 'grad_gdn_o_norm_g': 'grad_w', 'grad_gdn_w_out': 'grad_w', 'grad_s5_w_in': 'grad_w', 'grad_s5_lam_re': 'grad_w', 'grad_s5_lam_im': 'grad_w', 'grad_s5_log_dt': 'grad_w', 'grad_s5_b_re': 'grad_w', 'grad_s5_b_im': 'grad_w', 'grad_s5_c_re': 'grad_w', 'grad_s5_c_im': 'grad_w', 'grad_s5_d': 'grad_w', 'grad_s5_w_out': 'grad_w', 'grad_m2_w_in': 'grad_w', 'grad_m2_conv_w': 'grad_w', 'grad_m2_conv_b': 'grad_w', 'grad_m2_dt_bias': 'grad_w', 'grad_m2_a_log': 'grad_w', 'grad_m2_d': 'grad_w', 'grad_m2_norm_g': 'grad_w', 'grad_m2_w_out': 'grad_w', 'grad_final_norm_g': 'grad_w', 'delta_norm_mix_g': 'delta_w', 'delta_norm_mlp_g': 'delta_w', 'delta_mlp_w1': 'delta_w', 'delta_mlp_w2': 'delta_w', 'delta_gdn_w_in': 'delta_w', 'delta_gdn_conv_w': 'delta_w', 'delta_gdn_a_log': 'delta_w', 'delta_gdn_dt_bias': 'delta_w', 'delta_gdn_o_norm_g': 'delta_w', 'delta_gdn_w_out': 'delta_w', 'delta_s5_w_in': 'delta_w', 'delta_s5_lam_re': 'delta_w', 'delta_s5_lam_im': 'delta_w', 'delta_s5_log_dt': 'delta_w', 'delta_s5_b_re': 'delta_w', 'delta_s5_b_im': 'delta_w', 'delta_s5_c_re': 'delta_w', 'delta_s5_c_im': 'delta_w', 'delta_s5_d': 'delta_w', 'delta_s5_w_out': 'delta_w', 'delta_m2_w_in': 'delta_w', 'delta_m2_conv_w': 'delta_w', 'delta_m2_conv_b': 'delta_w', 'delta_m2_dt_bias': 'delta_w', 'delta_m2_a_log': 'delta_w', 'delta_m2_d': 'delta_w', 'delta_m2_norm_g': 'delta_w', 'delta_m2_w_out': 'delta_w', 'delta_final_norm_g': 'delta_w', 'new_m_norm_mix_g': 'new_m', 'new_m_norm_mlp_g': 'new_m', 'new_m_mlp_w1': 'new_m', 'new_m_mlp_w2': 'new_m', 'new_m_gdn_w_in': 'new_m', 'new_m_gdn_conv_w': 'new_m', 'new_m_gdn_a_log': 'new_m', 'new_m_gdn_dt_bias': 'new_m', 'new_m_gdn_o_norm_g': 'new_m', 'new_m_gdn_w_out': 'new_m', 'new_m_s5_w_in': 'new_m', 'new_m_s5_lam_re': 'new_m', 'new_m_s5_lam_im': 'new_m', 'new_m_s5_log_dt': 'new_m', 'new_m_s5_b_re': 'new_m', 'new_m_s5_b_im': 'new_m', 'new_m_s5_c_re': 'new_m', 'new_m_s5_c_im': 'new_m', 'new_m_s5_d': 'new_m', 'new_m_s5_w_out': 'new_m', 'new_m_m2_w_in': 'new_m', 'new_m_m2_conv_w': 'new_m', 'new_m_m2_conv_b': 'new_m', 'new_m_m2_dt_bias': 'new_m', 'new_m_m2_a_log': 'new_m', 'new_m_m2_d': 'new_m', 'new_m_m2_norm_g': 'new_m', 'new_m_m2_w_out': 'new_m', 'new_m_final_norm_g': 'new_m', 'new_v_norm_mix_g': 'new_v', 'new_v_norm_mlp_g': 'new_v', 'new_v_mlp_w1': 'new_v', 'new_v_mlp_w2': 'new_v', 'new_v_gdn_w_in': 'new_v', 'new_v_gdn_conv_w': 'new_v', 'new_v_gdn_a_log': 'new_v', 'new_v_gdn_dt_bias': 'new_v', 'new_v_gdn_o_norm_g': 'new_v', 'new_v_gdn_w_out': 'new_v', 'new_v_s5_w_in': 'new_v', 'new_v_s5_lam_re': 'new_v', 'new_v_s5_lam_im': 'new_v', 'new_v_s5_log_dt': 'new_v', 'new_v_s5_b_re': 'new_v', 'new_v_s5_b_im': 'new_v', 'new_v_s5_c_re': 'new_v', 'new_v_s5_c_im': 'new_v', 'new_v_s5_d': 'new_v', 'new_v_s5_w_out': 'new_v', 'new_v_m2_w_in': 'new_v', 'new_v_m2_conv_w': 'new_v', 'new_v_m2_conv_b': 'new_v', 'new_v_m2_dt_bias': 'new_v', 'new_v_m2_a_log': 'new_v', 'new_v_m2_d': 'new_v', 'new_v_m2_norm_g': 'new_v', 'new_v_m2_w_out': 'new_v', 'new_v_final_norm_g': 'new_v'}


def _forward(args):
    return _fwd_reference(*[args[k] for k in FWD_PARAMS])


def _output_shape():
    out = _jax.eval_shape(lambda: _forward(_fwd_setup_inputs(0)))
    return out.shape, out.dtype

N_MICROBATCH = 1
ADAM_LR = 0.001
ADAM_B1 = 0.9
ADAM_B2 = 0.999
ADAM_EPS = 1e-08
ADAM_WD = 0.01
ADAM_STEP = 10
PER_EXAMPLE_BATCH_AXIS = {'x': 0, 'loss_target': 0}
SHARED_INPUTS = []
_WEIGHT_DTYPES = {'norm_mix_g': _jnp.float32, 'norm_mlp_g': _jnp.float32, 'mlp_w1': _jnp.float32, 'mlp_w2': _jnp.float32, 'gdn_w_in': _jnp.float32, 'gdn_conv_w': _jnp.float32, 'gdn_a_log': _jnp.float32, 'gdn_dt_bias': _jnp.float32, 'gdn_o_norm_g': _jnp.float32, 'gdn_w_out': _jnp.float32, 's5_w_in': _jnp.float32, 's5_lam_re': _jnp.float32, 's5_lam_im': _jnp.float32, 's5_log_dt': _jnp.float32, 's5_b_re': _jnp.float32, 's5_b_im': _jnp.float32, 's5_c_re': _jnp.float32, 's5_c_im': _jnp.float32, 's5_d': _jnp.float32, 's5_w_out': _jnp.float32, 'm2_w_in': _jnp.float32, 'm2_conv_w': _jnp.float32, 'm2_conv_b': _jnp.float32, 'm2_dt_bias': _jnp.float32, 'm2_a_log': _jnp.float32, 'm2_d': _jnp.float32, 'm2_norm_g': _jnp.float32, 'm2_w_out': _jnp.float32, 'final_norm_g': _jnp.float32}
MOMENT_SCALE = {'norm_mix_g': 9.349665e-02, 'norm_mlp_g': 1.003088e-01, 'mlp_w1': 4.970894e-02, 'mlp_w2': 9.313434e-02, 'gdn_w_in': 5.688244e-02, 'gdn_conv_w': 5.434398e-02, 'gdn_a_log': 3.364942e-01, 'gdn_dt_bias': 3.283687e-01, 'gdn_o_norm_g': 2.050481e-01, 'gdn_w_out': 6.830080e-02, 's5_w_in': 3.811451e-02, 's5_lam_re': 1.048132e-02, 's5_lam_im': 1.256014e-02, 's5_log_dt': 4.994287e+00, 's5_b_re': 6.891111e-03, 's5_b_im': 7.062623e-03, 's5_c_re': 3.037019e-03, 's5_c_im': 2.996141e-03, 's5_d': 3.932513e-02, 's5_w_out': 2.742872e-02, 'm2_w_in': 3.751679e-02, 'm2_conv_w': 3.261909e-02, 'm2_conv_b': 4.154263e-02, 'm2_dt_bias': 6.755557e-02, 'm2_a_log': 1.007341e-01, 'm2_d': 2.215989e-01, 'm2_norm_g': 4.430735e-02, 'm2_w_out': 6.264236e-02, 'final_norm_g': 1.634779e+01}


def _to_microbatches(a, axis):
    t = _jnp.moveaxis(a, axis, 0)
    t = t.reshape((N_MICROBATCH, t.shape[0] // N_MICROBATCH) + t.shape[1:])
    return _jnp.moveaxis(t, 1, axis + 1)


def setup_inputs(seed: int = 0) -> dict:
    inp = _fwd_setup_inputs(seed)
    key = _jax.random.fold_in(_jax.random.key(seed), 7919)
    shape, _ = _output_shape()
    out = dict(inp)
    out["loss_target"] = _jax.random.normal(_jax.random.fold_in(key, 0), shape, _jnp.float32)
    for i, name in enumerate(TWIN_WEIGHTS):
        w = inp[name].astype(_jnp.float32)
        if MOMENT_SCALE is None:
            s = _jnp.sqrt(_jnp.mean(_jnp.square(w)) + 1e-30)
        else:
            s = MOMENT_SCALE[name]
        km, kv = _jax.random.split(_jax.random.fold_in(key, i + 1))
        out[name] = w
        out["m_" + name] = s * _jax.random.normal(km, w.shape, _jnp.float32)
        out["v_" + name] = (s * s) * _jax.random.uniform(kv, w.shape, _jnp.float32, 0.5, 1.5)
    if N_MICROBATCH > 1:
        for name, axis in PER_EXAMPLE_BATCH_AXIS.items():
            out[name] = _to_microbatches(out[name], axis)
    return {'x': out['x'], 'norm_mix_g': out['norm_mix_g'], 'norm_mlp_g': out['norm_mlp_g'], 'mlp_w1': out['mlp_w1'], 'mlp_w2': out['mlp_w2'], 'gdn_w_in': out['gdn_w_in'], 'gdn_conv_w': out['gdn_conv_w'], 'gdn_a_log': out['gdn_a_log'], 'gdn_dt_bias': out['gdn_dt_bias'], 'gdn_o_norm_g': out['gdn_o_norm_g'], 'gdn_w_out': out['gdn_w_out'], 's5_w_in': out['s5_w_in'], 's5_lam_re': out['s5_lam_re'], 's5_lam_im': out['s5_lam_im'], 's5_log_dt': out['s5_log_dt'], 's5_b_re': out['s5_b_re'], 's5_b_im': out['s5_b_im'], 's5_c_re': out['s5_c_re'], 's5_c_im': out['s5_c_im'], 's5_d': out['s5_d'], 's5_w_out': out['s5_w_out'], 'm2_w_in': out['m2_w_in'], 'm2_conv_w': out['m2_conv_w'], 'm2_conv_b': out['m2_conv_b'], 'm2_dt_bias': out['m2_dt_bias'], 'm2_a_log': out['m2_a_log'], 'm2_d': out['m2_d'], 'm2_norm_g': out['m2_norm_g'], 'm2_w_out': out['m2_w_out'], 'final_norm_g': out['final_norm_g'], 'loss_target': out['loss_target'], 'm_norm_mix_g': out['m_norm_mix_g'], 'm_norm_mlp_g': out['m_norm_mlp_g'], 'm_mlp_w1': out['m_mlp_w1'], 'm_mlp_w2': out['m_mlp_w2'], 'm_gdn_w_in': out['m_gdn_w_in'], 'm_gdn_conv_w': out['m_gdn_conv_w'], 'm_gdn_a_log': out['m_gdn_a_log'], 'm_gdn_dt_bias': out['m_gdn_dt_bias'], 'm_gdn_o_norm_g': out['m_gdn_o_norm_g'], 'm_gdn_w_out': out['m_gdn_w_out'], 'm_s5_w_in': out['m_s5_w_in'], 'm_s5_lam_re': out['m_s5_lam_re'], 'm_s5_lam_im': out['m_s5_lam_im'], 'm_s5_log_dt': out['m_s5_log_dt'], 'm_s5_b_re': out['m_s5_b_re'], 'm_s5_b_im': out['m_s5_b_im'], 'm_s5_c_re': out['m_s5_c_re'], 'm_s5_c_im': out['m_s5_c_im'], 'm_s5_d': out['m_s5_d'], 'm_s5_w_out': out['m_s5_w_out'], 'm_m2_w_in': out['m_m2_w_in'], 'm_m2_conv_w': out['m_m2_conv_w'], 'm_m2_conv_b': out['m_m2_conv_b'], 'm_m2_dt_bias': out['m_m2_dt_bias'], 'm_m2_a_log': out['m_m2_a_log'], 'm_m2_d': out['m_m2_d'], 'm_m2_norm_g': out['m_m2_norm_g'], 'm_m2_w_out': out['m_m2_w_out'], 'm_final_norm_g': out['m_final_norm_g'], 'v_norm_mix_g': out['v_norm_mix_g'], 'v_norm_mlp_g': out['v_norm_mlp_g'], 'v_mlp_w1': out['v_mlp_w1'], 'v_mlp_w2': out['v_mlp_w2'], 'v_gdn_w_in': out['v_gdn_w_in'], 'v_gdn_conv_w': out['v_gdn_conv_w'], 'v_gdn_a_log': out['v_gdn_a_log'], 'v_gdn_dt_bias': out['v_gdn_dt_bias'], 'v_gdn_o_norm_g': out['v_gdn_o_norm_g'], 'v_gdn_w_out': out['v_gdn_w_out'], 'v_s5_w_in': out['v_s5_w_in'], 'v_s5_lam_re': out['v_s5_lam_re'], 'v_s5_lam_im': out['v_s5_lam_im'], 'v_s5_log_dt': out['v_s5_log_dt'], 'v_s5_b_re': out['v_s5_b_re'], 'v_s5_b_im': out['v_s5_b_im'], 'v_s5_c_re': out['v_s5_c_re'], 'v_s5_c_im': out['v_s5_c_im'], 'v_s5_d': out['v_s5_d'], 'v_s5_w_out': out['v_s5_w_out'], 'v_m2_w_in': out['v_m2_w_in'], 'v_m2_conv_w': out['v_m2_conv_w'], 'v_m2_conv_b': out['v_m2_conv_b'], 'v_m2_dt_bias': out['v_m2_dt_bias'], 'v_m2_a_log': out['v_m2_a_log'], 'v_m2_d': out['v_m2_d'], 'v_m2_norm_g': out['v_m2_norm_g'], 'v_m2_w_out': out['v_m2_w_out'], 'v_final_norm_g': out['v_final_norm_g']}


def _loss(weights, diff, rest, loss_target):
    with _jax.named_scope("forward"):
        args = {**rest, TWIN_DIFF_INPUT: diff, **{k: w.astype(_WEIGHT_DTYPES[k]) for k, w in weights.items()}}
        y = _forward(args)
    with _jax.named_scope("loss_head"):
        err = _jnp.square(y.astype(_jnp.float32) - loss_target)
        return 0.5 * _jnp.sum(_jnp.mean(err, axis=-1)) if err.ndim else 0.5 * err


def _adamw(w, g, m, v):
    m = ADAM_B1 * m + (1.0 - ADAM_B1) * g
    v = ADAM_B2 * v + (1.0 - ADAM_B2) * _jnp.square(g)
    m_hat = m / (1.0 - ADAM_B1 ** ADAM_STEP)
    v_hat = v / (1.0 - ADAM_B2 ** ADAM_STEP)
    delta = -ADAM_LR * (m_hat / (_jnp.sqrt(v_hat) + ADAM_EPS) + ADAM_WD * w)
    return delta, m, v


def reference(x, norm_mix_g, norm_mlp_g, mlp_w1, mlp_w2, gdn_w_in, gdn_conv_w, gdn_a_log, gdn_dt_bias, gdn_o_norm_g, gdn_w_out, s5_w_in, s5_lam_re, s5_lam_im, s5_log_dt, s5_b_re, s5_b_im, s5_c_re, s5_c_im, s5_d, s5_w_out, m2_w_in, m2_conv_w, m2_conv_b, m2_dt_bias, m2_a_log, m2_d, m2_norm_g, m2_w_out, final_norm_g, loss_target, m_norm_mix_g, m_norm_mlp_g, m_mlp_w1, m_mlp_w2, m_gdn_w_in, m_gdn_conv_w, m_gdn_a_log, m_gdn_dt_bias, m_gdn_o_norm_g, m_gdn_w_out, m_s5_w_in, m_s5_lam_re, m_s5_lam_im, m_s5_log_dt, m_s5_b_re, m_s5_b_im, m_s5_c_re, m_s5_c_im, m_s5_d, m_s5_w_out, m_m2_w_in, m_m2_conv_w, m_m2_conv_b, m_m2_dt_bias, m_m2_a_log, m_m2_d, m_m2_norm_g, m_m2_w_out, m_final_norm_g, v_norm_mix_g, v_norm_mlp_g, v_mlp_w1, v_mlp_w2, v_gdn_w_in, v_gdn_conv_w, v_gdn_a_log, v_gdn_dt_bias, v_gdn_o_norm_g, v_gdn_w_out, v_s5_w_in, v_s5_lam_re, v_s5_lam_im, v_s5_log_dt, v_s5_b_re, v_s5_b_im, v_s5_c_re, v_s5_c_im, v_s5_d, v_s5_w_out, v_m2_w_in, v_m2_conv_w, v_m2_conv_b, v_m2_dt_bias, v_m2_a_log, v_m2_d, v_m2_norm_g, v_m2_w_out, v_final_norm_g):
    given = dict(x=x, norm_mix_g=norm_mix_g, norm_mlp_g=norm_mlp_g, mlp_w1=mlp_w1, mlp_w2=mlp_w2, gdn_w_in=gdn_w_in, gdn_conv_w=gdn_conv_w, gdn_a_log=gdn_a_log, gdn_dt_bias=gdn_dt_bias, gdn_o_norm_g=gdn_o_norm_g, gdn_w_out=gdn_w_out, s5_w_in=s5_w_in, s5_lam_re=s5_lam_re, s5_lam_im=s5_lam_im, s5_log_dt=s5_log_dt, s5_b_re=s5_b_re, s5_b_im=s5_b_im, s5_c_re=s5_c_re, s5_c_im=s5_c_im, s5_d=s5_d, s5_w_out=s5_w_out, m2_w_in=m2_w_in, m2_conv_w=m2_conv_w, m2_conv_b=m2_conv_b, m2_dt_bias=m2_dt_bias, m2_a_log=m2_a_log, m2_d=m2_d, m2_norm_g=m2_norm_g, m2_w_out=m2_w_out, final_norm_g=final_norm_g, loss_target=loss_target, m_norm_mix_g=m_norm_mix_g, m_norm_mlp_g=m_norm_mlp_g, m_mlp_w1=m_mlp_w1, m_mlp_w2=m_mlp_w2, m_gdn_w_in=m_gdn_w_in, m_gdn_conv_w=m_gdn_conv_w, m_gdn_a_log=m_gdn_a_log, m_gdn_dt_bias=m_gdn_dt_bias, m_gdn_o_norm_g=m_gdn_o_norm_g, m_gdn_w_out=m_gdn_w_out, m_s5_w_in=m_s5_w_in, m_s5_lam_re=m_s5_lam_re, m_s5_lam_im=m_s5_lam_im, m_s5_log_dt=m_s5_log_dt, m_s5_b_re=m_s5_b_re, m_s5_b_im=m_s5_b_im, m_s5_c_re=m_s5_c_re, m_s5_c_im=m_s5_c_im, m_s5_d=m_s5_d, m_s5_w_out=m_s5_w_out, m_m2_w_in=m_m2_w_in, m_m2_conv_w=m_m2_conv_w, m_m2_conv_b=m_m2_conv_b, m_m2_dt_bias=m_m2_dt_bias, m_m2_a_log=m_m2_a_log, m_m2_d=m_m2_d, m_m2_norm_g=m_m2_norm_g, m_m2_w_out=m_m2_w_out, m_final_norm_g=m_final_norm_g, v_norm_mix_g=v_norm_mix_g, v_norm_mlp_g=v_norm_mlp_g, v_mlp_w1=v_mlp_w1, v_mlp_w2=v_mlp_w2, v_gdn_w_in=v_gdn_w_in, v_gdn_conv_w=v_gdn_conv_w, v_gdn_a_log=v_gdn_a_log, v_gdn_dt_bias=v_gdn_dt_bias, v_gdn_o_norm_g=v_gdn_o_norm_g, v_gdn_w_out=v_gdn_w_out, v_s5_w_in=v_s5_w_in, v_s5_lam_re=v_s5_lam_re, v_s5_lam_im=v_s5_lam_im, v_s5_log_dt=v_s5_log_dt, v_s5_b_re=v_s5_b_re, v_s5_b_im=v_s5_b_im, v_s5_c_re=v_s5_c_re, v_s5_c_im=v_s5_c_im, v_s5_d=v_s5_d, v_s5_w_out=v_s5_w_out, v_m2_w_in=v_m2_w_in, v_m2_conv_w=v_m2_conv_w, v_m2_conv_b=v_m2_conv_b, v_m2_dt_bias=v_m2_dt_bias, v_m2_a_log=v_m2_a_log, v_m2_d=v_m2_d, v_m2_norm_g=v_m2_norm_g, v_m2_w_out=v_m2_w_out, v_final_norm_g=v_final_norm_g)
    weights = {n: given[n] for n in TWIN_WEIGHTS}
    shared = {n: given[n] for n in SHARED_INPUTS}
    per_example = {n: given[n] for n in ['x']}
    grad_fn = _jax.value_and_grad(_loss, argnums=(0, 1))

    def one_microbatch(ex, loss_target):
        ex = dict(ex)
        diff = ex.pop(TWIN_DIFF_INPUT)
        return grad_fn(weights, diff, {**shared, **ex}, loss_target)

    if N_MICROBATCH == 1:
        loss, (grad_w, grad_x) = one_microbatch(per_example, given["loss_target"])
    else:
        def body(carry, xs):
            loss_sum, grad_sum = carry
            l_k, (gw_k, gx_k) = one_microbatch(xs[0], xs[1])
            with _jax.named_scope("update"):
                return (loss_sum + l_k, _jax.tree.map(_jnp.add, grad_sum, gw_k)), gx_k

        init = (_jnp.zeros((), _jnp.float32), _jax.tree.map(_jnp.zeros_like, weights))
        (loss, grad_w), grad_x = _jax.lax.scan(body, init, (per_example, given["loss_target"]))
    with _jax.named_scope("update"):
        delta_w, new_m, new_v = {}, {}, {}
        for n in TWIN_WEIGHTS:
            delta_w[n], new_m[n], new_v[n] = _adamw(weights[n], grad_w[n], given["m_" + n], given["v_" + n])
    return (loss, grad_x, *[grad_w[n] for n in TWIN_WEIGHTS], *[delta_w[n] for n in TWIN_WEIGHTS],
            *[new_m[n] for n in TWIN_WEIGHTS], *[new_v[n] for n in TWIN_WEIGHTS])
```

```python
import functools

import jax
import jax.numpy as jnp
from jax import lax
from jax.experimental import pallas as pl
from jax.experimental.pallas import tpu as pltpu

f32 = jnp.float32
bf16 = jnp.bfloat16
HI = lax.Precision.HIGHEST
MESH = pl.DeviceIdType.MESH

N_DEV = 8
D_MODEL = 1024
D_FF = 4096
CHUNK = 64
RMS_EPS = 1e-6
GDN_HEADS = 8
GDN_EXT = 4224
S5_STATE = 64
M2_INNER = 2048
M2_EXT = 6272
M2_HEADS = 32
VMEM_LIMIT_BYTES = 56 * 1024 * 1024

ADAM_LR, ADAM_B1, ADAM_B2, ADAM_EPS, ADAM_WD, ADAM_STEP = 0.001, 0.9, 0.999, 1e-08, 0.01, 10

_NN = ((1,), (0,))
_NT = ((1,), (1,))
_TN = ((0,), (0,))


def _dot(a, b, dims=_NN):
    return lax.dot_general(a, b, (dims, ((), ())), precision=HI, preferred_element_type=f32)


def _iota(shape, dim):
    return lax.broadcasted_iota(jnp.int32, shape, dim)


def _params(n_grid):
    return pltpu.CompilerParams(dimension_semantics=("arbitrary",) * n_grid, vmem_limit_bytes=VMEM_LIMIT_BYTES)


def _row_tile(n_rows):
    return min(512, n_rows)


def _col_tile(n, cap=1024):
    best = 128
    for t in range(128, cap + 1, 128):
        if n % t == 0:
            best = t
    return best


def _mm(name, a, b, *, dims, grid, a_spec, b_spec, out_shape, out_spec, aux=(), a_fn=None, epi_fn=None,
        acc_shape, out_init=None):
    nk = grid[2]
    n_aux = len(aux)
    kinds = [x[2] for x in aux]

    def body(*refs):
        a_ref, b_ref = refs[0], refs[1]
        aux_refs = refs[2:2 + n_aux]
        pos = 2 + n_aux + (1 if out_init is not None else 0)
        o_ref, acc_ref = refs[pos], refs[pos + 1]
        k = pl.program_id(2)

        @pl.when(k == 0)
        def _():
            acc_ref[...] = jnp.zeros_like(acc_ref)

        av = a_ref[...]
        if a_fn is not None:
            av = a_fn(av, *[r[...] for r, kd in zip(aux_refs, kinds) if kd == "a"])
        acc_ref[...] += lax.dot_general(av.astype(bf16), b_ref[...].astype(bf16), (dims, ((), ())),
                                        preferred_element_type=f32)

        @pl.when(k == nk - 1)
        def _():
            r = acc_ref[...]
            if epi_fn is not None:
                r = epi_fn(r, *[x[...] for x, kd in zip(aux_refs, kinds) if kd == "e"])
            o_ref[...] = r.astype(o_ref.dtype)

    in_specs = [a_spec, b_spec] + [x[1] for x in aux]
    args = [a, b] + [x[0] for x in aux]
    aliases = {}
    if out_init is not None:
        in_specs.append(pl.BlockSpec(memory_space=pl.ANY))
        args.append(out_init)
        aliases = {len(args) - 1: 0}
    return pl.pallas_call(
        body, name=name, grid=grid, in_specs=in_specs, out_specs=out_spec, out_shape=out_shape,
        scratch_shapes=[pltpu.VMEM(acc_shape, f32)], input_output_aliases=aliases,
        compiler_params=_params(3),
    )(*args)


def _ew(name, f, ins, outs, grid):
    n_in = len(ins)
    modes = [o[4] for o in outs]

    def body(*refs):
        vals = [r[...] for r in refs[:n_in]]
        res = f(*vals)
        if not isinstance(res, (tuple, list)):
            res = (res,)
        for r, o_ref, mode in zip(res, refs[n_in:], modes):
            if mode is None:
                o_ref[...] = r.astype(o_ref.dtype)
                continue
            first = pl.program_id(1) == 0
            if mode == "all":
                first = jnp.logical_and(first, pl.program_id(0) == 0)

            @pl.when(first)
            def _(r=r, o_ref=o_ref):
                o_ref[...] = r.astype(o_ref.dtype)

            @pl.when(jnp.logical_not(first))
            def _(r=r, o_ref=o_ref):
                o_ref[...] += r.astype(o_ref.dtype)

    res = pl.pallas_call(
        body, name=name, grid=grid,
        in_specs=[pl.BlockSpec(blk, im) for _, blk, im in ins],
        out_specs=[pl.BlockSpec(o[2], o[3]) for o in outs],
        out_shape=[jax.ShapeDtypeStruct(o[0], o[1]) for o in outs],
        compiler_params=_params(2),
    )(*[a for a, _, _ in ins])
    return res


def _vjp_fn(f, n_primal):
    def g(*args):
        _, vjp = jax.vjp(f, *args[:n_primal])
        cts = args[n_primal:]
        return vjp(cts[0] if len(cts) == 1 else tuple(cts))
    return g


def _scan_fwd(name, step, n_state, state_shape, cins, ins, outs, n_units, n_chunks):
    n_c, n_in, n_out = len(cins), len(ins), len(outs)

    def body(*refs):
        c_refs = refs[:n_c]
        in_refs = refs[n_c:n_c + n_in]
        out_refs = refs[n_c + n_in:n_c + n_in + n_out]
        saved = refs[n_c + n_in + n_out:n_c + n_in + n_out + n_state]
        st = refs[n_c + n_in + n_out + n_state:]

        @pl.when(pl.program_id(1) == 0)
        def _():
            for s in st:
                s[...] = jnp.zeros_like(s)

        cur = [s[...] for s in st]
        for sv, s in zip(saved, cur):
            sv[...] = s
        new, res = step(cur, [r[...] for r in c_refs], [r[...] for r in in_refs])
        for s, n in zip(st, new):
            s[...] = n
        for o, r in zip(out_refs, res):
            o[...] = r

    sshape = (n_units, n_chunks) + state_shape
    sblock = (None, None) + state_shape
    nz = len(state_shape)
    res = pl.pallas_call(
        body, name=name, grid=(n_units, n_chunks),
        in_specs=[pl.BlockSpec(e[1], e[2]) for e in cins + ins],
        out_specs=[pl.BlockSpec(o[1], o[2]) for o in outs]
        + [pl.BlockSpec(sblock, lambda u, c: (u, c) + (0,) * nz)] * n_state,
        out_shape=[jax.ShapeDtypeStruct(o[0], f32) for o in outs]
        + [jax.ShapeDtypeStruct(sshape, f32)] * n_state,
        scratch_shapes=[pltpu.VMEM(state_shape, f32)] * n_state,
        compiler_params=_params(2),
    )(*[e[0] for e in cins + ins])
    return res[:n_out], res[n_out:]


def _scan_bwd(name, step, n_state, state_shape, cins, ins, saved, douts, n_units, n_chunks):
    n_c, n_in, n_do = len(cins), len(ins), len(douts)

    def flip(im):
        return lambda u, c: im(u, n_chunks - 1 - c)

    def body(*refs):
        p = 0
        c_refs = refs[p:p + n_c]; p += n_c
        in_refs = refs[p:p + n_in]; p += n_in
        sv_refs = refs[p:p + n_state]; p += n_state
        do_refs = refs[p:p + n_do]; p += n_do
        dc_refs = refs[p:p + n_c]; p += n_c
        di_refs = refs[p:p + n_in]; p += n_in
        dst = refs[p:]
        first = pl.program_id(1) == 0

        @pl.when(first)
        def _():
            for s in dst:
                s[...] = jnp.zeros_like(s)

        def fn(states, consts, vals):
            new, res = step(states, consts, vals)
            return tuple(new), tuple(res)

        prim = ([r[...] for r in sv_refs], [r[...] for r in c_refs], [r[...] for r in in_refs])
        _, vjp = jax.vjp(fn, *prim)
        d_states, d_consts, d_vals = vjp((tuple(s[...] for s in dst), tuple(r[...] for r in do_refs)))
        for s, g in zip(dst, d_states):
            s[...] = g
        for o, g in zip(di_refs, d_vals):
            o[...] = g
        for o, g in zip(dc_refs, d_consts):
            @pl.when(first)
            def _(o=o, g=g):
                o[...] = g

            @pl.when(jnp.logical_not(first))
            def _(o=o, g=g):
                o[...] += g

    nz = len(state_shape)
    sblock = (None, None) + state_shape
    def gshape(e):
        return e[3] if len(e) == 5 else e[0].shape

    def gmap(e):
        return e[4] if len(e) == 5 else e[2]

    in_specs = ([pl.BlockSpec(e[1], e[2]) for e in cins]
                + [pl.BlockSpec(e[1], flip(e[2])) for e in ins]
                + [pl.BlockSpec(sblock, lambda u, c: (u, n_chunks - 1 - c) + (0,) * nz)] * n_state
                + [pl.BlockSpec(e[1], flip(e[2])) for e in douts])
    out_specs = ([pl.BlockSpec(e[1], e[2]) for e in cins]
                 + [pl.BlockSpec(e[1], flip(gmap(e))) for e in ins])
    out_shape = [jax.ShapeDtypeStruct(gshape(e), f32) for e in cins + ins]
    res = pl.pallas_call(
        body, name=name, grid=(n_units, n_chunks), in_specs=in_specs, out_specs=out_specs, out_shape=out_shape,
        scratch_shapes=[pltpu.VMEM(state_shape, f32)] * n_state,
        compiler_params=_params(2),
    )(*([e[0] for e in cins + ins] + list(saved) + [e[0] for e in douts]))
    return res[:n_c], res[n_c:]


def _rms(x, g):
    return x * lax.rsqrt(jnp.mean(x * x, axis=-1, keepdims=True) + RMS_EPS) * g


def _rows(tm, width):
    return (tm, width), lambda r, z: (r, 0)


def _const(shape):
    return shape, lambda r, z: (0,) * len(shape)


def _rms_fwd(name, h, g):
    L, D = h.shape
    tm = _row_tile(L)
    return _ew(name, _rms, [(h, *_rows(tm, D)), (g, *_const((1, D)))],
               [((L, D), f32, *_rows(tm, D), None)], (L // tm, 1))[0]


def _rms_bwd(name, h, g, d_hn, d_res):
    L, D = h.shape
    tm = _row_tile(L)

    def f(hv, gv, dv, rv):
        dh, dg = _vjp_fn(_rms, 2)(hv, gv, dv)
        return dh + rv, dg

    return _ew(name, f, [(h, *_rows(tm, D)), (g, *_const((1, D))), (d_hn, *_rows(tm, D)), (d_res, *_rows(tm, D))],
               [((L, D), f32, *_rows(tm, D), None), ((1, D), f32, *_const((1, D)), "all")], (L // tm, 1))


def _loss_head(h, g, target):
    L, D = h.shape
    tm = _row_tile(L)

    def f(hv, gv, tv):
        def lf(a, b):
            e = jnp.square(_rms(a, b) - tv)
            return (0.5 / D) * jnp.sum(jnp.sum(e, axis=1, keepdims=True), axis=0, keepdims=True)

        val, vjp = jax.vjp(lf, hv, gv)
        dh, dg = vjp(jnp.ones((1, 1), f32))
        return jnp.broadcast_to(val, (1, 128)), dh, dg

    return _ew("loss_head", f, [(h, *_rows(tm, D)), (g, *_const((1, D))), (target, *_rows(tm, D))],
               [((1, 128), f32, *_const((1, 128)), "all"), ((L, D), f32, *_rows(tm, D), None),
                ((1, D), f32, *_const((1, D)), "all")], (L // tm, 1))


def _sqrelu(x):
    return jnp.square(jnp.maximum(x, 0.0))


def _mm_plain(name, a, b, dims, *, a_fn=None, epi_fn=None, aux=()):
    if dims == _NN:
        (M, K), N = a.shape, b.shape[1]
    elif dims == _NT:
        (M, K), N = a.shape, b.shape[0]
    else:
        (K, M), N = a.shape, b.shape[1]
    tm = min(512, M)
    tn = _col_tile(N)
    tk = _col_tile(K) if K % 128 == 0 else K
    tk = min(tk, 1024)
    if dims == _TN:
        tk = min(512, K)
        a_spec = pl.BlockSpec((tk, tm), lambda i, j, k: (k, i))
        b_spec = pl.BlockSpec((tk, tn), lambda i, j, k: (k, j))
        a_aux = pl.BlockSpec((tk, tm), lambda i, j, k: (k, i))
    elif dims == _NT:
        a_spec = pl.BlockSpec((tm, tk), lambda i, j, k: (i, k))
        b_spec = pl.BlockSpec((tn, tk), lambda i, j, k: (j, k))
        a_aux = pl.BlockSpec((tm, tk), lambda i, j, k: (i, k))
    else:
        a_spec = pl.BlockSpec((tm, tk), lambda i, j, k: (i, k))
        b_spec = pl.BlockSpec((tk, tn), lambda i, j, k: (k, j))
        a_aux = pl.BlockSpec((tm, tk), lambda i, j, k: (i, k))
    e_aux = pl.BlockSpec((tm, tn), lambda i, j, k: (i, j))
    aux_full = [(x, a_aux if kd == "a" else e_aux, kd) for x, kd in aux]
    return _mm(name, a, b, dims=dims, grid=(M // tm, N // tn, K // tk), a_spec=a_spec, b_spec=b_spec,
               out_shape=jax.ShapeDtypeStruct((M, N), f32), out_spec=pl.BlockSpec((tm, tn), lambda i, j, k: (i, j)),
               aux=aux_full, a_fn=a_fn, epi_fn=epi_fn, acc_shape=(tm, tn))


def _mlp_fwd(h, g, w1g, w2g, layer):
    L, D = h.shape
    tm = _row_tile(L)
    fs = D_FF // N_DEV
    hn = _rms_fwd("mlp_norm", h, g)
    h1 = _mm("mlp_up", hn, w1g, dims=_NN, grid=(L // tm, N_DEV, 1),
             a_spec=pl.BlockSpec((tm, D), lambda i, j, k: (i, 0)),
             b_spec=pl.BlockSpec((None, None, D, fs), lambda i, j, k: (j, layer, 0, 0)),
             out_shape=jax.ShapeDtypeStruct((L, D_FF), f32), out_spec=pl.BlockSpec((tm, fs), lambda i, j, k: (i, j)),
             acc_shape=(tm, fs))
    tn = 512
    h_out = _mm("mlp_down", h1, w2g, dims=_NN, grid=(L // tm, D // tn, N_DEV),
                a_spec=pl.BlockSpec((tm, fs), lambda i, j, k: (i, k)),
                b_spec=pl.BlockSpec((None, None, fs, tn), lambda i, j, k: (k, layer, 0, j)),
                out_shape=jax.ShapeDtypeStruct((L, D), f32), out_spec=pl.BlockSpec((tm, tn), lambda i, j, k: (i, j)),
                aux=[(h, pl.BlockSpec((tm, tn), lambda i, j, k: (i, j)), "e")],
                a_fn=_sqrelu, epi_fn=lambda acc, res: acc + res, acc_shape=(tm, tn))
    return h_out, hn, h1


def _mlp_bwd(dh, h, g, hn, h1, w1g, w2g, layer, dw1_buf, dw2_buf):
    L, D = h.shape
    tm = _row_tile(L)
    fs = D_FF // N_DEV
    tk = min(512, L)
    dh1 = _mm("mlp_down_dx", dh, w2g, dims=_NT, grid=(L // tm, N_DEV, 1),
              a_spec=pl.BlockSpec((tm, D), lambda i, j, k: (i, 0)),
              b_spec=pl.BlockSpec((None, None, fs, D), lambda i, j, k: (j, layer, 0, 0)),
              out_shape=jax.ShapeDtypeStruct((L, D_FF), f32), out_spec=pl.BlockSpec((tm, fs), lambda i, j, k: (i, j)),
              aux=[(h1, pl.BlockSpec((tm, fs), lambda i, j, k: (i, j)), "e")],
              epi_fn=lambda acc, pre: acc * (2.0 * jnp.maximum(pre, 0.0)), acc_shape=(tm, fs))
    dw2_buf = _mm("mlp_down_dw", h1, dh, dims=_TN, grid=(N_DEV, 1, L // tk),
                  a_spec=pl.BlockSpec((tk, fs), lambda i, j, k: (k, i)),
                  b_spec=pl.BlockSpec((tk, D), lambda i, j, k: (k, 0)),
                  out_shape=jax.ShapeDtypeStruct(dw2_buf.shape, f32),
                  out_spec=pl.BlockSpec((None, None, fs, D), lambda i, j, k: (i, layer, 0, 0)),
                  a_fn=_sqrelu, acc_shape=(fs, D), out_init=dw2_buf)
    tr = 512
    dw1_buf = _mm("mlp_up_dw", hn, dh1, dims=_TN, grid=(D // tr, N_DEV, L // tk),
                  a_spec=pl.BlockSpec((tk, tr), lambda i, j, k: (k, i)),
                  b_spec=pl.BlockSpec((tk, fs), lambda i, j, k: (k, j)),
                  out_shape=jax.ShapeDtypeStruct(dw1_buf.shape, f32),
                  out_spec=pl.BlockSpec((None, None, tr, fs), lambda i, j, k: (j, layer, i, 0)),
                  acc_shape=(tr, fs), out_init=dw1_buf)
    tn = 512
    dhn = _mm("mlp_up_dx", dh1, w1g, dims=_NT, grid=(L // tm, D // tn, N_DEV),
              a_spec=pl.BlockSpec((tm, fs), lambda i, j, k: (i, k)),
              b_spec=pl.BlockSpec((None, None, tn, fs), lambda i, j, k: (k, layer, j, 0)),
              out_shape=jax.ShapeDtypeStruct((L, D), f32), out_spec=pl.BlockSpec((tm, tn), lambda i, j, k: (i, j)),
              acc_shape=(tm, tn))
    dh_in, dg = _rms_bwd("mlp_norm_bwd", h, g, dhn, dh)
    return dh_in, dg, dw1_buf, dw2_buf


def _shift_dn(x, s, row):
    return x if s == 0 else jnp.where(row >= s, pltpu.roll(x, s, 0), 0.0)


def _shift_up(x, s, row):
    n = x.shape[0]
    return x if s == 0 else jnp.where(row < n - s, pltpu.roll(x, n - s, 0), 0.0)


def _conv_pre(x, w, b, row):
    c = jnp.broadcast_to(b, x.shape)
    for j in range(4):
        c = c + w[j:j + 1, :] * _shift_dn(x, 3 - j, row)
    return c


def _conv_fwd(name, x_arr, blk_off, w, b):
    L = x_arr.shape[0]
    C = w.shape[1]

    def f(x, wv, bv):
        c = _conv_pre(x, wv, bv, _iota(x.shape, 0))
        return c * jax.nn.sigmoid(c)

    return _ew(name, f, [(x_arr, (L, 128), lambda j, z: (0, blk_off + j)), (w, (4, 128), lambda j, z: (0, j)),
                         (b, (1, 128), lambda j, z: (0, j))],
               [((L, C), f32, (L, 128), lambda j, z: (0, j), None)], (C // 128, 1))[0]


def _conv_bwd(name, x_arr, blk_off, w, b, dy):
    L = x_arr.shape[0]
    C = w.shape[1]

    def f(x, wv, bv, g):
        row = _iota(x.shape, 0)
        c = _conv_pre(x, wv, bv, row)
        s = jax.nn.sigmoid(c)
        dc = g * (s * (1.0 + c * (1.0 - s)))
        dx = jnp.zeros_like(x)
        dw = jnp.zeros((4, 128), f32)
        r4 = _iota((4, 128), 0)
        for j in range(4):
            dx = dx + wv[j:j + 1, :] * _shift_up(dc, 3 - j, row)
            dwj = jnp.sum(dc * _shift_dn(x, 3 - j, row), axis=0, keepdims=True)
            dw = dw + jnp.where(r4 == j, jnp.broadcast_to(dwj, (4, 128)), 0.0)
        return dx, dw, jnp.sum(dc, axis=0, keepdims=True)

    return _ew(name, f, [(x_arr, (L, 128), lambda j, z: (0, blk_off + j)), (w, (4, 128), lambda j, z: (0, j)),
                         (b, (1, 128), lambda j, z: (0, j)), (dy, (L, 128), lambda j, z: (0, j))],
               [((L, C), f32, (L, 128), lambda j, z: (0, j), None), ((4, C), f32, (4, 128), lambda j, z: (0, j), None),
                ((1, C), f32, (1, 128), lambda j, z: (0, j), None)], (C // 128, 1))


def _l2norm(t):
    return t * lax.rsqrt(jnp.sum(t * t, axis=-1, keepdims=True) + 1e-6)


def _gdn_act(cq, ck, ab, alog, dtb):
    h = pl.program_id(1)
    qn = _l2norm(cq) * (128.0 ** -0.5)
    kn = _l2norm(ck)
    lane = _iota(ab.shape, 1)
    a_raw = jnp.sum(jnp.where(lane == h, ab, 0.0), axis=1, keepdims=True)
    b_raw = jnp.sum(jnp.where(lane == h + GDN_HEADS, ab, 0.0), axis=1, keepdims=True)
    lane1 = _iota(alog.shape, 1)
    al = jnp.sum(jnp.where(lane1 == h, alog, 0.0), axis=1, keepdims=True)
    db = jnp.sum(jnp.where(lane1 == h, dtb, 0.0), axis=1, keepdims=True)
    g = -jnp.exp(al) * jax.nn.softplus(a_raw + db)
    beta = jax.nn.sigmoid(b_raw)
    return qn, kn, jnp.broadcast_to(g, cq.shape), jnp.broadcast_to(beta, cq.shape)


def _gdn_chunk(states, consts, vals):
    (S,) = states
    q, k, v, gb, bb = vals
    C = q.shape[0]
    row, col = _iota((C, C), 0), _iota((C, C), 1)
    causal, strict = row >= col, row > col
    ltri = causal.astype(f32)
    eye = (row == col).astype(f32)
    Gb = _dot(ltri, gb)
    Gc = jnp.mean(Gb, axis=1, keepdims=True)
    e0 = (_iota(Gb.shape, 1) == 0).astype(f32)
    Gr = _dot(e0, Gb, _NT)
    bc = jnp.mean(bb, axis=1, keepdims=True)
    decay = jnp.where(causal, jnp.exp(jnp.where(causal, Gc - Gr, 0.0)), 0.0)
    A = jnp.where(strict, bc * _dot(k, k, _NT) * decay, 0.0)
    M = eye - A
    P = _dot(A, A)
    for it in range(5):
        M = M + _dot(M, P)
        if it < 4:
            P = _dot(P, P)
    eG = jnp.exp(Gc)
    u = _dot(M, v * bc)
    w = _dot(M, k * (bc * eG))
    qk = _dot(q, k, _NT) * decay
    g_last = jnp.sum(jnp.where(_iota(Gc.shape, 0) == C - 1, Gc, 0.0), axis=0, keepdims=True)
    v_new = u - _dot(w, S)
    o = _dot(q * eG, S) + _dot(qk, v_new)
    S_new = jnp.exp(g_last) * S + _dot(k * jnp.exp(g_last - Gc), v_new, _TN)
    return [S_new], [o]


def _gdn_post(o, gate, g):
    return _rms(o, g) * (gate * jax.nn.sigmoid(gate))


def _pad_row(v):
    return jnp.pad(v.astype(f32), (0, 128 - v.shape[0])).reshape(1, 128)


def _gdn_fwd(h, g_norm, w_ext, conv_w, a_log, dt_bias, o_norm_g, w_out):
    L, D = h.shape
    tm = _row_tile(L)
    nc = L // CHUNK
    H = GDN_HEADS
    hn = _rms_fwd("mix_norm", h, g_norm)
    proj = _mm_plain("gdn_in", hn, w_ext, _NN)
    zb = jnp.zeros((1, 3 * D), f32)
    cq = _conv_fwd("gdn_conv", proj, 0, conv_w, zb)
    alog, dtb = _pad_row(a_log), _pad_row(dt_bias)
    act_ins = [(cq, (tm, 128), lambda r, hh: (r, hh)), (cq, (tm, 128), lambda r, hh: (r, H + hh)),
               (proj, (tm, 128), lambda r, hh: (r, 4 * H)), (alog, (1, 128), lambda r, hh: (0, 0)),
               (dtb, (1, 128), lambda r, hh: (0, 0))]
    qn, kn, gb, bb = _ew("gdn_act", _gdn_act, act_ins,
                         [((L, D), f32, (tm, 128), lambda r, hh: (r, hh), None)] * 4, (L // tm, H))
    cblk = (CHUNK, 128)
    core_ins = [(qn, cblk, lambda u, c: (c, u)), (kn, cblk, lambda u, c: (c, u)),
                (cq, cblk, lambda u, c: (c, 2 * H + u), (L, D), lambda u, c: (c, u)),
                (gb, cblk, lambda u, c: (c, u)), (bb, cblk, lambda u, c: (c, u))]
    (o,), saved_s = _scan_fwd("gdn_core", _gdn_chunk, 1, (128, 128), [], core_ins,
                              [((L, D), cblk, lambda u, c: (c, u))], H, nc)
    on = o_norm_g.reshape(1, 128)
    post_ins = [(o, (tm, 128), lambda r, hh: (r, hh)), (proj, (tm, 128), lambda r, hh: (r, 3 * H + hh)),
                (on, (1, 128), lambda r, hh: (0, 0))]
    y = _ew("gdn_post", _gdn_post, post_ins, [((L, D), f32, (tm, 128), lambda r, hh: (r, hh), None)], (L // tm, H))[0]
    h_out = _mm_plain("gdn_out", y, w_out, _NN, epi_fn=lambda acc, res: acc + res, aux=[(h, "e")])
    saved = dict(hn=hn, proj=proj, cq=cq, alog=alog, dtb=dtb, act_ins=act_ins, core_ins=core_ins, saved_s=saved_s,
                 post_ins=post_ins, y=y, zb=zb)
    return h_out, saved


def _gdn_bwd(dh, h, g_norm, w_ext, conv_w, w_out, sv):
    L, D = h.shape
    tm = _row_tile(L)
    nc = L // CHUNK
    H = GDN_HEADS
    dy = _mm_plain("gdn_out_dx", dh, w_out, _NT)
    dw_out = _mm_plain("gdn_out_dw", sv["y"], dh, _TN)
    hd = ((L, D), f32, (tm, 128), lambda r, hh: (r, hh), None)
    d_o, d_gate, d_on = _ew("gdn_post_bwd", _vjp_fn(_gdn_post, 3),
                            sv["post_ins"] + [(dy, (tm, 128), lambda r, hh: (r, hh))],
                            [hd, hd, ((1, 128), f32, (1, 128), lambda r, hh: (0, 0), "all")], (L // tm, H))
    cblk = (CHUNK, 128)
    _, (dqn, dkn, dv, dgb, dbb) = _scan_bwd("gdn_core_bwd", _gdn_chunk, 1, (128, 128), [], sv["core_ins"],
                                            sv["saved_s"], [(d_o, cblk, lambda u, c: (c, u))], H, nc)
    cts = [(t, (tm, 128), lambda r, hh: (r, hh)) for t in (dqn, dkn, dgb, dbb)]
    row128 = ((1, 128), f32, (1, 128), lambda r, hh: (0, 0), "all")
    d_cq, d_ck, d_ab, d_alog, d_dtb = _ew(
        "gdn_act_bwd", _vjp_fn(_gdn_act, 5), sv["act_ins"] + cts,
        [hd, hd, ((L, 128), f32, (tm, 128), lambda r, hh: (r, 0), "inner"), row128, row128], (L // tm, H))
    d_conv_out = jnp.concatenate([d_cq, d_ck, dv], axis=1)
    d_conv_in, d_conv_w, _ = _conv_bwd("gdn_conv_bwd", sv["proj"], 0, conv_w, sv["zb"], d_conv_out)
    d_proj = jnp.concatenate([d_conv_in, d_gate, d_ab], axis=1)
    dw_ext = _mm_plain("gdn_in_dw", sv["hn"], d_proj, _TN)
    dhn = _mm_plain("gdn_in_dx", d_proj, w_ext, _NT)
    dh_in, dg = _rms_bwd("mix_norm_bwd", h, g_norm, dhn, dh)
    grads = dict(norm=dg, w_ext=dw_ext, conv_w=d_conv_w, a_log=d_alog[0, :H], dt_bias=d_dtb[0, :H],
                 o_norm_g=d_on[0], w_out=dw_out)
    return dh_in, grads


def _expand_lanes(row, width, rep):
    sel = ((_iota((128, width), 1) // rep) == _iota((128, width), 0)).astype(f32)
    return jnp.mean(_dot(jnp.broadcast_to(row, (8, 128)), sel), axis=0, keepdims=True)


def _s5_params(lre, lim, ldt, wbr, wbi):
    dt = jnp.exp(_expand_lanes(ldt, 512, S5_STATE))
    mag = jnp.exp(lre * dt)
    ang = lim * dt
    abr, abi = mag * jnp.cos(ang), mag * jnp.sin(ang)
    nr = abr - 1.0
    den = lre * lre + lim * lim
    cr = (nr * lre + abi * lim) / den
    ci = (abi * lre - nr * lim) / den
    return abr, abi, cr * wbr - ci * wbi, cr * wbi + ci * wbr


def _s5_scan(name, xr, xi, ar, ai, rev, want_prev):
    L, W = xr.shape
    nb = L // 8
    n_out = 4 if want_prev else 2

    def body(xr_ref, xi_ref, ar_ref, ai_ref, *outs):
        a_r = ar_ref[...]
        a_i = -ai_ref[...] if rev else ai_ref[...]

        def cm(p, q):
            return p[0] * q[0] - p[1] * q[1], p[0] * q[1] + p[1] * q[0]

        a1 = (a_r, a_i)
        a2 = cm(a1, a1)
        a3 = cm(a2, a1)
        a4 = cm(a2, a2)
        pw = [a1, a2, a3, a4, cm(a4, a1), cm(a4, a2), cm(a4, a3), cm(a4, a4)]
        row = _iota((8, 128), 0)
        tab_r = jnp.zeros((8, 128), f32)
        tab_i = jnp.zeros((8, 128), f32)
        for t in range(8):
            idx = 7 - t if rev else t
            tab_r = jnp.where(row == idx, jnp.broadcast_to(pw[t][0], (8, 128)), tab_r)
            tab_i = jnp.where(row == idx, jnp.broadcast_to(pw[t][1], (8, 128)), tab_i)
        lv = [(d, jnp.broadcast_to(p[0], (8, 128)), jnp.broadcast_to(p[1], (8, 128))) for d, p in ((1, a1), (2, a2), (4, a4))]

        def step(i, carry):
            cr, ci = carry
            blk = nb - 1 - i if rev else i
            r0 = pl.multiple_of(blk * 8, 8)
            x_r = xr_ref[pl.ds(r0, 8), :]
            x_i = xi_ref[pl.ds(r0, 8), :]
            for d, p_r, p_i in lv:
                if rev:
                    s_r = jnp.where(row < 8 - d, pltpu.roll(x_r, 8 - d, 0), 0.0)
                    s_i = jnp.where(row < 8 - d, pltpu.roll(x_i, 8 - d, 0), 0.0)
                else:
                    s_r = jnp.where(row >= d, pltpu.roll(x_r, d, 0), 0.0)
                    s_i = jnp.where(row >= d, pltpu.roll(x_i, d, 0), 0.0)
                x_r, x_i = x_r + p_r * s_r - p_i * s_i, x_i + p_r * s_i + p_i * s_r
            x_r, x_i = x_r + tab_r * cr - tab_i * ci, x_i + tab_r * ci + tab_i * cr
            outs[0][pl.ds(r0, 8), :] = x_r
            outs[1][pl.ds(r0, 8), :] = x_i
            if want_prev:
                outs[2][pl.ds(r0, 8), :] = jnp.where(row >= 1, pltpu.roll(x_r, 1, 0), cr)
                outs[3][pl.ds(r0, 8), :] = jnp.where(row >= 1, pltpu.roll(x_i, 1, 0), ci)
            e = 0 if rev else 7
            return jnp.broadcast_to(x_r[e:e + 1, :], (8, 128)), jnp.broadcast_to(x_i[e:e + 1, :], (8, 128))

        lax.fori_loop(0, nb, step, (jnp.zeros((8, 128), f32), jnp.zeros((8, 128), f32)))

    col = pl.BlockSpec((L, 128), lambda q, z: (0, q))
    aspec = pl.BlockSpec((None, 1, 128), lambda q, z: (q // 4, 0, q % 4))
    return pl.pallas_call(
        body, name=name, grid=(W // 128, 1), in_specs=[col, col, aspec, aspec], out_specs=[col] * n_out,
        out_shape=[jax.ShapeDtypeStruct((L, W), f32)] * n_out, compiler_params=_params(2),
    )(xr, xi, ar, ai)


def _blockdiag(t, n_in, n_out):
    t4 = t.reshape(8, 8, n_in, n_out)
    return jnp.einsum("jaio,ab->jaibo", t4, jnp.eye(8, dtype=t.dtype)).reshape(8, 8 * n_in, 8 * n_out)


def _blockdiag_t(w, n_in, n_out):
    w5 = w.reshape(8, 8, n_in, 8, n_out)
    return jnp.einsum("jaibo,ab->jaio", w5, jnp.eye(8, dtype=w.dtype)).reshape(64, n_in, n_out)


def _glu(ag, h):
    n = ag.shape[1] // 2
    return h + ag[:, :n] * jax.nn.sigmoid(ag[:, n:])


def _s5_fwd(h, g_norm, w_in, lam_re, lam_im, log_dt, b_re, b_im, c_re, c_im, d_skip, w_out_g):
    L, D = h.shape
    tm = _row_tile(L)
    W = 8 * 512
    hn = _rms_fwd("mix_norm", h, g_norm)
    u = _mm_plain("s5_in", hn, w_in, _NN)
    lre, lim = lam_re.reshape(8, 1, 512), lam_im.reshape(8, 1, 512)
    ldt = jnp.pad(log_dt.reshape(8, 1, 8), ((0, 0), (0, 0), (0, 120)))
    wbr = _blockdiag(b_re.transpose(0, 2, 1), 16, 64)
    wbi = _blockdiag(b_im.transpose(0, 2, 1), 16, 64)
    wcr = _blockdiag(c_re.transpose(0, 2, 1), 64, 16)
    wci = _blockdiag(c_im.transpose(0, 2, 1), 64, 16)
    jb = lambda shape: (shape, lambda j, z: (j, 0, 0))
    par_ins = [(lre, *jb((None, 1, 512))), (lim, *jb((None, 1, 512))), (ldt, *jb((None, 1, 128))),
               (wbr, *jb((None, 128, 512))), (wbi, *jb((None, 128, 512)))]
    abr, abi, bbr, bbi = _ew("s5_params", _s5_params, par_ins,
                             [((8, 1, 512), f32, *jb((None, 1, 512)), None)] * 2
                             + [((8, 128, 512), f32, *jb((None, 128, 512)), None)] * 2, (8, 1))

    def bu(name, wb):
        return _mm(name, u, wb, dims=_NN, grid=(L // tm, 8, 1),
                   a_spec=pl.BlockSpec((tm, 128), lambda i, j, k: (i, j)),
                   b_spec=pl.BlockSpec((None, 128, 512), lambda i, j, k: (j, 0, 0)),
                   out_shape=jax.ShapeDtypeStruct((L, W), f32), out_spec=pl.BlockSpec((tm, 512), lambda i, j, k: (i, j)),
                   acc_shape=(tm, 512))

    bur, bui = bu("s5_bu", bbr), bu("s5_bu", bbi)
    sr, si, pr, pi = _s5_scan("s5_scan", bur, bui, abr, abi, False, True)
    d_row = d_skip.reshape(1, D)
    cspec = dict(a_spec=pl.BlockSpec((tm, 512), lambda i, j, k: (i, j)),
                 b_spec=pl.BlockSpec((None, 512, 128), lambda i, j, k: (j, 0, 0)),
                 out_shape=jax.ShapeDtypeStruct((L, D), f32), out_spec=pl.BlockSpec((tm, 128), lambda i, j, k: (i, j)),
                 acc_shape=(tm, 128))
    e128 = pl.BlockSpec((tm, 128), lambda i, j, k: (i, j))
    pre1 = _mm("s5_c_re", sr, wcr, dims=_NN, grid=(L // tm, 8, 1), **cspec)
    pre = _mm("s5_c_im", si, wci, dims=_NN, grid=(L // tm, 8, 1),
              aux=[(pre1, e128, "e"), (u, e128, "e"), (d_row, pl.BlockSpec((1, 128), lambda i, j, k: (0, j)), "e")],
              epi_fn=lambda acc, p1, uu, dd: p1 - acc + dd * uu, **cspec)
    ws = D // N_DEV * 2
    ag = _mm("s5_out", pre, w_out_g, dims=_NN, grid=(L // tm, N_DEV, 1),
             a_spec=pl.BlockSpec((tm, D), lambda i, j, k: (i, 0)),
             b_spec=pl.BlockSpec((None, None, D, ws), lambda i, j, k: (j, 0, 0, 0)),
             out_shape=jax.ShapeDtypeStruct((L, 2 * D), f32), out_spec=pl.BlockSpec((tm, ws), lambda i, j, k: (i, j)),
             a_fn=jax.nn.gelu, acc_shape=(tm, ws))
    h_out = _ew("s5_glu", _glu, [(ag, *_rows(tm, 2 * D)), (h, *_rows(tm, D))],
                [((L, D), f32, *_rows(tm, D), None)], (L // tm, 1))[0]
    saved = dict(hn=hn, u=u, par_ins=par_ins, abr=abr, abi=abi, bbr=bbr, bbi=bbi, wcr=wcr, wci=wci, sr=sr, si=si,
                 pr=pr, pi=pi, pre=pre, ag=ag, d_row=d_row)
    return h_out, saved


def _s5_bwd(dh, h, g_norm, w_in, w_out_g, sv):
    L, D = h.shape
    tm = _row_tile(L)
    tk = min(512, L)
    W = 8 * 512
    ws = D // N_DEV * 2
    u, pre, d_row = sv["u"], sv["pre"], sv["d_row"]
    d_ag = _ew("s5_glu_bwd", lambda ag, hv, g: _vjp_fn(_glu, 2)(ag, hv, g)[0],
               [(sv["ag"], *_rows(tm, 2 * D)), (h, *_rows(tm, D)), (dh, *_rows(tm, D))],
               [((L, 2 * D), f32, *_rows(tm, 2 * D), None)], (L // tm, 1))[0]
    tr = 512
    dw_out = _mm("s5_out_dw", pre, d_ag, dims=_TN, grid=(D // tr, N_DEV, L // tk),
                 a_spec=pl.BlockSpec((tk, tr), lambda i, j, k: (k, i)),
                 b_spec=pl.BlockSpec((tk, ws), lambda i, j, k: (k, j)),
                 out_shape=jax.ShapeDtypeStruct((N_DEV, 1, D, ws), f32),
                 out_spec=pl.BlockSpec((None, None, tr, ws), lambda i, j, k: (j, 0, i, 0)),
                 a_fn=jax.nn.gelu, acc_shape=(tr, ws))
    tn = 512
    dpre = _mm("s5_out_dx", d_ag, w_out_g, dims=_NT, grid=(L // tm, D // tn, N_DEV),
               a_spec=pl.BlockSpec((tm, ws), lambda i, j, k: (i, k)),
               b_spec=pl.BlockSpec((None, None, tn, ws), lambda i, j, k: (k, 0, j, 0)),
               out_shape=jax.ShapeDtypeStruct((L, D), f32), out_spec=pl.BlockSpec((tm, tn), lambda i, j, k: (i, j)),
               aux=[(pre, pl.BlockSpec((tm, tn), lambda i, j, k: (i, j)), "e")],
               epi_fn=lambda acc, p: _vjp_fn(jax.nn.gelu, 1)(p, acc)[0], acc_shape=(tm, tn))
    d_d = _ew("s5_dskip", lambda a, b: jnp.sum(a * b, axis=0, keepdims=True),
              [(dpre, *_rows(tm, D)), (u, *_rows(tm, D))], [((1, D), f32, *_const((1, D)), "all")], (L // tm, 1))[0]
    neg = lambda acc: -acc
    dsspec = dict(dims=_NT, grid=(L // tm, 8, 1), a_spec=pl.BlockSpec((tm, 128), lambda i, j, k: (i, j)),
                  b_spec=pl.BlockSpec((None, 512, 128), lambda i, j, k: (j, 0, 0)),
                  out_shape=jax.ShapeDtypeStruct((L, W), f32), out_spec=pl.BlockSpec((tm, 512), lambda i, j, k: (i, j)),
                  acc_shape=(tm, 512))
    dsr = _mm("s5_c_re_dx", dpre, sv["wcr"], **dsspec)
    dsi = _mm("s5_c_im_dx", dpre, sv["wci"], epi_fn=neg, **dsspec)
    dwspec = dict(dims=_TN, grid=(8, 1, L // tk), a_spec=pl.BlockSpec((tk, 512), lambda i, j, k: (k, i)),
                  b_spec=pl.BlockSpec((tk, 128), lambda i, j, k: (k, i)),
                  out_shape=jax.ShapeDtypeStruct((8, 512, 128), f32),
                  out_spec=pl.BlockSpec((None, 512, 128), lambda i, j, k: (i, 0, 0)), acc_shape=(512, 128))
    dwcr = _mm("s5_c_re_dw", sv["sr"], dpre, **dwspec)
    dwci = _mm("s5_c_im_dw", sv["si"], dpre, epi_fn=neg, **dwspec)
    lr, li = _s5_scan("s5_scan_bwd", dsr, dsi, sv["abr"], sv["abi"], True, False)

    def da(lrv, liv, prv, piv):
        return (jnp.sum(lrv * prv + liv * piv, axis=0, keepdims=True),
                jnp.sum(liv * prv - lrv * piv, axis=0, keepdims=True))

    sblk = ((tm, 512), lambda j, r: (r, j))
    dabr, dabi = _ew("s5_dlam", da, [(lr, *sblk), (li, *sblk), (sv["pr"], *sblk), (sv["pi"], *sblk)],
                     [((8, 1, 512), f32, (None, 1, 512), lambda j, r: (j, 0, 0), "inner")] * 2, (8, L // tm))
    dbspec = dict(dims=_TN, grid=(8, 1, L // tk), a_spec=pl.BlockSpec((tk, 128), lambda i, j, k: (k, i)),
                  b_spec=pl.BlockSpec((tk, 512), lambda i, j, k: (k, i)),
                  out_shape=jax.ShapeDtypeStruct((8, 128, 512), f32),
                  out_spec=pl.BlockSpec((None, 128, 512), lambda i, j, k: (i, 0, 0)), acc_shape=(128, 512))
    dbbr = _mm("s5_bu_dw", u, lr, **dbspec)
    dbbi = _mm("s5_bu_dw", u, li, **dbspec)
    duspec = dict(dims=_NT, grid=(L // tm, 8, 1), a_spec=pl.BlockSpec((tm, 512), lambda i, j, k: (i, j)),
                  b_spec=pl.BlockSpec((None, 128, 512), lambda i, j, k: (j, 0, 0)),
                  out_shape=jax.ShapeDtypeStruct((L, D), f32), out_spec=pl.BlockSpec((tm, 128), lambda i, j, k: (i, j)),
                  acc_shape=(tm, 128))
    e128 = pl.BlockSpec((tm, 128), lambda i, j, k: (i, j))
    du1 = _mm("s5_bu_dx_re", lr, sv["bbr"], **duspec)
    du = _mm("s5_bu_dx_im", li, sv["bbi"],
             aux=[(du1, e128, "e"), (dpre, e128, "e"), (d_row, pl.BlockSpec((1, 128), lambda i, j, k: (0, j)), "e")],
             epi_fn=lambda acc, d1, dp, dd: acc + d1 + dp * dd, **duspec)
    jb = lambda shape: (shape, lambda j, z: (j, 0, 0))
    cts = [(dabr, *jb((None, 1, 512))), (dabi, *jb((None, 1, 512))), (dbbr, *jb((None, 128, 512))),
           (dbbi, *jb((None, 128, 512)))]
    dlre, dlim, dldt, dwbr, dwbi = _ew(
        "s5_params_bwd", _vjp_fn(_s5_params, 5), sv["par_ins"] + cts,
        [((8, 1, 512), f32, *jb((None, 1, 512)), None)] * 2 + [((8, 1, 128), f32, *jb((None, 1, 128)), None)]
        + [((8, 128, 512), f32, *jb((None, 128, 512)), None)] * 2, (8, 1))
    dw_in = _mm_plain("s5_in_dw", sv["hn"], du, _TN)
    dhn = _mm_plain("s5_in_dx", du, w_in, _NT)
    dh_in, dg = _rms_bwd("mix_norm_bwd", h, g_norm, dhn, dh)
    grads = dict(norm=dg, w_in=dw_in, lam_re=dlre.reshape(64, 64), lam_im=dlim.reshape(64, 64),
                 log_dt=dldt[:, 0, :8].reshape(64),
                 b_re=_blockdiag_t(dwbr, 16, 64).transpose(0, 2, 1), b_im=_blockdiag_t(dwbi, 16, 64).transpose(0, 2, 1),
                 c_re=_blockdiag_t(dwcr, 64, 16).transpose(0, 2, 1), c_im=_blockdiag_t(dwci, 64, 16).transpose(0, 2, 1),
                 d=d_d[0], w_out=dw_out)
    return dh_in, grads


def _m2_act(dt_raw, dtbias, alog):
    dt = jax.nn.softplus(dt_raw + dtbias)
    da = dt * (-jnp.exp(alog))
    sel = ((_iota((128, M2_INNER), 1) // 64) == _iota((128, M2_INNER), 0)).astype(f32)
    return _dot(dt, sel), _dot(da, sel)


def _m2_dexp(d):
    return _expand_lanes(d, M2_INNER, 64)


def _ssd_chunk(states, consts, vals):
    (S,) = states
    (dsk,) = consts
    x, dtb, dab, B, Cm = vals
    C = x.shape[0]
    row, col = _iota((C, C), 0), _iota((C, C), 1)
    causal = row >= col
    xdt = x * dtb
    cb = _dot(Cm, B, _NT)
    cum = _dot(causal.astype(f32), dab)
    lane = _iota((C, 128), 1)
    rowc = _iota((C, 128), 0)
    eye128 = _iota((128, 128), 0) == _iota((128, 128), 1)
    ys, new = [], []
    for half in range(2):
        sl = slice(128 * half, 128 * half + 128)
        cumh, xh, Sh = cum[:, sl], xdt[:, sl], S[sl, :]
        clast = jnp.sum(jnp.where(rowc == C - 1, cumh, 0.0), axis=0, keepdims=True)
        st = _dot(xh * jnp.exp(clast - cumh), B, _TN)
        y = _dot(Cm, Sh, _NT) * jnp.exp(cumh)
        for hh in range(2):
            m = jnp.logical_and(lane >= 64 * hh, lane < 64 * hh + 64)
            ccol = jnp.sum(jnp.where(m, cumh, 0.0), axis=1, keepdims=True) * (1.0 / 64)
            crow = _dot((lane == 64 * hh).astype(f32), cumh, _NT)
            lm = jnp.where(causal, jnp.exp(jnp.where(causal, ccol - crow, 0.0)), 0.0)
            y = y + _dot(cb * lm, jnp.where(m, xh, 0.0))
        cdcol = jnp.sum(jnp.where(eye128, jnp.broadcast_to(jnp.exp(clast), (128, 128)), 0.0), axis=1, keepdims=True)
        new.append(cdcol * Sh + st)
        ys.append(y)
    return [jnp.concatenate(new, axis=0)], [jnp.concatenate(ys, axis=1) + dsk * x]


def _m2_post(yc, z, ng):
    return _rms(yc * (z * jax.nn.sigmoid(z)), ng)


def _m2_fwd(h, g_norm, w_ext, conv_w, conv_b, dt_bias, a_log, d_skip, norm_g, w_out):
    L, D = h.shape
    tm = _row_tile(L)
    nc = L // CHUNK
    NI = M2_INNER
    hn = _rms_fwd("mix_norm", h, g_norm)
    proj = _mm_plain("m2_in", hn, w_ext, _NN)
    xbc = _conv_fwd("m2_conv", proj, NI // 128, conv_w, conv_b)
    dtb_row, alog_row, d_pad = _pad_row(dt_bias), _pad_row(a_log), _pad_row(d_skip)
    act_ins = [(proj, (tm, 128), lambda r, z: (r, 3 * NI // 128)), (dtb_row, *_const((1, 128))),
               (alog_row, *_const((1, 128)))]
    dtb, dab = _ew("m2_act", _m2_act, act_ins, [((L, NI), f32, *_rows(tm, NI), None)] * 2, (L // tm, 1))
    dsk = _ew("m2_dexp", _m2_dexp, [(d_pad, *_const((1, 128)))], [((1, NI), f32, *_const((1, NI)), None)], (1, 1))[0]
    x_blk, bc_blk = (CHUNK, 256), (CHUNK, 128)
    cins = [(dsk, (1, 256), lambda u, c: (0, u))]
    core_ins = [(xbc, x_blk, lambda u, c: (c, u), (L, NI), lambda u, c: (c, u)),
                (dtb, x_blk, lambda u, c: (c, u)), (dab, x_blk, lambda u, c: (c, u)),
                (xbc, bc_blk, lambda u, c: (c, 16 + u), (L, D), lambda u, c: (c, u)),
                (xbc, bc_blk, lambda u, c: (c, 24 + u), (L, D), lambda u, c: (c, u))]
    (yc,), saved_s = _scan_fwd("m2_core", _ssd_chunk, 1, (256, 128), cins, core_ins,
                               [((L, NI), x_blk, lambda u, c: (c, u))], 8, nc)
    gblk = ((tm, 256), lambda g, r: (r, g))
    post_ins = [(yc, *gblk), (proj, *gblk), (norm_g, (1, 256), lambda g, r: (0, g))]
    yn = _ew("m2_post", _m2_post, post_ins, [((L, NI), f32, *gblk, None)], (8, L // tm))[0]
    h_out = _mm_plain("m2_out", yn, w_out, _NN, epi_fn=lambda acc, res: acc + res, aux=[(h, "e")])
    saved = dict(hn=hn, proj=proj, act_ins=act_ins, d_pad=d_pad, cins=cins, core_ins=core_ins, saved_s=saved_s,
                 post_ins=post_ins, yn=yn)
    return h_out, saved


def _m2_bwd(dh, h, g_norm, w_ext, conv_w, conv_b, w_out, sv):
    L, D = h.shape
    tm = _row_tile(L)
    nc = L // CHUNK
    NI = M2_INNER
    dyn = _mm_plain("m2_out_dx", dh, w_out, _NT)
    dw_out = _mm_plain("m2_out_dw", sv["yn"], dh, _TN)
    gblk = ((tm, 256), lambda g, r: (r, g))
    d_yc, d_z, d_ng = _ew("m2_post_bwd", _vjp_fn(_m2_post, 3), sv["post_ins"] + [(dyn, *gblk)],
                          [((L, NI), f32, *gblk, None)] * 2 + [((1, NI), f32, (1, 256), lambda g, r: (0, g), "inner")],
                          (8, L // tm))
    (d_dsk,), (dx, d_dtb, d_dab, dB, dC) = _scan_bwd(
        "m2_core_bwd", _ssd_chunk, 1, (256, 128), sv["cins"], sv["core_ins"], sv["saved_s"],
        [(d_yc, (CHUNK, 256), lambda u, c: (c, u))], 8, nc)
    row128 = ((1, 128), f32, *_const((1, 128)), "all")
    d_dt_raw, d_dtbias, d_alog = _ew(
        "m2_act_bwd", _vjp_fn(_m2_act, 3), sv["act_ins"] + [(d_dtb, *_rows(tm, NI)), (d_dab, *_rows(tm, NI))],
        [((L, 128), f32, *_rows(tm, 128), None), row128, row128], (L // tm, 1))
    d_d = _ew("m2_dexp_bwd", _vjp_fn(_m2_dexp, 1), [(sv["d_pad"], *_const((1, 128))), (d_dsk, *_const((1, NI)))],
              [((1, 128), f32, *_const((1, 128)), None)], (1, 1))[0]
    d_conv_out = jnp.concatenate([dx, dB, dC], axis=1)
    d_conv_in, d_conv_w, d_conv_b = _conv_bwd("m2_conv_bwd", sv["proj"], NI // 128, conv_w, conv_b, d_conv_out)
    d_proj = jnp.concatenate([d_z, d_conv_in, d_dt_raw], axis=1)
    dw_ext = _mm_plain("m2_in_dw", sv["hn"], d_proj, _TN)
    dhn = _mm_plain("m2_in_dx", d_proj, w_ext, _NT)
    dh_in, dg = _rms_bwd("mix_norm_bwd", h, g_norm, dhn, dh)
    grads = dict(norm=dg, w_ext=dw_ext, conv_w=d_conv_w, conv_b=d_conv_b, dt_bias=d_dtbias[0, :M2_HEADS],
                 a_log=d_alog[0, :M2_HEADS], d=d_d[0, :M2_HEADS], norm_g=d_ng, w_out=dw_out)
    return dh_in, grads


def _mesh_pos():
    return lax.axis_index("x"), lax.axis_index("y"), lax.axis_index("c")


def _flip(pos, p):
    x, y, c = pos
    return (1 - x if p & 4 else x, 1 - y if p & 2 else y, 1 - c if p & 1 else c)


def _index(pos):
    return 4 * pos[0] + 2 * pos[1] + pos[2]


def _exchange(name, arrays, scatter):
    n = len(arrays)

    def body(*refs):
        ins, outs = refs[:n], refs[n:2 * n]
        send_sems, recv_sems, loc_sems = refs[2 * n:]
        me = _mesh_pos()
        mi = _index(me)

        def src(w, t):
            return ins[w].at[t] if scatter else ins[w]

        def copy(w, p):
            peer = _flip(me, p)
            return pltpu.make_async_remote_copy(
                src_ref=src(w, _index(peer)), dst_ref=outs[w].at[mi], send_sem=send_sems.at[w, p - 1],
                recv_sem=recv_sems.at[w, p - 1], device_id=peer, device_id_type=MESH)

        def arrival(w, p):
            peer = _flip(me, p)
            return pltpu.make_async_remote_copy(
                src_ref=src(w, mi), dst_ref=outs[w].at[_index(peer)], send_sem=send_sems.at[w, p - 1],
                recv_sem=recv_sems.at[w, p - 1], device_id=peer, device_id_type=MESH)

        local = [pltpu.make_async_copy(src(w, mi), outs[w].at[mi], loc_sems.at[w]) for w in range(n)]
        for cp in local:
            cp.start()
        sends = [copy(w, p) for p in range(1, N_DEV) for w in range(n)]
        for cp in sends:
            cp.start()
        for p in range(1, N_DEV):
            for w in range(n):
                arrival(w, p).wait_recv()
        for cp in sends:
            cp.wait_send()
        for cp in local:
            cp.wait()

    hbm = pl.BlockSpec(memory_space=pl.ANY)
    out_shape = [jax.ShapeDtypeStruct(a.shape if scatter else (N_DEV,) + a.shape, a.dtype) for a in arrays]
    return pl.pallas_call(
        body, name=name, in_specs=[hbm] * n, out_specs=[hbm] * n, out_shape=out_shape,
        scratch_shapes=[pltpu.SemaphoreType.DMA((n, N_DEV - 1)), pltpu.SemaphoreType.DMA((n, N_DEV - 1)),
                        pltpu.SemaphoreType.DMA((n,))],
    )(*arrays)


def _adamw(name, parts, w, m, v):
    R, C = w.shape
    tr = R if R <= 256 else (256 if C <= 512 else 128)
    bc1 = 1.0 - ADAM_B1 ** ADAM_STEP
    bc2 = 1.0 - ADAM_B2 ** ADAM_STEP

    def f(p, wv, mv, vv):
        g = p[0]
        for i in range(1, N_DEV):
            g = g + p[i]
        m2 = ADAM_B1 * mv + (1.0 - ADAM_B1) * g
        v2 = ADAM_B2 * vv + (1.0 - ADAM_B2) * jnp.square(g)
        delta = -ADAM_LR * ((m2 / bc1) / (jnp.sqrt(v2 / bc2) + ADAM_EPS) + ADAM_WD * wv)
        return g, delta, m2, v2

    blk = ((tr, C), lambda r, z: (r, 0))
    return _ew(name, f, [(parts, (N_DEV, tr, C), lambda r, z: (0, r, 0)), (w, *blk), (m, *blk), (v, *blk)],
               [((R, C), f32, *blk, None)] * 4, (R // tr, 1))


_WEIGHTS = ["norm_mix_g", "norm_mlp_g", "mlp_w1", "mlp_w2", "gdn_w_in", "gdn_conv_w", "gdn_a_log", "gdn_dt_bias",
            "gdn_o_norm_g", "gdn_w_out", "s5_w_in", "s5_lam_re", "s5_lam_im", "s5_log_dt", "s5_b_re", "s5_b_im",
            "s5_c_re", "s5_c_im", "s5_d", "s5_w_out", "m2_w_in", "m2_conv_w", "m2_conv_b", "m2_dt_bias", "m2_a_log",
            "m2_d", "m2_norm_g", "m2_w_out", "final_norm_g"]
_SHARDED = ["mlp_w1", "mlp_w2", "gdn_w_in", "gdn_w_out", "s5_w_in", "s5_w_out", "m2_w_in", "m2_w_out",
            "gdn_conv_w", "m2_conv_w", "m2_conv_b", "m2_norm_g"]
_MATRICES = _SHARDED[:8]
_REPLICATED = [n for n in _WEIGHTS if n not in _SHARDED]
_GDN_IN, _M2_IN = 4112, 6176
_LAYER_KIND = (0, 1, 2, 0)


def _as2d(a):
    return a.reshape(-1, a.shape[-1])


def _cols_from_shards(g, width):
    return g.transpose(1, 0, 2).reshape(g.shape[1], width)


def _cols_to_shards(a, width):
    return a[:, :width].reshape(a.shape[0], N_DEV, width // N_DEV).transpose(1, 0, 2)


def kernel(x, norm_mix_g, norm_mlp_g, mlp_w1, mlp_w2, gdn_w_in, gdn_conv_w, gdn_a_log, gdn_dt_bias, gdn_o_norm_g, gdn_w_out, s5_w_in, s5_lam_re, s5_lam_im, s5_log_dt, s5_b_re, s5_b_im, s5_c_re, s5_c_im, s5_d, s5_w_out, m2_w_in, m2_conv_w, m2_conv_b, m2_dt_bias, m2_a_log, m2_d, m2_norm_g, m2_w_out, final_norm_g, loss_target, m_norm_mix_g, m_norm_mlp_g, m_mlp_w1, m_mlp_w2, m_gdn_w_in, m_gdn_conv_w, m_gdn_a_log, m_gdn_dt_bias, m_gdn_o_norm_g, m_gdn_w_out, m_s5_w_in, m_s5_lam_re, m_s5_lam_im, m_s5_log_dt, m_s5_b_re, m_s5_b_im, m_s5_c_re, m_s5_c_im, m_s5_d, m_s5_w_out, m_m2_w_in, m_m2_conv_w, m_m2_conv_b, m_m2_dt_bias, m_m2_a_log, m_m2_d, m_m2_norm_g, m_m2_w_out, m_final_norm_g, v_norm_mix_g, v_norm_mlp_g, v_mlp_w1, v_mlp_w2, v_gdn_w_in, v_gdn_conv_w, v_gdn_a_log, v_gdn_dt_bias, v_gdn_o_norm_g, v_gdn_w_out, v_s5_w_in, v_s5_lam_re, v_s5_lam_im, v_s5_log_dt, v_s5_b_re, v_s5_b_im, v_s5_c_re, v_s5_c_im, v_s5_d, v_s5_w_out, v_m2_w_in, v_m2_conv_w, v_m2_conv_b, v_m2_dt_bias, v_m2_a_log, v_m2_d, v_m2_norm_g, v_m2_w_out, v_final_norm_g):
    args = locals()
    W = {n: args[n] for n in _WEIGHTS}
    MOM = {n: args["m_" + n] for n in _WEIGHTS}
    VAR = {n: args["v_" + n] for n in _WEIGHTS}
    h = x[0]
    target = loss_target[0]
    L, D = h.shape

    sends = [W[n].astype(bf16) if n in _MATRICES else _as2d(W[n]) for n in _SHARDED]
    G = dict(zip(_SHARDED, _exchange("gather_weights", sends, scatter=False)))
    w1g, w2g = G["mlp_w1"], G["mlp_w2"]
    gdn_in = [jnp.pad(_cols_from_shards(G["gdn_w_in"][:, j], _GDN_IN), ((0, 0), (0, GDN_EXT - _GDN_IN))) for j in range(2)]
    gdn_out = [G["gdn_w_out"][:, j].reshape(D, D) for j in range(2)]
    gdn_conv = [_cols_from_shards(G["gdn_conv_w"][:, 4 * j:4 * j + 4], 3 * D) for j in range(2)]
    s5_in = G["s5_w_in"].reshape(D, D)
    s5_out_g = G["s5_w_out"]
    m2_in = jnp.pad(_cols_from_shards(G["m2_w_in"][:, 0], _M2_IN), ((0, 0), (0, M2_EXT - _M2_IN)))
    m2_out = G["m2_w_out"].reshape(M2_INNER, D)
    m2_conv = _cols_from_shards(G["m2_conv_w"], 2 * M2_INNER)
    m2_cb = _cols_from_shards(G["m2_conv_b"], 2 * M2_INNER)
    m2_ng = _cols_from_shards(G["m2_norm_g"], M2_INNER)

    def mixer_fwd(i, hv):
        kind, j = _LAYER_KIND[i], i // 3
        gn = norm_mix_g[i].reshape(1, D)
        if kind == 0:
            return _gdn_fwd(hv, gn, gdn_in[j], gdn_conv[j], gdn_a_log[j], gdn_dt_bias[j], gdn_o_norm_g[j], gdn_out[j])
        if kind == 1:
            return _s5_fwd(hv, gn, s5_in, s5_lam_re[0], s5_lam_im[0], s5_log_dt[0], s5_b_re[0], s5_b_im[0],
                           s5_c_re[0], s5_c_im[0], s5_d[0], s5_out_g)
        return _m2_fwd(hv, gn, m2_in, m2_conv, m2_cb, m2_dt_bias[0], m2_a_log[0], m2_d[0], m2_ng, m2_out)

    def mixer_bwd(i, dh, hv, sv):
        kind, j = _LAYER_KIND[i], i // 3
        gn = norm_mix_g[i].reshape(1, D)
        if kind == 0:
            return _gdn_bwd(dh, hv, gn, gdn_in[j], gdn_conv[j], gdn_out[j], sv)
        if kind == 1:
            return _s5_bwd(dh, hv, gn, s5_in, s5_out_g, sv)
        return _m2_bwd(dh, hv, gn, m2_in, m2_conv, m2_cb, m2_out, sv)

    tape = []
    for i in range(4):
        h_mid, sv = mixer_fwd(i, h)
        h_next, hn, h1 = _mlp_fwd(h_mid, norm_mlp_g[i].reshape(1, D), w1g, w2g, i)
        tape.append((h, sv, h_mid, hn, h1))
        h = h_next
    loss_row, dh, d_final = _loss_head(h, final_norm_g.reshape(1, D), target)
    loss = lax.psum(loss_row[0, 0], ("x", "y", "c"))

    dw1 = lax.empty((N_DEV, 4, D, D_FF // N_DEV), f32)
    dw2 = lax.empty((N_DEV, 4, D_FF // N_DEV, D), f32)
    d_mix, d_mlp, mg = [None] * 4, [None] * 4, [None] * 4
    for i in reversed(range(4)):
        h_in, sv, h_mid, hn, h1 = tape[i]
        dh, d_mlp[i], dw1, dw2 = _mlp_bwd(dh, h_mid, norm_mlp_g[i].reshape(1, D), hn, h1, w1g, w2g, i, dw1, dw2)
        dh, mg[i] = mixer_bwd(i, dh, h_in, sv)
        d_mix[i] = mg[i]["norm"]
    grad_x = dh.reshape(1, L, D)
    ga, gb_, s5g, m2g = mg[0], mg[3], mg[1], mg[2]

    full = {
        "mlp_w1": dw1, "mlp_w2": dw2,
        "gdn_w_in": jnp.stack([_cols_to_shards(g["w_ext"], _GDN_IN) for g in (ga, gb_)], axis=1),
        "gdn_w_out": jnp.stack([g["w_out"].reshape(N_DEV, D // N_DEV, D) for g in (ga, gb_)], axis=1),
        "s5_w_in": s5g["w_in"].reshape(N_DEV, 1, D // N_DEV, D), "s5_w_out": s5g["w_out"],
        "m2_w_in": _cols_to_shards(m2g["w_ext"], _M2_IN)[:, None],
        "m2_w_out": m2g["w_out"].reshape(N_DEV, 1, M2_INNER // N_DEV, D),
        "gdn_conv_w": jnp.concatenate([_cols_to_shards(g["conv_w"], 3 * D) for g in (ga, gb_)], axis=1),
        "m2_conv_w": _cols_to_shards(m2g["conv_w"], 2 * M2_INNER),
        "m2_conv_b": _cols_to_shards(m2g["conv_b"], 2 * M2_INNER),
        "m2_norm_g": _cols_to_shards(m2g["norm_g"], M2_INNER),
    }
    sends = [full[n].reshape((N_DEV,) + _as2d(W[n]).shape) for n in _SHARDED]
    parts = dict(zip(_SHARDED, _exchange("scatter_grads", sends, scatter=True)))

    rep = {
        "norm_mix_g": jnp.concatenate(d_mix, axis=0), "norm_mlp_g": jnp.concatenate(d_mlp, axis=0),
        "gdn_a_log": jnp.stack([ga["a_log"], gb_["a_log"]]), "gdn_dt_bias": jnp.stack([ga["dt_bias"], gb_["dt_bias"]]),
        "gdn_o_norm_g": jnp.stack([ga["o_norm_g"], gb_["o_norm_g"]]),
        "s5_lam_re": s5g["lam_re"], "s5_lam_im": s5g["lam_im"], "s5_log_dt": s5g["log_dt"], "s5_b_re": s5g["b_re"],
        "s5_b_im": s5g["b_im"], "s5_c_re": s5g["c_re"], "s5_c_im": s5g["c_im"], "s5_d": s5g["d"],
        "m2_dt_bias": m2g["dt_bias"], "m2_a_log": m2g["a_log"], "m2_d": m2g["d"], "final_norm_g": d_final,
    }

    def pack(d):
        flat = jnp.concatenate([d[n].reshape(-1).astype(f32) for n in _REPLICATED])
        return jnp.pad(flat, (0, -flat.shape[0] % (256 * 128))).reshape(-1, 128)

    (rep_parts,) = _exchange("gather_small_grads", [pack(rep)], scatter=False)

    res = {}
    for n in _SHARDED:
        w2d = _as2d(W[n])
        out = _adamw("adamw_" + n, parts[n], w2d, _as2d(MOM[n]), _as2d(VAR[n]))
        res[n] = [o.reshape(W[n].shape) for o in out]
    out = _adamw("adamw_replicated", rep_parts, pack(W), pack(MOM), pack(VAR))
    off = 0
    for n in _REPLICATED:
        size = W[n].size
        res[n] = [o.reshape(-1)[off:off + size].reshape(W[n].shape) for o in out]
        off += size

    return (loss, grad_x, *[res[n][0] for n in _WEIGHTS], *[res[n][1] for n in _WEIGHTS],
            *[res[n][2] for n in _WEIGHTS], *[res[n][3] for n in _WEIGHTS])
```

```python
import functools

import jax
import jax.numpy as jnp
from jax import lax
from jax.experimental import pallas as pl
from jax.experimental.pallas import tpu as pltpu

f32 = jnp.float32
bf16 = jnp.bfloat16
HI = lax.Precision.HIGHEST
MESH = pl.DeviceIdType.MESH

N_DEV = 8
D_MODEL = 1024
D_FF = 4096
CHUNK = 64
RMS_EPS = 1e-6
GDN_HEADS = 8
GDN_HB = 4
GDN_EXT = 4224
S5_STATE = 64
M2_INNER = 2048
M2_EXT = 6272
M2_HEADS = 32
M2_GB = 2
VMEM_LIMIT_BYTES = 56 * 1024 * 1024

ADAM_LR, ADAM_B1, ADAM_B2, ADAM_EPS, ADAM_WD, ADAM_STEP = 0.001, 0.9, 0.999, 1e-08, 0.01, 10

_NN = ((1,), (0,))
_NT = ((1,), (1,))
_TN = ((0,), (0,))


def _dot(a, b, dims=_NN):
    return lax.dot_general(a, b, (dims, ((), ())), precision=HI, preferred_element_type=f32)


def _dotb(a, b, dims=_NN):
    return lax.dot_general(a.astype(bf16), b.astype(bf16), (dims, ((), ())), preferred_element_type=f32)


def _iota(shape, dim):
    return lax.broadcasted_iota(jnp.int32, shape, dim)


def _params(n_grid):
    return pltpu.CompilerParams(dimension_semantics=("arbitrary",) * n_grid, vmem_limit_bytes=VMEM_LIMIT_BYTES)


def _row_tile(n_rows):
    return min(512, n_rows)


def _col_tile(n, cap=1024):
    best = 128
    for t in range(128, cap + 1, 128):
        if n % t == 0:
            best = t
    return best


def _mm(name, a, b, *, dims, grid, a_spec, b_spec, out_shape, out_spec, aux=(), a_fn=None, epi_fn=None,
        acc_shape, out_init=None):
    nk = grid[2]
    n_aux = len(aux)
    kinds = [x[2] for x in aux]

    def body(*refs):
        a_ref, b_ref = refs[0], refs[1]
        aux_refs = refs[2:2 + n_aux]
        pos = 2 + n_aux + (1 if out_init is not None else 0)
        o_ref, acc_ref = refs[pos], refs[pos + 1]
        k = pl.program_id(2)

        @pl.when(k == 0)
        def _():
            acc_ref[...] = jnp.zeros_like(acc_ref)

        av = a_ref[...]
        if a_fn is not None:
            av = a_fn(av, *[r[...] for r, kd in zip(aux_refs, kinds) if kd == "a"])
        acc_ref[...] += lax.dot_general(av.astype(bf16), b_ref[...].astype(bf16), (dims, ((), ())),
                                        preferred_element_type=f32)

        @pl.when(k == nk - 1)
        def _():
            r = acc_ref[...]
            if epi_fn is not None:
                r = epi_fn(r, *[x[...] for x, kd in zip(aux_refs, kinds) if kd == "e"])
            o_ref[...] = r.astype(o_ref.dtype)

    in_specs = [a_spec, b_spec] + [x[1] for x in aux]
    args = [a, b] + [x[0] for x in aux]
    aliases = {}
    if out_init is not None:
        in_specs.append(pl.BlockSpec(memory_space=pl.ANY))
        args.append(out_init)
        aliases = {len(args) - 1: 0}
    return pl.pallas_call(
        body, name=name, grid=grid, in_specs=in_specs, out_specs=out_spec, out_shape=out_shape,
        scratch_shapes=[pltpu.VMEM(acc_shape, f32)], input_output_aliases=aliases,
        compiler_params=_params(3),
    )(*args)


def _ew(name, f, ins, outs, grid):
    n_in = len(ins)
    modes = [o[4] for o in outs]

    def body(*refs):
        vals = [r[...] for r in refs[:n_in]]
        res = f(*vals)
        if not isinstance(res, (tuple, list)):
            res = (res,)
        for r, o_ref, mode in zip(res, refs[n_in:], modes):
            if mode is None:
                o_ref[...] = r.astype(o_ref.dtype)
                continue
            first = pl.program_id(1) == 0
            if mode == "all":
                first = jnp.logical_and(first, pl.program_id(0) == 0)

            @pl.when(first)
            def _(r=r, o_ref=o_ref):
                o_ref[...] = r.astype(o_ref.dtype)

            @pl.when(jnp.logical_not(first))
            def _(r=r, o_ref=o_ref):
                o_ref[...] += r.astype(o_ref.dtype)

    res = pl.pallas_call(
        body, name=name, grid=grid,
        in_specs=[pl.BlockSpec(blk, im) for _, blk, im in ins],
        out_specs=[pl.BlockSpec(o[2], o[3]) for o in outs],
        out_shape=[jax.ShapeDtypeStruct(o[0], o[1]) for o in outs],
        compiler_params=_params(2),
    )(*[a for a, _, _ in ins])
    return res


def _vjp_fn(f, n_primal):
    def g(*args):
        _, vjp = jax.vjp(f, *args[:n_primal])
        cts = args[n_primal:]
        return vjp(cts[0] if len(cts) == 1 else tuple(cts))
    return g


def _scan_fwd(name, step, n_state, state_shape, cins, ins, outs, n_units, n_chunks):
    n_c, n_in, n_out = len(cins), len(ins), len(outs)

    def body(*refs):
        c_refs = refs[:n_c]
        in_refs = refs[n_c:n_c + n_in]
        out_refs = refs[n_c + n_in:n_c + n_in + n_out]
        saved = refs[n_c + n_in + n_out:n_c + n_in + n_out + n_state]
        st = refs[n_c + n_in + n_out + n_state:]

        @pl.when(pl.program_id(1) == 0)
        def _():
            for s in st:
                s[...] = jnp.zeros_like(s)

        cur = [s[...] for s in st]
        for sv, s in zip(saved, cur):
            sv[...] = s
        new, res = step(cur, [r[...] for r in c_refs], [r[...] for r in in_refs])
        for s, n in zip(st, new):
            s[...] = n
        for o, r in zip(out_refs, res):
            o[...] = r

    sshape = (n_units, n_chunks) + state_shape
    sblock = (None, None) + state_shape
    nz = len(state_shape)
    res = pl.pallas_call(
        body, name=name, grid=(n_units, n_chunks),
        in_specs=[pl.BlockSpec(e[1], e[2]) for e in cins + ins],
        out_specs=[pl.BlockSpec(o[1], o[2]) for o in outs]
        + [pl.BlockSpec(sblock, lambda u, c: (u, c) + (0,) * nz)] * n_state,
        out_shape=[jax.ShapeDtypeStruct(o[0], f32) for o in outs]
        + [jax.ShapeDtypeStruct(sshape, f32)] * n_state,
        scratch_shapes=[pltpu.VMEM(state_shape, f32)] * n_state,
        compiler_params=_params(2),
    )(*[e[0] for e in cins + ins])
    return res[:n_out], res[n_out:]


def _scan_bwd(name, step, n_state, state_shape, cins, ins, saved, douts, n_units, n_chunks):
    n_c, n_in, n_do = len(cins), len(ins), len(douts)

    def flip(im):
        return lambda u, c: im(u, n_chunks - 1 - c)

    def body(*refs):
        p = 0
        c_refs = refs[p:p + n_c]; p += n_c
        in_refs = refs[p:p + n_in]; p += n_in
        sv_refs = refs[p:p + n_state]; p += n_state
        do_refs = refs[p:p + n_do]; p += n_do
        dc_refs = refs[p:p + n_c]; p += n_c
        di_refs = refs[p:p + n_in]; p += n_in
        dst = refs[p:]
        first = pl.program_id(1) == 0

        @pl.when(first)
        def _():
            for s in dst:
                s[...] = jnp.zeros_like(s)

        def fn(states, consts, vals):
            new, res = step(states, consts, vals)
            return tuple(new), tuple(res)

        prim = ([r[...] for r in sv_refs], [r[...] for r in c_refs], [r[...] for r in in_refs])
        _, vjp = jax.vjp(fn, *prim)
        d_states, d_consts, d_vals = vjp((tuple(s[...] for s in dst), tuple(r[...] for r in do_refs)))
        for s, g in zip(dst, d_states):
            s[...] = g
        for o, g in zip(di_refs, d_vals):
            o[...] = g
        for o, g in zip(dc_refs, d_consts):
            @pl.when(first)
            def _(o=o, g=g):
                o[...] = g

            @pl.when(jnp.logical_not(first))
            def _(o=o, g=g):
                o[...] += g

    nz = len(state_shape)
    sblock = (None, None) + state_shape
    def gshape(e):
        return e[3] if len(e) == 5 else e[0].shape

    def gmap(e):
        return e[4] if len(e) == 5 else e[2]

    in_specs = ([pl.BlockSpec(e[1], e[2]) for e in cins]
                + [pl.BlockSpec(e[1], flip(e[2])) for e in ins]
                + [pl.BlockSpec(sblock, lambda u, c: (u, n_chunks - 1 - c) + (0,) * nz)] * n_state
                + [pl.BlockSpec(e[1], flip(e[2])) for e in douts])
    out_specs = ([pl.BlockSpec(e[1], e[2]) for e in cins]
                 + [pl.BlockSpec(e[1], flip(gmap(e))) for e in ins])
    out_shape = [jax.ShapeDtypeStruct(gshape(e), f32) for e in cins + ins]
    res = pl.pallas_call(
        body, name=name, grid=(n_units, n_chunks), in_specs=in_specs, out_specs=out_specs, out_shape=out_shape,
        scratch_shapes=[pltpu.VMEM(state_shape, f32)] * n_state,
        compiler_params=_params(2),
    )(*([e[0] for e in cins + ins] + list(saved) + [e[0] for e in douts]))
    return res[:n_c], res[n_c:]


def _rms(x, g):
    return x * lax.rsqrt(jnp.mean(x * x, axis=-1, keepdims=True) + RMS_EPS) * g


def _rows(tm, width):
    return (tm, width), lambda r, z: (r, 0)


def _const(shape):
    return shape, lambda r, z: (0,) * len(shape)


def _rms_fwd(name, h, g):
    L, D = h.shape
    tm = _row_tile(L)
    return _ew(name, _rms, [(h, *_rows(tm, D)), (g, *_const((1, D)))],
               [((L, D), f32, *_rows(tm, D), None)], (L // tm, 1))[0]


def _rms_bwd(name, h, g, d_hn, d_res):
    L, D = h.shape
    tm = _row_tile(L)

    def f(hv, gv, dv, rv):
        dh, dg = _vjp_fn(_rms, 2)(hv, gv, dv)
        return dh + rv, dg

    return _ew(name, f, [(h, *_rows(tm, D)), (g, *_const((1, D))), (d_hn, *_rows(tm, D)), (d_res, *_rows(tm, D))],
               [((L, D), f32, *_rows(tm, D), None), ((1, D), f32, *_const((1, D)), "all")], (L // tm, 1))


def _loss_head(h, g, target):
    L, D = h.shape
    tm = _row_tile(L)

    def f(hv, gv, tv):
        def lf(a, b):
            e = jnp.square(_rms(a, b) - tv)
            return (0.5 / D) * jnp.sum(jnp.sum(e, axis=1, keepdims=True), axis=0, keepdims=True)

        val, vjp = jax.vjp(lf, hv, gv)
        dh, dg = vjp(jnp.ones((1, 1), f32))
        return jnp.broadcast_to(val, (1, 128)), dh, dg

    return _ew("loss_head", f, [(h, *_rows(tm, D)), (g, *_const((1, D))), (target, *_rows(tm, D))],
               [((1, 128), f32, *_const((1, 128)), "all"), ((L, D), f32, *_rows(tm, D), None),
                ((1, D), f32, *_const((1, D)), "all")], (L // tm, 1))


def _sqrelu(x):
    return jnp.square(jnp.maximum(x, 0.0))


def _mm_plain(name, a, b, dims, *, a_fn=None, epi_fn=None, aux=()):
    if dims == _NN:
        (M, K), N = a.shape, b.shape[1]
    elif dims == _NT:
        (M, K), N = a.shape, b.shape[0]
    else:
        (K, M), N = a.shape, b.shape[1]
    tm = min(512, M)
    tn = _col_tile(N)
    tk = _col_tile(K) if K % 128 == 0 else K
    tk = min(tk, 1024)
    if dims == _TN:
        tk = min(512, K)
        a_spec = pl.BlockSpec((tk, tm), lambda i, j, k: (k, i))
        b_spec = pl.BlockSpec((tk, tn), lambda i, j, k: (k, j))
        a_aux = pl.BlockSpec((tk, tm), lambda i, j, k: (k, i))
    elif dims == _NT:
        a_spec = pl.BlockSpec((tm, tk), lambda i, j, k: (i, k))
        b_spec = pl.BlockSpec((tn, tk), lambda i, j, k: (j, k))
        a_aux = pl.BlockSpec((tm, tk), lambda i, j, k: (i, k))
    else:
        a_spec = pl.BlockSpec((tm, tk), lambda i, j, k: (i, k))
        b_spec = pl.BlockSpec((tk, tn), lambda i, j, k: (k, j))
        a_aux = pl.BlockSpec((tm, tk), lambda i, j, k: (i, k))
    e_aux = pl.BlockSpec((tm, tn), lambda i, j, k: (i, j))
    aux_full = [(x, a_aux if kd == "a" else e_aux, kd) for x, kd in aux]
    return _mm(name, a, b, dims=dims, grid=(M // tm, N // tn, K // tk), a_spec=a_spec, b_spec=b_spec,
               out_shape=jax.ShapeDtypeStruct((M, N), f32), out_spec=pl.BlockSpec((tm, tn), lambda i, j, k: (i, j)),
               aux=aux_full, a_fn=a_fn, epi_fn=epi_fn, acc_shape=(tm, tn))


def _mlp_fwd(h, g, w1g, w2g, layer):
    L, D = h.shape
    tm = _row_tile(L)
    fs = D_FF // N_DEV
    hn = _rms_fwd("mlp_norm", h, g)
    h1 = _mm("mlp_up", hn, w1g, dims=_NN, grid=(L // tm, N_DEV, 1),
             a_spec=pl.BlockSpec((tm, D), lambda i, j, k: (i, 0)),
             b_spec=pl.BlockSpec((None, None, D, fs), lambda i, j, k: (j, layer, 0, 0)),
             out_shape=jax.ShapeDtypeStruct((L, D_FF), f32), out_spec=pl.BlockSpec((tm, fs), lambda i, j, k: (i, j)),
             acc_shape=(tm, fs))
    tn = 512
    h_out = _mm("mlp_down", h1, w2g, dims=_NN, grid=(L // tm, D // tn, N_DEV),
                a_spec=pl.BlockSpec((tm, fs), lambda i, j, k: (i, k)),
                b_spec=pl.BlockSpec((None, None, fs, tn), lambda i, j, k: (k, layer, 0, j)),
                out_shape=jax.ShapeDtypeStruct((L, D), f32), out_spec=pl.BlockSpec((tm, tn), lambda i, j, k: (i, j)),
                aux=[(h, pl.BlockSpec((tm, tn), lambda i, j, k: (i, j)), "e")],
                a_fn=_sqrelu, epi_fn=lambda acc, res: acc + res, acc_shape=(tm, tn))
    return h_out, hn, h1


def _mlp_bwd(dh, h, g, hn, h1, w1g, w2g, layer, dw1_buf, dw2_buf):
    L, D = h.shape
    tm = _row_tile(L)
    fs = D_FF // N_DEV
    tk = min(512, L)
    dh1 = _mm("mlp_down_dx", dh, w2g, dims=_NT, grid=(L // tm, N_DEV, 1),
              a_spec=pl.BlockSpec((tm, D), lambda i, j, k: (i, 0)),
              b_spec=pl.BlockSpec((None, None, fs, D), lambda i, j, k: (j, layer, 0, 0)),
              out_shape=jax.ShapeDtypeStruct((L, D_FF), f32), out_spec=pl.BlockSpec((tm, fs), lambda i, j, k: (i, j)),
              aux=[(h1, pl.BlockSpec((tm, fs), lambda i, j, k: (i, j)), "e")],
              epi_fn=lambda acc, pre: acc * (2.0 * jnp.maximum(pre, 0.0)), acc_shape=(tm, fs))
    dw2_buf = _mm("mlp_down_dw", h1, dh, dims=_TN, grid=(N_DEV, 1, L // tk),
                  a_spec=pl.BlockSpec((tk, fs), lambda i, j, k: (k, i)),
                  b_spec=pl.BlockSpec((tk, D), lambda i, j, k: (k, 0)),
                  out_shape=jax.ShapeDtypeStruct(dw2_buf.shape, f32),
                  out_spec=pl.BlockSpec((None, None, fs, D), lambda i, j, k: (i, layer, 0, 0)),
                  a_fn=_sqrelu, acc_shape=(fs, D), out_init=dw2_buf)
    tr = 512
    dw1_buf = _mm("mlp_up_dw", hn, dh1, dims=_TN, grid=(D // tr, N_DEV, L // tk),
                  a_spec=pl.BlockSpec((tk, tr), lambda i, j, k: (k, i)),
                  b_spec=pl.BlockSpec((tk, fs), lambda i, j, k: (k, j)),
                  out_shape=jax.ShapeDtypeStruct(dw1_buf.shape, f32),
                  out_spec=pl.BlockSpec((None, None, tr, fs), lambda i, j, k: (j, layer, i, 0)),
                  acc_shape=(tr, fs), out_init=dw1_buf)
    tn = 512
    dhn = _mm("mlp_up_dx", dh1, w1g, dims=_NT, grid=(L // tm, D // tn, N_DEV),
              a_spec=pl.BlockSpec((tm, fs), lambda i, j, k: (i, k)),
              b_spec=pl.BlockSpec((None, None, tn, fs), lambda i, j, k: (k, layer, j, 0)),
              out_shape=jax.ShapeDtypeStruct((L, D), f32), out_spec=pl.BlockSpec((tm, tn), lambda i, j, k: (i, j)),
              acc_shape=(tm, tn))
    dh_in, dg = _rms_bwd("mlp_norm_bwd", h, g, dhn, dh)
    return dh_in, dg, dw1_buf, dw2_buf


def _shift_dn(x, s, row):
    return x if s == 0 else jnp.where(row >= s, pltpu.roll(x, s, 0), 0.0)


def _shift_up(x, s, row):
    n = x.shape[0]
    return x if s == 0 else jnp.where(row < n - s, pltpu.roll(x, n - s, 0), 0.0)


def _conv_pre(x, w, b, row):
    c = jnp.broadcast_to(b, x.shape)
    for j in range(4):
        c = c + w[j:j + 1, :] * _shift_dn(x, 3 - j, row)
    return c


def _conv_fwd(name, x_arr, blk_off, w, b):
    L = x_arr.shape[0]
    C = w.shape[1]

    def f(x, wv, bv):
        c = _conv_pre(x, wv, bv, _iota(x.shape, 0))
        return c * jax.nn.sigmoid(c)

    return _ew(name, f, [(x_arr, (L, 128), lambda j, z: (0, blk_off + j)), (w, (4, 128), lambda j, z: (0, j)),
                         (b, (1, 128), lambda j, z: (0, j))],
               [((L, C), f32, (L, 128), lambda j, z: (0, j), None)], (C // 128, 1))[0]


def _conv_bwd(name, x_arr, blk_off, w, b, dy):
    L = x_arr.shape[0]
    C = w.shape[1]

    def f(x, wv, bv, g):
        row = _iota(x.shape, 0)
        c = _conv_pre(x, wv, bv, row)
        s = jax.nn.sigmoid(c)
        dc = g * (s * (1.0 + c * (1.0 - s)))
        dx = jnp.zeros_like(x)
        dw = jnp.zeros((4, 128), f32)
        r4 = _iota((4, 128), 0)
        for j in range(4):
            dx = dx + wv[j:j + 1, :] * _shift_up(dc, 3 - j, row)
            dwj = jnp.sum(dc * _shift_dn(x, 3 - j, row), axis=0, keepdims=True)
            dw = dw + jnp.where(r4 == j, jnp.broadcast_to(dwj, (4, 128)), 0.0)
        return dx, dw, jnp.sum(dc, axis=0, keepdims=True)

    return _ew(name, f, [(x_arr, (L, 128), lambda j, z: (0, blk_off + j)), (w, (4, 128), lambda j, z: (0, j)),
                         (b, (1, 128), lambda j, z: (0, j)), (dy, (L, 128), lambda j, z: (0, j))],
               [((L, C), f32, (L, 128), lambda j, z: (0, j), None), ((4, C), f32, (4, 128), lambda j, z: (0, j), None),
                ((1, C), f32, (1, 128), lambda j, z: (0, j), None)], (C // 128, 1))


def _l2norm(t):
    return t * lax.rsqrt(jnp.sum(t * t, axis=-1, keepdims=True) + 1e-6)


def _gdn_act(cq, ck, ab, alog, dtb):
    h = pl.program_id(1)
    qn = _l2norm(cq) * (128.0 ** -0.5)
    kn = _l2norm(ck)
    lane = _iota(ab.shape, 1)
    a_raw = jnp.sum(jnp.where(lane == h, ab, 0.0), axis=1, keepdims=True)
    b_raw = jnp.sum(jnp.where(lane == h + GDN_HEADS, ab, 0.0), axis=1, keepdims=True)
    lane1 = _iota(alog.shape, 1)
    al = jnp.sum(jnp.where(lane1 == h, alog, 0.0), axis=1, keepdims=True)
    db = jnp.sum(jnp.where(lane1 == h, dtb, 0.0), axis=1, keepdims=True)
    g = -jnp.exp(al) * jax.nn.softplus(a_raw + db)
    beta = jax.nn.sigmoid(b_raw)
    return qn, kn, jnp.broadcast_to(g, cq.shape), jnp.broadcast_to(beta, cq.shape)


def _each(f, *lists):
    return [f(*a) for a in zip(*lists)]


def _gdn_chunk(states, consts, vals):
    S = list(states)
    cut = [slice(128 * i, 128 * i + 128) for i in range(len(S))]
    q, k, v, gb, bb = ([t[:, c] for c in cut] for t in vals)
    C = vals[0].shape[0]
    row, col = _iota((C, C), 0), _iota((C, C), 1)
    causal, strict = row >= col, row > col
    ltri = causal.astype(f32)
    eye = (row == col).astype(f32)
    e0 = (_iota((C, 128), 1) == 0).astype(f32)
    last = _iota((C, 1), 0) == C - 1
    Gb = _each(lambda g: _dot(ltri, g), gb)
    Gc = _each(lambda g: jnp.mean(g, axis=1, keepdims=True), Gb)
    Gr = _each(lambda g: _dot(e0, g, _NT), Gb)
    bc = _each(lambda b: jnp.mean(b, axis=1, keepdims=True), bb)
    decay = _each(lambda gc, gr: jnp.where(causal, jnp.exp(jnp.where(causal, gc - gr, 0.0)), 0.0), Gc, Gr)
    kk = _each(lambda a: _dotb(a, a, _NT), k)
    A = _each(lambda b, x, d: jnp.where(strict, b * x * d, 0.0), bc, kk, decay)
    M = _each(lambda a: eye - a, A)
    P = _each(lambda a: _dot(a, a), A)
    for it in range(5):
        M = _each(lambda m, p: m + _dot(m, p), M, P)
        if it < 4:
            P = _each(lambda p: _dot(p, p), P)
    eG = _each(jnp.exp, Gc)
    u = _each(lambda m, x, b: _dotb(m, x * b), M, v, bc)
    w = _each(lambda m, x, b, e: _dotb(m, x * (b * e)), M, k, bc, eG)
    qk = _each(lambda a, b, d: _dotb(a, b, _NT) * d, q, k, decay)
    g_last = _each(lambda gc: jnp.sum(jnp.where(last, gc, 0.0), axis=0, keepdims=True), Gc)
    v_new = _each(lambda a, b, s: a - _dotb(b, s), u, w, S)
    o = _each(lambda a, e, s, b, x: _dotb(a * e, s) + _dotb(b, x), q, eG, S, qk, v_new)
    S_new = _each(lambda gl, s, a, gc, x: jnp.exp(gl) * s + _dotb(a * jnp.exp(gl - gc), x, _TN), g_last, S, k, Gc, v_new)
    return S_new, [jnp.concatenate(o, axis=1)]


def _gdn_post(o, gate, g):
    return _rms(o, g) * (gate * jax.nn.sigmoid(gate))


def _pad_row(v):
    return jnp.pad(v.astype(f32), (0, 128 - v.shape[0])).reshape(1, 128)


def _gdn_fwd(h, g_norm, w_ext, conv_w, a_log, dt_bias, o_norm_g, w_out):
    L, D = h.shape
    tm = _row_tile(L)
    nc = L // CHUNK
    H = GDN_HEADS
    hn = _rms_fwd("mix_norm", h, g_norm)
    proj = _mm_plain("gdn_in", hn, w_ext, _NN)
    zb = jnp.zeros((1, 3 * D), f32)
    cq = _conv_fwd("gdn_conv", proj, 0, conv_w, zb)
    alog, dtb = _pad_row(a_log), _pad_row(dt_bias)
    act_ins = [(cq, (tm, 128), lambda r, hh: (r, hh)), (cq, (tm, 128), lambda r, hh: (r, H + hh)),
               (proj, (tm, 128), lambda r, hh: (r, 4 * H)), (alog, (1, 128), lambda r, hh: (0, 0)),
               (dtb, (1, 128), lambda r, hh: (0, 0))]
    qn, kn, gb, bb = _ew("gdn_act", _gdn_act, act_ins,
                         [((L, D), f32, (tm, 128), lambda r, hh: (r, hh), None)] * 4, (L // tm, H))
    cblk = (CHUNK, 128 * GDN_HB)
    core_ins = [(qn, cblk, lambda u, c: (c, u)), (kn, cblk, lambda u, c: (c, u)),
                (cq, cblk, lambda u, c: (c, 2 * H // GDN_HB + u), (L, D), lambda u, c: (c, u)),
                (gb, cblk, lambda u, c: (c, u)), (bb, cblk, lambda u, c: (c, u))]
    (o,), saved_s = _scan_fwd("gdn_core", _gdn_chunk, GDN_HB, (128, 128), [], core_ins,
                              [((L, D), cblk, lambda u, c: (c, u))], H // GDN_HB, nc)
    on = o_norm_g.reshape(1, 128)
    post_ins = [(o, (tm, 128), lambda r, hh: (r, hh)), (proj, (tm, 128), lambda r, hh: (r, 3 * H + hh)),
                (on, (1, 128), lambda r, hh: (0, 0))]
    y = _ew("gdn_post", _gdn_post, post_ins, [((L, D), f32, (tm, 128), lambda r, hh: (r, hh), None)], (L // tm, H))[0]
    h_out = _mm_plain("gdn_out", y, w_out, _NN, epi_fn=lambda acc, res: acc + res, aux=[(h, "e")])
    saved = dict(hn=hn, proj=proj, cq=cq, alog=alog, dtb=dtb, act_ins=act_ins, core_ins=core_ins, saved_s=saved_s,
                 post_ins=post_ins, y=y, zb=zb)
    return h_out, saved


def _gdn_bwd(dh, h, g_norm, w_ext, conv_w, w_out, sv):
    L, D = h.shape
    tm = _row_tile(L)
    nc = L // CHUNK
    H = GDN_HEADS
    dy = _mm_plain("gdn_out_dx", dh, w_out, _NT)
    dw_out = _mm_plain("gdn_out_dw", sv["y"], dh, _TN)
    hd = ((L, D), f32, (tm, 128), lambda r, hh: (r, hh), None)
    d_o, d_gate, d_on = _ew("gdn_post_bwd", _vjp_fn(_gdn_post, 3),
                            sv["post_ins"] + [(dy, (tm, 128), lambda r, hh: (r, hh))],
                            [hd, hd, ((1, 128), f32, (1, 128), lambda r, hh: (0, 0), "all")], (L // tm, H))
    cblk = (CHUNK, 128 * GDN_HB)
    _, (dqn, dkn, dv, dgb, dbb) = _scan_bwd("gdn_core_bwd", _gdn_chunk, GDN_HB, (128, 128), [], sv["core_ins"],
                                            sv["saved_s"], [(d_o, cblk, lambda u, c: (c, u))], H // GDN_HB, nc)
    cts = [(t, (tm, 128), lambda r, hh: (r, hh)) for t in (dqn, dkn, dgb, dbb)]
    row128 = ((1, 128), f32, (1, 128), lambda r, hh: (0, 0), "all")
    d_cq, d_ck, d_ab, d_alog, d_dtb = _ew(
        "gdn_act_bwd", _vjp_fn(_gdn_act, 5), sv["act_ins"] + cts,
        [hd, hd, ((L, 128), f32, (tm, 128), lambda r, hh: (r, 0), "inner"), row128, row128], (L // tm, H))
    d_conv_out = jnp.concatenate([d_cq, d_ck, dv], axis=1)
    d_conv_in, d_conv_w, _ = _conv_bwd("gdn_conv_bwd", sv["proj"], 0, conv_w, sv["zb"], d_conv_out)
    d_proj = jnp.concatenate([d_conv_in, d_gate, d_ab], axis=1)
    dw_ext = _mm_plain("gdn_in_dw", sv["hn"], d_proj, _TN)
    dhn = _mm_plain("gdn_in_dx", d_proj, w_ext, _NT)
    dh_in, dg = _rms_bwd("mix_norm_bwd", h, g_norm, dhn, dh)
    grads = dict(norm=dg, w_ext=dw_ext, conv_w=d_conv_w, a_log=d_alog[0, :H], dt_bias=d_dtb[0, :H],
                 o_norm_g=d_on[0], w_out=dw_out)
    return dh_in, grads


def _expand_lanes(row, width, rep):
    sel = ((_iota((128, width), 1) // rep) == _iota((128, width), 0)).astype(f32)
    return jnp.mean(_dot(jnp.broadcast_to(row, (8, 128)), sel), axis=0, keepdims=True)


def _s5_params(lre, lim, ldt, wbr, wbi):
    dt = jnp.exp(_expand_lanes(ldt, 512, S5_STATE))
    mag = jnp.exp(lre * dt)
    ang = lim * dt
    abr, abi = mag * jnp.cos(ang), mag * jnp.sin(ang)
    nr = abr - 1.0
    den = lre * lre + lim * lim
    cr = (nr * lre + abi * lim) / den
    ci = (abi * lre - nr * lim) / den
    return abr, abi, cr * wbr - ci * wbi, cr * wbi + ci * wbr


def _s5_scan(name, xr, xi, ar, ai, rev, want_prev):
    L, W = xr.shape
    nb = L // 8
    n_out = 4 if want_prev else 2

    def body(xr_ref, xi_ref, ar_ref, ai_ref, *outs):
        a_r = ar_ref[...]
        a_i = -ai_ref[...] if rev else ai_ref[...]

        def cm(p, q):
            return p[0] * q[0] - p[1] * q[1], p[0] * q[1] + p[1] * q[0]

        a1 = (a_r, a_i)
        a2 = cm(a1, a1)
        a3 = cm(a2, a1)
        a4 = cm(a2, a2)
        pw = [a1, a2, a3, a4, cm(a4, a1), cm(a4, a2), cm(a4, a3), cm(a4, a4)]
        row = _iota((8, 128), 0)
        tab_r = jnp.zeros((8, 128), f32)
        tab_i = jnp.zeros((8, 128), f32)
        for t in range(8):
            idx = 7 - t if rev else t
            tab_r = jnp.where(row == idx, jnp.broadcast_to(pw[t][0], (8, 128)), tab_r)
            tab_i = jnp.where(row == idx, jnp.broadcast_to(pw[t][1], (8, 128)), tab_i)
        lv = [(d, jnp.broadcast_to(p[0], (8, 128)), jnp.broadcast_to(p[1], (8, 128))) for d, p in ((1, a1), (2, a2), (4, a4))]

        def step(i, carry):
            cr, ci = carry
            blk = nb - 1 - i if rev else i
            r0 = pl.multiple_of(blk * 8, 8)
            x_r = xr_ref[pl.ds(r0, 8), :]
            x_i = xi_ref[pl.ds(r0, 8), :]
            for d, p_r, p_i in lv:
                if rev:
                    s_r = jnp.where(row < 8 - d, pltpu.roll(x_r, 8 - d, 0), 0.0)
                    s_i = jnp.where(row < 8 - d, pltpu.roll(x_i, 8 - d, 0), 0.0)
                else:
                    s_r = jnp.where(row >= d, pltpu.roll(x_r, d, 0), 0.0)
                    s_i = jnp.where(row >= d, pltpu.roll(x_i, d, 0), 0.0)
                x_r, x_i = x_r + p_r * s_r - p_i * s_i, x_i + p_r * s_i + p_i * s_r
            x_r, x_i = x_r + tab_r * cr - tab_i * ci, x_i + tab_r * ci + tab_i * cr
            outs[0][pl.ds(r0, 8), :] = x_r
            outs[1][pl.ds(r0, 8), :] = x_i
            if want_prev:
                outs[2][pl.ds(r0, 8), :] = jnp.where(row >= 1, pltpu.roll(x_r, 1, 0), cr)
                outs[3][pl.ds(r0, 8), :] = jnp.where(row >= 1, pltpu.roll(x_i, 1, 0), ci)
            e = 0 if rev else 7
            return jnp.broadcast_to(x_r[e:e + 1, :], (8, 128)), jnp.broadcast_to(x_i[e:e + 1, :], (8, 128))

        lax.fori_loop(0, nb, step, (jnp.zeros((8, 128), f32), jnp.zeros((8, 128), f32)))

    col = pl.BlockSpec((L, 128), lambda q, z: (0, q))
    aspec = pl.BlockSpec((None, 1, 128), lambda q, z: (q // 4, 0, q % 4))
    return pl.pallas_call(
        body, name=name, grid=(W // 128, 1), in_specs=[col, col, aspec, aspec], out_specs=[col] * n_out,
        out_shape=[jax.ShapeDtypeStruct((L, W), f32)] * n_out, compiler_params=_params(2),
    )(xr, xi, ar, ai)


def _blockdiag(t, n_in, n_out):
    t4 = t.reshape(8, 8, n_in, n_out)
    return jnp.einsum("jaio,ab->jaibo", t4, jnp.eye(8, dtype=t.dtype)).reshape(8, 8 * n_in, 8 * n_out)


def _blockdiag_t(w, n_in, n_out):
    w5 = w.reshape(8, 8, n_in, 8, n_out)
    return jnp.einsum("jaibo,ab->jaio", w5, jnp.eye(8, dtype=w.dtype)).reshape(64, n_in, n_out)


def _glu(ag, h):
    n = ag.shape[1] // 2
    return h + ag[:, :n] * jax.nn.sigmoid(ag[:, n:])


def _s5_fwd(h, g_norm, w_in, lam_re, lam_im, log_dt, b_re, b_im, c_re, c_im, d_skip, w_out_g):
    L, D = h.shape
    tm = _row_tile(L)
    W = 8 * 512
    hn = _rms_fwd("mix_norm", h, g_norm)
    u = _mm_plain("s5_in", hn, w_in, _NN)
    lre, lim = lam_re.reshape(8, 1, 512), lam_im.reshape(8, 1, 512)
    ldt = jnp.pad(log_dt.reshape(8, 1, 8), ((0, 0), (0, 0), (0, 120)))
    wbr = _blockdiag(b_re.transpose(0, 2, 1), 16, 64)
    wbi = _blockdiag(b_im.transpose(0, 2, 1), 16, 64)
    wcr = _blockdiag(c_re.transpose(0, 2, 1), 64, 16)
    wci = _blockdiag(c_im.transpose(0, 2, 1), 64, 16)
    jb = lambda shape: (shape, lambda j, z: (j, 0, 0))
    par_ins = [(lre, *jb((None, 1, 512))), (lim, *jb((None, 1, 512))), (ldt, *jb((None, 1, 128))),
               (wbr, *jb((None, 128, 512))), (wbi, *jb((None, 128, 512)))]
    abr, abi, bbr, bbi = _ew("s5_params", _s5_params, par_ins,
                             [((8, 1, 512), f32, *jb((None, 1, 512)), None)] * 2
                             + [((8, 128, 512), f32, *jb((None, 128, 512)), None)] * 2, (8, 1))

    def bu(name, wb):
        return _mm(name, u, wb, dims=_NN, grid=(L // tm, 8, 1),
                   a_spec=pl.BlockSpec((tm, 128), lambda i, j, k: (i, j)),
                   b_spec=pl.BlockSpec((None, 128, 512), lambda i, j, k: (j, 0, 0)),
                   out_shape=jax.ShapeDtypeStruct((L, W), f32), out_spec=pl.BlockSpec((tm, 512), lambda i, j, k: (i, j)),
                   acc_shape=(tm, 512))

    bur, bui = bu("s5_bu", bbr), bu("s5_bu", bbi)
    sr, si, pr, pi = _s5_scan("s5_scan", bur, bui, abr, abi, False, True)
    d_row = d_skip.reshape(1, D)
    cspec = dict(a_spec=pl.BlockSpec((tm, 512), lambda i, j, k: (i, j)),
                 b_spec=pl.BlockSpec((None, 512, 128), lambda i, j, k: (j, 0, 0)),
                 out_shape=jax.ShapeDtypeStruct((L, D), f32), out_spec=pl.BlockSpec((tm, 128), lambda i, j, k: (i, j)),
                 acc_shape=(tm, 128))
    e128 = pl.BlockSpec((tm, 128), lambda i, j, k: (i, j))
    pre1 = _mm("s5_c_re", sr, wcr, dims=_NN, grid=(L // tm, 8, 1), **cspec)
    pre = _mm("s5_c_im", si, wci, dims=_NN, grid=(L // tm, 8, 1),
              aux=[(pre1, e128, "e"), (u, e128, "e"), (d_row, pl.BlockSpec((1, 128), lambda i, j, k: (0, j)), "e")],
              epi_fn=lambda acc, p1, uu, dd: p1 - acc + dd * uu, **cspec)
    ws = D // N_DEV * 2
    ag = _mm("s5_out", pre, w_out_g, dims=_NN, grid=(L // tm, N_DEV, 1),
             a_spec=pl.BlockSpec((tm, D), lambda i, j, k: (i, 0)),
             b_spec=pl.BlockSpec((None, None, D, ws), lambda i, j, k: (j, 0, 0, 0)),
             out_shape=jax.ShapeDtypeStruct((L, 2 * D), f32), out_spec=pl.BlockSpec((tm, ws), lambda i, j, k: (i, j)),
             a_fn=jax.nn.gelu, acc_shape=(tm, ws))
    h_out = _ew("s5_glu", _glu, [(ag, *_rows(tm, 2 * D)), (h, *_rows(tm, D))],
                [((L, D), f32, *_rows(tm, D), None)], (L // tm, 1))[0]
    saved = dict(hn=hn, u=u, par_ins=par_ins, abr=abr, abi=abi, bbr=bbr, bbi=bbi, wcr=wcr, wci=wci, sr=sr, si=si,
                 pr=pr, pi=pi, pre=pre, ag=ag, d_row=d_row)
    return h_out, saved


def _s5_bwd(dh, h, g_norm, w_in, w_out_g, sv):
    L, D = h.shape
    tm = _row_tile(L)
    tk = min(512, L)
    W = 8 * 512
    ws = D // N_DEV * 2
    u, pre, d_row = sv["u"], sv["pre"], sv["d_row"]
    d_ag = _ew("s5_glu_bwd", lambda ag, hv, g: _vjp_fn(_glu, 2)(ag, hv, g)[0],
               [(sv["ag"], *_rows(tm, 2 * D)), (h, *_rows(tm, D)), (dh, *_rows(tm, D))],
               [((L, 2 * D), f32, *_rows(tm, 2 * D), None)], (L // tm, 1))[0]
    tr = 512
    dw_out = _mm("s5_out_dw", pre, d_ag, dims=_TN, grid=(D // tr, N_DEV, L // tk),
                 a_spec=pl.BlockSpec((tk, tr), lambda i, j, k: (k, i)),
                 b_spec=pl.BlockSpec((tk, ws), lambda i, j, k: (k, j)),
                 out_shape=jax.ShapeDtypeStruct((N_DEV, 1, D, ws), f32),
                 out_spec=pl.BlockSpec((None, None, tr, ws), lambda i, j, k: (j, 0, i, 0)),
                 a_fn=jax.nn.gelu, acc_shape=(tr, ws))
    tn = 512
    dpre = _mm("s5_out_dx", d_ag, w_out_g, dims=_NT, grid=(L // tm, D // tn, N_DEV),
               a_spec=pl.BlockSpec((tm, ws), lambda i, j, k: (i, k)),
               b_spec=pl.BlockSpec((None, None, tn, ws), lambda i, j, k: (k, 0, j, 0)),
               out_shape=jax.ShapeDtypeStruct((L, D), f32), out_spec=pl.BlockSpec((tm, tn), lambda i, j, k: (i, j)),
               aux=[(pre, pl.BlockSpec((tm, tn), lambda i, j, k: (i, j)), "e")],
               epi_fn=lambda acc, p: _vjp_fn(jax.nn.gelu, 1)(p, acc)[0], acc_shape=(tm, tn))
    d_d = _ew("s5_dskip", lambda a, b: jnp.sum(a * b, axis=0, keepdims=True),
              [(dpre, *_rows(tm, D)), (u, *_rows(tm, D))], [((1, D), f32, *_const((1, D)), "all")], (L // tm, 1))[0]
    neg = lambda acc: -acc
    dsspec = dict(dims=_NT, grid=(L // tm, 8, 1), a_spec=pl.BlockSpec((tm, 128), lambda i, j, k: (i, j)),
                  b_spec=pl.BlockSpec((None, 512, 128), lambda i, j, k: (j, 0, 0)),
                  out_shape=jax.ShapeDtypeStruct((L, W), f32), out_spec=pl.BlockSpec((tm, 512), lambda i, j, k: (i, j)),
                  acc_shape=(tm, 512))
    dsr = _mm("s5_c_re_dx", dpre, sv["wcr"], **dsspec)
    dsi = _mm("s5_c_im_dx", dpre, sv["wci"], epi_fn=neg, **dsspec)
    dwspec = dict(dims=_TN, grid=(8, 1, L // tk), a_spec=pl.BlockSpec((tk, 512), lambda i, j, k: (k, i)),
                  b_spec=pl.BlockSpec((tk, 128), lambda i, j, k: (k, i)),
                  out_shape=jax.ShapeDtypeStruct((8, 512, 128), f32),
                  out_spec=pl.BlockSpec((None, 512, 128), lambda i, j, k: (i, 0, 0)), acc_shape=(512, 128))
    dwcr = _mm("s5_c_re_dw", sv["sr"], dpre, **dwspec)
    dwci = _mm("s5_c_im_dw", sv["si"], dpre, epi_fn=neg, **dwspec)
    lr, li = _s5_scan("s5_scan_bwd", dsr, dsi, sv["abr"], sv["abi"], True, False)

    def da(lrv, liv, prv, piv):
        return (jnp.sum(lrv * prv + liv * piv, axis=0, keepdims=True),
                jnp.sum(liv * prv - lrv * piv, axis=0, keepdims=True))

    sblk = ((tm, 512), lambda j, r: (r, j))
    dabr, dabi = _ew("s5_dlam", da, [(lr, *sblk), (li, *sblk), (sv["pr"], *sblk), (sv["pi"], *sblk)],
                     [((8, 1, 512), f32, (None, 1, 512), lambda j, r: (j, 0, 0), "inner")] * 2, (8, L // tm))
    dbspec = dict(dims=_TN, grid=(8, 1, L // tk), a_spec=pl.BlockSpec((tk, 128), lambda i, j, k: (k, i)),
                  b_spec=pl.BlockSpec((tk, 512), lambda i, j, k: (k, i)),
                  out_shape=jax.ShapeDtypeStruct((8, 128, 512), f32),
                  out_spec=pl.BlockSpec((None, 128, 512), lambda i, j, k: (i, 0, 0)), acc_shape=(128, 512))
    dbbr = _mm("s5_bu_dw", u, lr, **dbspec)
    dbbi = _mm("s5_bu_dw", u, li, **dbspec)
    duspec = dict(dims=_NT, grid=(L // tm, 8, 1), a_spec=pl.BlockSpec((tm, 512), lambda i, j, k: (i, j)),
                  b_spec=pl.BlockSpec((None, 128, 512), lambda i, j, k: (j, 0, 0)),
                  out_shape=jax.ShapeDtypeStruct((L, D), f32), out_spec=pl.BlockSpec((tm, 128), lambda i, j, k: (i, j)),
                  acc_shape=(tm, 128))
    e128 = pl.BlockSpec((tm, 128), lambda i, j, k: (i, j))
    du1 = _mm("s5_bu_dx_re", lr, sv["bbr"], **duspec)
    du = _mm("s5_bu_dx_im", li, sv["bbi"],
             aux=[(du1, e128, "e"), (dpre, e128, "e"), (d_row, pl.BlockSpec((1, 128), lambda i, j, k: (0, j)), "e")],
             epi_fn=lambda acc, d1, dp, dd: acc + d1 + dp * dd, **duspec)
    jb = lambda shape: (shape, lambda j, z: (j, 0, 0))
    cts = [(dabr, *jb((None, 1, 512))), (dabi, *jb((None, 1, 512))), (dbbr, *jb((None, 128, 512))),
           (dbbi, *jb((None, 128, 512)))]
    dlre, dlim, dldt, dwbr, dwbi = _ew(
        "s5_params_bwd", _vjp_fn(_s5_params, 5), sv["par_ins"] + cts,
        [((8, 1, 512), f32, *jb((None, 1, 512)), None)] * 2 + [((8, 1, 128), f32, *jb((None, 1, 128)), None)]
        + [((8, 128, 512), f32, *jb((None, 128, 512)), None)] * 2, (8, 1))
    dw_in = _mm_plain("s5_in_dw", sv["hn"], du, _TN)
    dhn = _mm_plain("s5_in_dx", du, w_in, _NT)
    dh_in, dg = _rms_bwd("mix_norm_bwd", h, g_norm, dhn, dh)
    grads = dict(norm=dg, w_in=dw_in, lam_re=dlre.reshape(64, 64), lam_im=dlim.reshape(64, 64),
                 log_dt=dldt[:, 0, :8].reshape(64),
                 b_re=_blockdiag_t(dwbr, 16, 64).transpose(0, 2, 1), b_im=_blockdiag_t(dwbi, 16, 64).transpose(0, 2, 1),
                 c_re=_blockdiag_t(dwcr, 64, 16).transpose(0, 2, 1), c_im=_blockdiag_t(dwci, 64, 16).transpose(0, 2, 1),
                 d=d_d[0], w_out=dw_out)
    return dh_in, grads


def _m2_act(dt_raw, dtbias, alog):
    dt = jax.nn.softplus(dt_raw + dtbias)
    da = dt * (-jnp.exp(alog))
    sel = ((_iota((128, M2_INNER), 1) // 64) == _iota((128, M2_INNER), 0)).astype(f32)
    return _dot(dt, sel), _dot(da, sel)


def _m2_dexp(d):
    return _expand_lanes(d, M2_INNER, 64)


def _ssd_chunk(states, consts, vals):
    (dsk,) = consts
    S = list(states)
    n = len(S)
    cut = [slice(128 * i, 128 * i + 128) for i in range(n)]
    x, dtb, dab = ([t[:, c] for c in cut] for t in vals[:3])
    dsk = [dsk[:, c] for c in cut]
    B = [vals[3][:, cut[i // 2]] for i in range(n)]
    Cm = [vals[4][:, cut[i // 2]] for i in range(n)]
    C = vals[0].shape[0]
    row, col = _iota((C, C), 0), _iota((C, C), 1)
    causal = row >= col
    ltri = causal.astype(f32)
    lane = _iota((C, 128), 1)
    last = _iota((C, 128), 0) == C - 1
    eye128 = _iota((128, 128), 0) == _iota((128, 128), 1)
    head = [jnp.logical_and(lane >= 64 * hh, lane < 64 * hh + 64) for hh in range(2)]
    pick = [(lane == 64 * hh).astype(f32) for hh in range(2)]
    xdt = _each(lambda a, b: a * b, x, dtb)
    cb = _each(lambda c, b: _dotb(c, b, _NT), Cm[::2], B[::2])
    cum = _each(lambda a: _dot(ltri, a), dab)
    clast = _each(lambda a: jnp.sum(jnp.where(last, a, 0.0), axis=0, keepdims=True), cum)
    st = _each(lambda a, cl, cu, b: _dotb(a * jnp.exp(cl - cu), b, _TN), xdt, clast, cum, B)
    y = _each(lambda c, s, cu: _dotb(c, s, _NT) * jnp.exp(cu), Cm, S, cum)
    for hh in range(2):
        ccol = _each(lambda cu: jnp.sum(jnp.where(head[hh], cu, 0.0), axis=1, keepdims=True) * (1.0 / 64), cum)
        crow = _each(lambda cu: _dot(pick[hh], cu, _NT), cum)
        lm = _each(lambda a, b: jnp.where(causal, jnp.exp(jnp.where(causal, a - b, 0.0)), 0.0), ccol, crow)
        y = [y[i] + _dotb(cb[i // 2] * lm[i], jnp.where(head[hh], xdt[i], 0.0)) for i in range(n)]
    cdcol = _each(lambda cl: jnp.sum(jnp.where(eye128, jnp.broadcast_to(jnp.exp(cl), (128, 128)), 0.0),
                                     axis=1, keepdims=True), clast)
    S_new = _each(lambda c, s, t: c * s + t, cdcol, S, st)
    out = _each(lambda a, d, b: a + d * b, y, dsk, x)
    return S_new, [jnp.concatenate(out, axis=1)]


def _m2_post(yc, z, ng):
    return _rms(yc * (z * jax.nn.sigmoid(z)), ng)


def _m2_fwd(h, g_norm, w_ext, conv_w, conv_b, dt_bias, a_log, d_skip, norm_g, w_out):
    L, D = h.shape
    tm = _row_tile(L)
    nc = L // CHUNK
    NI = M2_INNER
    hn = _rms_fwd("mix_norm", h, g_norm)
    proj = _mm_plain("m2_in", hn, w_ext, _NN)
    xbc = _conv_fwd("m2_conv", proj, NI // 128, conv_w, conv_b)
    dtb_row, alog_row, d_pad = _pad_row(dt_bias), _pad_row(a_log), _pad_row(d_skip)
    act_ins = [(proj, (tm, 128), lambda r, z: (r, 3 * NI // 128)), (dtb_row, *_const((1, 128))),
               (alog_row, *_const((1, 128)))]
    dtb, dab = _ew("m2_act", _m2_act, act_ins, [((L, NI), f32, *_rows(tm, NI), None)] * 2, (L // tm, 1))
    dsk = _ew("m2_dexp", _m2_dexp, [(d_pad, *_const((1, 128)))], [((1, NI), f32, *_const((1, NI)), None)], (1, 1))[0]
    GB = M2_GB
    x_blk, bc_blk = (CHUNK, 256 * GB), (CHUNK, 128 * GB)
    cins = [(dsk, (1, 256 * GB), lambda u, c: (0, u))]
    core_ins = [(xbc, x_blk, lambda u, c: (c, u), (L, NI), lambda u, c: (c, u)),
                (dtb, x_blk, lambda u, c: (c, u)), (dab, x_blk, lambda u, c: (c, u)),
                (xbc, bc_blk, lambda u, c: (c, 16 // GB + u), (L, D), lambda u, c: (c, u)),
                (xbc, bc_blk, lambda u, c: (c, 24 // GB + u), (L, D), lambda u, c: (c, u))]
    (yc,), saved_s = _scan_fwd("m2_core", _ssd_chunk, 2 * GB, (128, 128), cins, core_ins,
                               [((L, NI), x_blk, lambda u, c: (c, u))], 8 // GB, nc)
    gblk = ((tm, 256), lambda g, r: (r, g))
    post_ins = [(yc, *gblk), (proj, *gblk), (norm_g, (1, 256), lambda g, r: (0, g))]
    yn = _ew("m2_post", _m2_post, post_ins, [((L, NI), f32, *gblk, None)], (8, L // tm))[0]
    h_out = _mm_plain("m2_out", yn, w_out, _NN, epi_fn=lambda acc, res: acc + res, aux=[(h, "e")])
    saved = dict(hn=hn, proj=proj, act_ins=act_ins, d_pad=d_pad, cins=cins, core_ins=core_ins, saved_s=saved_s,
                 post_ins=post_ins, yn=yn)
    return h_out, saved


def _m2_bwd(dh, h, g_norm, w_ext, conv_w, conv_b, w_out, sv):
    L, D = h.shape
    tm = _row_tile(L)
    nc = L // CHUNK
    NI = M2_INNER
    dyn = _mm_plain("m2_out_dx", dh, w_out, _NT)
    dw_out = _mm_plain("m2_out_dw", sv["yn"], dh, _TN)
    gblk = ((tm, 256), lambda g, r: (r, g))
    d_yc, d_z, d_ng = _ew("m2_post_bwd", _vjp_fn(_m2_post, 3), sv["post_ins"] + [(dyn, *gblk)],
                          [((L, NI), f32, *gblk, None)] * 2 + [((1, NI), f32, (1, 256), lambda g, r: (0, g), "inner")],
                          (8, L // tm))
    (d_dsk,), (dx, d_dtb, d_dab, dB, dC) = _scan_bwd(
        "m2_core_bwd", _ssd_chunk, 2 * M2_GB, (128, 128), sv["cins"], sv["core_ins"], sv["saved_s"],
        [(d_yc, (CHUNK, 256 * M2_GB), lambda u, c: (c, u))], 8 // M2_GB, nc)
    row128 = ((1, 128), f32, *_const((1, 128)), "all")
    d_dt_raw, d_dtbias, d_alog = _ew(
        "m2_act_bwd", _vjp_fn(_m2_act, 3), sv["act_ins"] + [(d_dtb, *_rows(tm, NI)), (d_dab, *_rows(tm, NI))],
        [((L, 128), f32, *_rows(tm, 128), None), row128, row128], (L // tm, 1))
    d_d = _ew("m2_dexp_bwd", _vjp_fn(_m2_dexp, 1), [(sv["d_pad"], *_const((1, 128))), (d_dsk, *_const((1, NI)))],
              [((1, 128), f32, *_const((1, 128)), None)], (1, 1))[0]
    d_conv_out = jnp.concatenate([dx, dB, dC], axis=1)
    d_conv_in, d_conv_w, d_conv_b = _conv_bwd("m2_conv_bwd", sv["proj"], NI // 128, conv_w, conv_b, d_conv_out)
    d_proj = jnp.concatenate([d_z, d_conv_in, d_dt_raw], axis=1)
    dw_ext = _mm_plain("m2_in_dw", sv["hn"], d_proj, _TN)
    dhn = _mm_plain("m2_in_dx", d_proj, w_ext, _NT)
    dh_in, dg = _rms_bwd("mix_norm_bwd", h, g_norm, dhn, dh)
    grads = dict(norm=dg, w_ext=dw_ext, conv_w=d_conv_w, conv_b=d_conv_b, dt_bias=d_dtbias[0, :M2_HEADS],
                 a_log=d_alog[0, :M2_HEADS], d=d_d[0, :M2_HEADS], norm_g=d_ng, w_out=dw_out)
    return dh_in, grads


def _mesh_pos():
    return lax.axis_index("x"), lax.axis_index("y"), lax.axis_index("c")


def _flip(pos, p):
    x, y, c = pos
    return (1 - x if p & 4 else x, 1 - y if p & 2 else y, 1 - c if p & 1 else c)


def _index(pos):
    return 4 * pos[0] + 2 * pos[1] + pos[2]


def _exchange(name, arrays, scatter):
    n = len(arrays)

    def body(*refs):
        ins, outs = refs[:n], refs[n:2 * n]
        send_sems, recv_sems, loc_sems = refs[2 * n:]
        me = _mesh_pos()
        mi = _index(me)

        def src(w, t):
            return ins[w].at[t] if scatter else ins[w]

        def copy(w, p):
            peer = _flip(me, p)
            return pltpu.make_async_remote_copy(
                src_ref=src(w, _index(peer)), dst_ref=outs[w].at[mi], send_sem=send_sems.at[w, p - 1],
                recv_sem=recv_sems.at[w, p - 1], device_id=peer, device_id_type=MESH)

        def arrival(w, p):
            peer = _flip(me, p)
            return pltpu.make_async_remote_copy(
                src_ref=src(w, mi), dst_ref=outs[w].at[_index(peer)], send_sem=send_sems.at[w, p - 1],
                recv_sem=recv_sems.at[w, p - 1], device_id=peer, device_id_type=MESH)

        local = [pltpu.make_async_copy(src(w, mi), outs[w].at[mi], loc_sems.at[w]) for w in range(n)]
        for cp in local:
            cp.start()
        sends = [copy(w, p) for p in range(1, N_DEV) for w in range(n)]
        for cp in sends:
            cp.start()
        for p in range(1, N_DEV):
            for w in range(n):
                arrival(w, p).wait_recv()
        for cp in sends:
            cp.wait_send()
        for cp in local:
            cp.wait()

    hbm = pl.BlockSpec(memory_space=pl.ANY)
    out_shape = [jax.ShapeDtypeStruct(a.shape if scatter else (N_DEV,) + a.shape, a.dtype) for a in arrays]
    return pl.pallas_call(
        body, name=name, in_specs=[hbm] * n, out_specs=[hbm] * n, out_shape=out_shape,
        scratch_shapes=[pltpu.SemaphoreType.DMA((n, N_DEV - 1)), pltpu.SemaphoreType.DMA((n, N_DEV - 1)),
                        pltpu.SemaphoreType.DMA((n,))],
    )(*arrays)


def _adamw(name, parts, w, m, v):
    R, C = w.shape
    tr = R if R <= 256 else (256 if C <= 512 else 128)
    bc1 = 1.0 - ADAM_B1 ** ADAM_STEP
    bc2 = 1.0 - ADAM_B2 ** ADAM_STEP

    def f(p, wv, mv, vv):
        g = p[0]
        for i in range(1, N_DEV):
            g = g + p[i]
        m2 = ADAM_B1 * mv + (1.0 - ADAM_B1) * g
        v2 = ADAM_B2 * vv + (1.0 - ADAM_B2) * jnp.square(g)
        delta = -ADAM_LR * ((m2 / bc1) / (jnp.sqrt(v2 / bc2) + ADAM_EPS) + ADAM_WD * wv)
        return g, delta, m2, v2

    blk = ((tr, C), lambda r, z: (r, 0))
    return _ew(name, f, [(parts, (N_DEV, tr, C), lambda r, z: (0, r, 0)), (w, *blk), (m, *blk), (v, *blk)],
               [((R, C), f32, *blk, None)] * 4, (R // tr, 1))


_WEIGHTS = ["norm_mix_g", "norm_mlp_g", "mlp_w1", "mlp_w2", "gdn_w_in", "gdn_conv_w", "gdn_a_log", "gdn_dt_bias",
            "gdn_o_norm_g", "gdn_w_out", "s5_w_in", "s5_lam_re", "s5_lam_im", "s5_log_dt", "s5_b_re", "s5_b_im",
            "s5_c_re", "s5_c_im", "s5_d", "s5_w_out", "m2_w_in", "m2_conv_w", "m2_conv_b", "m2_dt_bias", "m2_a_log",
            "m2_d", "m2_norm_g", "m2_w_out", "final_norm_g"]
_SHARDED = ["mlp_w1", "mlp_w2", "gdn_w_in", "gdn_w_out", "s5_w_in", "s5_w_out", "m2_w_in", "m2_w_out",
            "gdn_conv_w", "m2_conv_w", "m2_conv_b", "m2_norm_g"]
_MATRICES = _SHARDED[:8]
_REPLICATED = [n for n in _WEIGHTS if n not in _SHARDED]
_GDN_IN, _M2_IN = 4112, 6176
_LAYER_KIND = (0, 1, 2, 0)


def _as2d(a):
    return a.reshape(-1, a.shape[-1])


def _cols_from_shards(g, width):
    return g.transpose(1, 0, 2).reshape(g.shape[1], width)


def _cols_to_shards(a, width):
    return a[:, :width].reshape(a.shape[0], N_DEV, width // N_DEV).transpose(1, 0, 2)


def kernel(x, norm_mix_g, norm_mlp_g, mlp_w1, mlp_w2, gdn_w_in, gdn_conv_w, gdn_a_log, gdn_dt_bias, gdn_o_norm_g, gdn_w_out, s5_w_in, s5_lam_re, s5_lam_im, s5_log_dt, s5_b_re, s5_b_im, s5_c_re, s5_c_im, s5_d, s5_w_out, m2_w_in, m2_conv_w, m2_conv_b, m2_dt_bias, m2_a_log, m2_d, m2_norm_g, m2_w_out, final_norm_g, loss_target, m_norm_mix_g, m_norm_mlp_g, m_mlp_w1, m_mlp_w2, m_gdn_w_in, m_gdn_conv_w, m_gdn_a_log, m_gdn_dt_bias, m_gdn_o_norm_g, m_gdn_w_out, m_s5_w_in, m_s5_lam_re, m_s5_lam_im, m_s5_log_dt, m_s5_b_re, m_s5_b_im, m_s5_c_re, m_s5_c_im, m_s5_d, m_s5_w_out, m_m2_w_in, m_m2_conv_w, m_m2_conv_b, m_m2_dt_bias, m_m2_a_log, m_m2_d, m_m2_norm_g, m_m2_w_out, m_final_norm_g, v_norm_mix_g, v_norm_mlp_g, v_mlp_w1, v_mlp_w2, v_gdn_w_in, v_gdn_conv_w, v_gdn_a_log, v_gdn_dt_bias, v_gdn_o_norm_g, v_gdn_w_out, v_s5_w_in, v_s5_lam_re, v_s5_lam_im, v_s5_log_dt, v_s5_b_re, v_s5_b_im, v_s5_c_re, v_s5_c_im, v_s5_d, v_s5_w_out, v_m2_w_in, v_m2_conv_w, v_m2_conv_b, v_m2_dt_bias, v_m2_a_log, v_m2_d, v_m2_norm_g, v_m2_w_out, v_final_norm_g):
    args = locals()
    W = {n: args[n] for n in _WEIGHTS}
    MOM = {n: args["m_" + n] for n in _WEIGHTS}
    VAR = {n: args["v_" + n] for n in _WEIGHTS}
    h = x[0]
    target = loss_target[0]
    L, D = h.shape

    sends = [W[n].astype(bf16) if n in _MATRICES else _as2d(W[n]) for n in _SHARDED]
    G = dict(zip(_SHARDED, _exchange("gather_weights", sends, scatter=False)))
    w1g, w2g = G["mlp_w1"], G["mlp_w2"]
    gdn_in = [jnp.pad(_cols_from_shards(G["gdn_w_in"][:, j], _GDN_IN), ((0, 0), (0, GDN_EXT - _GDN_IN))) for j in range(2)]
    gdn_out = [G["gdn_w_out"][:, j].reshape(D, D) for j in range(2)]
    gdn_conv = [_cols_from_shards(G["gdn_conv_w"][:, 4 * j:4 * j + 4], 3 * D) for j in range(2)]
    s5_in = G["s5_w_in"].reshape(D, D)
    s5_out_g = G["s5_w_out"]
    m2_in = jnp.pad(_cols_from_shards(G["m2_w_in"][:, 0], _M2_IN), ((0, 0), (0, M2_EXT - _M2_IN)))
    m2_out = G["m2_w_out"].reshape(M2_INNER, D)
    m2_conv = _cols_from_shards(G["m2_conv_w"], 2 * M2_INNER)
    m2_cb = _cols_from_shards(G["m2_conv_b"], 2 * M2_INNER)
    m2_ng = _cols_from_shards(G["m2_norm_g"], M2_INNER)

    def mixer_fwd(i, hv):
        kind, j = _LAYER_KIND[i], i // 3
        gn = norm_mix_g[i].reshape(1, D)
        if kind == 0:
            return _gdn_fwd(hv, gn, gdn_in[j], gdn_conv[j], gdn_a_log[j], gdn_dt_bias[j], gdn_o_norm_g[j], gdn_out[j])
        if kind == 1:
            return _s5_fwd(hv, gn, s5_in, s5_lam_re[0], s5_lam_im[0], s5_log_dt[0], s5_b_re[0], s5_b_im[0],
                           s5_c_re[0], s5_c_im[0], s5_d[0], s5_out_g)
        return _m2_fwd(hv, gn, m2_in, m2_conv, m2_cb, m2_dt_bias[0], m2_a_log[0], m2_d[0], m2_ng, m2_out)

    def mixer_bwd(i, dh, hv, sv):
        kind, j = _LAYER_KIND[i], i // 3
        gn = norm_mix_g[i].reshape(1, D)
        if kind == 0:
            return _gdn_bwd(dh, hv, gn, gdn_in[j], gdn_conv[j], gdn_out[j], sv)
        if kind == 1:
            return _s5_bwd(dh, hv, gn, s5_in, s5_out_g, sv)
        return _m2_bwd(dh, hv, gn, m2_in, m2_conv, m2_cb, m2_out, sv)

    tape = []
    for i in range(4):
        h_mid, sv = mixer_fwd(i, h)
        h_next, hn, h1 = _mlp_fwd(h_mid, norm_mlp_g[i].reshape(1, D), w1g, w2g, i)
        tape.append((h, sv, h_mid, hn, h1))
        h = h_next
    loss_row, dh, d_final = _loss_head(h, final_norm_g.reshape(1, D), target)
    loss = lax.psum(loss_row[0, 0], ("x", "y", "c"))

    dw1 = lax.empty((N_DEV, 4, D, D_FF // N_DEV), f32)
    dw2 = lax.empty((N_DEV, 4, D_FF // N_DEV, D), f32)
    d_mix, d_mlp, mg = [None] * 4, [None] * 4, [None] * 4
    for i in reversed(range(4)):
        h_in, sv, h_mid, hn, h1 = tape[i]
        dh, d_mlp[i], dw1, dw2 = _mlp_bwd(dh, h_mid, norm_mlp_g[i].reshape(1, D), hn, h1, w1g, w2g, i, dw1, dw2)
        dh, mg[i] = mixer_bwd(i, dh, h_in, sv)
        d_mix[i] = mg[i]["norm"]
    grad_x = dh.reshape(1, L, D)
    ga, gb_, s5g, m2g = mg[0], mg[3], mg[1], mg[2]

    full = {
        "mlp_w1": dw1, "mlp_w2": dw2,
        "gdn_w_in": jnp.stack([_cols_to_shards(g["w_ext"], _GDN_IN) for g in (ga, gb_)], axis=1),
        "gdn_w_out": jnp.stack([g["w_out"].reshape(N_DEV, D // N_DEV, D) for g in (ga, gb_)], axis=1),
        "s5_w_in": s5g["w_in"].reshape(N_DEV, 1, D // N_DEV, D), "s5_w_out": s5g["w_out"],
        "m2_w_in": _cols_to_shards(m2g["w_ext"], _M2_IN)[:, None],
        "m2_w_out": m2g["w_out"].reshape(N_DEV, 1, M2_INNER // N_DEV, D),
        "gdn_conv_w": jnp.concatenate([_cols_to_shards(g["conv_w"], 3 * D) for g in (ga, gb_)], axis=1),
        "m2_conv_w": _cols_to_shards(m2g["conv_w"], 2 * M2_INNER),
        "m2_conv_b": _cols_to_shards(m2g["conv_b"], 2 * M2_INNER),
        "m2_norm_g": _cols_to_shards(m2g["norm_g"], M2_INNER),
    }
    sends = [full[n].reshape((N_DEV,) + _as2d(W[n]).shape) for n in _SHARDED]
    parts = dict(zip(_SHARDED, _exchange("scatter_grads", sends, scatter=True)))

    rep = {
        "norm_mix_g": jnp.concatenate(d_mix, axis=0), "norm_mlp_g": jnp.concatenate(d_mlp, axis=0),
        "gdn_a_log": jnp.stack([ga["a_log"], gb_["a_log"]]), "gdn_dt_bias": jnp.stack([ga["dt_bias"], gb_["dt_bias"]]),
        "gdn_o_norm_g": jnp.stack([ga["o_norm_g"], gb_["o_norm_g"]]),
        "s5_lam_re": s5g["lam_re"], "s5_lam_im": s5g["lam_im"], "s5_log_dt": s5g["log_dt"], "s5_b_re": s5g["b_re"],
        "s5_b_im": s5g["b_im"], "s5_c_re": s5g["c_re"], "s5_c_im": s5g["c_im"], "s5_d": s5g["d"],
        "m2_dt_bias": m2g["dt_bias"], "m2_a_log": m2g["a_log"], "m2_d": m2g["d"], "final_norm_g": d_final,
    }

    def pack(d):
        flat = jnp.concatenate([d[n].reshape(-1).astype(f32) for n in _REPLICATED])
        return jnp.pad(flat, (0, -flat.shape[0] % (256 * 128))).reshape(-1, 128)

    (rep_parts,) = _exchange("gather_small_grads", [pack(rep)], scatter=False)

    res = {}
    for n in _SHARDED:
        w2d = _as2d(W[n])
        out = _adamw("adamw_" + n, parts[n], w2d, _as2d(MOM[n]), _as2d(VAR[n]))
        res[n] = [o.reshape(W[n].shape) for o in out]
    out = _adamw("adamw_replicated", rep_parts, pack(W), pack(MOM), pack(VAR))
    off = 0
    for n in _REPLICATED:
        size = W[n].size
        res[n] = [o.reshape(-1)[off:off + size].reshape(W[n].shape) for o in out]
        off += size

    return (loss, grad_x, *[res[n][0] for n in _WEIGHTS], *[res[n][1] for n in _WEIGHTS],
            *[res[n][2] for n in _WEIGHTS], *[res[n][3] for n in _WEIGHTS])
```

```python
import functools

import jax
import jax.numpy as jnp
from jax import lax
from jax.experimental import pallas as pl
from jax.experimental.pallas import tpu as pltpu

f32 = jnp.float32
bf16 = jnp.bfloat16
HI = lax.Precision.HIGHEST
MESH = pl.DeviceIdType.MESH

N_DEV = 8
D_MODEL = 1024
D_FF = 4096
CHUNK = 64
RMS_EPS = 1e-6
GDN_HEADS = 8
GDN_HB = 4
GDN_EXT = 4224
S5_STATE = 64
M2_INNER = 2048
M2_EXT = 6272
M2_HEADS = 32
M2_GB = 2
VMEM_LIMIT_BYTES = 56 * 1024 * 1024

ADAM_LR, ADAM_B1, ADAM_B2, ADAM_EPS, ADAM_WD, ADAM_STEP = 0.001, 0.9, 0.999, 1e-08, 0.01, 10

_NN = ((1,), (0,))
_NT = ((1,), (1,))
_TN = ((0,), (0,))


def _dot(a, b, dims=_NN):
    return lax.dot_general(a, b, (dims, ((), ())), precision=HI, preferred_element_type=f32)


def _dotb(a, b, dims=_NN):
    return lax.dot_general(a.astype(bf16), b.astype(bf16), (dims, ((), ())), preferred_element_type=f32)


def _iota(shape, dim):
    return lax.broadcasted_iota(jnp.int32, shape, dim)


def _params(n_grid):
    return pltpu.CompilerParams(dimension_semantics=("arbitrary",) * n_grid, vmem_limit_bytes=VMEM_LIMIT_BYTES)


def _row_tile(n_rows):
    return min(512, n_rows)


def _col_tile(n, cap=1024):
    best = 128
    for t in range(128, cap + 1, 128):
        if n % t == 0:
            best = t
    return best


def _mm(name, a, b, *, dims, grid, a_spec, b_spec, out_shape, out_spec, aux=(), a_fn=None, epi_fn=None,
        acc_shape, out_init=None):
    nk = grid[2]
    n_aux = len(aux)
    kinds = [x[2] for x in aux]

    def body(*refs):
        a_ref, b_ref = refs[0], refs[1]
        aux_refs = refs[2:2 + n_aux]
        pos = 2 + n_aux + (1 if out_init is not None else 0)
        o_ref, acc_ref = refs[pos], refs[pos + 1]
        k = pl.program_id(2)

        @pl.when(k == 0)
        def _():
            acc_ref[...] = jnp.zeros_like(acc_ref)

        av = a_ref[...]
        if a_fn is not None:
            av = a_fn(av, *[r[...] for r, kd in zip(aux_refs, kinds) if kd == "a"])
        acc_ref[...] += lax.dot_general(av.astype(bf16), b_ref[...].astype(bf16), (dims, ((), ())),
                                        preferred_element_type=f32)

        @pl.when(k == nk - 1)
        def _():
            r = acc_ref[...]
            if epi_fn is not None:
                r = epi_fn(r, *[x[...] for x, kd in zip(aux_refs, kinds) if kd == "e"])
            o_ref[...] = r.astype(o_ref.dtype)

    in_specs = [a_spec, b_spec] + [x[1] for x in aux]
    args = [a, b] + [x[0] for x in aux]
    aliases = {}
    if out_init is not None:
        in_specs.append(pl.BlockSpec(memory_space=pl.ANY))
        args.append(out_init)
        aliases = {len(args) - 1: 0}
    return pl.pallas_call(
        body, name=name, grid=grid, in_specs=in_specs, out_specs=out_spec, out_shape=out_shape,
        scratch_shapes=[pltpu.VMEM(acc_shape, f32)], input_output_aliases=aliases,
        compiler_params=_params(3),
    )(*args)


def _ew(name, f, ins, outs, grid):
    n_in = len(ins)
    modes = [o[4] for o in outs]

    def body(*refs):
        vals = [r[...] for r in refs[:n_in]]
        res = f(*vals)
        if not isinstance(res, (tuple, list)):
            res = (res,)
        for r, o_ref, mode in zip(res, refs[n_in:], modes):
            if mode is None:
                o_ref[...] = r.astype(o_ref.dtype)
                continue
            first = pl.program_id(1) == 0
            if mode == "all":
                first = jnp.logical_and(first, pl.program_id(0) == 0)

            @pl.when(first)
            def _(r=r, o_ref=o_ref):
                o_ref[...] = r.astype(o_ref.dtype)

            @pl.when(jnp.logical_not(first))
            def _(r=r, o_ref=o_ref):
                o_ref[...] += r.astype(o_ref.dtype)

    res = pl.pallas_call(
        body, name=name, grid=grid,
        in_specs=[pl.BlockSpec(blk, im) for _, blk, im in ins],
        out_specs=[pl.BlockSpec(o[2], o[3]) for o in outs],
        out_shape=[jax.ShapeDtypeStruct(o[0], o[1]) for o in outs],
        compiler_params=_params(2),
    )(*[a for a, _, _ in ins])
    return res


def _vjp_fn(f, n_primal):
    def g(*args):
        _, vjp = jax.vjp(f, *args[:n_primal])
        cts = args[n_primal:]
        return vjp(cts[0] if len(cts) == 1 else tuple(cts))
    return g


def _scan_fwd(name, step, n_state, state_shape, cins, ins, outs, n_units, n_chunks):
    n_c, n_in, n_out = len(cins), len(ins), len(outs)

    def body(*refs):
        c_refs = refs[:n_c]
        in_refs = refs[n_c:n_c + n_in]
        out_refs = refs[n_c + n_in:n_c + n_in + n_out]
        saved = refs[n_c + n_in + n_out:n_c + n_in + n_out + n_state]
        st = refs[n_c + n_in + n_out + n_state:]

        @pl.when(pl.program_id(1) == 0)
        def _():
            for s in st:
                s[...] = jnp.zeros_like(s)

        cur = [s[...] for s in st]
        for sv, s in zip(saved, cur):
            sv[...] = s
        new, res = step(cur, [r[...] for r in c_refs], [r[...] for r in in_refs])
        for s, n in zip(st, new):
            s[...] = n
        for o, r in zip(out_refs, res):
            o[...] = r

    sshape = (n_units, n_chunks) + state_shape
    sblock = (None, None) + state_shape
    nz = len(state_shape)
    res = pl.pallas_call(
        body, name=name, grid=(n_units, n_chunks),
        in_specs=[pl.BlockSpec(e[1], e[2]) for e in cins + ins],
        out_specs=[pl.BlockSpec(o[1], o[2]) for o in outs]
        + [pl.BlockSpec(sblock, lambda u, c: (u, c) + (0,) * nz)] * n_state,
        out_shape=[jax.ShapeDtypeStruct(o[0], f32) for o in outs]
        + [jax.ShapeDtypeStruct(sshape, f32)] * n_state,
        scratch_shapes=[pltpu.VMEM(state_shape, f32)] * n_state,
        compiler_params=_params(2),
    )(*[e[0] for e in cins + ins])
    return res[:n_out], res[n_out:]


def _scan_bwd(name, step, n_state, state_shape, cins, ins, saved, douts, n_units, n_chunks):
    n_c, n_in, n_do = len(cins), len(ins), len(douts)

    def flip(im):
        return lambda u, c: im(u, n_chunks - 1 - c)

    def body(*refs):
        p = 0
        c_refs = refs[p:p + n_c]; p += n_c
        in_refs = refs[p:p + n_in]; p += n_in
        sv_refs = refs[p:p + n_state]; p += n_state
        do_refs = refs[p:p + n_do]; p += n_do
        dc_refs = refs[p:p + n_c]; p += n_c
        di_refs = refs[p:p + n_in]; p += n_in
        dst = refs[p:]
        first = pl.program_id(1) == 0

        @pl.when(first)
        def _():
            for s in dst:
                s[...] = jnp.zeros_like(s)

        def fn(states, consts, vals):
            new, res = step(states, consts, vals)
            return tuple(new), tuple(res)

        prim = ([r[...] for r in sv_refs], [r[...] for r in c_refs], [r[...] for r in in_refs])
        _, vjp = jax.vjp(fn, *prim)
        d_states, d_consts, d_vals = vjp((tuple(s[...] for s in dst), tuple(r[...] for r in do_refs)))
        for s, g in zip(dst, d_states):
            s[...] = g
        for o, g in zip(di_refs, d_vals):
            o[...] = g
        for o, g in zip(dc_refs, d_consts):
            @pl.when(first)
            def _(o=o, g=g):
                o[...] = g

            @pl.when(jnp.logical_not(first))
            def _(o=o, g=g):
                o[...] += g

    nz = len(state_shape)
    sblock = (None, None) + state_shape
    def gshape(e):
        return e[3] if len(e) == 5 else e[0].shape

    def gmap(e):
        return e[4] if len(e) == 5 else e[2]

    in_specs = ([pl.BlockSpec(e[1], e[2]) for e in cins]
                + [pl.BlockSpec(e[1], flip(e[2])) for e in ins]
                + [pl.BlockSpec(sblock, lambda u, c: (u, n_chunks - 1 - c) + (0,) * nz)] * n_state
                + [pl.BlockSpec(e[1], flip(e[2])) for e in douts])
    out_specs = ([pl.BlockSpec(e[1], e[2]) for e in cins]
                 + [pl.BlockSpec(e[1], flip(gmap(e))) for e in ins])
    out_shape = [jax.ShapeDtypeStruct(gshape(e), f32) for e in cins + ins]
    res = pl.pallas_call(
        body, name=name, grid=(n_units, n_chunks), in_specs=in_specs, out_specs=out_specs, out_shape=out_shape,
        scratch_shapes=[pltpu.VMEM(state_shape, f32)] * n_state,
        compiler_params=_params(2),
    )(*([e[0] for e in cins + ins] + list(saved) + [e[0] for e in douts]))
    return res[:n_c], res[n_c:]


def _rms(x, g):
    return x * lax.rsqrt(jnp.mean(x * x, axis=-1, keepdims=True) + RMS_EPS) * g


def _rows(tm, width):
    return (tm, width), lambda r, z: (r, 0)


def _const(shape):
    return shape, lambda r, z: (0,) * len(shape)


def _rms_fwd(name, h, g):
    L, D = h.shape
    tm = _row_tile(L)
    return _ew(name, _rms, [(h, *_rows(tm, D)), (g, *_const((1, D)))],
               [((L, D), f32, *_rows(tm, D), None)], (L // tm, 1))[0]


def _rms_bwd(name, h, g, d_hn, d_res):
    L, D = h.shape
    tm = _row_tile(L)

    def f(hv, gv, dv, rv):
        dh, dg = _vjp_fn(_rms, 2)(hv, gv, dv)
        return dh + rv, dg

    return _ew(name, f, [(h, *_rows(tm, D)), (g, *_const((1, D))), (d_hn, *_rows(tm, D)), (d_res, *_rows(tm, D))],
               [((L, D), f32, *_rows(tm, D), None), ((1, D), f32, *_const((1, D)), "all")], (L // tm, 1))


def _loss_head(h, g, target):
    L, D = h.shape
    tm = _row_tile(L)

    def f(hv, gv, tv):
        def lf(a, b):
            e = jnp.square(_rms(a, b) - tv)
            return (0.5 / D) * jnp.sum(jnp.sum(e, axis=1, keepdims=True), axis=0, keepdims=True)

        val, vjp = jax.vjp(lf, hv, gv)
        dh, dg = vjp(jnp.ones((1, 1), f32))
        return jnp.broadcast_to(val, (1, 128)), dh, dg

    return _ew("loss_head", f, [(h, *_rows(tm, D)), (g, *_const((1, D))), (target, *_rows(tm, D))],
               [((1, 128), f32, *_const((1, 128)), "all"), ((L, D), f32, *_rows(tm, D), None),
                ((1, D), f32, *_const((1, D)), "all")], (L // tm, 1))


def _sqrelu(x):
    return jnp.square(jnp.maximum(x, 0.0))


def _mm_plain(name, a, b, dims, *, a_fn=None, epi_fn=None, aux=()):
    if dims == _NN:
        (M, K), N = a.shape, b.shape[1]
    elif dims == _NT:
        (M, K), N = a.shape, b.shape[0]
    else:
        (K, M), N = a.shape, b.shape[1]
    tm = min(512, M)
    tn = _col_tile(N)
    tk = _col_tile(K) if K % 128 == 0 else K
    tk = min(tk, 1024)
    if dims == _TN:
        tk = min(512, K)
        a_spec = pl.BlockSpec((tk, tm), lambda i, j, k: (k, i))
        b_spec = pl.BlockSpec((tk, tn), lambda i, j, k: (k, j))
        a_aux = pl.BlockSpec((tk, tm), lambda i, j, k: (k, i))
    elif dims == _NT:
        a_spec = pl.BlockSpec((tm, tk), lambda i, j, k: (i, k))
        b_spec = pl.BlockSpec((tn, tk), lambda i, j, k: (j, k))
        a_aux = pl.BlockSpec((tm, tk), lambda i, j, k: (i, k))
    else:
        a_spec = pl.BlockSpec((tm, tk), lambda i, j, k: (i, k))
        b_spec = pl.BlockSpec((tk, tn), lambda i, j, k: (k, j))
        a_aux = pl.BlockSpec((tm, tk), lambda i, j, k: (i, k))
    e_aux = pl.BlockSpec((tm, tn), lambda i, j, k: (i, j))
    aux_full = [(x, a_aux if kd == "a" else e_aux, kd) for x, kd in aux]
    return _mm(name, a, b, dims=dims, grid=(M // tm, N // tn, K // tk), a_spec=a_spec, b_spec=b_spec,
               out_shape=jax.ShapeDtypeStruct((M, N), f32), out_spec=pl.BlockSpec((tm, tn), lambda i, j, k: (i, j)),
               aux=aux_full, a_fn=a_fn, epi_fn=epi_fn, acc_shape=(tm, tn))


def _mlp_fwd(h, g, w1g, w2g, layer):
    L, D = h.shape
    tm = _row_tile(L)
    fs = D_FF // N_DEV
    hn = _rms_fwd("mlp_norm", h, g)
    h1 = _mm("mlp_up", hn, w1g, dims=_NN, grid=(L // tm, N_DEV, 1),
             a_spec=pl.BlockSpec((tm, D), lambda i, j, k: (i, 0)),
             b_spec=pl.BlockSpec((None, None, D, fs), lambda i, j, k: (j, layer, 0, 0)),
             out_shape=jax.ShapeDtypeStruct((L, D_FF), f32), out_spec=pl.BlockSpec((tm, fs), lambda i, j, k: (i, j)),
             acc_shape=(tm, fs))
    tn = 512
    h_out = _mm("mlp_down", h1, w2g, dims=_NN, grid=(L // tm, D // tn, N_DEV),
                a_spec=pl.BlockSpec((tm, fs), lambda i, j, k: (i, k)),
                b_spec=pl.BlockSpec((None, None, fs, tn), lambda i, j, k: (k, layer, 0, j)),
                out_shape=jax.ShapeDtypeStruct((L, D), f32), out_spec=pl.BlockSpec((tm, tn), lambda i, j, k: (i, j)),
                aux=[(h, pl.BlockSpec((tm, tn), lambda i, j, k: (i, j)), "e")],
                a_fn=_sqrelu, epi_fn=lambda acc, res: acc + res, acc_shape=(tm, tn))
    return h_out, hn, h1


def _mlp_bwd(dh, h, g, hn, h1, w1g, w2g, layer, dw1_buf, dw2_buf):
    L, D = h.shape
    tm = _row_tile(L)
    fs = D_FF // N_DEV
    tk = min(512, L)
    dh1 = _mm("mlp_down_dx", dh, w2g, dims=_NT, grid=(L // tm, N_DEV, 1),
              a_spec=pl.BlockSpec((tm, D), lambda i, j, k: (i, 0)),
              b_spec=pl.BlockSpec((None, None, fs, D), lambda i, j, k: (j, layer, 0, 0)),
              out_shape=jax.ShapeDtypeStruct((L, D_FF), f32), out_spec=pl.BlockSpec((tm, fs), lambda i, j, k: (i, j)),
              aux=[(h1, pl.BlockSpec((tm, fs), lambda i, j, k: (i, j)), "e")],
              epi_fn=lambda acc, pre: acc * (2.0 * jnp.maximum(pre, 0.0)), acc_shape=(tm, fs))
    dw2_buf = _mm("mlp_down_dw", h1, dh, dims=_TN, grid=(N_DEV, 1, L // tk),
                  a_spec=pl.BlockSpec((tk, fs), lambda i, j, k: (k, i)),
                  b_spec=pl.BlockSpec((tk, D), lambda i, j, k: (k, 0)),
                  out_shape=jax.ShapeDtypeStruct(dw2_buf.shape, f32),
                  out_spec=pl.BlockSpec((None, None, fs, D), lambda i, j, k: (i, layer, 0, 0)),
                  a_fn=_sqrelu, acc_shape=(fs, D), out_init=dw2_buf)
    tr = 512
    dw1_buf = _mm("mlp_up_dw", hn, dh1, dims=_TN, grid=(D // tr, N_DEV, L // tk),
                  a_spec=pl.BlockSpec((tk, tr), lambda i, j, k: (k, i)),
                  b_spec=pl.BlockSpec((tk, fs), lambda i, j, k: (k, j)),
                  out_shape=jax.ShapeDtypeStruct(dw1_buf.shape, f32),
                  out_spec=pl.BlockSpec((None, None, tr, fs), lambda i, j, k: (j, layer, i, 0)),
                  acc_shape=(tr, fs), out_init=dw1_buf)
    tn = 512
    dhn = _mm("mlp_up_dx", dh1, w1g, dims=_NT, grid=(L // tm, D // tn, N_DEV),
              a_spec=pl.BlockSpec((tm, fs), lambda i, j, k: (i, k)),
              b_spec=pl.BlockSpec((None, None, tn, fs), lambda i, j, k: (k, layer, j, 0)),
              out_shape=jax.ShapeDtypeStruct((L, D), f32), out_spec=pl.BlockSpec((tm, tn), lambda i, j, k: (i, j)),
              acc_shape=(tm, tn))
    dh_in, dg = _rms_bwd("mlp_norm_bwd", h, g, dhn, dh)
    return dh_in, dg, dw1_buf, dw2_buf


def _shift_dn(x, s, row):
    return x if s == 0 else jnp.where(row >= s, pltpu.roll(x, s, 0), 0.0)


def _shift_up(x, s, row):
    n = x.shape[0]
    return x if s == 0 else jnp.where(row < n - s, pltpu.roll(x, n - s, 0), 0.0)


def _conv_pre(x, w, b, row):
    c = jnp.broadcast_to(b, x.shape)
    for j in range(4):
        c = c + w[j:j + 1, :] * _shift_dn(x, 3 - j, row)
    return c


def _conv_fwd(name, x_arr, blk_off, w, b):
    L = x_arr.shape[0]
    C = w.shape[1]

    def f(x, wv, bv):
        c = _conv_pre(x, wv, bv, _iota(x.shape, 0))
        return c * jax.nn.sigmoid(c)

    return _ew(name, f, [(x_arr, (L, 128), lambda j, z: (0, blk_off + j)), (w, (4, 128), lambda j, z: (0, j)),
                         (b, (1, 128), lambda j, z: (0, j))],
               [((L, C), f32, (L, 128), lambda j, z: (0, j), None)], (C // 128, 1))[0]


def _conv_bwd(name, x_arr, blk_off, w, b, dy):
    L = x_arr.shape[0]
    C = w.shape[1]

    def f(x, wv, bv, g):
        row = _iota(x.shape, 0)
        c = _conv_pre(x, wv, bv, row)
        s = jax.nn.sigmoid(c)
        dc = g * (s * (1.0 + c * (1.0 - s)))
        dx = jnp.zeros_like(x)
        dw = jnp.zeros((4, 128), f32)
        r4 = _iota((4, 128), 0)
        for j in range(4):
            dx = dx + wv[j:j + 1, :] * _shift_up(dc, 3 - j, row)
            dwj = jnp.sum(dc * _shift_dn(x, 3 - j, row), axis=0, keepdims=True)
            dw = dw + jnp.where(r4 == j, jnp.broadcast_to(dwj, (4, 128)), 0.0)
        return dx, dw, jnp.sum(dc, axis=0, keepdims=True)

    return _ew(name, f, [(x_arr, (L, 128), lambda j, z: (0, blk_off + j)), (w, (4, 128), lambda j, z: (0, j)),
                         (b, (1, 128), lambda j, z: (0, j)), (dy, (L, 128), lambda j, z: (0, j))],
               [((L, C), f32, (L, 128), lambda j, z: (0, j), None), ((4, C), f32, (4, 128), lambda j, z: (0, j), None),
                ((1, C), f32, (1, 128), lambda j, z: (0, j), None)], (C // 128, 1))


def _l2norm(t):
    return t * lax.rsqrt(jnp.sum(t * t, axis=-1, keepdims=True) + 1e-6)


def _gdn_act(cq, ck, ab, alog, dtb):
    h = pl.program_id(1)
    qn = _l2norm(cq) * (128.0 ** -0.5)
    kn = _l2norm(ck)
    lane = _iota(ab.shape, 1)
    a_raw = jnp.sum(jnp.where(lane == h, ab, 0.0), axis=1, keepdims=True)
    b_raw = jnp.sum(jnp.where(lane == h + GDN_HEADS, ab, 0.0), axis=1, keepdims=True)
    lane1 = _iota(alog.shape, 1)
    al = jnp.sum(jnp.where(lane1 == h, alog, 0.0), axis=1, keepdims=True)
    db = jnp.sum(jnp.where(lane1 == h, dtb, 0.0), axis=1, keepdims=True)
    g = -jnp.exp(al) * jax.nn.softplus(a_raw + db)
    beta = jax.nn.sigmoid(b_raw)
    return qn, kn, jnp.broadcast_to(g, cq.shape), jnp.broadcast_to(beta, cq.shape)


def _each(f, *lists):
    return [f(*a) for a in zip(*lists)]


def _gdn_chunk(states, consts, vals):
    S = list(states)
    cut = [slice(128 * i, 128 * i + 128) for i in range(len(S))]
    q, k, v, gb, bb = ([t[:, c] for c in cut] for t in vals)
    C = vals[0].shape[0]
    row, col = _iota((C, C), 0), _iota((C, C), 1)
    causal, strict = row >= col, row > col
    ltri = causal.astype(f32)
    eye = (row == col).astype(f32)
    e0 = (_iota((C, 128), 1) == 0).astype(f32)
    last = _iota((C, 1), 0) == C - 1
    Gb = _each(lambda g: _dot(ltri, g), gb)
    Gc = _each(lambda g: jnp.mean(g, axis=1, keepdims=True), Gb)
    Gr = _each(lambda g: _dot(e0, g, _NT), Gb)
    bc = _each(lambda b: jnp.mean(b, axis=1, keepdims=True), bb)
    decay = _each(lambda gc, gr: jnp.where(causal, jnp.exp(jnp.where(causal, gc - gr, 0.0)), 0.0), Gc, Gr)
    kk = _each(lambda a: _dotb(a, a, _NT), k)
    A = _each(lambda b, x, d: jnp.where(strict, b * x * d, 0.0), bc, kk, decay)
    M = _each(lambda a: eye - a, A)
    P = _each(lambda a: _dot(a, a), A)
    for it in range(5):
        M = _each(lambda m, p: m + _dot(m, p), M, P)
        if it < 4:
            P = _each(lambda p: _dot(p, p), P)
    eG = _each(jnp.exp, Gc)
    u = _each(lambda m, x, b: _dotb(m, x * b), M, v, bc)
    w = _each(lambda m, x, b, e: _dotb(m, x * (b * e)), M, k, bc, eG)
    qk = _each(lambda a, b, d: _dotb(a, b, _NT) * d, q, k, decay)
    g_last = _each(lambda gc: jnp.sum(jnp.where(last, gc, 0.0), axis=0, keepdims=True), Gc)
    v_new = _each(lambda a, b, s: a - _dotb(b, s), u, w, S)
    o = _each(lambda a, e, s, b, x: _dotb(a * e, s) + _dotb(b, x), q, eG, S, qk, v_new)
    S_new = _each(lambda gl, s, a, gc, x: jnp.exp(gl) * s + _dotb(a * jnp.exp(gl - gc), x, _TN), g_last, S, k, Gc, v_new)
    return S_new, [jnp.concatenate(o, axis=1)]


def _gdn_post(o, gate, g):
    return _rms(o, g) * (gate * jax.nn.sigmoid(gate))


def _pad_row(v):
    return jnp.pad(v.astype(f32), (0, 128 - v.shape[0])).reshape(1, 128)


def _gdn_fwd(h, g_norm, w_ext, conv_w, a_log, dt_bias, o_norm_g, w_out):
    L, D = h.shape
    tm = _row_tile(L)
    nc = L // CHUNK
    H = GDN_HEADS
    hn = _rms_fwd("mix_norm", h, g_norm)
    proj = _mm_plain("gdn_in", hn, w_ext, _NN)
    zb = jnp.zeros((1, 3 * D), f32)
    cq = _conv_fwd("gdn_conv", proj, 0, conv_w, zb)
    alog, dtb = _pad_row(a_log), _pad_row(dt_bias)
    act_ins = [(cq, (tm, 128), lambda r, hh: (r, hh)), (cq, (tm, 128), lambda r, hh: (r, H + hh)),
               (proj, (tm, 128), lambda r, hh: (r, 4 * H)), (alog, (1, 128), lambda r, hh: (0, 0)),
               (dtb, (1, 128), lambda r, hh: (0, 0))]
    qn, kn, gb, bb = _ew("gdn_act", _gdn_act, act_ins,
                         [((L, D), f32, (tm, 128), lambda r, hh: (r, hh), None)] * 4, (L // tm, H))
    cblk = (CHUNK, 128 * GDN_HB)
    core_ins = [(qn, cblk, lambda u, c: (c, u)), (kn, cblk, lambda u, c: (c, u)),
                (cq, cblk, lambda u, c: (c, 2 * H // GDN_HB + u), (L, D), lambda u, c: (c, u)),
                (gb, cblk, lambda u, c: (c, u)), (bb, cblk, lambda u, c: (c, u))]
    (o,), saved_s = _scan_fwd("gdn_core", _gdn_chunk, GDN_HB, (128, 128), [], core_ins,
                              [((L, D), cblk, lambda u, c: (c, u))], H // GDN_HB, nc)
    on = o_norm_g.reshape(1, 128)
    post_ins = [(o, (tm, 128), lambda r, hh: (r, hh)), (proj, (tm, 128), lambda r, hh: (r, 3 * H + hh)),
                (on, (1, 128), lambda r, hh: (0, 0))]
    y = _ew("gdn_post", _gdn_post, post_ins, [((L, D), f32, (tm, 128), lambda r, hh: (r, hh), None)], (L // tm, H))[0]
    h_out = _mm_plain("gdn_out", y, w_out, _NN, epi_fn=lambda acc, res: acc + res, aux=[(h, "e")])
    saved = dict(hn=hn, proj=proj, cq=cq, alog=alog, dtb=dtb, act_ins=act_ins, core_ins=core_ins, saved_s=saved_s,
                 post_ins=post_ins, y=y, zb=zb)
    return h_out, saved


def _gdn_bwd(dh, h, g_norm, w_ext, conv_w, w_out, sv):
    L, D = h.shape
    tm = _row_tile(L)
    nc = L // CHUNK
    H = GDN_HEADS
    dy = _mm_plain("gdn_out_dx", dh, w_out, _NT)
    dw_out = _mm_plain("gdn_out_dw", sv["y"], dh, _TN)
    hd = ((L, D), f32, (tm, 128), lambda r, hh: (r, hh), None)
    d_o, d_gate, d_on = _ew("gdn_post_bwd", _vjp_fn(_gdn_post, 3),
                            sv["post_ins"] + [(dy, (tm, 128), lambda r, hh: (r, hh))],
                            [hd, hd, ((1, 128), f32, (1, 128), lambda r, hh: (0, 0), "all")], (L // tm, H))
    cblk = (CHUNK, 128 * GDN_HB)
    _, (dqn, dkn, dv, dgb, dbb) = _scan_bwd("gdn_core_bwd", _gdn_chunk, GDN_HB, (128, 128), [], sv["core_ins"],
                                            sv["saved_s"], [(d_o, cblk, lambda u, c: (c, u))], H // GDN_HB, nc)
    cts = [(t, (tm, 128), lambda r, hh: (r, hh)) for t in (dqn, dkn, dgb, dbb)]
    row128 = ((1, 128), f32, (1, 128), lambda r, hh: (0, 0), "all")
    d_cq, d_ck, d_ab, d_alog, d_dtb = _ew(
        "gdn_act_bwd", _vjp_fn(_gdn_act, 5), sv["act_ins"] + cts,
        [hd, hd, ((L, 128), f32, (tm, 128), lambda r, hh: (r, 0), "inner"), row128, row128], (L // tm, H))
    d_conv_out = jnp.concatenate([d_cq, d_ck, dv], axis=1)
    d_conv_in, d_conv_w, _ = _conv_bwd("gdn_conv_bwd", sv["proj"], 0, conv_w, sv["zb"], d_conv_out)
    d_proj = jnp.concatenate([d_conv_in, d_gate, d_ab], axis=1)
    dw_ext = _mm_plain("gdn_in_dw", sv["hn"], d_proj, _TN)
    dhn = _mm_plain("gdn_in_dx", d_proj, w_ext, _NT)
    dh_in, dg = _rms_bwd("mix_norm_bwd", h, g_norm, dhn, dh)
    grads = dict(norm=dg, w_ext=dw_ext, conv_w=d_conv_w, a_log=d_alog[0, :H], dt_bias=d_dtb[0, :H],
                 o_norm_g=d_on[0], w_out=dw_out)
    return dh_in, grads


def _expand_lanes(row, width, rep):
    sel = ((_iota((128, width), 1) // rep) == _iota((128, width), 0)).astype(f32)
    return jnp.mean(_dot(jnp.broadcast_to(row, (8, 128)), sel), axis=0, keepdims=True)


def _s5_params(lre, lim, ldt, wbr, wbi):
    dt = jnp.exp(_expand_lanes(ldt, 512, S5_STATE))
    mag = jnp.exp(lre * dt)
    ang = lim * dt
    abr, abi = mag * jnp.cos(ang), mag * jnp.sin(ang)
    nr = abr - 1.0
    den = lre * lre + lim * lim
    cr = (nr * lre + abi * lim) / den
    ci = (abi * lre - nr * lim) / den
    return abr, abi, cr * wbr - ci * wbi, cr * wbi + ci * wbr


def _s5_scan(name, xr, xi, ar, ai, rev, want_prev):
    L, W = xr.shape
    nb = L // 8
    n_out = 4 if want_prev else 2

    def body(xr_ref, xi_ref, ar_ref, ai_ref, *outs):
        a_r = ar_ref[...]
        a_i = -ai_ref[...] if rev else ai_ref[...]

        def cm(p, q):
            return p[0] * q[0] - p[1] * q[1], p[0] * q[1] + p[1] * q[0]

        a1 = (a_r, a_i)
        a2 = cm(a1, a1)
        a3 = cm(a2, a1)
        a4 = cm(a2, a2)
        pw = [a1, a2, a3, a4, cm(a4, a1), cm(a4, a2), cm(a4, a3), cm(a4, a4)]
        row = _iota((8, 128), 0)
        tab_r = jnp.zeros((8, 128), f32)
        tab_i = jnp.zeros((8, 128), f32)
        for t in range(8):
            idx = 7 - t if rev else t
            tab_r = jnp.where(row == idx, jnp.broadcast_to(pw[t][0], (8, 128)), tab_r)
            tab_i = jnp.where(row == idx, jnp.broadcast_to(pw[t][1], (8, 128)), tab_i)
        lv = [(d, jnp.broadcast_to(p[0], (8, 128)), jnp.broadcast_to(p[1], (8, 128))) for d, p in ((1, a1), (2, a2), (4, a4))]

        def step(i, carry):
            cr, ci = carry
            blk = nb - 1 - i if rev else i
            r0 = pl.multiple_of(blk * 8, 8)
            x_r = xr_ref[pl.ds(r0, 8), :]
            x_i = xi_ref[pl.ds(r0, 8), :]
            for d, p_r, p_i in lv:
                if rev:
                    s_r = jnp.where(row < 8 - d, pltpu.roll(x_r, 8 - d, 0), 0.0)
                    s_i = jnp.where(row < 8 - d, pltpu.roll(x_i, 8 - d, 0), 0.0)
                else:
                    s_r = jnp.where(row >= d, pltpu.roll(x_r, d, 0), 0.0)
                    s_i = jnp.where(row >= d, pltpu.roll(x_i, d, 0), 0.0)
                x_r, x_i = x_r + p_r * s_r - p_i * s_i, x_i + p_r * s_i + p_i * s_r
            x_r, x_i = x_r + tab_r * cr - tab_i * ci, x_i + tab_r * ci + tab_i * cr
            outs[0][pl.ds(r0, 8), :] = x_r
            outs[1][pl.ds(r0, 8), :] = x_i
            if want_prev:
                outs[2][pl.ds(r0, 8), :] = jnp.where(row >= 1, pltpu.roll(x_r, 1, 0), cr)
                outs[3][pl.ds(r0, 8), :] = jnp.where(row >= 1, pltpu.roll(x_i, 1, 0), ci)
            e = 0 if rev else 7
            return jnp.broadcast_to(x_r[e:e + 1, :], (8, 128)), jnp.broadcast_to(x_i[e:e + 1, :], (8, 128))

        lax.fori_loop(0, nb, step, (jnp.zeros((8, 128), f32), jnp.zeros((8, 128), f32)))

    col = pl.BlockSpec((L, 128), lambda q, z: (0, q))
    aspec = pl.BlockSpec((None, 1, 128), lambda q, z: (q // 4, 0, q % 4))
    return pl.pallas_call(
        body, name=name, grid=(W // 128, 1), in_specs=[col, col, aspec, aspec], out_specs=[col] * n_out,
        out_shape=[jax.ShapeDtypeStruct((L, W), f32)] * n_out, compiler_params=_params(2),
    )(xr, xi, ar, ai)


def _blockdiag(t, n_in, n_out):
    t4 = t.reshape(8, 8, n_in, n_out)
    return jnp.einsum("jaio,ab->jaibo", t4, jnp.eye(8, dtype=t.dtype)).reshape(8, 8 * n_in, 8 * n_out)


def _blockdiag_t(w, n_in, n_out):
    w5 = w.reshape(8, 8, n_in, 8, n_out)
    return jnp.einsum("jaibo,ab->jaio", w5, jnp.eye(8, dtype=w.dtype)).reshape(64, n_in, n_out)


def _glu(ag, h):
    n = ag.shape[1] // 2
    return h + ag[:, :n] * jax.nn.sigmoid(ag[:, n:])


def _s5_fwd(h, g_norm, w_in, lam_re, lam_im, log_dt, b_re, b_im, c_re, c_im, d_skip, w_out_g):
    L, D = h.shape
    tm = _row_tile(L)
    W = 8 * 512
    hn = _rms_fwd("mix_norm", h, g_norm)
    u = _mm_plain("s5_in", hn, w_in, _NN)
    lre, lim = lam_re.reshape(8, 1, 512), lam_im.reshape(8, 1, 512)
    ldt = jnp.pad(log_dt.reshape(8, 1, 8), ((0, 0), (0, 0), (0, 120)))
    wbr = _blockdiag(b_re.transpose(0, 2, 1), 16, 64)
    wbi = _blockdiag(b_im.transpose(0, 2, 1), 16, 64)
    wcr = _blockdiag(c_re.transpose(0, 2, 1), 64, 16)
    wci = _blockdiag(c_im.transpose(0, 2, 1), 64, 16)
    jb = lambda shape: (shape, lambda j, z: (j, 0, 0))
    par_ins = [(lre, *jb((None, 1, 512))), (lim, *jb((None, 1, 512))), (ldt, *jb((None, 1, 128))),
               (wbr, *jb((None, 128, 512))), (wbi, *jb((None, 128, 512)))]
    abr, abi, bbr, bbi = _ew("s5_params", _s5_params, par_ins,
                             [((8, 1, 512), f32, *jb((None, 1, 512)), None)] * 2
                             + [((8, 128, 512), f32, *jb((None, 128, 512)), None)] * 2, (8, 1))

    def bu(name, wb):
        return _mm(name, u, wb, dims=_NN, grid=(L // tm, 8, 1),
                   a_spec=pl.BlockSpec((tm, 128), lambda i, j, k: (i, j)),
                   b_spec=pl.BlockSpec((None, 128, 512), lambda i, j, k: (j, 0, 0)),
                   out_shape=jax.ShapeDtypeStruct((L, W), f32), out_spec=pl.BlockSpec((tm, 512), lambda i, j, k: (i, j)),
                   acc_shape=(tm, 512))

    bur, bui = bu("s5_bu", bbr), bu("s5_bu", bbi)
    sr, si, pr, pi = _s5_scan("s5_scan", bur, bui, abr, abi, False, True)
    d_row = d_skip.reshape(1, D)
    cspec = dict(a_spec=pl.BlockSpec((tm, 512), lambda i, j, k: (i, j)),
                 b_spec=pl.BlockSpec((None, 512, 128), lambda i, j, k: (j, 0, 0)),
                 out_shape=jax.ShapeDtypeStruct((L, D), f32), out_spec=pl.BlockSpec((tm, 128), lambda i, j, k: (i, j)),
                 acc_shape=(tm, 128))
    e128 = pl.BlockSpec((tm, 128), lambda i, j, k: (i, j))
    pre1 = _mm("s5_c_re", sr, wcr, dims=_NN, grid=(L // tm, 8, 1), **cspec)
    pre = _mm("s5_c_im", si, wci, dims=_NN, grid=(L // tm, 8, 1),
              aux=[(pre1, e128, "e"), (u, e128, "e"), (d_row, pl.BlockSpec((1, 128), lambda i, j, k: (0, j)), "e")],
              epi_fn=lambda acc, p1, uu, dd: p1 - acc + dd * uu, **cspec)
    ws = D // N_DEV * 2
    ag = _mm("s5_out", pre, w_out_g, dims=_NN, grid=(L // tm, N_DEV, 1),
             a_spec=pl.BlockSpec((tm, D), lambda i, j, k: (i, 0)),
             b_spec=pl.BlockSpec((None, None, D, ws), lambda i, j, k: (j, 0, 0, 0)),
             out_shape=jax.ShapeDtypeStruct((L, 2 * D), f32), out_spec=pl.BlockSpec((tm, ws), lambda i, j, k: (i, j)),
             a_fn=jax.nn.gelu, acc_shape=(tm, ws))
    h_out = _ew("s5_glu", _glu, [(ag, *_rows(tm, 2 * D)), (h, *_rows(tm, D))],
                [((L, D), f32, *_rows(tm, D), None)], (L // tm, 1))[0]
    saved = dict(hn=hn, u=u, par_ins=par_ins, abr=abr, abi=abi, bbr=bbr, bbi=bbi, wcr=wcr, wci=wci, sr=sr, si=si,
                 pr=pr, pi=pi, pre=pre, ag=ag, d_row=d_row)
    return h_out, saved


def _s5_bwd(dh, h, g_norm, w_in, w_out_g, sv):
    L, D = h.shape
    tm = _row_tile(L)
    tk = min(512, L)
    W = 8 * 512
    ws = D // N_DEV * 2
    u, pre, d_row = sv["u"], sv["pre"], sv["d_row"]
    d_ag = _ew("s5_glu_bwd", lambda ag, hv, g: _vjp_fn(_glu, 2)(ag, hv, g)[0],
               [(sv["ag"], *_rows(tm, 2 * D)), (h, *_rows(tm, D)), (dh, *_rows(tm, D))],
               [((L, 2 * D), f32, *_rows(tm, 2 * D), None)], (L // tm, 1))[0]
    tr = 512
    dw_out = _mm("s5_out_dw", pre, d_ag, dims=_TN, grid=(D // tr, N_DEV, L // tk),
                 a_spec=pl.BlockSpec((tk, tr), lambda i, j, k: (k, i)),
                 b_spec=pl.BlockSpec((tk, ws), lambda i, j, k: (k, j)),
                 out_shape=jax.ShapeDtypeStruct((N_DEV, 1, D, ws), f32),
                 out_spec=pl.BlockSpec((None, None, tr, ws), lambda i, j, k: (j, 0, i, 0)),
                 a_fn=jax.nn.gelu, acc_shape=(tr, ws))
    tn = 512
    dpre = _mm("s5_out_dx", d_ag, w_out_g, dims=_NT, grid=(L // tm, D // tn, N_DEV),
               a_spec=pl.BlockSpec((tm, ws), lambda i, j, k: (i, k)),
               b_spec=pl.BlockSpec((None, None, tn, ws), lambda i, j, k: (k, 0, j, 0)),
               out_shape=jax.ShapeDtypeStruct((L, D), f32), out_spec=pl.BlockSpec((tm, tn), lambda i, j, k: (i, j)),
               aux=[(pre, pl.BlockSpec((tm, tn), lambda i, j, k: (i, j)), "e")],
               epi_fn=lambda acc, p: _vjp_fn(jax.nn.gelu, 1)(p, acc)[0], acc_shape=(tm, tn))
    d_d = _ew("s5_dskip", lambda a, b: jnp.sum(a * b, axis=0, keepdims=True),
              [(dpre, *_rows(tm, D)), (u, *_rows(tm, D))], [((1, D), f32, *_const((1, D)), "all")], (L // tm, 1))[0]
    neg = lambda acc: -acc
    dsspec = dict(dims=_NT, grid=(L // tm, 8, 1), a_spec=pl.BlockSpec((tm, 128), lambda i, j, k: (i, j)),
                  b_spec=pl.BlockSpec((None, 512, 128), lambda i, j, k: (j, 0, 0)),
                  out_shape=jax.ShapeDtypeStruct((L, W), f32), out_spec=pl.BlockSpec((tm, 512), lambda i, j, k: (i, j)),
                  acc_shape=(tm, 512))
    dsr = _mm("s5_c_re_dx", dpre, sv["wcr"], **dsspec)
    dsi = _mm("s5_c_im_dx", dpre, sv["wci"], epi_fn=neg, **dsspec)
    dwspec = dict(dims=_TN, grid=(8, 1, L // tk), a_spec=pl.BlockSpec((tk, 512), lambda i, j, k: (k, i)),
                  b_spec=pl.BlockSpec((tk, 128), lambda i, j, k: (k, i)),
                  out_shape=jax.ShapeDtypeStruct((8, 512, 128), f32),
                  out_spec=pl.BlockSpec((None, 512, 128), lambda i, j, k: (i, 0, 0)), acc_shape=(512, 128))
    dwcr = _mm("s5_c_re_dw", sv["sr"], dpre, **dwspec)
    dwci = _mm("s5_c_im_dw", sv["si"], dpre, epi_fn=neg, **dwspec)
    lr, li = _s5_scan("s5_scan_bwd", dsr, dsi, sv["abr"], sv["abi"], True, False)

    def da(lrv, liv, prv, piv):
        return (jnp.sum(lrv * prv + liv * piv, axis=0, keepdims=True),
                jnp.sum(liv * prv - lrv * piv, axis=0, keepdims=True))

    sblk = ((tm, 512), lambda j, r: (r, j))
    dabr, dabi = _ew("s5_dlam", da, [(lr, *sblk), (li, *sblk), (sv["pr"], *sblk), (sv["pi"], *sblk)],
                     [((8, 1, 512), f32, (None, 1, 512), lambda j, r: (j, 0, 0), "inner")] * 2, (8, L // tm))
    dbspec = dict(dims=_TN, grid=(8, 1, L // tk), a_spec=pl.BlockSpec((tk, 128), lambda i, j, k: (k, i)),
                  b_spec=pl.BlockSpec((tk, 512), lambda i, j, k: (k, i)),
                  out_shape=jax.ShapeDtypeStruct((8, 128, 512), f32),
                  out_spec=pl.BlockSpec((None, 128, 512), lambda i, j, k: (i, 0, 0)), acc_shape=(128, 512))
    dbbr = _mm("s5_bu_dw", u, lr, **dbspec)
    dbbi = _mm("s5_bu_dw", u, li, **dbspec)
    duspec = dict(dims=_NT, grid=(L // tm, 8, 1), a_spec=pl.BlockSpec((tm, 512), lambda i, j, k: (i, j)),
                  b_spec=pl.BlockSpec((None, 128, 512), lambda i, j, k: (j, 0, 0)),
                  out_shape=jax.ShapeDtypeStruct((L, D), f32), out_spec=pl.BlockSpec((tm, 128), lambda i, j, k: (i, j)),
                  acc_shape=(tm, 128))
    e128 = pl.BlockSpec((tm, 128), lambda i, j, k: (i, j))
    du1 = _mm("s5_bu_dx_re", lr, sv["bbr"], **duspec)
    du = _mm("s5_bu_dx_im", li, sv["bbi"],
             aux=[(du1, e128, "e"), (dpre, e128, "e"), (d_row, pl.BlockSpec((1, 128), lambda i, j, k: (0, j)), "e")],
             epi_fn=lambda acc, d1, dp, dd: acc + d1 + dp * dd, **duspec)
    jb = lambda shape: (shape, lambda j, z: (j, 0, 0))
    cts = [(dabr, *jb((None, 1, 512))), (dabi, *jb((None, 1, 512))), (dbbr, *jb((None, 128, 512))),
           (dbbi, *jb((None, 128, 512)))]
    dlre, dlim, dldt, dwbr, dwbi = _ew(
        "s5_params_bwd", _vjp_fn(_s5_params, 5), sv["par_ins"] + cts,
        [((8, 1, 512), f32, *jb((None, 1, 512)), None)] * 2 + [((8, 1, 128), f32, *jb((None, 1, 128)), None)]
        + [((8, 128, 512), f32, *jb((None, 128, 512)), None)] * 2, (8, 1))
    dw_in = _mm_plain("s5_in_dw", sv["hn"], du, _TN)
    dhn = _mm_plain("s5_in_dx", du, w_in, _NT)
    dh_in, dg = _rms_bwd("mix_norm_bwd", h, g_norm, dhn, dh)
    grads = dict(norm=dg, w_in=dw_in, lam_re=dlre.reshape(64, 64), lam_im=dlim.reshape(64, 64),
                 log_dt=dldt[:, 0, :8].reshape(64),
                 b_re=_blockdiag_t(dwbr, 16, 64).transpose(0, 2, 1), b_im=_blockdiag_t(dwbi, 16, 64).transpose(0, 2, 1),
                 c_re=_blockdiag_t(dwcr, 64, 16).transpose(0, 2, 1), c_im=_blockdiag_t(dwci, 64, 16).transpose(0, 2, 1),
                 d=d_d[0], w_out=dw_out)
    return dh_in, grads


def _m2_act(dt_raw, dtbias, alog):
    dt = jax.nn.softplus(dt_raw + dtbias)
    da = dt * (-jnp.exp(alog))
    sel = ((_iota((128, M2_INNER), 1) // 64) == _iota((128, M2_INNER), 0)).astype(f32)
    return _dot(dt, sel), _dot(da, sel)


def _m2_dexp(d):
    return _expand_lanes(d, M2_INNER, 64)


def _ssd_chunk(states, consts, vals):
    (dsk,) = consts
    S = list(states)
    n = len(S)
    cut = [slice(128 * i, 128 * i + 128) for i in range(n)]
    x, dtb, dab = ([t[:, c] for c in cut] for t in vals[:3])
    dsk = [dsk[:, c] for c in cut]
    B = [vals[3][:, cut[i // 2]] for i in range(n)]
    Cm = [vals[4][:, cut[i // 2]] for i in range(n)]
    C = vals[0].shape[0]
    row, col = _iota((C, C), 0), _iota((C, C), 1)
    causal = row >= col
    ltri = causal.astype(f32)
    lane = _iota((C, 128), 1)
    last = _iota((C, 128), 0) == C - 1
    eye128 = _iota((128, 128), 0) == _iota((128, 128), 1)
    head = [jnp.logical_and(lane >= 64 * hh, lane < 64 * hh + 64) for hh in range(2)]
    pick = [(lane == 64 * hh).astype(f32) for hh in range(2)]
    xdt = _each(lambda a, b: a * b, x, dtb)
    cb = _each(lambda c, b: _dotb(c, b, _NT), Cm[::2], B[::2])
    cum = _each(lambda a: _dot(ltri, a), dab)
    clast = _each(lambda a: jnp.sum(jnp.where(last, a, 0.0), axis=0, keepdims=True), cum)
    st = _each(lambda a, cl, cu, b: _dotb(a * jnp.exp(cl - cu), b, _TN), xdt, clast, cum, B)
    y = _each(lambda c, s, cu: _dotb(c, s, _NT) * jnp.exp(cu), Cm, S, cum)
    for hh in range(2):
        ccol = _each(lambda cu: jnp.sum(jnp.where(head[hh], cu, 0.0), axis=1, keepdims=True) * (1.0 / 64), cum)
        crow = _each(lambda cu: _dot(pick[hh], cu, _NT), cum)
        lm = _each(lambda a, b: jnp.where(causal, jnp.exp(jnp.where(causal, a - b, 0.0)), 0.0), ccol, crow)
        y = [y[i] + _dotb(cb[i // 2] * lm[i], jnp.where(head[hh], xdt[i], 0.0)) for i in range(n)]
    cdcol = _each(lambda cl: jnp.sum(jnp.where(eye128, jnp.broadcast_to(jnp.exp(cl), (128, 128)), 0.0),
                                     axis=1, keepdims=True), clast)
    S_new = _each(lambda c, s, t: c * s + t, cdcol, S, st)
    out = _each(lambda a, d, b: a + d * b, y, dsk, x)
    return S_new, [jnp.concatenate(out, axis=1)]


def _m2_post(yc, z, ng):
    return _rms(yc * (z * jax.nn.sigmoid(z)), ng)


def _m2_fwd(h, g_norm, w_ext, conv_w, conv_b, dt_bias, a_log, d_skip, norm_g, w_out):
    L, D = h.shape
    tm = _row_tile(L)
    nc = L // CHUNK
    NI = M2_INNER
    hn = _rms_fwd("mix_norm", h, g_norm)
    proj = _mm_plain("m2_in", hn, w_ext, _NN)
    xbc = _conv_fwd("m2_conv", proj, NI // 128, conv_w, conv_b)
    dtb_row, alog_row, d_pad = _pad_row(dt_bias), _pad_row(a_log), _pad_row(d_skip)
    act_ins = [(proj, (tm, 128), lambda r, z: (r, 3 * NI // 128)), (dtb_row, *_const((1, 128))),
               (alog_row, *_const((1, 128)))]
    dtb, dab = _ew("m2_act", _m2_act, act_ins, [((L, NI), f32, *_rows(tm, NI), None)] * 2, (L // tm, 1))
    dsk = _ew("m2_dexp", _m2_dexp, [(d_pad, *_const((1, 128)))], [((1, NI), f32, *_const((1, NI)), None)], (1, 1))[0]
    GB = M2_GB
    x_blk, bc_blk = (CHUNK, 256 * GB), (CHUNK, 128 * GB)
    cins = [(dsk, (1, 256 * GB), lambda u, c: (0, u))]
    core_ins = [(xbc, x_blk, lambda u, c: (c, u), (L, NI), lambda u, c: (c, u)),
                (dtb, x_blk, lambda u, c: (c, u)), (dab, x_blk, lambda u, c: (c, u)),
                (xbc, bc_blk, lambda u, c: (c, 16 // GB + u), (L, D), lambda u, c: (c, u)),
                (xbc, bc_blk, lambda u, c: (c, 24 // GB + u), (L, D), lambda u, c: (c, u))]
    (yc,), saved_s = _scan_fwd("m2_core", _ssd_chunk, 2 * GB, (128, 128), cins, core_ins,
                               [((L, NI), x_blk, lambda u, c: (c, u))], 8 // GB, nc)
    gblk = ((tm, 256), lambda g, r: (r, g))
    post_ins = [(yc, *gblk), (proj, *gblk), (norm_g, (1, 256), lambda g, r: (0, g))]
    yn = _ew("m2_post", _m2_post, post_ins, [((L, NI), f32, *gblk, None)], (8, L // tm))[0]
    h_out = _mm_plain("m2_out", yn, w_out, _NN, epi_fn=lambda acc, res: acc + res, aux=[(h, "e")])
    saved = dict(hn=hn, proj=proj, act_ins=act_ins, d_pad=d_pad, cins=cins, core_ins=core_ins, saved_s=saved_s,
                 post_ins=post_ins, yn=yn)
    return h_out, saved


def _m2_bwd(dh, h, g_norm, w_ext, conv_w, conv_b, w_out, sv):
    L, D = h.shape
    tm = _row_tile(L)
    nc = L // CHUNK
    NI = M2_INNER
    dyn = _mm_plain("m2_out_dx", dh, w_out, _NT)
    dw_out = _mm_plain("m2_out_dw", sv["yn"], dh, _TN)
    gblk = ((tm, 256), lambda g, r: (r, g))
    d_yc, d_z, d_ng = _ew("m2_post_bwd", _vjp_fn(_m2_post, 3), sv["post_ins"] + [(dyn, *gblk)],
                          [((L, NI), f32, *gblk, None)] * 2 + [((1, NI), f32, (1, 256), lambda g, r: (0, g), "inner")],
                          (8, L // tm))
    (d_dsk,), (dx, d_dtb, d_dab, dB, dC) = _scan_bwd(
        "m2_core_bwd", _ssd_chunk, 2 * M2_GB, (128, 128), sv["cins"], sv["core_ins"], sv["saved_s"],
        [(d_yc, (CHUNK, 256 * M2_GB), lambda u, c: (c, u))], 8 // M2_GB, nc)
    row128 = ((1, 128), f32, *_const((1, 128)), "all")
    d_dt_raw, d_dtbias, d_alog = _ew(
        "m2_act_bwd", _vjp_fn(_m2_act, 3), sv["act_ins"] + [(d_dtb, *_rows(tm, NI)), (d_dab, *_rows(tm, NI))],
        [((L, 128), f32, *_rows(tm, 128), None), row128, row128], (L // tm, 1))
    d_d = _ew("m2_dexp_bwd", _vjp_fn(_m2_dexp, 1), [(sv["d_pad"], *_const((1, 128))), (d_dsk, *_const((1, NI)))],
              [((1, 128), f32, *_const((1, 128)), None)], (1, 1))[0]
    d_conv_out = jnp.concatenate([dx, dB, dC], axis=1)
    d_conv_in, d_conv_w, d_conv_b = _conv_bwd("m2_conv_bwd", sv["proj"], NI // 128, conv_w, conv_b, d_conv_out)
    d_proj = jnp.concatenate([d_z, d_conv_in, d_dt_raw], axis=1)
    dw_ext = _mm_plain("m2_in_dw", sv["hn"], d_proj, _TN)
    dhn = _mm_plain("m2_in_dx", d_proj, w_ext, _NT)
    dh_in, dg = _rms_bwd("mix_norm_bwd", h, g_norm, dhn, dh)
    grads = dict(norm=dg, w_ext=dw_ext, conv_w=d_conv_w, conv_b=d_conv_b, dt_bias=d_dtbias[0, :M2_HEADS],
                 a_log=d_alog[0, :M2_HEADS], d=d_d[0, :M2_HEADS], norm_g=d_ng, w_out=dw_out)
    return dh_in, grads


def _mesh_pos():
    return lax.axis_index("x"), lax.axis_index("y"), lax.axis_index("c")


def _flip(pos, p):
    x, y, c = pos
    return (1 - x if p & 4 else x, 1 - y if p & 2 else y, 1 - c if p & 1 else c)


def _index(pos):
    return 4 * pos[0] + 2 * pos[1] + pos[2]


def _comm_call(name, body, arrays, out_shape, n_sem):
    n = len(arrays)
    hbm = pl.BlockSpec(memory_space=pl.ANY)
    return pl.pallas_call(
        body, name=name, in_specs=[hbm] * n, out_specs=[hbm] * n, out_shape=out_shape,
        scratch_shapes=[pltpu.SemaphoreType.DMA((n, n_sem)), pltpu.SemaphoreType.DMA((n, n_sem)),
                        pltpu.SemaphoreType.DMA((n,))],
    )(*arrays)


def _gather(name, arrays):
    n = len(arrays)

    def body(*refs):
        ins, outs = refs[:n], refs[n:2 * n]
        send_sems, recv_sems, loc_sems = refs[2 * n:]
        me = _mesh_pos()
        c = me[2]
        sib = _flip(me, 1)
        chips = [_flip(me, 4), _flip(me, 2), _flip(me, 6)]

        def copy(w, k, block, to, src=None):
            slab = outs[w].at[_index(block)]
            return pltpu.make_async_remote_copy(
                src_ref=slab if src is None else src, dst_ref=slab, send_sem=send_sems.at[w, k],
                recv_sem=recv_sems.at[w, k], device_id=to, device_id_type=MESH)

        local = [pltpu.make_async_copy(ins[w], outs[w].at[_index(me)], loc_sems.at[w]) for w in range(n)]
        for cp in local:
            cp.start()
        first = [copy(w, 0, me, sib, src=ins[w]) for w in range(n)]
        first += [copy(w, 1 + j, me, chip, src=ins[w]) for j, chip in enumerate(chips) for w in range(n)]
        for cp in first:
            cp.start()
        passed = []
        for j, chip in enumerate(chips):
            for w in range(n):
                copy(w, 1 + j, chip, me).wait_recv()
                fwd = copy(w, 4 + j, chip, sib)
                fwd.start()
                passed.append(fwd)
        for w in range(n):
            copy(w, 0, sib, me).wait_recv()
        for j, chip in enumerate(chips):
            for w in range(n):
                copy(w, 4 + j, (chip[0], chip[1], 1 - c), me).wait_recv()
        for cp in first + passed:
            cp.wait_send()
        for cp in local:
            cp.wait()

    out_shape = [jax.ShapeDtypeStruct((N_DEV,) + a.shape, a.dtype) for a in arrays]
    return _comm_call(name, body, arrays, out_shape, N_DEV - 1)


def _scatter_pair(name, arrays):
    n = len(arrays)

    def body(*refs):
        ins, outs = refs[:n], refs[n:2 * n]
        send_sems, recv_sems, _ = refs[2 * n:]
        me = _mesh_pos()
        c = me[2]
        sib = _flip(me, 1)

        def copy(w, q):
            return pltpu.make_async_remote_copy(
                src_ref=ins[w].at[2 * q + 1 - c], dst_ref=outs[w].at[q], send_sem=send_sems.at[w, q],
                recv_sem=recv_sems.at[w, q], device_id=sib, device_id_type=MESH)

        cps = [copy(w, q) for q in range(4) for w in range(n)]
        for cp in cps:
            cp.start()
        for cp in cps:
            cp.wait()

    out_shape = [jax.ShapeDtypeStruct((4,) + a.shape[1:], a.dtype) for a in arrays]
    return _comm_call(name, body, arrays, out_shape, 4)


def _pair_add(name, mine, theirs, dtype):
    _, R, C = mine.shape
    tr = R if R <= 256 else (256 if C <= 512 else 128)
    blk = ((4, tr, C), lambda r, z: (0, r, 0))
    return _ew(name, lambda a, b: a + b, [(mine, *blk), (theirs, *blk)], [(mine.shape, dtype, *blk, None)],
               (R // tr, 1))[0]


def _scatter_chips(name, arrays):
    n = len(arrays)

    def body(*refs):
        ins, outs = refs[:n], refs[n:2 * n]
        send_sems, recv_sems, loc_sems = refs[2 * n:]
        me = _mesh_pos()
        mq = 2 * me[0] + me[1]
        peers = [_flip(me, 4), _flip(me, 2), _flip(me, 6)]

        def copy(w, k):
            peer = peers[k]
            return pltpu.make_async_remote_copy(
                src_ref=ins[w].at[2 * peer[0] + peer[1]], dst_ref=outs[w].at[mq], send_sem=send_sems.at[w, k],
                recv_sem=recv_sems.at[w, k], device_id=peer, device_id_type=MESH)

        def arrival(w, k):
            peer = peers[k]
            return pltpu.make_async_remote_copy(
                src_ref=ins[w].at[mq], dst_ref=outs[w].at[2 * peer[0] + peer[1]], send_sem=send_sems.at[w, k],
                recv_sem=recv_sems.at[w, k], device_id=peer, device_id_type=MESH)

        local = [pltpu.make_async_copy(ins[w].at[mq], outs[w].at[mq], loc_sems.at[w]) for w in range(n)]
        for cp in local:
            cp.start()
        sends = [copy(w, k) for k in range(3) for w in range(n)]
        for cp in sends:
            cp.start()
        for k in range(3):
            for w in range(n):
                arrival(w, k).wait_recv()
        for cp in sends:
            cp.wait_send()
        for cp in local:
            cp.wait()

    out_shape = [jax.ShapeDtypeStruct(a.shape, a.dtype) for a in arrays]
    return _comm_call(name, body, arrays, out_shape, 3)


def _adamw(name, parts, w, m, v):
    R, C = w.shape
    n_parts = parts.shape[0]
    tr = R if R <= 256 else (256 if C <= 512 else 128)
    bc1 = 1.0 - ADAM_B1 ** ADAM_STEP
    bc2 = 1.0 - ADAM_B2 ** ADAM_STEP

    def f(p, wv, mv, vv):
        g = p[0].astype(f32)
        for i in range(1, n_parts):
            g = g + p[i].astype(f32)
        m2 = ADAM_B1 * mv + (1.0 - ADAM_B1) * g
        v2 = ADAM_B2 * vv + (1.0 - ADAM_B2) * jnp.square(g)
        delta = -ADAM_LR * ((m2 / bc1) / (jnp.sqrt(v2 / bc2) + ADAM_EPS) + ADAM_WD * wv)
        return g, delta, m2, v2

    blk = ((tr, C), lambda r, z: (r, 0))
    return _ew(name, f, [(parts, (n_parts, tr, C), lambda r, z: (0, r, 0)), (w, *blk), (m, *blk), (v, *blk)],
               [((R, C), f32, *blk, None)] * 4, (R // tr, 1))


_WEIGHTS = ["norm_mix_g", "norm_mlp_g", "mlp_w1", "mlp_w2", "gdn_w_in", "gdn_conv_w", "gdn_a_log", "gdn_dt_bias",
            "gdn_o_norm_g", "gdn_w_out", "s5_w_in", "s5_lam_re", "s5_lam_im", "s5_log_dt", "s5_b_re", "s5_b_im",
            "s5_c_re", "s5_c_im", "s5_d", "s5_w_out", "m2_w_in", "m2_conv_w", "m2_conv_b", "m2_dt_bias", "m2_a_log",
            "m2_d", "m2_norm_g", "m2_w_out", "final_norm_g"]
_SHARDED = ["mlp_w1", "mlp_w2", "gdn_w_in", "gdn_w_out", "s5_w_in", "s5_w_out", "m2_w_in", "m2_w_out",
            "gdn_conv_w", "m2_conv_w", "m2_conv_b", "m2_norm_g"]
_MATRICES = _SHARDED[:8]
_REPLICATED = [n for n in _WEIGHTS if n not in _SHARDED]
_GDN_IN, _M2_IN = 4112, 6176
_LAYER_KIND = (0, 1, 2, 0)


def _as2d(a):
    return a.reshape(-1, a.shape[-1])


def _cols_from_shards(g, width):
    return g.transpose(1, 0, 2).reshape(g.shape[1], width)


def _cols_to_shards(a, width):
    return a[:, :width].reshape(a.shape[0], N_DEV, width // N_DEV).transpose(1, 0, 2)


def kernel(x, norm_mix_g, norm_mlp_g, mlp_w1, mlp_w2, gdn_w_in, gdn_conv_w, gdn_a_log, gdn_dt_bias, gdn_o_norm_g, gdn_w_out, s5_w_in, s5_lam_re, s5_lam_im, s5_log_dt, s5_b_re, s5_b_im, s5_c_re, s5_c_im, s5_d, s5_w_out, m2_w_in, m2_conv_w, m2_conv_b, m2_dt_bias, m2_a_log, m2_d, m2_norm_g, m2_w_out, final_norm_g, loss_target, m_norm_mix_g, m_norm_mlp_g, m_mlp_w1, m_mlp_w2, m_gdn_w_in, m_gdn_conv_w, m_gdn_a_log, m_gdn_dt_bias, m_gdn_o_norm_g, m_gdn_w_out, m_s5_w_in, m_s5_lam_re, m_s5_lam_im, m_s5_log_dt, m_s5_b_re, m_s5_b_im, m_s5_c_re, m_s5_c_im, m_s5_d, m_s5_w_out, m_m2_w_in, m_m2_conv_w, m_m2_conv_b, m_m2_dt_bias, m_m2_a_log, m_m2_d, m_m2_norm_g, m_m2_w_out, m_final_norm_g, v_norm_mix_g, v_norm_mlp_g, v_mlp_w1, v_mlp_w2, v_gdn_w_in, v_gdn_conv_w, v_gdn_a_log, v_gdn_dt_bias, v_gdn_o_norm_g, v_gdn_w_out, v_s5_w_in, v_s5_lam_re, v_s5_lam_im, v_s5_log_dt, v_s5_b_re, v_s5_b_im, v_s5_c_re, v_s5_c_im, v_s5_d, v_s5_w_out, v_m2_w_in, v_m2_conv_w, v_m2_conv_b, v_m2_dt_bias, v_m2_a_log, v_m2_d, v_m2_norm_g, v_m2_w_out, v_final_norm_g):
    args = locals()
    W = {n: args[n] for n in _WEIGHTS}
    MOM = {n: args["m_" + n] for n in _WEIGHTS}
    VAR = {n: args["v_" + n] for n in _WEIGHTS}
    h = x[0]
    target = loss_target[0]
    L, D = h.shape

    sends = [W[n].astype(bf16) if n in _MATRICES else _as2d(W[n]) for n in _SHARDED]
    G = dict(zip(_SHARDED, _gather("gather_weights", sends)))
    w1g, w2g = G["mlp_w1"], G["mlp_w2"]
    gdn_in = [jnp.pad(_cols_from_shards(G["gdn_w_in"][:, j], _GDN_IN), ((0, 0), (0, GDN_EXT - _GDN_IN))) for j in range(2)]
    gdn_out = [G["gdn_w_out"][:, j].reshape(D, D) for j in range(2)]
    gdn_conv = [_cols_from_shards(G["gdn_conv_w"][:, 4 * j:4 * j + 4], 3 * D) for j in range(2)]
    s5_in = G["s5_w_in"].reshape(D, D)
    s5_out_g = G["s5_w_out"]
    m2_in = jnp.pad(_cols_from_shards(G["m2_w_in"][:, 0], _M2_IN), ((0, 0), (0, M2_EXT - _M2_IN)))
    m2_out = G["m2_w_out"].reshape(M2_INNER, D)
    m2_conv = _cols_from_shards(G["m2_conv_w"], 2 * M2_INNER)
    m2_cb = _cols_from_shards(G["m2_conv_b"], 2 * M2_INNER)
    m2_ng = _cols_from_shards(G["m2_norm_g"], M2_INNER)

    def mixer_fwd(i, hv):
        kind, j = _LAYER_KIND[i], i // 3
        gn = norm_mix_g[i].reshape(1, D)
        if kind == 0:
            return _gdn_fwd(hv, gn, gdn_in[j], gdn_conv[j], gdn_a_log[j], gdn_dt_bias[j], gdn_o_norm_g[j], gdn_out[j])
        if kind == 1:
            return _s5_fwd(hv, gn, s5_in, s5_lam_re[0], s5_lam_im[0], s5_log_dt[0], s5_b_re[0], s5_b_im[0],
                           s5_c_re[0], s5_c_im[0], s5_d[0], s5_out_g)
        return _m2_fwd(hv, gn, m2_in, m2_conv, m2_cb, m2_dt_bias[0], m2_a_log[0], m2_d[0], m2_ng, m2_out)

    def mixer_bwd(i, dh, hv, sv):
        kind, j = _LAYER_KIND[i], i // 3
        gn = norm_mix_g[i].reshape(1, D)
        if kind == 0:
            return _gdn_bwd(dh, hv, gn, gdn_in[j], gdn_conv[j], gdn_out[j], sv)
        if kind == 1:
            return _s5_bwd(dh, hv, gn, s5_in, s5_out_g, sv)
        return _m2_bwd(dh, hv, gn, m2_in, m2_conv, m2_cb, m2_out, sv)

    tape = []
    for i in range(4):
        h_mid, sv = mixer_fwd(i, h)
        h_next, hn, h1 = _mlp_fwd(h_mid, norm_mlp_g[i].reshape(1, D), w1g, w2g, i)
        tape.append((h, sv, h_mid, hn, h1))
        h = h_next
    loss_row, dh, d_final = _loss_head(h, final_norm_g.reshape(1, D), target)
    loss = lax.psum(loss_row[0, 0], ("x", "y", "c"))

    dw1 = lax.empty((N_DEV, 4, D, D_FF // N_DEV), f32)
    dw2 = lax.empty((N_DEV, 4, D_FF // N_DEV, D), f32)
    d_mix, d_mlp, mg = [None] * 4, [None] * 4, [None] * 4
    for i in reversed(range(4)):
        h_in, sv, h_mid, hn, h1 = tape[i]
        dh, d_mlp[i], dw1, dw2 = _mlp_bwd(dh, h_mid, norm_mlp_g[i].reshape(1, D), hn, h1, w1g, w2g, i, dw1, dw2)
        dh, mg[i] = mixer_bwd(i, dh, h_in, sv)
        d_mix[i] = mg[i]["norm"]
    grad_x = dh.reshape(1, L, D)
    ga, gb_, s5g, m2g = mg[0], mg[3], mg[1], mg[2]

    full = {
        "mlp_w1": dw1, "mlp_w2": dw2,
        "gdn_w_in": jnp.stack([_cols_to_shards(g["w_ext"], _GDN_IN) for g in (ga, gb_)], axis=1),
        "gdn_w_out": jnp.stack([g["w_out"].reshape(N_DEV, D // N_DEV, D) for g in (ga, gb_)], axis=1),
        "s5_w_in": s5g["w_in"].reshape(N_DEV, 1, D // N_DEV, D), "s5_w_out": s5g["w_out"],
        "m2_w_in": _cols_to_shards(m2g["w_ext"], _M2_IN)[:, None],
        "m2_w_out": m2g["w_out"].reshape(N_DEV, 1, M2_INNER // N_DEV, D),
        "gdn_conv_w": jnp.concatenate([_cols_to_shards(g["conv_w"], 3 * D) for g in (ga, gb_)], axis=1),
        "m2_conv_w": _cols_to_shards(m2g["conv_w"], 2 * M2_INNER),
        "m2_conv_b": _cols_to_shards(m2g["conv_b"], 2 * M2_INNER),
        "m2_norm_g": _cols_to_shards(m2g["norm_g"], M2_INNER),
    }
    sends = [full[n].reshape((N_DEV,) + _as2d(W[n]).shape) for n in _SHARDED]
    core = lax.axis_index("c")
    theirs = _scatter_pair("scatter_pair", sends)
    chip_sums = []
    for n, mine8, th in zip(_SHARDED, sends, theirs):
        mine = lax.dynamic_index_in_dim(mine8.reshape((4, 2) + mine8.shape[1:]), core, axis=1, keepdims=False)
        chip_sums.append(_pair_add("pair_add_" + n, mine, th, bf16 if n in _MATRICES else f32))
    parts = dict(zip(_SHARDED, _scatter_chips("scatter_chips", chip_sums)))

    rep = {
        "norm_mix_g": jnp.concatenate(d_mix, axis=0), "norm_mlp_g": jnp.concatenate(d_mlp, axis=0),
        "gdn_a_log": jnp.stack([ga["a_log"], gb_["a_log"]]), "gdn_dt_bias": jnp.stack([ga["dt_bias"], gb_["dt_bias"]]),
        "gdn_o_norm_g": jnp.stack([ga["o_norm_g"], gb_["o_norm_g"]]),
        "s5_lam_re": s5g["lam_re"], "s5_lam_im": s5g["lam_im"], "s5_log_dt": s5g["log_dt"], "s5_b_re": s5g["b_re"],
        "s5_b_im": s5g["b_im"], "s5_c_re": s5g["c_re"], "s5_c_im": s5g["c_im"], "s5_d": s5g["d"],
        "m2_dt_bias": m2g["dt_bias"], "m2_a_log": m2g["a_log"], "m2_d": m2g["d"], "final_norm_g": d_final,
    }

    def pack(d):
        flat = jnp.concatenate([d[n].reshape(-1).astype(f32) for n in _REPLICATED])
        return jnp.pad(flat, (0, -flat.shape[0] % (256 * 128))).reshape(-1, 128)

    (rep_parts,) = _gather("gather_small_grads", [pack(rep)])

    res = {}
    for n in _SHARDED:
        w2d = _as2d(W[n])
        out = _adamw("adamw_" + n, parts[n], w2d, _as2d(MOM[n]), _as2d(VAR[n]))
        res[n] = [o.reshape(W[n].shape) for o in out]
    out = _adamw("adamw_replicated", rep_parts, pack(W), pack(MOM), pack(VAR))
    off = 0
    for n in _REPLICATED:
        size = W[n].size
        res[n] = [o.reshape(-1)[off:off + size].reshape(W[n].shape) for o in out]
        off += size

    return (loss, grad_x, *[res[n][0] for n in _WEIGHTS], *[res[n][1] for n in _WEIGHTS],
            *[res[n][2] for n in _WEIGHTS], *[res[n][3] for n in _WEIGHTS])
```

```python
import functools

import jax
import jax.numpy as jnp
from jax import lax
from jax.experimental import pallas as pl
from jax.experimental.pallas import tpu as pltpu

f32 = jnp.float32
bf16 = jnp.bfloat16
HI = lax.Precision.HIGHEST
MESH = pl.DeviceIdType.MESH

N_DEV = 8
D_MODEL = 1024
D_FF = 4096
CHUNK = 64
RMS_EPS = 1e-6
GDN_HEADS = 8
GDN_HB = 4
GDN_EXT = 4224
S5_STATE = 64
M2_INNER = 2048
M2_EXT = 6272
M2_HEADS = 32
M2_GB = 2
VMEM_LIMIT_BYTES = 56 * 1024 * 1024

ADAM_LR, ADAM_B1, ADAM_B2, ADAM_EPS, ADAM_WD, ADAM_STEP = 0.001, 0.9, 0.999, 1e-08, 0.01, 10

_NN = ((1,), (0,))
_NT = ((1,), (1,))
_TN = ((0,), (0,))


def _dot(a, b, dims=_NN):
    return lax.dot_general(a, b, (dims, ((), ())), precision=HI, preferred_element_type=f32)


def _dotb(a, b, dims=_NN):
    return lax.dot_general(a.astype(bf16), b.astype(bf16), (dims, ((), ())), preferred_element_type=f32)


def _bdot(p, q, dims):
    return lax.dot_general(p, q, (dims, ((), ())), preferred_element_type=f32)


def _pieces(x, n):
    out = []
    for _ in range(n - 1):
        p = x.astype(bf16)
        out.append(p)
        x = x - p.astype(f32)
    return out + [x.astype(bf16)]


def _dot01_raw(mask, b, dims=_NN, mask_first=True):
    m = mask.astype(bf16)
    p = _pieces(b, 3)
    if mask_first:
        return _bdot(m, p[0], dims) + (_bdot(m, p[1], dims) + _bdot(m, p[2], dims))
    return _bdot(p[0], m, dims) + (_bdot(p[1], m, dims) + _bdot(p[2], m, dims))


@jax.custom_vjp
def _dot01_nn(mask, b):
    return _dot01_raw(mask, b, _NN)


@jax.custom_vjp
def _dot01_nt(mask, b):
    return _dot01_raw(mask, b, _NT)


_dot01_nn.defvjp(lambda m, b: (_dot01_raw(m, b, _NN), m),
                 lambda m, ct: (jnp.zeros_like(m), _dot01_raw(m, ct, _TN, mask_first=True)))
_dot01_nt.defvjp(lambda m, b: (_dot01_raw(m, b, _NT), m),
                 lambda m, ct: (jnp.zeros_like(m), _dot01_raw(m, ct, _TN, mask_first=False)))


def _dot01_vjp(mask, b, dims=_NN):
    return _dot01_nn(mask, b) if dims == _NN else _dot01_nt(mask, b)


def _dot3_raw(a, b, dims=_NN):
    (ah, al), (bh, bl) = _pieces(a, 2), _pieces(b, 2)
    return _bdot(ah, bh, dims) + (_bdot(ah, bl, dims) + _bdot(al, bh, dims))


@jax.custom_vjp
def _dot3_vjp(a, b):
    return _dot3_raw(a, b)


_dot3_vjp.defvjp(lambda a, b: (_dot3_raw(a, b), (a, b)),
                 lambda res, ct: (_dot3_raw(ct, res[1], _NT), _dot3_raw(res[0], ct, _TN)))


class _Dots:
    def __init__(self, dot3, dot01):
        self.dot3, self.dot01 = dot3, dot01


_PLAIN_DOTS = _Dots(_dot3_raw, _dot01_raw)
_VJP_DOTS = _Dots(_dot3_vjp, _dot01_vjp)


def _iota(shape, dim):
    return lax.broadcasted_iota(jnp.int32, shape, dim)


def _params(n_grid):
    return pltpu.CompilerParams(dimension_semantics=("arbitrary",) * n_grid, vmem_limit_bytes=VMEM_LIMIT_BYTES)


def _row_tile(n_rows):
    return min(512, n_rows)


def _mm_rows(n_rows):
    return min(1024, n_rows)


def _col_tile(n, cap=1024):
    best = 128
    for t in range(128, cap + 1, 128):
        if n % t == 0:
            best = t
    return best


def _mm(name, a, b, *, dims, grid, a_spec, b_spec, out_shape, out_spec, aux=(), a_fn=None, epi_fn=None,
        acc_shape, out_init=None):
    nk = grid[2]
    n_aux = len(aux)
    kinds = [x[2] for x in aux]

    def body(*refs):
        a_ref, b_ref = refs[0], refs[1]
        aux_refs = refs[2:2 + n_aux]
        pos = 2 + n_aux + (1 if out_init is not None else 0)
        o_ref, acc_ref = refs[pos], refs[pos + 1]
        k = pl.program_id(2)

        @pl.when(k == 0)
        def _():
            acc_ref[...] = jnp.zeros_like(acc_ref)

        av = a_ref[...]
        if a_fn is not None:
            av = a_fn(av, *[r[...] for r, kd in zip(aux_refs, kinds) if kd == "a"])
        acc_ref[...] += lax.dot_general(av.astype(bf16), b_ref[...].astype(bf16), (dims, ((), ())),
                                        preferred_element_type=f32)

        @pl.when(k == nk - 1)
        def _():
            r = acc_ref[...]
            if epi_fn is not None:
                r = epi_fn(r, *[x[...] for x, kd in zip(aux_refs, kinds) if kd == "e"])
            o_ref[...] = r.astype(o_ref.dtype)

    in_specs = [a_spec, b_spec] + [x[1] for x in aux]
    args = [a, b] + [x[0] for x in aux]
    aliases = {}
    if out_init is not None:
        in_specs.append(pl.BlockSpec(memory_space=pl.ANY))
        args.append(out_init)
        aliases = {len(args) - 1: 0}
    return pl.pallas_call(
        body, name=name, grid=grid, in_specs=in_specs, out_specs=out_spec, out_shape=out_shape,
        scratch_shapes=[pltpu.VMEM(acc_shape, f32)], input_output_aliases=aliases,
        compiler_params=_params(3),
    )(*args)


def _ew(name, f, ins, outs, grid):
    n_in = len(ins)
    modes = [o[4] for o in outs]

    def body(*refs):
        vals = [r[...] for r in refs[:n_in]]
        res = f(*vals)
        if not isinstance(res, (tuple, list)):
            res = (res,)
        for r, o_ref, mode in zip(res, refs[n_in:], modes):
            if mode is None:
                o_ref[...] = r.astype(o_ref.dtype)
                continue
            first = pl.program_id(1) == 0
            if mode == "all":
                first = jnp.logical_and(first, pl.program_id(0) == 0)

            @pl.when(first)
            def _(r=r, o_ref=o_ref):
                o_ref[...] = r.astype(o_ref.dtype)

            @pl.when(jnp.logical_not(first))
            def _(r=r, o_ref=o_ref):
                o_ref[...] += r.astype(o_ref.dtype)

    res = pl.pallas_call(
        body, name=name, grid=grid,
        in_specs=[pl.BlockSpec(blk, im) for _, blk, im in ins],
        out_specs=[pl.BlockSpec(o[2], o[3]) for o in outs],
        out_shape=[jax.ShapeDtypeStruct(o[0], o[1]) for o in outs],
        compiler_params=_params(2),
    )(*[a for a, _, _ in ins])
    return res


def _vjp_fn(f, n_primal):
    def g(*args):
        _, vjp = jax.vjp(f, *args[:n_primal])
        cts = args[n_primal:]
        return vjp(cts[0] if len(cts) == 1 else tuple(cts))
    return g


def _scan_fwd(name, step, n_state, state_shape, cins, ins, outs, n_units, n_chunks):
    n_c, n_in, n_out = len(cins), len(ins), len(outs)

    def body(*refs):
        c_refs = refs[:n_c]
        in_refs = refs[n_c:n_c + n_in]
        out_refs = refs[n_c + n_in:n_c + n_in + n_out]
        saved = refs[n_c + n_in + n_out:n_c + n_in + n_out + n_state]
        st = refs[n_c + n_in + n_out + n_state:]

        @pl.when(pl.program_id(1) == 0)
        def _():
            for s in st:
                s[...] = jnp.zeros_like(s)

        cur = [s[...] for s in st]
        for sv, s in zip(saved, cur):
            sv[...] = s
        new, res = step(cur, [r[...] for r in c_refs], [r[...] for r in in_refs], _PLAIN_DOTS)
        for s, n in zip(st, new):
            s[...] = n
        for o, r in zip(out_refs, res):
            o[...] = r

    sshape = (n_units, n_chunks) + state_shape
    sblock = (None, None) + state_shape
    nz = len(state_shape)
    res = pl.pallas_call(
        body, name=name, grid=(n_units, n_chunks),
        in_specs=[pl.BlockSpec(e[1], e[2]) for e in cins + ins],
        out_specs=[pl.BlockSpec(o[1], o[2]) for o in outs]
        + [pl.BlockSpec(sblock, lambda u, c: (u, c) + (0,) * nz)] * n_state,
        out_shape=[jax.ShapeDtypeStruct(o[0], f32) for o in outs]
        + [jax.ShapeDtypeStruct(sshape, f32)] * n_state,
        scratch_shapes=[pltpu.VMEM(state_shape, f32)] * n_state,
        compiler_params=_params(2),
    )(*[e[0] for e in cins + ins])
    return res[:n_out], res[n_out:]


def _scan_bwd(name, step, n_state, state_shape, cins, ins, saved, douts, n_units, n_chunks):
    n_c, n_in, n_do = len(cins), len(ins), len(douts)

    def flip(im):
        return lambda u, c: im(u, n_chunks - 1 - c)

    def body(*refs):
        p = 0
        c_refs = refs[p:p + n_c]; p += n_c
        in_refs = refs[p:p + n_in]; p += n_in
        sv_refs = refs[p:p + n_state]; p += n_state
        do_refs = refs[p:p + n_do]; p += n_do
        dc_refs = refs[p:p + n_c]; p += n_c
        di_refs = refs[p:p + n_in]; p += n_in
        dst = refs[p:]
        first = pl.program_id(1) == 0

        @pl.when(first)
        def _():
            for s in dst:
                s[...] = jnp.zeros_like(s)

        def fn(states, consts, vals):
            new, res = step(states, consts, vals, _VJP_DOTS)
            return tuple(new), tuple(res)

        prim = ([r[...] for r in sv_refs], [r[...] for r in c_refs], [r[...] for r in in_refs])
        _, vjp = jax.vjp(fn, *prim)
        d_states, d_consts, d_vals = vjp((tuple(s[...] for s in dst), tuple(r[...] for r in do_refs)))
        for s, g in zip(dst, d_states):
            s[...] = g
        for o, g in zip(di_refs, d_vals):
            o[...] = g
        for o, g in zip(dc_refs, d_consts):
            @pl.when(first)
            def _(o=o, g=g):
                o[...] = g

            @pl.when(jnp.logical_not(first))
            def _(o=o, g=g):
                o[...] += g

    nz = len(state_shape)
    sblock = (None, None) + state_shape
    def gshape(e):
        return e[3] if len(e) == 5 else e[0].shape

    def gmap(e):
        return e[4] if len(e) == 5 else e[2]

    in_specs = ([pl.BlockSpec(e[1], e[2]) for e in cins]
                + [pl.BlockSpec(e[1], flip(e[2])) for e in ins]
                + [pl.BlockSpec(sblock, lambda u, c: (u, n_chunks - 1 - c) + (0,) * nz)] * n_state
                + [pl.BlockSpec(e[1], flip(e[2])) for e in douts])
    out_specs = ([pl.BlockSpec(e[1], e[2]) for e in cins]
                 + [pl.BlockSpec(e[1], flip(gmap(e))) for e in ins])
    out_shape = [jax.ShapeDtypeStruct(gshape(e), f32) for e in cins + ins]
    res = pl.pallas_call(
        body, name=name, grid=(n_units, n_chunks), in_specs=in_specs, out_specs=out_specs, out_shape=out_shape,
        scratch_shapes=[pltpu.VMEM(state_shape, f32)] * n_state,
        compiler_params=_params(2),
    )(*([e[0] for e in cins + ins] + list(saved) + [e[0] for e in douts]))
    return res[:n_c], res[n_c:]


def _rms(x, g):
    return x * lax.rsqrt(jnp.mean(x * x, axis=-1, keepdims=True) + RMS_EPS) * g


def _rows(tm, width):
    return (tm, width), lambda r, z: (r, 0)


def _const(shape):
    return shape, lambda r, z: (0,) * len(shape)


def _rms_fwd(name, h, g):
    L, D = h.shape
    tm = _row_tile(L)
    return _ew(name, _rms, [(h, *_rows(tm, D)), (g, *_const((1, D)))],
               [((L, D), f32, *_rows(tm, D), None)], (L // tm, 1))[0]


def _rms_bwd(name, h, g, d_hn, d_res):
    L, D = h.shape
    tm = _row_tile(L)

    def f(hv, gv, dv, rv):
        dh, dg = _vjp_fn(_rms, 2)(hv, gv, dv)
        return dh + rv, dg

    return _ew(name, f, [(h, *_rows(tm, D)), (g, *_const((1, D))), (d_hn, *_rows(tm, D)), (d_res, *_rows(tm, D))],
               [((L, D), f32, *_rows(tm, D), None), ((1, D), f32, *_const((1, D)), "all")], (L // tm, 1))


def _loss_head(h, g, target):
    L, D = h.shape
    tm = _row_tile(L)

    def f(hv, gv, tv):
        def lf(a, b):
            e = jnp.square(_rms(a, b) - tv)
            return (0.5 / D) * jnp.sum(jnp.sum(e, axis=1, keepdims=True), axis=0, keepdims=True)

        val, vjp = jax.vjp(lf, hv, gv)
        dh, dg = vjp(jnp.ones((1, 1), f32))
        return jnp.broadcast_to(val, (1, 128)), dh, dg

    return _ew("loss_head", f, [(h, *_rows(tm, D)), (g, *_const((1, D))), (target, *_rows(tm, D))],
               [((1, 128), f32, *_const((1, 128)), "all"), ((L, D), f32, *_rows(tm, D), None),
                ((1, D), f32, *_const((1, D)), "all")], (L // tm, 1))


def _sqrelu(x):
    return jnp.square(jnp.maximum(x, 0.0))


def _mm_plain(name, a, b, dims, *, a_fn=None, epi_fn=None, aux=()):
    if dims == _NN:
        (M, K), N = a.shape, b.shape[1]
    elif dims == _NT:
        (M, K), N = a.shape, b.shape[0]
    else:
        (K, M), N = a.shape, b.shape[1]
    tm = _mm_rows(M)
    tn = _col_tile(N, 1536)
    tk = _col_tile(K)
    if dims == _TN:
        tk = min(512, K)
        a_spec = pl.BlockSpec((tk, tm), lambda i, j, k: (k, i))
        b_spec = pl.BlockSpec((tk, tn), lambda i, j, k: (k, j))
        a_aux = pl.BlockSpec((tk, tm), lambda i, j, k: (k, i))
    elif dims == _NT:
        a_spec = pl.BlockSpec((tm, tk), lambda i, j, k: (i, k))
        b_spec = pl.BlockSpec((tn, tk), lambda i, j, k: (j, k))
        a_aux = pl.BlockSpec((tm, tk), lambda i, j, k: (i, k))
    else:
        a_spec = pl.BlockSpec((tm, tk), lambda i, j, k: (i, k))
        b_spec = pl.BlockSpec((tk, tn), lambda i, j, k: (k, j))
        a_aux = pl.BlockSpec((tm, tk), lambda i, j, k: (i, k))
    e_aux = pl.BlockSpec((tm, tn), lambda i, j, k: (i, j))
    aux_full = [(x, a_aux if kd == "a" else e_aux, kd) for x, kd in aux]
    return _mm(name, a, b, dims=dims, grid=(M // tm, N // tn, K // tk), a_spec=a_spec, b_spec=b_spec,
               out_shape=jax.ShapeDtypeStruct((M, N), f32), out_spec=pl.BlockSpec((tm, tn), lambda i, j, k: (i, j)),
               aux=aux_full, a_fn=a_fn, epi_fn=epi_fn, acc_shape=(tm, tn))


def _mlp_fwd(h, g, w1g, w2g, layer):
    L, D = h.shape
    tm = _mm_rows(L)
    fs = D_FF // N_DEV
    hn = _rms_fwd("mlp_norm", h, g)
    h1 = _mm("mlp_up", hn, w1g, dims=_NN, grid=(L // tm, N_DEV, 1),
             a_spec=pl.BlockSpec((tm, D), lambda i, j, k: (i, 0)),
             b_spec=pl.BlockSpec((None, None, D, fs), lambda i, j, k: (j, layer, 0, 0)),
             out_shape=jax.ShapeDtypeStruct((L, D_FF), f32), out_spec=pl.BlockSpec((tm, fs), lambda i, j, k: (i, j)),
             acc_shape=(tm, fs))
    tn = D
    h_out = _mm("mlp_down", h1, w2g, dims=_NN, grid=(L // tm, D // tn, N_DEV),
                a_spec=pl.BlockSpec((tm, fs), lambda i, j, k: (i, k)),
                b_spec=pl.BlockSpec((None, None, fs, tn), lambda i, j, k: (k, layer, 0, j)),
                out_shape=jax.ShapeDtypeStruct((L, D), f32), out_spec=pl.BlockSpec((tm, tn), lambda i, j, k: (i, j)),
                aux=[(h, pl.BlockSpec((tm, tn), lambda i, j, k: (i, j)), "e")],
                a_fn=_sqrelu, epi_fn=lambda acc, res: acc + res, acc_shape=(tm, tn))
    return h_out, hn, h1


def _mlp_bwd(dh, h, g, hn, h1, w1g, w2g, layer, dw1_buf, dw2_buf):
    L, D = h.shape
    tm = _mm_rows(L)
    fs = D_FF // N_DEV
    tk = _mm_rows(L)
    dh1 = _mm("mlp_down_dx", dh, w2g, dims=_NT, grid=(L // tm, N_DEV, 1),
              a_spec=pl.BlockSpec((tm, D), lambda i, j, k: (i, 0)),
              b_spec=pl.BlockSpec((None, None, fs, D), lambda i, j, k: (j, layer, 0, 0)),
              out_shape=jax.ShapeDtypeStruct((L, D_FF), f32), out_spec=pl.BlockSpec((tm, fs), lambda i, j, k: (i, j)),
              aux=[(h1, pl.BlockSpec((tm, fs), lambda i, j, k: (i, j)), "e")],
              epi_fn=lambda acc, pre: acc * (2.0 * jnp.maximum(pre, 0.0)), acc_shape=(tm, fs))
    dw2_buf = _mm("mlp_down_dw", h1, dh, dims=_TN, grid=(N_DEV, 1, L // tk),
                  a_spec=pl.BlockSpec((tk, fs), lambda i, j, k: (k, i)),
                  b_spec=pl.BlockSpec((tk, D), lambda i, j, k: (k, 0)),
                  out_shape=jax.ShapeDtypeStruct(dw2_buf.shape, f32),
                  out_spec=pl.BlockSpec((None, None, fs, D), lambda i, j, k: (i, layer, 0, 0)),
                  a_fn=_sqrelu, acc_shape=(fs, D), out_init=dw2_buf)
    tr = D
    dw1_buf = _mm("mlp_up_dw", hn, dh1, dims=_TN, grid=(D // tr, N_DEV, L // tk),
                  a_spec=pl.BlockSpec((tk, tr), lambda i, j, k: (k, i)),
                  b_spec=pl.BlockSpec((tk, fs), lambda i, j, k: (k, j)),
                  out_shape=jax.ShapeDtypeStruct(dw1_buf.shape, f32),
                  out_spec=pl.BlockSpec((None, None, tr, fs), lambda i, j, k: (j, layer, i, 0)),
                  acc_shape=(tr, fs), out_init=dw1_buf)
    tn = D
    dhn = _mm("mlp_up_dx", dh1, w1g, dims=_NT, grid=(L // tm, D // tn, N_DEV),
              a_spec=pl.BlockSpec((tm, fs), lambda i, j, k: (i, k)),
              b_spec=pl.BlockSpec((None, None, tn, fs), lambda i, j, k: (k, layer, j, 0)),
              out_shape=jax.ShapeDtypeStruct((L, D), f32), out_spec=pl.BlockSpec((tm, tn), lambda i, j, k: (i, j)),
              acc_shape=(tm, tn))
    dh_in, dg = _rms_bwd("mlp_norm_bwd", h, g, dhn, dh)
    return dh_in, dg, dw1_buf, dw2_buf


def _shift_dn(x, s, row):
    return x if s == 0 else jnp.where(row >= s, pltpu.roll(x, s, 0), 0.0)


def _shift_up(x, s, row):
    n = x.shape[0]
    return x if s == 0 else jnp.where(row < n - s, pltpu.roll(x, n - s, 0), 0.0)


def _conv_pre(x, w, b, row):
    c = jnp.broadcast_to(b, x.shape)
    for j in range(4):
        c = c + w[j:j + 1, :] * _shift_dn(x, 3 - j, row)
    return c


def _conv_fwd(name, x_arr, blk_off, w, b):
    L = x_arr.shape[0]
    C = w.shape[1]

    def f(x, wv, bv):
        c = _conv_pre(x, wv, bv, _iota(x.shape, 0))
        return c * jax.nn.sigmoid(c)

    return _ew(name, f, [(x_arr, (L, 128), lambda j, z: (0, blk_off + j)), (w, (4, 128), lambda j, z: (0, j)),
                         (b, (1, 128), lambda j, z: (0, j))],
               [((L, C), f32, (L, 128), lambda j, z: (0, j), None)], (C // 128, 1))[0]


def _conv_bwd(name, x_arr, blk_off, w, b, dy):
    L = x_arr.shape[0]
    C = w.shape[1]

    def f(x, wv, bv, g):
        row = _iota(x.shape, 0)
        c = _conv_pre(x, wv, bv, row)
        s = jax.nn.sigmoid(c)
        dc = g * (s * (1.0 + c * (1.0 - s)))
        dx = jnp.zeros_like(x)
        dw = jnp.zeros((4, 128), f32)
        r4 = _iota((4, 128), 0)
        for j in range(4):
            dx = dx + wv[j:j + 1, :] * _shift_up(dc, 3 - j, row)
            dwj = jnp.sum(dc * _shift_dn(x, 3 - j, row), axis=0, keepdims=True)
            dw = dw + jnp.where(r4 == j, jnp.broadcast_to(dwj, (4, 128)), 0.0)
        return dx, dw, jnp.sum(dc, axis=0, keepdims=True)

    return _ew(name, f, [(x_arr, (L, 128), lambda j, z: (0, blk_off + j)), (w, (4, 128), lambda j, z: (0, j)),
                         (b, (1, 128), lambda j, z: (0, j)), (dy, (L, 128), lambda j, z: (0, j))],
               [((L, C), f32, (L, 128), lambda j, z: (0, j), None), ((4, C), f32, (4, 128), lambda j, z: (0, j), None),
                ((1, C), f32, (1, 128), lambda j, z: (0, j), None)], (C // 128, 1))


def _l2norm(t):
    return t * lax.rsqrt(jnp.sum(t * t, axis=-1, keepdims=True) + 1e-6)


def _gdn_act(cq, ck, ab, alog, dtb):
    h = pl.program_id(1)
    qn = _l2norm(cq) * (128.0 ** -0.5)
    kn = _l2norm(ck)
    lane = _iota(ab.shape, 1)
    a_raw = jnp.sum(jnp.where(lane == h, ab, 0.0), axis=1, keepdims=True)
    b_raw = jnp.sum(jnp.where(lane == h + GDN_HEADS, ab, 0.0), axis=1, keepdims=True)
    lane1 = _iota(alog.shape, 1)
    al = jnp.sum(jnp.where(lane1 == h, alog, 0.0), axis=1, keepdims=True)
    db = jnp.sum(jnp.where(lane1 == h, dtb, 0.0), axis=1, keepdims=True)
    g = -jnp.exp(al) * jax.nn.softplus(a_raw + db)
    beta = jax.nn.sigmoid(b_raw)
    return qn, kn, jnp.broadcast_to(g, cq.shape), jnp.broadcast_to(beta, cq.shape)


def _each(f, *lists):
    return [f(*a) for a in zip(*lists)]


def _gdn_chunk(states, consts, vals, dots):
    S = list(states)
    cut = [slice(128 * i, 128 * i + 128) for i in range(len(S))]
    q, k, v, gb, bb = ([t[:, c] for c in cut] for t in vals)
    C = vals[0].shape[0]
    row, col = _iota((C, C), 0), _iota((C, C), 1)
    causal, strict = row >= col, row > col
    ltri = causal.astype(f32)
    eye = (row == col).astype(f32)
    e0 = (_iota((C, 128), 1) == 0).astype(f32)
    last = _iota((C, 1), 0) == C - 1
    Gb = _each(lambda g: dots.dot01(ltri, g), gb)
    Gc = _each(lambda g: jnp.mean(g, axis=1, keepdims=True), Gb)
    Gr = _each(lambda g: dots.dot01(e0, g, _NT), Gb)
    bc = _each(lambda b: jnp.mean(b, axis=1, keepdims=True), bb)
    decay = _each(lambda gc, gr: jnp.where(causal, jnp.exp(jnp.where(causal, gc - gr, 0.0)), 0.0), Gc, Gr)
    kk = _each(lambda a: _dotb(a, a, _NT), k)
    A = _each(lambda b, x, d: jnp.where(strict, b * x * d, 0.0), bc, kk, decay)
    M = _each(lambda a: eye - a, A)
    P = _each(lambda a: dots.dot3(a, a), A)
    for it in range(5):
        M = _each(lambda m, p: m + dots.dot3(m, p), M, P)
        if it < 4:
            P = _each(lambda p: dots.dot3(p, p), P)
    eG = _each(jnp.exp, Gc)
    u = _each(lambda m, x, b: _dotb(m, x * b), M, v, bc)
    w = _each(lambda m, x, b, e: _dotb(m, x * (b * e)), M, k, bc, eG)
    qk = _each(lambda a, b, d: _dotb(a, b, _NT) * d, q, k, decay)
    g_last = _each(lambda gc: jnp.sum(jnp.where(last, gc, 0.0), axis=0, keepdims=True), Gc)
    v_new = _each(lambda a, b, s: a - _dotb(b, s), u, w, S)
    o = _each(lambda a, e, s, b, x: _dotb(a * e, s) + _dotb(b, x), q, eG, S, qk, v_new)
    S_new = _each(lambda gl, s, a, gc, x: jnp.exp(gl) * s + _dotb(a * jnp.exp(gl - gc), x, _TN), g_last, S, k, Gc, v_new)
    return S_new, [jnp.concatenate(o, axis=1)]


def _gdn_post(o, gate, g):
    return _rms(o, g) * (gate * jax.nn.sigmoid(gate))


def _pad_row(v):
    return jnp.pad(v.astype(f32), (0, 128 - v.shape[0])).reshape(1, 128)


def _gdn_fwd(h, g_norm, w_ext, conv_w, a_log, dt_bias, o_norm_g, w_out):
    L, D = h.shape
    tm = _row_tile(L)
    nc = L // CHUNK
    H = GDN_HEADS
    hn = _rms_fwd("mix_norm", h, g_norm)
    proj = _mm_plain("gdn_in", hn, w_ext, _NN)
    zb = jnp.zeros((1, 3 * D), f32)
    cq = _conv_fwd("gdn_conv", proj, 0, conv_w, zb)
    alog, dtb = _pad_row(a_log), _pad_row(dt_bias)
    act_ins = [(cq, (tm, 128), lambda r, hh: (r, hh)), (cq, (tm, 128), lambda r, hh: (r, H + hh)),
               (proj, (tm, 128), lambda r, hh: (r, 4 * H)), (alog, (1, 128), lambda r, hh: (0, 0)),
               (dtb, (1, 128), lambda r, hh: (0, 0))]
    qn, kn, gb, bb = _ew("gdn_act", _gdn_act, act_ins,
                         [((L, D), f32, (tm, 128), lambda r, hh: (r, hh), None)] * 4, (L // tm, H))
    cblk = (CHUNK, 128 * GDN_HB)
    core_ins = [(qn, cblk, lambda u, c: (c, u)), (kn, cblk, lambda u, c: (c, u)),
                (cq, cblk, lambda u, c: (c, 2 * H // GDN_HB + u), (L, D), lambda u, c: (c, u)),
                (gb, cblk, lambda u, c: (c, u)), (bb, cblk, lambda u, c: (c, u))]
    (o,), saved_s = _scan_fwd("gdn_core", _gdn_chunk, GDN_HB, (128, 128), [], core_ins,
                              [((L, D), cblk, lambda u, c: (c, u))], H // GDN_HB, nc)
    on = o_norm_g.reshape(1, 128)
    post_ins = [(o, (tm, 128), lambda r, hh: (r, hh)), (proj, (tm, 128), lambda r, hh: (r, 3 * H + hh)),
                (on, (1, 128), lambda r, hh: (0, 0))]
    y = _ew("gdn_post", _gdn_post, post_ins, [((L, D), f32, (tm, 128), lambda r, hh: (r, hh), None)], (L // tm, H))[0]
    h_out = _mm_plain("gdn_out", y, w_out, _NN, epi_fn=lambda acc, res: acc + res, aux=[(h, "e")])
    saved = dict(hn=hn, proj=proj, cq=cq, alog=alog, dtb=dtb, act_ins=act_ins, core_ins=core_ins, saved_s=saved_s,
                 post_ins=post_ins, y=y, zb=zb)
    return h_out, saved


def _gdn_bwd(dh, h, g_norm, w_ext, conv_w, w_out, sv):
    L, D = h.shape
    tm = _row_tile(L)
    nc = L // CHUNK
    H = GDN_HEADS
    dy = _mm_plain("gdn_out_dx", dh, w_out, _NT)
    dw_out = _mm_plain("gdn_out_dw", sv["y"], dh, _TN)
    hd = ((L, D), f32, (tm, 128), lambda r, hh: (r, hh), None)
    d_o, d_gate, d_on = _ew("gdn_post_bwd", _vjp_fn(_gdn_post, 3),
                            sv["post_ins"] + [(dy, (tm, 128), lambda r, hh: (r, hh))],
                            [hd, hd, ((1, 128), f32, (1, 128), lambda r, hh: (0, 0), "all")], (L // tm, H))
    cblk = (CHUNK, 128 * GDN_HB)
    _, (dqn, dkn, dv, dgb, dbb) = _scan_bwd("gdn_core_bwd", _gdn_chunk, GDN_HB, (128, 128), [], sv["core_ins"],
                                            sv["saved_s"], [(d_o, cblk, lambda u, c: (c, u))], H // GDN_HB, nc)
    cts = [(t, (tm, 128), lambda r, hh: (r, hh)) for t in (dqn, dkn, dgb, dbb)]
    row128 = ((1, 128), f32, (1, 128), lambda r, hh: (0, 0), "all")
    d_cq, d_ck, d_ab, d_alog, d_dtb = _ew(
        "gdn_act_bwd", _vjp_fn(_gdn_act, 5), sv["act_ins"] + cts,
        [hd, hd, ((L, 128), f32, (tm, 128), lambda r, hh: (r, 0), "inner"), row128, row128], (L // tm, H))
    d_conv_out = jnp.concatenate([d_cq, d_ck, dv], axis=1)
    d_conv_in, d_conv_w, _ = _conv_bwd("gdn_conv_bwd", sv["proj"], 0, conv_w, sv["zb"], d_conv_out)
    d_proj = jnp.concatenate([d_conv_in, d_gate, d_ab], axis=1)
    dw_ext = _mm_plain("gdn_in_dw", sv["hn"], d_proj, _TN)
    dhn = _mm_plain("gdn_in_dx", d_proj, w_ext, _NT)
    dh_in, dg = _rms_bwd("mix_norm_bwd", h, g_norm, dhn, dh)
    grads = dict(norm=dg, w_ext=dw_ext, conv_w=d_conv_w, a_log=d_alog[0, :H], dt_bias=d_dtb[0, :H],
                 o_norm_g=d_on[0], w_out=dw_out)
    return dh_in, grads


def _expand_lanes(row, width, rep):
    sel = ((_iota((128, width), 1) // rep) == _iota((128, width), 0)).astype(f32)
    return jnp.mean(_dot(jnp.broadcast_to(row, (8, 128)), sel), axis=0, keepdims=True)


def _s5_params(lre, lim, ldt, wbr, wbi):
    dt = jnp.exp(_expand_lanes(ldt, 512, S5_STATE))
    mag = jnp.exp(lre * dt)
    ang = lim * dt
    abr, abi = mag * jnp.cos(ang), mag * jnp.sin(ang)
    nr = abr - 1.0
    den = lre * lre + lim * lim
    cr = (nr * lre + abi * lim) / den
    ci = (abi * lre - nr * lim) / den
    return abr, abi, cr * wbr - ci * wbi, cr * wbi + ci * wbr


def _s5_scan(name, xr, xi, ar, ai, rev, want_prev):
    L, W = xr.shape
    nb = L // 8
    n_out = 4 if want_prev else 2

    def body(xr_ref, xi_ref, ar_ref, ai_ref, *outs):
        a_r = ar_ref[...]
        a_i = -ai_ref[...] if rev else ai_ref[...]

        def cm(p, q):
            return p[0] * q[0] - p[1] * q[1], p[0] * q[1] + p[1] * q[0]

        a1 = (a_r, a_i)
        a2 = cm(a1, a1)
        a3 = cm(a2, a1)
        a4 = cm(a2, a2)
        pw = [a1, a2, a3, a4, cm(a4, a1), cm(a4, a2), cm(a4, a3), cm(a4, a4)]
        row = _iota((8, 128), 0)
        tab_r = jnp.zeros((8, 128), f32)
        tab_i = jnp.zeros((8, 128), f32)
        for t in range(8):
            idx = 7 - t if rev else t
            tab_r = jnp.where(row == idx, jnp.broadcast_to(pw[t][0], (8, 128)), tab_r)
            tab_i = jnp.where(row == idx, jnp.broadcast_to(pw[t][1], (8, 128)), tab_i)
        lv = [(d, jnp.broadcast_to(p[0], (8, 128)), jnp.broadcast_to(p[1], (8, 128))) for d, p in ((1, a1), (2, a2), (4, a4))]

        def step(i, carry):
            cr, ci = carry
            blk = nb - 1 - i if rev else i
            r0 = pl.multiple_of(blk * 8, 8)
            x_r = xr_ref[pl.ds(r0, 8), :]
            x_i = xi_ref[pl.ds(r0, 8), :]
            for d, p_r, p_i in lv:
                if rev:
                    s_r = jnp.where(row < 8 - d, pltpu.roll(x_r, 8 - d, 0), 0.0)
                    s_i = jnp.where(row < 8 - d, pltpu.roll(x_i, 8 - d, 0), 0.0)
                else:
                    s_r = jnp.where(row >= d, pltpu.roll(x_r, d, 0), 0.0)
                    s_i = jnp.where(row >= d, pltpu.roll(x_i, d, 0), 0.0)
                x_r, x_i = x_r + p_r * s_r - p_i * s_i, x_i + p_r * s_i + p_i * s_r
            x_r, x_i = x_r + tab_r * cr - tab_i * ci, x_i + tab_r * ci + tab_i * cr
            outs[0][pl.ds(r0, 8), :] = x_r
            outs[1][pl.ds(r0, 8), :] = x_i
            if want_prev:
                outs[2][pl.ds(r0, 8), :] = jnp.where(row >= 1, pltpu.roll(x_r, 1, 0), cr)
                outs[3][pl.ds(r0, 8), :] = jnp.where(row >= 1, pltpu.roll(x_i, 1, 0), ci)
            e = 0 if rev else 7
            return jnp.broadcast_to(x_r[e:e + 1, :], (8, 128)), jnp.broadcast_to(x_i[e:e + 1, :], (8, 128))

        lax.fori_loop(0, nb, step, (jnp.zeros((8, 128), f32), jnp.zeros((8, 128), f32)))

    col = pl.BlockSpec((L, 128), lambda q, z: (0, q))
    aspec = pl.BlockSpec((None, 1, 128), lambda q, z: (q // 4, 0, q % 4))
    return pl.pallas_call(
        body, name=name, grid=(W // 128, 1), in_specs=[col, col, aspec, aspec], out_specs=[col] * n_out,
        out_shape=[jax.ShapeDtypeStruct((L, W), f32)] * n_out, compiler_params=_params(2),
    )(xr, xi, ar, ai)


def _blockdiag(t, n_in, n_out):
    t4 = t.reshape(8, 8, n_in, n_out)
    return jnp.einsum("jaio,ab->jaibo", t4, jnp.eye(8, dtype=t.dtype)).reshape(8, 8 * n_in, 8 * n_out)


def _blockdiag_t(w, n_in, n_out):
    w5 = w.reshape(8, 8, n_in, 8, n_out)
    return jnp.einsum("jaibo,ab->jaio", w5, jnp.eye(8, dtype=w.dtype)).reshape(64, n_in, n_out)


def _glu(ag, h):
    n = ag.shape[1] // 2
    return h + ag[:, :n] * jax.nn.sigmoid(ag[:, n:])


def _s5_fwd(h, g_norm, w_in, lam_re, lam_im, log_dt, b_re, b_im, c_re, c_im, d_skip, w_out_g):
    L, D = h.shape
    tm = _row_tile(L)
    W = 8 * 512
    hn = _rms_fwd("mix_norm", h, g_norm)
    u = _mm_plain("s5_in", hn, w_in, _NN)
    lre, lim = lam_re.reshape(8, 1, 512), lam_im.reshape(8, 1, 512)
    ldt = jnp.pad(log_dt.reshape(8, 1, 8), ((0, 0), (0, 0), (0, 120)))
    wbr = _blockdiag(b_re.transpose(0, 2, 1), 16, 64)
    wbi = _blockdiag(b_im.transpose(0, 2, 1), 16, 64)
    wcr = _blockdiag(c_re.transpose(0, 2, 1), 64, 16)
    wci = _blockdiag(c_im.transpose(0, 2, 1), 64, 16)
    jb = lambda shape: (shape, lambda j, z: (j, 0, 0))
    par_ins = [(lre, *jb((None, 1, 512))), (lim, *jb((None, 1, 512))), (ldt, *jb((None, 1, 128))),
               (wbr, *jb((None, 128, 512))), (wbi, *jb((None, 128, 512)))]
    abr, abi, bbr, bbi = _ew("s5_params", _s5_params, par_ins,
                             [((8, 1, 512), f32, *jb((None, 1, 512)), None)] * 2
                             + [((8, 128, 512), f32, *jb((None, 128, 512)), None)] * 2, (8, 1))

    def bu(name, wb):
        return _mm(name, u, wb, dims=_NN, grid=(L // tm, 8, 1),
                   a_spec=pl.BlockSpec((tm, 128), lambda i, j, k: (i, j)),
                   b_spec=pl.BlockSpec((None, 128, 512), lambda i, j, k: (j, 0, 0)),
                   out_shape=jax.ShapeDtypeStruct((L, W), f32), out_spec=pl.BlockSpec((tm, 512), lambda i, j, k: (i, j)),
                   acc_shape=(tm, 512))

    bur, bui = bu("s5_bu", bbr), bu("s5_bu", bbi)
    sr, si, pr, pi = _s5_scan("s5_scan", bur, bui, abr, abi, False, True)
    d_row = d_skip.reshape(1, D)
    cspec = dict(a_spec=pl.BlockSpec((tm, 512), lambda i, j, k: (i, j)),
                 b_spec=pl.BlockSpec((None, 512, 128), lambda i, j, k: (j, 0, 0)),
                 out_shape=jax.ShapeDtypeStruct((L, D), f32), out_spec=pl.BlockSpec((tm, 128), lambda i, j, k: (i, j)),
                 acc_shape=(tm, 128))
    e128 = pl.BlockSpec((tm, 128), lambda i, j, k: (i, j))
    pre1 = _mm("s5_c_re", sr, wcr, dims=_NN, grid=(L // tm, 8, 1), **cspec)
    pre = _mm("s5_c_im", si, wci, dims=_NN, grid=(L // tm, 8, 1),
              aux=[(pre1, e128, "e"), (u, e128, "e"), (d_row, pl.BlockSpec((1, 128), lambda i, j, k: (0, j)), "e")],
              epi_fn=lambda acc, p1, uu, dd: p1 - acc + dd * uu, **cspec)
    ws = D // N_DEV * 2
    ag = _mm("s5_out", pre, w_out_g, dims=_NN, grid=(L // tm, N_DEV, 1),
             a_spec=pl.BlockSpec((tm, D), lambda i, j, k: (i, 0)),
             b_spec=pl.BlockSpec((None, None, D, ws), lambda i, j, k: (j, 0, 0, 0)),
             out_shape=jax.ShapeDtypeStruct((L, 2 * D), f32), out_spec=pl.BlockSpec((tm, ws), lambda i, j, k: (i, j)),
             a_fn=jax.nn.gelu, acc_shape=(tm, ws))
    h_out = _ew("s5_glu", _glu, [(ag, *_rows(tm, 2 * D)), (h, *_rows(tm, D))],
                [((L, D), f32, *_rows(tm, D), None)], (L // tm, 1))[0]
    saved = dict(hn=hn, u=u, par_ins=par_ins, abr=abr, abi=abi, bbr=bbr, bbi=bbi, wcr=wcr, wci=wci, sr=sr, si=si,
                 pr=pr, pi=pi, pre=pre, ag=ag, d_row=d_row)
    return h_out, saved


def _s5_bwd(dh, h, g_norm, w_in, w_out_g, sv):
    L, D = h.shape
    tm = _row_tile(L)
    tk = min(512, L)
    W = 8 * 512
    ws = D // N_DEV * 2
    u, pre, d_row = sv["u"], sv["pre"], sv["d_row"]
    d_ag = _ew("s5_glu_bwd", lambda ag, hv, g: _vjp_fn(_glu, 2)(ag, hv, g)[0],
               [(sv["ag"], *_rows(tm, 2 * D)), (h, *_rows(tm, D)), (dh, *_rows(tm, D))],
               [((L, 2 * D), f32, *_rows(tm, 2 * D), None)], (L // tm, 1))[0]
    tr = 512
    dw_out = _mm("s5_out_dw", pre, d_ag, dims=_TN, grid=(D // tr, N_DEV, L // tk),
                 a_spec=pl.BlockSpec((tk, tr), lambda i, j, k: (k, i)),
                 b_spec=pl.BlockSpec((tk, ws), lambda i, j, k: (k, j)),
                 out_shape=jax.ShapeDtypeStruct((N_DEV, 1, D, ws), f32),
                 out_spec=pl.BlockSpec((None, None, tr, ws), lambda i, j, k: (j, 0, i, 0)),
                 a_fn=jax.nn.gelu, acc_shape=(tr, ws))
    tn = 512
    dpre = _mm("s5_out_dx", d_ag, w_out_g, dims=_NT, grid=(L // tm, D // tn, N_DEV),
               a_spec=pl.BlockSpec((tm, ws), lambda i, j, k: (i, k)),
               b_spec=pl.BlockSpec((None, None, tn, ws), lambda i, j, k: (k, 0, j, 0)),
               out_shape=jax.ShapeDtypeStruct((L, D), f32), out_spec=pl.BlockSpec((tm, tn), lambda i, j, k: (i, j)),
               aux=[(pre, pl.BlockSpec((tm, tn), lambda i, j, k: (i, j)), "e")],
               epi_fn=lambda acc, p: _vjp_fn(jax.nn.gelu, 1)(p, acc)[0], acc_shape=(tm, tn))
    d_d = _ew("s5_dskip", lambda a, b: jnp.sum(a * b, axis=0, keepdims=True),
              [(dpre, *_rows(tm, D)), (u, *_rows(tm, D))], [((1, D), f32, *_const((1, D)), "all")], (L // tm, 1))[0]
    neg = lambda acc: -acc
    dsspec = dict(dims=_NT, grid=(L // tm, 8, 1), a_spec=pl.BlockSpec((tm, 128), lambda i, j, k: (i, j)),
                  b_spec=pl.BlockSpec((None, 512, 128), lambda i, j, k: (j, 0, 0)),
                  out_shape=jax.ShapeDtypeStruct((L, W), f32), out_spec=pl.BlockSpec((tm, 512), lambda i, j, k: (i, j)),
                  acc_shape=(tm, 512))
    dsr = _mm("s5_c_re_dx", dpre, sv["wcr"], **dsspec)
    dsi = _mm("s5_c_im_dx", dpre, sv["wci"], epi_fn=neg, **dsspec)
    dwspec = dict(dims=_TN, grid=(8, 1, L // tk), a_spec=pl.BlockSpec((tk, 512), lambda i, j, k: (k, i)),
                  b_spec=pl.BlockSpec((tk, 128), lambda i, j, k: (k, i)),
                  out_shape=jax.ShapeDtypeStruct((8, 512, 128), f32),
                  out_spec=pl.BlockSpec((None, 512, 128), lambda i, j, k: (i, 0, 0)), acc_shape=(512, 128))
    dwcr = _mm("s5_c_re_dw", sv["sr"], dpre, **dwspec)
    dwci = _mm("s5_c_im_dw", sv["si"], dpre, epi_fn=neg, **dwspec)
    lr, li = _s5_scan("s5_scan_bwd", dsr, dsi, sv["abr"], sv["abi"], True, False)

    def da(lrv, liv, prv, piv):
        return (jnp.sum(lrv * prv + liv * piv, axis=0, keepdims=True),
                jnp.sum(liv * prv - lrv * piv, axis=0, keepdims=True))

    sblk = ((tm, 512), lambda j, r: (r, j))
    dabr, dabi = _ew("s5_dlam", da, [(lr, *sblk), (li, *sblk), (sv["pr"], *sblk), (sv["pi"], *sblk)],
                     [((8, 1, 512), f32, (None, 1, 512), lambda j, r: (j, 0, 0), "inner")] * 2, (8, L // tm))
    dbspec = dict(dims=_TN, grid=(8, 1, L // tk), a_spec=pl.BlockSpec((tk, 128), lambda i, j, k: (k, i)),
                  b_spec=pl.BlockSpec((tk, 512), lambda i, j, k: (k, i)),
                  out_shape=jax.ShapeDtypeStruct((8, 128, 512), f32),
                  out_spec=pl.BlockSpec((None, 128, 512), lambda i, j, k: (i, 0, 0)), acc_shape=(128, 512))
    dbbr = _mm("s5_bu_dw", u, lr, **dbspec)
    dbbi = _mm("s5_bu_dw", u, li, **dbspec)
    duspec = dict(dims=_NT, grid=(L // tm, 8, 1), a_spec=pl.BlockSpec((tm, 512), lambda i, j, k: (i, j)),
                  b_spec=pl.BlockSpec((None, 128, 512), lambda i, j, k: (j, 0, 0)),
                  out_shape=jax.ShapeDtypeStruct((L, D), f32), out_spec=pl.BlockSpec((tm, 128), lambda i, j, k: (i, j)),
                  acc_shape=(tm, 128))
    e128 = pl.BlockSpec((tm, 128), lambda i, j, k: (i, j))
    du1 = _mm("s5_bu_dx_re", lr, sv["bbr"], **duspec)
    du = _mm("s5_bu_dx_im", li, sv["bbi"],
             aux=[(du1, e128, "e"), (dpre, e128, "e"), (d_row, pl.BlockSpec((1, 128), lambda i, j, k: (0, j)), "e")],
             epi_fn=lambda acc, d1, dp, dd: acc + d1 + dp * dd, **duspec)
    jb = lambda shape: (shape, lambda j, z: (j, 0, 0))
    cts = [(dabr, *jb((None, 1, 512))), (dabi, *jb((None, 1, 512))), (dbbr, *jb((None, 128, 512))),
           (dbbi, *jb((None, 128, 512)))]
    dlre, dlim, dldt, dwbr, dwbi = _ew(
        "s5_params_bwd", _vjp_fn(_s5_params, 5), sv["par_ins"] + cts,
        [((8, 1, 512), f32, *jb((None, 1, 512)), None)] * 2 + [((8, 1, 128), f32, *jb((None, 1, 128)), None)]
        + [((8, 128, 512), f32, *jb((None, 128, 512)), None)] * 2, (8, 1))
    dw_in = _mm_plain("s5_in_dw", sv["hn"], du, _TN)
    dhn = _mm_plain("s5_in_dx", du, w_in, _NT)
    dh_in, dg = _rms_bwd("mix_norm_bwd", h, g_norm, dhn, dh)
    grads = dict(norm=dg, w_in=dw_in, lam_re=dlre.reshape(64, 64), lam_im=dlim.reshape(64, 64),
                 log_dt=dldt[:, 0, :8].reshape(64),
                 b_re=_blockdiag_t(dwbr, 16, 64).transpose(0, 2, 1), b_im=_blockdiag_t(dwbi, 16, 64).transpose(0, 2, 1),
                 c_re=_blockdiag_t(dwcr, 64, 16).transpose(0, 2, 1), c_im=_blockdiag_t(dwci, 64, 16).transpose(0, 2, 1),
                 d=d_d[0], w_out=dw_out)
    return dh_in, grads


def _m2_act(dt_raw, dtbias, alog):
    dt = jax.nn.softplus(dt_raw + dtbias)
    da = dt * (-jnp.exp(alog))
    sel = ((_iota((128, M2_INNER), 1) // 64) == _iota((128, M2_INNER), 0)).astype(f32)
    return _dot(dt, sel), _dot(da, sel)


def _m2_dexp(d):
    return _expand_lanes(d, M2_INNER, 64)


def _ssd_chunk(states, consts, vals, dots):
    (dsk,) = consts
    S = list(states)
    n = len(S)
    cut = [slice(128 * i, 128 * i + 128) for i in range(n)]
    x, dtb, dab = ([t[:, c] for c in cut] for t in vals[:3])
    dsk = [dsk[:, c] for c in cut]
    B = [vals[3][:, cut[i // 2]] for i in range(n)]
    Cm = [vals[4][:, cut[i // 2]] for i in range(n)]
    C = vals[0].shape[0]
    row, col = _iota((C, C), 0), _iota((C, C), 1)
    causal = row >= col
    ltri = causal.astype(f32)
    lane = _iota((C, 128), 1)
    last = _iota((C, 128), 0) == C - 1
    eye128 = _iota((128, 128), 0) == _iota((128, 128), 1)
    head = [jnp.logical_and(lane >= 64 * hh, lane < 64 * hh + 64) for hh in range(2)]
    pick = [(lane == 64 * hh).astype(f32) for hh in range(2)]
    xdt = _each(lambda a, b: a * b, x, dtb)
    cb = _each(lambda c, b: _dotb(c, b, _NT), Cm[::2], B[::2])
    cum = _each(lambda a: dots.dot01(ltri, a), dab)
    clast = _each(lambda a: jnp.sum(jnp.where(last, a, 0.0), axis=0, keepdims=True), cum)
    st = _each(lambda a, cl, cu, b: _dotb(a * jnp.exp(cl - cu), b, _TN), xdt, clast, cum, B)
    y = _each(lambda c, s, cu: _dotb(c, s, _NT) * jnp.exp(cu), Cm, S, cum)
    for hh in range(2):
        ccol = _each(lambda cu: jnp.sum(jnp.where(head[hh], cu, 0.0), axis=1, keepdims=True) * (1.0 / 64), cum)
        crow = _each(lambda cu: dots.dot01(pick[hh], cu, _NT), cum)
        lm = _each(lambda a, b: jnp.where(causal, jnp.exp(jnp.where(causal, a - b, 0.0)), 0.0), ccol, crow)
        y = [y[i] + _dotb(cb[i // 2] * lm[i], jnp.where(head[hh], xdt[i], 0.0)) for i in range(n)]
    cdcol = _each(lambda cl: jnp.sum(jnp.where(eye128, jnp.broadcast_to(jnp.exp(cl), (128, 128)), 0.0),
                                     axis=1, keepdims=True), clast)
    S_new = _each(lambda c, s, t: c * s + t, cdcol, S, st)
    out = _each(lambda a, d, b: a + d * b, y, dsk, x)
    return S_new, [jnp.concatenate(out, axis=1)]


def _m2_post(yc, z, ng):
    return _rms(yc * (z * jax.nn.sigmoid(z)), ng)


def _m2_fwd(h, g_norm, w_ext, conv_w, conv_b, dt_bias, a_log, d_skip, norm_g, w_out):
    L, D = h.shape
    tm = _row_tile(L)
    nc = L // CHUNK
    NI = M2_INNER
    hn = _rms_fwd("mix_norm", h, g_norm)
    proj = _mm_plain("m2_in", hn, w_ext, _NN)
    xbc = _conv_fwd("m2_conv", proj, NI // 128, conv_w, conv_b)
    dtb_row, alog_row, d_pad = _pad_row(dt_bias), _pad_row(a_log), _pad_row(d_skip)
    act_ins = [(proj, (tm, 128), lambda r, z: (r, 3 * NI // 128)), (dtb_row, *_const((1, 128))),
               (alog_row, *_const((1, 128)))]
    dtb, dab = _ew("m2_act", _m2_act, act_ins, [((L, NI), f32, *_rows(tm, NI), None)] * 2, (L // tm, 1))
    dsk = _ew("m2_dexp", _m2_dexp, [(d_pad, *_const((1, 128)))], [((1, NI), f32, *_const((1, NI)), None)], (1, 1))[0]
    GB = M2_GB
    x_blk, bc_blk = (CHUNK, 256 * GB), (CHUNK, 128 * GB)
    cins = [(dsk, (1, 256 * GB), lambda u, c: (0, u))]
    core_ins = [(xbc, x_blk, lambda u, c: (c, u), (L, NI), lambda u, c: (c, u)),
                (dtb, x_blk, lambda u, c: (c, u)), (dab, x_blk, lambda u, c: (c, u)),
                (xbc, bc_blk, lambda u, c: (c, 16 // GB + u), (L, D), lambda u, c: (c, u)),
                (xbc, bc_blk, lambda u, c: (c, 24 // GB + u), (L, D), lambda u, c: (c, u))]
    (yc,), saved_s = _scan_fwd("m2_core", _ssd_chunk, 2 * GB, (128, 128), cins, core_ins,
                               [((L, NI), x_blk, lambda u, c: (c, u))], 8 // GB, nc)
    gblk = ((tm, 256), lambda g, r: (r, g))
    post_ins = [(yc, *gblk), (proj, *gblk), (norm_g, (1, 256), lambda g, r: (0, g))]
    yn = _ew("m2_post", _m2_post, post_ins, [((L, NI), f32, *gblk, None)], (8, L // tm))[0]
    h_out = _mm_plain("m2_out", yn, w_out, _NN, epi_fn=lambda acc, res: acc + res, aux=[(h, "e")])
    saved = dict(hn=hn, proj=proj, act_ins=act_ins, d_pad=d_pad, cins=cins, core_ins=core_ins, saved_s=saved_s,
                 post_ins=post_ins, yn=yn)
    return h_out, saved


def _m2_bwd(dh, h, g_norm, w_ext, conv_w, conv_b, w_out, sv):
    L, D = h.shape
    tm = _row_tile(L)
    nc = L // CHUNK
    NI = M2_INNER
    dyn = _mm_plain("m2_out_dx", dh, w_out, _NT)
    dw_out = _mm_plain("m2_out_dw", sv["yn"], dh, _TN)
    gblk = ((tm, 256), lambda g, r: (r, g))
    d_yc, d_z, d_ng = _ew("m2_post_bwd", _vjp_fn(_m2_post, 3), sv["post_ins"] + [(dyn, *gblk)],
                          [((L, NI), f32, *gblk, None)] * 2 + [((1, NI), f32, (1, 256), lambda g, r: (0, g), "inner")],
                          (8, L // tm))
    (d_dsk,), (dx, d_dtb, d_dab, dB, dC) = _scan_bwd(
        "m2_core_bwd", _ssd_chunk, 2 * M2_GB, (128, 128), sv["cins"], sv["core_ins"], sv["saved_s"],
        [(d_yc, (CHUNK, 256 * M2_GB), lambda u, c: (c, u))], 8 // M2_GB, nc)
    row128 = ((1, 128), f32, *_const((1, 128)), "all")
    d_dt_raw, d_dtbias, d_alog = _ew(
        "m2_act_bwd", _vjp_fn(_m2_act, 3), sv["act_ins"] + [(d_dtb, *_rows(tm, NI)), (d_dab, *_rows(tm, NI))],
        [((L, 128), f32, *_rows(tm, 128), None), row128, row128], (L // tm, 1))
    d_d = _ew("m2_dexp_bwd", _vjp_fn(_m2_dexp, 1), [(sv["d_pad"], *_const((1, 128))), (d_dsk, *_const((1, NI)))],
              [((1, 128), f32, *_const((1, 128)), None)], (1, 1))[0]
    d_conv_out = jnp.concatenate([dx, dB, dC], axis=1)
    d_conv_in, d_conv_w, d_conv_b = _conv_bwd("m2_conv_bwd", sv["proj"], NI // 128, conv_w, conv_b, d_conv_out)
    d_proj = jnp.concatenate([d_z, d_conv_in, d_dt_raw], axis=1)
    dw_ext = _mm_plain("m2_in_dw", sv["hn"], d_proj, _TN)
    dhn = _mm_plain("m2_in_dx", d_proj, w_ext, _NT)
    dh_in, dg = _rms_bwd("mix_norm_bwd", h, g_norm, dhn, dh)
    grads = dict(norm=dg, w_ext=dw_ext, conv_w=d_conv_w, conv_b=d_conv_b, dt_bias=d_dtbias[0, :M2_HEADS],
                 a_log=d_alog[0, :M2_HEADS], d=d_d[0, :M2_HEADS], norm_g=d_ng, w_out=dw_out)
    return dh_in, grads


def _mesh_pos():
    return lax.axis_index("x"), lax.axis_index("y"), lax.axis_index("c")


def _flip(pos, p):
    x, y, c = pos
    return (1 - x if p & 4 else x, 1 - y if p & 2 else y, 1 - c if p & 1 else c)


def _index(pos):
    return 4 * pos[0] + 2 * pos[1] + pos[2]


def _comm_call(name, body, arrays, out_shape, n_sem):
    n = len(arrays)
    hbm = pl.BlockSpec(memory_space=pl.ANY)
    return pl.pallas_call(
        body, name=name, in_specs=[hbm] * n, out_specs=[hbm] * len(out_shape), out_shape=out_shape,
        scratch_shapes=[pltpu.SemaphoreType.DMA((n, n_sem)), pltpu.SemaphoreType.DMA((n, n_sem)),
                        pltpu.SemaphoreType.DMA((n, 4))],
    )(*arrays)


def _gather(name, arrays):
    n = len(arrays)

    def body(*refs):
        ins, outs = refs[:n], refs[n:2 * n]
        send_sems, recv_sems, loc_sems = refs[2 * n:]
        me = _mesh_pos()
        c = me[2]
        sib = _flip(me, 1)
        chips = [_flip(me, 4), _flip(me, 2), _flip(me, 6)]

        def copy(w, k, block, to, src=None):
            slab = outs[w].at[_index(block)]
            return pltpu.make_async_remote_copy(
                src_ref=slab if src is None else src, dst_ref=slab, send_sem=send_sems.at[w, k],
                recv_sem=recv_sems.at[w, k], device_id=to, device_id_type=MESH)

        local = [pltpu.make_async_copy(ins[w], outs[w].at[_index(me)], loc_sems.at[w, 0]) for w in range(n)]
        for cp in local:
            cp.start()
        first = [copy(w, 0, me, sib, src=ins[w]) for w in range(n)]
        first += [copy(w, 1 + j, me, chip, src=ins[w]) for j, chip in enumerate(chips) for w in range(n)]
        for cp in first:
            cp.start()
        passed = []
        for j, chip in enumerate(chips):
            for w in range(n):
                copy(w, 1 + j, chip, me).wait_recv()
                fwd = copy(w, 4 + j, chip, sib)
                fwd.start()
                passed.append(fwd)
        for w in range(n):
            copy(w, 0, sib, me).wait_recv()
        for j, chip in enumerate(chips):
            for w in range(n):
                copy(w, 4 + j, (chip[0], chip[1], 1 - c), me).wait_recv()
        for cp in first + passed:
            cp.wait_send()
        for cp in local:
            cp.wait()

    out_shape = [jax.ShapeDtypeStruct((N_DEV,) + a.shape, a.dtype) for a in arrays]
    return _comm_call(name, body, arrays, out_shape, N_DEV - 1)


def _scatter_pair(name, arrays):
    n = len(arrays)

    def body(*refs):
        ins, theirs, mine = refs[:n], refs[n:2 * n], refs[2 * n:3 * n]
        send_sems, recv_sems, loc_sems = refs[3 * n:]
        me = _mesh_pos()
        c = me[2]
        sib = _flip(me, 1)

        def copy(w, q):
            return pltpu.make_async_remote_copy(
                src_ref=ins[w].at[2 * q + 1 - c], dst_ref=theirs[w].at[q], send_sem=send_sems.at[w, q],
                recv_sem=recv_sems.at[w, q], device_id=sib, device_id_type=MESH)

        cps = [copy(w, q) for q in range(4) for w in range(n)]
        for cp in cps:
            cp.start()
        local = [pltpu.make_async_copy(ins[w].at[2 * q + c], mine[w].at[q], loc_sems.at[w, q])
                 for q in range(4) for w in range(n)]
        for cp in local:
            cp.start()
        for cp in cps:
            cp.wait()
        for cp in local:
            cp.wait()

    out_shape = [jax.ShapeDtypeStruct((4,) + a.shape[1:], a.dtype) for a in arrays] * 2
    res = _comm_call(name, body, arrays, out_shape, 4)
    return res[:n], res[n:]


def _pair_add(name, mine, theirs, dtype):
    _, R, C = mine.shape
    tr = R if R <= 256 else (256 if C <= 512 else 128)
    blk = ((4, tr, C), lambda r, z: (0, r, 0))
    return _ew(name, lambda a, b: a + b, [(mine, *blk), (theirs, *blk)], [(mine.shape, dtype, *blk, None)],
               (R // tr, 1))[0]


def _scatter_chips(name, arrays):
    n = len(arrays)

    def body(*refs):
        ins, outs = refs[:n], refs[n:2 * n]
        send_sems, recv_sems, loc_sems = refs[2 * n:]
        me = _mesh_pos()
        mq = 2 * me[0] + me[1]
        peers = [_flip(me, 4), _flip(me, 2), _flip(me, 6)]

        def copy(w, k):
            peer = peers[k]
            return pltpu.make_async_remote_copy(
                src_ref=ins[w].at[2 * peer[0] + peer[1]], dst_ref=outs[w].at[mq], send_sem=send_sems.at[w, k],
                recv_sem=recv_sems.at[w, k], device_id=peer, device_id_type=MESH)

        def arrival(w, k):
            peer = peers[k]
            return pltpu.make_async_remote_copy(
                src_ref=ins[w].at[mq], dst_ref=outs[w].at[2 * peer[0] + peer[1]], send_sem=send_sems.at[w, k],
                recv_sem=recv_sems.at[w, k], device_id=peer, device_id_type=MESH)

        local = [pltpu.make_async_copy(ins[w].at[mq], outs[w].at[mq], loc_sems.at[w, 0]) for w in range(n)]
        for cp in local:
            cp.start()
        sends = [copy(w, k) for k in range(3) for w in range(n)]
        for cp in sends:
            cp.start()
        for k in range(3):
            for w in range(n):
                arrival(w, k).wait_recv()
        for cp in sends:
            cp.wait_send()
        for cp in local:
            cp.wait()

    out_shape = [jax.ShapeDtypeStruct(a.shape, a.dtype) for a in arrays]
    return _comm_call(name, body, arrays, out_shape, 3)


def _adamw(name, parts, w, m, v):
    R, C = w.shape
    n_parts = parts.shape[0]
    tr = R if R <= 256 else (256 if C <= 512 else 128)
    bc1 = 1.0 - ADAM_B1 ** ADAM_STEP
    bc2 = 1.0 - ADAM_B2 ** ADAM_STEP

    def f(p, wv, mv, vv):
        g = p[0].astype(f32)
        for i in range(1, n_parts):
            g = g + p[i].astype(f32)
        m2 = ADAM_B1 * mv + (1.0 - ADAM_B1) * g
        v2 = ADAM_B2 * vv + (1.0 - ADAM_B2) * jnp.square(g)
        delta = -ADAM_LR * ((m2 / bc1) / (jnp.sqrt(v2 / bc2) + ADAM_EPS) + ADAM_WD * wv)
        return g, delta, m2, v2

    blk = ((tr, C), lambda r, z: (r, 0))
    return _ew(name, f, [(parts, (n_parts, tr, C), lambda r, z: (0, r, 0)), (w, *blk), (m, *blk), (v, *blk)],
               [((R, C), f32, *blk, None)] * 4, (R // tr, 1))


_WEIGHTS = ["norm_mix_g", "norm_mlp_g", "mlp_w1", "mlp_w2", "gdn_w_in", "gdn_conv_w", "gdn_a_log", "gdn_dt_bias",
            "gdn_o_norm_g", "gdn_w_out", "s5_w_in", "s5_lam_re", "s5_lam_im", "s5_log_dt", "s5_b_re", "s5_b_im",
            "s5_c_re", "s5_c_im", "s5_d", "s5_w_out", "m2_w_in", "m2_conv_w", "m2_conv_b", "m2_dt_bias", "m2_a_log",
            "m2_d", "m2_norm_g", "m2_w_out", "final_norm_g"]
_SHARDED = ["mlp_w1", "mlp_w2", "gdn_w_in", "gdn_w_out", "s5_w_in", "s5_w_out", "m2_w_in", "m2_w_out",
            "gdn_conv_w", "m2_conv_w", "m2_conv_b", "m2_norm_g"]
_MATRICES = _SHARDED[:8]
_REPLICATED = [n for n in _WEIGHTS if n not in _SHARDED]
_GDN_IN, _M2_IN = 4112, 6176
_LAYER_KIND = (0, 1, 2, 0)


def _as2d(a):
    return a.reshape(-1, a.shape[-1])


def _cols_from_shards(g, width):
    return g.transpose(1, 0, 2).reshape(g.shape[1], width)


def _cols_to_shards(a, width):
    return a[:, :width].reshape(a.shape[0], N_DEV, width // N_DEV).transpose(1, 0, 2)


def kernel(x, norm_mix_g, norm_mlp_g, mlp_w1, mlp_w2, gdn_w_in, gdn_conv_w, gdn_a_log, gdn_dt_bias, gdn_o_norm_g, gdn_w_out, s5_w_in, s5_lam_re, s5_lam_im, s5_log_dt, s5_b_re, s5_b_im, s5_c_re, s5_c_im, s5_d, s5_w_out, m2_w_in, m2_conv_w, m2_conv_b, m2_dt_bias, m2_a_log, m2_d, m2_norm_g, m2_w_out, final_norm_g, loss_target, m_norm_mix_g, m_norm_mlp_g, m_mlp_w1, m_mlp_w2, m_gdn_w_in, m_gdn_conv_w, m_gdn_a_log, m_gdn_dt_bias, m_gdn_o_norm_g, m_gdn_w_out, m_s5_w_in, m_s5_lam_re, m_s5_lam_im, m_s5_log_dt, m_s5_b_re, m_s5_b_im, m_s5_c_re, m_s5_c_im, m_s5_d, m_s5_w_out, m_m2_w_in, m_m2_conv_w, m_m2_conv_b, m_m2_dt_bias, m_m2_a_log, m_m2_d, m_m2_norm_g, m_m2_w_out, m_final_norm_g, v_norm_mix_g, v_norm_mlp_g, v_mlp_w1, v_mlp_w2, v_gdn_w_in, v_gdn_conv_w, v_gdn_a_log, v_gdn_dt_bias, v_gdn_o_norm_g, v_gdn_w_out, v_s5_w_in, v_s5_lam_re, v_s5_lam_im, v_s5_log_dt, v_s5_b_re, v_s5_b_im, v_s5_c_re, v_s5_c_im, v_s5_d, v_s5_w_out, v_m2_w_in, v_m2_conv_w, v_m2_conv_b, v_m2_dt_bias, v_m2_a_log, v_m2_d, v_m2_norm_g, v_m2_w_out, v_final_norm_g):
    args = locals()
    W = {n: args[n] for n in _WEIGHTS}
    MOM = {n: args["m_" + n] for n in _WEIGHTS}
    VAR = {n: args["v_" + n] for n in _WEIGHTS}
    h = x[0]
    target = loss_target[0]
    L, D = h.shape

    sends = [W[n].astype(bf16) if n in _MATRICES else _as2d(W[n]) for n in _SHARDED]
    G = dict(zip(_SHARDED, _gather("gather_weights", sends)))
    w1g, w2g = G["mlp_w1"], G["mlp_w2"]
    gdn_in = [jnp.pad(_cols_from_shards(G["gdn_w_in"][:, j], _GDN_IN), ((0, 0), (0, GDN_EXT - _GDN_IN))) for j in range(2)]
    gdn_out = [G["gdn_w_out"][:, j].reshape(D, D) for j in range(2)]
    gdn_conv = [_cols_from_shards(G["gdn_conv_w"][:, 4 * j:4 * j + 4], 3 * D) for j in range(2)]
    s5_in = G["s5_w_in"].reshape(D, D)
    s5_out_g = G["s5_w_out"]
    m2_in = jnp.pad(_cols_from_shards(G["m2_w_in"][:, 0], _M2_IN), ((0, 0), (0, M2_EXT - _M2_IN)))
    m2_out = G["m2_w_out"].reshape(M2_INNER, D)
    m2_conv = _cols_from_shards(G["m2_conv_w"], 2 * M2_INNER)
    m2_cb = _cols_from_shards(G["m2_conv_b"], 2 * M2_INNER)
    m2_ng = _cols_from_shards(G["m2_norm_g"], M2_INNER)

    def mixer_fwd(i, hv):
        kind, j = _LAYER_KIND[i], i // 3
        gn = norm_mix_g[i].reshape(1, D)
        if kind == 0:
            return _gdn_fwd(hv, gn, gdn_in[j], gdn_conv[j], gdn_a_log[j], gdn_dt_bias[j], gdn_o_norm_g[j], gdn_out[j])
        if kind == 1:
            return _s5_fwd(hv, gn, s5_in, s5_lam_re[0], s5_lam_im[0], s5_log_dt[0], s5_b_re[0], s5_b_im[0],
                           s5_c_re[0], s5_c_im[0], s5_d[0], s5_out_g)
        return _m2_fwd(hv, gn, m2_in, m2_conv, m2_cb, m2_dt_bias[0], m2_a_log[0], m2_d[0], m2_ng, m2_out)

    def mixer_bwd(i, dh, hv, sv):
        kind, j = _LAYER_KIND[i], i // 3
        gn = norm_mix_g[i].reshape(1, D)
        if kind == 0:
            return _gdn_bwd(dh, hv, gn, gdn_in[j], gdn_conv[j], gdn_out[j], sv)
        if kind == 1:
            return _s5_bwd(dh, hv, gn, s5_in, s5_out_g, sv)
        return _m2_bwd(dh, hv, gn, m2_in, m2_conv, m2_cb, m2_out, sv)

    tape = []
    for i in range(4):
        h_mid, sv = mixer_fwd(i, h)
        h_next, hn, h1 = _mlp_fwd(h_mid, norm_mlp_g[i].reshape(1, D), w1g, w2g, i)
        tape.append((h, sv, h_mid, hn, h1))
        h = h_next
    loss_row, dh, d_final = _loss_head(h, final_norm_g.reshape(1, D), target)
    loss = lax.psum(loss_row[0, 0], ("x", "y", "c"))

    dw1 = lax.empty((N_DEV, 4, D, D_FF // N_DEV), f32)
    dw2 = lax.empty((N_DEV, 4, D_FF // N_DEV, D), f32)
    d_mix, d_mlp, mg = [None] * 4, [None] * 4, [None] * 4
    for i in reversed(range(4)):
        h_in, sv, h_mid, hn, h1 = tape[i]
        dh, d_mlp[i], dw1, dw2 = _mlp_bwd(dh, h_mid, norm_mlp_g[i].reshape(1, D), hn, h1, w1g, w2g, i, dw1, dw2)
        dh, mg[i] = mixer_bwd(i, dh, h_in, sv)
        d_mix[i] = mg[i]["norm"]
    grad_x = dh.reshape(1, L, D)
    ga, gb_, s5g, m2g = mg[0], mg[3], mg[1], mg[2]

    full = {
        "mlp_w1": dw1, "mlp_w2": dw2,
        "gdn_w_in": jnp.stack([_cols_to_shards(g["w_ext"], _GDN_IN) for g in (ga, gb_)], axis=1),
        "gdn_w_out": jnp.stack([g["w_out"].reshape(N_DEV, D // N_DEV, D) for g in (ga, gb_)], axis=1),
        "s5_w_in": s5g["w_in"].reshape(N_DEV, 1, D // N_DEV, D), "s5_w_out": s5g["w_out"],
        "m2_w_in": _cols_to_shards(m2g["w_ext"], _M2_IN)[:, None],
        "m2_w_out": m2g["w_out"].reshape(N_DEV, 1, M2_INNER // N_DEV, D),
        "gdn_conv_w": jnp.concatenate([_cols_to_shards(g["conv_w"], 3 * D) for g in (ga, gb_)], axis=1),
        "m2_conv_w": _cols_to_shards(m2g["conv_w"], 2 * M2_INNER),
        "m2_conv_b": _cols_to_shards(m2g["conv_b"], 2 * M2_INNER),
        "m2_norm_g": _cols_to_shards(m2g["norm_g"], M2_INNER),
    }
    sends = [full[n].reshape((N_DEV,) + _as2d(W[n]).shape) for n in _SHARDED]
    theirs, mine = _scatter_pair("scatter_pair", sends)
    chip_sums = [_pair_add("pair_add_" + n, mi, th, bf16 if n in _MATRICES else f32)
                 for n, mi, th in zip(_SHARDED, mine, theirs)]
    parts = dict(zip(_SHARDED, _scatter_chips("scatter_chips", chip_sums)))

    rep = {
        "norm_mix_g": jnp.concatenate(d_mix, axis=0), "norm_mlp_g": jnp.concatenate(d_mlp, axis=0),
        "gdn_a_log": jnp.stack([ga["a_log"], gb_["a_log"]]), "gdn_dt_bias": jnp.stack([ga["dt_bias"], gb_["dt_bias"]]),
        "gdn_o_norm_g": jnp.stack([ga["o_norm_g"], gb_["o_norm_g"]]),
        "s5_lam_re": s5g["lam_re"], "s5_lam_im": s5g["lam_im"], "s5_log_dt": s5g["log_dt"], "s5_b_re": s5g["b_re"],
        "s5_b_im": s5g["b_im"], "s5_c_re": s5g["c_re"], "s5_c_im": s5g["c_im"], "s5_d": s5g["d"],
        "m2_dt_bias": m2g["dt_bias"], "m2_a_log": m2g["a_log"], "m2_d": m2g["d"], "final_norm_g": d_final,
    }

    def pack(d):
        flat = jnp.concatenate([d[n].reshape(-1).astype(f32) for n in _REPLICATED])
        return jnp.pad(flat, (0, -flat.shape[0] % (256 * 128))).reshape(-1, 128)

    (rep_parts,) = _gather("gather_small_grads", [pack(rep)])

    res = {}
    for n in _SHARDED:
        w2d = _as2d(W[n])
        out = _adamw("adamw_" + n, parts[n], w2d, _as2d(MOM[n]), _as2d(VAR[n]))
        res[n] = [o.reshape(W[n].shape) for o in out]
    out = _adamw("adamw_replicated", rep_parts, pack(W), pack(MOM), pack(VAR))
    off = 0
    for n in _REPLICATED:
        size = W[n].size
        res[n] = [o.reshape(-1)[off:off + size].reshape(W[n].shape) for o in out]
        off += size

    return (loss, grad_x, *[res[n][0] for n in _WEIGHTS], *[res[n][1] for n in _WEIGHTS],
            *[res[n][2] for n in _WEIGHTS], *[res[n][3] for n in _WEIGHTS])
```

```python
import functools

import jax
import jax.numpy as jnp
from jax import lax
from jax.experimental import pallas as pl
from jax.experimental.pallas import tpu as pltpu

f32 = jnp.float32
bf16 = jnp.bfloat16
HI = lax.Precision.HIGHEST
MESH = pl.DeviceIdType.MESH

N_DEV = 8
D_MODEL = 1024
D_FF = 4096
CHUNK = 64
RMS_EPS = 1e-6
GDN_HEADS = 8
GDN_HB = 4
GDN_EXT = 4224
S5_STATE = 64
M2_INNER = 2048
M2_EXT = 6272
M2_HEADS = 32
M2_GB = 2
VMEM_LIMIT_BYTES = 56 * 1024 * 1024

ADAM_LR, ADAM_B1, ADAM_B2, ADAM_EPS, ADAM_WD, ADAM_STEP = 0.001, 0.9, 0.999, 1e-08, 0.01, 10

_NN = ((1,), (0,))
_NT = ((1,), (1,))
_TN = ((0,), (0,))


def _dot(a, b, dims=_NN):
    return lax.dot_general(a, b, (dims, ((), ())), precision=HI, preferred_element_type=f32)


def _dotb(a, b, dims=_NN):
    return lax.dot_general(a.astype(bf16), b.astype(bf16), (dims, ((), ())), preferred_element_type=f32)


def _bdot(p, q, dims):
    return lax.dot_general(p, q, (dims, ((), ())), preferred_element_type=f32)


def _pieces(x, n):
    out = []
    for _ in range(n - 1):
        p = x.astype(bf16)
        out.append(p)
        x = x - p.astype(f32)
    return out + [x.astype(bf16)]


def _dot01_raw(mask, b, dims=_NN, mask_first=True):
    m = mask.astype(bf16)
    p = _pieces(b, 3)
    if mask_first:
        return _bdot(m, p[0], dims) + (_bdot(m, p[1], dims) + _bdot(m, p[2], dims))
    return _bdot(p[0], m, dims) + (_bdot(p[1], m, dims) + _bdot(p[2], m, dims))


@jax.custom_vjp
def _dot01_nn(mask, b):
    return _dot01_raw(mask, b, _NN)


@jax.custom_vjp
def _dot01_nt(mask, b):
    return _dot01_raw(mask, b, _NT)


_dot01_nn.defvjp(lambda m, b: (_dot01_raw(m, b, _NN), m),
                 lambda m, ct: (jnp.zeros_like(m), _dot01_raw(m, ct, _TN, mask_first=True)))
_dot01_nt.defvjp(lambda m, b: (_dot01_raw(m, b, _NT), m),
                 lambda m, ct: (jnp.zeros_like(m), _dot01_raw(m, ct, _TN, mask_first=False)))


def _dot01_vjp(mask, b, dims=_NN):
    return _dot01_nn(mask, b) if dims == _NN else _dot01_nt(mask, b)


def _dot3_raw(a, b, dims=_NN):
    (ah, al), (bh, bl) = _pieces(a, 2), _pieces(b, 2)
    return _bdot(ah, bh, dims) + (_bdot(ah, bl, dims) + _bdot(al, bh, dims))


@jax.custom_vjp
def _dot3_vjp(a, b):
    return _dot3_raw(a, b)


_dot3_vjp.defvjp(lambda a, b: (_dot3_raw(a, b), (a, b)),
                 lambda res, ct: (_dot3_raw(ct, res[1], _NT), _dot3_raw(res[0], ct, _TN)))


class _Dots:
    def __init__(self, dot3, dot01):
        self.dot3, self.dot01 = dot3, dot01


_PLAIN_DOTS = _Dots(_dot3_raw, _dot01_raw)
_VJP_DOTS = _Dots(_dot3_vjp, _dot01_vjp)


def _iota(shape, dim):
    return lax.broadcasted_iota(jnp.int32, shape, dim)


def _params(n_grid):
    return pltpu.CompilerParams(dimension_semantics=("arbitrary",) * n_grid, vmem_limit_bytes=VMEM_LIMIT_BYTES)


def _row_tile(n_rows):
    return min(512, n_rows)


def _mm_rows(n_rows):
    return min(1024, n_rows)


def _col_tile(n, cap=1024):
    best = 128
    for t in range(128, cap + 1, 128):
        if n % t == 0:
            best = t
    return best


def _mm(name, a, b, *, dims, grid, a_spec, b_spec, out_shape, out_spec, aux=(), a_fn=None, epi_fn=None,
        acc_shape, out_init=None):
    nk = grid[2]
    n_aux = len(aux)
    kinds = [x[2] for x in aux]

    def body(*refs):
        a_ref, b_ref = refs[0], refs[1]
        aux_refs = refs[2:2 + n_aux]
        pos = 2 + n_aux + (1 if out_init is not None else 0)
        o_ref, acc_ref = refs[pos], refs[pos + 1]
        k = pl.program_id(2)

        @pl.when(k == 0)
        def _():
            acc_ref[...] = jnp.zeros_like(acc_ref)

        av = a_ref[...]
        if a_fn is not None:
            av = a_fn(av, *[r[...] for r, kd in zip(aux_refs, kinds) if kd == "a"])
        acc_ref[...] += lax.dot_general(av.astype(bf16), b_ref[...].astype(bf16), (dims, ((), ())),
                                        preferred_element_type=f32)

        @pl.when(k == nk - 1)
        def _():
            r = acc_ref[...]
            if epi_fn is not None:
                r = epi_fn(r, *[x[...] for x, kd in zip(aux_refs, kinds) if kd == "e"])
            o_ref[...] = r.astype(o_ref.dtype)

    in_specs = [a_spec, b_spec] + [x[1] for x in aux]
    args = [a, b] + [x[0] for x in aux]
    aliases = {}
    if out_init is not None:
        in_specs.append(pl.BlockSpec(memory_space=pl.ANY))
        args.append(out_init)
        aliases = {len(args) - 1: 0}
    return pl.pallas_call(
        body, name=name, grid=grid, in_specs=in_specs, out_specs=out_spec, out_shape=out_shape,
        scratch_shapes=[pltpu.VMEM(acc_shape, f32)], input_output_aliases=aliases,
        compiler_params=_params(3),
    )(*args)


def _ew(name, f, ins, outs, grid):
    n_in = len(ins)
    modes = [o[4] for o in outs]

    def body(*refs):
        vals = [r[...] for r in refs[:n_in]]
        res = f(*vals)
        if not isinstance(res, (tuple, list)):
            res = (res,)
        for r, o_ref, mode in zip(res, refs[n_in:], modes):
            if mode is None:
                o_ref[...] = r.astype(o_ref.dtype)
                continue
            first = pl.program_id(1) == 0
            if mode == "all":
                first = jnp.logical_and(first, pl.program_id(0) == 0)

            @pl.when(first)
            def _(r=r, o_ref=o_ref):
                o_ref[...] = r.astype(o_ref.dtype)

            @pl.when(jnp.logical_not(first))
            def _(r=r, o_ref=o_ref):
                o_ref[...] += r.astype(o_ref.dtype)

    res = pl.pallas_call(
        body, name=name, grid=grid,
        in_specs=[pl.BlockSpec(blk, im) for _, blk, im in ins],
        out_specs=[pl.BlockSpec(o[2], o[3]) for o in outs],
        out_shape=[jax.ShapeDtypeStruct(o[0], o[1]) for o in outs],
        compiler_params=_params(2),
    )(*[a for a, _, _ in ins])
    return res


def _vjp_fn(f, n_primal):
    def g(*args):
        _, vjp = jax.vjp(f, *args[:n_primal])
        cts = args[n_primal:]
        return vjp(cts[0] if len(cts) == 1 else tuple(cts))
    return g


def _scan_fwd(name, step, n_state, state_shape, cins, ins, outs, n_units, n_chunks):
    n_c, n_in, n_out = len(cins), len(ins), len(outs)

    def body(*refs):
        c_refs = refs[:n_c]
        in_refs = refs[n_c:n_c + n_in]
        out_refs = refs[n_c + n_in:n_c + n_in + n_out]
        saved = refs[n_c + n_in + n_out:n_c + n_in + n_out + n_state]
        st = refs[n_c + n_in + n_out + n_state:]

        @pl.when(pl.program_id(1) == 0)
        def _():
            for s in st:
                s[...] = jnp.zeros_like(s)

        cur = [s[...] for s in st]
        for sv, s in zip(saved, cur):
            sv[...] = s
        new, res = step(cur, [r[...] for r in c_refs], [r[...] for r in in_refs], _PLAIN_DOTS)
        for s, n in zip(st, new):
            s[...] = n
        for o, r in zip(out_refs, res):
            o[...] = r

    sshape = (n_units, n_chunks) + state_shape
    sblock = (None, None) + state_shape
    nz = len(state_shape)
    res = pl.pallas_call(
        body, name=name, grid=(n_units, n_chunks),
        in_specs=[pl.BlockSpec(e[1], e[2]) for e in cins + ins],
        out_specs=[pl.BlockSpec(o[1], o[2]) for o in outs]
        + [pl.BlockSpec(sblock, lambda u, c: (u, c) + (0,) * nz)] * n_state,
        out_shape=[jax.ShapeDtypeStruct(o[0], f32) for o in outs]
        + [jax.ShapeDtypeStruct(sshape, f32)] * n_state,
        scratch_shapes=[pltpu.VMEM(state_shape, f32)] * n_state,
        compiler_params=_params(2),
    )(*[e[0] for e in cins + ins])
    return res[:n_out], res[n_out:]


def _scan_bwd(name, step, n_state, state_shape, cins, ins, saved, douts, n_units, n_chunks):
    n_c, n_in, n_do = len(cins), len(ins), len(douts)

    def flip(im):
        return lambda u, c: im(u, n_chunks - 1 - c)

    def body(*refs):
        p = 0
        c_refs = refs[p:p + n_c]; p += n_c
        in_refs = refs[p:p + n_in]; p += n_in
        sv_refs = refs[p:p + n_state]; p += n_state
        do_refs = refs[p:p + n_do]; p += n_do
        dc_refs = refs[p:p + n_c]; p += n_c
        di_refs = refs[p:p + n_in]; p += n_in
        dst = refs[p:]
        first = pl.program_id(1) == 0

        @pl.when(first)
        def _():
            for s in dst:
                s[...] = jnp.zeros_like(s)

        def fn(states, consts, vals):
            new, res = step(states, consts, vals, _VJP_DOTS)
            return tuple(new), tuple(res)

        prim = ([r[...] for r in sv_refs], [r[...] for r in c_refs], [r[...] for r in in_refs])
        _, vjp = jax.vjp(fn, *prim)
        d_states, d_consts, d_vals = vjp((tuple(s[...] for s in dst), tuple(r[...] for r in do_refs)))
        for s, g in zip(dst, d_states):
            s[...] = g
        for o, g in zip(di_refs, d_vals):
            o[...] = g
        for o, g in zip(dc_refs, d_consts):
            @pl.when(first)
            def _(o=o, g=g):
                o[...] = g

            @pl.when(jnp.logical_not(first))
            def _(o=o, g=g):
                o[...] += g

    nz = len(state_shape)
    sblock = (None, None) + state_shape
    def gshape(e):
        return e[3] if len(e) == 5 else e[0].shape

    def gmap(e):
        return e[4] if len(e) == 5 else e[2]

    in_specs = ([pl.BlockSpec(e[1], e[2]) for e in cins]
                + [pl.BlockSpec(e[1], flip(e[2])) for e in ins]
                + [pl.BlockSpec(sblock, lambda u, c: (u, n_chunks - 1 - c) + (0,) * nz)] * n_state
                + [pl.BlockSpec(e[1], flip(e[2])) for e in douts])
    out_specs = ([pl.BlockSpec(e[1], e[2]) for e in cins]
                 + [pl.BlockSpec(e[1], flip(gmap(e))) for e in ins])
    out_shape = [jax.ShapeDtypeStruct(gshape(e), f32) for e in cins + ins]
    res = pl.pallas_call(
        body, name=name, grid=(n_units, n_chunks), in_specs=in_specs, out_specs=out_specs, out_shape=out_shape,
        scratch_shapes=[pltpu.VMEM(state_shape, f32)] * n_state,
        compiler_params=_params(2),
    )(*([e[0] for e in cins + ins] + list(saved) + [e[0] for e in douts]))
    return res[:n_c], res[n_c:]


def _rms(x, g):
    return x * lax.rsqrt(jnp.mean(x * x, axis=-1, keepdims=True) + RMS_EPS) * g


def _rows(tm, width):
    return (tm, width), lambda r, z: (r, 0)


def _const(shape):
    return shape, lambda r, z: (0,) * len(shape)


def _rms_fwd(name, h, g):
    L, D = h.shape
    tm = _row_tile(L)
    return _ew(name, _rms, [(h, *_rows(tm, D)), (g, *_const((1, D)))],
               [((L, D), f32, *_rows(tm, D), None)], (L // tm, 1))[0]


def _rms_bwd(name, h, g, d_hn, d_res):
    L, D = h.shape
    tm = _row_tile(L)

    def f(hv, gv, dv, rv):
        dh, dg = _vjp_fn(_rms, 2)(hv, gv, dv)
        return dh + rv, dg

    return _ew(name, f, [(h, *_rows(tm, D)), (g, *_const((1, D))), (d_hn, *_rows(tm, D)), (d_res, *_rows(tm, D))],
               [((L, D), f32, *_rows(tm, D), None), ((1, D), f32, *_const((1, D)), "all")], (L // tm, 1))


def _loss_head(h, g, target):
    L, D = h.shape
    tm = _row_tile(L)

    def f(hv, gv, tv):
        def lf(a, b):
            e = jnp.square(_rms(a, b) - tv)
            return (0.5 / D) * jnp.sum(jnp.sum(e, axis=1, keepdims=True), axis=0, keepdims=True)

        val, vjp = jax.vjp(lf, hv, gv)
        dh, dg = vjp(jnp.ones((1, 1), f32))
        return jnp.broadcast_to(val, (1, 128)), dh, dg

    return _ew("loss_head", f, [(h, *_rows(tm, D)), (g, *_const((1, D))), (target, *_rows(tm, D))],
               [((1, 128), f32, *_const((1, 128)), "all"), ((L, D), f32, *_rows(tm, D), None),
                ((1, D), f32, *_const((1, D)), "all")], (L // tm, 1))


def _sqrelu(x):
    return jnp.square(jnp.maximum(x, 0.0))


def _mm_plain(name, a, b, dims, *, a_fn=None, epi_fn=None, aux=()):
    if dims == _NN:
        (M, K), N = a.shape, b.shape[1]
    elif dims == _NT:
        (M, K), N = a.shape, b.shape[0]
    else:
        (K, M), N = a.shape, b.shape[1]
    tm = _mm_rows(M)
    tn = _col_tile(N, 1536)
    tk = _col_tile(K)
    if dims == _TN:
        tk = min(512, K)
        a_spec = pl.BlockSpec((tk, tm), lambda i, j, k: (k, i))
        b_spec = pl.BlockSpec((tk, tn), lambda i, j, k: (k, j))
        a_aux = pl.BlockSpec((tk, tm), lambda i, j, k: (k, i))
    elif dims == _NT:
        a_spec = pl.BlockSpec((tm, tk), lambda i, j, k: (i, k))
        b_spec = pl.BlockSpec((tn, tk), lambda i, j, k: (j, k))
        a_aux = pl.BlockSpec((tm, tk), lambda i, j, k: (i, k))
    else:
        a_spec = pl.BlockSpec((tm, tk), lambda i, j, k: (i, k))
        b_spec = pl.BlockSpec((tk, tn), lambda i, j, k: (k, j))
        a_aux = pl.BlockSpec((tm, tk), lambda i, j, k: (i, k))
    e_aux = pl.BlockSpec((tm, tn), lambda i, j, k: (i, j))
    aux_full = [(x, a_aux if kd == "a" else e_aux, kd) for x, kd in aux]
    return _mm(name, a, b, dims=dims, grid=(M // tm, N // tn, K // tk), a_spec=a_spec, b_spec=b_spec,
               out_shape=jax.ShapeDtypeStruct((M, N), f32), out_spec=pl.BlockSpec((tm, tn), lambda i, j, k: (i, j)),
               aux=aux_full, a_fn=a_fn, epi_fn=epi_fn, acc_shape=(tm, tn))


def _mlp_fwd(h, g, w1g, w2g, layer):
    L, D = h.shape
    tm = _mm_rows(L)
    fs = D_FF // N_DEV
    hn = _rms_fwd("mlp_norm", h, g)
    h1 = _mm("mlp_up", hn, w1g, dims=_NN, grid=(L // tm, N_DEV, 1),
             a_spec=pl.BlockSpec((tm, D), lambda i, j, k: (i, 0)),
             b_spec=pl.BlockSpec((None, None, D, fs), lambda i, j, k: (j, layer, 0, 0)),
             out_shape=jax.ShapeDtypeStruct((L, D_FF), f32), out_spec=pl.BlockSpec((tm, fs), lambda i, j, k: (i, j)),
             acc_shape=(tm, fs))
    tn = D
    h_out = _mm("mlp_down", h1, w2g, dims=_NN, grid=(L // tm, D // tn, N_DEV),
                a_spec=pl.BlockSpec((tm, fs), lambda i, j, k: (i, k)),
                b_spec=pl.BlockSpec((None, None, fs, tn), lambda i, j, k: (k, layer, 0, j)),
                out_shape=jax.ShapeDtypeStruct((L, D), f32), out_spec=pl.BlockSpec((tm, tn), lambda i, j, k: (i, j)),
                aux=[(h, pl.BlockSpec((tm, tn), lambda i, j, k: (i, j)), "e")],
                a_fn=_sqrelu, epi_fn=lambda acc, res: acc + res, acc_shape=(tm, tn))
    return h_out, hn, h1


def _mlp_bwd(dh, h, g, hn, h1, w1g, w2g, layer, dw1_buf, dw2_buf):
    L, D = h.shape
    tm = _mm_rows(L)
    fs = D_FF // N_DEV
    tk = _mm_rows(L)
    dh1 = _mm("mlp_down_dx", dh, w2g, dims=_NT, grid=(L // tm, N_DEV, 1),
              a_spec=pl.BlockSpec((tm, D), lambda i, j, k: (i, 0)),
              b_spec=pl.BlockSpec((None, None, fs, D), lambda i, j, k: (j, layer, 0, 0)),
              out_shape=jax.ShapeDtypeStruct((L, D_FF), f32), out_spec=pl.BlockSpec((tm, fs), lambda i, j, k: (i, j)),
              aux=[(h1, pl.BlockSpec((tm, fs), lambda i, j, k: (i, j)), "e")],
              epi_fn=lambda acc, pre: acc * (2.0 * jnp.maximum(pre, 0.0)), acc_shape=(tm, fs))
    dw2_buf = _mm("mlp_down_dw", h1, dh, dims=_TN, grid=(N_DEV, 1, L // tk),
                  a_spec=pl.BlockSpec((tk, fs), lambda i, j, k: (k, i)),
                  b_spec=pl.BlockSpec((tk, D), lambda i, j, k: (k, 0)),
                  out_shape=jax.ShapeDtypeStruct(dw2_buf.shape, f32),
                  out_spec=pl.BlockSpec((None, None, fs, D), lambda i, j, k: (i, layer, 0, 0)),
                  a_fn=_sqrelu, acc_shape=(fs, D), out_init=dw2_buf)
    tr = D
    dw1_buf = _mm("mlp_up_dw", hn, dh1, dims=_TN, grid=(D // tr, N_DEV, L // tk),
                  a_spec=pl.BlockSpec((tk, tr), lambda i, j, k: (k, i)),
                  b_spec=pl.BlockSpec((tk, fs), lambda i, j, k: (k, j)),
                  out_shape=jax.ShapeDtypeStruct(dw1_buf.shape, f32),
                  out_spec=pl.BlockSpec((None, None, tr, fs), lambda i, j, k: (j, layer, i, 0)),
                  acc_shape=(tr, fs), out_init=dw1_buf)
    tn = D
    dhn = _mm("mlp_up_dx", dh1, w1g, dims=_NT, grid=(L // tm, D // tn, N_DEV),
              a_spec=pl.BlockSpec((tm, fs), lambda i, j, k: (i, k)),
              b_spec=pl.BlockSpec((None, None, tn, fs), lambda i, j, k: (k, layer, j, 0)),
              out_shape=jax.ShapeDtypeStruct((L, D), f32), out_spec=pl.BlockSpec((tm, tn), lambda i, j, k: (i, j)),
              acc_shape=(tm, tn))
    dh_in, dg = _rms_bwd("mlp_norm_bwd", h, g, dhn, dh)
    return dh_in, dg, dw1_buf, dw2_buf


def _shift_dn(x, s, row):
    return x if s == 0 else jnp.where(row >= s, pltpu.roll(x, s, 0), 0.0)


def _shift_up(x, s, row):
    n = x.shape[0]
    return x if s == 0 else jnp.where(row < n - s, pltpu.roll(x, n - s, 0), 0.0)


def _conv_pre(x, w, b, row):
    c = jnp.broadcast_to(b, x.shape)
    for j in range(4):
        c = c + w[j:j + 1, :] * _shift_dn(x, 3 - j, row)
    return c


def _conv_fwd(name, x_arr, blk_off, w, b):
    L = x_arr.shape[0]
    C = w.shape[1]

    def f(x, wv, bv):
        c = _conv_pre(x, wv, bv, _iota(x.shape, 0))
        return c * jax.nn.sigmoid(c)

    return _ew(name, f, [(x_arr, (L, 128), lambda j, z: (0, blk_off + j)), (w, (4, 128), lambda j, z: (0, j)),
                         (b, (1, 128), lambda j, z: (0, j))],
               [((L, C), f32, (L, 128), lambda j, z: (0, j), None)], (C // 128, 1))[0]


def _conv_bwd(name, x_arr, blk_off, w, b, dy):
    L = x_arr.shape[0]
    C = w.shape[1]

    def f(x, wv, bv, g):
        row = _iota(x.shape, 0)
        c = _conv_pre(x, wv, bv, row)
        s = jax.nn.sigmoid(c)
        dc = g * (s * (1.0 + c * (1.0 - s)))
        dx = jnp.zeros_like(x)
        dw = jnp.zeros((4, 128), f32)
        r4 = _iota((4, 128), 0)
        for j in range(4):
            dx = dx + wv[j:j + 1, :] * _shift_up(dc, 3 - j, row)
            dwj = jnp.sum(dc * _shift_dn(x, 3 - j, row), axis=0, keepdims=True)
            dw = dw + jnp.where(r4 == j, jnp.broadcast_to(dwj, (4, 128)), 0.0)
        return dx, dw, jnp.sum(dc, axis=0, keepdims=True)

    return _ew(name, f, [(x_arr, (L, 128), lambda j, z: (0, blk_off + j)), (w, (4, 128), lambda j, z: (0, j)),
                         (b, (1, 128), lambda j, z: (0, j)), (dy, (L, 128), lambda j, z: (0, j))],
               [((L, C), f32, (L, 128), lambda j, z: (0, j), None), ((4, C), f32, (4, 128), lambda j, z: (0, j), None),
                ((1, C), f32, (1, 128), lambda j, z: (0, j), None)], (C // 128, 1))


def _l2norm(t):
    return t * lax.rsqrt(jnp.sum(t * t, axis=-1, keepdims=True) + 1e-6)


def _gdn_act(cq, ck, ab, alog, dtb):
    h = pl.program_id(1)
    qn = _l2norm(cq) * (128.0 ** -0.5)
    kn = _l2norm(ck)
    lane = _iota(ab.shape, 1)
    a_raw = jnp.sum(jnp.where(lane == h, ab, 0.0), axis=1, keepdims=True)
    b_raw = jnp.sum(jnp.where(lane == h + GDN_HEADS, ab, 0.0), axis=1, keepdims=True)
    lane1 = _iota(alog.shape, 1)
    al = jnp.sum(jnp.where(lane1 == h, alog, 0.0), axis=1, keepdims=True)
    db = jnp.sum(jnp.where(lane1 == h, dtb, 0.0), axis=1, keepdims=True)
    g = -jnp.exp(al) * jax.nn.softplus(a_raw + db)
    beta = jax.nn.sigmoid(b_raw)
    return qn, kn, jnp.broadcast_to(g, cq.shape), jnp.broadcast_to(beta, cq.shape)


def _each(f, *lists):
    return [f(*a) for a in zip(*lists)]


def _gdn_chunk(states, consts, vals, dots):
    S = list(states)
    cut = [slice(128 * i, 128 * i + 128) for i in range(len(S))]
    q, k, v, gb, bb = ([t[:, c] for c in cut] for t in vals)
    C = vals[0].shape[0]
    row, col = _iota((C, C), 0), _iota((C, C), 1)
    causal, strict = row >= col, row > col
    ltri = causal.astype(f32)
    eye = (row == col).astype(f32)
    e0 = (_iota((C, 128), 1) == 0).astype(f32)
    last = _iota((C, 1), 0) == C - 1
    Gb = _each(lambda g: dots.dot01(ltri, g), gb)
    Gc = _each(lambda g: jnp.mean(g, axis=1, keepdims=True), Gb)
    Gr = _each(lambda g: dots.dot01(e0, g, _NT), Gb)
    bc = _each(lambda b: jnp.mean(b, axis=1, keepdims=True), bb)
    decay = _each(lambda gc, gr: jnp.where(causal, jnp.exp(jnp.where(causal, gc - gr, 0.0)), 0.0), Gc, Gr)
    kk = _each(lambda a: _dotb(a, a, _NT), k)
    A = _each(lambda b, x, d: jnp.where(strict, b * x * d, 0.0), bc, kk, decay)
    M = _each(lambda a: eye - a, A)
    P = _each(lambda a: dots.dot3(a, a), A)
    for it in range(5):
        M = _each(lambda m, p: m + dots.dot3(m, p), M, P)
        if it < 4:
            P = _each(lambda p: dots.dot3(p, p), P)
    eG = _each(jnp.exp, Gc)
    u = _each(lambda m, x, b: _dotb(m, x * b), M, v, bc)
    w = _each(lambda m, x, b, e: _dotb(m, x * (b * e)), M, k, bc, eG)
    qk = _each(lambda a, b, d: _dotb(a, b, _NT) * d, q, k, decay)
    g_last = _each(lambda gc: jnp.sum(jnp.where(last, gc, 0.0), axis=0, keepdims=True), Gc)
    v_new = _each(lambda a, b, s: a - _dotb(b, s), u, w, S)
    o = _each(lambda a, e, s, b, x: _dotb(a * e, s) + _dotb(b, x), q, eG, S, qk, v_new)
    S_new = _each(lambda gl, s, a, gc, x: jnp.exp(gl) * s + _dotb(a * jnp.exp(gl - gc), x, _TN), g_last, S, k, Gc, v_new)
    return S_new, [jnp.concatenate(o, axis=1)]


def _gdn_post(o, gate, g):
    return _rms(o, g) * (gate * jax.nn.sigmoid(gate))


def _pad_row(v):
    return jnp.pad(v.astype(f32), (0, 128 - v.shape[0])).reshape(1, 128)


def _gdn_fwd(h, g_norm, w_ext, conv_w, a_log, dt_bias, o_norm_g, w_out):
    L, D = h.shape
    tm = _row_tile(L)
    nc = L // CHUNK
    H = GDN_HEADS
    hn = _rms_fwd("mix_norm", h, g_norm)
    proj = _mm_plain("gdn_in", hn, w_ext, _NN)
    zb = jnp.zeros((1, 3 * D), f32)
    cq = _conv_fwd("gdn_conv", proj, 0, conv_w, zb)
    alog, dtb = _pad_row(a_log), _pad_row(dt_bias)
    act_ins = [(cq, (tm, 128), lambda r, hh: (r, hh)), (cq, (tm, 128), lambda r, hh: (r, H + hh)),
               (proj, (tm, 128), lambda r, hh: (r, 4 * H)), (alog, (1, 128), lambda r, hh: (0, 0)),
               (dtb, (1, 128), lambda r, hh: (0, 0))]
    qn, kn, gb, bb = _ew("gdn_act", _gdn_act, act_ins,
                         [((L, D), f32, (tm, 128), lambda r, hh: (r, hh), None)] * 4, (L // tm, H))
    cblk = (CHUNK, 128 * GDN_HB)
    core_ins = [(qn, cblk, lambda u, c: (c, u)), (kn, cblk, lambda u, c: (c, u)),
                (cq, cblk, lambda u, c: (c, 2 * H // GDN_HB + u), (L, D), lambda u, c: (c, u)),
                (gb, cblk, lambda u, c: (c, u)), (bb, cblk, lambda u, c: (c, u))]
    (o,), saved_s = _scan_fwd("gdn_core", _gdn_chunk, GDN_HB, (128, 128), [], core_ins,
                              [((L, D), cblk, lambda u, c: (c, u))], H // GDN_HB, nc)
    on = o_norm_g.reshape(1, 128)
    post_ins = [(o, (tm, 128), lambda r, hh: (r, hh)), (proj, (tm, 128), lambda r, hh: (r, 3 * H + hh)),
                (on, (1, 128), lambda r, hh: (0, 0))]
    y = _ew("gdn_post", _gdn_post, post_ins, [((L, D), f32, (tm, 128), lambda r, hh: (r, hh), None)], (L // tm, H))[0]
    h_out = _mm_plain("gdn_out", y, w_out, _NN, epi_fn=lambda acc, res: acc + res, aux=[(h, "e")])
    saved = dict(hn=hn, proj=proj, cq=cq, alog=alog, dtb=dtb, act_ins=act_ins, core_ins=core_ins, saved_s=saved_s,
                 post_ins=post_ins, y=y, zb=zb)
    return h_out, saved


def _gdn_bwd(dh, h, g_norm, w_ext, conv_w, w_out, sv):
    L, D = h.shape
    tm = _row_tile(L)
    nc = L // CHUNK
    H = GDN_HEADS
    dy = _mm_plain("gdn_out_dx", dh, w_out, _NT)
    dw_out = _mm_plain("gdn_out_dw", sv["y"], dh, _TN)
    hd = ((L, D), f32, (tm, 128), lambda r, hh: (r, hh), None)
    d_o, d_gate, d_on = _ew("gdn_post_bwd", _vjp_fn(_gdn_post, 3),
                            sv["post_ins"] + [(dy, (tm, 128), lambda r, hh: (r, hh))],
                            [hd, hd, ((1, 128), f32, (1, 128), lambda r, hh: (0, 0), "all")], (L // tm, H))
    cblk = (CHUNK, 128 * GDN_HB)
    _, (dqn, dkn, dv, dgb, dbb) = _scan_bwd("gdn_core_bwd", _gdn_chunk, GDN_HB, (128, 128), [], sv["core_ins"],
                                            sv["saved_s"], [(d_o, cblk, lambda u, c: (c, u))], H // GDN_HB, nc)
    cts = [(t, (tm, 128), lambda r, hh: (r, hh)) for t in (dqn, dkn, dgb, dbb)]
    row128 = ((1, 128), f32, (1, 128), lambda r, hh: (0, 0), "all")
    d_cq, d_ck, d_ab, d_alog, d_dtb = _ew(
        "gdn_act_bwd", _vjp_fn(_gdn_act, 5), sv["act_ins"] + cts,
        [hd, hd, ((L, 128), f32, (tm, 128), lambda r, hh: (r, 0), "inner"), row128, row128], (L // tm, H))
    d_conv_out = jnp.concatenate([d_cq, d_ck, dv], axis=1)
    d_conv_in, d_conv_w, _ = _conv_bwd("gdn_conv_bwd", sv["proj"], 0, conv_w, sv["zb"], d_conv_out)
    d_proj = jnp.concatenate([d_conv_in, d_gate, d_ab], axis=1)
    dw_ext = _mm_plain("gdn_in_dw", sv["hn"], d_proj, _TN)
    dhn = _mm_plain("gdn_in_dx", d_proj, w_ext, _NT)
    dh_in, dg = _rms_bwd("mix_norm_bwd", h, g_norm, dhn, dh)
    grads = dict(norm=dg, w_ext=dw_ext, conv_w=d_conv_w, a_log=d_alog[0, :H], dt_bias=d_dtb[0, :H],
                 o_norm_g=d_on[0], w_out=dw_out)
    return dh_in, grads


def _expand_lanes(row, width, rep):
    sel = ((_iota((128, width), 1) // rep) == _iota((128, width), 0)).astype(f32)
    return jnp.mean(_dot(jnp.broadcast_to(row, (8, 128)), sel), axis=0, keepdims=True)


def _s5_params(lre, lim, ldt, wbr, wbi):
    dt = jnp.exp(_expand_lanes(ldt, 512, S5_STATE))
    mag = jnp.exp(lre * dt)
    ang = lim * dt
    abr, abi = mag * jnp.cos(ang), mag * jnp.sin(ang)
    nr = abr - 1.0
    den = lre * lre + lim * lim
    cr = (nr * lre + abi * lim) / den
    ci = (abi * lre - nr * lim) / den
    return abr, abi, cr * wbr - ci * wbi, cr * wbi + ci * wbr


def _s5_scan(name, xr, xi, ar, ai, rev, want_prev):
    L, W = xr.shape
    nb = L // 8
    n_out = 4 if want_prev else 2

    def body(xr_ref, xi_ref, ar_ref, ai_ref, *outs):
        a_r = ar_ref[...]
        a_i = -ai_ref[...] if rev else ai_ref[...]

        def cm(p, q):
            return p[0] * q[0] - p[1] * q[1], p[0] * q[1] + p[1] * q[0]

        a1 = (a_r, a_i)
        a2 = cm(a1, a1)
        a3 = cm(a2, a1)
        a4 = cm(a2, a2)
        pw = [a1, a2, a3, a4, cm(a4, a1), cm(a4, a2), cm(a4, a3), cm(a4, a4)]
        row = _iota((8, 128), 0)
        tab_r = jnp.zeros((8, 128), f32)
        tab_i = jnp.zeros((8, 128), f32)
        for t in range(8):
            idx = 7 - t if rev else t
            tab_r = jnp.where(row == idx, jnp.broadcast_to(pw[t][0], (8, 128)), tab_r)
            tab_i = jnp.where(row == idx, jnp.broadcast_to(pw[t][1], (8, 128)), tab_i)
        lv = [(d, jnp.broadcast_to(p[0], (8, 128)), jnp.broadcast_to(p[1], (8, 128))) for d, p in ((1, a1), (2, a2), (4, a4))]

        def step(i, carry):
            cr, ci = carry
            blk = nb - 1 - i if rev else i
            r0 = pl.multiple_of(blk * 8, 8)
            x_r = xr_ref[pl.ds(r0, 8), :]
            x_i = xi_ref[pl.ds(r0, 8), :]
            for d, p_r, p_i in lv:
                if rev:
                    s_r = jnp.where(row < 8 - d, pltpu.roll(x_r, 8 - d, 0), 0.0)
                    s_i = jnp.where(row < 8 - d, pltpu.roll(x_i, 8 - d, 0), 0.0)
                else:
                    s_r = jnp.where(row >= d, pltpu.roll(x_r, d, 0), 0.0)
                    s_i = jnp.where(row >= d, pltpu.roll(x_i, d, 0), 0.0)
                x_r, x_i = x_r + p_r * s_r - p_i * s_i, x_i + p_r * s_i + p_i * s_r
            x_r, x_i = x_r + tab_r * cr - tab_i * ci, x_i + tab_r * ci + tab_i * cr
            outs[0][pl.ds(r0, 8), :] = x_r
            outs[1][pl.ds(r0, 8), :] = x_i
            if want_prev:
                outs[2][pl.ds(r0, 8), :] = jnp.where(row >= 1, pltpu.roll(x_r, 1, 0), cr)
                outs[3][pl.ds(r0, 8), :] = jnp.where(row >= 1, pltpu.roll(x_i, 1, 0), ci)
            e = 0 if rev else 7
            return jnp.broadcast_to(x_r[e:e + 1, :], (8, 128)), jnp.broadcast_to(x_i[e:e + 1, :], (8, 128))

        lax.fori_loop(0, nb, step, (jnp.zeros((8, 128), f32), jnp.zeros((8, 128), f32)))

    col = pl.BlockSpec((L, 128), lambda q, z: (0, q))
    aspec = pl.BlockSpec((None, 1, 128), lambda q, z: (q // 4, 0, q % 4))
    return pl.pallas_call(
        body, name=name, grid=(W // 128, 1), in_specs=[col, col, aspec, aspec], out_specs=[col] * n_out,
        out_shape=[jax.ShapeDtypeStruct((L, W), f32)] * n_out, compiler_params=_params(2),
    )(xr, xi, ar, ai)


def _blockdiag(t, n_in, n_out):
    t4 = t.reshape(8, 8, n_in, n_out)
    return jnp.einsum("jaio,ab->jaibo", t4, jnp.eye(8, dtype=t.dtype)).reshape(8, 8 * n_in, 8 * n_out)


def _blockdiag_t(w, n_in, n_out):
    w5 = w.reshape(8, 8, n_in, 8, n_out)
    return jnp.einsum("jaibo,ab->jaio", w5, jnp.eye(8, dtype=w.dtype)).reshape(64, n_in, n_out)


def _glu(ag, h):
    n = ag.shape[1] // 2
    return h + ag[:, :n] * jax.nn.sigmoid(ag[:, n:])


def _s5_fwd(h, g_norm, w_in, lam_re, lam_im, log_dt, b_re, b_im, c_re, c_im, d_skip, w_out_g):
    L, D = h.shape
    tm, te = _mm_rows(L), _row_tile(L)
    W = 8 * 512
    hn = _rms_fwd("mix_norm", h, g_norm)
    u = _mm_plain("s5_in", hn, w_in, _NN)
    lre, lim = lam_re.reshape(8, 1, 512), lam_im.reshape(8, 1, 512)
    ldt = jnp.pad(log_dt.reshape(8, 1, 8), ((0, 0), (0, 0), (0, 120)))
    wbr = _blockdiag(b_re.transpose(0, 2, 1), 16, 64)
    wbi = _blockdiag(b_im.transpose(0, 2, 1), 16, 64)
    wcr = _blockdiag(c_re.transpose(0, 2, 1), 64, 16)
    wci = _blockdiag(c_im.transpose(0, 2, 1), 64, 16)
    jb = lambda shape: (shape, lambda j, z: (j, 0, 0))
    par_ins = [(lre, *jb((None, 1, 512))), (lim, *jb((None, 1, 512))), (ldt, *jb((None, 1, 128))),
               (wbr, *jb((None, 128, 512))), (wbi, *jb((None, 128, 512)))]
    abr, abi, bbr, bbi = _ew("s5_params", _s5_params, par_ins,
                             [((8, 1, 512), f32, *jb((None, 1, 512)), None)] * 2
                             + [((8, 128, 512), f32, *jb((None, 128, 512)), None)] * 2, (8, 1))

    def bu(name, wb):
        return _mm(name, u, wb, dims=_NN, grid=(L // tm, 8, 1),
                   a_spec=pl.BlockSpec((tm, 128), lambda i, j, k: (i, j)),
                   b_spec=pl.BlockSpec((None, 128, 512), lambda i, j, k: (j, 0, 0)),
                   out_shape=jax.ShapeDtypeStruct((L, W), f32), out_spec=pl.BlockSpec((tm, 512), lambda i, j, k: (i, j)),
                   acc_shape=(tm, 512))

    bur, bui = bu("s5_bu", bbr), bu("s5_bu", bbi)
    sr, si, pr, pi = _s5_scan("s5_scan", bur, bui, abr, abi, False, True)
    d_row = d_skip.reshape(1, D)
    cspec = dict(a_spec=pl.BlockSpec((tm, 512), lambda i, j, k: (i, j)),
                 b_spec=pl.BlockSpec((None, 512, 128), lambda i, j, k: (j, 0, 0)),
                 out_shape=jax.ShapeDtypeStruct((L, D), f32), out_spec=pl.BlockSpec((tm, 128), lambda i, j, k: (i, j)),
                 acc_shape=(tm, 128))
    e128 = pl.BlockSpec((tm, 128), lambda i, j, k: (i, j))
    pre1 = _mm("s5_c_re", sr, wcr, dims=_NN, grid=(L // tm, 8, 1), **cspec)
    pre = _mm("s5_c_im", si, wci, dims=_NN, grid=(L // tm, 8, 1),
              aux=[(pre1, e128, "e"), (u, e128, "e"), (d_row, pl.BlockSpec((1, 128), lambda i, j, k: (0, j)), "e")],
              epi_fn=lambda acc, p1, uu, dd: p1 - acc + dd * uu, **cspec)
    ws = D // N_DEV * 2
    ag = _mm("s5_out", pre, w_out_g, dims=_NN, grid=(L // tm, N_DEV, 1),
             a_spec=pl.BlockSpec((tm, D), lambda i, j, k: (i, 0)),
             b_spec=pl.BlockSpec((None, None, D, ws), lambda i, j, k: (j, 0, 0, 0)),
             out_shape=jax.ShapeDtypeStruct((L, 2 * D), f32), out_spec=pl.BlockSpec((tm, ws), lambda i, j, k: (i, j)),
             a_fn=jax.nn.gelu, acc_shape=(tm, ws))
    h_out = _ew("s5_glu", _glu, [(ag, *_rows(te, 2 * D)), (h, *_rows(te, D))],
                [((L, D), f32, *_rows(te, D), None)], (L // te, 1))[0]
    saved = dict(hn=hn, u=u, par_ins=par_ins, abr=abr, abi=abi, bbr=bbr, bbi=bbi, wcr=wcr, wci=wci, sr=sr, si=si,
                 pr=pr, pi=pi, pre=pre, ag=ag, d_row=d_row)
    return h_out, saved


def _s5_bwd(dh, h, g_norm, w_in, w_out_g, sv):
    L, D = h.shape
    tm, te = _mm_rows(L), _row_tile(L)
    tk = min(512, L)
    W = 8 * 512
    ws = D // N_DEV * 2
    u, pre, d_row = sv["u"], sv["pre"], sv["d_row"]
    d_ag = _ew("s5_glu_bwd", lambda ag, hv, g: _vjp_fn(_glu, 2)(ag, hv, g)[0],
               [(sv["ag"], *_rows(te, 2 * D)), (h, *_rows(te, D)), (dh, *_rows(te, D))],
               [((L, 2 * D), f32, *_rows(te, 2 * D), None)], (L // te, 1))[0]
    tr = 512
    dw_out = _mm("s5_out_dw", pre, d_ag, dims=_TN, grid=(D // tr, N_DEV, L // tk),
                 a_spec=pl.BlockSpec((tk, tr), lambda i, j, k: (k, i)),
                 b_spec=pl.BlockSpec((tk, ws), lambda i, j, k: (k, j)),
                 out_shape=jax.ShapeDtypeStruct((N_DEV, 1, D, ws), f32),
                 out_spec=pl.BlockSpec((None, None, tr, ws), lambda i, j, k: (j, 0, i, 0)),
                 a_fn=jax.nn.gelu, acc_shape=(tr, ws))
    tn = 512
    dpre = _mm("s5_out_dx", d_ag, w_out_g, dims=_NT, grid=(L // tm, D // tn, N_DEV),
               a_spec=pl.BlockSpec((tm, ws), lambda i, j, k: (i, k)),
               b_spec=pl.BlockSpec((None, None, tn, ws), lambda i, j, k: (k, 0, j, 0)),
               out_shape=jax.ShapeDtypeStruct((L, D), f32), out_spec=pl.BlockSpec((tm, tn), lambda i, j, k: (i, j)),
               aux=[(pre, pl.BlockSpec((tm, tn), lambda i, j, k: (i, j)), "e")],
               epi_fn=lambda acc, p: _vjp_fn(jax.nn.gelu, 1)(p, acc)[0], acc_shape=(tm, tn))
    d_d = _ew("s5_dskip", lambda a, b: jnp.sum(a * b, axis=0, keepdims=True),
              [(dpre, *_rows(te, D)), (u, *_rows(te, D))], [((1, D), f32, *_const((1, D)), "all")], (L // te, 1))[0]
    neg = lambda acc: -acc
    dsspec = dict(dims=_NT, grid=(L // tm, 8, 1), a_spec=pl.BlockSpec((tm, 128), lambda i, j, k: (i, j)),
                  b_spec=pl.BlockSpec((None, 512, 128), lambda i, j, k: (j, 0, 0)),
                  out_shape=jax.ShapeDtypeStruct((L, W), f32), out_spec=pl.BlockSpec((tm, 512), lambda i, j, k: (i, j)),
                  acc_shape=(tm, 512))
    dsr = _mm("s5_c_re_dx", dpre, sv["wcr"], **dsspec)
    dsi = _mm("s5_c_im_dx", dpre, sv["wci"], epi_fn=neg, **dsspec)
    dwspec = dict(dims=_TN, grid=(8, 1, L // tk), a_spec=pl.BlockSpec((tk, 512), lambda i, j, k: (k, i)),
                  b_spec=pl.BlockSpec((tk, 128), lambda i, j, k: (k, i)),
                  out_shape=jax.ShapeDtypeStruct((8, 512, 128), f32),
                  out_spec=pl.BlockSpec((None, 512, 128), lambda i, j, k: (i, 0, 0)), acc_shape=(512, 128))
    dwcr = _mm("s5_c_re_dw", sv["sr"], dpre, **dwspec)
    dwci = _mm("s5_c_im_dw", sv["si"], dpre, epi_fn=neg, **dwspec)
    lr, li = _s5_scan("s5_scan_bwd", dsr, dsi, sv["abr"], sv["abi"], True, False)

    def da(lrv, liv, prv, piv):
        return (jnp.sum(lrv * prv + liv * piv, axis=0, keepdims=True),
                jnp.sum(liv * prv - lrv * piv, axis=0, keepdims=True))

    sblk = ((te, 512), lambda j, r: (r, j))
    dabr, dabi = _ew("s5_dlam", da, [(lr, *sblk), (li, *sblk), (sv["pr"], *sblk), (sv["pi"], *sblk)],
                     [((8, 1, 512), f32, (None, 1, 512), lambda j, r: (j, 0, 0), "inner")] * 2, (8, L // te))
    dbspec = dict(dims=_TN, grid=(8, 1, L // tk), a_spec=pl.BlockSpec((tk, 128), lambda i, j, k: (k, i)),
                  b_spec=pl.BlockSpec((tk, 512), lambda i, j, k: (k, i)),
                  out_shape=jax.ShapeDtypeStruct((8, 128, 512), f32),
                  out_spec=pl.BlockSpec((None, 128, 512), lambda i, j, k: (i, 0, 0)), acc_shape=(128, 512))
    dbbr = _mm("s5_bu_dw", u, lr, **dbspec)
    dbbi = _mm("s5_bu_dw", u, li, **dbspec)
    duspec = dict(dims=_NT, grid=(L // tm, 8, 1), a_spec=pl.BlockSpec((tm, 512), lambda i, j, k: (i, j)),
                  b_spec=pl.BlockSpec((None, 128, 512), lambda i, j, k: (j, 0, 0)),
                  out_shape=jax.ShapeDtypeStruct((L, D), f32), out_spec=pl.BlockSpec((tm, 128), lambda i, j, k: (i, j)),
                  acc_shape=(tm, 128))
    e128 = pl.BlockSpec((tm, 128), lambda i, j, k: (i, j))
    du1 = _mm("s5_bu_dx_re", lr, sv["bbr"], **duspec)
    du = _mm("s5_bu_dx_im", li, sv["bbi"],
             aux=[(du1, e128, "e"), (dpre, e128, "e"), (d_row, pl.BlockSpec((1, 128), lambda i, j, k: (0, j)), "e")],
             epi_fn=lambda acc, d1, dp, dd: acc + d1 + dp * dd, **duspec)
    jb = lambda shape: (shape, lambda j, z: (j, 0, 0))
    cts = [(dabr, *jb((None, 1, 512))), (dabi, *jb((None, 1, 512))), (dbbr, *jb((None, 128, 512))),
           (dbbi, *jb((None, 128, 512)))]
    dlre, dlim, dldt, dwbr, dwbi = _ew(
        "s5_params_bwd", _vjp_fn(_s5_params, 5), sv["par_ins"] + cts,
        [((8, 1, 512), f32, *jb((None, 1, 512)), None)] * 2 + [((8, 1, 128), f32, *jb((None, 1, 128)), None)]
        + [((8, 128, 512), f32, *jb((None, 128, 512)), None)] * 2, (8, 1))
    dw_in = _mm_plain("s5_in_dw", sv["hn"], du, _TN)
    dhn = _mm_plain("s5_in_dx", du, w_in, _NT)
    dh_in, dg = _rms_bwd("mix_norm_bwd", h, g_norm, dhn, dh)
    grads = dict(norm=dg, w_in=dw_in, lam_re=dlre.reshape(64, 64), lam_im=dlim.reshape(64, 64),
                 log_dt=dldt[:, 0, :8].reshape(64),
                 b_re=_blockdiag_t(dwbr, 16, 64).transpose(0, 2, 1), b_im=_blockdiag_t(dwbi, 16, 64).transpose(0, 2, 1),
                 c_re=_blockdiag_t(dwcr, 64, 16).transpose(0, 2, 1), c_im=_blockdiag_t(dwci, 64, 16).transpose(0, 2, 1),
                 d=d_d[0], w_out=dw_out)
    return dh_in, grads


def _m2_act(dt_raw, dtbias, alog):
    dt = jax.nn.softplus(dt_raw + dtbias)
    da = dt * (-jnp.exp(alog))
    sel = ((_iota((128, M2_INNER), 1) // 64) == _iota((128, M2_INNER), 0)).astype(f32)
    return _dot(dt, sel), _dot(da, sel)


def _m2_dexp(d):
    return _expand_lanes(d, M2_INNER, 64)


def _ssd_chunk(states, consts, vals, dots):
    (dsk,) = consts
    S = list(states)
    n = len(S)
    cut = [slice(128 * i, 128 * i + 128) for i in range(n)]
    x, dtb, dab = ([t[:, c] for c in cut] for t in vals[:3])
    dsk = [dsk[:, c] for c in cut]
    B = [vals[3][:, cut[i // 2]] for i in range(n)]
    Cm = [vals[4][:, cut[i // 2]] for i in range(n)]
    C = vals[0].shape[0]
    row, col = _iota((C, C), 0), _iota((C, C), 1)
    causal = row >= col
    ltri = causal.astype(f32)
    lane = _iota((C, 128), 1)
    last = _iota((C, 128), 0) == C - 1
    eye128 = _iota((128, 128), 0) == _iota((128, 128), 1)
    head = [jnp.logical_and(lane >= 64 * hh, lane < 64 * hh + 64) for hh in range(2)]
    pick = [(lane == 64 * hh).astype(f32) for hh in range(2)]
    xdt = _each(lambda a, b: a * b, x, dtb)
    cb = _each(lambda c, b: _dotb(c, b, _NT), Cm[::2], B[::2])
    cum = _each(lambda a: dots.dot01(ltri, a), dab)
    clast = _each(lambda a: jnp.sum(jnp.where(last, a, 0.0), axis=0, keepdims=True), cum)
    st = _each(lambda a, cl, cu, b: _dotb(a * jnp.exp(cl - cu), b, _TN), xdt, clast, cum, B)
    y = _each(lambda c, s, cu: _dotb(c, s, _NT) * jnp.exp(cu), Cm, S, cum)
    for hh in range(2):
        ccol = _each(lambda cu: jnp.sum(jnp.where(head[hh], cu, 0.0), axis=1, keepdims=True) * (1.0 / 64), cum)
        crow = _each(lambda cu: dots.dot01(pick[hh], cu, _NT), cum)
        lm = _each(lambda a, b: jnp.where(causal, jnp.exp(jnp.where(causal, a - b, 0.0)), 0.0), ccol, crow)
        y = [y[i] + _dotb(cb[i // 2] * lm[i], jnp.where(head[hh], xdt[i], 0.0)) for i in range(n)]
    cdcol = _each(lambda cl: jnp.sum(jnp.where(eye128, jnp.broadcast_to(jnp.exp(cl), (128, 128)), 0.0),
                                     axis=1, keepdims=True), clast)
    S_new = _each(lambda c, s, t: c * s + t, cdcol, S, st)
    out = _each(lambda a, d, b: a + d * b, y, dsk, x)
    return S_new, [jnp.concatenate(out, axis=1)]


def _m2_post(yc, z, ng):
    return _rms(yc * (z * jax.nn.sigmoid(z)), ng)


def _m2_fwd(h, g_norm, w_ext, conv_w, conv_b, dt_bias, a_log, d_skip, norm_g, w_out):
    L, D = h.shape
    tm = _row_tile(L)
    nc = L // CHUNK
    NI = M2_INNER
    hn = _rms_fwd("mix_norm", h, g_norm)
    proj = _mm_plain("m2_in", hn, w_ext, _NN)
    xbc = _conv_fwd("m2_conv", proj, NI // 128, conv_w, conv_b)
    dtb_row, alog_row, d_pad = _pad_row(dt_bias), _pad_row(a_log), _pad_row(d_skip)
    act_ins = [(proj, (tm, 128), lambda r, z: (r, 3 * NI // 128)), (dtb_row, *_const((1, 128))),
               (alog_row, *_const((1, 128)))]
    dtb, dab = _ew("m2_act", _m2_act, act_ins, [((L, NI), f32, *_rows(tm, NI), None)] * 2, (L // tm, 1))
    dsk = _ew("m2_dexp", _m2_dexp, [(d_pad, *_const((1, 128)))], [((1, NI), f32, *_const((1, NI)), None)], (1, 1))[0]
    GB = M2_GB
    x_blk, bc_blk = (CHUNK, 256 * GB), (CHUNK, 128 * GB)
    cins = [(dsk, (1, 256 * GB), lambda u, c: (0, u))]
    core_ins = [(xbc, x_blk, lambda u, c: (c, u), (L, NI), lambda u, c: (c, u)),
                (dtb, x_blk, lambda u, c: (c, u)), (dab, x_blk, lambda u, c: (c, u)),
                (xbc, bc_blk, lambda u, c: (c, 16 // GB + u), (L, D), lambda u, c: (c, u)),
                (xbc, bc_blk, lambda u, c: (c, 24 // GB + u), (L, D), lambda u, c: (c, u))]
    (yc,), saved_s = _scan_fwd("m2_core", _ssd_chunk, 2 * GB, (128, 128), cins, core_ins,
                               [((L, NI), x_blk, lambda u, c: (c, u))], 8 // GB, nc)
    gblk = ((tm, 256), lambda g, r: (r, g))
    post_ins = [(yc, *gblk), (proj, *gblk), (norm_g, (1, 256), lambda g, r: (0, g))]
    yn = _ew("m2_post", _m2_post, post_ins, [((L, NI), f32, *gblk, None)], (8, L // tm))[0]
    h_out = _mm_plain("m2_out", yn, w_out, _NN, epi_fn=lambda acc, res: acc + res, aux=[(h, "e")])
    saved = dict(hn=hn, proj=proj, act_ins=act_ins, d_pad=d_pad, cins=cins, core_ins=core_ins, saved_s=saved_s,
                 post_ins=post_ins, yn=yn)
    return h_out, saved


def _m2_bwd(dh, h, g_norm, w_ext, conv_w, conv_b, w_out, sv):
    L, D = h.shape
    tm = _row_tile(L)
    nc = L // CHUNK
    NI = M2_INNER
    dyn = _mm_plain("m2_out_dx", dh, w_out, _NT)
    dw_out = _mm_plain("m2_out_dw", sv["yn"], dh, _TN)
    gblk = ((tm, 256), lambda g, r: (r, g))
    d_yc, d_z, d_ng = _ew("m2_post_bwd", _vjp_fn(_m2_post, 3), sv["post_ins"] + [(dyn, *gblk)],
                          [((L, NI), f32, *gblk, None)] * 2 + [((1, NI), f32, (1, 256), lambda g, r: (0, g), "inner")],
                          (8, L // tm))
    (d_dsk,), (dx, d_dtb, d_dab, dB, dC) = _scan_bwd(
        "m2_core_bwd", _ssd_chunk, 2 * M2_GB, (128, 128), sv["cins"], sv["core_ins"], sv["saved_s"],
        [(d_yc, (CHUNK, 256 * M2_GB), lambda u, c: (c, u))], 8 // M2_GB, nc)
    row128 = ((1, 128), f32, *_const((1, 128)), "all")
    d_dt_raw, d_dtbias, d_alog = _ew(
        "m2_act_bwd", _vjp_fn(_m2_act, 3), sv["act_ins"] + [(d_dtb, *_rows(tm, NI)), (d_dab, *_rows(tm, NI))],
        [((L, 128), f32, *_rows(tm, 128), None), row128, row128], (L // tm, 1))
    d_d = _ew("m2_dexp_bwd", _vjp_fn(_m2_dexp, 1), [(sv["d_pad"], *_const((1, 128))), (d_dsk, *_const((1, NI)))],
              [((1, 128), f32, *_const((1, 128)), None)], (1, 1))[0]
    d_conv_out = jnp.concatenate([dx, dB, dC], axis=1)
    d_conv_in, d_conv_w, d_conv_b = _conv_bwd("m2_conv_bwd", sv["proj"], NI // 128, conv_w, conv_b, d_conv_out)
    d_proj = jnp.concatenate([d_z, d_conv_in, d_dt_raw], axis=1)
    dw_ext = _mm_plain("m2_in_dw", sv["hn"], d_proj, _TN)
    dhn = _mm_plain("m2_in_dx", d_proj, w_ext, _NT)
    dh_in, dg = _rms_bwd("mix_norm_bwd", h, g_norm, dhn, dh)
    grads = dict(norm=dg, w_ext=dw_ext, conv_w=d_conv_w, conv_b=d_conv_b, dt_bias=d_dtbias[0, :M2_HEADS],
                 a_log=d_alog[0, :M2_HEADS], d=d_d[0, :M2_HEADS], norm_g=d_ng, w_out=dw_out)
    return dh_in, grads


def _mesh_pos():
    return lax.axis_index("x"), lax.axis_index("y"), lax.axis_index("c")


def _flip(pos, p):
    x, y, c = pos
    return (1 - x if p & 4 else x, 1 - y if p & 2 else y, 1 - c if p & 1 else c)


def _index(pos):
    return 4 * pos[0] + 2 * pos[1] + pos[2]


def _comm_call(name, body, arrays, out_shape, n_sem):
    n = len(arrays)
    hbm = pl.BlockSpec(memory_space=pl.ANY)
    return pl.pallas_call(
        body, name=name, in_specs=[hbm] * n, out_specs=[hbm] * len(out_shape), out_shape=out_shape,
        scratch_shapes=[pltpu.SemaphoreType.DMA((n, n_sem)), pltpu.SemaphoreType.DMA((n, n_sem)),
                        pltpu.SemaphoreType.DMA((n, 4))],
    )(*arrays)


def _gather(name, arrays):
    n = len(arrays)

    def body(*refs):
        ins, outs = refs[:n], refs[n:2 * n]
        send_sems, recv_sems, loc_sems = refs[2 * n:]
        me = _mesh_pos()
        c = me[2]
        sib = _flip(me, 1)
        chips = [_flip(me, 4), _flip(me, 2), _flip(me, 6)]

        def copy(w, k, block, to, src=None):
            slab = outs[w].at[_index(block)]
            return pltpu.make_async_remote_copy(
                src_ref=slab if src is None else src, dst_ref=slab, send_sem=send_sems.at[w, k],
                recv_sem=recv_sems.at[w, k], device_id=to, device_id_type=MESH)

        local = [pltpu.make_async_copy(ins[w], outs[w].at[_index(me)], loc_sems.at[w, 0]) for w in range(n)]
        for cp in local:
            cp.start()
        first = [copy(w, 0, me, sib, src=ins[w]) for w in range(n)]
        first += [copy(w, 1 + j, me, chip, src=ins[w]) for j, chip in enumerate(chips) for w in range(n)]
        for cp in first:
            cp.start()
        passed = []
        for j, chip in enumerate(chips):
            for w in range(n):
                copy(w, 1 + j, chip, me).wait_recv()
                fwd = copy(w, 4 + j, chip, sib)
                fwd.start()
                passed.append(fwd)
        for w in range(n):
            copy(w, 0, sib, me).wait_recv()
        for j, chip in enumerate(chips):
            for w in range(n):
                copy(w, 4 + j, (chip[0], chip[1], 1 - c), me).wait_recv()
        for cp in first + passed:
            cp.wait_send()
        for cp in local:
            cp.wait()

    out_shape = [jax.ShapeDtypeStruct((N_DEV,) + a.shape, a.dtype) for a in arrays]
    return _comm_call(name, body, arrays, out_shape, N_DEV - 1)


def _scatter_pair(name, arrays):
    n = len(arrays)

    def body(*refs):
        ins, outs = refs[:n], refs[n:2 * n]
        send_sems, recv_sems, _ = refs[2 * n:]
        me = _mesh_pos()
        c = me[2]
        sib = _flip(me, 1)

        def copy(w, q):
            return pltpu.make_async_remote_copy(
                src_ref=ins[w].at[2 * q + 1 - c], dst_ref=outs[w].at[q], send_sem=send_sems.at[w, q],
                recv_sem=recv_sems.at[w, q], device_id=sib, device_id_type=MESH)

        cps = [copy(w, q) for q in range(4) for w in range(n)]
        for cp in cps:
            cp.start()
        for cp in cps:
            cp.wait()

    out_shape = [jax.ShapeDtypeStruct((4,) + a.shape[1:], a.dtype) for a in arrays]
    return _comm_call(name, body, arrays, out_shape, 4)


def _pair_add(name, full, theirs, core, dtype):
    _, R, C = theirs.shape
    tr = R if R <= 256 else (256 if C <= 512 else 128)

    def body(core_ref, mine_ref, theirs_ref, o_ref):
        o_ref[...] = (mine_ref[...] + theirs_ref[...]).astype(o_ref.dtype)

    blk = pl.BlockSpec((4, tr, C), lambda r, cr: (0, r, 0))
    grid_spec = pltpu.PrefetchScalarGridSpec(
        num_scalar_prefetch=1, grid=(R // tr,),
        in_specs=[pl.BlockSpec((4, None, tr, C), lambda r, cr: (0, cr[0], r, 0)), blk], out_specs=blk)
    return pl.pallas_call(
        body, name=name, grid_spec=grid_spec, out_shape=jax.ShapeDtypeStruct((4, R, C), dtype),
        compiler_params=_params(1),
    )(core.reshape(1).astype(jnp.int32), full.reshape(4, 2, R, C), theirs)


def _scatter_chips(name, arrays):
    n = len(arrays)

    def body(*refs):
        ins, outs = refs[:n], refs[n:2 * n]
        send_sems, recv_sems, loc_sems = refs[2 * n:]
        me = _mesh_pos()
        mq = 2 * me[0] + me[1]
        peers = [_flip(me, 4), _flip(me, 2), _flip(me, 6)]

        def copy(w, k):
            peer = peers[k]
            return pltpu.make_async_remote_copy(
                src_ref=ins[w].at[2 * peer[0] + peer[1]], dst_ref=outs[w].at[mq], send_sem=send_sems.at[w, k],
                recv_sem=recv_sems.at[w, k], device_id=peer, device_id_type=MESH)

        def arrival(w, k):
            peer = peers[k]
            return pltpu.make_async_remote_copy(
                src_ref=ins[w].at[mq], dst_ref=outs[w].at[2 * peer[0] + peer[1]], send_sem=send_sems.at[w, k],
                recv_sem=recv_sems.at[w, k], device_id=peer, device_id_type=MESH)

        local = [pltpu.make_async_copy(ins[w].at[mq], outs[w].at[mq], loc_sems.at[w, 0]) for w in range(n)]
        for cp in local:
            cp.start()
        sends = [copy(w, k) for k in range(3) for w in range(n)]
        for cp in sends:
            cp.start()
        for k in range(3):
            for w in range(n):
                arrival(w, k).wait_recv()
        for cp in sends:
            cp.wait_send()
        for cp in local:
            cp.wait()

    out_shape = [jax.ShapeDtypeStruct(a.shape, a.dtype) for a in arrays]
    return _comm_call(name, body, arrays, out_shape, 3)


def _adamw(name, parts, w, m, v):
    R, C = w.shape
    n_parts = parts.shape[0]
    tr = R if R <= 256 else (256 if C <= 512 else 128)
    bc1 = 1.0 - ADAM_B1 ** ADAM_STEP
    bc2 = 1.0 - ADAM_B2 ** ADAM_STEP

    def f(p, wv, mv, vv):
        g = p[0].astype(f32)
        for i in range(1, n_parts):
            g = g + p[i].astype(f32)
        m2 = ADAM_B1 * mv + (1.0 - ADAM_B1) * g
        v2 = ADAM_B2 * vv + (1.0 - ADAM_B2) * jnp.square(g)
        delta = -ADAM_LR * ((m2 / bc1) / (jnp.sqrt(v2 / bc2) + ADAM_EPS) + ADAM_WD * wv)
        return g, delta, m2, v2

    blk = ((tr, C), lambda r, z: (r, 0))
    return _ew(name, f, [(parts, (n_parts, tr, C), lambda r, z: (0, r, 0)), (w, *blk), (m, *blk), (v, *blk)],
               [((R, C), f32, *blk, None)] * 4, (R // tr, 1))


_WEIGHTS = ["norm_mix_g", "norm_mlp_g", "mlp_w1", "mlp_w2", "gdn_w_in", "gdn_conv_w", "gdn_a_log", "gdn_dt_bias",
            "gdn_o_norm_g", "gdn_w_out", "s5_w_in", "s5_lam_re", "s5_lam_im", "s5_log_dt", "s5_b_re", "s5_b_im",
            "s5_c_re", "s5_c_im", "s5_d", "s5_w_out", "m2_w_in", "m2_conv_w", "m2_conv_b", "m2_dt_bias", "m2_a_log",
            "m2_d", "m2_norm_g", "m2_w_out", "final_norm_g"]
_SHARDED = ["mlp_w1", "mlp_w2", "gdn_w_in", "gdn_w_out", "s5_w_in", "s5_w_out", "m2_w_in", "m2_w_out",
            "gdn_conv_w", "m2_conv_w", "m2_conv_b", "m2_norm_g"]
_MATRICES = _SHARDED[:8]
_REPLICATED = [n for n in _WEIGHTS if n not in _SHARDED]
_GDN_IN, _M2_IN = 4112, 6176
_LAYER_KIND = (0, 1, 2, 0)


def _as2d(a):
    return a.reshape(-1, a.shape[-1])


def _cols_from_shards(g, width):
    return g.transpose(1, 0, 2).reshape(g.shape[1], width)


def _cols_to_shards(a, width):
    return a[:, :width].reshape(a.shape[0], N_DEV, width // N_DEV).transpose(1, 0, 2)


def kernel(x, norm_mix_g, norm_mlp_g, mlp_w1, mlp_w2, gdn_w_in, gdn_conv_w, gdn_a_log, gdn_dt_bias, gdn_o_norm_g, gdn_w_out, s5_w_in, s5_lam_re, s5_lam_im, s5_log_dt, s5_b_re, s5_b_im, s5_c_re, s5_c_im, s5_d, s5_w_out, m2_w_in, m2_conv_w, m2_conv_b, m2_dt_bias, m2_a_log, m2_d, m2_norm_g, m2_w_out, final_norm_g, loss_target, m_norm_mix_g, m_norm_mlp_g, m_mlp_w1, m_mlp_w2, m_gdn_w_in, m_gdn_conv_w, m_gdn_a_log, m_gdn_dt_bias, m_gdn_o_norm_g, m_gdn_w_out, m_s5_w_in, m_s5_lam_re, m_s5_lam_im, m_s5_log_dt, m_s5_b_re, m_s5_b_im, m_s5_c_re, m_s5_c_im, m_s5_d, m_s5_w_out, m_m2_w_in, m_m2_conv_w, m_m2_conv_b, m_m2_dt_bias, m_m2_a_log, m_m2_d, m_m2_norm_g, m_m2_w_out, m_final_norm_g, v_norm_mix_g, v_norm_mlp_g, v_mlp_w1, v_mlp_w2, v_gdn_w_in, v_gdn_conv_w, v_gdn_a_log, v_gdn_dt_bias, v_gdn_o_norm_g, v_gdn_w_out, v_s5_w_in, v_s5_lam_re, v_s5_lam_im, v_s5_log_dt, v_s5_b_re, v_s5_b_im, v_s5_c_re, v_s5_c_im, v_s5_d, v_s5_w_out, v_m2_w_in, v_m2_conv_w, v_m2_conv_b, v_m2_dt_bias, v_m2_a_log, v_m2_d, v_m2_norm_g, v_m2_w_out, v_final_norm_g):
    args = locals()
    W = {n: args[n] for n in _WEIGHTS}
    MOM = {n: args["m_" + n] for n in _WEIGHTS}
    VAR = {n: args["v_" + n] for n in _WEIGHTS}
    h = x[0]
    target = loss_target[0]
    L, D = h.shape

    sends = [W[n].astype(bf16) if n in _MATRICES else _as2d(W[n]) for n in _SHARDED]
    G = dict(zip(_SHARDED, _gather("gather_weights", sends)))
    w1g, w2g = G["mlp_w1"], G["mlp_w2"]
    gdn_in = [jnp.pad(_cols_from_shards(G["gdn_w_in"][:, j], _GDN_IN), ((0, 0), (0, GDN_EXT - _GDN_IN))) for j in range(2)]
    gdn_out = [G["gdn_w_out"][:, j].reshape(D, D) for j in range(2)]
    gdn_conv = [_cols_from_shards(G["gdn_conv_w"][:, 4 * j:4 * j + 4], 3 * D) for j in range(2)]
    s5_in = G["s5_w_in"].reshape(D, D)
    s5_out_g = G["s5_w_out"]
    m2_in = jnp.pad(_cols_from_shards(G["m2_w_in"][:, 0], _M2_IN), ((0, 0), (0, M2_EXT - _M2_IN)))
    m2_out = G["m2_w_out"].reshape(M2_INNER, D)
    m2_conv = _cols_from_shards(G["m2_conv_w"], 2 * M2_INNER)
    m2_cb = _cols_from_shards(G["m2_conv_b"], 2 * M2_INNER)
    m2_ng = _cols_from_shards(G["m2_norm_g"], M2_INNER)

    def mixer_fwd(i, hv):
        kind, j = _LAYER_KIND[i], i // 3
        gn = norm_mix_g[i].reshape(1, D)
        if kind == 0:
            return _gdn_fwd(hv, gn, gdn_in[j], gdn_conv[j], gdn_a_log[j], gdn_dt_bias[j], gdn_o_norm_g[j], gdn_out[j])
        if kind == 1:
            return _s5_fwd(hv, gn, s5_in, s5_lam_re[0], s5_lam_im[0], s5_log_dt[0], s5_b_re[0], s5_b_im[0],
                           s5_c_re[0], s5_c_im[0], s5_d[0], s5_out_g)
        return _m2_fwd(hv, gn, m2_in, m2_conv, m2_cb, m2_dt_bias[0], m2_a_log[0], m2_d[0], m2_ng, m2_out)

    def mixer_bwd(i, dh, hv, sv):
        kind, j = _LAYER_KIND[i], i // 3
        gn = norm_mix_g[i].reshape(1, D)
        if kind == 0:
            return _gdn_bwd(dh, hv, gn, gdn_in[j], gdn_conv[j], gdn_out[j], sv)
        if kind == 1:
            return _s5_bwd(dh, hv, gn, s5_in, s5_out_g, sv)
        return _m2_bwd(dh, hv, gn, m2_in, m2_conv, m2_cb, m2_out, sv)

    tape = []
    for i in range(4):
        h_mid, sv = mixer_fwd(i, h)
        h_next, hn, h1 = _mlp_fwd(h_mid, norm_mlp_g[i].reshape(1, D), w1g, w2g, i)
        tape.append((h, sv, h_mid, hn, h1))
        h = h_next
    loss_row, dh, d_final = _loss_head(h, final_norm_g.reshape(1, D), target)
    loss = lax.psum(loss_row[0, 0], ("x", "y", "c"))

    dw1 = lax.empty((N_DEV, 4, D, D_FF // N_DEV), f32)
    dw2 = lax.empty((N_DEV, 4, D_FF // N_DEV, D), f32)
    d_mix, d_mlp, mg = [None] * 4, [None] * 4, [None] * 4
    for i in reversed(range(4)):
        h_in, sv, h_mid, hn, h1 = tape[i]
        dh, d_mlp[i], dw1, dw2 = _mlp_bwd(dh, h_mid, norm_mlp_g[i].reshape(1, D), hn, h1, w1g, w2g, i, dw1, dw2)
        dh, mg[i] = mixer_bwd(i, dh, h_in, sv)
        d_mix[i] = mg[i]["norm"]
    grad_x = dh.reshape(1, L, D)
    ga, gb_, s5g, m2g = mg[0], mg[3], mg[1], mg[2]

    full = {
        "mlp_w1": dw1, "mlp_w2": dw2,
        "gdn_w_in": jnp.stack([_cols_to_shards(g["w_ext"], _GDN_IN) for g in (ga, gb_)], axis=1),
        "gdn_w_out": jnp.stack([g["w_out"].reshape(N_DEV, D // N_DEV, D) for g in (ga, gb_)], axis=1),
        "s5_w_in": s5g["w_in"].reshape(N_DEV, 1, D // N_DEV, D), "s5_w_out": s5g["w_out"],
        "m2_w_in": _cols_to_shards(m2g["w_ext"], _M2_IN)[:, None],
        "m2_w_out": m2g["w_out"].reshape(N_DEV, 1, M2_INNER // N_DEV, D),
        "gdn_conv_w": jnp.concatenate([_cols_to_shards(g["conv_w"], 3 * D) for g in (ga, gb_)], axis=1),
        "m2_conv_w": _cols_to_shards(m2g["conv_w"], 2 * M2_INNER),
        "m2_conv_b": _cols_to_shards(m2g["conv_b"], 2 * M2_INNER),
        "m2_norm_g": _cols_to_shards(m2g["norm_g"], M2_INNER),
    }
    sends = [full[n].reshape((N_DEV,) + _as2d(W[n]).shape) for n in _SHARDED]
    core = lax.axis_index("c")
    theirs = _scatter_pair("scatter_pair", sends)
    chip_sums = [_pair_add("pair_add_" + n, full8, th, core, bf16 if n in _MATRICES else f32)
                 for n, full8, th in zip(_SHARDED, sends, theirs)]
    parts = dict(zip(_SHARDED, _scatter_chips("scatter_chips", chip_sums)))

    rep = {
        "norm_mix_g": jnp.concatenate(d_mix, axis=0), "norm_mlp_g": jnp.concatenate(d_mlp, axis=0),
        "gdn_a_log": jnp.stack([ga["a_log"], gb_["a_log"]]), "gdn_dt_bias": jnp.stack([ga["dt_bias"], gb_["dt_bias"]]),
        "gdn_o_norm_g": jnp.stack([ga["o_norm_g"], gb_["o_norm_g"]]),
        "s5_lam_re": s5g["lam_re"], "s5_lam_im": s5g["lam_im"], "s5_log_dt": s5g["log_dt"], "s5_b_re": s5g["b_re"],
        "s5_b_im": s5g["b_im"], "s5_c_re": s5g["c_re"], "s5_c_im": s5g["c_im"], "s5_d": s5g["d"],
        "m2_dt_bias": m2g["dt_bias"], "m2_a_log": m2g["a_log"], "m2_d": m2g["d"], "final_norm_g": d_final,
    }

    def pack(d):
        flat = jnp.concatenate([d[n].reshape(-1).astype(f32) for n in _REPLICATED])
        return jnp.pad(flat, (0, -flat.shape[0] % (256 * 128))).reshape(-1, 128)

    (rep_parts,) = _gather("gather_small_grads", [pack(rep)])

    res = {}
    for n in _SHARDED:
        w2d = _as2d(W[n])
        out = _adamw("adamw_" + n, parts[n], w2d, _as2d(MOM[n]), _as2d(VAR[n]))
        res[n] = [o.reshape(W[n].shape) for o in out]
    out = _adamw("adamw_replicated", rep_parts, pack(W), pack(MOM), pack(VAR))
    off = 0
    for n in _REPLICATED:
        size = W[n].size
        res[n] = [o.reshape(-1)[off:off + size].reshape(W[n].shape) for o in out]
        off += size

    return (loss, grad_x, *[res[n][0] for n in _WEIGHTS], *[res[n][1] for n in _WEIGHTS],
            *[res[n][2] for n in _WEIGHTS], *[res[n][3] for n in _WEIGHTS])
```

```python
import functools

import jax
import jax.numpy as jnp
from jax import lax
from jax.experimental import pallas as pl
from jax.experimental.pallas import tpu as pltpu

f32 = jnp.float32
bf16 = jnp.bfloat16
HI = lax.Precision.HIGHEST
MESH = pl.DeviceIdType.MESH

N_DEV = 8
D_MODEL = 1024
D_FF = 4096
CHUNK = 64
RMS_EPS = 1e-6
GDN_HEADS = 8
GDN_HB = 8
GDN_EXT = 4224
S5_STATE = 64
S5_SCAN_LANES = 256
M2_INNER = 2048
M2_EXT = 6272
M2_HEADS = 32
M2_GB = 4
VMEM_LIMIT_BYTES = 56 * 1024 * 1024

ADAM_LR, ADAM_B1, ADAM_B2, ADAM_EPS, ADAM_WD, ADAM_STEP = 0.001, 0.9, 0.999, 1e-08, 0.01, 10

_NN = ((1,), (0,))
_NT = ((1,), (1,))
_TN = ((0,), (0,))


def _dot(a, b, dims=_NN):
    return lax.dot_general(a, b, (dims, ((), ())), precision=HI, preferred_element_type=f32)


def _dotb(a, b, dims=_NN):
    return lax.dot_general(a.astype(bf16), b.astype(bf16), (dims, ((), ())), preferred_element_type=f32)


def _bdot(p, q, dims):
    return lax.dot_general(p, q, (dims, ((), ())), preferred_element_type=f32)


def _pieces(x, n):
    out = []
    for _ in range(n - 1):
        p = x.astype(bf16)
        out.append(p)
        x = x - p.astype(f32)
    return out + [x.astype(bf16)]


def _dot01_raw(mask, b, dims=_NN, mask_first=True):
    m = mask.astype(bf16)
    p = _pieces(b, 3)
    if mask_first:
        return _bdot(m, p[0], dims) + (_bdot(m, p[1], dims) + _bdot(m, p[2], dims))
    return _bdot(p[0], m, dims) + (_bdot(p[1], m, dims) + _bdot(p[2], m, dims))


@jax.custom_vjp
def _dot01_nn(mask, b):
    return _dot01_raw(mask, b, _NN)


@jax.custom_vjp
def _dot01_nt(mask, b):
    return _dot01_raw(mask, b, _NT)


_dot01_nn.defvjp(lambda m, b: (_dot01_raw(m, b, _NN), m),
                 lambda m, ct: (jnp.zeros_like(m), _dot01_raw(m, ct, _TN, mask_first=True)))
_dot01_nt.defvjp(lambda m, b: (_dot01_raw(m, b, _NT), m),
                 lambda m, ct: (jnp.zeros_like(m), _dot01_raw(m, ct, _TN, mask_first=False)))


def _dot01_vjp(mask, b, dims=_NN):
    return _dot01_nn(mask, b) if dims == _NN else _dot01_nt(mask, b)


def _dot3_raw(a, b, dims=_NN):
    (ah, al), (bh, bl) = _pieces(a, 2), _pieces(b, 2)
    return _bdot(ah, bh, dims) + (_bdot(ah, bl, dims) + _bdot(al, bh, dims))


@jax.custom_vjp
def _dot3_vjp(a, b):
    return _dot3_raw(a, b)


_dot3_vjp.defvjp(lambda a, b: (_dot3_raw(a, b), (a, b)),
                 lambda res, ct: (_dot3_raw(ct, res[1], _NT), _dot3_raw(res[0], ct, _TN)))


class _Dots:
    def __init__(self, dot3, dot01):
        self.dot3, self.dot01 = dot3, dot01


_PLAIN_DOTS = _Dots(_dot3_raw, _dot01_raw)
_VJP_DOTS = _Dots(_dot3_vjp, _dot01_vjp)


def _iota(shape, dim):
    return lax.broadcasted_iota(jnp.int32, shape, dim)


def _params(n_grid):
    return pltpu.CompilerParams(dimension_semantics=("arbitrary",) * n_grid, vmem_limit_bytes=VMEM_LIMIT_BYTES)


def _row_tile(n_rows):
    return min(512, n_rows)


def _mm_rows(n_rows):
    return min(1024, n_rows)


def _col_tile(n, cap=1024):
    best = 128
    for t in range(128, cap + 1, 128):
        if n % t == 0:
            best = t
    return best


def _mm(name, a, b, *, dims, grid, a_spec, b_spec, out_shape, out_spec, aux=(), a_fn=None, epi_fn=None,
        acc_shape, out_init=None):
    nk = grid[2]
    n_aux = len(aux)
    kinds = [x[2] for x in aux]

    def body(*refs):
        a_ref, b_ref = refs[0], refs[1]
        aux_refs = refs[2:2 + n_aux]
        pos = 2 + n_aux + (1 if out_init is not None else 0)
        o_ref, acc_ref = refs[pos], refs[pos + 1]
        k = pl.program_id(2)

        @pl.when(k == 0)
        def _():
            acc_ref[...] = jnp.zeros_like(acc_ref)

        av = a_ref[...]
        if a_fn is not None:
            av = a_fn(av, *[r[...] for r, kd in zip(aux_refs, kinds) if kd == "a"])
        acc_ref[...] += lax.dot_general(av.astype(bf16), b_ref[...].astype(bf16), (dims, ((), ())),
                                        preferred_element_type=f32)

        @pl.when(k == nk - 1)
        def _():
            r = acc_ref[...]
            if epi_fn is not None:
                r = epi_fn(r, *[x[...] for x, kd in zip(aux_refs, kinds) if kd == "e"])
            o_ref[...] = r.astype(o_ref.dtype)

    in_specs = [a_spec, b_spec] + [x[1] for x in aux]
    args = [a, b] + [x[0] for x in aux]
    aliases = {}
    if out_init is not None:
        in_specs.append(pl.BlockSpec(memory_space=pl.ANY))
        args.append(out_init)
        aliases = {len(args) - 1: 0}
    return pl.pallas_call(
        body, name=name, grid=grid, in_specs=in_specs, out_specs=out_spec, out_shape=out_shape,
        scratch_shapes=[pltpu.VMEM(acc_shape, f32)], input_output_aliases=aliases,
        compiler_params=_params(3),
    )(*args)


def _ew(name, f, ins, outs, grid):
    n_in = len(ins)
    modes = [o[4] for o in outs]

    def body(*refs):
        vals = [r[...] for r in refs[:n_in]]
        res = f(*vals)
        if not isinstance(res, (tuple, list)):
            res = (res,)
        for r, o_ref, mode in zip(res, refs[n_in:], modes):
            if mode is None:
                o_ref[...] = r.astype(o_ref.dtype)
                continue
            first = pl.program_id(1) == 0
            if mode == "all":
                first = jnp.logical_and(first, pl.program_id(0) == 0)

            @pl.when(first)
            def _(r=r, o_ref=o_ref):
                o_ref[...] = r.astype(o_ref.dtype)

            @pl.when(jnp.logical_not(first))
            def _(r=r, o_ref=o_ref):
                o_ref[...] += r.astype(o_ref.dtype)

    res = pl.pallas_call(
        body, name=name, grid=grid,
        in_specs=[pl.BlockSpec(blk, im) for _, blk, im in ins],
        out_specs=[pl.BlockSpec(o[2], o[3]) for o in outs],
        out_shape=[jax.ShapeDtypeStruct(o[0], o[1]) for o in outs],
        compiler_params=_params(2),
    )(*[a for a, _, _ in ins])
    return res


def _vjp_fn(f, n_primal):
    def g(*args):
        _, vjp = jax.vjp(f, *args[:n_primal])
        cts = args[n_primal:]
        return vjp(cts[0] if len(cts) == 1 else tuple(cts))
    return g


def _scan_fwd(name, step, n_state, state_shape, cins, ins, outs, n_units, n_chunks):
    n_c, n_in, n_out = len(cins), len(ins), len(outs)

    def body(*refs):
        c_refs = refs[:n_c]
        in_refs = refs[n_c:n_c + n_in]
        out_refs = refs[n_c + n_in:n_c + n_in + n_out]
        saved = refs[n_c + n_in + n_out:n_c + n_in + n_out + n_state]
        st = refs[n_c + n_in + n_out + n_state:]

        @pl.when(pl.program_id(1) == 0)
        def _():
            for s in st:
                s[...] = jnp.zeros_like(s)

        cur = [s[...] for s in st]
        for sv, s in zip(saved, cur):
            sv[...] = s
        new, res = step(cur, [r[...] for r in c_refs], [r[...] for r in in_refs], _PLAIN_DOTS)
        for s, n in zip(st, new):
            s[...] = n
        for o, r in zip(out_refs, res):
            o[...] = r

    sshape = (n_units, n_chunks) + state_shape
    sblock = (None, None) + state_shape
    nz = len(state_shape)
    res = pl.pallas_call(
        body, name=name, grid=(n_units, n_chunks),
        in_specs=[pl.BlockSpec(e[1], e[2]) for e in cins + ins],
        out_specs=[pl.BlockSpec(o[1], o[2]) for o in outs]
        + [pl.BlockSpec(sblock, lambda u, c: (u, c) + (0,) * nz)] * n_state,
        out_shape=[jax.ShapeDtypeStruct(o[0], f32) for o in outs]
        + [jax.ShapeDtypeStruct(sshape, f32)] * n_state,
        scratch_shapes=[pltpu.VMEM(state_shape, f32)] * n_state,
        compiler_params=_params(2),
    )(*[e[0] for e in cins + ins])
    return res[:n_out], res[n_out:]


def _scan_bwd(name, step, n_state, state_shape, cins, ins, saved, douts, n_units, n_chunks):
    n_c, n_in, n_do = len(cins), len(ins), len(douts)

    def flip(im):
        return lambda u, c: im(u, n_chunks - 1 - c)

    def body(*refs):
        p = 0
        c_refs = refs[p:p + n_c]; p += n_c
        in_refs = refs[p:p + n_in]; p += n_in
        sv_refs = refs[p:p + n_state]; p += n_state
        do_refs = refs[p:p + n_do]; p += n_do
        dc_refs = refs[p:p + n_c]; p += n_c
        di_refs = refs[p:p + n_in]; p += n_in
        dst = refs[p:]
        first = pl.program_id(1) == 0

        @pl.when(first)
        def _():
            for s in dst:
                s[...] = jnp.zeros_like(s)

        def fn(states, consts, vals):
            new, res = step(states, consts, vals, _VJP_DOTS)
            return tuple(new), tuple(res)

        prim = ([r[...] for r in sv_refs], [r[...] for r in c_refs], [r[...] for r in in_refs])
        _, vjp = jax.vjp(fn, *prim)
        d_states, d_consts, d_vals = vjp((tuple(s[...] for s in dst), tuple(r[...] for r in do_refs)))
        for s, g in zip(dst, d_states):
            s[...] = g
        for o, g in zip(di_refs, d_vals):
            o[...] = g
        for o, g in zip(dc_refs, d_consts):
            @pl.when(first)
            def _(o=o, g=g):
                o[...] = g

            @pl.when(jnp.logical_not(first))
            def _(o=o, g=g):
                o[...] += g

    nz = len(state_shape)
    sblock = (None, None) + state_shape
    def gshape(e):
        return e[3] if len(e) == 5 else e[0].shape

    def gmap(e):
        return e[4] if len(e) == 5 else e[2]

    in_specs = ([pl.BlockSpec(e[1], e[2]) for e in cins]
                + [pl.BlockSpec(e[1], flip(e[2])) for e in ins]
                + [pl.BlockSpec(sblock, lambda u, c: (u, n_chunks - 1 - c) + (0,) * nz)] * n_state
                + [pl.BlockSpec(e[1], flip(e[2])) for e in douts])
    out_specs = ([pl.BlockSpec(e[1], e[2]) for e in cins]
                 + [pl.BlockSpec(e[1], flip(gmap(e))) for e in ins])
    out_shape = [jax.ShapeDtypeStruct(gshape(e), f32) for e in cins + ins]
    res = pl.pallas_call(
        body, name=name, grid=(n_units, n_chunks), in_specs=in_specs, out_specs=out_specs, out_shape=out_shape,
        scratch_shapes=[pltpu.VMEM(state_shape, f32)] * n_state,
        compiler_params=_params(2),
    )(*([e[0] for e in cins + ins] + list(saved) + [e[0] for e in douts]))
    return res[:n_c], res[n_c:]


def _rms(x, g):
    return x * lax.rsqrt(jnp.mean(x * x, axis=-1, keepdims=True) + RMS_EPS) * g


def _rows(tm, width):
    return (tm, width), lambda r, z: (r, 0)


def _const(shape):
    return shape, lambda r, z: (0,) * len(shape)


def _rms_fwd(name, h, g):
    L, D = h.shape
    tm = _row_tile(L)
    return _ew(name, _rms, [(h, *_rows(tm, D)), (g, *_const((1, D)))],
               [((L, D), f32, *_rows(tm, D), None)], (L // tm, 1))[0]


def _rms_bwd(name, h, g, d_hn, d_res):
    L, D = h.shape
    tm = _row_tile(L)

    def f(hv, gv, dv, rv):
        dh, dg = _vjp_fn(_rms, 2)(hv, gv, dv)
        return dh + rv, dg

    return _ew(name, f, [(h, *_rows(tm, D)), (g, *_const((1, D))), (d_hn, *_rows(tm, D)), (d_res, *_rows(tm, D))],
               [((L, D), f32, *_rows(tm, D), None), ((1, D), f32, *_const((1, D)), "all")], (L // tm, 1))


def _loss_head(h, g, target):
    L, D = h.shape
    tm = _row_tile(L)

    def f(hv, gv, tv):
        def lf(a, b):
            e = jnp.square(_rms(a, b) - tv)
            return (0.5 / D) * jnp.sum(jnp.sum(e, axis=1, keepdims=True), axis=0, keepdims=True)

        val, vjp = jax.vjp(lf, hv, gv)
        dh, dg = vjp(jnp.ones((1, 1), f32))
        return jnp.broadcast_to(val, (1, 128)), dh, dg

    return _ew("loss_head", f, [(h, *_rows(tm, D)), (g, *_const((1, D))), (target, *_rows(tm, D))],
               [((1, 128), f32, *_const((1, 128)), "all"), ((L, D), f32, *_rows(tm, D), None),
                ((1, D), f32, *_const((1, D)), "all")], (L // tm, 1))


def _sqrelu(x):
    return jnp.square(jnp.maximum(x, 0.0))


def _mm_plain(name, a, b, dims, *, a_fn=None, epi_fn=None, aux=()):
    if dims == _NN:
        (M, K), N = a.shape, b.shape[1]
    elif dims == _NT:
        (M, K), N = a.shape, b.shape[0]
    else:
        (K, M), N = a.shape, b.shape[1]
    tm = _mm_rows(M)
    tn = _col_tile(N, 1536)
    tk = _col_tile(K)
    if dims == _TN:
        tk = min(512, K)
        a_spec = pl.BlockSpec((tk, tm), lambda i, j, k: (k, i))
        b_spec = pl.BlockSpec((tk, tn), lambda i, j, k: (k, j))
        a_aux = pl.BlockSpec((tk, tm), lambda i, j, k: (k, i))
    elif dims == _NT:
        a_spec = pl.BlockSpec((tm, tk), lambda i, j, k: (i, k))
        b_spec = pl.BlockSpec((tn, tk), lambda i, j, k: (j, k))
        a_aux = pl.BlockSpec((tm, tk), lambda i, j, k: (i, k))
    else:
        a_spec = pl.BlockSpec((tm, tk), lambda i, j, k: (i, k))
        b_spec = pl.BlockSpec((tk, tn), lambda i, j, k: (k, j))
        a_aux = pl.BlockSpec((tm, tk), lambda i, j, k: (i, k))
    e_aux = pl.BlockSpec((tm, tn), lambda i, j, k: (i, j))
    aux_full = [(x, a_aux if kd == "a" else e_aux, kd) for x, kd in aux]
    return _mm(name, a, b, dims=dims, grid=(M // tm, N // tn, K // tk), a_spec=a_spec, b_spec=b_spec,
               out_shape=jax.ShapeDtypeStruct((M, N), f32), out_spec=pl.BlockSpec((tm, tn), lambda i, j, k: (i, j)),
               aux=aux_full, a_fn=a_fn, epi_fn=epi_fn, acc_shape=(tm, tn))


def _mlp_fwd(h, g, w1g, w2g, layer):
    L, D = h.shape
    tm = _mm_rows(L)
    fs = D_FF // N_DEV
    hn = _rms_fwd("mlp_norm", h, g)
    h1 = _mm("mlp_up", hn, w1g, dims=_NN, grid=(L // tm, N_DEV, 1),
             a_spec=pl.BlockSpec((tm, D), lambda i, j, k: (i, 0)),
             b_spec=pl.BlockSpec((None, None, D, fs), lambda i, j, k: (j, layer, 0, 0)),
             out_shape=jax.ShapeDtypeStruct((L, D_FF), f32), out_spec=pl.BlockSpec((tm, fs), lambda i, j, k: (i, j)),
             acc_shape=(tm, fs))
    tn = D
    h_out = _mm("mlp_down", h1, w2g, dims=_NN, grid=(L // tm, D // tn, N_DEV),
                a_spec=pl.BlockSpec((tm, fs), lambda i, j, k: (i, k)),
                b_spec=pl.BlockSpec((None, None, fs, tn), lambda i, j, k: (k, layer, 0, j)),
                out_shape=jax.ShapeDtypeStruct((L, D), f32), out_spec=pl.BlockSpec((tm, tn), lambda i, j, k: (i, j)),
                aux=[(h, pl.BlockSpec((tm, tn), lambda i, j, k: (i, j)), "e")],
                a_fn=_sqrelu, epi_fn=lambda acc, res: acc + res, acc_shape=(tm, tn))
    return h_out, hn, h1


def _mlp_bwd(dh, h, g, hn, h1, w1g, w2g, layer, dw1_buf, dw2_buf):
    L, D = h.shape
    tm = _mm_rows(L)
    fs = D_FF // N_DEV
    tk = _mm_rows(L)
    dh1 = _mm("mlp_down_dx", dh, w2g, dims=_NT, grid=(L // tm, N_DEV, 1),
              a_spec=pl.BlockSpec((tm, D), lambda i, j, k: (i, 0)),
              b_spec=pl.BlockSpec((None, None, fs, D), lambda i, j, k: (j, layer, 0, 0)),
              out_shape=jax.ShapeDtypeStruct((L, D_FF), f32), out_spec=pl.BlockSpec((tm, fs), lambda i, j, k: (i, j)),
              aux=[(h1, pl.BlockSpec((tm, fs), lambda i, j, k: (i, j)), "e")],
              epi_fn=lambda acc, pre: acc * (2.0 * jnp.maximum(pre, 0.0)), acc_shape=(tm, fs))
    dw2_buf = _mm("mlp_down_dw", h1, dh, dims=_TN, grid=(N_DEV, 1, L // tk),
                  a_spec=pl.BlockSpec((tk, fs), lambda i, j, k: (k, i)),
                  b_spec=pl.BlockSpec((tk, D), lambda i, j, k: (k, 0)),
                  out_shape=jax.ShapeDtypeStruct(dw2_buf.shape, f32),
                  out_spec=pl.BlockSpec((None, None, fs, D), lambda i, j, k: (i, layer, 0, 0)),
                  a_fn=_sqrelu, acc_shape=(fs, D), out_init=dw2_buf)
    tr = D
    dw1_buf = _mm("mlp_up_dw", hn, dh1, dims=_TN, grid=(D // tr, N_DEV, L // tk),
                  a_spec=pl.BlockSpec((tk, tr), lambda i, j, k: (k, i)),
                  b_spec=pl.BlockSpec((tk, fs), lambda i, j, k: (k, j)),
                  out_shape=jax.ShapeDtypeStruct(dw1_buf.shape, f32),
                  out_spec=pl.BlockSpec((None, None, tr, fs), lambda i, j, k: (j, layer, i, 0)),
                  acc_shape=(tr, fs), out_init=dw1_buf)
    tn = D
    dhn = _mm("mlp_up_dx", dh1, w1g, dims=_NT, grid=(L // tm, D // tn, N_DEV),
              a_spec=pl.BlockSpec((tm, fs), lambda i, j, k: (i, k)),
              b_spec=pl.BlockSpec((None, None, tn, fs), lambda i, j, k: (k, layer, j, 0)),
              out_shape=jax.ShapeDtypeStruct((L, D), f32), out_spec=pl.BlockSpec((tm, tn), lambda i, j, k: (i, j)),
              acc_shape=(tm, tn))
    dh_in, dg = _rms_bwd("mlp_norm_bwd", h, g, dhn, dh)
    return dh_in, dg, dw1_buf, dw2_buf


def _shift_dn(x, s, row):
    return x if s == 0 else jnp.where(row >= s, pltpu.roll(x, s, 0), 0.0)


def _shift_up(x, s, row):
    n = x.shape[0]
    return x if s == 0 else jnp.where(row < n - s, pltpu.roll(x, n - s, 0), 0.0)


def _conv_pre(x, w, b, row):
    c = jnp.broadcast_to(b, x.shape)
    for j in range(4):
        c = c + w[j:j + 1, :] * _shift_dn(x, 3 - j, row)
    return c


def _conv_fwd(name, x_arr, blk_off, w, b):
    L = x_arr.shape[0]
    C = w.shape[1]

    def f(x, wv, bv):
        c = _conv_pre(x, wv, bv, _iota(x.shape, 0))
        return c * jax.nn.sigmoid(c)

    return _ew(name, f, [(x_arr, (L, 128), lambda j, z: (0, blk_off + j)), (w, (4, 128), lambda j, z: (0, j)),
                         (b, (1, 128), lambda j, z: (0, j))],
               [((L, C), f32, (L, 128), lambda j, z: (0, j), None)], (C // 128, 1))[0]


def _conv_bwd(name, x_arr, blk_off, w, b, dy):
    L = x_arr.shape[0]
    C = w.shape[1]

    def f(x, wv, bv, g):
        row = _iota(x.shape, 0)
        c = _conv_pre(x, wv, bv, row)
        s = jax.nn.sigmoid(c)
        dc = g * (s * (1.0 + c * (1.0 - s)))
        dx = jnp.zeros_like(x)
        dw = jnp.zeros((4, 128), f32)
        r4 = _iota((4, 128), 0)
        for j in range(4):
            dx = dx + wv[j:j + 1, :] * _shift_up(dc, 3 - j, row)
            dwj = jnp.sum(dc * _shift_dn(x, 3 - j, row), axis=0, keepdims=True)
            dw = dw + jnp.where(r4 == j, jnp.broadcast_to(dwj, (4, 128)), 0.0)
        return dx, dw, jnp.sum(dc, axis=0, keepdims=True)

    return _ew(name, f, [(x_arr, (L, 128), lambda j, z: (0, blk_off + j)), (w, (4, 128), lambda j, z: (0, j)),
                         (b, (1, 128), lambda j, z: (0, j)), (dy, (L, 128), lambda j, z: (0, j))],
               [((L, C), f32, (L, 128), lambda j, z: (0, j), None), ((4, C), f32, (4, 128), lambda j, z: (0, j), None),
                ((1, C), f32, (1, 128), lambda j, z: (0, j), None)], (C // 128, 1))


def _l2norm(t):
    return t * lax.rsqrt(jnp.sum(t * t, axis=-1, keepdims=True) + 1e-6)


def _gdn_act(cq, ck, ab, alog, dtb):
    h = pl.program_id(1)
    qn = _l2norm(cq) * (128.0 ** -0.5)
    kn = _l2norm(ck)
    lane = _iota(ab.shape, 1)
    a_raw = jnp.sum(jnp.where(lane == h, ab, 0.0), axis=1, keepdims=True)
    b_raw = jnp.sum(jnp.where(lane == h + GDN_HEADS, ab, 0.0), axis=1, keepdims=True)
    lane1 = _iota(alog.shape, 1)
    al = jnp.sum(jnp.where(lane1 == h, alog, 0.0), axis=1, keepdims=True)
    db = jnp.sum(jnp.where(lane1 == h, dtb, 0.0), axis=1, keepdims=True)
    g = -jnp.exp(al) * jax.nn.softplus(a_raw + db)
    beta = jax.nn.sigmoid(b_raw)
    return qn, kn, jnp.broadcast_to(g, cq.shape), jnp.broadcast_to(beta, cq.shape)


def _each(f, *lists):
    return [f(*a) for a in zip(*lists)]


def _gdn_chunk(states, consts, vals, dots):
    S = list(states)
    cut = [slice(128 * i, 128 * i + 128) for i in range(len(S))]
    q, k, v, gb, bb = ([t[:, c] for c in cut] for t in vals)
    C = vals[0].shape[0]
    row, col = _iota((C, C), 0), _iota((C, C), 1)
    causal, strict = row >= col, row > col
    ltri = causal.astype(f32)
    eye = (row == col).astype(f32)
    e0 = (_iota((C, 128), 1) == 0).astype(f32)
    last = _iota((C, 1), 0) == C - 1
    Gb = _each(lambda g: dots.dot01(ltri, g), gb)
    Gc = _each(lambda g: jnp.mean(g, axis=1, keepdims=True), Gb)
    Gr = _each(lambda g: dots.dot01(e0, g, _NT), Gb)
    bc = _each(lambda b: jnp.mean(b, axis=1, keepdims=True), bb)
    decay = _each(lambda gc, gr: jnp.where(causal, jnp.exp(jnp.where(causal, gc - gr, 0.0)), 0.0), Gc, Gr)
    kk = _each(lambda a: _dotb(a, a, _NT), k)
    A = _each(lambda b, x, d: jnp.where(strict, b * x * d, 0.0), bc, kk, decay)
    M = _each(lambda a: eye - a, A)
    P = _each(lambda a: dots.dot3(a, a), A)
    for it in range(5):
        M = _each(lambda m, p: m + dots.dot3(m, p), M, P)
        if it < 4:
            P = _each(lambda p: dots.dot3(p, p), P)
    eG = _each(jnp.exp, Gc)
    u = _each(lambda m, x, b: _dotb(m, x * b), M, v, bc)
    w = _each(lambda m, x, b, e: _dotb(m, x * (b * e)), M, k, bc, eG)
    qk = _each(lambda a, b, d: _dotb(a, b, _NT) * d, q, k, decay)
    g_last = _each(lambda gc: jnp.sum(jnp.where(last, gc, 0.0), axis=0, keepdims=True), Gc)
    v_new = _each(lambda a, b, s: a - _dotb(b, s), u, w, S)
    o = _each(lambda a, e, s, b, x: _dotb(a * e, s) + _dotb(b, x), q, eG, S, qk, v_new)
    S_new = _each(lambda gl, s, a, gc, x: jnp.exp(gl) * s + _dotb(a * jnp.exp(gl - gc), x, _TN), g_last, S, k, Gc, v_new)
    return S_new, [jnp.concatenate(o, axis=1)]


def _gdn_post(o, gate, g):
    return _rms(o, g) * (gate * jax.nn.sigmoid(gate))


def _pad_row(v):
    return jnp.pad(v.astype(f32), (0, 128 - v.shape[0])).reshape(1, 128)


def _gdn_fwd(h, g_norm, w_ext, conv_w, a_log, dt_bias, o_norm_g, w_out):
    L, D = h.shape
    tm = _row_tile(L)
    nc = L // CHUNK
    H = GDN_HEADS
    hn = _rms_fwd("mix_norm", h, g_norm)
    proj = _mm_plain("gdn_in", hn, w_ext, _NN)
    zb = jnp.zeros((1, 3 * D), f32)
    cq = _conv_fwd("gdn_conv", proj, 0, conv_w, zb)
    alog, dtb = _pad_row(a_log), _pad_row(dt_bias)
    act_ins = [(cq, (tm, 128), lambda r, hh: (r, hh)), (cq, (tm, 128), lambda r, hh: (r, H + hh)),
               (proj, (tm, 128), lambda r, hh: (r, 4 * H)), (alog, (1, 128), lambda r, hh: (0, 0)),
               (dtb, (1, 128), lambda r, hh: (0, 0))]
    qn, kn, gb, bb = _ew("gdn_act", _gdn_act, act_ins,
                         [((L, D), f32, (tm, 128), lambda r, hh: (r, hh), None)] * 4, (L // tm, H))
    cblk = (CHUNK, 128 * GDN_HB)
    core_ins = [(qn, cblk, lambda u, c: (c, u)), (kn, cblk, lambda u, c: (c, u)),
                (cq, cblk, lambda u, c: (c, 2 * H // GDN_HB + u), (L, D), lambda u, c: (c, u)),
                (gb, cblk, lambda u, c: (c, u)), (bb, cblk, lambda u, c: (c, u))]
    (o,), saved_s = _scan_fwd("gdn_core", _gdn_chunk, GDN_HB, (128, 128), [], core_ins,
                              [((L, D), cblk, lambda u, c: (c, u))], H // GDN_HB, nc)
    on = o_norm_g.reshape(1, 128)
    post_ins = [(o, (tm, 128), lambda r, hh: (r, hh)), (proj, (tm, 128), lambda r, hh: (r, 3 * H + hh)),
                (on, (1, 128), lambda r, hh: (0, 0))]
    y = _ew("gdn_post", _gdn_post, post_ins, [((L, D), f32, (tm, 128), lambda r, hh: (r, hh), None)], (L // tm, H))[0]
    h_out = _mm_plain("gdn_out", y, w_out, _NN, epi_fn=lambda acc, res: acc + res, aux=[(h, "e")])
    saved = dict(hn=hn, proj=proj, cq=cq, alog=alog, dtb=dtb, act_ins=act_ins, core_ins=core_ins, saved_s=saved_s,
                 post_ins=post_ins, y=y, zb=zb)
    return h_out, saved


def _gdn_bwd(dh, h, g_norm, w_ext, conv_w, w_out, sv):
    L, D = h.shape
    tm = _row_tile(L)
    nc = L // CHUNK
    H = GDN_HEADS
    dy = _mm_plain("gdn_out_dx", dh, w_out, _NT)
    dw_out = _mm_plain("gdn_out_dw", sv["y"], dh, _TN)
    hd = ((L, D), f32, (tm, 128), lambda r, hh: (r, hh), None)
    d_o, d_gate, d_on = _ew("gdn_post_bwd", _vjp_fn(_gdn_post, 3),
                            sv["post_ins"] + [(dy, (tm, 128), lambda r, hh: (r, hh))],
                            [hd, hd, ((1, 128), f32, (1, 128), lambda r, hh: (0, 0), "all")], (L // tm, H))
    cblk = (CHUNK, 128 * GDN_HB)
    _, (dqn, dkn, dv, dgb, dbb) = _scan_bwd("gdn_core_bwd", _gdn_chunk, GDN_HB, (128, 128), [], sv["core_ins"],
                                            sv["saved_s"], [(d_o, cblk, lambda u, c: (c, u))], H // GDN_HB, nc)
    cts = [(t, (tm, 128), lambda r, hh: (r, hh)) for t in (dqn, dkn, dgb, dbb)]
    row128 = ((1, 128), f32, (1, 128), lambda r, hh: (0, 0), "all")
    d_cq, d_ck, d_ab, d_alog, d_dtb = _ew(
        "gdn_act_bwd", _vjp_fn(_gdn_act, 5), sv["act_ins"] + cts,
        [hd, hd, ((L, 128), f32, (tm, 128), lambda r, hh: (r, 0), "inner"), row128, row128], (L // tm, H))
    d_conv_out = jnp.concatenate([d_cq, d_ck, dv], axis=1)
    d_conv_in, d_conv_w, _ = _conv_bwd("gdn_conv_bwd", sv["proj"], 0, conv_w, sv["zb"], d_conv_out)
    d_proj = jnp.concatenate([d_conv_in, d_gate, d_ab], axis=1)
    dw_ext = _mm_plain("gdn_in_dw", sv["hn"], d_proj, _TN)
    dhn = _mm_plain("gdn_in_dx", d_proj, w_ext, _NT)
    dh_in, dg = _rms_bwd("mix_norm_bwd", h, g_norm, dhn, dh)
    grads = dict(norm=dg, w_ext=dw_ext, conv_w=d_conv_w, a_log=d_alog[0, :H], dt_bias=d_dtb[0, :H],
                 o_norm_g=d_on[0], w_out=dw_out)
    return dh_in, grads


def _expand_lanes(row, width, rep):
    sel = ((_iota((128, width), 1) // rep) == _iota((128, width), 0)).astype(f32)
    return jnp.mean(_dot(jnp.broadcast_to(row, (8, 128)), sel), axis=0, keepdims=True)


def _s5_params(lre, lim, ldt, wbr, wbi):
    dt = jnp.exp(_expand_lanes(ldt, 512, S5_STATE))
    mag = jnp.exp(lre * dt)
    ang = lim * dt
    abr, abi = mag * jnp.cos(ang), mag * jnp.sin(ang)
    nr = abr - 1.0
    den = lre * lre + lim * lim
    cr = (nr * lre + abi * lim) / den
    ci = (abi * lre - nr * lim) / den
    return abr, abi, cr * wbr - ci * wbi, cr * wbi + ci * wbr


def _s5_scan(name, xr, xi, ar, ai, rev, want_prev):
    L, W = xr.shape
    nb = L // 8
    n_out = 4 if want_prev else 2

    def body(xr_ref, xi_ref, ar_ref, ai_ref, *outs):
        a_r = ar_ref[...]
        a_i = -ai_ref[...] if rev else ai_ref[...]

        def cm(p, q):
            return p[0] * q[0] - p[1] * q[1], p[0] * q[1] + p[1] * q[0]

        a1 = (a_r, a_i)
        a2 = cm(a1, a1)
        a3 = cm(a2, a1)
        a4 = cm(a2, a2)
        pw = [a1, a2, a3, a4, cm(a4, a1), cm(a4, a2), cm(a4, a3), cm(a4, a4)]
        blk8 = (8, S5_SCAN_LANES)
        row = _iota(blk8, 0)
        tab_r = jnp.zeros(blk8, f32)
        tab_i = jnp.zeros(blk8, f32)
        for t in range(8):
            idx = 7 - t if rev else t
            tab_r = jnp.where(row == idx, jnp.broadcast_to(pw[t][0], blk8), tab_r)
            tab_i = jnp.where(row == idx, jnp.broadcast_to(pw[t][1], blk8), tab_i)
        lv = [(d, jnp.broadcast_to(p[0], blk8), jnp.broadcast_to(p[1], blk8)) for d, p in ((1, a1), (2, a2), (4, a4))]

        def step(i, carry):
            cr, ci = carry
            blk = nb - 1 - i if rev else i
            r0 = pl.multiple_of(blk * 8, 8)
            x_r = xr_ref[pl.ds(r0, 8), :]
            x_i = xi_ref[pl.ds(r0, 8), :]
            for d, p_r, p_i in lv:
                if rev:
                    s_r = jnp.where(row < 8 - d, pltpu.roll(x_r, 8 - d, 0), 0.0)
                    s_i = jnp.where(row < 8 - d, pltpu.roll(x_i, 8 - d, 0), 0.0)
                else:
                    s_r = jnp.where(row >= d, pltpu.roll(x_r, d, 0), 0.0)
                    s_i = jnp.where(row >= d, pltpu.roll(x_i, d, 0), 0.0)
                x_r, x_i = x_r + p_r * s_r - p_i * s_i, x_i + p_r * s_i + p_i * s_r
            x_r, x_i = x_r + tab_r * cr - tab_i * ci, x_i + tab_r * ci + tab_i * cr
            outs[0][pl.ds(r0, 8), :] = x_r
            outs[1][pl.ds(r0, 8), :] = x_i
            if want_prev:
                outs[2][pl.ds(r0, 8), :] = jnp.where(row >= 1, pltpu.roll(x_r, 1, 0), cr)
                outs[3][pl.ds(r0, 8), :] = jnp.where(row >= 1, pltpu.roll(x_i, 1, 0), ci)
            e = 0 if rev else 7
            return jnp.broadcast_to(x_r[e:e + 1, :], blk8), jnp.broadcast_to(x_i[e:e + 1, :], blk8)

        lax.fori_loop(0, nb, step, (jnp.zeros(blk8, f32), jnp.zeros(blk8, f32)))

    per = 512 // S5_SCAN_LANES
    col = pl.BlockSpec((L, S5_SCAN_LANES), lambda q, z: (0, q))
    aspec = pl.BlockSpec((None, 1, S5_SCAN_LANES), lambda q, z: (q // per, 0, q % per))
    return pl.pallas_call(
        body, name=name, grid=(W // S5_SCAN_LANES, 1), in_specs=[col, col, aspec, aspec], out_specs=[col] * n_out,
        out_shape=[jax.ShapeDtypeStruct((L, W), f32)] * n_out, compiler_params=_params(2),
    )(xr, xi, ar, ai)


def _blockdiag(t, n_in, n_out):
    t4 = t.reshape(8, 8, n_in, n_out)
    return jnp.einsum("jaio,ab->jaibo", t4, jnp.eye(8, dtype=t.dtype)).reshape(8, 8 * n_in, 8 * n_out)


def _blockdiag_t(w, n_in, n_out):
    w5 = w.reshape(8, 8, n_in, 8, n_out)
    return jnp.einsum("jaibo,ab->jaio", w5, jnp.eye(8, dtype=w.dtype)).reshape(64, n_in, n_out)


def _glu(ag, h):
    n = ag.shape[1] // 2
    return h + ag[:, :n] * jax.nn.sigmoid(ag[:, n:])


def _s5_fwd(h, g_norm, w_in, lam_re, lam_im, log_dt, b_re, b_im, c_re, c_im, d_skip, w_out_g):
    L, D = h.shape
    tm, te = _mm_rows(L), _row_tile(L)
    W = 8 * 512
    hn = _rms_fwd("mix_norm", h, g_norm)
    u = _mm_plain("s5_in", hn, w_in, _NN)
    lre, lim = lam_re.reshape(8, 1, 512), lam_im.reshape(8, 1, 512)
    ldt = jnp.pad(log_dt.reshape(8, 1, 8), ((0, 0), (0, 0), (0, 120)))
    wbr = _blockdiag(b_re.transpose(0, 2, 1), 16, 64)
    wbi = _blockdiag(b_im.transpose(0, 2, 1), 16, 64)
    wcr = _blockdiag(c_re.transpose(0, 2, 1), 64, 16)
    wci = _blockdiag(c_im.transpose(0, 2, 1), 64, 16)
    jb = lambda shape: (shape, lambda j, z: (j, 0, 0))
    par_ins = [(lre, *jb((None, 1, 512))), (lim, *jb((None, 1, 512))), (ldt, *jb((None, 1, 128))),
               (wbr, *jb((None, 128, 512))), (wbi, *jb((None, 128, 512)))]
    abr, abi, bbr, bbi = _ew("s5_params", _s5_params, par_ins,
                             [((8, 1, 512), f32, *jb((None, 1, 512)), None)] * 2
                             + [((8, 128, 512), f32, *jb((None, 128, 512)), None)] * 2, (8, 1))

    def bu(name, wb):
        return _mm(name, u, wb, dims=_NN, grid=(L // tm, 8, 1),
                   a_spec=pl.BlockSpec((tm, 128), lambda i, j, k: (i, j)),
                   b_spec=pl.BlockSpec((None, 128, 512), lambda i, j, k: (j, 0, 0)),
                   out_shape=jax.ShapeDtypeStruct((L, W), f32), out_spec=pl.BlockSpec((tm, 512), lambda i, j, k: (i, j)),
                   acc_shape=(tm, 512))

    bur, bui = bu("s5_bu", bbr), bu("s5_bu", bbi)
    sr, si, pr, pi = _s5_scan("s5_scan", bur, bui, abr, abi, False, True)
    d_row = d_skip.reshape(1, D)
    cspec = dict(a_spec=pl.BlockSpec((tm, 512), lambda i, j, k: (i, j)),
                 b_spec=pl.BlockSpec((None, 512, 128), lambda i, j, k: (j, 0, 0)),
                 out_shape=jax.ShapeDtypeStruct((L, D), f32), out_spec=pl.BlockSpec((tm, 128), lambda i, j, k: (i, j)),
                 acc_shape=(tm, 128))
    e128 = pl.BlockSpec((tm, 128), lambda i, j, k: (i, j))
    pre1 = _mm("s5_c_re", sr, wcr, dims=_NN, grid=(L // tm, 8, 1), **cspec)
    pre = _mm("s5_c_im", si, wci, dims=_NN, grid=(L // tm, 8, 1),
              aux=[(pre1, e128, "e"), (u, e128, "e"), (d_row, pl.BlockSpec((1, 128), lambda i, j, k: (0, j)), "e")],
              epi_fn=lambda acc, p1, uu, dd: p1 - acc + dd * uu, **cspec)
    ws = D // N_DEV * 2
    ag = _mm("s5_out", pre, w_out_g, dims=_NN, grid=(L // tm, N_DEV, 1),
             a_spec=pl.BlockSpec((tm, D), lambda i, j, k: (i, 0)),
             b_spec=pl.BlockSpec((None, None, D, ws), lambda i, j, k: (j, 0, 0, 0)),
             out_shape=jax.ShapeDtypeStruct((L, 2 * D), f32), out_spec=pl.BlockSpec((tm, ws), lambda i, j, k: (i, j)),
             a_fn=jax.nn.gelu, acc_shape=(tm, ws))
    h_out = _ew("s5_glu", _glu, [(ag, *_rows(te, 2 * D)), (h, *_rows(te, D))],
                [((L, D), f32, *_rows(te, D), None)], (L // te, 1))[0]
    saved = dict(hn=hn, u=u, par_ins=par_ins, abr=abr, abi=abi, bbr=bbr, bbi=bbi, wcr=wcr, wci=wci, sr=sr, si=si,
                 pr=pr, pi=pi, pre=pre, ag=ag, d_row=d_row)
    return h_out, saved


def _s5_bwd(dh, h, g_norm, w_in, w_out_g, sv):
    L, D = h.shape
    tm, te = _mm_rows(L), _row_tile(L)
    tk = min(512, L)
    W = 8 * 512
    ws = D // N_DEV * 2
    u, pre, d_row = sv["u"], sv["pre"], sv["d_row"]
    d_ag = _ew("s5_glu_bwd", lambda ag, hv, g: _vjp_fn(_glu, 2)(ag, hv, g)[0],
               [(sv["ag"], *_rows(te, 2 * D)), (h, *_rows(te, D)), (dh, *_rows(te, D))],
               [((L, 2 * D), f32, *_rows(te, 2 * D), None)], (L // te, 1))[0]
    tr = 512
    dw_out = _mm("s5_out_dw", pre, d_ag, dims=_TN, grid=(D // tr, N_DEV, L // tk),
                 a_spec=pl.BlockSpec((tk, tr), lambda i, j, k: (k, i)),
                 b_spec=pl.BlockSpec((tk, ws), lambda i, j, k: (k, j)),
                 out_shape=jax.ShapeDtypeStruct((N_DEV, 1, D, ws), f32),
                 out_spec=pl.BlockSpec((None, None, tr, ws), lambda i, j, k: (j, 0, i, 0)),
                 a_fn=jax.nn.gelu, acc_shape=(tr, ws))
    tn = 512
    dpre = _mm("s5_out_dx", d_ag, w_out_g, dims=_NT, grid=(L // tm, D // tn, N_DEV),
               a_spec=pl.BlockSpec((tm, ws), lambda i, j, k: (i, k)),
               b_spec=pl.BlockSpec((None, None, tn, ws), lambda i, j, k: (k, 0, j, 0)),
               out_shape=jax.ShapeDtypeStruct((L, D), f32), out_spec=pl.BlockSpec((tm, tn), lambda i, j, k: (i, j)),
               aux=[(pre, pl.BlockSpec((tm, tn), lambda i, j, k: (i, j)), "e")],
               epi_fn=lambda acc, p: _vjp_fn(jax.nn.gelu, 1)(p, acc)[0], acc_shape=(tm, tn))
    d_d = _ew("s5_dskip", lambda a, b: jnp.sum(a * b, axis=0, keepdims=True),
              [(dpre, *_rows(te, D)), (u, *_rows(te, D))], [((1, D), f32, *_const((1, D)), "all")], (L // te, 1))[0]
    neg = lambda acc: -acc
    dsspec = dict(dims=_NT, grid=(L // tm, 8, 1), a_spec=pl.BlockSpec((tm, 128), lambda i, j, k: (i, j)),
                  b_spec=pl.BlockSpec((None, 512, 128), lambda i, j, k: (j, 0, 0)),
                  out_shape=jax.ShapeDtypeStruct((L, W), f32), out_spec=pl.BlockSpec((tm, 512), lambda i, j, k: (i, j)),
                  acc_shape=(tm, 512))
    dsr = _mm("s5_c_re_dx", dpre, sv["wcr"], **dsspec)
    dsi = _mm("s5_c_im_dx", dpre, sv["wci"], epi_fn=neg, **dsspec)
    dwspec = dict(dims=_TN, grid=(8, 1, L // tk), a_spec=pl.BlockSpec((tk, 512), lambda i, j, k: (k, i)),
                  b_spec=pl.BlockSpec((tk, 128), lambda i, j, k: (k, i)),
                  out_shape=jax.ShapeDtypeStruct((8, 512, 128), f32),
                  out_spec=pl.BlockSpec((None, 512, 128), lambda i, j, k: (i, 0, 0)), acc_shape=(512, 128))
    dwcr = _mm("s5_c_re_dw", sv["sr"], dpre, **dwspec)
    dwci = _mm("s5_c_im_dw", sv["si"], dpre, epi_fn=neg, **dwspec)
    lr, li = _s5_scan("s5_scan_bwd", dsr, dsi, sv["abr"], sv["abi"], True, False)

    def da(lrv, liv, prv, piv):
        return (jnp.sum(lrv * prv + liv * piv, axis=0, keepdims=True),
                jnp.sum(liv * prv - lrv * piv, axis=0, keepdims=True))

    sblk = ((te, 512), lambda j, r: (r, j))
    dabr, dabi = _ew("s5_dlam", da, [(lr, *sblk), (li, *sblk), (sv["pr"], *sblk), (sv["pi"], *sblk)],
                     [((8, 1, 512), f32, (None, 1, 512), lambda j, r: (j, 0, 0), "inner")] * 2, (8, L // te))
    dbspec = dict(dims=_TN, grid=(8, 1, L // tk), a_spec=pl.BlockSpec((tk, 128), lambda i, j, k: (k, i)),
                  b_spec=pl.BlockSpec((tk, 512), lambda i, j, k: (k, i)),
                  out_shape=jax.ShapeDtypeStruct((8, 128, 512), f32),
                  out_spec=pl.BlockSpec((None, 128, 512), lambda i, j, k: (i, 0, 0)), acc_shape=(128, 512))
    dbbr = _mm("s5_bu_dw", u, lr, **dbspec)
    dbbi = _mm("s5_bu_dw", u, li, **dbspec)
    duspec = dict(dims=_NT, grid=(L // tm, 8, 1), a_spec=pl.BlockSpec((tm, 512), lambda i, j, k: (i, j)),
                  b_spec=pl.BlockSpec((None, 128, 512), lambda i, j, k: (j, 0, 0)),
                  out_shape=jax.ShapeDtypeStruct((L, D), f32), out_spec=pl.BlockSpec((tm, 128), lambda i, j, k: (i, j)),
                  acc_shape=(tm, 128))
    e128 = pl.BlockSpec((tm, 128), lambda i, j, k: (i, j))
    du1 = _mm("s5_bu_dx_re", lr, sv["bbr"], **duspec)
    du = _mm("s5_bu_dx_im", li, sv["bbi"],
             aux=[(du1, e128, "e"), (dpre, e128, "e"), (d_row, pl.BlockSpec((1, 128), lambda i, j, k: (0, j)), "e")],
             epi_fn=lambda acc, d1, dp, dd: acc + d1 + dp * dd, **duspec)
    jb = lambda shape: (shape, lambda j, z: (j, 0, 0))
    cts = [(dabr, *jb((None, 1, 512))), (dabi, *jb((None, 1, 512))), (dbbr, *jb((None, 128, 512))),
           (dbbi, *jb((None, 128, 512)))]
    dlre, dlim, dldt, dwbr, dwbi = _ew(
        "s5_params_bwd", _vjp_fn(_s5_params, 5), sv["par_ins"] + cts,
        [((8, 1, 512), f32, *jb((None, 1, 512)), None)] * 2 + [((8, 1, 128), f32, *jb((None, 1, 128)), None)]
        + [((8, 128, 512), f32, *jb((None, 128, 512)), None)] * 2, (8, 1))
    dw_in = _mm_plain("s5_in_dw", sv["hn"], du, _TN)
    dhn = _mm_plain("s5_in_dx", du, w_in, _NT)
    dh_in, dg = _rms_bwd("mix_norm_bwd", h, g_norm, dhn, dh)
    grads = dict(norm=dg, w_in=dw_in, lam_re=dlre.reshape(64, 64), lam_im=dlim.reshape(64, 64),
                 log_dt=dldt[:, 0, :8].reshape(64),
                 b_re=_blockdiag_t(dwbr, 16, 64).transpose(0, 2, 1), b_im=_blockdiag_t(dwbi, 16, 64).transpose(0, 2, 1),
                 c_re=_blockdiag_t(dwcr, 64, 16).transpose(0, 2, 1), c_im=_blockdiag_t(dwci, 64, 16).transpose(0, 2, 1),
                 d=d_d[0], w_out=dw_out)
    return dh_in, grads


def _m2_act(dt_raw, dtbias, alog):
    dt = jax.nn.softplus(dt_raw + dtbias)
    da = dt * (-jnp.exp(alog))
    sel = ((_iota((128, M2_INNER), 1) // 64) == _iota((128, M2_INNER), 0)).astype(f32)
    return _dot(dt, sel), _dot(da, sel)


def _m2_dexp(d):
    return _expand_lanes(d, M2_INNER, 64)


def _ssd_chunk(states, consts, vals, dots):
    (dsk,) = consts
    S = list(states)
    n = len(S)
    cut = [slice(128 * i, 128 * i + 128) for i in range(n)]
    x, dtb, dab = ([t[:, c] for c in cut] for t in vals[:3])
    dsk = [dsk[:, c] for c in cut]
    B = [vals[3][:, cut[i // 2]] for i in range(n)]
    Cm = [vals[4][:, cut[i // 2]] for i in range(n)]
    C = vals[0].shape[0]
    row, col = _iota((C, C), 0), _iota((C, C), 1)
    causal = row >= col
    ltri = causal.astype(f32)
    lane = _iota((C, 128), 1)
    last = _iota((C, 128), 0) == C - 1
    eye128 = _iota((128, 128), 0) == _iota((128, 128), 1)
    head = [jnp.logical_and(lane >= 64 * hh, lane < 64 * hh + 64) for hh in range(2)]
    pick = [(lane == 64 * hh).astype(f32) for hh in range(2)]
    xdt = _each(lambda a, b: a * b, x, dtb)
    cb = _each(lambda c, b: _dotb(c, b, _NT), Cm[::2], B[::2])
    cum = _each(lambda a: dots.dot01(ltri, a), dab)
    clast = _each(lambda a: jnp.sum(jnp.where(last, a, 0.0), axis=0, keepdims=True), cum)
    st = _each(lambda a, cl, cu, b: _dotb(a * jnp.exp(cl - cu), b, _TN), xdt, clast, cum, B)
    y = _each(lambda c, s, cu: _dotb(c, s, _NT) * jnp.exp(cu), Cm, S, cum)
    for hh in range(2):
        ccol = _each(lambda cu: jnp.sum(jnp.where(head[hh], cu, 0.0), axis=1, keepdims=True) * (1.0 / 64), cum)
        crow = _each(lambda cu: dots.dot01(pick[hh], cu, _NT), cum)
        lm = _each(lambda a, b: jnp.where(causal, jnp.exp(jnp.where(causal, a - b, 0.0)), 0.0), ccol, crow)
        y = [y[i] + _dotb(cb[i // 2] * lm[i], jnp.where(head[hh], xdt[i], 0.0)) for i in range(n)]
    cdcol = _each(lambda cl: jnp.sum(jnp.where(eye128, jnp.broadcast_to(jnp.exp(cl), (128, 128)), 0.0),
                                     axis=1, keepdims=True), clast)
    S_new = _each(lambda c, s, t: c * s + t, cdcol, S, st)
    out = _each(lambda a, d, b: a + d * b, y, dsk, x)
    return S_new, [jnp.concatenate(out, axis=1)]


def _m2_post(yc, z, ng):
    return _rms(yc * (z * jax.nn.sigmoid(z)), ng)


def _m2_fwd(h, g_norm, w_ext, conv_w, conv_b, dt_bias, a_log, d_skip, norm_g, w_out):
    L, D = h.shape
    tm = _row_tile(L)
    nc = L // CHUNK
    NI = M2_INNER
    hn = _rms_fwd("mix_norm", h, g_norm)
    proj = _mm_plain("m2_in", hn, w_ext, _NN)
    xbc = _conv_fwd("m2_conv", proj, NI // 128, conv_w, conv_b)
    dtb_row, alog_row, d_pad = _pad_row(dt_bias), _pad_row(a_log), _pad_row(d_skip)
    act_ins = [(proj, (tm, 128), lambda r, z: (r, 3 * NI // 128)), (dtb_row, *_const((1, 128))),
               (alog_row, *_const((1, 128)))]
    dtb, dab = _ew("m2_act", _m2_act, act_ins, [((L, NI), f32, *_rows(tm, NI), None)] * 2, (L // tm, 1))
    dsk = _ew("m2_dexp", _m2_dexp, [(d_pad, *_const((1, 128)))], [((1, NI), f32, *_const((1, NI)), None)], (1, 1))[0]
    GB = M2_GB
    x_blk, bc_blk = (CHUNK, 256 * GB), (CHUNK, 128 * GB)
    cins = [(dsk, (1, 256 * GB), lambda u, c: (0, u))]
    core_ins = [(xbc, x_blk, lambda u, c: (c, u), (L, NI), lambda u, c: (c, u)),
                (dtb, x_blk, lambda u, c: (c, u)), (dab, x_blk, lambda u, c: (c, u)),
                (xbc, bc_blk, lambda u, c: (c, 16 // GB + u), (L, D), lambda u, c: (c, u)),
                (xbc, bc_blk, lambda u, c: (c, 24 // GB + u), (L, D), lambda u, c: (c, u))]
    (yc,), saved_s = _scan_fwd("m2_core", _ssd_chunk, 2 * GB, (128, 128), cins, core_ins,
                               [((L, NI), x_blk, lambda u, c: (c, u))], 8 // GB, nc)
    gblk = ((tm, 256), lambda g, r: (r, g))
    post_ins = [(yc, *gblk), (proj, *gblk), (norm_g, (1, 256), lambda g, r: (0, g))]
    yn = _ew("m2_post", _m2_post, post_ins, [((L, NI), f32, *gblk, None)], (8, L // tm))[0]
    h_out = _mm_plain("m2_out", yn, w_out, _NN, epi_fn=lambda acc, res: acc + res, aux=[(h, "e")])
    saved = dict(hn=hn, proj=proj, act_ins=act_ins, d_pad=d_pad, cins=cins, core_ins=core_ins, saved_s=saved_s,
                 post_ins=post_ins, yn=yn)
    return h_out, saved


def _m2_bwd(dh, h, g_norm, w_ext, conv_w, conv_b, w_out, sv):
    L, D = h.shape
    tm = _row_tile(L)
    nc = L // CHUNK
    NI = M2_INNER
    dyn = _mm_plain("m2_out_dx", dh, w_out, _NT)
    dw_out = _mm_plain("m2_out_dw", sv["yn"], dh, _TN)
    gblk = ((tm, 256), lambda g, r: (r, g))
    d_yc, d_z, d_ng = _ew("m2_post_bwd", _vjp_fn(_m2_post, 3), sv["post_ins"] + [(dyn, *gblk)],
                          [((L, NI), f32, *gblk, None)] * 2 + [((1, NI), f32, (1, 256), lambda g, r: (0, g), "inner")],
                          (8, L // tm))
    (d_dsk,), (dx, d_dtb, d_dab, dB, dC) = _scan_bwd(
        "m2_core_bwd", _ssd_chunk, 2 * M2_GB, (128, 128), sv["cins"], sv["core_ins"], sv["saved_s"],
        [(d_yc, (CHUNK, 256 * M2_GB), lambda u, c: (c, u))], 8 // M2_GB, nc)
    row128 = ((1, 128), f32, *_const((1, 128)), "all")
    d_dt_raw, d_dtbias, d_alog = _ew(
        "m2_act_bwd", _vjp_fn(_m2_act, 3), sv["act_ins"] + [(d_dtb, *_rows(tm, NI)), (d_dab, *_rows(tm, NI))],
        [((L, 128), f32, *_rows(tm, 128), None), row128, row128], (L // tm, 1))
    d_d = _ew("m2_dexp_bwd", _vjp_fn(_m2_dexp, 1), [(sv["d_pad"], *_const((1, 128))), (d_dsk, *_const((1, NI)))],
              [((1, 128), f32, *_const((1, 128)), None)], (1, 1))[0]
    d_conv_out = jnp.concatenate([dx, dB, dC], axis=1)
    d_conv_in, d_conv_w, d_conv_b = _conv_bwd("m2_conv_bwd", sv["proj"], NI // 128, conv_w, conv_b, d_conv_out)
    d_proj = jnp.concatenate([d_z, d_conv_in, d_dt_raw], axis=1)
    dw_ext = _mm_plain("m2_in_dw", sv["hn"], d_proj, _TN)
    dhn = _mm_plain("m2_in_dx", d_proj, w_ext, _NT)
    dh_in, dg = _rms_bwd("mix_norm_bwd", h, g_norm, dhn, dh)
    grads = dict(norm=dg, w_ext=dw_ext, conv_w=d_conv_w, conv_b=d_conv_b, dt_bias=d_dtbias[0, :M2_HEADS],
                 a_log=d_alog[0, :M2_HEADS], d=d_d[0, :M2_HEADS], norm_g=d_ng, w_out=dw_out)
    return dh_in, grads


def _mesh_pos():
    return lax.axis_index("x"), lax.axis_index("y"), lax.axis_index("c")


def _flip(pos, p):
    x, y, c = pos
    return (1 - x if p & 4 else x, 1 - y if p & 2 else y, 1 - c if p & 1 else c)


def _index(pos):
    return 4 * pos[0] + 2 * pos[1] + pos[2]


def _comm_call(name, body, arrays, out_shape, n_sem):
    n = len(arrays)
    hbm = pl.BlockSpec(memory_space=pl.ANY)
    return pl.pallas_call(
        body, name=name, in_specs=[hbm] * n, out_specs=[hbm] * len(out_shape), out_shape=out_shape,
        scratch_shapes=[pltpu.SemaphoreType.DMA((n, n_sem)), pltpu.SemaphoreType.DMA((n, n_sem)),
                        pltpu.SemaphoreType.DMA((n, 4))],
    )(*arrays)


def _gather(name, arrays):
    n = len(arrays)

    def body(*refs):
        ins, outs = refs[:n], refs[n:2 * n]
        send_sems, recv_sems, loc_sems = refs[2 * n:]
        me = _mesh_pos()
        c = me[2]
        sib = _flip(me, 1)
        chips = [_flip(me, 4), _flip(me, 2), _flip(me, 6)]

        def copy(w, k, block, to, src=None):
            slab = outs[w].at[_index(block)]
            return pltpu.make_async_remote_copy(
                src_ref=slab if src is None else src, dst_ref=slab, send_sem=send_sems.at[w, k],
                recv_sem=recv_sems.at[w, k], device_id=to, device_id_type=MESH)

        local = [pltpu.make_async_copy(ins[w], outs[w].at[_index(me)], loc_sems.at[w, 0]) for w in range(n)]
        for cp in local:
            cp.start()
        first = [copy(w, 0, me, sib, src=ins[w]) for w in range(n)]
        first += [copy(w, 1 + j, me, chip, src=ins[w]) for j, chip in enumerate(chips) for w in range(n)]
        for cp in first:
            cp.start()
        passed = []
        for j, chip in enumerate(chips):
            for w in range(n):
                copy(w, 1 + j, chip, me).wait_recv()
                fwd = copy(w, 4 + j, chip, sib)
                fwd.start()
                passed.append(fwd)
        for w in range(n):
            copy(w, 0, sib, me).wait_recv()
        for j, chip in enumerate(chips):
            for w in range(n):
                copy(w, 4 + j, (chip[0], chip[1], 1 - c), me).wait_recv()
        for cp in first + passed:
            cp.wait_send()
        for cp in local:
            cp.wait()

    out_shape = [jax.ShapeDtypeStruct((N_DEV,) + a.shape, a.dtype) for a in arrays]
    return _comm_call(name, body, arrays, out_shape, N_DEV - 1)


def _scatter_pair(name, arrays):
    n = len(arrays)

    def body(*refs):
        ins, outs = refs[:n], refs[n:2 * n]
        send_sems, recv_sems, _ = refs[2 * n:]
        me = _mesh_pos()
        c = me[2]
        sib = _flip(me, 1)

        def copy(w, q):
            return pltpu.make_async_remote_copy(
                src_ref=ins[w].at[2 * q + 1 - c], dst_ref=outs[w].at[q], send_sem=send_sems.at[w, q],
                recv_sem=recv_sems.at[w, q], device_id=sib, device_id_type=MESH)

        cps = [copy(w, q) for q in range(4) for w in range(n)]
        for cp in cps:
            cp.start()
        for cp in cps:
            cp.wait()

    out_shape = [jax.ShapeDtypeStruct((4,) + a.shape[1:], a.dtype) for a in arrays]
    return _comm_call(name, body, arrays, out_shape, 4)


def _pair_add(name, full, theirs, core, dtype):
    _, R, C = theirs.shape
    tr = R if R <= 256 else (256 if C <= 512 else 128)

    def body(core_ref, mine_ref, theirs_ref, o_ref):
        o_ref[...] = (mine_ref[...] + theirs_ref[...]).astype(o_ref.dtype)

    blk = pl.BlockSpec((4, tr, C), lambda r, cr: (0, r, 0))
    grid_spec = pltpu.PrefetchScalarGridSpec(
        num_scalar_prefetch=1, grid=(R // tr,),
        in_specs=[pl.BlockSpec((4, None, tr, C), lambda r, cr: (0, cr[0], r, 0)), blk], out_specs=blk)
    return pl.pallas_call(
        body, name=name, grid_spec=grid_spec, out_shape=jax.ShapeDtypeStruct((4, R, C), dtype),
        compiler_params=_params(1),
    )(core.reshape(1).astype(jnp.int32), full.reshape(4, 2, R, C), theirs)


def _scatter_chips(name, arrays):
    n = len(arrays)

    def body(*refs):
        ins, outs = refs[:n], refs[n:2 * n]
        send_sems, recv_sems, loc_sems = refs[2 * n:]
        me = _mesh_pos()
        mq = 2 * me[0] + me[1]
        peers = [_flip(me, 4), _flip(me, 2), _flip(me, 6)]

        def copy(w, k):
            peer = peers[k]
            return pltpu.make_async_remote_copy(
                src_ref=ins[w].at[2 * peer[0] + peer[1]], dst_ref=outs[w].at[mq], send_sem=send_sems.at[w, k],
                recv_sem=recv_sems.at[w, k], device_id=peer, device_id_type=MESH)

        def arrival(w, k):
            peer = peers[k]
            return pltpu.make_async_remote_copy(
                src_ref=ins[w].at[mq], dst_ref=outs[w].at[2 * peer[0] + peer[1]], send_sem=send_sems.at[w, k],
                recv_sem=recv_sems.at[w, k], device_id=peer, device_id_type=MESH)

        local = [pltpu.make_async_copy(ins[w].at[mq], outs[w].at[mq], loc_sems.at[w, 0]) for w in range(n)]
        for cp in local:
            cp.start()
        sends = [copy(w, k) for k in range(3) for w in range(n)]
        for cp in sends:
            cp.start()
        for k in range(3):
            for w in range(n):
                arrival(w, k).wait_recv()
        for cp in sends:
            cp.wait_send()
        for cp in local:
            cp.wait()

    out_shape = [jax.ShapeDtypeStruct(a.shape, a.dtype) for a in arrays]
    return _comm_call(name, body, arrays, out_shape, 3)


def _adamw(name, parts, w, m, v):
    R, C = w.shape
    n_parts = parts.shape[0]
    tr = R if R <= 256 else (256 if C <= 512 else 128)
    bc1 = 1.0 - ADAM_B1 ** ADAM_STEP
    bc2 = 1.0 - ADAM_B2 ** ADAM_STEP

    def f(p, wv, mv, vv):
        g = p[0].astype(f32)
        for i in range(1, n_parts):
            g = g + p[i].astype(f32)
        m2 = ADAM_B1 * mv + (1.0 - ADAM_B1) * g
        v2 = ADAM_B2 * vv + (1.0 - ADAM_B2) * jnp.square(g)
        delta = -ADAM_LR * ((m2 / bc1) / (jnp.sqrt(v2 / bc2) + ADAM_EPS) + ADAM_WD * wv)
        return g, delta, m2, v2

    blk = ((tr, C), lambda r, z: (r, 0))
    return _ew(name, f, [(parts, (n_parts, tr, C), lambda r, z: (0, r, 0)), (w, *blk), (m, *blk), (v, *blk)],
               [((R, C), f32, *blk, None)] * 4, (R // tr, 1))


_WEIGHTS = ["norm_mix_g", "norm_mlp_g", "mlp_w1", "mlp_w2", "gdn_w_in", "gdn_conv_w", "gdn_a_log", "gdn_dt_bias",
            "gdn_o_norm_g", "gdn_w_out", "s5_w_in", "s5_lam_re", "s5_lam_im", "s5_log_dt", "s5_b_re", "s5_b_im",
            "s5_c_re", "s5_c_im", "s5_d", "s5_w_out", "m2_w_in", "m2_conv_w", "m2_conv_b", "m2_dt_bias", "m2_a_log",
            "m2_d", "m2_norm_g", "m2_w_out", "final_norm_g"]
_SHARDED = ["mlp_w1", "mlp_w2", "gdn_w_in", "gdn_w_out", "s5_w_in", "s5_w_out", "m2_w_in", "m2_w_out",
            "gdn_conv_w", "m2_conv_w", "m2_conv_b", "m2_norm_g"]
_MATRICES = _SHARDED[:8]
_REPLICATED = [n for n in _WEIGHTS if n not in _SHARDED]
_GDN_IN, _M2_IN = 4112, 6176
_LAYER_KIND = (0, 1, 2, 0)


def _as2d(a):
    return a.reshape(-1, a.shape[-1])


def _cols_from_shards(g, width):
    return g.transpose(1, 0, 2).reshape(g.shape[1], width)


def _cols_to_shards(a, width):
    return a[:, :width].reshape(a.shape[0], N_DEV, width // N_DEV).transpose(1, 0, 2)


def kernel(x, norm_mix_g, norm_mlp_g, mlp_w1, mlp_w2, gdn_w_in, gdn_conv_w, gdn_a_log, gdn_dt_bias, gdn_o_norm_g, gdn_w_out, s5_w_in, s5_lam_re, s5_lam_im, s5_log_dt, s5_b_re, s5_b_im, s5_c_re, s5_c_im, s5_d, s5_w_out, m2_w_in, m2_conv_w, m2_conv_b, m2_dt_bias, m2_a_log, m2_d, m2_norm_g, m2_w_out, final_norm_g, loss_target, m_norm_mix_g, m_norm_mlp_g, m_mlp_w1, m_mlp_w2, m_gdn_w_in, m_gdn_conv_w, m_gdn_a_log, m_gdn_dt_bias, m_gdn_o_norm_g, m_gdn_w_out, m_s5_w_in, m_s5_lam_re, m_s5_lam_im, m_s5_log_dt, m_s5_b_re, m_s5_b_im, m_s5_c_re, m_s5_c_im, m_s5_d, m_s5_w_out, m_m2_w_in, m_m2_conv_w, m_m2_conv_b, m_m2_dt_bias, m_m2_a_log, m_m2_d, m_m2_norm_g, m_m2_w_out, m_final_norm_g, v_norm_mix_g, v_norm_mlp_g, v_mlp_w1, v_mlp_w2, v_gdn_w_in, v_gdn_conv_w, v_gdn_a_log, v_gdn_dt_bias, v_gdn_o_norm_g, v_gdn_w_out, v_s5_w_in, v_s5_lam_re, v_s5_lam_im, v_s5_log_dt, v_s5_b_re, v_s5_b_im, v_s5_c_re, v_s5_c_im, v_s5_d, v_s5_w_out, v_m2_w_in, v_m2_conv_w, v_m2_conv_b, v_m2_dt_bias, v_m2_a_log, v_m2_d, v_m2_norm_g, v_m2_w_out, v_final_norm_g):
    args = locals()
    W = {n: args[n] for n in _WEIGHTS}
    MOM = {n: args["m_" + n] for n in _WEIGHTS}
    VAR = {n: args["v_" + n] for n in _WEIGHTS}
    h = x[0]
    target = loss_target[0]
    L, D = h.shape

    sends = [W[n].astype(bf16) if n in _MATRICES else _as2d(W[n]) for n in _SHARDED]
    G = dict(zip(_SHARDED, _gather("gather_weights", sends)))
    w1g, w2g = G["mlp_w1"], G["mlp_w2"]
    gdn_in = [jnp.pad(_cols_from_shards(G["gdn_w_in"][:, j], _GDN_IN), ((0, 0), (0, GDN_EXT - _GDN_IN))) for j in range(2)]
    gdn_out = [G["gdn_w_out"][:, j].reshape(D, D) for j in range(2)]
    gdn_conv = [_cols_from_shards(G["gdn_conv_w"][:, 4 * j:4 * j + 4], 3 * D) for j in range(2)]
    s5_in = G["s5_w_in"].reshape(D, D)
    s5_out_g = G["s5_w_out"]
    m2_in = jnp.pad(_cols_from_shards(G["m2_w_in"][:, 0], _M2_IN), ((0, 0), (0, M2_EXT - _M2_IN)))
    m2_out = G["m2_w_out"].reshape(M2_INNER, D)
    m2_conv = _cols_from_shards(G["m2_conv_w"], 2 * M2_INNER)
    m2_cb = _cols_from_shards(G["m2_conv_b"], 2 * M2_INNER)
    m2_ng = _cols_from_shards(G["m2_norm_g"], M2_INNER)

    def mixer_fwd(i, hv):
        kind, j = _LAYER_KIND[i], i // 3
        gn = norm_mix_g[i].reshape(1, D)
        if kind == 0:
            return _gdn_fwd(hv, gn, gdn_in[j], gdn_conv[j], gdn_a_log[j], gdn_dt_bias[j], gdn_o_norm_g[j], gdn_out[j])
        if kind == 1:
            return _s5_fwd(hv, gn, s5_in, s5_lam_re[0], s5_lam_im[0], s5_log_dt[0], s5_b_re[0], s5_b_im[0],
                           s5_c_re[0], s5_c_im[0], s5_d[0], s5_out_g)
        return _m2_fwd(hv, gn, m2_in, m2_conv, m2_cb, m2_dt_bias[0], m2_a_log[0], m2_d[0], m2_ng, m2_out)

    def mixer_bwd(i, dh, hv, sv):
        kind, j = _LAYER_KIND[i], i // 3
        gn = norm_mix_g[i].reshape(1, D)
        if kind == 0:
            return _gdn_bwd(dh, hv, gn, gdn_in[j], gdn_conv[j], gdn_out[j], sv)
        if kind == 1:
            return _s5_bwd(dh, hv, gn, s5_in, s5_out_g, sv)
        return _m2_bwd(dh, hv, gn, m2_in, m2_conv, m2_cb, m2_out, sv)

    tape = []
    for i in range(4):
        h_mid, sv = mixer_fwd(i, h)
        h_next, hn, h1 = _mlp_fwd(h_mid, norm_mlp_g[i].reshape(1, D), w1g, w2g, i)
        tape.append((h, sv, h_mid, hn, h1))
        h = h_next
    loss_row, dh, d_final = _loss_head(h, final_norm_g.reshape(1, D), target)
    loss = lax.psum(loss_row[0, 0], ("x", "y", "c"))

    dw1 = lax.empty((N_DEV, 4, D, D_FF // N_DEV), f32)
    dw2 = lax.empty((N_DEV, 4, D_FF // N_DEV, D), f32)
    d_mix, d_mlp, mg = [None] * 4, [None] * 4, [None] * 4
    for i in reversed(range(4)):
        h_in, sv, h_mid, hn, h1 = tape[i]
        dh, d_mlp[i], dw1, dw2 = _mlp_bwd(dh, h_mid, norm_mlp_g[i].reshape(1, D), hn, h1, w1g, w2g, i, dw1, dw2)
        dh, mg[i] = mixer_bwd(i, dh, h_in, sv)
        d_mix[i] = mg[i]["norm"]
    grad_x = dh.reshape(1, L, D)
    ga, gb_, s5g, m2g = mg[0], mg[3], mg[1], mg[2]

    full = {
        "mlp_w1": dw1, "mlp_w2": dw2,
        "gdn_w_in": jnp.stack([_cols_to_shards(g["w_ext"], _GDN_IN) for g in (ga, gb_)], axis=1),
        "gdn_w_out": jnp.stack([g["w_out"].reshape(N_DEV, D // N_DEV, D) for g in (ga, gb_)], axis=1),
        "s5_w_in": s5g["w_in"].reshape(N_DEV, 1, D // N_DEV, D), "s5_w_out": s5g["w_out"],
        "m2_w_in": _cols_to_shards(m2g["w_ext"], _M2_IN)[:, None],
        "m2_w_out": m2g["w_out"].reshape(N_DEV, 1, M2_INNER // N_DEV, D),
        "gdn_conv_w": jnp.concatenate([_cols_to_shards(g["conv_w"], 3 * D) for g in (ga, gb_)], axis=1),
        "m2_conv_w": _cols_to_shards(m2g["conv_w"], 2 * M2_INNER),
        "m2_conv_b": _cols_to_shards(m2g["conv_b"], 2 * M2_INNER),
        "m2_norm_g": _cols_to_shards(m2g["norm_g"], M2_INNER),
    }
    sends = [full[n].reshape((N_DEV,) + _as2d(W[n]).shape) for n in _SHARDED]
    core = lax.axis_index("c")
    theirs = _scatter_pair("scatter_pair", sends)
    chip_sums = [_pair_add("pair_add_" + n, full8, th, core, bf16 if n in _MATRICES else f32)
                 for n, full8, th in zip(_SHARDED, sends, theirs)]
    parts = dict(zip(_SHARDED, _scatter_chips("scatter_chips", chip_sums)))

    rep = {
        "norm_mix_g": jnp.concatenate(d_mix, axis=0), "norm_mlp_g": jnp.concatenate(d_mlp, axis=0),
        "gdn_a_log": jnp.stack([ga["a_log"], gb_["a_log"]]), "gdn_dt_bias": jnp.stack([ga["dt_bias"], gb_["dt_bias"]]),
        "gdn_o_norm_g": jnp.stack([ga["o_norm_g"], gb_["o_norm_g"]]),
        "s5_lam_re": s5g["lam_re"], "s5_lam_im": s5g["lam_im"], "s5_log_dt": s5g["log_dt"], "s5_b_re": s5g["b_re"],
        "s5_b_im": s5g["b_im"], "s5_c_re": s5g["c_re"], "s5_c_im": s5g["c_im"], "s5_d": s5g["d"],
        "m2_dt_bias": m2g["dt_bias"], "m2_a_log": m2g["a_log"], "m2_d": m2g["d"], "final_norm_g": d_final,
    }

    def pack(d):
        flat = jnp.concatenate([d[n].reshape(-1).astype(f32) for n in _REPLICATED])
        return jnp.pad(flat, (0, -flat.shape[0] % (256 * 128))).reshape(-1, 128)

    (rep_parts,) = _gather("gather_small_grads", [pack(rep)])

    res = {}
    for n in _SHARDED:
        w2d = _as2d(W[n])
        out = _adamw("adamw_" + n, parts[n], w2d, _as2d(MOM[n]), _as2d(VAR[n]))
        res[n] = [o.reshape(W[n].shape) for o in out]
    out = _adamw("adamw_replicated", rep_parts, pack(W), pack(MOM), pack(VAR))
    off = 0
    for n in _REPLICATED:
        size = W[n].size
        res[n] = [o.reshape(-1)[off:off + size].reshape(W[n].shape) for o in out]
        off += size

    return (loss, grad_x, *[res[n][0] for n in _WEIGHTS], *[res[n][1] for n in _WEIGHTS],
            *[res[n][2] for n in _WEIGHTS], *[res[n][3] for n in _WEIGHTS])
```

```python
import functools

import jax
import jax.numpy as jnp
from jax import lax
from jax.experimental import pallas as pl
from jax.experimental.pallas import tpu as pltpu

f32 = jnp.float32
bf16 = jnp.bfloat16
HI = lax.Precision.HIGHEST
MESH = pl.DeviceIdType.MESH

N_DEV = 8
D_MODEL = 1024
D_FF = 4096
CHUNK = 64
RMS_EPS = 1e-6
GDN_HEADS = 8
GDN_HB = 8
GDN_EXT = 4224
S5_STATE = 64
S5_SCAN_LANES = 256
M2_INNER = 2048
M2_EXT = 6272
M2_HEADS = 32
M2_GB = 4
VMEM_LIMIT_BYTES = 56 * 1024 * 1024

ADAM_LR, ADAM_B1, ADAM_B2, ADAM_EPS, ADAM_WD, ADAM_STEP = 0.001, 0.9, 0.999, 1e-08, 0.01, 10

_NN = ((1,), (0,))
_NT = ((1,), (1,))
_TN = ((0,), (0,))


def _dot(a, b, dims=_NN):
    return lax.dot_general(a, b, (dims, ((), ())), precision=HI, preferred_element_type=f32)


def _dotb(a, b, dims=_NN):
    return lax.dot_general(a.astype(bf16), b.astype(bf16), (dims, ((), ())), preferred_element_type=f32)


def _bdot(p, q, dims):
    return lax.dot_general(p, q, (dims, ((), ())), preferred_element_type=f32)


def _pieces(x, n):
    out = []
    for _ in range(n - 1):
        p = x.astype(bf16)
        out.append(p)
        x = x - p.astype(f32)
    return out + [x.astype(bf16)]


def _dot01_raw(mask, b, dims=_NN, mask_first=True):
    m = mask.astype(bf16)
    p = _pieces(b, 3)
    if mask_first:
        return _bdot(m, p[0], dims) + (_bdot(m, p[1], dims) + _bdot(m, p[2], dims))
    return _bdot(p[0], m, dims) + (_bdot(p[1], m, dims) + _bdot(p[2], m, dims))


@jax.custom_vjp
def _dot01_nn(mask, b):
    return _dot01_raw(mask, b, _NN)


@jax.custom_vjp
def _dot01_nt(mask, b):
    return _dot01_raw(mask, b, _NT)


_dot01_nn.defvjp(lambda m, b: (_dot01_raw(m, b, _NN), m),
                 lambda m, ct: (jnp.zeros_like(m), _dot01_raw(m, ct, _TN, mask_first=True)))
_dot01_nt.defvjp(lambda m, b: (_dot01_raw(m, b, _NT), m),
                 lambda m, ct: (jnp.zeros_like(m), _dot01_raw(m, ct, _TN, mask_first=False)))


def _dot01_vjp(mask, b, dims=_NN):
    return _dot01_nn(mask, b) if dims == _NN else _dot01_nt(mask, b)


def _dot3_raw(a, b, dims=_NN):
    (ah, al), (bh, bl) = _pieces(a, 2), _pieces(b, 2)
    return _bdot(ah, bh, dims) + (_bdot(ah, bl, dims) + _bdot(al, bh, dims))


@jax.custom_vjp
def _dot3_vjp(a, b):
    return _dot3_raw(a, b)


_dot3_vjp.defvjp(lambda a, b: (_dot3_raw(a, b), (a, b)),
                 lambda res, ct: (_dot3_raw(ct, res[1], _NT), _dot3_raw(res[0], ct, _TN)))


class _Dots:
    def __init__(self, dot3, dot01):
        self.dot3, self.dot01 = dot3, dot01


_PLAIN_DOTS = _Dots(_dot3_raw, _dot01_raw)
_VJP_DOTS = _Dots(_dot3_vjp, _dot01_vjp)


def _iota(shape, dim):
    return lax.broadcasted_iota(jnp.int32, shape, dim)


def _params(n_grid):
    return pltpu.CompilerParams(dimension_semantics=("arbitrary",) * n_grid, vmem_limit_bytes=VMEM_LIMIT_BYTES)


def _row_tile(n_rows):
    return min(512, n_rows)


def _head_rows(n_rows):
    return min(2048, n_rows)


def _mm_rows(n_rows):
    return min(1024, n_rows)


def _col_tile(n, cap=1024):
    best = 128
    for t in range(128, cap + 1, 128):
        if n % t == 0:
            best = t
    return best


def _mm(name, a, b, *, dims, grid, a_spec, b_spec, out_shape, out_spec, aux=(), a_fn=None, epi_fn=None,
        acc_shape, out_init=None):
    nk = grid[2]
    n_aux = len(aux)
    kinds = [x[2] for x in aux]

    def body(*refs):
        a_ref, b_ref = refs[0], refs[1]
        aux_refs = refs[2:2 + n_aux]
        pos = 2 + n_aux + (1 if out_init is not None else 0)
        o_ref, acc_ref = refs[pos], refs[pos + 1]
        k = pl.program_id(2)

        @pl.when(k == 0)
        def _():
            acc_ref[...] = jnp.zeros_like(acc_ref)

        av = a_ref[...]
        if a_fn is not None:
            av = a_fn(av, *[r[...] for r, kd in zip(aux_refs, kinds) if kd == "a"])
        acc_ref[...] += lax.dot_general(av.astype(bf16), b_ref[...].astype(bf16), (dims, ((), ())),
                                        preferred_element_type=f32)

        @pl.when(k == nk - 1)
        def _():
            r = acc_ref[...]
            if epi_fn is not None:
                r = epi_fn(r, *[x[...] for x, kd in zip(aux_refs, kinds) if kd == "e"])
            o_ref[...] = r.astype(o_ref.dtype)

    in_specs = [a_spec, b_spec] + [x[1] for x in aux]
    args = [a, b] + [x[0] for x in aux]
    aliases = {}
    if out_init is not None:
        in_specs.append(pl.BlockSpec(memory_space=pl.ANY))
        args.append(out_init)
        aliases = {len(args) - 1: 0}
    return pl.pallas_call(
        body, name=name, grid=grid, in_specs=in_specs, out_specs=out_spec, out_shape=out_shape,
        scratch_shapes=[pltpu.VMEM(acc_shape, f32)], input_output_aliases=aliases,
        compiler_params=_params(3),
    )(*args)


def _ew(name, f, ins, outs, grid):
    n_in = len(ins)
    modes = [o[4] for o in outs]

    def body(*refs):
        vals = [r[...] for r in refs[:n_in]]
        res = f(*vals)
        if not isinstance(res, (tuple, list)):
            res = (res,)
        for r, o_ref, mode in zip(res, refs[n_in:], modes):
            if mode is None:
                o_ref[...] = r.astype(o_ref.dtype)
                continue
            first = pl.program_id(1) == 0
            if mode == "all":
                first = jnp.logical_and(first, pl.program_id(0) == 0)

            @pl.when(first)
            def _(r=r, o_ref=o_ref):
                o_ref[...] = r.astype(o_ref.dtype)

            @pl.when(jnp.logical_not(first))
            def _(r=r, o_ref=o_ref):
                o_ref[...] += r.astype(o_ref.dtype)

    res = pl.pallas_call(
        body, name=name, grid=grid,
        in_specs=[pl.BlockSpec(blk, im) for _, blk, im in ins],
        out_specs=[pl.BlockSpec(o[2], o[3]) for o in outs],
        out_shape=[jax.ShapeDtypeStruct(o[0], o[1]) for o in outs],
        compiler_params=_params(2),
    )(*[a for a, _, _ in ins])
    return res


def _vjp_fn(f, n_primal):
    def g(*args):
        _, vjp = jax.vjp(f, *args[:n_primal])
        cts = args[n_primal:]
        return vjp(cts[0] if len(cts) == 1 else tuple(cts))
    return g


def _scan_fwd(name, step, n_state, state_shape, cins, ins, outs, n_units, n_chunks):
    n_c, n_in, n_out = len(cins), len(ins), len(outs)

    def body(*refs):
        c_refs = refs[:n_c]
        in_refs = refs[n_c:n_c + n_in]
        out_refs = refs[n_c + n_in:n_c + n_in + n_out]
        saved = refs[n_c + n_in + n_out:n_c + n_in + n_out + n_state]
        st = refs[n_c + n_in + n_out + n_state:]

        @pl.when(pl.program_id(1) == 0)
        def _():
            for s in st:
                s[...] = jnp.zeros_like(s)

        cur = [s[...] for s in st]
        for sv, s in zip(saved, cur):
            sv[...] = s
        new, res = step(cur, [r[...] for r in c_refs], [r[...] for r in in_refs], _PLAIN_DOTS)
        for s, n in zip(st, new):
            s[...] = n
        for o, r in zip(out_refs, res):
            o[...] = r

    sshape = (n_units, n_chunks) + state_shape
    sblock = (None, None) + state_shape
    nz = len(state_shape)
    res = pl.pallas_call(
        body, name=name, grid=(n_units, n_chunks),
        in_specs=[pl.BlockSpec(e[1], e[2]) for e in cins + ins],
        out_specs=[pl.BlockSpec(o[1], o[2]) for o in outs]
        + [pl.BlockSpec(sblock, lambda u, c: (u, c) + (0,) * nz)] * n_state,
        out_shape=[jax.ShapeDtypeStruct(o[0], f32) for o in outs]
        + [jax.ShapeDtypeStruct(sshape, f32)] * n_state,
        scratch_shapes=[pltpu.VMEM(state_shape, f32)] * n_state,
        compiler_params=_params(2),
    )(*[e[0] for e in cins + ins])
    return res[:n_out], res[n_out:]


def _scan_bwd(name, step, n_state, state_shape, cins, ins, saved, douts, n_units, n_chunks):
    n_c, n_in, n_do = len(cins), len(ins), len(douts)

    def flip(im):
        return lambda u, c: im(u, n_chunks - 1 - c)

    def body(*refs):
        p = 0
        c_refs = refs[p:p + n_c]; p += n_c
        in_refs = refs[p:p + n_in]; p += n_in
        sv_refs = refs[p:p + n_state]; p += n_state
        do_refs = refs[p:p + n_do]; p += n_do
        dc_refs = refs[p:p + n_c]; p += n_c
        di_refs = refs[p:p + n_in]; p += n_in
        dst = refs[p:]
        first = pl.program_id(1) == 0

        @pl.when(first)
        def _():
            for s in dst:
                s[...] = jnp.zeros_like(s)

        def fn(states, consts, vals):
            new, res = step(states, consts, vals, _VJP_DOTS)
            return tuple(new), tuple(res)

        prim = ([r[...] for r in sv_refs], [r[...] for r in c_refs], [r[...] for r in in_refs])
        _, vjp = jax.vjp(fn, *prim)
        d_states, d_consts, d_vals = vjp((tuple(s[...] for s in dst), tuple(r[...] for r in do_refs)))
        for s, g in zip(dst, d_states):
            s[...] = g
        for o, g in zip(di_refs, d_vals):
            o[...] = g
        for o, g in zip(dc_refs, d_consts):
            @pl.when(first)
            def _(o=o, g=g):
                o[...] = g

            @pl.when(jnp.logical_not(first))
            def _(o=o, g=g):
                o[...] += g

    nz = len(state_shape)
    sblock = (None, None) + state_shape
    def gshape(e):
        return e[3] if len(e) == 5 else e[0].shape

    def gmap(e):
        return e[4] if len(e) == 5 else e[2]

    in_specs = ([pl.BlockSpec(e[1], e[2]) for e in cins]
                + [pl.BlockSpec(e[1], flip(e[2])) for e in ins]
                + [pl.BlockSpec(sblock, lambda u, c: (u, n_chunks - 1 - c) + (0,) * nz)] * n_state
                + [pl.BlockSpec(e[1], flip(e[2])) for e in douts])
    out_specs = ([pl.BlockSpec(e[1], e[2]) for e in cins]
                 + [pl.BlockSpec(e[1], flip(gmap(e))) for e in ins])
    out_shape = [jax.ShapeDtypeStruct(gshape(e), f32) for e in cins + ins]
    res = pl.pallas_call(
        body, name=name, grid=(n_units, n_chunks), in_specs=in_specs, out_specs=out_specs, out_shape=out_shape,
        scratch_shapes=[pltpu.VMEM(state_shape, f32)] * n_state,
        compiler_params=_params(2),
    )(*([e[0] for e in cins + ins] + list(saved) + [e[0] for e in douts]))
    return res[:n_c], res[n_c:]


def _rms(x, g):
    return x * lax.rsqrt(jnp.mean(x * x, axis=-1, keepdims=True) + RMS_EPS) * g


def _rows(tm, width):
    return (tm, width), lambda r, z: (r, 0)


def _const(shape):
    return shape, lambda r, z: (0,) * len(shape)


def _rms_fwd(name, h, g):
    L, D = h.shape
    tm = _row_tile(L)
    return _ew(name, _rms, [(h, *_rows(tm, D)), (g, *_const((1, D)))],
               [((L, D), f32, *_rows(tm, D), None)], (L // tm, 1))[0]


def _rms_bwd(name, h, g, d_hn, d_res):
    L, D = h.shape
    tm = _row_tile(L)

    def f(hv, gv, dv, rv):
        dh, dg = _vjp_fn(_rms, 2)(hv, gv, dv)
        return dh + rv, dg

    return _ew(name, f, [(h, *_rows(tm, D)), (g, *_const((1, D))), (d_hn, *_rows(tm, D)), (d_res, *_rows(tm, D))],
               [((L, D), f32, *_rows(tm, D), None), ((1, D), f32, *_const((1, D)), "all")], (L // tm, 1))


def _loss_head(h, g, target):
    L, D = h.shape
    tm = _row_tile(L)

    def f(hv, gv, tv):
        def lf(a, b):
            e = jnp.square(_rms(a, b) - tv)
            return (0.5 / D) * jnp.sum(jnp.sum(e, axis=1, keepdims=True), axis=0, keepdims=True)

        val, vjp = jax.vjp(lf, hv, gv)
        dh, dg = vjp(jnp.ones((1, 1), f32))
        return jnp.broadcast_to(val, (1, 128)), dh, dg

    return _ew("loss_head", f, [(h, *_rows(tm, D)), (g, *_const((1, D))), (target, *_rows(tm, D))],
               [((1, 128), f32, *_const((1, 128)), "all"), ((L, D), f32, *_rows(tm, D), None),
                ((1, D), f32, *_const((1, D)), "all")], (L // tm, 1))


def _sqrelu(x):
    return jnp.square(jnp.maximum(x, 0.0))


def _mm_plain(name, a, b, dims, *, a_fn=None, epi_fn=None, aux=()):
    if dims == _NN:
        (M, K), N = a.shape, b.shape[1]
    elif dims == _NT:
        (M, K), N = a.shape, b.shape[0]
    else:
        (K, M), N = a.shape, b.shape[1]
    tm = _mm_rows(M)
    tn = _col_tile(N, 1536)
    tk = _col_tile(K)
    if dims == _TN:
        tk = min(512, K)
        a_spec = pl.BlockSpec((tk, tm), lambda i, j, k: (k, i))
        b_spec = pl.BlockSpec((tk, tn), lambda i, j, k: (k, j))
        a_aux = pl.BlockSpec((tk, tm), lambda i, j, k: (k, i))
    elif dims == _NT:
        a_spec = pl.BlockSpec((tm, tk), lambda i, j, k: (i, k))
        b_spec = pl.BlockSpec((tn, tk), lambda i, j, k: (j, k))
        a_aux = pl.BlockSpec((tm, tk), lambda i, j, k: (i, k))
    else:
        a_spec = pl.BlockSpec((tm, tk), lambda i, j, k: (i, k))
        b_spec = pl.BlockSpec((tk, tn), lambda i, j, k: (k, j))
        a_aux = pl.BlockSpec((tm, tk), lambda i, j, k: (i, k))
    e_aux = pl.BlockSpec((tm, tn), lambda i, j, k: (i, j))
    aux_full = [(x, a_aux if kd == "a" else e_aux, kd) for x, kd in aux]
    return _mm(name, a, b, dims=dims, grid=(M // tm, N // tn, K // tk), a_spec=a_spec, b_spec=b_spec,
               out_shape=jax.ShapeDtypeStruct((M, N), f32), out_spec=pl.BlockSpec((tm, tn), lambda i, j, k: (i, j)),
               aux=aux_full, a_fn=a_fn, epi_fn=epi_fn, acc_shape=(tm, tn))


def _mlp_fwd(h, g, w1g, w2g, layer):
    L, D = h.shape
    tm = _mm_rows(L)
    fs = D_FF // N_DEV
    hn = _rms_fwd("mlp_norm", h, g)
    h1 = _mm("mlp_up", hn, w1g, dims=_NN, grid=(L // tm, N_DEV, 1),
             a_spec=pl.BlockSpec((tm, D), lambda i, j, k: (i, 0)),
             b_spec=pl.BlockSpec((None, None, D, fs), lambda i, j, k: (j, layer, 0, 0)),
             out_shape=jax.ShapeDtypeStruct((L, D_FF), f32), out_spec=pl.BlockSpec((tm, fs), lambda i, j, k: (i, j)),
             acc_shape=(tm, fs))
    tn = D
    h_out = _mm("mlp_down", h1, w2g, dims=_NN, grid=(L // tm, D // tn, N_DEV),
                a_spec=pl.BlockSpec((tm, fs), lambda i, j, k: (i, k)),
                b_spec=pl.BlockSpec((None, None, fs, tn), lambda i, j, k: (k, layer, 0, j)),
                out_shape=jax.ShapeDtypeStruct((L, D), f32), out_spec=pl.BlockSpec((tm, tn), lambda i, j, k: (i, j)),
                aux=[(h, pl.BlockSpec((tm, tn), lambda i, j, k: (i, j)), "e")],
                a_fn=_sqrelu, epi_fn=lambda acc, res: acc + res, acc_shape=(tm, tn))
    return h_out, hn, h1


def _mlp_bwd(dh, h, g, hn, h1, w1g, w2g, layer, dw1_buf, dw2_buf):
    L, D = h.shape
    tm = _mm_rows(L)
    fs = D_FF // N_DEV
    tk = _mm_rows(L)
    dh1 = _mm("mlp_down_dx", dh, w2g, dims=_NT, grid=(L // tm, N_DEV, 1),
              a_spec=pl.BlockSpec((tm, D), lambda i, j, k: (i, 0)),
              b_spec=pl.BlockSpec((None, None, fs, D), lambda i, j, k: (j, layer, 0, 0)),
              out_shape=jax.ShapeDtypeStruct((L, D_FF), f32), out_spec=pl.BlockSpec((tm, fs), lambda i, j, k: (i, j)),
              aux=[(h1, pl.BlockSpec((tm, fs), lambda i, j, k: (i, j)), "e")],
              epi_fn=lambda acc, pre: acc * (2.0 * jnp.maximum(pre, 0.0)), acc_shape=(tm, fs))
    dw2_buf = _mm("mlp_down_dw", h1, dh, dims=_TN, grid=(N_DEV, 1, L // tk),
                  a_spec=pl.BlockSpec((tk, fs), lambda i, j, k: (k, i)),
                  b_spec=pl.BlockSpec((tk, D), lambda i, j, k: (k, 0)),
                  out_shape=jax.ShapeDtypeStruct(dw2_buf.shape, f32),
                  out_spec=pl.BlockSpec((None, None, fs, D), lambda i, j, k: (i, layer, 0, 0)),
                  a_fn=_sqrelu, acc_shape=(fs, D), out_init=dw2_buf)
    tr = D
    dw1_buf = _mm("mlp_up_dw", hn, dh1, dims=_TN, grid=(D // tr, N_DEV, L // tk),
                  a_spec=pl.BlockSpec((tk, tr), lambda i, j, k: (k, i)),
                  b_spec=pl.BlockSpec((tk, fs), lambda i, j, k: (k, j)),
                  out_shape=jax.ShapeDtypeStruct(dw1_buf.shape, f32),
                  out_spec=pl.BlockSpec((None, None, tr, fs), lambda i, j, k: (j, layer, i, 0)),
                  acc_shape=(tr, fs), out_init=dw1_buf)
    tn = D
    dhn = _mm("mlp_up_dx", dh1, w1g, dims=_NT, grid=(L // tm, D // tn, N_DEV),
              a_spec=pl.BlockSpec((tm, fs), lambda i, j, k: (i, k)),
              b_spec=pl.BlockSpec((None, None, tn, fs), lambda i, j, k: (k, layer, j, 0)),
              out_shape=jax.ShapeDtypeStruct((L, D), f32), out_spec=pl.BlockSpec((tm, tn), lambda i, j, k: (i, j)),
              acc_shape=(tm, tn))
    dh_in, dg = _rms_bwd("mlp_norm_bwd", h, g, dhn, dh)
    return dh_in, dg, dw1_buf, dw2_buf


def _shift_dn(x, s, row):
    return x if s == 0 else jnp.where(row >= s, pltpu.roll(x, s, 0), 0.0)


def _shift_up(x, s, row):
    n = x.shape[0]
    return x if s == 0 else jnp.where(row < n - s, pltpu.roll(x, n - s, 0), 0.0)


def _conv_pre(x, w, b, row):
    c = jnp.broadcast_to(b, x.shape)
    for j in range(4):
        c = c + w[j:j + 1, :] * _shift_dn(x, 3 - j, row)
    return c


def _conv_fwd(name, x_arr, blk_off, w, b):
    L = x_arr.shape[0]
    C = w.shape[1]

    def f(x, wv, bv):
        c = _conv_pre(x, wv, bv, _iota(x.shape, 0))
        return c * jax.nn.sigmoid(c)

    return _ew(name, f, [(x_arr, (L, 128), lambda j, z: (0, blk_off + j)), (w, (4, 128), lambda j, z: (0, j)),
                         (b, (1, 128), lambda j, z: (0, j))],
               [((L, C), f32, (L, 128), lambda j, z: (0, j), None)], (C // 128, 1))[0]


def _conv_bwd(name, x_arr, blk_off, w, b, dy):
    L = x_arr.shape[0]
    C = w.shape[1]

    def f(x, wv, bv, g):
        row = _iota(x.shape, 0)
        c = _conv_pre(x, wv, bv, row)
        s = jax.nn.sigmoid(c)
        dc = g * (s * (1.0 + c * (1.0 - s)))
        dx = jnp.zeros_like(x)
        dw = jnp.zeros((4, 128), f32)
        r4 = _iota((4, 128), 0)
        for j in range(4):
            dx = dx + wv[j:j + 1, :] * _shift_up(dc, 3 - j, row)
            dwj = jnp.sum(dc * _shift_dn(x, 3 - j, row), axis=0, keepdims=True)
            dw = dw + jnp.where(r4 == j, jnp.broadcast_to(dwj, (4, 128)), 0.0)
        return dx, dw, jnp.sum(dc, axis=0, keepdims=True)

    return _ew(name, f, [(x_arr, (L, 128), lambda j, z: (0, blk_off + j)), (w, (4, 128), lambda j, z: (0, j)),
                         (b, (1, 128), lambda j, z: (0, j)), (dy, (L, 128), lambda j, z: (0, j))],
               [((L, C), f32, (L, 128), lambda j, z: (0, j), None), ((4, C), f32, (4, 128), lambda j, z: (0, j), None),
                ((1, C), f32, (1, 128), lambda j, z: (0, j), None)], (C // 128, 1))


def _l2norm(t):
    return t * lax.rsqrt(jnp.sum(t * t, axis=-1, keepdims=True) + 1e-6)


def _gdn_act(cq, ck, ab, alog, dtb):
    h = pl.program_id(1)
    qn = _l2norm(cq) * (128.0 ** -0.5)
    kn = _l2norm(ck)
    lane = _iota(ab.shape, 1)
    a_raw = jnp.sum(jnp.where(lane == h, ab, 0.0), axis=1, keepdims=True)
    b_raw = jnp.sum(jnp.where(lane == h + GDN_HEADS, ab, 0.0), axis=1, keepdims=True)
    lane1 = _iota(alog.shape, 1)
    al = jnp.sum(jnp.where(lane1 == h, alog, 0.0), axis=1, keepdims=True)
    db = jnp.sum(jnp.where(lane1 == h, dtb, 0.0), axis=1, keepdims=True)
    g = -jnp.exp(al) * jax.nn.softplus(a_raw + db)
    beta = jax.nn.sigmoid(b_raw)
    return qn, kn, jnp.broadcast_to(g, cq.shape), jnp.broadcast_to(beta, cq.shape)


def _each(f, *lists):
    return [f(*a) for a in zip(*lists)]


def _gdn_chunk(states, consts, vals, dots):
    S = list(states)
    cut = [slice(128 * i, 128 * i + 128) for i in range(len(S))]
    q, k, v, gb, bb = ([t[:, c] for c in cut] for t in vals)
    C = vals[0].shape[0]
    row, col = _iota((C, C), 0), _iota((C, C), 1)
    causal, strict = row >= col, row > col
    ltri = causal.astype(f32)
    eye = (row == col).astype(f32)
    e0 = (_iota((C, 128), 1) == 0).astype(f32)
    last = _iota((C, 1), 0) == C - 1
    Gb = _each(lambda g: dots.dot01(ltri, g), gb)
    Gc = _each(lambda g: jnp.mean(g, axis=1, keepdims=True), Gb)
    Gr = _each(lambda g: dots.dot01(e0, g, _NT), Gb)
    bc = _each(lambda b: jnp.mean(b, axis=1, keepdims=True), bb)
    decay = _each(lambda gc, gr: jnp.where(causal, jnp.exp(jnp.where(causal, gc - gr, 0.0)), 0.0), Gc, Gr)
    kk = _each(lambda a: _dotb(a, a, _NT), k)
    A = _each(lambda b, x, d: jnp.where(strict, b * x * d, 0.0), bc, kk, decay)
    M = _each(lambda a: eye - a, A)
    P = _each(lambda a: dots.dot3(a, a), A)
    for it in range(5):
        M = _each(lambda m, p: m + dots.dot3(m, p), M, P)
        if it < 4:
            P = _each(lambda p: dots.dot3(p, p), P)
    eG = _each(jnp.exp, Gc)
    u = _each(lambda m, x, b: _dotb(m, x * b), M, v, bc)
    w = _each(lambda m, x, b, e: _dotb(m, x * (b * e)), M, k, bc, eG)
    qk = _each(lambda a, b, d: _dotb(a, b, _NT) * d, q, k, decay)
    g_last = _each(lambda gc: jnp.sum(jnp.where(last, gc, 0.0), axis=0, keepdims=True), Gc)
    v_new = _each(lambda a, b, s: a - _dotb(b, s), u, w, S)
    o = _each(lambda a, e, s, b, x: _dotb(a * e, s) + _dotb(b, x), q, eG, S, qk, v_new)
    S_new = _each(lambda gl, s, a, gc, x: jnp.exp(gl) * s + _dotb(a * jnp.exp(gl - gc), x, _TN), g_last, S, k, Gc, v_new)
    return S_new, [jnp.concatenate(o, axis=1)]


def _gdn_post(o, gate, g):
    return _rms(o, g) * (gate * jax.nn.sigmoid(gate))


def _pad_row(v):
    return jnp.pad(v.astype(f32), (0, 128 - v.shape[0])).reshape(1, 128)


def _gdn_fwd(h, g_norm, w_ext, conv_w, a_log, dt_bias, o_norm_g, w_out):
    L, D = h.shape
    tm = _head_rows(L)
    nc = L // CHUNK
    H = GDN_HEADS
    hn = _rms_fwd("mix_norm", h, g_norm)
    proj = _mm_plain("gdn_in", hn, w_ext, _NN)
    zb = jnp.zeros((1, 3 * D), f32)
    cq = _conv_fwd("gdn_conv", proj, 0, conv_w, zb)
    alog, dtb = _pad_row(a_log), _pad_row(dt_bias)
    act_ins = [(cq, (tm, 128), lambda r, hh: (r, hh)), (cq, (tm, 128), lambda r, hh: (r, H + hh)),
               (proj, (tm, 128), lambda r, hh: (r, 4 * H)), (alog, (1, 128), lambda r, hh: (0, 0)),
               (dtb, (1, 128), lambda r, hh: (0, 0))]
    qn, kn, gb, bb = _ew("gdn_act", _gdn_act, act_ins,
                         [((L, D), f32, (tm, 128), lambda r, hh: (r, hh), None)] * 4, (L // tm, H))
    cblk = (CHUNK, 128 * GDN_HB)
    core_ins = [(qn, cblk, lambda u, c: (c, u)), (kn, cblk, lambda u, c: (c, u)),
                (cq, cblk, lambda u, c: (c, 2 * H // GDN_HB + u), (L, D), lambda u, c: (c, u)),
                (gb, cblk, lambda u, c: (c, u)), (bb, cblk, lambda u, c: (c, u))]
    (o,), saved_s = _scan_fwd("gdn_core", _gdn_chunk, GDN_HB, (128, 128), [], core_ins,
                              [((L, D), cblk, lambda u, c: (c, u))], H // GDN_HB, nc)
    on = o_norm_g.reshape(1, 128)
    post_ins = [(o, (tm, 128), lambda r, hh: (r, hh)), (proj, (tm, 128), lambda r, hh: (r, 3 * H + hh)),
                (on, (1, 128), lambda r, hh: (0, 0))]
    y = _ew("gdn_post", _gdn_post, post_ins, [((L, D), f32, (tm, 128), lambda r, hh: (r, hh), None)], (L // tm, H))[0]
    h_out = _mm_plain("gdn_out", y, w_out, _NN, epi_fn=lambda acc, res: acc + res, aux=[(h, "e")])
    saved = dict(hn=hn, proj=proj, cq=cq, alog=alog, dtb=dtb, act_ins=act_ins, core_ins=core_ins, saved_s=saved_s,
                 post_ins=post_ins, y=y, zb=zb)
    return h_out, saved


def _gdn_bwd(dh, h, g_norm, w_ext, conv_w, w_out, sv):
    L, D = h.shape
    tm = _head_rows(L)
    nc = L // CHUNK
    H = GDN_HEADS
    dy = _mm_plain("gdn_out_dx", dh, w_out, _NT)
    dw_out = _mm_plain("gdn_out_dw", sv["y"], dh, _TN)
    hd = ((L, D), f32, (tm, 128), lambda r, hh: (r, hh), None)
    d_o, d_gate, d_on = _ew("gdn_post_bwd", _vjp_fn(_gdn_post, 3),
                            sv["post_ins"] + [(dy, (tm, 128), lambda r, hh: (r, hh))],
                            [hd, hd, ((1, 128), f32, (1, 128), lambda r, hh: (0, 0), "all")], (L // tm, H))
    cblk = (CHUNK, 128 * GDN_HB)
    _, (dqn, dkn, dv, dgb, dbb) = _scan_bwd("gdn_core_bwd", _gdn_chunk, GDN_HB, (128, 128), [], sv["core_ins"],
                                            sv["saved_s"], [(d_o, cblk, lambda u, c: (c, u))], H // GDN_HB, nc)
    cts = [(t, (tm, 128), lambda r, hh: (r, hh)) for t in (dqn, dkn, dgb, dbb)]
    row128 = ((1, 128), f32, (1, 128), lambda r, hh: (0, 0), "all")
    d_cq, d_ck, d_ab, d_alog, d_dtb = _ew(
        "gdn_act_bwd", _vjp_fn(_gdn_act, 5), sv["act_ins"] + cts,
        [hd, hd, ((L, 128), f32, (tm, 128), lambda r, hh: (r, 0), "inner"), row128, row128], (L // tm, H))
    d_conv_out = jnp.concatenate([d_cq, d_ck, dv], axis=1)
    d_conv_in, d_conv_w, _ = _conv_bwd("gdn_conv_bwd", sv["proj"], 0, conv_w, sv["zb"], d_conv_out)
    d_proj = jnp.concatenate([d_conv_in, d_gate, d_ab], axis=1)
    dw_ext = _mm_plain("gdn_in_dw", sv["hn"], d_proj, _TN)
    dhn = _mm_plain("gdn_in_dx", d_proj, w_ext, _NT)
    dh_in, dg = _rms_bwd("mix_norm_bwd", h, g_norm, dhn, dh)
    grads = dict(norm=dg, w_ext=dw_ext, conv_w=d_conv_w, a_log=d_alog[0, :H], dt_bias=d_dtb[0, :H],
                 o_norm_g=d_on[0], w_out=dw_out)
    return dh_in, grads


def _expand_lanes(row, width, rep):
    sel = ((_iota((128, width), 1) // rep) == _iota((128, width), 0)).astype(f32)
    return jnp.mean(_dot(jnp.broadcast_to(row, (8, 128)), sel), axis=0, keepdims=True)


def _s5_params(lre, lim, ldt, wbr, wbi):
    dt = jnp.exp(_expand_lanes(ldt, 512, S5_STATE))
    mag = jnp.exp(lre * dt)
    ang = lim * dt
    abr, abi = mag * jnp.cos(ang), mag * jnp.sin(ang)
    nr = abr - 1.0
    den = lre * lre + lim * lim
    cr = (nr * lre + abi * lim) / den
    ci = (abi * lre - nr * lim) / den
    return abr, abi, cr * wbr - ci * wbi, cr * wbi + ci * wbr


def _s5_scan(name, xr, xi, ar, ai, rev, want_prev):
    L, W = xr.shape
    nb = L // 8
    n_out = 4 if want_prev else 2

    def body(xr_ref, xi_ref, ar_ref, ai_ref, *outs):
        a_r = ar_ref[...]
        a_i = -ai_ref[...] if rev else ai_ref[...]

        def cm(p, q):
            return p[0] * q[0] - p[1] * q[1], p[0] * q[1] + p[1] * q[0]

        a1 = (a_r, a_i)
        a2 = cm(a1, a1)
        a3 = cm(a2, a1)
        a4 = cm(a2, a2)
        pw = [a1, a2, a3, a4, cm(a4, a1), cm(a4, a2), cm(a4, a3), cm(a4, a4)]
        blk8 = (8, S5_SCAN_LANES)
        row = _iota(blk8, 0)
        tab_r = jnp.zeros(blk8, f32)
        tab_i = jnp.zeros(blk8, f32)
        for t in range(8):
            idx = 7 - t if rev else t
            tab_r = jnp.where(row == idx, jnp.broadcast_to(pw[t][0], blk8), tab_r)
            tab_i = jnp.where(row == idx, jnp.broadcast_to(pw[t][1], blk8), tab_i)
        lv = [(d, jnp.broadcast_to(p[0], blk8), jnp.broadcast_to(p[1], blk8)) for d, p in ((1, a1), (2, a2), (4, a4))]

        def step(i, carry):
            cr, ci = carry
            blk = nb - 1 - i if rev else i
            r0 = pl.multiple_of(blk * 8, 8)
            x_r = xr_ref[pl.ds(r0, 8), :]
            x_i = xi_ref[pl.ds(r0, 8), :]
            for d, p_r, p_i in lv:
                if rev:
                    s_r = jnp.where(row < 8 - d, pltpu.roll(x_r, 8 - d, 0), 0.0)
                    s_i = jnp.where(row < 8 - d, pltpu.roll(x_i, 8 - d, 0), 0.0)
                else:
                    s_r = jnp.where(row >= d, pltpu.roll(x_r, d, 0), 0.0)
                    s_i = jnp.where(row >= d, pltpu.roll(x_i, d, 0), 0.0)
                x_r, x_i = x_r + p_r * s_r - p_i * s_i, x_i + p_r * s_i + p_i * s_r
            x_r, x_i = x_r + tab_r * cr - tab_i * ci, x_i + tab_r * ci + tab_i * cr
            outs[0][pl.ds(r0, 8), :] = x_r
            outs[1][pl.ds(r0, 8), :] = x_i
            if want_prev:
                outs[2][pl.ds(r0, 8), :] = jnp.where(row >= 1, pltpu.roll(x_r, 1, 0), cr)
                outs[3][pl.ds(r0, 8), :] = jnp.where(row >= 1, pltpu.roll(x_i, 1, 0), ci)
            e = 0 if rev else 7
            return jnp.broadcast_to(x_r[e:e + 1, :], blk8), jnp.broadcast_to(x_i[e:e + 1, :], blk8)

        lax.fori_loop(0, nb, step, (jnp.zeros(blk8, f32), jnp.zeros(blk8, f32)))

    per = 512 // S5_SCAN_LANES
    col = pl.BlockSpec((L, S5_SCAN_LANES), lambda q, z: (0, q))
    aspec = pl.BlockSpec((None, 1, S5_SCAN_LANES), lambda q, z: (q // per, 0, q % per))
    return pl.pallas_call(
        body, name=name, grid=(W // S5_SCAN_LANES, 1), in_specs=[col, col, aspec, aspec], out_specs=[col] * n_out,
        out_shape=[jax.ShapeDtypeStruct((L, W), f32)] * n_out, compiler_params=_params(2),
    )(xr, xi, ar, ai)


def _blockdiag(t, n_in, n_out):
    t4 = t.reshape(8, 8, n_in, n_out)
    return jnp.einsum("jaio,ab->jaibo", t4, jnp.eye(8, dtype=t.dtype)).reshape(8, 8 * n_in, 8 * n_out)


def _blockdiag_t(w, n_in, n_out):
    w5 = w.reshape(8, 8, n_in, 8, n_out)
    return jnp.einsum("jaibo,ab->jaio", w5, jnp.eye(8, dtype=w.dtype)).reshape(64, n_in, n_out)


def _glu(ag, h):
    n = ag.shape[1] // 2
    return h + ag[:, :n] * jax.nn.sigmoid(ag[:, n:])


def _s5_fwd(h, g_norm, w_in, lam_re, lam_im, log_dt, b_re, b_im, c_re, c_im, d_skip, w_out_g):
    L, D = h.shape
    tm, te = _mm_rows(L), _row_tile(L)
    W = 8 * 512
    hn = _rms_fwd("mix_norm", h, g_norm)
    u = _mm_plain("s5_in", hn, w_in, _NN)
    lre, lim = lam_re.reshape(8, 1, 512), lam_im.reshape(8, 1, 512)
    ldt = jnp.pad(log_dt.reshape(8, 1, 8), ((0, 0), (0, 0), (0, 120)))
    wbr = _blockdiag(b_re.transpose(0, 2, 1), 16, 64)
    wbi = _blockdiag(b_im.transpose(0, 2, 1), 16, 64)
    wcr = _blockdiag(c_re.transpose(0, 2, 1), 64, 16)
    wci = _blockdiag(c_im.transpose(0, 2, 1), 64, 16)
    jb = lambda shape: (shape, lambda j, z: (j, 0, 0))
    par_ins = [(lre, *jb((None, 1, 512))), (lim, *jb((None, 1, 512))), (ldt, *jb((None, 1, 128))),
               (wbr, *jb((None, 128, 512))), (wbi, *jb((None, 128, 512)))]
    abr, abi, bbr, bbi = _ew("s5_params", _s5_params, par_ins,
                             [((8, 1, 512), f32, *jb((None, 1, 512)), None)] * 2
                             + [((8, 128, 512), f32, *jb((None, 128, 512)), None)] * 2, (8, 1))

    def bu(name, wb):
        return _mm(name, u, wb, dims=_NN, grid=(L // tm, 8, 1),
                   a_spec=pl.BlockSpec((tm, 128), lambda i, j, k: (i, j)),
                   b_spec=pl.BlockSpec((None, 128, 512), lambda i, j, k: (j, 0, 0)),
                   out_shape=jax.ShapeDtypeStruct((L, W), f32), out_spec=pl.BlockSpec((tm, 512), lambda i, j, k: (i, j)),
                   acc_shape=(tm, 512))

    bur, bui = bu("s5_bu", bbr), bu("s5_bu", bbi)
    sr, si, pr, pi = _s5_scan("s5_scan", bur, bui, abr, abi, False, True)
    d_row = d_skip.reshape(1, D)
    cspec = dict(a_spec=pl.BlockSpec((tm, 512), lambda i, j, k: (i, j)),
                 b_spec=pl.BlockSpec((None, 512, 128), lambda i, j, k: (j, 0, 0)),
                 out_shape=jax.ShapeDtypeStruct((L, D), f32), out_spec=pl.BlockSpec((tm, 128), lambda i, j, k: (i, j)),
                 acc_shape=(tm, 128))
    e128 = pl.BlockSpec((tm, 128), lambda i, j, k: (i, j))
    pre1 = _mm("s5_c_re", sr, wcr, dims=_NN, grid=(L // tm, 8, 1), **cspec)
    pre = _mm("s5_c_im", si, wci, dims=_NN, grid=(L // tm, 8, 1),
              aux=[(pre1, e128, "e"), (u, e128, "e"), (d_row, pl.BlockSpec((1, 128), lambda i, j, k: (0, j)), "e")],
              epi_fn=lambda acc, p1, uu, dd: p1 - acc + dd * uu, **cspec)
    ws = D // N_DEV * 2
    ag = _mm("s5_out", pre, w_out_g, dims=_NN, grid=(L // tm, N_DEV, 1),
             a_spec=pl.BlockSpec((tm, D), lambda i, j, k: (i, 0)),
             b_spec=pl.BlockSpec((None, None, D, ws), lambda i, j, k: (j, 0, 0, 0)),
             out_shape=jax.ShapeDtypeStruct((L, 2 * D), f32), out_spec=pl.BlockSpec((tm, ws), lambda i, j, k: (i, j)),
             a_fn=jax.nn.gelu, acc_shape=(tm, ws))
    h_out = _ew("s5_glu", _glu, [(ag, *_rows(te, 2 * D)), (h, *_rows(te, D))],
                [((L, D), f32, *_rows(te, D), None)], (L // te, 1))[0]
    saved = dict(hn=hn, u=u, par_ins=par_ins, abr=abr, abi=abi, bbr=bbr, bbi=bbi, wcr=wcr, wci=wci, sr=sr, si=si,
                 pr=pr, pi=pi, pre=pre, ag=ag, d_row=d_row)
    return h_out, saved


def _s5_bwd(dh, h, g_norm, w_in, w_out_g, sv):
    L, D = h.shape
    tm, te = _mm_rows(L), _row_tile(L)
    tk = min(512, L)
    W = 8 * 512
    ws = D // N_DEV * 2
    u, pre, d_row = sv["u"], sv["pre"], sv["d_row"]
    d_ag = _ew("s5_glu_bwd", lambda ag, hv, g: _vjp_fn(_glu, 2)(ag, hv, g)[0],
               [(sv["ag"], *_rows(te, 2 * D)), (h, *_rows(te, D)), (dh, *_rows(te, D))],
               [((L, 2 * D), f32, *_rows(te, 2 * D), None)], (L // te, 1))[0]
    tr = D
    dw_out = _mm("s5_out_dw", pre, d_ag, dims=_TN, grid=(D // tr, N_DEV, L // tk),
                 a_spec=pl.BlockSpec((tk, tr), lambda i, j, k: (k, i)),
                 b_spec=pl.BlockSpec((tk, ws), lambda i, j, k: (k, j)),
                 out_shape=jax.ShapeDtypeStruct((N_DEV, 1, D, ws), f32),
                 out_spec=pl.BlockSpec((None, None, tr, ws), lambda i, j, k: (j, 0, i, 0)),
                 a_fn=jax.nn.gelu, acc_shape=(tr, ws))
    tn = 512
    dpre = _mm("s5_out_dx", d_ag, w_out_g, dims=_NT, grid=(L // tm, D // tn, N_DEV),
               a_spec=pl.BlockSpec((tm, ws), lambda i, j, k: (i, k)),
               b_spec=pl.BlockSpec((None, None, tn, ws), lambda i, j, k: (k, 0, j, 0)),
               out_shape=jax.ShapeDtypeStruct((L, D), f32), out_spec=pl.BlockSpec((tm, tn), lambda i, j, k: (i, j)),
               aux=[(pre, pl.BlockSpec((tm, tn), lambda i, j, k: (i, j)), "e")],
               epi_fn=lambda acc, p: _vjp_fn(jax.nn.gelu, 1)(p, acc)[0], acc_shape=(tm, tn))
    d_d = _ew("s5_dskip", lambda a, b: jnp.sum(a * b, axis=0, keepdims=True),
              [(dpre, *_rows(te, D)), (u, *_rows(te, D))], [((1, D), f32, *_const((1, D)), "all")], (L // te, 1))[0]
    neg = lambda acc: -acc
    dsspec = dict(dims=_NT, grid=(L // tm, 8, 1), a_spec=pl.BlockSpec((tm, 128), lambda i, j, k: (i, j)),
                  b_spec=pl.BlockSpec((None, 512, 128), lambda i, j, k: (j, 0, 0)),
                  out_shape=jax.ShapeDtypeStruct((L, W), f32), out_spec=pl.BlockSpec((tm, 512), lambda i, j, k: (i, j)),
                  acc_shape=(tm, 512))
    dsr = _mm("s5_c_re_dx", dpre, sv["wcr"], **dsspec)
    dsi = _mm("s5_c_im_dx", dpre, sv["wci"], epi_fn=neg, **dsspec)
    dwspec = dict(dims=_TN, grid=(8, 1, L // tk), a_spec=pl.BlockSpec((tk, 512), lambda i, j, k: (k, i)),
                  b_spec=pl.BlockSpec((tk, 128), lambda i, j, k: (k, i)),
                  out_shape=jax.ShapeDtypeStruct((8, 512, 128), f32),
                  out_spec=pl.BlockSpec((None, 512, 128), lambda i, j, k: (i, 0, 0)), acc_shape=(512, 128))
    dwcr = _mm("s5_c_re_dw", sv["sr"], dpre, **dwspec)
    dwci = _mm("s5_c_im_dw", sv["si"], dpre, epi_fn=neg, **dwspec)
    lr, li = _s5_scan("s5_scan_bwd", dsr, dsi, sv["abr"], sv["abi"], True, False)

    def da(lrv, liv, prv, piv):
        return (jnp.sum(lrv * prv + liv * piv, axis=0, keepdims=True),
                jnp.sum(liv * prv - lrv * piv, axis=0, keepdims=True))

    sblk = ((te, 512), lambda j, r: (r, j))
    dabr, dabi = _ew("s5_dlam", da, [(lr, *sblk), (li, *sblk), (sv["pr"], *sblk), (sv["pi"], *sblk)],
                     [((8, 1, 512), f32, (None, 1, 512), lambda j, r: (j, 0, 0), "inner")] * 2, (8, L // te))
    dbspec = dict(dims=_TN, grid=(8, 1, L // tk), a_spec=pl.BlockSpec((tk, 128), lambda i, j, k: (k, i)),
                  b_spec=pl.BlockSpec((tk, 512), lambda i, j, k: (k, i)),
                  out_shape=jax.ShapeDtypeStruct((8, 128, 512), f32),
                  out_spec=pl.BlockSpec((None, 128, 512), lambda i, j, k: (i, 0, 0)), acc_shape=(128, 512))
    dbbr = _mm("s5_bu_dw", u, lr, **dbspec)
    dbbi = _mm("s5_bu_dw", u, li, **dbspec)
    duspec = dict(dims=_NT, grid=(L // tm, 8, 1), a_spec=pl.BlockSpec((tm, 512), lambda i, j, k: (i, j)),
                  b_spec=pl.BlockSpec((None, 128, 512), lambda i, j, k: (j, 0, 0)),
                  out_shape=jax.ShapeDtypeStruct((L, D), f32), out_spec=pl.BlockSpec((tm, 128), lambda i, j, k: (i, j)),
                  acc_shape=(tm, 128))
    e128 = pl.BlockSpec((tm, 128), lambda i, j, k: (i, j))
    du1 = _mm("s5_bu_dx_re", lr, sv["bbr"], **duspec)
    du = _mm("s5_bu_dx_im", li, sv["bbi"],
             aux=[(du1, e128, "e"), (dpre, e128, "e"), (d_row, pl.BlockSpec((1, 128), lambda i, j, k: (0, j)), "e")],
             epi_fn=lambda acc, d1, dp, dd: acc + d1 + dp * dd, **duspec)
    jb = lambda shape: (shape, lambda j, z: (j, 0, 0))
    cts = [(dabr, *jb((None, 1, 512))), (dabi, *jb((None, 1, 512))), (dbbr, *jb((None, 128, 512))),
           (dbbi, *jb((None, 128, 512)))]
    dlre, dlim, dldt, dwbr, dwbi = _ew(
        "s5_params_bwd", _vjp_fn(_s5_params, 5), sv["par_ins"] + cts,
        [((8, 1, 512), f32, *jb((None, 1, 512)), None)] * 2 + [((8, 1, 128), f32, *jb((None, 1, 128)), None)]
        + [((8, 128, 512), f32, *jb((None, 128, 512)), None)] * 2, (8, 1))
    dw_in = _mm_plain("s5_in_dw", sv["hn"], du, _TN)
    dhn = _mm_plain("s5_in_dx", du, w_in, _NT)
    dh_in, dg = _rms_bwd("mix_norm_bwd", h, g_norm, dhn, dh)
    grads = dict(norm=dg, w_in=dw_in, lam_re=dlre.reshape(64, 64), lam_im=dlim.reshape(64, 64),
                 log_dt=dldt[:, 0, :8].reshape(64),
                 b_re=_blockdiag_t(dwbr, 16, 64).transpose(0, 2, 1), b_im=_blockdiag_t(dwbi, 16, 64).transpose(0, 2, 1),
                 c_re=_blockdiag_t(dwcr, 64, 16).transpose(0, 2, 1), c_im=_blockdiag_t(dwci, 64, 16).transpose(0, 2, 1),
                 d=d_d[0], w_out=dw_out)
    return dh_in, grads


def _m2_act(dt_raw, dtbias, alog):
    dt = jax.nn.softplus(dt_raw + dtbias)
    da = dt * (-jnp.exp(alog))
    sel = ((_iota((128, M2_INNER), 1) // 64) == _iota((128, M2_INNER), 0)).astype(f32)
    return _dot(dt, sel), _dot(da, sel)


def _m2_dexp(d):
    return _expand_lanes(d, M2_INNER, 64)


def _ssd_chunk(states, consts, vals, dots):
    (dsk,) = consts
    S = list(states)
    n = len(S)
    cut = [slice(128 * i, 128 * i + 128) for i in range(n)]
    x, dtb, dab = ([t[:, c] for c in cut] for t in vals[:3])
    dsk = [dsk[:, c] for c in cut]
    B = [vals[3][:, cut[i // 2]] for i in range(n)]
    Cm = [vals[4][:, cut[i // 2]] for i in range(n)]
    C = vals[0].shape[0]
    row, col = _iota((C, C), 0), _iota((C, C), 1)
    causal = row >= col
    ltri = causal.astype(f32)
    lane = _iota((C, 128), 1)
    last = _iota((C, 128), 0) == C - 1
    eye128 = _iota((128, 128), 0) == _iota((128, 128), 1)
    head = [jnp.logical_and(lane >= 64 * hh, lane < 64 * hh + 64) for hh in range(2)]
    pick = [(lane == 64 * hh).astype(f32) for hh in range(2)]
    xdt = _each(lambda a, b: a * b, x, dtb)
    cb = _each(lambda c, b: _dotb(c, b, _NT), Cm[::2], B[::2])
    cum = _each(lambda a: dots.dot01(ltri, a), dab)
    clast = _each(lambda a: jnp.sum(jnp.where(last, a, 0.0), axis=0, keepdims=True), cum)
    st = _each(lambda a, cl, cu, b: _dotb(a * jnp.exp(cl - cu), b, _TN), xdt, clast, cum, B)
    y = _each(lambda c, s, cu: _dotb(c, s, _NT) * jnp.exp(cu), Cm, S, cum)
    for hh in range(2):
        ccol = _each(lambda cu: jnp.sum(jnp.where(head[hh], cu, 0.0), axis=1, keepdims=True) * (1.0 / 64), cum)
        crow = _each(lambda cu: dots.dot01(pick[hh], cu, _NT), cum)
        lm = _each(lambda a, b: jnp.where(causal, jnp.exp(jnp.where(causal, a - b, 0.0)), 0.0), ccol, crow)
        y = [y[i] + _dotb(cb[i // 2] * lm[i], jnp.where(head[hh], xdt[i], 0.0)) for i in range(n)]
    cdcol = _each(lambda cl: jnp.sum(jnp.where(eye128, jnp.broadcast_to(jnp.exp(cl), (128, 128)), 0.0),
                                     axis=1, keepdims=True), clast)
    S_new = _each(lambda c, s, t: c * s + t, cdcol, S, st)
    out = _each(lambda a, d, b: a + d * b, y, dsk, x)
    return S_new, [jnp.concatenate(out, axis=1)]


def _m2_post(yc, z, ng):
    return _rms(yc * (z * jax.nn.sigmoid(z)), ng)


def _m2_fwd(h, g_norm, w_ext, conv_w, conv_b, dt_bias, a_log, d_skip, norm_g, w_out):
    L, D = h.shape
    tm = _row_tile(L)
    nc = L // CHUNK
    NI = M2_INNER
    hn = _rms_fwd("mix_norm", h, g_norm)
    proj = _mm_plain("m2_in", hn, w_ext, _NN)
    xbc = _conv_fwd("m2_conv", proj, NI // 128, conv_w, conv_b)
    dtb_row, alog_row, d_pad = _pad_row(dt_bias), _pad_row(a_log), _pad_row(d_skip)
    act_ins = [(proj, (tm, 128), lambda r, z: (r, 3 * NI // 128)), (dtb_row, *_const((1, 128))),
               (alog_row, *_const((1, 128)))]
    dtb, dab = _ew("m2_act", _m2_act, act_ins, [((L, NI), f32, *_rows(tm, NI), None)] * 2, (L // tm, 1))
    dsk = _ew("m2_dexp", _m2_dexp, [(d_pad, *_const((1, 128)))], [((1, NI), f32, *_const((1, NI)), None)], (1, 1))[0]
    GB = M2_GB
    x_blk, bc_blk = (CHUNK, 256 * GB), (CHUNK, 128 * GB)
    cins = [(dsk, (1, 256 * GB), lambda u, c: (0, u))]
    core_ins = [(xbc, x_blk, lambda u, c: (c, u), (L, NI), lambda u, c: (c, u)),
                (dtb, x_blk, lambda u, c: (c, u)), (dab, x_blk, lambda u, c: (c, u)),
                (xbc, bc_blk, lambda u, c: (c, 16 // GB + u), (L, D), lambda u, c: (c, u)),
                (xbc, bc_blk, lambda u, c: (c, 24 // GB + u), (L, D), lambda u, c: (c, u))]
    (yc,), saved_s = _scan_fwd("m2_core", _ssd_chunk, 2 * GB, (128, 128), cins, core_ins,
                               [((L, NI), x_blk, lambda u, c: (c, u))], 8 // GB, nc)
    tp = _head_rows(L)
    gblk = ((tp, 256), lambda g, r: (r, g))
    post_ins = [(yc, *gblk), (proj, *gblk), (norm_g, (1, 256), lambda g, r: (0, g))]
    yn = _ew("m2_post", _m2_post, post_ins, [((L, NI), f32, *gblk, None)], (8, L // tp))[0]
    h_out = _mm_plain("m2_out", yn, w_out, _NN, epi_fn=lambda acc, res: acc + res, aux=[(h, "e")])
    saved = dict(hn=hn, proj=proj, act_ins=act_ins, d_pad=d_pad, cins=cins, core_ins=core_ins, saved_s=saved_s,
                 post_ins=post_ins, yn=yn)
    return h_out, saved


def _m2_bwd(dh, h, g_norm, w_ext, conv_w, conv_b, w_out, sv):
    L, D = h.shape
    tm = _row_tile(L)
    nc = L // CHUNK
    NI = M2_INNER
    dyn = _mm_plain("m2_out_dx", dh, w_out, _NT)
    dw_out = _mm_plain("m2_out_dw", sv["yn"], dh, _TN)
    tp = _head_rows(L)
    gblk = ((tp, 256), lambda g, r: (r, g))
    d_yc, d_z, d_ng = _ew("m2_post_bwd", _vjp_fn(_m2_post, 3), sv["post_ins"] + [(dyn, *gblk)],
                          [((L, NI), f32, *gblk, None)] * 2 + [((1, NI), f32, (1, 256), lambda g, r: (0, g), "inner")],
                          (8, L // tp))
    (d_dsk,), (dx, d_dtb, d_dab, dB, dC) = _scan_bwd(
        "m2_core_bwd", _ssd_chunk, 2 * M2_GB, (128, 128), sv["cins"], sv["core_ins"], sv["saved_s"],
        [(d_yc, (CHUNK, 256 * M2_GB), lambda u, c: (c, u))], 8 // M2_GB, nc)
    row128 = ((1, 128), f32, *_const((1, 128)), "all")
    d_dt_raw, d_dtbias, d_alog = _ew(
        "m2_act_bwd", _vjp_fn(_m2_act, 3), sv["act_ins"] + [(d_dtb, *_rows(tm, NI)), (d_dab, *_rows(tm, NI))],
        [((L, 128), f32, *_rows(tm, 128), None), row128, row128], (L // tm, 1))
    d_d = _ew("m2_dexp_bwd", _vjp_fn(_m2_dexp, 1), [(sv["d_pad"], *_const((1, 128))), (d_dsk, *_const((1, NI)))],
              [((1, 128), f32, *_const((1, 128)), None)], (1, 1))[0]
    d_conv_out = jnp.concatenate([dx, dB, dC], axis=1)
    d_conv_in, d_conv_w, d_conv_b = _conv_bwd("m2_conv_bwd", sv["proj"], NI // 128, conv_w, conv_b, d_conv_out)
    d_proj = jnp.concatenate([d_z, d_conv_in, d_dt_raw], axis=1)
    dw_ext = _mm_plain("m2_in_dw", sv["hn"], d_proj, _TN)
    dhn = _mm_plain("m2_in_dx", d_proj, w_ext, _NT)
    dh_in, dg = _rms_bwd("mix_norm_bwd", h, g_norm, dhn, dh)
    grads = dict(norm=dg, w_ext=dw_ext, conv_w=d_conv_w, conv_b=d_conv_b, dt_bias=d_dtbias[0, :M2_HEADS],
                 a_log=d_alog[0, :M2_HEADS], d=d_d[0, :M2_HEADS], norm_g=d_ng, w_out=dw_out)
    return dh_in, grads


def _mesh_pos():
    return lax.axis_index("x"), lax.axis_index("y"), lax.axis_index("c")


def _flip(pos, p):
    x, y, c = pos
    return (1 - x if p & 4 else x, 1 - y if p & 2 else y, 1 - c if p & 1 else c)


def _index(pos):
    return 4 * pos[0] + 2 * pos[1] + pos[2]


def _comm_call(name, body, arrays, out_shape, n_sem):
    n = len(arrays)
    hbm = pl.BlockSpec(memory_space=pl.ANY)
    return pl.pallas_call(
        body, name=name, in_specs=[hbm] * n, out_specs=[hbm] * len(out_shape), out_shape=out_shape,
        scratch_shapes=[pltpu.SemaphoreType.DMA((n, n_sem)), pltpu.SemaphoreType.DMA((n, n_sem)),
                        pltpu.SemaphoreType.DMA((n, 4))],
    )(*arrays)


def _gather(name, arrays):
    n = len(arrays)

    def body(*refs):
        ins, outs = refs[:n], refs[n:2 * n]
        send_sems, recv_sems, loc_sems = refs[2 * n:]
        me = _mesh_pos()
        c = me[2]
        sib = _flip(me, 1)
        chips = [_flip(me, 4), _flip(me, 2), _flip(me, 6)]

        def copy(w, k, block, to, src=None):
            slab = outs[w].at[_index(block)]
            return pltpu.make_async_remote_copy(
                src_ref=slab if src is None else src, dst_ref=slab, send_sem=send_sems.at[w, k],
                recv_sem=recv_sems.at[w, k], device_id=to, device_id_type=MESH)

        local = [pltpu.make_async_copy(ins[w], outs[w].at[_index(me)], loc_sems.at[w, 0]) for w in range(n)]
        for cp in local:
            cp.start()
        first = [copy(w, 0, me, sib, src=ins[w]) for w in range(n)]
        first += [copy(w, 1 + j, me, chip, src=ins[w]) for j, chip in enumerate(chips) for w in range(n)]
        for cp in first:
            cp.start()
        passed = []
        for j, chip in enumerate(chips):
            for w in range(n):
                copy(w, 1 + j, chip, me).wait_recv()
                fwd = copy(w, 4 + j, chip, sib)
                fwd.start()
                passed.append(fwd)
        for w in range(n):
            copy(w, 0, sib, me).wait_recv()
        for j, chip in enumerate(chips):
            for w in range(n):
                copy(w, 4 + j, (chip[0], chip[1], 1 - c), me).wait_recv()
        for cp in first + passed:
            cp.wait_send()
        for cp in local:
            cp.wait()

    out_shape = [jax.ShapeDtypeStruct((N_DEV,) + a.shape, a.dtype) for a in arrays]
    return _comm_call(name, body, arrays, out_shape, N_DEV - 1)


def _scatter_pair(name, arrays):
    n = len(arrays)

    def body(*refs):
        ins, outs = refs[:n], refs[n:2 * n]
        send_sems, recv_sems, _ = refs[2 * n:]
        me = _mesh_pos()
        c = me[2]
        sib = _flip(me, 1)

        def copy(w, q):
            return pltpu.make_async_remote_copy(
                src_ref=ins[w].at[2 * q + 1 - c], dst_ref=outs[w].at[q], send_sem=send_sems.at[w, q],
                recv_sem=recv_sems.at[w, q], device_id=sib, device_id_type=MESH)

        cps = [copy(w, q) for q in range(4) for w in range(n)]
        for cp in cps:
            cp.start()
        for cp in cps:
            cp.wait()

    out_shape = [jax.ShapeDtypeStruct((4,) + a.shape[1:], a.dtype) for a in arrays]
    return _comm_call(name, body, arrays, out_shape, 4)


def _pair_add(name, full, theirs, core, dtype):
    _, R, C = theirs.shape
    tr = R if R <= 256 else (256 if C <= 512 else 128)

    def body(core_ref, mine_ref, theirs_ref, o_ref):
        o_ref[...] = (mine_ref[...] + theirs_ref[...]).astype(o_ref.dtype)

    blk = pl.BlockSpec((4, tr, C), lambda r, cr: (0, r, 0))
    grid_spec = pltpu.PrefetchScalarGridSpec(
        num_scalar_prefetch=1, grid=(R // tr,),
        in_specs=[pl.BlockSpec((4, None, tr, C), lambda r, cr: (0, cr[0], r, 0)), blk], out_specs=blk)
    return pl.pallas_call(
        body, name=name, grid_spec=grid_spec, out_shape=jax.ShapeDtypeStruct((4, R, C), dtype),
        compiler_params=_params(1),
    )(core.reshape(1).astype(jnp.int32), full.reshape(4, 2, R, C), theirs)


def _scatter_chips(name, arrays):
    n = len(arrays)

    def body(*refs):
        ins, outs = refs[:n], refs[n:2 * n]
        send_sems, recv_sems, loc_sems = refs[2 * n:]
        me = _mesh_pos()
        mq = 2 * me[0] + me[1]
        peers = [_flip(me, 4), _flip(me, 2), _flip(me, 6)]

        def copy(w, k):
            peer = peers[k]
            return pltpu.make_async_remote_copy(
                src_ref=ins[w].at[2 * peer[0] + peer[1]], dst_ref=outs[w].at[mq], send_sem=send_sems.at[w, k],
                recv_sem=recv_sems.at[w, k], device_id=peer, device_id_type=MESH)

        def arrival(w, k):
            peer = peers[k]
            return pltpu.make_async_remote_copy(
                src_ref=ins[w].at[mq], dst_ref=outs[w].at[2 * peer[0] + peer[1]], send_sem=send_sems.at[w, k],
                recv_sem=recv_sems.at[w, k], device_id=peer, device_id_type=MESH)

        local = [pltpu.make_async_copy(ins[w].at[mq], outs[w].at[mq], loc_sems.at[w, 0]) for w in range(n)]
        for cp in local:
            cp.start()
        sends = [copy(w, k) for k in range(3) for w in range(n)]
        for cp in sends:
            cp.start()
        for k in range(3):
            for w in range(n):
                arrival(w, k).wait_recv()
        for cp in sends:
            cp.wait_send()
        for cp in local:
            cp.wait()

    out_shape = [jax.ShapeDtypeStruct(a.shape, a.dtype) for a in arrays]
    return _comm_call(name, body, arrays, out_shape, 3)


def _adamw(name, parts, w, m, v):
    R, C = w.shape
    n_parts = parts.shape[0]
    tr = R if R <= 256 else (256 if C <= 512 else 128)
    bc1 = 1.0 - ADAM_B1 ** ADAM_STEP
    bc2 = 1.0 - ADAM_B2 ** ADAM_STEP

    def f(p, wv, mv, vv):
        g = p[0].astype(f32)
        for i in range(1, n_parts):
            g = g + p[i].astype(f32)
        m2 = ADAM_B1 * mv + (1.0 - ADAM_B1) * g
        v2 = ADAM_B2 * vv + (1.0 - ADAM_B2) * jnp.square(g)
        delta = -ADAM_LR * ((m2 / bc1) / (jnp.sqrt(v2 / bc2) + ADAM_EPS) + ADAM_WD * wv)
        return g, delta, m2, v2

    blk = ((tr, C), lambda r, z: (r, 0))
    return _ew(name, f, [(parts, (n_parts, tr, C), lambda r, z: (0, r, 0)), (w, *blk), (m, *blk), (v, *blk)],
               [((R, C), f32, *blk, None)] * 4, (R // tr, 1))


_WEIGHTS = ["norm_mix_g", "norm_mlp_g", "mlp_w1", "mlp_w2", "gdn_w_in", "gdn_conv_w", "gdn_a_log", "gdn_dt_bias",
            "gdn_o_norm_g", "gdn_w_out", "s5_w_in", "s5_lam_re", "s5_lam_im", "s5_log_dt", "s5_b_re", "s5_b_im",
            "s5_c_re", "s5_c_im", "s5_d", "s5_w_out", "m2_w_in", "m2_conv_w", "m2_conv_b", "m2_dt_bias", "m2_a_log",
            "m2_d", "m2_norm_g", "m2_w_out", "final_norm_g"]
_SHARDED = ["mlp_w1", "mlp_w2", "gdn_w_in", "gdn_w_out", "s5_w_in", "s5_w_out", "m2_w_in", "m2_w_out",
            "gdn_conv_w", "m2_conv_w", "m2_conv_b", "m2_norm_g"]
_MATRICES = _SHARDED[:8]
_REPLICATED = [n for n in _WEIGHTS if n not in _SHARDED]
_GDN_IN, _M2_IN = 4112, 6176
_LAYER_KIND = (0, 1, 2, 0)


def _as2d(a):
    return a.reshape(-1, a.shape[-1])


def _cols_from_shards(g, width):
    return g.transpose(1, 0, 2).reshape(g.shape[1], width)


def _cols_to_shards(a, width):
    return a[:, :width].reshape(a.shape[0], N_DEV, width // N_DEV).transpose(1, 0, 2)


def kernel(x, norm_mix_g, norm_mlp_g, mlp_w1, mlp_w2, gdn_w_in, gdn_conv_w, gdn_a_log, gdn_dt_bias, gdn_o_norm_g, gdn_w_out, s5_w_in, s5_lam_re, s5_lam_im, s5_log_dt, s5_b_re, s5_b_im, s5_c_re, s5_c_im, s5_d, s5_w_out, m2_w_in, m2_conv_w, m2_conv_b, m2_dt_bias, m2_a_log, m2_d, m2_norm_g, m2_w_out, final_norm_g, loss_target, m_norm_mix_g, m_norm_mlp_g, m_mlp_w1, m_mlp_w2, m_gdn_w_in, m_gdn_conv_w, m_gdn_a_log, m_gdn_dt_bias, m_gdn_o_norm_g, m_gdn_w_out, m_s5_w_in, m_s5_lam_re, m_s5_lam_im, m_s5_log_dt, m_s5_b_re, m_s5_b_im, m_s5_c_re, m_s5_c_im, m_s5_d, m_s5_w_out, m_m2_w_in, m_m2_conv_w, m_m2_conv_b, m_m2_dt_bias, m_m2_a_log, m_m2_d, m_m2_norm_g, m_m2_w_out, m_final_norm_g, v_norm_mix_g, v_norm_mlp_g, v_mlp_w1, v_mlp_w2, v_gdn_w_in, v_gdn_conv_w, v_gdn_a_log, v_gdn_dt_bias, v_gdn_o_norm_g, v_gdn_w_out, v_s5_w_in, v_s5_lam_re, v_s5_lam_im, v_s5_log_dt, v_s5_b_re, v_s5_b_im, v_s5_c_re, v_s5_c_im, v_s5_d, v_s5_w_out, v_m2_w_in, v_m2_conv_w, v_m2_conv_b, v_m2_dt_bias, v_m2_a_log, v_m2_d, v_m2_norm_g, v_m2_w_out, v_final_norm_g):
    args = locals()
    W = {n: args[n] for n in _WEIGHTS}
    MOM = {n: args["m_" + n] for n in _WEIGHTS}
    VAR = {n: args["v_" + n] for n in _WEIGHTS}
    h = x[0]
    target = loss_target[0]
    L, D = h.shape

    sends = [W[n].astype(bf16) if n in _MATRICES else _as2d(W[n]) for n in _SHARDED]
    G = dict(zip(_SHARDED, _gather("gather_weights", sends)))
    w1g, w2g = G["mlp_w1"], G["mlp_w2"]
    gdn_in = [jnp.pad(_cols_from_shards(G["gdn_w_in"][:, j], _GDN_IN), ((0, 0), (0, GDN_EXT - _GDN_IN))) for j in range(2)]
    gdn_out = [G["gdn_w_out"][:, j].reshape(D, D) for j in range(2)]
    gdn_conv = [_cols_from_shards(G["gdn_conv_w"][:, 4 * j:4 * j + 4], 3 * D) for j in range(2)]
    s5_in = G["s5_w_in"].reshape(D, D)
    s5_out_g = G["s5_w_out"]
    m2_in = jnp.pad(_cols_from_shards(G["m2_w_in"][:, 0], _M2_IN), ((0, 0), (0, M2_EXT - _M2_IN)))
    m2_out = G["m2_w_out"].reshape(M2_INNER, D)
    m2_conv = _cols_from_shards(G["m2_conv_w"], 2 * M2_INNER)
    m2_cb = _cols_from_shards(G["m2_conv_b"], 2 * M2_INNER)
    m2_ng = _cols_from_shards(G["m2_norm_g"], M2_INNER)

    def mixer_fwd(i, hv):
        kind, j = _LAYER_KIND[i], i // 3
        gn = norm_mix_g[i].reshape(1, D)
        if kind == 0:
            return _gdn_fwd(hv, gn, gdn_in[j], gdn_conv[j], gdn_a_log[j], gdn_dt_bias[j], gdn_o_norm_g[j], gdn_out[j])
        if kind == 1:
            return _s5_fwd(hv, gn, s5_in, s5_lam_re[0], s5_lam_im[0], s5_log_dt[0], s5_b_re[0], s5_b_im[0],
                           s5_c_re[0], s5_c_im[0], s5_d[0], s5_out_g)
        return _m2_fwd(hv, gn, m2_in, m2_conv, m2_cb, m2_dt_bias[0], m2_a_log[0], m2_d[0], m2_ng, m2_out)

    def mixer_bwd(i, dh, hv, sv):
        kind, j = _LAYER_KIND[i], i // 3
        gn = norm_mix_g[i].reshape(1, D)
        if kind == 0:
            return _gdn_bwd(dh, hv, gn, gdn_in[j], gdn_conv[j], gdn_out[j], sv)
        if kind == 1:
            return _s5_bwd(dh, hv, gn, s5_in, s5_out_g, sv)
        return _m2_bwd(dh, hv, gn, m2_in, m2_conv, m2_cb, m2_out, sv)

    tape = []
    for i in range(4):
        h_mid, sv = mixer_fwd(i, h)
        h_next, hn, h1 = _mlp_fwd(h_mid, norm_mlp_g[i].reshape(1, D), w1g, w2g, i)
        tape.append((h, sv, h_mid, hn, h1))
        h = h_next
    loss_row, dh, d_final = _loss_head(h, final_norm_g.reshape(1, D), target)
    loss = lax.psum(loss_row[0, 0], ("x", "y", "c"))

    dw1 = lax.empty((N_DEV, 4, D, D_FF // N_DEV), f32)
    dw2 = lax.empty((N_DEV, 4, D_FF // N_DEV, D), f32)
    d_mix, d_mlp, mg = [None] * 4, [None] * 4, [None] * 4
    for i in reversed(range(4)):
        h_in, sv, h_mid, hn, h1 = tape[i]
        dh, d_mlp[i], dw1, dw2 = _mlp_bwd(dh, h_mid, norm_mlp_g[i].reshape(1, D), hn, h1, w1g, w2g, i, dw1, dw2)
        dh, mg[i] = mixer_bwd(i, dh, h_in, sv)
        d_mix[i] = mg[i]["norm"]
    grad_x = dh.reshape(1, L, D)
    ga, gb_, s5g, m2g = mg[0], mg[3], mg[1], mg[2]

    full = {
        "mlp_w1": dw1, "mlp_w2": dw2,
        "gdn_w_in": jnp.stack([_cols_to_shards(g["w_ext"], _GDN_IN) for g in (ga, gb_)], axis=1),
        "gdn_w_out": jnp.stack([g["w_out"].reshape(N_DEV, D // N_DEV, D) for g in (ga, gb_)], axis=1),
        "s5_w_in": s5g["w_in"].reshape(N_DEV, 1, D // N_DEV, D), "s5_w_out": s5g["w_out"],
        "m2_w_in": _cols_to_shards(m2g["w_ext"], _M2_IN)[:, None],
        "m2_w_out": m2g["w_out"].reshape(N_DEV, 1, M2_INNER // N_DEV, D),
        "gdn_conv_w": jnp.concatenate([_cols_to_shards(g["conv_w"], 3 * D) for g in (ga, gb_)], axis=1),
        "m2_conv_w": _cols_to_shards(m2g["conv_w"], 2 * M2_INNER),
        "m2_conv_b": _cols_to_shards(m2g["conv_b"], 2 * M2_INNER),
        "m2_norm_g": _cols_to_shards(m2g["norm_g"], M2_INNER),
    }
    sends = [full[n].reshape((N_DEV,) + _as2d(W[n]).shape) for n in _SHARDED]
    core = lax.axis_index("c")
    theirs = _scatter_pair("scatter_pair", sends)
    chip_sums = [_pair_add("pair_add_" + n, full8, th, core, bf16 if n in _MATRICES else f32)
                 for n, full8, th in zip(_SHARDED, sends, theirs)]
    parts = dict(zip(_SHARDED, _scatter_chips("scatter_chips", chip_sums)))

    rep = {
        "norm_mix_g": jnp.concatenate(d_mix, axis=0), "norm_mlp_g": jnp.concatenate(d_mlp, axis=0),
        "gdn_a_log": jnp.stack([ga["a_log"], gb_["a_log"]]), "gdn_dt_bias": jnp.stack([ga["dt_bias"], gb_["dt_bias"]]),
        "gdn_o_norm_g": jnp.stack([ga["o_norm_g"], gb_["o_norm_g"]]),
        "s5_lam_re": s5g["lam_re"], "s5_lam_im": s5g["lam_im"], "s5_log_dt": s5g["log_dt"], "s5_b_re": s5g["b_re"],
        "s5_b_im": s5g["b_im"], "s5_c_re": s5g["c_re"], "s5_c_im": s5g["c_im"], "s5_d": s5g["d"],
        "m2_dt_bias": m2g["dt_bias"], "m2_a_log": m2g["a_log"], "m2_d": m2g["d"], "final_norm_g": d_final,
    }

    def pack(d):
        flat = jnp.concatenate([d[n].reshape(-1).astype(f32) for n in _REPLICATED])
        return jnp.pad(flat, (0, -flat.shape[0] % (256 * 128))).reshape(-1, 128)

    (rep_parts,) = _gather("gather_small_grads", [pack(rep)])

    res = {}
    for n in _SHARDED:
        w2d = _as2d(W[n])
        out = _adamw("adamw_" + n, parts[n], w2d, _as2d(MOM[n]), _as2d(VAR[n]))
        res[n] = [o.reshape(W[n].shape) for o in out]
    out = _adamw("adamw_replicated", rep_parts, pack(W), pack(MOM), pack(VAR))
    off = 0
    for n in _REPLICATED:
        size = W[n].size
        res[n] = [o.reshape(-1)[off:off + size].reshape(W[n].shape) for o in out]
        off += size

    return (loss, grad_x, *[res[n][0] for n in _WEIGHTS], *[res[n][1] for n in _WEIGHTS],
            *[res[n][2] for n in _WEIGHTS], *[res[n][3] for n in _WEIGHTS])
```

```python
import functools

import jax
import jax.numpy as jnp
from jax import lax
from jax.experimental import pallas as pl
from jax.experimental.pallas import tpu as pltpu

f32 = jnp.float32
bf16 = jnp.bfloat16
HI = lax.Precision.HIGHEST
MESH = pl.DeviceIdType.MESH

N_DEV = 8
D_MODEL = 1024
D_FF = 4096
CHUNK = 64
RMS_EPS = 1e-6
GDN_HEADS = 8
GDN_HB = 8
GDN_EXT = 4224
S5_STATE = 64
S5_SCAN_LANES = 256
M2_INNER = 2048
M2_EXT = 6272
M2_HEADS = 32
M2_GB = 4
VMEM_LIMIT_BYTES = 56 * 1024 * 1024

ADAM_LR, ADAM_B1, ADAM_B2, ADAM_EPS, ADAM_WD, ADAM_STEP = 0.001, 0.9, 0.999, 1e-08, 0.01, 10

_NN = ((1,), (0,))
_NT = ((1,), (1,))
_TN = ((0,), (0,))


def _dot(a, b, dims=_NN):
    return lax.dot_general(a, b, (dims, ((), ())), precision=HI, preferred_element_type=f32)


def _dotb(a, b, dims=_NN):
    return lax.dot_general(a.astype(bf16), b.astype(bf16), (dims, ((), ())), preferred_element_type=f32)


def _bdot(p, q, dims):
    return lax.dot_general(p, q, (dims, ((), ())), preferred_element_type=f32)


def _pieces(x, n):
    out = []
    for _ in range(n - 1):
        p = x.astype(bf16)
        out.append(p)
        x = x - p.astype(f32)
    return out + [x.astype(bf16)]


def _dot01_raw(mask, b, dims=_NN, mask_first=True):
    m = mask.astype(bf16)
    p = _pieces(b, 3)
    if mask_first:
        return _bdot(m, p[0], dims) + (_bdot(m, p[1], dims) + _bdot(m, p[2], dims))
    return _bdot(p[0], m, dims) + (_bdot(p[1], m, dims) + _bdot(p[2], m, dims))


@jax.custom_vjp
def _dot01_nn(mask, b):
    return _dot01_raw(mask, b, _NN)


@jax.custom_vjp
def _dot01_nt(mask, b):
    return _dot01_raw(mask, b, _NT)


_dot01_nn.defvjp(lambda m, b: (_dot01_raw(m, b, _NN), m),
                 lambda m, ct: (jnp.zeros_like(m), _dot01_raw(m, ct, _TN, mask_first=True)))
_dot01_nt.defvjp(lambda m, b: (_dot01_raw(m, b, _NT), m),
                 lambda m, ct: (jnp.zeros_like(m), _dot01_raw(m, ct, _TN, mask_first=False)))


def _dot01_vjp(mask, b, dims=_NN):
    return _dot01_nn(mask, b) if dims == _NN else _dot01_nt(mask, b)


def _dot3_raw(a, b, dims=_NN):
    (ah, al), (bh, bl) = _pieces(a, 2), _pieces(b, 2)
    return _bdot(ah, bh, dims) + (_bdot(ah, bl, dims) + _bdot(al, bh, dims))


@jax.custom_vjp
def _dot3_vjp(a, b):
    return _dot3_raw(a, b)


_dot3_vjp.defvjp(lambda a, b: (_dot3_raw(a, b), (a, b)),
                 lambda res, ct: (_dot3_raw(ct, res[1], _NT), _dot3_raw(res[0], ct, _TN)))


class _Dots:
    def __init__(self, dot3, dot01):
        self.dot3, self.dot01 = dot3, dot01


_PLAIN_DOTS = _Dots(_dot3_raw, _dot01_raw)
_VJP_DOTS = _Dots(_dot3_vjp, _dot01_vjp)


def _iota(shape, dim):
    return lax.broadcasted_iota(jnp.int32, shape, dim)


def _params(n_grid):
    return pltpu.CompilerParams(dimension_semantics=("arbitrary",) * n_grid, vmem_limit_bytes=VMEM_LIMIT_BYTES)


def _row_tile(n_rows):
    return min(512, n_rows)


def _head_rows(n_rows):
    return min(2048, n_rows)


def _mm_rows(n_rows):
    return min(1024, n_rows)


def _col_tile(n, cap=1024):
    best = 128
    for t in range(128, cap + 1, 128):
        if n % t == 0:
            best = t
    return best


def _mm(name, a, b, *, dims, grid, a_spec, b_spec, out_shape, out_spec, aux=(), a_fn=None, epi_fn=None,
        acc_shape, out_init=None):
    nk = grid[2]
    n_aux = len(aux)
    kinds = [x[2] for x in aux]

    def body(*refs):
        a_ref, b_ref = refs[0], refs[1]
        aux_refs = refs[2:2 + n_aux]
        pos = 2 + n_aux + (1 if out_init is not None else 0)
        o_ref, acc_ref = refs[pos], refs[pos + 1]
        k = pl.program_id(2)

        @pl.when(k == 0)
        def _():
            acc_ref[...] = jnp.zeros_like(acc_ref)

        av = a_ref[...]
        if a_fn is not None:
            av = a_fn(av, *[r[...] for r, kd in zip(aux_refs, kinds) if kd == "a"])
        acc_ref[...] += lax.dot_general(av.astype(bf16), b_ref[...].astype(bf16), (dims, ((), ())),
                                        preferred_element_type=f32)

        @pl.when(k == nk - 1)
        def _():
            r = acc_ref[...]
            if epi_fn is not None:
                r = epi_fn(r, *[x[...] for x, kd in zip(aux_refs, kinds) if kd == "e"])
            o_ref[...] = r.astype(o_ref.dtype)

    in_specs = [a_spec, b_spec] + [x[1] for x in aux]
    args = [a, b] + [x[0] for x in aux]
    aliases = {}
    if out_init is not None:
        in_specs.append(pl.BlockSpec(memory_space=pl.ANY))
        args.append(out_init)
        aliases = {len(args) - 1: 0}
    return pl.pallas_call(
        body, name=name, grid=grid, in_specs=in_specs, out_specs=out_spec, out_shape=out_shape,
        scratch_shapes=[pltpu.VMEM(acc_shape, f32)], input_output_aliases=aliases,
        compiler_params=_params(3),
    )(*args)


def _ew(name, f, ins, outs, grid):
    n_in = len(ins)
    modes = [o[4] for o in outs]

    def body(*refs):
        vals = [r[...] for r in refs[:n_in]]
        res = f(*vals)
        if not isinstance(res, (tuple, list)):
            res = (res,)
        for r, o_ref, mode in zip(res, refs[n_in:], modes):
            if mode is None:
                o_ref[...] = r.astype(o_ref.dtype)
                continue
            first = pl.program_id(1) == 0
            if mode == "all":
                first = jnp.logical_and(first, pl.program_id(0) == 0)

            @pl.when(first)
            def _(r=r, o_ref=o_ref):
                o_ref[...] = r.astype(o_ref.dtype)

            @pl.when(jnp.logical_not(first))
            def _(r=r, o_ref=o_ref):
                o_ref[...] += r.astype(o_ref.dtype)

    res = pl.pallas_call(
        body, name=name, grid=grid,
        in_specs=[pl.BlockSpec(blk, im) for _, blk, im in ins],
        out_specs=[pl.BlockSpec(o[2], o[3]) for o in outs],
        out_shape=[jax.ShapeDtypeStruct(o[0], o[1]) for o in outs],
        compiler_params=_params(2),
    )(*[a for a, _, _ in ins])
    return res


def _vjp_fn(f, n_primal):
    def g(*args):
        _, vjp = jax.vjp(f, *args[:n_primal])
        cts = args[n_primal:]
        return vjp(cts[0] if len(cts) == 1 else tuple(cts))
    return g


def _scan_fwd(name, step, n_state, state_shape, cins, ins, outs, n_units, n_chunks):
    n_c, n_in, n_out = len(cins), len(ins), len(outs)

    def body(*refs):
        c_refs = refs[:n_c]
        in_refs = refs[n_c:n_c + n_in]
        out_refs = refs[n_c + n_in:n_c + n_in + n_out]
        saved = refs[n_c + n_in + n_out:n_c + n_in + n_out + n_state]
        st = refs[n_c + n_in + n_out + n_state:]

        @pl.when(pl.program_id(1) == 0)
        def _():
            for s in st:
                s[...] = jnp.zeros_like(s)

        cur = [s[...] for s in st]
        for sv, s in zip(saved, cur):
            sv[...] = s
        new, res = step(cur, [r[...] for r in c_refs], [r[...] for r in in_refs], _PLAIN_DOTS)
        for s, n in zip(st, new):
            s[...] = n
        for o, r in zip(out_refs, res):
            o[...] = r

    sshape = (n_units, n_chunks) + state_shape
    sblock = (None, None) + state_shape
    nz = len(state_shape)
    res = pl.pallas_call(
        body, name=name, grid=(n_units, n_chunks),
        in_specs=[pl.BlockSpec(e[1], e[2]) for e in cins + ins],
        out_specs=[pl.BlockSpec(o[1], o[2]) for o in outs]
        + [pl.BlockSpec(sblock, lambda u, c: (u, c) + (0,) * nz)] * n_state,
        out_shape=[jax.ShapeDtypeStruct(o[0], f32) for o in outs]
        + [jax.ShapeDtypeStruct(sshape, f32)] * n_state,
        scratch_shapes=[pltpu.VMEM(state_shape, f32)] * n_state,
        compiler_params=_params(2),
    )(*[e[0] for e in cins + ins])
    return res[:n_out], res[n_out:]


def _scan_bwd(name, step, n_state, state_shape, cins, ins, saved, douts, n_units, n_chunks):
    n_c, n_in, n_do = len(cins), len(ins), len(douts)

    def flip(im):
        return lambda u, c: im(u, n_chunks - 1 - c)

    def body(*refs):
        p = 0
        c_refs = refs[p:p + n_c]; p += n_c
        in_refs = refs[p:p + n_in]; p += n_in
        sv_refs = refs[p:p + n_state]; p += n_state
        do_refs = refs[p:p + n_do]; p += n_do
        dc_refs = refs[p:p + n_c]; p += n_c
        di_refs = refs[p:p + n_in]; p += n_in
        dst = refs[p:]
        first = pl.program_id(1) == 0

        @pl.when(first)
        def _():
            for s in dst:
                s[...] = jnp.zeros_like(s)

        def fn(states, consts, vals):
            new, res = step(states, consts, vals, _VJP_DOTS)
            return tuple(new), tuple(res)

        prim = ([r[...] for r in sv_refs], [r[...] for r in c_refs], [r[...] for r in in_refs])
        _, vjp = jax.vjp(fn, *prim)
        d_states, d_consts, d_vals = vjp((tuple(s[...] for s in dst), tuple(r[...] for r in do_refs)))
        for s, g in zip(dst, d_states):
            s[...] = g
        for o, g in zip(di_refs, d_vals):
            o[...] = g
        for o, g in zip(dc_refs, d_consts):
            @pl.when(first)
            def _(o=o, g=g):
                o[...] = g

            @pl.when(jnp.logical_not(first))
            def _(o=o, g=g):
                o[...] += g

    nz = len(state_shape)
    sblock = (None, None) + state_shape
    def gshape(e):
        return e[3] if len(e) == 5 else e[0].shape

    def gmap(e):
        return e[4] if len(e) == 5 else e[2]

    in_specs = ([pl.BlockSpec(e[1], e[2]) for e in cins]
                + [pl.BlockSpec(e[1], flip(e[2])) for e in ins]
                + [pl.BlockSpec(sblock, lambda u, c: (u, n_chunks - 1 - c) + (0,) * nz)] * n_state
                + [pl.BlockSpec(e[1], flip(e[2])) for e in douts])
    out_specs = ([pl.BlockSpec(e[1], e[2]) for e in cins]
                 + [pl.BlockSpec(e[1], flip(gmap(e))) for e in ins])
    out_shape = [jax.ShapeDtypeStruct(gshape(e), f32) for e in cins + ins]
    res = pl.pallas_call(
        body, name=name, grid=(n_units, n_chunks), in_specs=in_specs, out_specs=out_specs, out_shape=out_shape,
        scratch_shapes=[pltpu.VMEM(state_shape, f32)] * n_state,
        compiler_params=_params(2),
    )(*([e[0] for e in cins + ins] + list(saved) + [e[0] for e in douts]))
    return res[:n_c], res[n_c:]


def _rms(x, g):
    return x * lax.rsqrt(jnp.mean(x * x, axis=-1, keepdims=True) + RMS_EPS) * g


def _rows(tm, width):
    return (tm, width), lambda r, z: (r, 0)


def _const(shape):
    return shape, lambda r, z: (0,) * len(shape)


def _rms_fwd(name, h, g):
    L, D = h.shape
    tm = _row_tile(L)
    return _ew(name, _rms, [(h, *_rows(tm, D)), (g, *_const((1, D)))],
               [((L, D), f32, *_rows(tm, D), None)], (L // tm, 1))[0]


def _rms_bwd(name, h, g, d_hn, d_res):
    L, D = h.shape
    tm = _row_tile(L)

    def f(hv, gv, dv, rv):
        dh, dg = _vjp_fn(_rms, 2)(hv, gv, dv)
        return dh + rv, dg

    return _ew(name, f, [(h, *_rows(tm, D)), (g, *_const((1, D))), (d_hn, *_rows(tm, D)), (d_res, *_rows(tm, D))],
               [((L, D), f32, *_rows(tm, D), None), ((1, D), f32, *_const((1, D)), "all")], (L // tm, 1))


def _loss_head(h, g, target):
    L, D = h.shape
    tm = _row_tile(L)

    def f(hv, gv, tv):
        def lf(a, b):
            e = jnp.square(_rms(a, b) - tv)
            return (0.5 / D) * jnp.sum(jnp.sum(e, axis=1, keepdims=True), axis=0, keepdims=True)

        val, vjp = jax.vjp(lf, hv, gv)
        dh, dg = vjp(jnp.ones((1, 1), f32))
        return jnp.broadcast_to(val, (1, 128)), dh, dg

    return _ew("loss_head", f, [(h, *_rows(tm, D)), (g, *_const((1, D))), (target, *_rows(tm, D))],
               [((1, 128), f32, *_const((1, 128)), "all"), ((L, D), f32, *_rows(tm, D), None),
                ((1, D), f32, *_const((1, D)), "all")], (L // tm, 1))


def _sqrelu(x):
    return jnp.square(jnp.maximum(x, 0.0))


def _mm_plain(name, a, b, dims, *, a_fn=None, epi_fn=None, aux=()):
    if dims == _NN:
        (M, K), N = a.shape, b.shape[1]
    elif dims == _NT:
        (M, K), N = a.shape, b.shape[0]
    else:
        (K, M), N = a.shape, b.shape[1]
    tm = _mm_rows(M)
    tn = _col_tile(N, 1536)
    tk = _col_tile(K)
    if dims == _TN:
        tk = min(512, K)
        a_spec = pl.BlockSpec((tk, tm), lambda i, j, k: (k, i))
        b_spec = pl.BlockSpec((tk, tn), lambda i, j, k: (k, j))
        a_aux = pl.BlockSpec((tk, tm), lambda i, j, k: (k, i))
    elif dims == _NT:
        a_spec = pl.BlockSpec((tm, tk), lambda i, j, k: (i, k))
        b_spec = pl.BlockSpec((tn, tk), lambda i, j, k: (j, k))
        a_aux = pl.BlockSpec((tm, tk), lambda i, j, k: (i, k))
    else:
        a_spec = pl.BlockSpec((tm, tk), lambda i, j, k: (i, k))
        b_spec = pl.BlockSpec((tk, tn), lambda i, j, k: (k, j))
        a_aux = pl.BlockSpec((tm, tk), lambda i, j, k: (i, k))
    e_aux = pl.BlockSpec((tm, tn), lambda i, j, k: (i, j))
    aux_full = [(x, a_aux if kd == "a" else e_aux, kd) for x, kd in aux]
    return _mm(name, a, b, dims=dims, grid=(M // tm, N // tn, K // tk), a_spec=a_spec, b_spec=b_spec,
               out_shape=jax.ShapeDtypeStruct((M, N), f32), out_spec=pl.BlockSpec((tm, tn), lambda i, j, k: (i, j)),
               aux=aux_full, a_fn=a_fn, epi_fn=epi_fn, acc_shape=(tm, tn))


def _mlp_fwd(h, g, w1g, w2g, layer):
    L, D = h.shape
    tm = _mm_rows(L)
    fs = D_FF // N_DEV
    hn = _rms_fwd("mlp_norm", h, g)
    h1 = _mm("mlp_up", hn, w1g, dims=_NN, grid=(L // tm, N_DEV, 1),
             a_spec=pl.BlockSpec((tm, D), lambda i, j, k: (i, 0)),
             b_spec=pl.BlockSpec((None, None, D, fs), lambda i, j, k: (j, layer, 0, 0)),
             out_shape=jax.ShapeDtypeStruct((L, D_FF), f32), out_spec=pl.BlockSpec((tm, fs), lambda i, j, k: (i, j)),
             acc_shape=(tm, fs))
    tn = D
    h_out = _mm("mlp_down", h1, w2g, dims=_NN, grid=(L // tm, D // tn, N_DEV),
                a_spec=pl.BlockSpec((tm, fs), lambda i, j, k: (i, k)),
                b_spec=pl.BlockSpec((None, None, fs, tn), lambda i, j, k: (k, layer, 0, j)),
                out_shape=jax.ShapeDtypeStruct((L, D), f32), out_spec=pl.BlockSpec((tm, tn), lambda i, j, k: (i, j)),
                aux=[(h, pl.BlockSpec((tm, tn), lambda i, j, k: (i, j)), "e")],
                a_fn=_sqrelu, epi_fn=lambda acc, res: acc + res, acc_shape=(tm, tn))
    return h_out, hn, h1


def _mlp_bwd(dh, h, g, hn, h1, w1g, w2g, wl, layer, dw1_buf, dw2_buf):
    L, D = h.shape
    tm = _mm_rows(L)
    fs = D_FF // N_DEV
    tk = _mm_rows(L)
    dh1 = _mm("mlp_down_dx", dh, w2g, dims=_NT, grid=(L // tm, N_DEV, 1),
              a_spec=pl.BlockSpec((tm, D), lambda i, j, k: (i, 0)),
              b_spec=pl.BlockSpec((None, None, fs, D), lambda i, j, k: (j, wl, 0, 0)),
              out_shape=jax.ShapeDtypeStruct((L, D_FF), f32), out_spec=pl.BlockSpec((tm, fs), lambda i, j, k: (i, j)),
              aux=[(h1, pl.BlockSpec((tm, fs), lambda i, j, k: (i, j)), "e")],
              epi_fn=lambda acc, pre: acc * (2.0 * jnp.maximum(pre, 0.0)), acc_shape=(tm, fs))
    dw2_buf = _mm("mlp_down_dw", h1, dh, dims=_TN, grid=(N_DEV, 1, L // tk),
                  a_spec=pl.BlockSpec((tk, fs), lambda i, j, k: (k, i)),
                  b_spec=pl.BlockSpec((tk, D), lambda i, j, k: (k, 0)),
                  out_shape=jax.ShapeDtypeStruct(dw2_buf.shape, f32),
                  out_spec=pl.BlockSpec((None, None, fs, D), lambda i, j, k: (i, layer, 0, 0)),
                  a_fn=_sqrelu, acc_shape=(fs, D), out_init=dw2_buf)
    tr = D
    dw1_buf = _mm("mlp_up_dw", hn, dh1, dims=_TN, grid=(D // tr, N_DEV, L // tk),
                  a_spec=pl.BlockSpec((tk, tr), lambda i, j, k: (k, i)),
                  b_spec=pl.BlockSpec((tk, fs), lambda i, j, k: (k, j)),
                  out_shape=jax.ShapeDtypeStruct(dw1_buf.shape, f32),
                  out_spec=pl.BlockSpec((None, None, tr, fs), lambda i, j, k: (j, layer, i, 0)),
                  acc_shape=(tr, fs), out_init=dw1_buf)
    tn = D
    dhn = _mm("mlp_up_dx", dh1, w1g, dims=_NT, grid=(L // tm, D // tn, N_DEV),
              a_spec=pl.BlockSpec((tm, fs), lambda i, j, k: (i, k)),
              b_spec=pl.BlockSpec((None, None, tn, fs), lambda i, j, k: (k, wl, j, 0)),
              out_shape=jax.ShapeDtypeStruct((L, D), f32), out_spec=pl.BlockSpec((tm, tn), lambda i, j, k: (i, j)),
              acc_shape=(tm, tn))
    dh_in, dg = _rms_bwd("mlp_norm_bwd", h, g, dhn, dh)
    return dh_in, dg, dw1_buf, dw2_buf


def _shift_dn(x, s, row):
    return x if s == 0 else jnp.where(row >= s, pltpu.roll(x, s, 0), 0.0)


def _shift_up(x, s, row):
    n = x.shape[0]
    return x if s == 0 else jnp.where(row < n - s, pltpu.roll(x, n - s, 0), 0.0)


def _conv_pre(x, w, b, row):
    c = jnp.broadcast_to(b, x.shape)
    for j in range(4):
        c = c + w[j:j + 1, :] * _shift_dn(x, 3 - j, row)
    return c


def _conv_fwd(name, x_arr, blk_off, w, b):
    L = x_arr.shape[0]
    C = w.shape[1]

    def f(x, wv, bv):
        c = _conv_pre(x, wv, bv, _iota(x.shape, 0))
        return c * jax.nn.sigmoid(c)

    return _ew(name, f, [(x_arr, (L, 128), lambda j, z: (0, blk_off + j)), (w, (4, 128), lambda j, z: (0, j)),
                         (b, (1, 128), lambda j, z: (0, j))],
               [((L, C), f32, (L, 128), lambda j, z: (0, j), None)], (C // 128, 1))[0]


def _conv_bwd(name, x_arr, blk_off, w, b, dy):
    L = x_arr.shape[0]
    C = w.shape[1]

    def f(x, wv, bv, g):
        row = _iota(x.shape, 0)
        c = _conv_pre(x, wv, bv, row)
        s = jax.nn.sigmoid(c)
        dc = g * (s * (1.0 + c * (1.0 - s)))
        dx = jnp.zeros_like(x)
        dw = jnp.zeros((4, 128), f32)
        r4 = _iota((4, 128), 0)
        for j in range(4):
            dx = dx + wv[j:j + 1, :] * _shift_up(dc, 3 - j, row)
            dwj = jnp.sum(dc * _shift_dn(x, 3 - j, row), axis=0, keepdims=True)
            dw = dw + jnp.where(r4 == j, jnp.broadcast_to(dwj, (4, 128)), 0.0)
        return dx, dw, jnp.sum(dc, axis=0, keepdims=True)

    return _ew(name, f, [(x_arr, (L, 128), lambda j, z: (0, blk_off + j)), (w, (4, 128), lambda j, z: (0, j)),
                         (b, (1, 128), lambda j, z: (0, j)), (dy, (L, 128), lambda j, z: (0, j))],
               [((L, C), f32, (L, 128), lambda j, z: (0, j), None), ((4, C), f32, (4, 128), lambda j, z: (0, j), None),
                ((1, C), f32, (1, 128), lambda j, z: (0, j), None)], (C // 128, 1))


def _l2norm(t):
    return t * lax.rsqrt(jnp.sum(t * t, axis=-1, keepdims=True) + 1e-6)


def _gdn_act(cq, ck, ab, alog, dtb):
    h = pl.program_id(1)
    qn = _l2norm(cq) * (128.0 ** -0.5)
    kn = _l2norm(ck)
    lane = _iota(ab.shape, 1)
    a_raw = jnp.sum(jnp.where(lane == h, ab, 0.0), axis=1, keepdims=True)
    b_raw = jnp.sum(jnp.where(lane == h + GDN_HEADS, ab, 0.0), axis=1, keepdims=True)
    lane1 = _iota(alog.shape, 1)
    al = jnp.sum(jnp.where(lane1 == h, alog, 0.0), axis=1, keepdims=True)
    db = jnp.sum(jnp.where(lane1 == h, dtb, 0.0), axis=1, keepdims=True)
    g = -jnp.exp(al) * jax.nn.softplus(a_raw + db)
    beta = jax.nn.sigmoid(b_raw)
    return qn, kn, jnp.broadcast_to(g, cq.shape), jnp.broadcast_to(beta, cq.shape)


def _each(f, *lists):
    return [f(*a) for a in zip(*lists)]


def _gdn_chunk(states, consts, vals, dots):
    S = list(states)
    cut = [slice(128 * i, 128 * i + 128) for i in range(len(S))]
    q, k, v, gb, bb = ([t[:, c] for c in cut] for t in vals)
    C = vals[0].shape[0]
    row, col = _iota((C, C), 0), _iota((C, C), 1)
    causal, strict = row >= col, row > col
    ltri = causal.astype(f32)
    eye = (row == col).astype(f32)
    e0 = (_iota((C, 128), 1) == 0).astype(f32)
    last = _iota((C, 1), 0) == C - 1
    Gb = _each(lambda g: dots.dot01(ltri, g), gb)
    Gc = _each(lambda g: jnp.mean(g, axis=1, keepdims=True), Gb)
    Gr = _each(lambda g: dots.dot01(e0, g, _NT), Gb)
    bc = _each(lambda b: jnp.mean(b, axis=1, keepdims=True), bb)
    decay = _each(lambda gc, gr: jnp.where(causal, jnp.exp(jnp.where(causal, gc - gr, 0.0)), 0.0), Gc, Gr)
    kk = _each(lambda a: _dotb(a, a, _NT), k)
    A = _each(lambda b, x, d: jnp.where(strict, b * x * d, 0.0), bc, kk, decay)
    M = _each(lambda a: eye - a, A)
    P = _each(lambda a: dots.dot3(a, a), A)
    for it in range(5):
        M = _each(lambda m, p: m + dots.dot3(m, p), M, P)
        if it < 4:
            P = _each(lambda p: dots.dot3(p, p), P)
    eG = _each(jnp.exp, Gc)
    u = _each(lambda m, x, b: _dotb(m, x * b), M, v, bc)
    w = _each(lambda m, x, b, e: _dotb(m, x * (b * e)), M, k, bc, eG)
    qk = _each(lambda a, b, d: _dotb(a, b, _NT) * d, q, k, decay)
    g_last = _each(lambda gc: jnp.sum(jnp.where(last, gc, 0.0), axis=0, keepdims=True), Gc)
    v_new = _each(lambda a, b, s: a - _dotb(b, s), u, w, S)
    o = _each(lambda a, e, s, b, x: _dotb(a * e, s) + _dotb(b, x), q, eG, S, qk, v_new)
    S_new = _each(lambda gl, s, a, gc, x: jnp.exp(gl) * s + _dotb(a * jnp.exp(gl - gc), x, _TN), g_last, S, k, Gc, v_new)
    return S_new, [jnp.concatenate(o, axis=1)]


def _gdn_post(o, gate, g):
    return _rms(o, g) * (gate * jax.nn.sigmoid(gate))


def _pad_row(v):
    return jnp.pad(v.astype(f32), (0, 128 - v.shape[0])).reshape(1, 128)


def _gdn_fwd(h, g_norm, w_ext, conv_w, a_log, dt_bias, o_norm_g, w_out):
    L, D = h.shape
    tm = _head_rows(L)
    nc = L // CHUNK
    H = GDN_HEADS
    hn = _rms_fwd("mix_norm", h, g_norm)
    proj = _mm_plain("gdn_in", hn, w_ext, _NN)
    zb = jnp.zeros((1, 3 * D), f32)
    cq = _conv_fwd("gdn_conv", proj, 0, conv_w, zb)
    alog, dtb = _pad_row(a_log), _pad_row(dt_bias)
    act_ins = [(cq, (tm, 128), lambda r, hh: (r, hh)), (cq, (tm, 128), lambda r, hh: (r, H + hh)),
               (proj, (tm, 128), lambda r, hh: (r, 4 * H)), (alog, (1, 128), lambda r, hh: (0, 0)),
               (dtb, (1, 128), lambda r, hh: (0, 0))]
    qn, kn, gb, bb = _ew("gdn_act", _gdn_act, act_ins,
                         [((L, D), f32, (tm, 128), lambda r, hh: (r, hh), None)] * 4, (L // tm, H))
    cblk = (CHUNK, 128 * GDN_HB)
    core_ins = [(qn, cblk, lambda u, c: (c, u)), (kn, cblk, lambda u, c: (c, u)),
                (cq, cblk, lambda u, c: (c, 2 * H // GDN_HB + u), (L, D), lambda u, c: (c, u)),
                (gb, cblk, lambda u, c: (c, u)), (bb, cblk, lambda u, c: (c, u))]
    (o,), saved_s = _scan_fwd("gdn_core", _gdn_chunk, GDN_HB, (128, 128), [], core_ins,
                              [((L, D), cblk, lambda u, c: (c, u))], H // GDN_HB, nc)
    on = o_norm_g.reshape(1, 128)
    post_ins = [(o, (tm, 128), lambda r, hh: (r, hh)), (proj, (tm, 128), lambda r, hh: (r, 3 * H + hh)),
                (on, (1, 128), lambda r, hh: (0, 0))]
    y = _ew("gdn_post", _gdn_post, post_ins, [((L, D), f32, (tm, 128), lambda r, hh: (r, hh), None)], (L // tm, H))[0]
    h_out = _mm_plain("gdn_out", y, w_out, _NN, epi_fn=lambda acc, res: acc + res, aux=[(h, "e")])
    saved = dict(hn=hn, proj=proj, cq=cq, alog=alog, dtb=dtb, act_ins=act_ins, core_ins=core_ins, saved_s=saved_s,
                 post_ins=post_ins, y=y, zb=zb)
    return h_out, saved


def _gdn_bwd(dh, h, g_norm, w_ext, conv_w, w_out, sv):
    L, D = h.shape
    tm = _head_rows(L)
    nc = L // CHUNK
    H = GDN_HEADS
    dy = _mm_plain("gdn_out_dx", dh, w_out, _NT)
    dw_out = _mm_plain("gdn_out_dw", sv["y"], dh, _TN)
    hd = ((L, D), f32, (tm, 128), lambda r, hh: (r, hh), None)
    d_o, d_gate, d_on = _ew("gdn_post_bwd", _vjp_fn(_gdn_post, 3),
                            sv["post_ins"] + [(dy, (tm, 128), lambda r, hh: (r, hh))],
                            [hd, hd, ((1, 128), f32, (1, 128), lambda r, hh: (0, 0), "all")], (L // tm, H))
    cblk = (CHUNK, 128 * GDN_HB)
    _, (dqn, dkn, dv, dgb, dbb) = _scan_bwd("gdn_core_bwd", _gdn_chunk, GDN_HB, (128, 128), [], sv["core_ins"],
                                            sv["saved_s"], [(d_o, cblk, lambda u, c: (c, u))], H // GDN_HB, nc)
    cts = [(t, (tm, 128), lambda r, hh: (r, hh)) for t in (dqn, dkn, dgb, dbb)]
    row128 = ((1, 128), f32, (1, 128), lambda r, hh: (0, 0), "all")
    d_cq, d_ck, d_ab, d_alog, d_dtb = _ew(
        "gdn_act_bwd", _vjp_fn(_gdn_act, 5), sv["act_ins"] + cts,
        [hd, hd, ((L, 128), f32, (tm, 128), lambda r, hh: (r, 0), "inner"), row128, row128], (L // tm, H))
    d_conv_out = jnp.concatenate([d_cq, d_ck, dv], axis=1)
    d_conv_in, d_conv_w, _ = _conv_bwd("gdn_conv_bwd", sv["proj"], 0, conv_w, sv["zb"], d_conv_out)
    d_proj = jnp.concatenate([d_conv_in, d_gate, d_ab], axis=1)
    dw_ext = _mm_plain("gdn_in_dw", sv["hn"], d_proj, _TN)
    dhn = _mm_plain("gdn_in_dx", d_proj, w_ext, _NT)
    dh_in, dg = _rms_bwd("mix_norm_bwd", h, g_norm, dhn, dh)
    grads = dict(norm=dg, w_ext=dw_ext, conv_w=d_conv_w, a_log=d_alog[0, :H], dt_bias=d_dtb[0, :H],
                 o_norm_g=d_on[0], w_out=dw_out)
    return dh_in, grads


def _expand_lanes(row, width, rep):
    sel = ((_iota((128, width), 1) // rep) == _iota((128, width), 0)).astype(f32)
    return jnp.mean(_dot(jnp.broadcast_to(row, (8, 128)), sel), axis=0, keepdims=True)


def _s5_params(lre, lim, ldt, wbr, wbi):
    dt = jnp.exp(_expand_lanes(ldt, 512, S5_STATE))
    mag = jnp.exp(lre * dt)
    ang = lim * dt
    abr, abi = mag * jnp.cos(ang), mag * jnp.sin(ang)
    nr = abr - 1.0
    den = lre * lre + lim * lim
    cr = (nr * lre + abi * lim) / den
    ci = (abi * lre - nr * lim) / den
    return abr, abi, cr * wbr - ci * wbi, cr * wbi + ci * wbr


def _s5_scan(name, xr, xi, ar, ai, rev, want_prev):
    L, W = xr.shape
    nb = L // 8
    n_out = 4 if want_prev else 2

    def body(xr_ref, xi_ref, ar_ref, ai_ref, *outs):
        a_r = ar_ref[...]
        a_i = -ai_ref[...] if rev else ai_ref[...]

        def cm(p, q):
            return p[0] * q[0] - p[1] * q[1], p[0] * q[1] + p[1] * q[0]

        a1 = (a_r, a_i)
        a2 = cm(a1, a1)
        a3 = cm(a2, a1)
        a4 = cm(a2, a2)
        pw = [a1, a2, a3, a4, cm(a4, a1), cm(a4, a2), cm(a4, a3), cm(a4, a4)]
        blk8 = (8, S5_SCAN_LANES)
        row = _iota(blk8, 0)
        tab_r = jnp.zeros(blk8, f32)
        tab_i = jnp.zeros(blk8, f32)
        for t in range(8):
            idx = 7 - t if rev else t
            tab_r = jnp.where(row == idx, jnp.broadcast_to(pw[t][0], blk8), tab_r)
            tab_i = jnp.where(row == idx, jnp.broadcast_to(pw[t][1], blk8), tab_i)
        lv = [(d, jnp.broadcast_to(p[0], blk8), jnp.broadcast_to(p[1], blk8)) for d, p in ((1, a1), (2, a2), (4, a4))]

        def step(i, carry):
            cr, ci = carry
            blk = nb - 1 - i if rev else i
            r0 = pl.multiple_of(blk * 8, 8)
            x_r = xr_ref[pl.ds(r0, 8), :]
            x_i = xi_ref[pl.ds(r0, 8), :]
            for d, p_r, p_i in lv:
                if rev:
                    s_r = jnp.where(row < 8 - d, pltpu.roll(x_r, 8 - d, 0), 0.0)
                    s_i = jnp.where(row < 8 - d, pltpu.roll(x_i, 8 - d, 0), 0.0)
                else:
                    s_r = jnp.where(row >= d, pltpu.roll(x_r, d, 0), 0.0)
                    s_i = jnp.where(row >= d, pltpu.roll(x_i, d, 0), 0.0)
                x_r, x_i = x_r + p_r * s_r - p_i * s_i, x_i + p_r * s_i + p_i * s_r
            x_r, x_i = x_r + tab_r * cr - tab_i * ci, x_i + tab_r * ci + tab_i * cr
            outs[0][pl.ds(r0, 8), :] = x_r
            outs[1][pl.ds(r0, 8), :] = x_i
            if want_prev:
                outs[2][pl.ds(r0, 8), :] = jnp.where(row >= 1, pltpu.roll(x_r, 1, 0), cr)
                outs[3][pl.ds(r0, 8), :] = jnp.where(row >= 1, pltpu.roll(x_i, 1, 0), ci)
            e = 0 if rev else 7
            return jnp.broadcast_to(x_r[e:e + 1, :], blk8), jnp.broadcast_to(x_i[e:e + 1, :], blk8)

        lax.fori_loop(0, nb, step, (jnp.zeros(blk8, f32), jnp.zeros(blk8, f32)))

    per = 512 // S5_SCAN_LANES
    col = pl.BlockSpec((L, S5_SCAN_LANES), lambda q, z: (0, q))
    aspec = pl.BlockSpec((None, 1, S5_SCAN_LANES), lambda q, z: (q // per, 0, q % per))
    return pl.pallas_call(
        body, name=name, grid=(W // S5_SCAN_LANES, 1), in_specs=[col, col, aspec, aspec], out_specs=[col] * n_out,
        out_shape=[jax.ShapeDtypeStruct((L, W), f32)] * n_out, compiler_params=_params(2),
    )(xr, xi, ar, ai)


def _blockdiag(t, n_in, n_out):
    t4 = t.reshape(8, 8, n_in, n_out)
    return jnp.einsum("jaio,ab->jaibo", t4, jnp.eye(8, dtype=t.dtype)).reshape(8, 8 * n_in, 8 * n_out)


def _blockdiag_t(w, n_in, n_out):
    w5 = w.reshape(8, 8, n_in, 8, n_out)
    return jnp.einsum("jaibo,ab->jaio", w5, jnp.eye(8, dtype=w.dtype)).reshape(64, n_in, n_out)


def _glu(ag, h):
    n = ag.shape[1] // 2
    return h + ag[:, :n] * jax.nn.sigmoid(ag[:, n:])


def _s5_fwd(h, g_norm, w_in, lam_re, lam_im, log_dt, b_re, b_im, c_re, c_im, d_skip, w_out_g):
    L, D = h.shape
    tm, te = _mm_rows(L), _row_tile(L)
    W = 8 * 512
    hn = _rms_fwd("mix_norm", h, g_norm)
    u = _mm_plain("s5_in", hn, w_in, _NN)
    lre, lim = lam_re.reshape(8, 1, 512), lam_im.reshape(8, 1, 512)
    ldt = jnp.pad(log_dt.reshape(8, 1, 8), ((0, 0), (0, 0), (0, 120)))
    wbr = _blockdiag(b_re.transpose(0, 2, 1), 16, 64)
    wbi = _blockdiag(b_im.transpose(0, 2, 1), 16, 64)
    wcr = _blockdiag(c_re.transpose(0, 2, 1), 64, 16)
    wci = _blockdiag(c_im.transpose(0, 2, 1), 64, 16)
    jb = lambda shape: (shape, lambda j, z: (j, 0, 0))
    par_ins = [(lre, *jb((None, 1, 512))), (lim, *jb((None, 1, 512))), (ldt, *jb((None, 1, 128))),
               (wbr, *jb((None, 128, 512))), (wbi, *jb((None, 128, 512)))]
    abr, abi, bbr, bbi = _ew("s5_params", _s5_params, par_ins,
                             [((8, 1, 512), f32, *jb((None, 1, 512)), None)] * 2
                             + [((8, 128, 512), f32, *jb((None, 128, 512)), None)] * 2, (8, 1))

    def bu(name, wb):
        return _mm(name, u, wb, dims=_NN, grid=(L // tm, 8, 1),
                   a_spec=pl.BlockSpec((tm, 128), lambda i, j, k: (i, j)),
                   b_spec=pl.BlockSpec((None, 128, 512), lambda i, j, k: (j, 0, 0)),
                   out_shape=jax.ShapeDtypeStruct((L, W), f32), out_spec=pl.BlockSpec((tm, 512), lambda i, j, k: (i, j)),
                   acc_shape=(tm, 512))

    bur, bui = bu("s5_bu", bbr), bu("s5_bu", bbi)
    sr, si, pr, pi = _s5_scan("s5_scan", bur, bui, abr, abi, False, True)
    d_row = d_skip.reshape(1, D)
    cspec = dict(a_spec=pl.BlockSpec((tm, 512), lambda i, j, k: (i, j)),
                 b_spec=pl.BlockSpec((None, 512, 128), lambda i, j, k: (j, 0, 0)),
                 out_shape=jax.ShapeDtypeStruct((L, D), f32), out_spec=pl.BlockSpec((tm, 128), lambda i, j, k: (i, j)),
                 acc_shape=(tm, 128))
    e128 = pl.BlockSpec((tm, 128), lambda i, j, k: (i, j))
    pre1 = _mm("s5_c_re", sr, wcr, dims=_NN, grid=(L // tm, 8, 1), **cspec)
    pre = _mm("s5_c_im", si, wci, dims=_NN, grid=(L // tm, 8, 1),
              aux=[(pre1, e128, "e"), (u, e128, "e"), (d_row, pl.BlockSpec((1, 128), lambda i, j, k: (0, j)), "e")],
              epi_fn=lambda acc, p1, uu, dd: p1 - acc + dd * uu, **cspec)
    ws = D // N_DEV * 2
    ag = _mm("s5_out", pre, w_out_g, dims=_NN, grid=(L // tm, N_DEV, 1),
             a_spec=pl.BlockSpec((tm, D), lambda i, j, k: (i, 0)),
             b_spec=pl.BlockSpec((None, None, D, ws), lambda i, j, k: (j, 0, 0, 0)),
             out_shape=jax.ShapeDtypeStruct((L, 2 * D), f32), out_spec=pl.BlockSpec((tm, ws), lambda i, j, k: (i, j)),
             a_fn=jax.nn.gelu, acc_shape=(tm, ws))
    h_out = _ew("s5_glu", _glu, [(ag, *_rows(te, 2 * D)), (h, *_rows(te, D))],
                [((L, D), f32, *_rows(te, D), None)], (L // te, 1))[0]
    saved = dict(hn=hn, u=u, par_ins=par_ins, abr=abr, abi=abi, bbr=bbr, bbi=bbi, wcr=wcr, wci=wci, sr=sr, si=si,
                 pr=pr, pi=pi, pre=pre, ag=ag, d_row=d_row)
    return h_out, saved


def _s5_bwd(dh, h, g_norm, w_in, w_out_g, sv):
    L, D = h.shape
    tm, te = _mm_rows(L), _row_tile(L)
    tk = min(512, L)
    W = 8 * 512
    ws = D // N_DEV * 2
    u, pre, d_row = sv["u"], sv["pre"], sv["d_row"]
    d_ag = _ew("s5_glu_bwd", lambda ag, hv, g: _vjp_fn(_glu, 2)(ag, hv, g)[0],
               [(sv["ag"], *_rows(te, 2 * D)), (h, *_rows(te, D)), (dh, *_rows(te, D))],
               [((L, 2 * D), f32, *_rows(te, 2 * D), None)], (L // te, 1))[0]
    tr = D
    dw_out = _mm("s5_out_dw", pre, d_ag, dims=_TN, grid=(D // tr, N_DEV, L // tk),
                 a_spec=pl.BlockSpec((tk, tr), lambda i, j, k: (k, i)),
                 b_spec=pl.BlockSpec((tk, ws), lambda i, j, k: (k, j)),
                 out_shape=jax.ShapeDtypeStruct((N_DEV, 1, D, ws), f32),
                 out_spec=pl.BlockSpec((None, None, tr, ws), lambda i, j, k: (j, 0, i, 0)),
                 a_fn=jax.nn.gelu, acc_shape=(tr, ws))
    tn = 512
    dpre = _mm("s5_out_dx", d_ag, w_out_g, dims=_NT, grid=(L // tm, D // tn, N_DEV),
               a_spec=pl.BlockSpec((tm, ws), lambda i, j, k: (i, k)),
               b_spec=pl.BlockSpec((None, None, tn, ws), lambda i, j, k: (k, 0, j, 0)),
               out_shape=jax.ShapeDtypeStruct((L, D), f32), out_spec=pl.BlockSpec((tm, tn), lambda i, j, k: (i, j)),
               aux=[(pre, pl.BlockSpec((tm, tn), lambda i, j, k: (i, j)), "e")],
               epi_fn=lambda acc, p: _vjp_fn(jax.nn.gelu, 1)(p, acc)[0], acc_shape=(tm, tn))
    d_d = _ew("s5_dskip", lambda a, b: jnp.sum(a * b, axis=0, keepdims=True),
              [(dpre, *_rows(te, D)), (u, *_rows(te, D))], [((1, D), f32, *_const((1, D)), "all")], (L // te, 1))[0]
    neg = lambda acc: -acc
    dsspec = dict(dims=_NT, grid=(L // tm, 8, 1), a_spec=pl.BlockSpec((tm, 128), lambda i, j, k: (i, j)),
                  b_spec=pl.BlockSpec((None, 512, 128), lambda i, j, k: (j, 0, 0)),
                  out_shape=jax.ShapeDtypeStruct((L, W), f32), out_spec=pl.BlockSpec((tm, 512), lambda i, j, k: (i, j)),
                  acc_shape=(tm, 512))
    dsr = _mm("s5_c_re_dx", dpre, sv["wcr"], **dsspec)
    dsi = _mm("s5_c_im_dx", dpre, sv["wci"], epi_fn=neg, **dsspec)
    dwspec = dict(dims=_TN, grid=(8, 1, L // tk), a_spec=pl.BlockSpec((tk, 512), lambda i, j, k: (k, i)),
                  b_spec=pl.BlockSpec((tk, 128), lambda i, j, k: (k, i)),
                  out_shape=jax.ShapeDtypeStruct((8, 512, 128), f32),
                  out_spec=pl.BlockSpec((None, 512, 128), lambda i, j, k: (i, 0, 0)), acc_shape=(512, 128))
    dwcr = _mm("s5_c_re_dw", sv["sr"], dpre, **dwspec)
    dwci = _mm("s5_c_im_dw", sv["si"], dpre, epi_fn=neg, **dwspec)
    lr, li = _s5_scan("s5_scan_bwd", dsr, dsi, sv["abr"], sv["abi"], True, False)

    def da(lrv, liv, prv, piv):
        return (jnp.sum(lrv * prv + liv * piv, axis=0, keepdims=True),
                jnp.sum(liv * prv - lrv * piv, axis=0, keepdims=True))

    sblk = ((te, 512), lambda j, r: (r, j))
    dabr, dabi = _ew("s5_dlam", da, [(lr, *sblk), (li, *sblk), (sv["pr"], *sblk), (sv["pi"], *sblk)],
                     [((8, 1, 512), f32, (None, 1, 512), lambda j, r: (j, 0, 0), "inner")] * 2, (8, L // te))
    dbspec = dict(dims=_TN, grid=(8, 1, L // tk), a_spec=pl.BlockSpec((tk, 128), lambda i, j, k: (k, i)),
                  b_spec=pl.BlockSpec((tk, 512), lambda i, j, k: (k, i)),
                  out_shape=jax.ShapeDtypeStruct((8, 128, 512), f32),
                  out_spec=pl.BlockSpec((None, 128, 512), lambda i, j, k: (i, 0, 0)), acc_shape=(128, 512))
    dbbr = _mm("s5_bu_dw", u, lr, **dbspec)
    dbbi = _mm("s5_bu_dw", u, li, **dbspec)
    duspec = dict(dims=_NT, grid=(L // tm, 8, 1), a_spec=pl.BlockSpec((tm, 512), lambda i, j, k: (i, j)),
                  b_spec=pl.BlockSpec((None, 128, 512), lambda i, j, k: (j, 0, 0)),
                  out_shape=jax.ShapeDtypeStruct((L, D), f32), out_spec=pl.BlockSpec((tm, 128), lambda i, j, k: (i, j)),
                  acc_shape=(tm, 128))
    e128 = pl.BlockSpec((tm, 128), lambda i, j, k: (i, j))
    du1 = _mm("s5_bu_dx_re", lr, sv["bbr"], **duspec)
    du = _mm("s5_bu_dx_im", li, sv["bbi"],
             aux=[(du1, e128, "e"), (dpre, e128, "e"), (d_row, pl.BlockSpec((1, 128), lambda i, j, k: (0, j)), "e")],
             epi_fn=lambda acc, d1, dp, dd: acc + d1 + dp * dd, **duspec)
    jb = lambda shape: (shape, lambda j, z: (j, 0, 0))
    cts = [(dabr, *jb((None, 1, 512))), (dabi, *jb((None, 1, 512))), (dbbr, *jb((None, 128, 512))),
           (dbbi, *jb((None, 128, 512)))]
    dlre, dlim, dldt, dwbr, dwbi = _ew(
        "s5_params_bwd", _vjp_fn(_s5_params, 5), sv["par_ins"] + cts,
        [((8, 1, 512), f32, *jb((None, 1, 512)), None)] * 2 + [((8, 1, 128), f32, *jb((None, 1, 128)), None)]
        + [((8, 128, 512), f32, *jb((None, 128, 512)), None)] * 2, (8, 1))
    dw_in = _mm_plain("s5_in_dw", sv["hn"], du, _TN)
    dhn = _mm_plain("s5_in_dx", du, w_in, _NT)
    dh_in, dg = _rms_bwd("mix_norm_bwd", h, g_norm, dhn, dh)
    grads = dict(norm=dg, w_in=dw_in, lam_re=dlre.reshape(64, 64), lam_im=dlim.reshape(64, 64),
                 log_dt=dldt[:, 0, :8].reshape(64),
                 b_re=_blockdiag_t(dwbr, 16, 64).transpose(0, 2, 1), b_im=_blockdiag_t(dwbi, 16, 64).transpose(0, 2, 1),
                 c_re=_blockdiag_t(dwcr, 64, 16).transpose(0, 2, 1), c_im=_blockdiag_t(dwci, 64, 16).transpose(0, 2, 1),
                 d=d_d[0], w_out=dw_out)
    return dh_in, grads


def _m2_act(dt_raw, dtbias, alog):
    dt = jax.nn.softplus(dt_raw + dtbias)
    da = dt * (-jnp.exp(alog))
    sel = ((_iota((128, M2_INNER), 1) // 64) == _iota((128, M2_INNER), 0)).astype(f32)
    return _dot(dt, sel), _dot(da, sel)


def _m2_dexp(d):
    return _expand_lanes(d, M2_INNER, 64)


def _ssd_chunk(states, consts, vals, dots):
    (dsk,) = consts
    S = list(states)
    n = len(S)
    cut = [slice(128 * i, 128 * i + 128) for i in range(n)]
    x, dtb, dab = ([t[:, c] for c in cut] for t in vals[:3])
    dsk = [dsk[:, c] for c in cut]
    B = [vals[3][:, cut[i // 2]] for i in range(n)]
    Cm = [vals[4][:, cut[i // 2]] for i in range(n)]
    C = vals[0].shape[0]
    row, col = _iota((C, C), 0), _iota((C, C), 1)
    causal = row >= col
    ltri = causal.astype(f32)
    lane = _iota((C, 128), 1)
    last = _iota((C, 128), 0) == C - 1
    eye128 = _iota((128, 128), 0) == _iota((128, 128), 1)
    head = [jnp.logical_and(lane >= 64 * hh, lane < 64 * hh + 64) for hh in range(2)]
    pick = [(lane == 64 * hh).astype(f32) for hh in range(2)]
    xdt = _each(lambda a, b: a * b, x, dtb)
    cb = _each(lambda c, b: _dotb(c, b, _NT), Cm[::2], B[::2])
    cum = _each(lambda a: dots.dot01(ltri, a), dab)
    clast = _each(lambda a: jnp.sum(jnp.where(last, a, 0.0), axis=0, keepdims=True), cum)
    st = _each(lambda a, cl, cu, b: _dotb(a * jnp.exp(cl - cu), b, _TN), xdt, clast, cum, B)
    y = _each(lambda c, s, cu: _dotb(c, s, _NT) * jnp.exp(cu), Cm, S, cum)
    for hh in range(2):
        ccol = _each(lambda cu: jnp.sum(jnp.where(head[hh], cu, 0.0), axis=1, keepdims=True) * (1.0 / 64), cum)
        crow = _each(lambda cu: dots.dot01(pick[hh], cu, _NT), cum)
        lm = _each(lambda a, b: jnp.where(causal, jnp.exp(jnp.where(causal, a - b, 0.0)), 0.0), ccol, crow)
        y = [y[i] + _dotb(cb[i // 2] * lm[i], jnp.where(head[hh], xdt[i], 0.0)) for i in range(n)]
    cdcol = _each(lambda cl: jnp.sum(jnp.where(eye128, jnp.broadcast_to(jnp.exp(cl), (128, 128)), 0.0),
                                     axis=1, keepdims=True), clast)
    S_new = _each(lambda c, s, t: c * s + t, cdcol, S, st)
    out = _each(lambda a, d, b: a + d * b, y, dsk, x)
    return S_new, [jnp.concatenate(out, axis=1)]


def _m2_post(yc, z, ng):
    return _rms(yc * (z * jax.nn.sigmoid(z)), ng)


def _m2_fwd(h, g_norm, w_ext, conv_w, conv_b, dt_bias, a_log, d_skip, norm_g, w_out):
    L, D = h.shape
    tm = _row_tile(L)
    nc = L // CHUNK
    NI = M2_INNER
    hn = _rms_fwd("mix_norm", h, g_norm)
    proj = _mm_plain("m2_in", hn, w_ext, _NN)
    xbc = _conv_fwd("m2_conv", proj, NI // 128, conv_w, conv_b)
    dtb_row, alog_row, d_pad = _pad_row(dt_bias), _pad_row(a_log), _pad_row(d_skip)
    act_ins = [(proj, (tm, 128), lambda r, z: (r, 3 * NI // 128)), (dtb_row, *_const((1, 128))),
               (alog_row, *_const((1, 128)))]
    dtb, dab = _ew("m2_act", _m2_act, act_ins, [((L, NI), f32, *_rows(tm, NI), None)] * 2, (L // tm, 1))
    dsk = _ew("m2_dexp", _m2_dexp, [(d_pad, *_const((1, 128)))], [((1, NI), f32, *_const((1, NI)), None)], (1, 1))[0]
    GB = M2_GB
    x_blk, bc_blk = (CHUNK, 256 * GB), (CHUNK, 128 * GB)
    cins = [(dsk, (1, 256 * GB), lambda u, c: (0, u))]
    core_ins = [(xbc, x_blk, lambda u, c: (c, u), (L, NI), lambda u, c: (c, u)),
                (dtb, x_blk, lambda u, c: (c, u)), (dab, x_blk, lambda u, c: (c, u)),
                (xbc, bc_blk, lambda u, c: (c, 16 // GB + u), (L, D), lambda u, c: (c, u)),
                (xbc, bc_blk, lambda u, c: (c, 24 // GB + u), (L, D), lambda u, c: (c, u))]
    (yc,), saved_s = _scan_fwd("m2_core", _ssd_chunk, 2 * GB, (128, 128), cins, core_ins,
                               [((L, NI), x_blk, lambda u, c: (c, u))], 8 // GB, nc)
    tp = _head_rows(L)
    gblk = ((tp, 256), lambda g, r: (r, g))
    post_ins = [(yc, *gblk), (proj, *gblk), (norm_g, (1, 256), lambda g, r: (0, g))]
    yn = _ew("m2_post", _m2_post, post_ins, [((L, NI), f32, *gblk, None)], (8, L // tp))[0]
    h_out = _mm_plain("m2_out", yn, w_out, _NN, epi_fn=lambda acc, res: acc + res, aux=[(h, "e")])
    saved = dict(hn=hn, proj=proj, act_ins=act_ins, d_pad=d_pad, cins=cins, core_ins=core_ins, saved_s=saved_s,
                 post_ins=post_ins, yn=yn)
    return h_out, saved


def _m2_bwd(dh, h, g_norm, w_ext, conv_w, conv_b, w_out, sv):
    L, D = h.shape
    tm = _row_tile(L)
    nc = L // CHUNK
    NI = M2_INNER
    dyn = _mm_plain("m2_out_dx", dh, w_out, _NT)
    dw_out = _mm_plain("m2_out_dw", sv["yn"], dh, _TN)
    tp = _head_rows(L)
    gblk = ((tp, 256), lambda g, r: (r, g))
    d_yc, d_z, d_ng = _ew("m2_post_bwd", _vjp_fn(_m2_post, 3), sv["post_ins"] + [(dyn, *gblk)],
                          [((L, NI), f32, *gblk, None)] * 2 + [((1, NI), f32, (1, 256), lambda g, r: (0, g), "inner")],
                          (8, L // tp))
    (d_dsk,), (dx, d_dtb, d_dab, dB, dC) = _scan_bwd(
        "m2_core_bwd", _ssd_chunk, 2 * M2_GB, (128, 128), sv["cins"], sv["core_ins"], sv["saved_s"],
        [(d_yc, (CHUNK, 256 * M2_GB), lambda u, c: (c, u))], 8 // M2_GB, nc)
    row128 = ((1, 128), f32, *_const((1, 128)), "all")
    d_dt_raw, d_dtbias, d_alog = _ew(
        "m2_act_bwd", _vjp_fn(_m2_act, 3), sv["act_ins"] + [(d_dtb, *_rows(tm, NI)), (d_dab, *_rows(tm, NI))],
        [((L, 128), f32, *_rows(tm, 128), None), row128, row128], (L // tm, 1))
    d_d = _ew("m2_dexp_bwd", _vjp_fn(_m2_dexp, 1), [(sv["d_pad"], *_const((1, 128))), (d_dsk, *_const((1, NI)))],
              [((1, 128), f32, *_const((1, 128)), None)], (1, 1))[0]
    d_conv_out = jnp.concatenate([dx, dB, dC], axis=1)
    d_conv_in, d_conv_w, d_conv_b = _conv_bwd("m2_conv_bwd", sv["proj"], NI // 128, conv_w, conv_b, d_conv_out)
    d_proj = jnp.concatenate([d_z, d_conv_in, d_dt_raw], axis=1)
    dw_ext = _mm_plain("m2_in_dw", sv["hn"], d_proj, _TN)
    dhn = _mm_plain("m2_in_dx", d_proj, w_ext, _NT)
    dh_in, dg = _rms_bwd("mix_norm_bwd", h, g_norm, dhn, dh)
    grads = dict(norm=dg, w_ext=dw_ext, conv_w=d_conv_w, conv_b=d_conv_b, dt_bias=d_dtbias[0, :M2_HEADS],
                 a_log=d_alog[0, :M2_HEADS], d=d_d[0, :M2_HEADS], norm_g=d_ng, w_out=dw_out)
    return dh_in, grads


def _mesh_pos():
    return lax.axis_index("x"), lax.axis_index("y"), lax.axis_index("c")


def _flip(pos, p):
    x, y, c = pos
    return (1 - x if p & 4 else x, 1 - y if p & 2 else y, 1 - c if p & 1 else c)


def _index(pos):
    return 4 * pos[0] + 2 * pos[1] + pos[2]


def _comm_call(name, body, arrays, out_shape, n_sem):
    n = len(arrays)
    hbm = pl.BlockSpec(memory_space=pl.ANY)
    return pl.pallas_call(
        body, name=name, in_specs=[hbm] * n, out_specs=[hbm] * len(out_shape), out_shape=out_shape,
        scratch_shapes=[pltpu.SemaphoreType.DMA((n, n_sem)), pltpu.SemaphoreType.DMA((n, n_sem)),
                        pltpu.SemaphoreType.DMA((n, 4))],
    )(*arrays)


def _gather(name, arrays):
    n = len(arrays)

    def body(*refs):
        ins, outs = refs[:n], refs[n:2 * n]
        send_sems, recv_sems, loc_sems = refs[2 * n:]
        me = _mesh_pos()
        c = me[2]
        sib = _flip(me, 1)
        chips = [_flip(me, 4), _flip(me, 2), _flip(me, 6)]

        def copy(w, k, block, to, src=None):
            slab = outs[w].at[_index(block)]
            return pltpu.make_async_remote_copy(
                src_ref=slab if src is None else src, dst_ref=slab, send_sem=send_sems.at[w, k],
                recv_sem=recv_sems.at[w, k], device_id=to, device_id_type=MESH)

        local = [pltpu.make_async_copy(ins[w], outs[w].at[_index(me)], loc_sems.at[w, 0]) for w in range(n)]
        for cp in local:
            cp.start()
        first = [copy(w, 0, me, sib, src=ins[w]) for w in range(n)]
        first += [copy(w, 1 + j, me, chip, src=ins[w]) for j, chip in enumerate(chips) for w in range(n)]
        for cp in first:
            cp.start()
        passed = []
        for j, chip in enumerate(chips):
            for w in range(n):
                copy(w, 1 + j, chip, me).wait_recv()
                fwd = copy(w, 4 + j, chip, sib)
                fwd.start()
                passed.append(fwd)
        for w in range(n):
            copy(w, 0, sib, me).wait_recv()
        for j, chip in enumerate(chips):
            for w in range(n):
                copy(w, 4 + j, (chip[0], chip[1], 1 - c), me).wait_recv()
        for cp in first + passed:
            cp.wait_send()
        for cp in local:
            cp.wait()

    out_shape = [jax.ShapeDtypeStruct((N_DEV,) + a.shape, a.dtype) for a in arrays]
    return _comm_call(name, body, arrays, out_shape, N_DEV - 1)


_HBM = pl.BlockSpec(memory_space=pltpu.HBM)
_SEM = pl.BlockSpec(memory_space=pltpu.SEMAPHORE)
_SPLIT_COPIES = 4


def _split_targets(me):
    return [_flip(me, 1), _flip(me, 4), _flip(me, 2), _flip(me, 6)]


def _gather_start(name, arrays, lands):
    n = len(arrays)
    ns = n * _SPLIT_COPIES

    def body(*refs):
        ins, land = refs[:n], refs[n:2 * n]
        send_sems, recv_sems = refs[2 * n:2 * n + ns], refs[2 * n + ns:2 * n + 2 * ns]
        token = refs[4 * n + 2 * ns]
        me = _mesh_pos()
        for w in range(n):
            for k, to in enumerate(_split_targets(me)):
                pltpu.make_async_remote_copy(
                    src_ref=ins[w], dst_ref=land[w].at[_index(me)], send_sem=send_sems[w * _SPLIT_COPIES + k],
                    recv_sem=recv_sems[w * _SPLIT_COPIES + k], device_id=to, device_id_type=MESH).start()
        token[...] = jnp.zeros_like(token)

    sem = pltpu.SemaphoreType.DMA(())
    res = pl.pallas_call(
        body, name=name,
        out_shape=(*[sem] * (2 * ns), *[pltpu.HBM(a.shape, a.dtype) for a in arrays],
                   *[pltpu.HBM(a.shape, a.dtype) for a in lands], jax.ShapeDtypeStruct((8, 128), f32)),
        in_specs=[_HBM] * (2 * n),
        out_specs=(*[_SEM] * (2 * ns), *[_HBM] * (2 * n), pl.BlockSpec(memory_space=pltpu.VMEM)),
        input_output_aliases={i: 2 * ns + i for i in range(2 * n)},
        compiler_params=pltpu.CompilerParams(has_side_effects=pltpu.SideEffectType.DATAFLOW_SIDE_EFFECTING),
    )(*[pltpu.with_memory_space_constraint(a, pltpu.HBM) for a in list(arrays) + list(lands)])
    sems, rest = res[:2 * ns], res[2 * ns:]
    return sems[:ns], sems[ns:], rest[:n], rest[n:2 * n], rest[2 * n]


def _gather_wait(name, arrays, lands, send_sems, recv_sems, after):
    n = len(arrays)
    ns = n * _SPLIT_COPIES

    def body(*refs):
        ins, land = refs[:n], refs[n:2 * n]
        s_sems, r_sems = refs[2 * n:2 * n + ns], refs[2 * n + ns:2 * n + 2 * ns]
        me = _mesh_pos()
        for w in range(n):
            for k, peer in enumerate(_split_targets(me)):
                cp = pltpu.make_async_remote_copy(
                    src_ref=ins[w], dst_ref=land[w].at[_index(peer)], send_sem=s_sems[w * _SPLIT_COPIES + k],
                    recv_sem=r_sems[w * _SPLIT_COPIES + k], device_id=peer, device_id_type=MESH)
                cp.wait_send()
                cp.wait_recv()

    res = pl.pallas_call(
        body, name=name,
        out_shape=(*[pltpu.HBM(a.shape, a.dtype) for a in arrays], *[pltpu.HBM(a.shape, a.dtype) for a in lands]),
        in_specs=[_HBM] * (2 * n) + [_SEM] * (2 * ns) + [pl.BlockSpec(memory_space=pl.ANY)],
        out_specs=tuple([_HBM] * (2 * n)), input_output_aliases={i: i for i in range(2 * n)},
        compiler_params=pltpu.CompilerParams(has_side_effects=pltpu.SideEffectType.DATAFLOW_SIDE_EFFECTING),
    )(*arrays, *lands, *send_sems, *recv_sems, after)
    return res[n:]


def _gather_forward(name, lands):
    n = len(lands)

    def body(*refs):
        outs = refs[n:2 * n]
        send_sems, recv_sems, _ = refs[2 * n:]
        me = _mesh_pos()
        sib = _flip(me, 1)
        chips = [_flip(me, 4), _flip(me, 2), _flip(me, 6)]

        def copy(w, j, block):
            slab = outs[w].at[_index(block)]
            return pltpu.make_async_remote_copy(src_ref=slab, dst_ref=slab, send_sem=send_sems.at[w, j],
                                                recv_sem=recv_sems.at[w, j], device_id=sib, device_id_type=MESH)

        sends = [copy(w, j, chip) for j, chip in enumerate(chips) for w in range(n)]
        for cp in sends:
            cp.start()
        for j, chip in enumerate(chips):
            for w in range(n):
                copy(w, j, (chip[0], chip[1], 1 - me[2])).wait_recv()
        for cp in sends:
            cp.wait_send()

    hbm = pl.BlockSpec(memory_space=pl.ANY)
    return pl.pallas_call(
        body, name=name, in_specs=[hbm] * n, out_specs=[hbm] * n,
        out_shape=[jax.ShapeDtypeStruct(a.shape, a.dtype) for a in lands],
        input_output_aliases={i: i for i in range(n)},
        scratch_shapes=[pltpu.SemaphoreType.DMA((n, 3)), pltpu.SemaphoreType.DMA((n, 3)), pltpu.SemaphoreType.DMA((n, 4))],
    )(*lands)


def _scatter_pair(name, arrays):
    n = len(arrays)

    def body(*refs):
        ins, outs = refs[:n], refs[n:2 * n]
        send_sems, recv_sems, _ = refs[2 * n:]
        me = _mesh_pos()
        c = me[2]
        sib = _flip(me, 1)

        def copy(w, q):
            return pltpu.make_async_remote_copy(
                src_ref=ins[w].at[2 * q + 1 - c], dst_ref=outs[w].at[q], send_sem=send_sems.at[w, q],
                recv_sem=recv_sems.at[w, q], device_id=sib, device_id_type=MESH)

        cps = [copy(w, q) for q in range(4) for w in range(n)]
        for cp in cps:
            cp.start()
        for cp in cps:
            cp.wait()

    out_shape = [jax.ShapeDtypeStruct((4,) + a.shape[1:], a.dtype) for a in arrays]
    return _comm_call(name, body, arrays, out_shape, 4)


def _pair_add(name, full, theirs, core, dtype):
    _, R, C = theirs.shape
    tr = R if R <= 256 else (256 if C <= 512 else 128)

    def body(core_ref, mine_ref, theirs_ref, o_ref):
        o_ref[...] = (mine_ref[...] + theirs_ref[...]).astype(o_ref.dtype)

    blk = pl.BlockSpec((4, tr, C), lambda r, cr: (0, r, 0))
    grid_spec = pltpu.PrefetchScalarGridSpec(
        num_scalar_prefetch=1, grid=(R // tr,),
        in_specs=[pl.BlockSpec((4, None, tr, C), lambda r, cr: (0, cr[0], r, 0)), blk], out_specs=blk)
    return pl.pallas_call(
        body, name=name, grid_spec=grid_spec, out_shape=jax.ShapeDtypeStruct((4, R, C), dtype),
        compiler_params=_params(1),
    )(core.reshape(1).astype(jnp.int32), full.reshape(4, 2, R, C), theirs)


def _scatter_chips(name, arrays):
    n = len(arrays)

    def body(*refs):
        ins, outs = refs[:n], refs[n:2 * n]
        send_sems, recv_sems, loc_sems = refs[2 * n:]
        me = _mesh_pos()
        mq = 2 * me[0] + me[1]
        peers = [_flip(me, 4), _flip(me, 2), _flip(me, 6)]

        def copy(w, k):
            peer = peers[k]
            return pltpu.make_async_remote_copy(
                src_ref=ins[w].at[2 * peer[0] + peer[1]], dst_ref=outs[w].at[mq], send_sem=send_sems.at[w, k],
                recv_sem=recv_sems.at[w, k], device_id=peer, device_id_type=MESH)

        def arrival(w, k):
            peer = peers[k]
            return pltpu.make_async_remote_copy(
                src_ref=ins[w].at[mq], dst_ref=outs[w].at[2 * peer[0] + peer[1]], send_sem=send_sems.at[w, k],
                recv_sem=recv_sems.at[w, k], device_id=peer, device_id_type=MESH)

        local = [pltpu.make_async_copy(ins[w].at[mq], outs[w].at[mq], loc_sems.at[w, 0]) for w in range(n)]
        for cp in local:
            cp.start()
        sends = [copy(w, k) for k in range(3) for w in range(n)]
        for cp in sends:
            cp.start()
        for k in range(3):
            for w in range(n):
                arrival(w, k).wait_recv()
        for cp in sends:
            cp.wait_send()
        for cp in local:
            cp.wait()

    out_shape = [jax.ShapeDtypeStruct(a.shape, a.dtype) for a in arrays]
    return _comm_call(name, body, arrays, out_shape, 3)


def _adamw(name, parts, w, m, v):
    R, C = w.shape
    n_parts = parts.shape[0]
    tr = R if R <= 256 else (256 if C <= 512 else 128)
    bc1 = 1.0 - ADAM_B1 ** ADAM_STEP
    bc2 = 1.0 - ADAM_B2 ** ADAM_STEP

    def f(p, wv, mv, vv):
        g = p[0].astype(f32)
        for i in range(1, n_parts):
            g = g + p[i].astype(f32)
        m2 = ADAM_B1 * mv + (1.0 - ADAM_B1) * g
        v2 = ADAM_B2 * vv + (1.0 - ADAM_B2) * jnp.square(g)
        delta = -ADAM_LR * ((m2 / bc1) / (jnp.sqrt(v2 / bc2) + ADAM_EPS) + ADAM_WD * wv)
        return g, delta, m2, v2

    blk = ((tr, C), lambda r, z: (r, 0))
    return _ew(name, f, [(parts, (n_parts, tr, C), lambda r, z: (0, r, 0)), (w, *blk), (m, *blk), (v, *blk)],
               [((R, C), f32, *blk, None)] * 4, (R // tr, 1))


_WEIGHTS = ["norm_mix_g", "norm_mlp_g", "mlp_w1", "mlp_w2", "gdn_w_in", "gdn_conv_w", "gdn_a_log", "gdn_dt_bias",
            "gdn_o_norm_g", "gdn_w_out", "s5_w_in", "s5_lam_re", "s5_lam_im", "s5_log_dt", "s5_b_re", "s5_b_im",
            "s5_c_re", "s5_c_im", "s5_d", "s5_w_out", "m2_w_in", "m2_conv_w", "m2_conv_b", "m2_dt_bias", "m2_a_log",
            "m2_d", "m2_norm_g", "m2_w_out", "final_norm_g"]
_SHARDED = ["mlp_w1", "mlp_w2", "gdn_w_in", "gdn_w_out", "s5_w_in", "s5_w_out", "m2_w_in", "m2_w_out",
            "gdn_conv_w", "m2_conv_w", "m2_conv_b", "m2_norm_g"]
_MATRICES = _SHARDED[:8]
_REPLICATED = [n for n in _WEIGHTS if n not in _SHARDED]
_GDN_IN, _M2_IN = 4112, 6176
_LAYER_KIND = (0, 1, 2, 0)


def _as2d(a):
    return a.reshape(-1, a.shape[-1])


def _cols_from_shards(g, width):
    return g.transpose(1, 0, 2).reshape(g.shape[1], width)


def _cols_to_shards(a, width):
    return a[:, :width].reshape(a.shape[0], N_DEV, width // N_DEV).transpose(1, 0, 2)


def kernel(x, norm_mix_g, norm_mlp_g, mlp_w1, mlp_w2, gdn_w_in, gdn_conv_w, gdn_a_log, gdn_dt_bias, gdn_o_norm_g, gdn_w_out, s5_w_in, s5_lam_re, s5_lam_im, s5_log_dt, s5_b_re, s5_b_im, s5_c_re, s5_c_im, s5_d, s5_w_out, m2_w_in, m2_conv_w, m2_conv_b, m2_dt_bias, m2_a_log, m2_d, m2_norm_g, m2_w_out, final_norm_g, loss_target, m_norm_mix_g, m_norm_mlp_g, m_mlp_w1, m_mlp_w2, m_gdn_w_in, m_gdn_conv_w, m_gdn_a_log, m_gdn_dt_bias, m_gdn_o_norm_g, m_gdn_w_out, m_s5_w_in, m_s5_lam_re, m_s5_lam_im, m_s5_log_dt, m_s5_b_re, m_s5_b_im, m_s5_c_re, m_s5_c_im, m_s5_d, m_s5_w_out, m_m2_w_in, m_m2_conv_w, m_m2_conv_b, m_m2_dt_bias, m_m2_a_log, m_m2_d, m_m2_norm_g, m_m2_w_out, m_final_norm_g, v_norm_mix_g, v_norm_mlp_g, v_mlp_w1, v_mlp_w2, v_gdn_w_in, v_gdn_conv_w, v_gdn_a_log, v_gdn_dt_bias, v_gdn_o_norm_g, v_gdn_w_out, v_s5_w_in, v_s5_lam_re, v_s5_lam_im, v_s5_log_dt, v_s5_b_re, v_s5_b_im, v_s5_c_re, v_s5_c_im, v_s5_d, v_s5_w_out, v_m2_w_in, v_m2_conv_w, v_m2_conv_b, v_m2_dt_bias, v_m2_a_log, v_m2_d, v_m2_norm_g, v_m2_w_out, v_final_norm_g):
    args = locals()
    W = {n: args[n] for n in _WEIGHTS}
    MOM = {n: args["m_" + n] for n in _WEIGHTS}
    VAR = {n: args["v_" + n] for n in _WEIGHTS}
    h = x[0]
    target = loss_target[0]
    L, D = h.shape

    first = [mlp_w1[0:1].astype(bf16), mlp_w2[0:1].astype(bf16), gdn_w_in[0:1].astype(bf16),
             gdn_w_out[0:1].astype(bf16), _as2d(gdn_conv_w), _as2d(m2_conv_w), _as2d(m2_conv_b), _as2d(m2_norm_g)]
    w1g0, w2g0, gin0, gout0, gconv, m2_cw, m2_cbg, m2_ngg = _gather("gather_first", first)
    packed = [gdn_w_in[1:2], gdn_w_out[1:2], s5_w_in, s5_w_out, m2_w_in, m2_w_out]
    pack = jnp.concatenate([a.astype(bf16).reshape(-1) for a in packed]).reshape(-1, 128)
    rest = [mlp_w1[1:4].astype(bf16), mlp_w2[1:4].astype(bf16), pack]
    me = 4 * lax.axis_index("x") + 2 * lax.axis_index("y") + lax.axis_index("c")
    lands = [lax.dynamic_update_slice(lax.empty((N_DEV,) + a.shape, a.dtype), a[None], (me,) + (0,) * a.ndim)
             for a in rest]
    send_sems, recv_sems, rest_thru, lands_thru, token = _gather_start("gather_rest_start", rest, lands)

    def gdn_weights(gin, gout, conv, j):
        return (jnp.pad(_cols_from_shards(gin[:, 0], _GDN_IN), ((0, 0), (0, GDN_EXT - _GDN_IN))),
                gout[:, 0].reshape(D, D), _cols_from_shards(conv[:, 4 * j:4 * j + 4], 3 * D))

    gdn_in, gdn_out, gdn_conv = [None, None], [None, None], [None, None]
    gdn_in[0], gdn_out[0], gdn_conv[0] = gdn_weights(gin0, gout0, gconv, 0)

    norm_mix = [norm_mix_g[i].reshape(1, D) for i in range(4)]
    norm_mix[0] = norm_mix[0] + token[0, 0]
    late = {}

    def mixer_fwd(i, hv):
        kind, j = _LAYER_KIND[i], i // 3
        gn = norm_mix[i]
        s5_in, s5_out_g = late.get("s5_in"), late.get("s5_out_g")
        m2_in, m2_conv, m2_cb, m2_ng, m2_out = (late.get(k) for k in ("m2_in", "m2_conv", "m2_cb", "m2_ng", "m2_out"))
        if kind == 0:
            return _gdn_fwd(hv, gn, gdn_in[j], gdn_conv[j], gdn_a_log[j], gdn_dt_bias[j], gdn_o_norm_g[j], gdn_out[j])
        if kind == 1:
            return _s5_fwd(hv, gn, s5_in, s5_lam_re[0], s5_lam_im[0], s5_log_dt[0], s5_b_re[0], s5_b_im[0],
                           s5_c_re[0], s5_c_im[0], s5_d[0], s5_out_g)
        return _m2_fwd(hv, gn, m2_in, m2_conv, m2_cb, m2_dt_bias[0], m2_a_log[0], m2_d[0], m2_ng, m2_out)

    def mixer_bwd(i, dh, hv, sv):
        kind, j = _LAYER_KIND[i], i // 3
        gn = norm_mix[i]
        s5_in, s5_out_g = late["s5_in"], late["s5_out_g"]
        m2_in, m2_conv, m2_cb, m2_out = (late[k] for k in ("m2_in", "m2_conv", "m2_cb", "m2_out"))
        if kind == 0:
            return _gdn_bwd(dh, hv, gn, gdn_in[j], gdn_conv[j], gdn_out[j], sv)
        if kind == 1:
            return _s5_bwd(dh, hv, gn, s5_in, s5_out_g, sv)
        return _m2_bwd(dh, hv, gn, m2_in, m2_conv, m2_cb, m2_out, sv)

    tape = []
    mlp_w = [(w1g0, w2g0, 0)]
    for i in range(4):
        if i == 1:
            landed = _gather_wait("gather_rest_wait", rest_thru, lands_thru, send_sems, recv_sems, h)
            w1gr, w2gr, pack_g = _gather_forward("gather_rest_forward", landed)
            flat, off, pieces = pack_g.reshape(N_DEV, -1), 0, []
            for a in packed:
                pieces.append(flat[:, off:off + a.size].reshape((N_DEV,) + a.shape))
                off += a.size
            gin1, gout1, s5_in_g, s5_out_g, m2_in_g, m2_out_g = pieces
            mlp_w += [(w1gr, w2gr, k) for k in range(3)]
            gdn_in[1], gdn_out[1], gdn_conv[1] = gdn_weights(gin1, gout1, gconv, 1)
            late.update(
                s5_in=s5_in_g.reshape(D, D), s5_out_g=s5_out_g,
                m2_in=jnp.pad(_cols_from_shards(m2_in_g[:, 0], _M2_IN), ((0, 0), (0, M2_EXT - _M2_IN))),
                m2_out=m2_out_g.reshape(M2_INNER, D), m2_conv=_cols_from_shards(m2_cw, 2 * M2_INNER),
                m2_cb=_cols_from_shards(m2_cbg, 2 * M2_INNER), m2_ng=_cols_from_shards(m2_ngg, M2_INNER))
        h_mid, sv = mixer_fwd(i, h)
        h_next, hn, h1 = _mlp_fwd(h_mid, norm_mlp_g[i].reshape(1, D), *mlp_w[i])
        tape.append((h, sv, h_mid, hn, h1))
        h = h_next
    loss_row, dh, d_final = _loss_head(h, final_norm_g.reshape(1, D), target)
    loss = lax.psum(loss_row[0, 0], ("x", "y", "c"))

    dw1 = lax.empty((N_DEV, 4, D, D_FF // N_DEV), f32)
    dw2 = lax.empty((N_DEV, 4, D_FF // N_DEV, D), f32)
    d_mix, d_mlp, mg = [None] * 4, [None] * 4, [None] * 4
    for i in reversed(range(4)):
        h_in, sv, h_mid, hn, h1 = tape[i]
        dh, d_mlp[i], dw1, dw2 = _mlp_bwd(dh, h_mid, norm_mlp_g[i].reshape(1, D), hn, h1, *mlp_w[i], i, dw1, dw2)
        dh, mg[i] = mixer_bwd(i, dh, h_in, sv)
        d_mix[i] = mg[i]["norm"]
    grad_x = dh.reshape(1, L, D)
    ga, gb_, s5g, m2g = mg[0], mg[3], mg[1], mg[2]

    full = {
        "mlp_w1": dw1, "mlp_w2": dw2,
        "gdn_w_in": jnp.stack([_cols_to_shards(g["w_ext"], _GDN_IN) for g in (ga, gb_)], axis=1),
        "gdn_w_out": jnp.stack([g["w_out"].reshape(N_DEV, D // N_DEV, D) for g in (ga, gb_)], axis=1),
        "s5_w_in": s5g["w_in"].reshape(N_DEV, 1, D // N_DEV, D), "s5_w_out": s5g["w_out"],
        "m2_w_in": _cols_to_shards(m2g["w_ext"], _M2_IN)[:, None],
        "m2_w_out": m2g["w_out"].reshape(N_DEV, 1, M2_INNER // N_DEV, D),
        "gdn_conv_w": jnp.concatenate([_cols_to_shards(g["conv_w"], 3 * D) for g in (ga, gb_)], axis=1),
        "m2_conv_w": _cols_to_shards(m2g["conv_w"], 2 * M2_INNER),
        "m2_conv_b": _cols_to_shards(m2g["conv_b"], 2 * M2_INNER),
        "m2_norm_g": _cols_to_shards(m2g["norm_g"], M2_INNER),
    }
    sends = [full[n].reshape((N_DEV,) + _as2d(W[n]).shape) for n in _SHARDED]
    core = lax.axis_index("c")
    theirs = _scatter_pair("scatter_pair", sends)
    chip_sums = [_pair_add("pair_add_" + n, full8, th, core, bf16 if n in _MATRICES else f32)
                 for n, full8, th in zip(_SHARDED, sends, theirs)]
    parts = dict(zip(_SHARDED, _scatter_chips("scatter_chips", chip_sums)))

    rep = {
        "norm_mix_g": jnp.concatenate(d_mix, axis=0), "norm_mlp_g": jnp.concatenate(d_mlp, axis=0),
        "gdn_a_log": jnp.stack([ga["a_log"], gb_["a_log"]]), "gdn_dt_bias": jnp.stack([ga["dt_bias"], gb_["dt_bias"]]),
        "gdn_o_norm_g": jnp.stack([ga["o_norm_g"], gb_["o_norm_g"]]),
        "s5_lam_re": s5g["lam_re"], "s5_lam_im": s5g["lam_im"], "s5_log_dt": s5g["log_dt"], "s5_b_re": s5g["b_re"],
        "s5_b_im": s5g["b_im"], "s5_c_re": s5g["c_re"], "s5_c_im": s5g["c_im"], "s5_d": s5g["d"],
        "m2_dt_bias": m2g["dt_bias"], "m2_a_log": m2g["a_log"], "m2_d": m2g["d"], "final_norm_g": d_final,
    }

    def pack(d):
        flat = jnp.concatenate([d[n].reshape(-1).astype(f32) for n in _REPLICATED])
        return jnp.pad(flat, (0, -flat.shape[0] % (256 * 128))).reshape(-1, 128)

    (rep_parts,) = _gather("gather_small_grads", [pack(rep)])

    res = {}
    for n in _SHARDED:
        w2d = _as2d(W[n])
        out = _adamw("adamw_" + n, parts[n], w2d, _as2d(MOM[n]), _as2d(VAR[n]))
        res[n] = [o.reshape(W[n].shape) for o in out]
    out = _adamw("adamw_replicated", rep_parts, pack(W), pack(MOM), pack(VAR))
    off = 0
    for n in _REPLICATED:
        size = W[n].size
        res[n] = [o.reshape(-1)[off:off + size].reshape(W[n].shape) for o in out]
        off += size

    return (loss, grad_x, *[res[n][0] for n in _WEIGHTS], *[res[n][1] for n in _WEIGHTS],
            *[res[n][2] for n in _WEIGHTS], *[res[n][3] for n in _WEIGHTS])
```

```python
import functools

import jax
import jax.numpy as jnp
from jax import lax
from jax.experimental import pallas as pl
from jax.experimental.pallas import tpu as pltpu

f32 = jnp.float32
bf16 = jnp.bfloat16
HI = lax.Precision.HIGHEST
MESH = pl.DeviceIdType.MESH

N_DEV = 8
D_MODEL = 1024
D_FF = 4096
CHUNK = 64
RMS_EPS = 1e-6
GDN_HEADS = 8
GDN_HB = 8
GDN_EXT = 4224
S5_STATE = 64
S5_SCAN_LANES = 256
M2_INNER = 2048
M2_EXT = 6272
M2_HEADS = 32
M2_GB = 4
VMEM_LIMIT_BYTES = 56 * 1024 * 1024

ADAM_LR, ADAM_B1, ADAM_B2, ADAM_EPS, ADAM_WD, ADAM_STEP = 0.001, 0.9, 0.999, 1e-08, 0.01, 10

_NN = ((1,), (0,))
_NT = ((1,), (1,))
_TN = ((0,), (0,))


def _dot(a, b, dims=_NN):
    return lax.dot_general(a, b, (dims, ((), ())), precision=HI, preferred_element_type=f32)


def _dotb(a, b, dims=_NN):
    return lax.dot_general(a.astype(bf16), b.astype(bf16), (dims, ((), ())), preferred_element_type=f32)


def _bdot(p, q, dims):
    return lax.dot_general(p, q, (dims, ((), ())), preferred_element_type=f32)


def _pieces(x, n):
    out = []
    for _ in range(n - 1):
        p = x.astype(bf16)
        out.append(p)
        x = x - p.astype(f32)
    return out + [x.astype(bf16)]


def _dot01_raw(mask, b, dims=_NN, mask_first=True):
    m = mask.astype(bf16)
    p = _pieces(b, 3)
    if mask_first:
        return _bdot(m, p[0], dims) + (_bdot(m, p[1], dims) + _bdot(m, p[2], dims))
    return _bdot(p[0], m, dims) + (_bdot(p[1], m, dims) + _bdot(p[2], m, dims))


@jax.custom_vjp
def _dot01_nn(mask, b):
    return _dot01_raw(mask, b, _NN)


@jax.custom_vjp
def _dot01_nt(mask, b):
    return _dot01_raw(mask, b, _NT)


_dot01_nn.defvjp(lambda m, b: (_dot01_raw(m, b, _NN), m),
                 lambda m, ct: (jnp.zeros_like(m), _dot01_raw(m, ct, _TN, mask_first=True)))
_dot01_nt.defvjp(lambda m, b: (_dot01_raw(m, b, _NT), m),
                 lambda m, ct: (jnp.zeros_like(m), _dot01_raw(m, ct, _TN, mask_first=False)))


def _dot01_vjp(mask, b, dims=_NN):
    return _dot01_nn(mask, b) if dims == _NN else _dot01_nt(mask, b)


def _dot3_raw(a, b, dims=_NN):
    (ah, al), (bh, bl) = _pieces(a, 2), _pieces(b, 2)
    return _bdot(ah, bh, dims) + (_bdot(ah, bl, dims) + _bdot(al, bh, dims))


@jax.custom_vjp
def _dot3_vjp(a, b):
    return _dot3_raw(a, b)


_dot3_vjp.defvjp(lambda a, b: (_dot3_raw(a, b), (a, b)),
                 lambda res, ct: (_dot3_raw(ct, res[1], _NT), _dot3_raw(res[0], ct, _TN)))


class _Dots:
    def __init__(self, dot3, dot01):
        self.dot3, self.dot01 = dot3, dot01


_PLAIN_DOTS = _Dots(_dot3_raw, _dot01_raw)
_VJP_DOTS = _Dots(_dot3_vjp, _dot01_vjp)


def _iota(shape, dim):
    return lax.broadcasted_iota(jnp.int32, shape, dim)


def _params(n_grid):
    return pltpu.CompilerParams(dimension_semantics=("arbitrary",) * n_grid, vmem_limit_bytes=VMEM_LIMIT_BYTES)


def _row_tile(n_rows):
    return min(512, n_rows)


def _head_rows(n_rows):
    return min(2048, n_rows)


def _mm_rows(n_rows):
    return min(1024, n_rows)


def _col_tile(n, cap=1024):
    best = 128
    for t in range(128, cap + 1, 128):
        if n % t == 0:
            best = t
    return best


def _mm(name, a, b, *, dims, grid, a_spec, b_spec, out_shape, out_spec, aux=(), a_fn=None, epi_fn=None,
        acc_shape, out_init=None):
    nk = grid[2]
    n_aux = len(aux)
    kinds = [x[2] for x in aux]

    def body(*refs):
        a_ref, b_ref = refs[0], refs[1]
        aux_refs = refs[2:2 + n_aux]
        pos = 2 + n_aux + (1 if out_init is not None else 0)
        o_ref, acc_ref = refs[pos], refs[pos + 1]
        k = pl.program_id(2)

        @pl.when(k == 0)
        def _():
            acc_ref[...] = jnp.zeros_like(acc_ref)

        av = a_ref[...]
        if a_fn is not None:
            av = a_fn(av, *[r[...] for r, kd in zip(aux_refs, kinds) if kd == "a"])
        acc_ref[...] += lax.dot_general(av.astype(bf16), b_ref[...].astype(bf16), (dims, ((), ())),
                                        preferred_element_type=f32)

        @pl.when(k == nk - 1)
        def _():
            r = acc_ref[...]
            if epi_fn is not None:
                r = epi_fn(r, *[x[...] for x, kd in zip(aux_refs, kinds) if kd == "e"])
            o_ref[...] = r.astype(o_ref.dtype)

    in_specs = [a_spec, b_spec] + [x[1] for x in aux]
    args = [a, b] + [x[0] for x in aux]
    aliases = {}
    if out_init is not None:
        in_specs.append(pl.BlockSpec(memory_space=pl.ANY))
        args.append(out_init)
        aliases = {len(args) - 1: 0}
    return pl.pallas_call(
        body, name=name, grid=grid, in_specs=in_specs, out_specs=out_spec, out_shape=out_shape,
        scratch_shapes=[pltpu.VMEM(acc_shape, f32)], input_output_aliases=aliases,
        compiler_params=_params(3),
    )(*args)


def _ew(name, f, ins, outs, grid):
    n_in = len(ins)
    modes = [o[4] for o in outs]

    def body(*refs):
        vals = [r[...] for r in refs[:n_in]]
        res = f(*vals)
        if not isinstance(res, (tuple, list)):
            res = (res,)
        for r, o_ref, mode in zip(res, refs[n_in:], modes):
            if mode is None:
                o_ref[...] = r.astype(o_ref.dtype)
                continue
            first = pl.program_id(1) == 0
            if mode == "all":
                first = jnp.logical_and(first, pl.program_id(0) == 0)

            @pl.when(first)
            def _(r=r, o_ref=o_ref):
                o_ref[...] = r.astype(o_ref.dtype)

            @pl.when(jnp.logical_not(first))
            def _(r=r, o_ref=o_ref):
                o_ref[...] += r.astype(o_ref.dtype)

    res = pl.pallas_call(
        body, name=name, grid=grid,
        in_specs=[pl.BlockSpec(blk, im) for _, blk, im in ins],
        out_specs=[pl.BlockSpec(o[2], o[3]) for o in outs],
        out_shape=[jax.ShapeDtypeStruct(o[0], o[1]) for o in outs],
        compiler_params=_params(2),
    )(*[a for a, _, _ in ins])
    return res


def _vjp_fn(f, n_primal):
    def g(*args):
        _, vjp = jax.vjp(f, *args[:n_primal])
        cts = args[n_primal:]
        return vjp(cts[0] if len(cts) == 1 else tuple(cts))
    return g


def _scan_fwd(name, step, n_state, state_shape, cins, ins, outs, n_units, n_chunks):
    n_c, n_in, n_out = len(cins), len(ins), len(outs)

    def body(*refs):
        c_refs = refs[:n_c]
        in_refs = refs[n_c:n_c + n_in]
        out_refs = refs[n_c + n_in:n_c + n_in + n_out]
        saved = refs[n_c + n_in + n_out:n_c + n_in + n_out + n_state]
        st = refs[n_c + n_in + n_out + n_state:]

        @pl.when(pl.program_id(1) == 0)
        def _():
            for s in st:
                s[...] = jnp.zeros_like(s)

        cur = [s[...] for s in st]
        for sv, s in zip(saved, cur):
            sv[...] = s
        new, res = step(cur, [r[...] for r in c_refs], [r[...] for r in in_refs], _PLAIN_DOTS)
        for s, n in zip(st, new):
            s[...] = n
        for o, r in zip(out_refs, res):
            o[...] = r

    sshape = (n_units, n_chunks) + state_shape
    sblock = (None, None) + state_shape
    nz = len(state_shape)
    res = pl.pallas_call(
        body, name=name, grid=(n_units, n_chunks),
        in_specs=[pl.BlockSpec(e[1], e[2]) for e in cins + ins],
        out_specs=[pl.BlockSpec(o[1], o[2]) for o in outs]
        + [pl.BlockSpec(sblock, lambda u, c: (u, c) + (0,) * nz)] * n_state,
        out_shape=[jax.ShapeDtypeStruct(o[0], f32) for o in outs]
        + [jax.ShapeDtypeStruct(sshape, f32)] * n_state,
        scratch_shapes=[pltpu.VMEM(state_shape, f32)] * n_state,
        compiler_params=_params(2),
    )(*[e[0] for e in cins + ins])
    return res[:n_out], res[n_out:]


def _scan_bwd(name, step, n_state, state_shape, cins, ins, saved, douts, n_units, n_chunks):
    n_c, n_in, n_do = len(cins), len(ins), len(douts)

    def flip(im):
        return lambda u, c: im(u, n_chunks - 1 - c)

    def body(*refs):
        p = 0
        c_refs = refs[p:p + n_c]; p += n_c
        in_refs = refs[p:p + n_in]; p += n_in
        sv_refs = refs[p:p + n_state]; p += n_state
        do_refs = refs[p:p + n_do]; p += n_do
        dc_refs = refs[p:p + n_c]; p += n_c
        di_refs = refs[p:p + n_in]; p += n_in
        dst = refs[p:]
        first = pl.program_id(1) == 0

        @pl.when(first)
        def _():
            for s in dst:
                s[...] = jnp.zeros_like(s)

        def fn(states, consts, vals):
            new, res = step(states, consts, vals, _VJP_DOTS)
            return tuple(new), tuple(res)

        prim = ([r[...] for r in sv_refs], [r[...] for r in c_refs], [r[...] for r in in_refs])
        _, vjp = jax.vjp(fn, *prim)
        d_states, d_consts, d_vals = vjp((tuple(s[...] for s in dst), tuple(r[...] for r in do_refs)))
        for s, g in zip(dst, d_states):
            s[...] = g
        for o, g in zip(di_refs, d_vals):
            o[...] = g
        for o, g in zip(dc_refs, d_consts):
            @pl.when(first)
            def _(o=o, g=g):
                o[...] = g

            @pl.when(jnp.logical_not(first))
            def _(o=o, g=g):
                o[...] += g

    nz = len(state_shape)
    sblock = (None, None) + state_shape
    def gshape(e):
        return e[3] if len(e) == 5 else e[0].shape

    def gmap(e):
        return e[4] if len(e) == 5 else e[2]

    in_specs = ([pl.BlockSpec(e[1], e[2]) for e in cins]
                + [pl.BlockSpec(e[1], flip(e[2])) for e in ins]
                + [pl.BlockSpec(sblock, lambda u, c: (u, n_chunks - 1 - c) + (0,) * nz)] * n_state
                + [pl.BlockSpec(e[1], flip(e[2])) for e in douts])
    out_specs = ([pl.BlockSpec(e[1], e[2]) for e in cins]
                 + [pl.BlockSpec(e[1], flip(gmap(e))) for e in ins])
    out_shape = [jax.ShapeDtypeStruct(gshape(e), f32) for e in cins + ins]
    res = pl.pallas_call(
        body, name=name, grid=(n_units, n_chunks), in_specs=in_specs, out_specs=out_specs, out_shape=out_shape,
        scratch_shapes=[pltpu.VMEM(state_shape, f32)] * n_state,
        compiler_params=_params(2),
    )(*([e[0] for e in cins + ins] + list(saved) + [e[0] for e in douts]))
    return res[:n_c], res[n_c:]


def _rms(x, g):
    return x * lax.rsqrt(jnp.mean(x * x, axis=-1, keepdims=True) + RMS_EPS) * g


def _rows(tm, width):
    return (tm, width), lambda r, z: (r, 0)


def _const(shape):
    return shape, lambda r, z: (0,) * len(shape)


def _rms_fwd(name, h, g):
    L, D = h.shape
    tm = _row_tile(L)
    return _ew(name, _rms, [(h, *_rows(tm, D)), (g, *_const((1, D)))],
               [((L, D), f32, *_rows(tm, D), None)], (L // tm, 1))[0]


def _rms_bwd(name, h, g, d_hn, d_res):
    L, D = h.shape
    tm = _row_tile(L)

    def f(hv, gv, dv, rv):
        dh, dg = _vjp_fn(_rms, 2)(hv, gv, dv)
        return dh + rv, dg

    return _ew(name, f, [(h, *_rows(tm, D)), (g, *_const((1, D))), (d_hn, *_rows(tm, D)), (d_res, *_rows(tm, D))],
               [((L, D), f32, *_rows(tm, D), None), ((1, D), f32, *_const((1, D)), "all")], (L // tm, 1))


def _loss_head(h, g, target):
    L, D = h.shape
    tm = _row_tile(L)

    def f(hv, gv, tv):
        def lf(a, b):
            e = jnp.square(_rms(a, b) - tv)
            return (0.5 / D) * jnp.sum(jnp.sum(e, axis=1, keepdims=True), axis=0, keepdims=True)

        val, vjp = jax.vjp(lf, hv, gv)
        dh, dg = vjp(jnp.ones((1, 1), f32))
        return jnp.broadcast_to(val, (1, 128)), dh, dg

    return _ew("loss_head", f, [(h, *_rows(tm, D)), (g, *_const((1, D))), (target, *_rows(tm, D))],
               [((1, 128), f32, *_const((1, 128)), "all"), ((L, D), f32, *_rows(tm, D), None),
                ((1, D), f32, *_const((1, D)), "all")], (L // tm, 1))


def _sqrelu(x):
    return jnp.square(jnp.maximum(x, 0.0))


def _mm_plain(name, a, b, dims, *, a_fn=None, epi_fn=None, aux=()):
    if dims == _NN:
        (M, K), N = a.shape, b.shape[1]
    elif dims == _NT:
        (M, K), N = a.shape, b.shape[0]
    else:
        (K, M), N = a.shape, b.shape[1]
    tm = _mm_rows(M)
    tn = _col_tile(N, 1536)
    tk = _col_tile(K)
    if dims == _TN:
        tk = min(512, K)
        a_spec = pl.BlockSpec((tk, tm), lambda i, j, k: (k, i))
        b_spec = pl.BlockSpec((tk, tn), lambda i, j, k: (k, j))
        a_aux = pl.BlockSpec((tk, tm), lambda i, j, k: (k, i))
    elif dims == _NT:
        a_spec = pl.BlockSpec((tm, tk), lambda i, j, k: (i, k))
        b_spec = pl.BlockSpec((tn, tk), lambda i, j, k: (j, k))
        a_aux = pl.BlockSpec((tm, tk), lambda i, j, k: (i, k))
    else:
        a_spec = pl.BlockSpec((tm, tk), lambda i, j, k: (i, k))
        b_spec = pl.BlockSpec((tk, tn), lambda i, j, k: (k, j))
        a_aux = pl.BlockSpec((tm, tk), lambda i, j, k: (i, k))
    e_aux = pl.BlockSpec((tm, tn), lambda i, j, k: (i, j))
    aux_full = [(x, a_aux if kd == "a" else e_aux, kd) for x, kd in aux]
    return _mm(name, a, b, dims=dims, grid=(M // tm, N // tn, K // tk), a_spec=a_spec, b_spec=b_spec,
               out_shape=jax.ShapeDtypeStruct((M, N), f32), out_spec=pl.BlockSpec((tm, tn), lambda i, j, k: (i, j)),
               aux=aux_full, a_fn=a_fn, epi_fn=epi_fn, acc_shape=(tm, tn))


def _mlp_fwd(h, g, w1g, w2g, layer):
    L, D = h.shape
    tm = _mm_rows(L)
    fs = D_FF // N_DEV
    hn = _rms_fwd("mlp_norm", h, g)
    h1 = _mm("mlp_up", hn, w1g, dims=_NN, grid=(L // tm, N_DEV, 1),
             a_spec=pl.BlockSpec((tm, D), lambda i, j, k: (i, 0)),
             b_spec=pl.BlockSpec((None, None, D, fs), lambda i, j, k: (j, layer, 0, 0)),
             out_shape=jax.ShapeDtypeStruct((L, D_FF), f32), out_spec=pl.BlockSpec((tm, fs), lambda i, j, k: (i, j)),
             acc_shape=(tm, fs))
    tn = D
    h_out = _mm("mlp_down", h1, w2g, dims=_NN, grid=(L // tm, D // tn, N_DEV),
                a_spec=pl.BlockSpec((tm, fs), lambda i, j, k: (i, k)),
                b_spec=pl.BlockSpec((None, None, fs, tn), lambda i, j, k: (k, layer, 0, j)),
                out_shape=jax.ShapeDtypeStruct((L, D), f32), out_spec=pl.BlockSpec((tm, tn), lambda i, j, k: (i, j)),
                aux=[(h, pl.BlockSpec((tm, tn), lambda i, j, k: (i, j)), "e")],
                a_fn=_sqrelu, epi_fn=lambda acc, res: acc + res, acc_shape=(tm, tn))
    return h_out, hn, h1


def _mlp_bwd(dh, h, g, hn, h1, w1g, w2g, wl, layer, dw1_buf, dw2_buf):
    L, D = h.shape
    tm = _mm_rows(L)
    fs = D_FF // N_DEV
    tk = _mm_rows(L)
    dh1 = _mm("mlp_down_dx", dh, w2g, dims=_NT, grid=(L // tm, N_DEV, 1),
              a_spec=pl.BlockSpec((tm, D), lambda i, j, k: (i, 0)),
              b_spec=pl.BlockSpec((None, None, fs, D), lambda i, j, k: (j, wl, 0, 0)),
              out_shape=jax.ShapeDtypeStruct((L, D_FF), f32), out_spec=pl.BlockSpec((tm, fs), lambda i, j, k: (i, j)),
              aux=[(h1, pl.BlockSpec((tm, fs), lambda i, j, k: (i, j)), "e")],
              epi_fn=lambda acc, pre: acc * (2.0 * jnp.maximum(pre, 0.0)), acc_shape=(tm, fs))
    dw2_buf = _mm("mlp_down_dw", h1, dh, dims=_TN, grid=(N_DEV, 1, L // tk),
                  a_spec=pl.BlockSpec((tk, fs), lambda i, j, k: (k, i)),
                  b_spec=pl.BlockSpec((tk, D), lambda i, j, k: (k, 0)),
                  out_shape=jax.ShapeDtypeStruct(dw2_buf.shape, f32),
                  out_spec=pl.BlockSpec((None, None, fs, D), lambda i, j, k: (i, layer, 0, 0)),
                  a_fn=_sqrelu, acc_shape=(fs, D), out_init=dw2_buf)
    tr = D
    dw1_buf = _mm("mlp_up_dw", hn, dh1, dims=_TN, grid=(D // tr, N_DEV, L // tk),
                  a_spec=pl.BlockSpec((tk, tr), lambda i, j, k: (k, i)),
                  b_spec=pl.BlockSpec((tk, fs), lambda i, j, k: (k, j)),
                  out_shape=jax.ShapeDtypeStruct(dw1_buf.shape, f32),
                  out_spec=pl.BlockSpec((None, None, tr, fs), lambda i, j, k: (j, layer, i, 0)),
                  acc_shape=(tr, fs), out_init=dw1_buf)
    tn = D
    dhn = _mm("mlp_up_dx", dh1, w1g, dims=_NT, grid=(L // tm, D // tn, N_DEV),
              a_spec=pl.BlockSpec((tm, fs), lambda i, j, k: (i, k)),
              b_spec=pl.BlockSpec((None, None, tn, fs), lambda i, j, k: (k, wl, j, 0)),
              out_shape=jax.ShapeDtypeStruct((L, D), f32), out_spec=pl.BlockSpec((tm, tn), lambda i, j, k: (i, j)),
              acc_shape=(tm, tn))
    dh_in, dg = _rms_bwd("mlp_norm_bwd", h, g, dhn, dh)
    return dh_in, dg, dw1_buf, dw2_buf


def _shift_dn(x, s, row):
    return x if s == 0 else jnp.where(row >= s, pltpu.roll(x, s, 0), 0.0)


def _shift_up(x, s, row):
    n = x.shape[0]
    return x if s == 0 else jnp.where(row < n - s, pltpu.roll(x, n - s, 0), 0.0)


def _conv_pre(x, w, b, row):
    c = jnp.broadcast_to(b, x.shape)
    for j in range(4):
        c = c + w[j:j + 1, :] * _shift_dn(x, 3 - j, row)
    return c


def _conv_fwd(name, x_arr, blk_off, w, b):
    L = x_arr.shape[0]
    C = w.shape[1]

    def f(x, wv, bv):
        c = _conv_pre(x, wv, bv, _iota(x.shape, 0))
        return c * jax.nn.sigmoid(c)

    return _ew(name, f, [(x_arr, (L, 128), lambda j, z: (0, blk_off + j)), (w, (4, 128), lambda j, z: (0, j)),
                         (b, (1, 128), lambda j, z: (0, j))],
               [((L, C), f32, (L, 128), lambda j, z: (0, j), None)], (C // 128, 1))[0]


def _conv_bwd(name, x_arr, blk_off, w, b, dy):
    L = x_arr.shape[0]
    C = w.shape[1]

    def f(x, wv, bv, g):
        row = _iota(x.shape, 0)
        c = _conv_pre(x, wv, bv, row)
        s = jax.nn.sigmoid(c)
        dc = g * (s * (1.0 + c * (1.0 - s)))
        dx = jnp.zeros_like(x)
        dw = jnp.zeros((4, 128), f32)
        r4 = _iota((4, 128), 0)
        for j in range(4):
            dx = dx + wv[j:j + 1, :] * _shift_up(dc, 3 - j, row)
            dwj = jnp.sum(dc * _shift_dn(x, 3 - j, row), axis=0, keepdims=True)
            dw = dw + jnp.where(r4 == j, jnp.broadcast_to(dwj, (4, 128)), 0.0)
        return dx, dw, jnp.sum(dc, axis=0, keepdims=True)

    return _ew(name, f, [(x_arr, (L, 128), lambda j, z: (0, blk_off + j)), (w, (4, 128), lambda j, z: (0, j)),
                         (b, (1, 128), lambda j, z: (0, j)), (dy, (L, 128), lambda j, z: (0, j))],
               [((L, C), f32, (L, 128), lambda j, z: (0, j), None), ((4, C), f32, (4, 128), lambda j, z: (0, j), None),
                ((1, C), f32, (1, 128), lambda j, z: (0, j), None)], (C // 128, 1))


def _l2norm(t):
    return t * lax.rsqrt(jnp.sum(t * t, axis=-1, keepdims=True) + 1e-6)


def _gdn_act(cq, ck, ab, alog, dtb):
    h = pl.program_id(1)
    qn = _l2norm(cq) * (128.0 ** -0.5)
    kn = _l2norm(ck)
    lane = _iota(ab.shape, 1)
    a_raw = jnp.sum(jnp.where(lane == h, ab, 0.0), axis=1, keepdims=True)
    b_raw = jnp.sum(jnp.where(lane == h + GDN_HEADS, ab, 0.0), axis=1, keepdims=True)
    lane1 = _iota(alog.shape, 1)
    al = jnp.sum(jnp.where(lane1 == h, alog, 0.0), axis=1, keepdims=True)
    db = jnp.sum(jnp.where(lane1 == h, dtb, 0.0), axis=1, keepdims=True)
    g = -jnp.exp(al) * jax.nn.softplus(a_raw + db)
    beta = jax.nn.sigmoid(b_raw)
    return qn, kn, jnp.broadcast_to(g, cq.shape), jnp.broadcast_to(beta, cq.shape)


def _each(f, *lists):
    return [f(*a) for a in zip(*lists)]


def _gdn_chunk(states, consts, vals, dots):
    S = list(states)
    cut = [slice(128 * i, 128 * i + 128) for i in range(len(S))]
    q, k, v, gb, bb = ([t[:, c] for c in cut] for t in vals)
    C = vals[0].shape[0]
    row, col = _iota((C, C), 0), _iota((C, C), 1)
    causal, strict = row >= col, row > col
    ltri = causal.astype(f32)
    eye = (row == col).astype(f32)
    e0 = (_iota((C, 128), 1) == 0).astype(f32)
    last = _iota((C, 1), 0) == C - 1
    Gb = _each(lambda g: dots.dot01(ltri, g), gb)
    Gc = _each(lambda g: jnp.mean(g, axis=1, keepdims=True), Gb)
    Gr = _each(lambda g: dots.dot01(e0, g, _NT), Gb)
    bc = _each(lambda b: jnp.mean(b, axis=1, keepdims=True), bb)
    decay = _each(lambda gc, gr: jnp.where(causal, jnp.exp(jnp.where(causal, gc - gr, 0.0)), 0.0), Gc, Gr)
    kk = _each(lambda a: _dotb(a, a, _NT), k)
    A = _each(lambda b, x, d: jnp.where(strict, b * x * d, 0.0), bc, kk, decay)
    M = _each(lambda a: eye - a, A)
    P = _each(lambda a: dots.dot3(a, a), A)
    for it in range(5):
        M = _each(lambda m, p: m + dots.dot3(m, p), M, P)
        if it < 4:
            P = _each(lambda p: dots.dot3(p, p), P)
    eG = _each(jnp.exp, Gc)
    u = _each(lambda m, x, b: _dotb(m, x * b), M, v, bc)
    w = _each(lambda m, x, b, e: _dotb(m, x * (b * e)), M, k, bc, eG)
    qk = _each(lambda a, b, d: _dotb(a, b, _NT) * d, q, k, decay)
    g_last = _each(lambda gc: jnp.sum(jnp.where(last, gc, 0.0), axis=0, keepdims=True), Gc)
    v_new = _each(lambda a, b, s: a - _dotb(b, s), u, w, S)
    o = _each(lambda a, e, s, b, x: _dotb(a * e, s) + _dotb(b, x), q, eG, S, qk, v_new)
    S_new = _each(lambda gl, s, a, gc, x: jnp.exp(gl) * s + _dotb(a * jnp.exp(gl - gc), x, _TN), g_last, S, k, Gc, v_new)
    return S_new, [jnp.concatenate(o, axis=1)]


def _gdn_post(o, gate, g):
    return _rms(o, g) * (gate * jax.nn.sigmoid(gate))


def _pad_row(v):
    return jnp.pad(v.astype(f32), (0, 128 - v.shape[0])).reshape(1, 128)


def _gdn_fwd(h, g_norm, w_ext, conv_w, a_log, dt_bias, o_norm_g, w_out):
    L, D = h.shape
    tm = _head_rows(L)
    nc = L // CHUNK
    H = GDN_HEADS
    hn = _rms_fwd("mix_norm", h, g_norm)
    proj = _mm_plain("gdn_in", hn, w_ext, _NN)
    zb = jnp.zeros((1, 3 * D), f32)
    cq = _conv_fwd("gdn_conv", proj, 0, conv_w, zb)
    alog, dtb = _pad_row(a_log), _pad_row(dt_bias)
    act_ins = [(cq, (tm, 128), lambda r, hh: (r, hh)), (cq, (tm, 128), lambda r, hh: (r, H + hh)),
               (proj, (tm, 128), lambda r, hh: (r, 4 * H)), (alog, (1, 128), lambda r, hh: (0, 0)),
               (dtb, (1, 128), lambda r, hh: (0, 0))]
    qn, kn, gb, bb = _ew("gdn_act", _gdn_act, act_ins,
                         [((L, D), f32, (tm, 128), lambda r, hh: (r, hh), None)] * 4, (L // tm, H))
    cblk = (CHUNK, 128 * GDN_HB)
    core_ins = [(qn, cblk, lambda u, c: (c, u)), (kn, cblk, lambda u, c: (c, u)),
                (cq, cblk, lambda u, c: (c, 2 * H // GDN_HB + u), (L, D), lambda u, c: (c, u)),
                (gb, cblk, lambda u, c: (c, u)), (bb, cblk, lambda u, c: (c, u))]
    (o,), saved_s = _scan_fwd("gdn_core", _gdn_chunk, GDN_HB, (128, 128), [], core_ins,
                              [((L, D), cblk, lambda u, c: (c, u))], H // GDN_HB, nc)
    on = o_norm_g.reshape(1, 128)
    post_ins = [(o, (tm, 128), lambda r, hh: (r, hh)), (proj, (tm, 128), lambda r, hh: (r, 3 * H + hh)),
                (on, (1, 128), lambda r, hh: (0, 0))]
    y = _ew("gdn_post", _gdn_post, post_ins, [((L, D), f32, (tm, 128), lambda r, hh: (r, hh), None)], (L // tm, H))[0]
    h_out = _mm_plain("gdn_out", y, w_out, _NN, epi_fn=lambda acc, res: acc + res, aux=[(h, "e")])
    saved = dict(hn=hn, proj=proj, cq=cq, alog=alog, dtb=dtb, act_ins=act_ins, core_ins=core_ins, saved_s=saved_s,
                 post_ins=post_ins, y=y, zb=zb)
    return h_out, saved


def _gdn_bwd(dh, h, g_norm, w_ext, conv_w, w_out, sv):
    L, D = h.shape
    tm = _head_rows(L)
    nc = L // CHUNK
    H = GDN_HEADS
    dy = _mm_plain("gdn_out_dx", dh, w_out, _NT)
    dw_out = _mm_plain("gdn_out_dw", sv["y"], dh, _TN)
    hd = ((L, D), f32, (tm, 128), lambda r, hh: (r, hh), None)
    d_o, d_gate, d_on = _ew("gdn_post_bwd", _vjp_fn(_gdn_post, 3),
                            sv["post_ins"] + [(dy, (tm, 128), lambda r, hh: (r, hh))],
                            [hd, hd, ((1, 128), f32, (1, 128), lambda r, hh: (0, 0), "all")], (L // tm, H))
    cblk = (CHUNK, 128 * GDN_HB)
    _, (dqn, dkn, dv, dgb, dbb) = _scan_bwd("gdn_core_bwd", _gdn_chunk, GDN_HB, (128, 128), [], sv["core_ins"],
                                            sv["saved_s"], [(d_o, cblk, lambda u, c: (c, u))], H // GDN_HB, nc)
    cts = [(t, (tm, 128), lambda r, hh: (r, hh)) for t in (dqn, dkn, dgb, dbb)]
    row128 = ((1, 128), f32, (1, 128), lambda r, hh: (0, 0), "all")
    d_cq, d_ck, d_ab, d_alog, d_dtb = _ew(
        "gdn_act_bwd", _vjp_fn(_gdn_act, 5), sv["act_ins"] + cts,
        [hd, hd, ((L, 128), f32, (tm, 128), lambda r, hh: (r, 0), "inner"), row128, row128], (L // tm, H))
    d_conv_out = jnp.concatenate([d_cq, d_ck, dv], axis=1)
    d_conv_in, d_conv_w, _ = _conv_bwd("gdn_conv_bwd", sv["proj"], 0, conv_w, sv["zb"], d_conv_out)
    d_proj = jnp.concatenate([d_conv_in, d_gate, d_ab], axis=1)
    dw_ext = _mm_plain("gdn_in_dw", sv["hn"], d_proj, _TN)
    dhn = _mm_plain("gdn_in_dx", d_proj, w_ext, _NT)
    dh_in, dg = _rms_bwd("mix_norm_bwd", h, g_norm, dhn, dh)
    grads = dict(norm=dg, w_ext=dw_ext, conv_w=d_conv_w, a_log=d_alog[0, :H], dt_bias=d_dtb[0, :H],
                 o_norm_g=d_on[0], w_out=dw_out)
    return dh_in, grads


def _expand_lanes(row, width, rep):
    sel = ((_iota((128, width), 1) // rep) == _iota((128, width), 0)).astype(f32)
    return jnp.mean(_dot(jnp.broadcast_to(row, (8, 128)), sel), axis=0, keepdims=True)


def _s5_params(lre, lim, ldt, wbr, wbi):
    dt = jnp.exp(_expand_lanes(ldt, 512, S5_STATE))
    mag = jnp.exp(lre * dt)
    ang = lim * dt
    abr, abi = mag * jnp.cos(ang), mag * jnp.sin(ang)
    nr = abr - 1.0
    den = lre * lre + lim * lim
    cr = (nr * lre + abi * lim) / den
    ci = (abi * lre - nr * lim) / den
    return abr, abi, cr * wbr - ci * wbi, cr * wbi + ci * wbr


def _s5_scan(name, xr, xi, ar, ai, rev, want_prev):
    L, W = xr.shape
    nb = L // 8
    n_out = 4 if want_prev else 2

    def body(xr_ref, xi_ref, ar_ref, ai_ref, *outs):
        a_r = ar_ref[...]
        a_i = -ai_ref[...] if rev else ai_ref[...]

        def cm(p, q):
            return p[0] * q[0] - p[1] * q[1], p[0] * q[1] + p[1] * q[0]

        a1 = (a_r, a_i)
        a2 = cm(a1, a1)
        a3 = cm(a2, a1)
        a4 = cm(a2, a2)
        pw = [a1, a2, a3, a4, cm(a4, a1), cm(a4, a2), cm(a4, a3), cm(a4, a4)]
        blk8 = (8, S5_SCAN_LANES)
        row = _iota(blk8, 0)
        tab_r = jnp.zeros(blk8, f32)
        tab_i = jnp.zeros(blk8, f32)
        for t in range(8):
            idx = 7 - t if rev else t
            tab_r = jnp.where(row == idx, jnp.broadcast_to(pw[t][0], blk8), tab_r)
            tab_i = jnp.where(row == idx, jnp.broadcast_to(pw[t][1], blk8), tab_i)
        lv = [(d, jnp.broadcast_to(p[0], blk8), jnp.broadcast_to(p[1], blk8)) for d, p in ((1, a1), (2, a2), (4, a4))]

        def step(i, carry):
            cr, ci = carry
            blk = nb - 1 - i if rev else i
            r0 = pl.multiple_of(blk * 8, 8)
            x_r = xr_ref[pl.ds(r0, 8), :]
            x_i = xi_ref[pl.ds(r0, 8), :]
            for d, p_r, p_i in lv:
                if rev:
                    s_r = jnp.where(row < 8 - d, pltpu.roll(x_r, 8 - d, 0), 0.0)
                    s_i = jnp.where(row < 8 - d, pltpu.roll(x_i, 8 - d, 0), 0.0)
                else:
                    s_r = jnp.where(row >= d, pltpu.roll(x_r, d, 0), 0.0)
                    s_i = jnp.where(row >= d, pltpu.roll(x_i, d, 0), 0.0)
                x_r, x_i = x_r + p_r * s_r - p_i * s_i, x_i + p_r * s_i + p_i * s_r
            x_r, x_i = x_r + tab_r * cr - tab_i * ci, x_i + tab_r * ci + tab_i * cr
            outs[0][pl.ds(r0, 8), :] = x_r
            outs[1][pl.ds(r0, 8), :] = x_i
            if want_prev:
                outs[2][pl.ds(r0, 8), :] = jnp.where(row >= 1, pltpu.roll(x_r, 1, 0), cr)
                outs[3][pl.ds(r0, 8), :] = jnp.where(row >= 1, pltpu.roll(x_i, 1, 0), ci)
            e = 0 if rev else 7
            return jnp.broadcast_to(x_r[e:e + 1, :], blk8), jnp.broadcast_to(x_i[e:e + 1, :], blk8)

        lax.fori_loop(0, nb, step, (jnp.zeros(blk8, f32), jnp.zeros(blk8, f32)))

    per = 512 // S5_SCAN_LANES
    col = pl.BlockSpec((L, S5_SCAN_LANES), lambda q, z: (0, q))
    aspec = pl.BlockSpec((None, 1, S5_SCAN_LANES), lambda q, z: (q // per, 0, q % per))
    return pl.pallas_call(
        body, name=name, grid=(W // S5_SCAN_LANES, 1), in_specs=[col, col, aspec, aspec], out_specs=[col] * n_out,
        out_shape=[jax.ShapeDtypeStruct((L, W), f32)] * n_out, compiler_params=_params(2),
    )(xr, xi, ar, ai)


def _blockdiag(t, n_in, n_out):
    t4 = t.reshape(8, 8, n_in, n_out)
    return jnp.einsum("jaio,ab->jaibo", t4, jnp.eye(8, dtype=t.dtype)).reshape(8, 8 * n_in, 8 * n_out)


def _blockdiag_t(w, n_in, n_out):
    w5 = w.reshape(8, 8, n_in, 8, n_out)
    return jnp.einsum("jaibo,ab->jaio", w5, jnp.eye(8, dtype=w.dtype)).reshape(64, n_in, n_out)


def _glu(ag, h):
    n = ag.shape[1] // 2
    return h + ag[:, :n] * jax.nn.sigmoid(ag[:, n:])


def _s5_fwd(h, g_norm, w_in, lam_re, lam_im, log_dt, b_re, b_im, c_re, c_im, d_skip, w_out_g):
    L, D = h.shape
    tm, te = _mm_rows(L), _row_tile(L)
    W = 8 * 512
    hn = _rms_fwd("mix_norm", h, g_norm)
    u = _mm_plain("s5_in", hn, w_in, _NN)
    lre, lim = lam_re.reshape(8, 1, 512), lam_im.reshape(8, 1, 512)
    ldt = jnp.pad(log_dt.reshape(8, 1, 8), ((0, 0), (0, 0), (0, 120)))
    wbr = _blockdiag(b_re.transpose(0, 2, 1), 16, 64)
    wbi = _blockdiag(b_im.transpose(0, 2, 1), 16, 64)
    wcr = _blockdiag(c_re.transpose(0, 2, 1), 64, 16)
    wci = _blockdiag(c_im.transpose(0, 2, 1), 64, 16)
    jb = lambda shape: (shape, lambda j, z: (j, 0, 0))
    par_ins = [(lre, *jb((None, 1, 512))), (lim, *jb((None, 1, 512))), (ldt, *jb((None, 1, 128))),
               (wbr, *jb((None, 128, 512))), (wbi, *jb((None, 128, 512)))]
    abr, abi, bbr, bbi = _ew("s5_params", _s5_params, par_ins,
                             [((8, 1, 512), f32, *jb((None, 1, 512)), None)] * 2
                             + [((8, 128, 512), f32, *jb((None, 128, 512)), None)] * 2, (8, 1))

    def bu(name, wb):
        return _mm(name, u, wb, dims=_NN, grid=(L // tm, 8, 1),
                   a_spec=pl.BlockSpec((tm, 128), lambda i, j, k: (i, j)),
                   b_spec=pl.BlockSpec((None, 128, 512), lambda i, j, k: (j, 0, 0)),
                   out_shape=jax.ShapeDtypeStruct((L, W), f32), out_spec=pl.BlockSpec((tm, 512), lambda i, j, k: (i, j)),
                   acc_shape=(tm, 512))

    bur, bui = bu("s5_bu", bbr), bu("s5_bu", bbi)
    sr, si, pr, pi = _s5_scan("s5_scan", bur, bui, abr, abi, False, True)
    d_row = d_skip.reshape(1, D)
    cspec = dict(a_spec=pl.BlockSpec((tm, 512), lambda i, j, k: (i, j)),
                 b_spec=pl.BlockSpec((None, 512, 128), lambda i, j, k: (j, 0, 0)),
                 out_shape=jax.ShapeDtypeStruct((L, D), f32), out_spec=pl.BlockSpec((tm, 128), lambda i, j, k: (i, j)),
                 acc_shape=(tm, 128))
    e128 = pl.BlockSpec((tm, 128), lambda i, j, k: (i, j))
    pre1 = _mm("s5_c_re", sr, wcr, dims=_NN, grid=(L // tm, 8, 1), **cspec)
    pre = _mm("s5_c_im", si, wci, dims=_NN, grid=(L // tm, 8, 1),
              aux=[(pre1, e128, "e"), (u, e128, "e"), (d_row, pl.BlockSpec((1, 128), lambda i, j, k: (0, j)), "e")],
              epi_fn=lambda acc, p1, uu, dd: p1 - acc + dd * uu, **cspec)
    ws = D // N_DEV * 2
    ag = _mm("s5_out", pre, w_out_g, dims=_NN, grid=(L // tm, N_DEV, 1),
             a_spec=pl.BlockSpec((tm, D), lambda i, j, k: (i, 0)),
             b_spec=pl.BlockSpec((None, None, D, ws), lambda i, j, k: (j, 0, 0, 0)),
             out_shape=jax.ShapeDtypeStruct((L, 2 * D), f32), out_spec=pl.BlockSpec((tm, ws), lambda i, j, k: (i, j)),
             a_fn=jax.nn.gelu, acc_shape=(tm, ws))
    h_out = _ew("s5_glu", _glu, [(ag, *_rows(te, 2 * D)), (h, *_rows(te, D))],
                [((L, D), f32, *_rows(te, D), None)], (L // te, 1))[0]
    saved = dict(hn=hn, u=u, par_ins=par_ins, abr=abr, abi=abi, bbr=bbr, bbi=bbi, wcr=wcr, wci=wci, sr=sr, si=si,
                 pr=pr, pi=pi, pre=pre, ag=ag, d_row=d_row)
    return h_out, saved


def _s5_bwd(dh, h, g_norm, w_in, w_out_g, sv):
    L, D = h.shape
    tm, te = _mm_rows(L), _row_tile(L)
    tk = min(512, L)
    W = 8 * 512
    ws = D // N_DEV * 2
    u, pre, d_row = sv["u"], sv["pre"], sv["d_row"]
    d_ag = _ew("s5_glu_bwd", lambda ag, hv, g: _vjp_fn(_glu, 2)(ag, hv, g)[0],
               [(sv["ag"], *_rows(te, 2 * D)), (h, *_rows(te, D)), (dh, *_rows(te, D))],
               [((L, 2 * D), f32, *_rows(te, 2 * D), None)], (L // te, 1))[0]
    tr = D
    dw_out = _mm("s5_out_dw", pre, d_ag, dims=_TN, grid=(D // tr, N_DEV, L // tk),
                 a_spec=pl.BlockSpec((tk, tr), lambda i, j, k: (k, i)),
                 b_spec=pl.BlockSpec((tk, ws), lambda i, j, k: (k, j)),
                 out_shape=jax.ShapeDtypeStruct((N_DEV, 1, D, ws), f32),
                 out_spec=pl.BlockSpec((None, None, tr, ws), lambda i, j, k: (j, 0, i, 0)),
                 a_fn=jax.nn.gelu, acc_shape=(tr, ws))
    tn = 512
    dpre = _mm("s5_out_dx", d_ag, w_out_g, dims=_NT, grid=(L // tm, D // tn, N_DEV),
               a_spec=pl.BlockSpec((tm, ws), lambda i, j, k: (i, k)),
               b_spec=pl.BlockSpec((None, None, tn, ws), lambda i, j, k: (k, 0, j, 0)),
               out_shape=jax.ShapeDtypeStruct((L, D), f32), out_spec=pl.BlockSpec((tm, tn), lambda i, j, k: (i, j)),
               aux=[(pre, pl.BlockSpec((tm, tn), lambda i, j, k: (i, j)), "e")],
               epi_fn=lambda acc, p: _vjp_fn(jax.nn.gelu, 1)(p, acc)[0], acc_shape=(tm, tn))
    d_d = _ew("s5_dskip", lambda a, b: jnp.sum(a * b, axis=0, keepdims=True),
              [(dpre, *_rows(te, D)), (u, *_rows(te, D))], [((1, D), f32, *_const((1, D)), "all")], (L // te, 1))[0]
    neg = lambda acc: -acc
    dsspec = dict(dims=_NT, grid=(L // tm, 8, 1), a_spec=pl.BlockSpec((tm, 128), lambda i, j, k: (i, j)),
                  b_spec=pl.BlockSpec((None, 512, 128), lambda i, j, k: (j, 0, 0)),
                  out_shape=jax.ShapeDtypeStruct((L, W), f32), out_spec=pl.BlockSpec((tm, 512), lambda i, j, k: (i, j)),
                  acc_shape=(tm, 512))
    dsr = _mm("s5_c_re_dx", dpre, sv["wcr"], **dsspec)
    dsi = _mm("s5_c_im_dx", dpre, sv["wci"], epi_fn=neg, **dsspec)
    dwspec = dict(dims=_TN, grid=(8, 1, L // tk), a_spec=pl.BlockSpec((tk, 512), lambda i, j, k: (k, i)),
                  b_spec=pl.BlockSpec((tk, 128), lambda i, j, k: (k, i)),
                  out_shape=jax.ShapeDtypeStruct((8, 512, 128), f32),
                  out_spec=pl.BlockSpec((None, 512, 128), lambda i, j, k: (i, 0, 0)), acc_shape=(512, 128))
    dwcr = _mm("s5_c_re_dw", sv["sr"], dpre, **dwspec)
    dwci = _mm("s5_c_im_dw", sv["si"], dpre, epi_fn=neg, **dwspec)
    lr, li = _s5_scan("s5_scan_bwd", dsr, dsi, sv["abr"], sv["abi"], True, False)

    def da(lrv, liv, prv, piv):
        return (jnp.sum(lrv * prv + liv * piv, axis=0, keepdims=True),
                jnp.sum(liv * prv - lrv * piv, axis=0, keepdims=True))

    sblk = ((te, 512), lambda j, r: (r, j))
    dabr, dabi = _ew("s5_dlam", da, [(lr, *sblk), (li, *sblk), (sv["pr"], *sblk), (sv["pi"], *sblk)],
                     [((8, 1, 512), f32, (None, 1, 512), lambda j, r: (j, 0, 0), "inner")] * 2, (8, L // te))
    dbspec = dict(dims=_TN, grid=(8, 1, L // tk), a_spec=pl.BlockSpec((tk, 128), lambda i, j, k: (k, i)),
                  b_spec=pl.BlockSpec((tk, 512), lambda i, j, k: (k, i)),
                  out_shape=jax.ShapeDtypeStruct((8, 128, 512), f32),
                  out_spec=pl.BlockSpec((None, 128, 512), lambda i, j, k: (i, 0, 0)), acc_shape=(128, 512))
    dbbr = _mm("s5_bu_dw", u, lr, **dbspec)
    dbbi = _mm("s5_bu_dw", u, li, **dbspec)
    duspec = dict(dims=_NT, grid=(L // tm, 8, 1), a_spec=pl.BlockSpec((tm, 512), lambda i, j, k: (i, j)),
                  b_spec=pl.BlockSpec((None, 128, 512), lambda i, j, k: (j, 0, 0)),
                  out_shape=jax.ShapeDtypeStruct((L, D), f32), out_spec=pl.BlockSpec((tm, 128), lambda i, j, k: (i, j)),
                  acc_shape=(tm, 128))
    e128 = pl.BlockSpec((tm, 128), lambda i, j, k: (i, j))
    du1 = _mm("s5_bu_dx_re", lr, sv["bbr"], **duspec)
    du = _mm("s5_bu_dx_im", li, sv["bbi"],
             aux=[(du1, e128, "e"), (dpre, e128, "e"), (d_row, pl.BlockSpec((1, 128), lambda i, j, k: (0, j)), "e")],
             epi_fn=lambda acc, d1, dp, dd: acc + d1 + dp * dd, **duspec)
    jb = lambda shape: (shape, lambda j, z: (j, 0, 0))
    cts = [(dabr, *jb((None, 1, 512))), (dabi, *jb((None, 1, 512))), (dbbr, *jb((None, 128, 512))),
           (dbbi, *jb((None, 128, 512)))]
    dlre, dlim, dldt, dwbr, dwbi = _ew(
        "s5_params_bwd", _vjp_fn(_s5_params, 5), sv["par_ins"] + cts,
        [((8, 1, 512), f32, *jb((None, 1, 512)), None)] * 2 + [((8, 1, 128), f32, *jb((None, 1, 128)), None)]
        + [((8, 128, 512), f32, *jb((None, 128, 512)), None)] * 2, (8, 1))
    dw_in = _mm_plain("s5_in_dw", sv["hn"], du, _TN)
    dhn = _mm_plain("s5_in_dx", du, w_in, _NT)
    dh_in, dg = _rms_bwd("mix_norm_bwd", h, g_norm, dhn, dh)
    grads = dict(norm=dg, w_in=dw_in, lam_re=dlre.reshape(64, 64), lam_im=dlim.reshape(64, 64),
                 log_dt=dldt[:, 0, :8].reshape(64),
                 b_re=_blockdiag_t(dwbr, 16, 64).transpose(0, 2, 1), b_im=_blockdiag_t(dwbi, 16, 64).transpose(0, 2, 1),
                 c_re=_blockdiag_t(dwcr, 64, 16).transpose(0, 2, 1), c_im=_blockdiag_t(dwci, 64, 16).transpose(0, 2, 1),
                 d=d_d[0], w_out=dw_out)
    return dh_in, grads


def _m2_act(dt_raw, dtbias, alog):
    dt = jax.nn.softplus(dt_raw + dtbias)
    da = dt * (-jnp.exp(alog))
    sel = ((_iota((128, M2_INNER), 1) // 64) == _iota((128, M2_INNER), 0)).astype(f32)
    return _dot(dt, sel), _dot(da, sel)


def _m2_dexp(d):
    return _expand_lanes(d, M2_INNER, 64)


def _ssd_chunk(states, consts, vals, dots):
    (dsk,) = consts
    S = list(states)
    n = len(S)
    cut = [slice(128 * i, 128 * i + 128) for i in range(n)]
    x, dtb, dab = ([t[:, c] for c in cut] for t in vals[:3])
    dsk = [dsk[:, c] for c in cut]
    B = [vals[3][:, cut[i // 2]] for i in range(n)]
    Cm = [vals[4][:, cut[i // 2]] for i in range(n)]
    C = vals[0].shape[0]
    row, col = _iota((C, C), 0), _iota((C, C), 1)
    causal = row >= col
    ltri = causal.astype(f32)
    lane = _iota((C, 128), 1)
    last = _iota((C, 128), 0) == C - 1
    eye128 = _iota((128, 128), 0) == _iota((128, 128), 1)
    head = [jnp.logical_and(lane >= 64 * hh, lane < 64 * hh + 64) for hh in range(2)]
    pick = [(lane == 64 * hh).astype(f32) for hh in range(2)]
    xdt = _each(lambda a, b: a * b, x, dtb)
    cb = _each(lambda c, b: _dotb(c, b, _NT), Cm[::2], B[::2])
    cum = _each(lambda a: dots.dot01(ltri, a), dab)
    clast = _each(lambda a: jnp.sum(jnp.where(last, a, 0.0), axis=0, keepdims=True), cum)
    st = _each(lambda a, cl, cu, b: _dotb(a * jnp.exp(cl - cu), b, _TN), xdt, clast, cum, B)
    y = _each(lambda c, s, cu: _dotb(c, s, _NT) * jnp.exp(cu), Cm, S, cum)
    for hh in range(2):
        ccol = _each(lambda cu: jnp.sum(jnp.where(head[hh], cu, 0.0), axis=1, keepdims=True) * (1.0 / 64), cum)
        crow = _each(lambda cu: dots.dot01(pick[hh], cu, _NT), cum)
        lm = _each(lambda a, b: jnp.where(causal, jnp.exp(jnp.where(causal, a - b, 0.0)), 0.0), ccol, crow)
        y = [y[i] + _dotb(cb[i // 2] * lm[i], jnp.where(head[hh], xdt[i], 0.0)) for i in range(n)]
    cdcol = _each(lambda cl: jnp.sum(jnp.where(eye128, jnp.broadcast_to(jnp.exp(cl), (128, 128)), 0.0),
                                     axis=1, keepdims=True), clast)
    S_new = _each(lambda c, s, t: c * s + t, cdcol, S, st)
    out = _each(lambda a, d, b: a + d * b, y, dsk, x)
    return S_new, [jnp.concatenate(out, axis=1)]


def _m2_post(yc, z, ng):
    return _rms(yc * (z * jax.nn.sigmoid(z)), ng)


def _m2_fwd(h, g_norm, w_ext, conv_w, conv_b, dt_bias, a_log, d_skip, norm_g, w_out):
    L, D = h.shape
    tm = _row_tile(L)
    nc = L // CHUNK
    NI = M2_INNER
    hn = _rms_fwd("mix_norm", h, g_norm)
    proj = _mm_plain("m2_in", hn, w_ext, _NN)
    xbc = _conv_fwd("m2_conv", proj, NI // 128, conv_w, conv_b)
    dtb_row, alog_row, d_pad = _pad_row(dt_bias), _pad_row(a_log), _pad_row(d_skip)
    act_ins = [(proj, (tm, 128), lambda r, z: (r, 3 * NI // 128)), (dtb_row, *_const((1, 128))),
               (alog_row, *_const((1, 128)))]
    dtb, dab = _ew("m2_act", _m2_act, act_ins, [((L, NI), f32, *_rows(tm, NI), None)] * 2, (L // tm, 1))
    dsk = _ew("m2_dexp", _m2_dexp, [(d_pad, *_const((1, 128)))], [((1, NI), f32, *_const((1, NI)), None)], (1, 1))[0]
    GB = M2_GB
    x_blk, bc_blk = (CHUNK, 256 * GB), (CHUNK, 128 * GB)
    cins = [(dsk, (1, 256 * GB), lambda u, c: (0, u))]
    core_ins = [(xbc, x_blk, lambda u, c: (c, u), (L, NI), lambda u, c: (c, u)),
                (dtb, x_blk, lambda u, c: (c, u)), (dab, x_blk, lambda u, c: (c, u)),
                (xbc, bc_blk, lambda u, c: (c, 16 // GB + u), (L, D), lambda u, c: (c, u)),
                (xbc, bc_blk, lambda u, c: (c, 24 // GB + u), (L, D), lambda u, c: (c, u))]
    (yc,), saved_s = _scan_fwd("m2_core", _ssd_chunk, 2 * GB, (128, 128), cins, core_ins,
                               [((L, NI), x_blk, lambda u, c: (c, u))], 8 // GB, nc)
    tp = _head_rows(L)
    gblk = ((tp, 256), lambda g, r: (r, g))
    post_ins = [(yc, *gblk), (proj, *gblk), (norm_g, (1, 256), lambda g, r: (0, g))]
    yn = _ew("m2_post", _m2_post, post_ins, [((L, NI), f32, *gblk, None)], (8, L // tp))[0]
    h_out = _mm_plain("m2_out", yn, w_out, _NN, epi_fn=lambda acc, res: acc + res, aux=[(h, "e")])
    saved = dict(hn=hn, proj=proj, act_ins=act_ins, d_pad=d_pad, cins=cins, core_ins=core_ins, saved_s=saved_s,
                 post_ins=post_ins, yn=yn)
    return h_out, saved


def _m2_bwd(dh, h, g_norm, w_ext, conv_w, conv_b, w_out, sv):
    L, D = h.shape
    tm = _row_tile(L)
    nc = L // CHUNK
    NI = M2_INNER
    dyn = _mm_plain("m2_out_dx", dh, w_out, _NT)
    dw_out = _mm_plain("m2_out_dw", sv["yn"], dh, _TN)
    tp = _head_rows(L)
    gblk = ((tp, 256), lambda g, r: (r, g))
    d_yc, d_z, d_ng = _ew("m2_post_bwd", _vjp_fn(_m2_post, 3), sv["post_ins"] + [(dyn, *gblk)],
                          [((L, NI), f32, *gblk, None)] * 2 + [((1, NI), f32, (1, 256), lambda g, r: (0, g), "inner")],
                          (8, L // tp))
    (d_dsk,), (dx, d_dtb, d_dab, dB, dC) = _scan_bwd(
        "m2_core_bwd", _ssd_chunk, 2 * M2_GB, (128, 128), sv["cins"], sv["core_ins"], sv["saved_s"],
        [(d_yc, (CHUNK, 256 * M2_GB), lambda u, c: (c, u))], 8 // M2_GB, nc)
    row128 = ((1, 128), f32, *_const((1, 128)), "all")
    d_dt_raw, d_dtbias, d_alog = _ew(
        "m2_act_bwd", _vjp_fn(_m2_act, 3), sv["act_ins"] + [(d_dtb, *_rows(tm, NI)), (d_dab, *_rows(tm, NI))],
        [((L, 128), f32, *_rows(tm, 128), None), row128, row128], (L // tm, 1))
    d_d = _ew("m2_dexp_bwd", _vjp_fn(_m2_dexp, 1), [(sv["d_pad"], *_const((1, 128))), (d_dsk, *_const((1, NI)))],
              [((1, 128), f32, *_const((1, 128)), None)], (1, 1))[0]
    d_conv_out = jnp.concatenate([dx, dB, dC], axis=1)
    d_conv_in, d_conv_w, d_conv_b = _conv_bwd("m2_conv_bwd", sv["proj"], NI // 128, conv_w, conv_b, d_conv_out)
    d_proj = jnp.concatenate([d_z, d_conv_in, d_dt_raw], axis=1)
    dw_ext = _mm_plain("m2_in_dw", sv["hn"], d_proj, _TN)
    dhn = _mm_plain("m2_in_dx", d_proj, w_ext, _NT)
    dh_in, dg = _rms_bwd("mix_norm_bwd", h, g_norm, dhn, dh)
    grads = dict(norm=dg, w_ext=dw_ext, conv_w=d_conv_w, conv_b=d_conv_b, dt_bias=d_dtbias[0, :M2_HEADS],
                 a_log=d_alog[0, :M2_HEADS], d=d_d[0, :M2_HEADS], norm_g=d_ng, w_out=dw_out)
    return dh_in, grads


def _mesh_pos():
    return lax.axis_index("x"), lax.axis_index("y"), lax.axis_index("c")


def _flip(pos, p):
    x, y, c = pos
    return (1 - x if p & 4 else x, 1 - y if p & 2 else y, 1 - c if p & 1 else c)


def _index(pos):
    return 4 * pos[0] + 2 * pos[1] + pos[2]


def _comm_call(name, body, arrays, out_shape, n_sem):
    n = len(arrays)
    hbm = pl.BlockSpec(memory_space=pl.ANY)
    return pl.pallas_call(
        body, name=name, in_specs=[hbm] * n, out_specs=[hbm] * len(out_shape), out_shape=out_shape,
        scratch_shapes=[pltpu.SemaphoreType.DMA((n, n_sem)), pltpu.SemaphoreType.DMA((n, n_sem)),
                        pltpu.SemaphoreType.DMA((n, 4))],
    )(*arrays)


def _gather(name, arrays):
    n = len(arrays)

    def body(*refs):
        ins, outs = refs[:n], refs[n:2 * n]
        send_sems, recv_sems, loc_sems = refs[2 * n:]
        me = _mesh_pos()
        c = me[2]
        sib = _flip(me, 1)
        chips = [_flip(me, 4), _flip(me, 2), _flip(me, 6)]

        def copy(w, k, block, to, src=None):
            slab = outs[w].at[_index(block)]
            return pltpu.make_async_remote_copy(
                src_ref=slab if src is None else src, dst_ref=slab, send_sem=send_sems.at[w, k],
                recv_sem=recv_sems.at[w, k], device_id=to, device_id_type=MESH)

        local = [pltpu.make_async_copy(ins[w], outs[w].at[_index(me)], loc_sems.at[w, 0]) for w in range(n)]
        for cp in local:
            cp.start()
        first = [copy(w, 0, me, sib, src=ins[w]) for w in range(n)]
        first += [copy(w, 1 + j, me, chip, src=ins[w]) for j, chip in enumerate(chips) for w in range(n)]
        for cp in first:
            cp.start()
        passed = []
        for j, chip in enumerate(chips):
            for w in range(n):
                copy(w, 1 + j, chip, me).wait_recv()
                fwd = copy(w, 4 + j, chip, sib)
                fwd.start()
                passed.append(fwd)
        for w in range(n):
            copy(w, 0, sib, me).wait_recv()
        for j, chip in enumerate(chips):
            for w in range(n):
                copy(w, 4 + j, (chip[0], chip[1], 1 - c), me).wait_recv()
        for cp in first + passed:
            cp.wait_send()
        for cp in local:
            cp.wait()

    out_shape = [jax.ShapeDtypeStruct((N_DEV,) + a.shape, a.dtype) for a in arrays]
    return _comm_call(name, body, arrays, out_shape, N_DEV - 1)


_HBM = pl.BlockSpec(memory_space=pltpu.HBM)
_SEM = pl.BlockSpec(memory_space=pltpu.SEMAPHORE)
_SPLIT_COPIES = 4


def _split_targets(me):
    return [_flip(me, 1), _flip(me, 4), _flip(me, 2), _flip(me, 6)]


def _gather_start(name, arrays, lands):
    n = len(arrays)
    ns = n * _SPLIT_COPIES

    def body(*refs):
        ins, land = refs[:n], refs[n:2 * n]
        send_sems, recv_sems = refs[2 * n:2 * n + ns], refs[2 * n + ns:2 * n + 2 * ns]
        token = refs[4 * n + 2 * ns]
        me = _mesh_pos()
        for w in range(n):
            for k, to in enumerate(_split_targets(me)):
                pltpu.make_async_remote_copy(
                    src_ref=ins[w], dst_ref=land[w].at[_index(me)], send_sem=send_sems[w * _SPLIT_COPIES + k],
                    recv_sem=recv_sems[w * _SPLIT_COPIES + k], device_id=to, device_id_type=MESH).start()
        token[...] = jnp.zeros_like(token)

    sem = pltpu.SemaphoreType.DMA(())
    res = pl.pallas_call(
        body, name=name,
        out_shape=(*[sem] * (2 * ns), *[pltpu.HBM(a.shape, a.dtype) for a in arrays],
                   *[pltpu.HBM(a.shape, a.dtype) for a in lands], jax.ShapeDtypeStruct((8, 128), f32)),
        in_specs=[_HBM] * (2 * n),
        out_specs=(*[_SEM] * (2 * ns), *[_HBM] * (2 * n), pl.BlockSpec(memory_space=pltpu.VMEM)),
        input_output_aliases={i: 2 * ns + i for i in range(2 * n)},
        compiler_params=pltpu.CompilerParams(has_side_effects=pltpu.SideEffectType.DATAFLOW_SIDE_EFFECTING),
    )(*[pltpu.with_memory_space_constraint(a, pltpu.HBM) for a in list(arrays) + list(lands)])
    sems, rest = res[:2 * ns], res[2 * ns:]
    return sems[:ns], sems[ns:], rest[:n], rest[n:2 * n], rest[2 * n]


def _gather_wait(name, arrays, lands, send_sems, recv_sems, after):
    n = len(arrays)
    ns = n * _SPLIT_COPIES

    def body(*refs):
        ins, land = refs[:n], refs[n:2 * n]
        s_sems, r_sems = refs[2 * n:2 * n + ns], refs[2 * n + ns:2 * n + 2 * ns]
        me = _mesh_pos()
        for w in range(n):
            for k, peer in enumerate(_split_targets(me)):
                cp = pltpu.make_async_remote_copy(
                    src_ref=ins[w], dst_ref=land[w].at[_index(peer)], send_sem=s_sems[w * _SPLIT_COPIES + k],
                    recv_sem=r_sems[w * _SPLIT_COPIES + k], device_id=peer, device_id_type=MESH)
                cp.wait_send()
                cp.wait_recv()

    res = pl.pallas_call(
        body, name=name,
        out_shape=(*[pltpu.HBM(a.shape, a.dtype) for a in arrays], *[pltpu.HBM(a.shape, a.dtype) for a in lands]),
        in_specs=[_HBM] * (2 * n) + [_SEM] * (2 * ns) + [pl.BlockSpec(memory_space=pl.ANY)],
        out_specs=tuple([_HBM] * (2 * n)), input_output_aliases={i: i for i in range(2 * n)},
        compiler_params=pltpu.CompilerParams(has_side_effects=pltpu.SideEffectType.DATAFLOW_SIDE_EFFECTING),
    )(*arrays, *lands, *send_sems, *recv_sems, after)
    return res[n:]


def _gather_forward(name, lands):
    n = len(lands)

    def body(*refs):
        outs = refs[n:2 * n]
        send_sems, recv_sems, _ = refs[2 * n:]
        me = _mesh_pos()
        sib = _flip(me, 1)
        held = [_flip(me, 4), _flip(me, 2), _flip(me, 6), sib]

        def copy(w, j, block):
            slab = outs[w].at[_index(block)]
            return pltpu.make_async_remote_copy(src_ref=slab, dst_ref=slab, send_sem=send_sems.at[w, j],
                                                recv_sem=recv_sems.at[w, j], device_id=sib, device_id_type=MESH)

        sends = [copy(w, j, blk) for j, blk in enumerate(held) for w in range(n)]
        for cp in sends:
            cp.start()
        for j, blk in enumerate(held):
            for w in range(n):
                copy(w, j, (blk[0], blk[1], 1 - blk[2])).wait_recv()
        for cp in sends:
            cp.wait_send()

    hbm = pl.BlockSpec(memory_space=pl.ANY)
    return pl.pallas_call(
        body, name=name, in_specs=[hbm] * n, out_specs=[hbm] * n,
        out_shape=[jax.ShapeDtypeStruct(a.shape, a.dtype) for a in lands],
        input_output_aliases={i: i for i in range(n)},
        scratch_shapes=[pltpu.SemaphoreType.DMA((n, 4)), pltpu.SemaphoreType.DMA((n, 4)), pltpu.SemaphoreType.DMA((n, 4))],
    )(*lands)


def _scatter_pair(name, arrays):
    n = len(arrays)

    def body(*refs):
        ins, outs = refs[:n], refs[n:2 * n]
        send_sems, recv_sems, _ = refs[2 * n:]
        me = _mesh_pos()
        c = me[2]
        sib = _flip(me, 1)

        def copy(w, q):
            return pltpu.make_async_remote_copy(
                src_ref=ins[w].at[2 * q + 1 - c], dst_ref=outs[w].at[q], send_sem=send_sems.at[w, q],
                recv_sem=recv_sems.at[w, q], device_id=sib, device_id_type=MESH)

        cps = [copy(w, q) for q in range(4) for w in range(n)]
        for cp in cps:
            cp.start()
        for cp in cps:
            cp.wait()

    out_shape = [jax.ShapeDtypeStruct((4,) + a.shape[1:], a.dtype) for a in arrays]
    return _comm_call(name, body, arrays, out_shape, 4)


def _pair_add(name, full, theirs, core, dtype):
    _, R, C = theirs.shape
    tr = R if R <= 256 else (256 if C <= 512 else 128)

    def body(core_ref, mine_ref, theirs_ref, o_ref):
        o_ref[...] = (mine_ref[...] + theirs_ref[...]).astype(o_ref.dtype)

    blk = pl.BlockSpec((4, tr, C), lambda r, cr: (0, r, 0))
    grid_spec = pltpu.PrefetchScalarGridSpec(
        num_scalar_prefetch=1, grid=(R // tr,),
        in_specs=[pl.BlockSpec((4, None, tr, C), lambda r, cr: (0, cr[0], r, 0)), blk], out_specs=blk)
    return pl.pallas_call(
        body, name=name, grid_spec=grid_spec, out_shape=jax.ShapeDtypeStruct((4, R, C), dtype),
        compiler_params=_params(1),
    )(core.reshape(1).astype(jnp.int32), full.reshape(4, 2, R, C), theirs)


def _scatter_chips(name, arrays):
    n = len(arrays)

    def body(*refs):
        ins, outs = refs[:n], refs[n:2 * n]
        send_sems, recv_sems, loc_sems = refs[2 * n:]
        me = _mesh_pos()
        mq = 2 * me[0] + me[1]
        peers = [_flip(me, 4), _flip(me, 2), _flip(me, 6)]

        def copy(w, k):
            peer = peers[k]
            return pltpu.make_async_remote_copy(
                src_ref=ins[w].at[2 * peer[0] + peer[1]], dst_ref=outs[w].at[mq], send_sem=send_sems.at[w, k],
                recv_sem=recv_sems.at[w, k], device_id=peer, device_id_type=MESH)

        def arrival(w, k):
            peer = peers[k]
            return pltpu.make_async_remote_copy(
                src_ref=ins[w].at[mq], dst_ref=outs[w].at[2 * peer[0] + peer[1]], send_sem=send_sems.at[w, k],
                recv_sem=recv_sems.at[w, k], device_id=peer, device_id_type=MESH)

        local = [pltpu.make_async_copy(ins[w].at[mq], outs[w].at[mq], loc_sems.at[w, 0]) for w in range(n)]
        for cp in local:
            cp.start()
        sends = [copy(w, k) for k in range(3) for w in range(n)]
        for cp in sends:
            cp.start()
        for k in range(3):
            for w in range(n):
                arrival(w, k).wait_recv()
        for cp in sends:
            cp.wait_send()
        for cp in local:
            cp.wait()

    out_shape = [jax.ShapeDtypeStruct(a.shape, a.dtype) for a in arrays]
    return _comm_call(name, body, arrays, out_shape, 3)


def _adamw(name, parts, w, m, v):
    R, C = w.shape
    n_parts = parts.shape[0]
    tr = R if R <= 256 else (256 if C <= 512 else 128)
    bc1 = 1.0 - ADAM_B1 ** ADAM_STEP
    bc2 = 1.0 - ADAM_B2 ** ADAM_STEP

    def f(p, wv, mv, vv):
        g = p[0].astype(f32)
        for i in range(1, n_parts):
            g = g + p[i].astype(f32)
        m2 = ADAM_B1 * mv + (1.0 - ADAM_B1) * g
        v2 = ADAM_B2 * vv + (1.0 - ADAM_B2) * jnp.square(g)
        delta = -ADAM_LR * ((m2 / bc1) / (jnp.sqrt(v2 / bc2) + ADAM_EPS) + ADAM_WD * wv)
        return g, delta, m2, v2

    blk = ((tr, C), lambda r, z: (r, 0))
    return _ew(name, f, [(parts, (n_parts, tr, C), lambda r, z: (0, r, 0)), (w, *blk), (m, *blk), (v, *blk)],
               [((R, C), f32, *blk, None)] * 4, (R // tr, 1))


_WEIGHTS = ["norm_mix_g", "norm_mlp_g", "mlp_w1", "mlp_w2", "gdn_w_in", "gdn_conv_w", "gdn_a_log", "gdn_dt_bias",
            "gdn_o_norm_g", "gdn_w_out", "s5_w_in", "s5_lam_re", "s5_lam_im", "s5_log_dt", "s5_b_re", "s5_b_im",
            "s5_c_re", "s5_c_im", "s5_d", "s5_w_out", "m2_w_in", "m2_conv_w", "m2_conv_b", "m2_dt_bias", "m2_a_log",
            "m2_d", "m2_norm_g", "m2_w_out", "final_norm_g"]
_SHARDED = ["mlp_w1", "mlp_w2", "gdn_w_in", "gdn_w_out", "s5_w_in", "s5_w_out", "m2_w_in", "m2_w_out",
            "gdn_conv_w", "m2_conv_w", "m2_conv_b", "m2_norm_g"]
_MATRICES = _SHARDED[:8]
_REPLICATED = [n for n in _WEIGHTS if n not in _SHARDED]
_GDN_IN, _M2_IN = 4112, 6176
_LAYER_KIND = (0, 1, 2, 0)


def _as2d(a):
    return a.reshape(-1, a.shape[-1])


def _cols_from_shards(g, width):
    return g.transpose(1, 0, 2).reshape(g.shape[1], width)


def _cols_to_shards(a, width):
    return a[:, :width].reshape(a.shape[0], N_DEV, width // N_DEV).transpose(1, 0, 2)


def kernel(x, norm_mix_g, norm_mlp_g, mlp_w1, mlp_w2, gdn_w_in, gdn_conv_w, gdn_a_log, gdn_dt_bias, gdn_o_norm_g, gdn_w_out, s5_w_in, s5_lam_re, s5_lam_im, s5_log_dt, s5_b_re, s5_b_im, s5_c_re, s5_c_im, s5_d, s5_w_out, m2_w_in, m2_conv_w, m2_conv_b, m2_dt_bias, m2_a_log, m2_d, m2_norm_g, m2_w_out, final_norm_g, loss_target, m_norm_mix_g, m_norm_mlp_g, m_mlp_w1, m_mlp_w2, m_gdn_w_in, m_gdn_conv_w, m_gdn_a_log, m_gdn_dt_bias, m_gdn_o_norm_g, m_gdn_w_out, m_s5_w_in, m_s5_lam_re, m_s5_lam_im, m_s5_log_dt, m_s5_b_re, m_s5_b_im, m_s5_c_re, m_s5_c_im, m_s5_d, m_s5_w_out, m_m2_w_in, m_m2_conv_w, m_m2_conv_b, m_m2_dt_bias, m_m2_a_log, m_m2_d, m_m2_norm_g, m_m2_w_out, m_final_norm_g, v_norm_mix_g, v_norm_mlp_g, v_mlp_w1, v_mlp_w2, v_gdn_w_in, v_gdn_conv_w, v_gdn_a_log, v_gdn_dt_bias, v_gdn_o_norm_g, v_gdn_w_out, v_s5_w_in, v_s5_lam_re, v_s5_lam_im, v_s5_log_dt, v_s5_b_re, v_s5_b_im, v_s5_c_re, v_s5_c_im, v_s5_d, v_s5_w_out, v_m2_w_in, v_m2_conv_w, v_m2_conv_b, v_m2_dt_bias, v_m2_a_log, v_m2_d, v_m2_norm_g, v_m2_w_out, v_final_norm_g):
    args = locals()
    W = {n: args[n] for n in _WEIGHTS}
    MOM = {n: args["m_" + n] for n in _WEIGHTS}
    VAR = {n: args["v_" + n] for n in _WEIGHTS}
    h = x[0]
    target = loss_target[0]
    L, D = h.shape

    first = [mlp_w1[0:1].astype(bf16), mlp_w2[0:1].astype(bf16), gdn_w_in[0:1].astype(bf16),
             gdn_w_out[0:1].astype(bf16), _as2d(gdn_conv_w), _as2d(m2_conv_w), _as2d(m2_conv_b), _as2d(m2_norm_g)]
    w1g0, w2g0, gin0, gout0, gconv, m2_cw, m2_cbg, m2_ngg = _gather("gather_first", first)
    stacked = jnp.concatenate([gdn_w_out[1], s5_w_in[0], m2_w_out[0]], axis=0).astype(bf16)
    rest = [mlp_w1[1:4].astype(bf16), mlp_w2[1:4].astype(bf16), gdn_w_in[1:2].astype(bf16), s5_w_out.astype(bf16),
            m2_w_in.astype(bf16), stacked]
    lands = [lax.empty((N_DEV,) + a.shape, a.dtype) for a in rest]
    send_sems, recv_sems, rest_thru, lands_thru, token = _gather_start("gather_rest_start", rest, lands)

    def gdn_weights(gin, gout, conv, j):
        return (jnp.pad(_cols_from_shards(gin[:, 0], _GDN_IN), ((0, 0), (0, GDN_EXT - _GDN_IN))),
                gout[:, 0].reshape(D, D), _cols_from_shards(conv[:, 4 * j:4 * j + 4], 3 * D))

    gdn_in, gdn_out, gdn_conv = [None, None], [None, None], [None, None]
    gdn_in[0], gdn_out[0], gdn_conv[0] = gdn_weights(gin0, gout0, gconv, 0)

    norm_mix = [norm_mix_g[i].reshape(1, D) for i in range(4)]
    norm_mix[0] = norm_mix[0] + token[0, 0]
    late = {}

    def mixer_fwd(i, hv):
        kind, j = _LAYER_KIND[i], i // 3
        gn = norm_mix[i]
        s5_in, s5_out_g = late.get("s5_in"), late.get("s5_out_g")
        m2_in, m2_conv, m2_cb, m2_ng, m2_out = (late.get(k) for k in ("m2_in", "m2_conv", "m2_cb", "m2_ng", "m2_out"))
        if kind == 0:
            return _gdn_fwd(hv, gn, gdn_in[j], gdn_conv[j], gdn_a_log[j], gdn_dt_bias[j], gdn_o_norm_g[j], gdn_out[j])
        if kind == 1:
            return _s5_fwd(hv, gn, s5_in, s5_lam_re[0], s5_lam_im[0], s5_log_dt[0], s5_b_re[0], s5_b_im[0],
                           s5_c_re[0], s5_c_im[0], s5_d[0], s5_out_g)
        return _m2_fwd(hv, gn, m2_in, m2_conv, m2_cb, m2_dt_bias[0], m2_a_log[0], m2_d[0], m2_ng, m2_out)

    def mixer_bwd(i, dh, hv, sv):
        kind, j = _LAYER_KIND[i], i // 3
        gn = norm_mix[i]
        s5_in, s5_out_g = late["s5_in"], late["s5_out_g"]
        m2_in, m2_conv, m2_cb, m2_out = (late[k] for k in ("m2_in", "m2_conv", "m2_cb", "m2_out"))
        if kind == 0:
            return _gdn_bwd(dh, hv, gn, gdn_in[j], gdn_conv[j], gdn_out[j], sv)
        if kind == 1:
            return _s5_bwd(dh, hv, gn, s5_in, s5_out_g, sv)
        return _m2_bwd(dh, hv, gn, m2_in, m2_conv, m2_cb, m2_out, sv)

    tape = []
    mlp_w = [(w1g0, w2g0, 0)]
    for i in range(4):
        if i == 1:
            landed = _gather_wait("gather_rest_wait", rest_thru, lands_thru, send_sems, recv_sems, h)
            w1gr, w2gr, gin1, s5_out_g, m2_in_g, rows_g = _gather_forward("gather_rest_forward", landed)
            gout1, s5_in_g, m2_out_g = rows_g[:, None, 0:128], rows_g[:, 128:256], rows_g[:, 256:512]
            mlp_w += [(w1gr, w2gr, k) for k in range(3)]
            gdn_in[1], gdn_out[1], gdn_conv[1] = gdn_weights(gin1, gout1, gconv, 1)
            late.update(
                s5_in=s5_in_g.reshape(D, D), s5_out_g=s5_out_g,
                m2_in=jnp.pad(_cols_from_shards(m2_in_g[:, 0], _M2_IN), ((0, 0), (0, M2_EXT - _M2_IN))),
                m2_out=m2_out_g.reshape(M2_INNER, D), m2_conv=_cols_from_shards(m2_cw, 2 * M2_INNER),
                m2_cb=_cols_from_shards(m2_cbg, 2 * M2_INNER), m2_ng=_cols_from_shards(m2_ngg, M2_INNER))
        h_mid, sv = mixer_fwd(i, h)
        h_next, hn, h1 = _mlp_fwd(h_mid, norm_mlp_g[i].reshape(1, D), *mlp_w[i])
        tape.append((h, sv, h_mid, hn, h1))
        h = h_next
    loss_row, dh, d_final = _loss_head(h, final_norm_g.reshape(1, D), target)
    loss = lax.psum(loss_row[0, 0], ("x", "y", "c"))

    dw1 = lax.empty((N_DEV, 4, D, D_FF // N_DEV), f32)
    dw2 = lax.empty((N_DEV, 4, D_FF // N_DEV, D), f32)
    d_mix, d_mlp, mg = [None] * 4, [None] * 4, [None] * 4
    for i in reversed(range(4)):
        h_in, sv, h_mid, hn, h1 = tape[i]
        dh, d_mlp[i], dw1, dw2 = _mlp_bwd(dh, h_mid, norm_mlp_g[i].reshape(1, D), hn, h1, *mlp_w[i], i, dw1, dw2)
        dh, mg[i] = mixer_bwd(i, dh, h_in, sv)
        d_mix[i] = mg[i]["norm"]
    grad_x = dh.reshape(1, L, D)
    ga, gb_, s5g, m2g = mg[0], mg[3], mg[1], mg[2]

    full = {
        "mlp_w1": dw1, "mlp_w2": dw2,
        "gdn_w_in": jnp.stack([_cols_to_shards(g["w_ext"], _GDN_IN) for g in (ga, gb_)], axis=1),
        "gdn_w_out": jnp.stack([g["w_out"].reshape(N_DEV, D // N_DEV, D) for g in (ga, gb_)], axis=1),
        "s5_w_in": s5g["w_in"].reshape(N_DEV, 1, D // N_DEV, D), "s5_w_out": s5g["w_out"],
        "m2_w_in": _cols_to_shards(m2g["w_ext"], _M2_IN)[:, None],
        "m2_w_out": m2g["w_out"].reshape(N_DEV, 1, M2_INNER // N_DEV, D),
        "gdn_conv_w": jnp.concatenate([_cols_to_shards(g["conv_w"], 3 * D) for g in (ga, gb_)], axis=1),
        "m2_conv_w": _cols_to_shards(m2g["conv_w"], 2 * M2_INNER),
        "m2_conv_b": _cols_to_shards(m2g["conv_b"], 2 * M2_INNER),
        "m2_norm_g": _cols_to_shards(m2g["norm_g"], M2_INNER),
    }
    sends = [full[n].reshape((N_DEV,) + _as2d(W[n]).shape) for n in _SHARDED]
    core = lax.axis_index("c")
    theirs = _scatter_pair("scatter_pair", sends)
    chip_sums = [_pair_add("pair_add_" + n, full8, th, core, bf16 if n in _MATRICES else f32)
                 for n, full8, th in zip(_SHARDED, sends, theirs)]
    parts = dict(zip(_SHARDED, _scatter_chips("scatter_chips", chip_sums)))

    rep = {
        "norm_mix_g": jnp.concatenate(d_mix, axis=0), "norm_mlp_g": jnp.concatenate(d_mlp, axis=0),
        "gdn_a_log": jnp.stack([ga["a_log"], gb_["a_log"]]), "gdn_dt_bias": jnp.stack([ga["dt_bias"], gb_["dt_bias"]]),
        "gdn_o_norm_g": jnp.stack([ga["o_norm_g"], gb_["o_norm_g"]]),
        "s5_lam_re": s5g["lam_re"], "s5_lam_im": s5g["lam_im"], "s5_log_dt": s5g["log_dt"], "s5_b_re": s5g["b_re"],
        "s5_b_im": s5g["b_im"], "s5_c_re": s5g["c_re"], "s5_c_im": s5g["c_im"], "s5_d": s5g["d"],
        "m2_dt_bias": m2g["dt_bias"], "m2_a_log": m2g["a_log"], "m2_d": m2g["d"], "final_norm_g": d_final,
    }

    def pack(d):
        flat = jnp.concatenate([d[n].reshape(-1).astype(f32) for n in _REPLICATED])
        return jnp.pad(flat, (0, -flat.shape[0] % (256 * 128))).reshape(-1, 128)

    (rep_parts,) = _gather("gather_small_grads", [pack(rep)])

    res = {}
    for n in _SHARDED:
        w2d = _as2d(W[n])
        out = _adamw("adamw_" + n, parts[n], w2d, _as2d(MOM[n]), _as2d(VAR[n]))
        res[n] = [o.reshape(W[n].shape) for o in out]
    out = _adamw("adamw_replicated", rep_parts, pack(W), pack(MOM), pack(VAR))
    off = 0
    for n in _REPLICATED:
        size = W[n].size
        res[n] = [o.reshape(-1)[off:off + size].reshape(W[n].shape) for o in out]
        off += size

    return (loss, grad_x, *[res[n][0] for n in _WEIGHTS], *[res[n][1] for n in _WEIGHTS],
            *[res[n][2] for n in _WEIGHTS], *[res[n][3] for n in _WEIGHTS])
```

```python
import functools

import jax
import jax.numpy as jnp
from jax import lax
from jax.experimental import pallas as pl
from jax.experimental.pallas import tpu as pltpu

f32 = jnp.float32
bf16 = jnp.bfloat16
HI = lax.Precision.HIGHEST
MESH = pl.DeviceIdType.MESH

N_DEV = 8
D_MODEL = 1024
D_FF = 4096
CHUNK = 64
RMS_EPS = 1e-6
GDN_HEADS = 8
GDN_HB = 8
GDN_EXT = 4224
S5_STATE = 64
S5_SCAN_LANES = 256
M2_INNER = 2048
M2_EXT = 6272
M2_HEADS = 32
M2_GB = 4
VMEM_LIMIT_BYTES = 56 * 1024 * 1024

ADAM_LR, ADAM_B1, ADAM_B2, ADAM_EPS, ADAM_WD, ADAM_STEP = 0.001, 0.9, 0.999, 1e-08, 0.01, 10

_NN = ((1,), (0,))
_NT = ((1,), (1,))
_TN = ((0,), (0,))


def _dot(a, b, dims=_NN):
    return lax.dot_general(a, b, (dims, ((), ())), precision=HI, preferred_element_type=f32)


def _dotb(a, b, dims=_NN):
    return lax.dot_general(a.astype(bf16), b.astype(bf16), (dims, ((), ())), preferred_element_type=f32)


def _bdot(p, q, dims):
    return lax.dot_general(p, q, (dims, ((), ())), preferred_element_type=f32)


def _pieces(x, n):
    out = []
    for _ in range(n - 1):
        p = x.astype(bf16)
        out.append(p)
        x = x - p.astype(f32)
    return out + [x.astype(bf16)]


def _dot01_raw(mask, b, dims=_NN, mask_first=True):
    m = mask.astype(bf16)
    p = _pieces(b, 3)
    if mask_first:
        return _bdot(m, p[0], dims) + (_bdot(m, p[1], dims) + _bdot(m, p[2], dims))
    return _bdot(p[0], m, dims) + (_bdot(p[1], m, dims) + _bdot(p[2], m, dims))


@jax.custom_vjp
def _dot01_nn(mask, b):
    return _dot01_raw(mask, b, _NN)


@jax.custom_vjp
def _dot01_nt(mask, b):
    return _dot01_raw(mask, b, _NT)


_dot01_nn.defvjp(lambda m, b: (_dot01_raw(m, b, _NN), m),
                 lambda m, ct: (jnp.zeros_like(m), _dot01_raw(m, ct, _TN, mask_first=True)))
_dot01_nt.defvjp(lambda m, b: (_dot01_raw(m, b, _NT), m),
                 lambda m, ct: (jnp.zeros_like(m), _dot01_raw(m, ct, _TN, mask_first=False)))


def _dot01_vjp(mask, b, dims=_NN):
    return _dot01_nn(mask, b) if dims == _NN else _dot01_nt(mask, b)


def _dot3_raw(a, b, dims=_NN):
    (ah, al), (bh, bl) = _pieces(a, 2), _pieces(b, 2)
    return _bdot(ah, bh, dims) + (_bdot(ah, bl, dims) + _bdot(al, bh, dims))


@jax.custom_vjp
def _dot3_vjp(a, b):
    return _dot3_raw(a, b)


_dot3_vjp.defvjp(lambda a, b: (_dot3_raw(a, b), (a, b)),
                 lambda res, ct: (_dot3_raw(ct, res[1], _NT), _dot3_raw(res[0], ct, _TN)))


class _Dots:
    def __init__(self, dot3, dot01):
        self.dot3, self.dot01 = dot3, dot01


_PLAIN_DOTS = _Dots(_dot3_raw, _dot01_raw)
_VJP_DOTS = _Dots(_dot3_vjp, _dot01_vjp)


def _iota(shape, dim):
    return lax.broadcasted_iota(jnp.int32, shape, dim)


def _params(n_grid):
    return pltpu.CompilerParams(dimension_semantics=("arbitrary",) * n_grid, vmem_limit_bytes=VMEM_LIMIT_BYTES)


def _row_tile(n_rows):
    return min(512, n_rows)


def _head_rows(n_rows):
    return min(2048, n_rows)


def _mm_rows(n_rows):
    return min(1024, n_rows)


def _col_tile(n, cap=1024):
    best = 128
    for t in range(128, cap + 1, 128):
        if n % t == 0:
            best = t
    return best


def _mm(name, a, b, *, dims, grid, a_spec, b_spec, out_shape, out_spec, aux=(), a_fn=None, epi_fn=None,
        acc_shape, out_init=None, cache_a=False):
    nk = grid[2]
    n_aux = len(aux)
    kinds = [x[2] for x in aux]
    cache_a = cache_a and nk == 1 and grid[1] > 1 and not any(kd == "a" for kd in kinds)

    def body_single(*refs):
        a_ref, b_ref = refs[0], refs[1]
        aux_refs = refs[2:2 + n_aux]
        pos = 2 + n_aux + (1 if out_init is not None else 0)
        o_ref = refs[pos]

        def a_tile():
            av = a_ref[...]
            if a_fn is not None:
                av = a_fn(av, *[r[...] for r, kd in zip(aux_refs, kinds) if kd == "a"])
            return av.astype(bf16)

        if cache_a:
            a_bf = refs[pos + 1]

            @pl.when(pl.program_id(1) == 0)
            def _():
                a_bf[...] = a_tile()

            av = a_bf[...]
        else:
            av = a_tile()
        r = lax.dot_general(av, b_ref[...].astype(bf16), (dims, ((), ())), preferred_element_type=f32)
        if epi_fn is not None:
            r = epi_fn(r, *[x[...] for x, kd in zip(aux_refs, kinds) if kd == "e"])
        o_ref[...] = r.astype(o_ref.dtype)

    def body(*refs):
        a_ref, b_ref = refs[0], refs[1]
        aux_refs = refs[2:2 + n_aux]
        pos = 2 + n_aux + (1 if out_init is not None else 0)
        o_ref, acc_ref = refs[pos], refs[pos + 1]
        k = pl.program_id(2)

        @pl.when(k == 0)
        def _():
            acc_ref[...] = jnp.zeros_like(acc_ref)

        av = a_ref[...]
        if a_fn is not None:
            av = a_fn(av, *[r[...] for r, kd in zip(aux_refs, kinds) if kd == "a"])
        acc_ref[...] += lax.dot_general(av.astype(bf16), b_ref[...].astype(bf16), (dims, ((), ())),
                                        preferred_element_type=f32)

        @pl.when(k == nk - 1)
        def _():
            r = acc_ref[...]
            if epi_fn is not None:
                r = epi_fn(r, *[x[...] for x, kd in zip(aux_refs, kinds) if kd == "e"])
            o_ref[...] = r.astype(o_ref.dtype)

    in_specs = [a_spec, b_spec] + [x[1] for x in aux]
    args = [a, b] + [x[0] for x in aux]
    aliases = {}
    if out_init is not None:
        in_specs.append(pl.BlockSpec(memory_space=pl.ANY))
        args.append(out_init)
        aliases = {len(args) - 1: 0}
    if nk == 1:
        a_block = tuple(d for d in a_spec.block_shape if d is not None)
        scratch = [pltpu.VMEM(a_block, bf16)] if cache_a else []
    else:
        scratch = [pltpu.VMEM(acc_shape, f32)]
    return pl.pallas_call(
        body_single if nk == 1 else body, name=name, grid=grid, in_specs=in_specs, out_specs=out_spec,
        out_shape=out_shape, scratch_shapes=scratch, input_output_aliases=aliases, compiler_params=_params(3),
    )(*args)


def _ew(name, f, ins, outs, grid):
    n_in = len(ins)
    modes = [o[4] for o in outs]

    def body(*refs):
        vals = [r[...] for r in refs[:n_in]]
        res = f(*vals)
        if not isinstance(res, (tuple, list)):
            res = (res,)
        for r, o_ref, mode in zip(res, refs[n_in:], modes):
            if mode is None:
                o_ref[...] = r.astype(o_ref.dtype)
                continue
            first = pl.program_id(1) == 0
            if mode == "all":
                first = jnp.logical_and(first, pl.program_id(0) == 0)

            @pl.when(first)
            def _(r=r, o_ref=o_ref):
                o_ref[...] = r.astype(o_ref.dtype)

            @pl.when(jnp.logical_not(first))
            def _(r=r, o_ref=o_ref):
                o_ref[...] += r.astype(o_ref.dtype)

    res = pl.pallas_call(
        body, name=name, grid=grid,
        in_specs=[pl.BlockSpec(blk, im) for _, blk, im in ins],
        out_specs=[pl.BlockSpec(o[2], o[3]) for o in outs],
        out_shape=[jax.ShapeDtypeStruct(o[0], o[1]) for o in outs],
        compiler_params=_params(2),
    )(*[a for a, _, _ in ins])
    return res


def _vjp_fn(f, n_primal):
    def g(*args):
        _, vjp = jax.vjp(f, *args[:n_primal])
        cts = args[n_primal:]
        return vjp(cts[0] if len(cts) == 1 else tuple(cts))
    return g


def _scan_fwd(name, step, n_state, state_shape, cins, ins, outs, n_units, n_chunks):
    n_c, n_in, n_out = len(cins), len(ins), len(outs)

    def body(*refs):
        c_refs = refs[:n_c]
        in_refs = refs[n_c:n_c + n_in]
        out_refs = refs[n_c + n_in:n_c + n_in + n_out]
        saved = refs[n_c + n_in + n_out:n_c + n_in + n_out + n_state]
        st = refs[n_c + n_in + n_out + n_state:]

        @pl.when(pl.program_id(1) == 0)
        def _():
            for s in st:
                s[...] = jnp.zeros_like(s)

        cur = [s[...] for s in st]
        for sv, s in zip(saved, cur):
            sv[...] = s
        new, res = step(cur, [r[...] for r in c_refs], [r[...] for r in in_refs], _PLAIN_DOTS)
        for s, n in zip(st, new):
            s[...] = n
        for o, r in zip(out_refs, res):
            o[...] = r

    sshape = (n_units, n_chunks) + state_shape
    sblock = (None, None) + state_shape
    nz = len(state_shape)
    res = pl.pallas_call(
        body, name=name, grid=(n_units, n_chunks),
        in_specs=[pl.BlockSpec(e[1], e[2]) for e in cins + ins],
        out_specs=[pl.BlockSpec(o[1], o[2]) for o in outs]
        + [pl.BlockSpec(sblock, lambda u, c: (u, c) + (0,) * nz)] * n_state,
        out_shape=[jax.ShapeDtypeStruct(o[0], f32) for o in outs]
        + [jax.ShapeDtypeStruct(sshape, f32)] * n_state,
        scratch_shapes=[pltpu.VMEM(state_shape, f32)] * n_state,
        compiler_params=_params(2),
    )(*[e[0] for e in cins + ins])
    return res[:n_out], res[n_out:]


def _scan_bwd(name, step, n_state, state_shape, cins, ins, saved, douts, n_units, n_chunks):
    n_c, n_in, n_do = len(cins), len(ins), len(douts)

    def flip(im):
        return lambda u, c: im(u, n_chunks - 1 - c)

    def body(*refs):
        p = 0
        c_refs = refs[p:p + n_c]; p += n_c
        in_refs = refs[p:p + n_in]; p += n_in
        sv_refs = refs[p:p + n_state]; p += n_state
        do_refs = refs[p:p + n_do]; p += n_do
        dc_refs = refs[p:p + n_c]; p += n_c
        di_refs = refs[p:p + n_in]; p += n_in
        dst = refs[p:]
        first = pl.program_id(1) == 0

        @pl.when(first)
        def _():
            for s in dst:
                s[...] = jnp.zeros_like(s)

        def fn(states, consts, vals):
            new, res = step(states, consts, vals, _VJP_DOTS)
            return tuple(new), tuple(res)

        prim = ([r[...] for r in sv_refs], [r[...] for r in c_refs], [r[...] for r in in_refs])
        _, vjp = jax.vjp(fn, *prim)
        d_states, d_consts, d_vals = vjp((tuple(s[...] for s in dst), tuple(r[...] for r in do_refs)))
        for s, g in zip(dst, d_states):
            s[...] = g
        for o, g in zip(di_refs, d_vals):
            o[...] = g
        for o, g in zip(dc_refs, d_consts):
            @pl.when(first)
            def _(o=o, g=g):
                o[...] = g

            @pl.when(jnp.logical_not(first))
            def _(o=o, g=g):
                o[...] += g

    nz = len(state_shape)
    sblock = (None, None) + state_shape
    def gshape(e):
        return e[3] if len(e) == 5 else e[0].shape

    def gmap(e):
        return e[4] if len(e) == 5 else e[2]

    in_specs = ([pl.BlockSpec(e[1], e[2]) for e in cins]
                + [pl.BlockSpec(e[1], flip(e[2])) for e in ins]
                + [pl.BlockSpec(sblock, lambda u, c: (u, n_chunks - 1 - c) + (0,) * nz)] * n_state
                + [pl.BlockSpec(e[1], flip(e[2])) for e in douts])
    out_specs = ([pl.BlockSpec(e[1], e[2]) for e in cins]
                 + [pl.BlockSpec(e[1], flip(gmap(e))) for e in ins])
    out_shape = [jax.ShapeDtypeStruct(gshape(e), f32) for e in cins + ins]
    res = pl.pallas_call(
        body, name=name, grid=(n_units, n_chunks), in_specs=in_specs, out_specs=out_specs, out_shape=out_shape,
        scratch_shapes=[pltpu.VMEM(state_shape, f32)] * n_state,
        compiler_params=_params(2),
    )(*([e[0] for e in cins + ins] + list(saved) + [e[0] for e in douts]))
    return res[:n_c], res[n_c:]


def _rms(x, g):
    return x * lax.rsqrt(jnp.mean(x * x, axis=-1, keepdims=True) + RMS_EPS) * g


def _rows(tm, width):
    return (tm, width), lambda r, z: (r, 0)


def _const(shape):
    return shape, lambda r, z: (0,) * len(shape)


def _rms_fwd(name, h, g):
    L, D = h.shape
    tm = _row_tile(L)
    return _ew(name, _rms, [(h, *_rows(tm, D)), (g, *_const((1, D)))],
               [((L, D), f32, *_rows(tm, D), None)], (L // tm, 1))[0]


def _rms_bwd(name, h, g, d_hn, d_res):
    L, D = h.shape
    tm = _row_tile(L)

    def f(hv, gv, dv, rv):
        dh, dg = _vjp_fn(_rms, 2)(hv, gv, dv)
        return dh + rv, dg

    return _ew(name, f, [(h, *_rows(tm, D)), (g, *_const((1, D))), (d_hn, *_rows(tm, D)), (d_res, *_rows(tm, D))],
               [((L, D), f32, *_rows(tm, D), None), ((1, D), f32, *_const((1, D)), "all")], (L // tm, 1))


def _loss_head(h, g, target):
    L, D = h.shape
    tm = _row_tile(L)

    def f(hv, gv, tv):
        def lf(a, b):
            e = jnp.square(_rms(a, b) - tv)
            return (0.5 / D) * jnp.sum(jnp.sum(e, axis=1, keepdims=True), axis=0, keepdims=True)

        val, vjp = jax.vjp(lf, hv, gv)
        dh, dg = vjp(jnp.ones((1, 1), f32))
        return jnp.broadcast_to(val, (1, 128)), dh, dg

    return _ew("loss_head", f, [(h, *_rows(tm, D)), (g, *_const((1, D))), (target, *_rows(tm, D))],
               [((1, 128), f32, *_const((1, 128)), "all"), ((L, D), f32, *_rows(tm, D), None),
                ((1, D), f32, *_const((1, D)), "all")], (L // tm, 1))


def _sqrelu(x):
    return jnp.square(jnp.maximum(x, 0.0))


def _mm_plain(name, a, b, dims, *, a_fn=None, epi_fn=None, aux=()):
    if dims == _NN:
        (M, K), N = a.shape, b.shape[1]
    elif dims == _NT:
        (M, K), N = a.shape, b.shape[0]
    else:
        (K, M), N = a.shape, b.shape[1]
    tm = _mm_rows(M)
    tn = _col_tile(N, 1536)
    tk = _col_tile(K)
    if dims == _TN:
        tk = min(512, K)
        a_spec = pl.BlockSpec((tk, tm), lambda i, j, k: (k, i))
        b_spec = pl.BlockSpec((tk, tn), lambda i, j, k: (k, j))
        a_aux = pl.BlockSpec((tk, tm), lambda i, j, k: (k, i))
    elif dims == _NT:
        a_spec = pl.BlockSpec((tm, tk), lambda i, j, k: (i, k))
        b_spec = pl.BlockSpec((tn, tk), lambda i, j, k: (j, k))
        a_aux = pl.BlockSpec((tm, tk), lambda i, j, k: (i, k))
    else:
        a_spec = pl.BlockSpec((tm, tk), lambda i, j, k: (i, k))
        b_spec = pl.BlockSpec((tk, tn), lambda i, j, k: (k, j))
        a_aux = pl.BlockSpec((tm, tk), lambda i, j, k: (i, k))
    e_aux = pl.BlockSpec((tm, tn), lambda i, j, k: (i, j))
    aux_full = [(x, a_aux if kd == "a" else e_aux, kd) for x, kd in aux]
    return _mm(name, a, b, dims=dims, grid=(M // tm, N // tn, K // tk), a_spec=a_spec, b_spec=b_spec,
               out_shape=jax.ShapeDtypeStruct((M, N), f32), out_spec=pl.BlockSpec((tm, tn), lambda i, j, k: (i, j)),
               aux=aux_full, a_fn=a_fn, epi_fn=epi_fn, acc_shape=(tm, tn), cache_a=True)


def _mlp_fwd(h, g, w1g, w2g, layer):
    L, D = h.shape
    tm = _mm_rows(L)
    fs = D_FF // N_DEV
    hn = _rms_fwd("mlp_norm", h, g)
    h1 = _mm("mlp_up", hn, w1g, dims=_NN, grid=(L // tm, N_DEV, 1),
             a_spec=pl.BlockSpec((tm, D), lambda i, j, k: (i, 0)),
             b_spec=pl.BlockSpec((None, None, D, fs), lambda i, j, k: (j, layer, 0, 0)),
             out_shape=jax.ShapeDtypeStruct((L, D_FF), f32), out_spec=pl.BlockSpec((tm, fs), lambda i, j, k: (i, j)),
             acc_shape=(tm, fs), cache_a=True)
    tn = D
    h_out = _mm("mlp_down", h1, w2g, dims=_NN, grid=(L // tm, D // tn, N_DEV),
                a_spec=pl.BlockSpec((tm, fs), lambda i, j, k: (i, k)),
                b_spec=pl.BlockSpec((None, None, fs, tn), lambda i, j, k: (k, layer, 0, j)),
                out_shape=jax.ShapeDtypeStruct((L, D), f32), out_spec=pl.BlockSpec((tm, tn), lambda i, j, k: (i, j)),
                aux=[(h, pl.BlockSpec((tm, tn), lambda i, j, k: (i, j)), "e")],
                a_fn=_sqrelu, epi_fn=lambda acc, res: acc + res, acc_shape=(tm, tn))
    return h_out, hn, h1


def _mlp_bwd(dh, h, g, hn, h1, w1g, w2g, wl, layer, dw1_buf, dw2_buf):
    L, D = h.shape
    tm = _mm_rows(L)
    fs = D_FF // N_DEV
    tk = _mm_rows(L)
    dh1 = _mm("mlp_down_dx", dh, w2g, dims=_NT, grid=(L // tm, N_DEV, 1),
              a_spec=pl.BlockSpec((tm, D), lambda i, j, k: (i, 0)),
              b_spec=pl.BlockSpec((None, None, fs, D), lambda i, j, k: (j, wl, 0, 0)),
              out_shape=jax.ShapeDtypeStruct((L, D_FF), f32), out_spec=pl.BlockSpec((tm, fs), lambda i, j, k: (i, j)),
              aux=[(h1, pl.BlockSpec((tm, fs), lambda i, j, k: (i, j)), "e")],
              epi_fn=lambda acc, pre: acc * (2.0 * jnp.maximum(pre, 0.0)), acc_shape=(tm, fs), cache_a=True)
    dw2_buf = _mm("mlp_down_dw", h1, dh, dims=_TN, grid=(N_DEV, 1, L // tk),
                  a_spec=pl.BlockSpec((tk, fs), lambda i, j, k: (k, i)),
                  b_spec=pl.BlockSpec((tk, D), lambda i, j, k: (k, 0)),
                  out_shape=jax.ShapeDtypeStruct(dw2_buf.shape, f32),
                  out_spec=pl.BlockSpec((None, None, fs, D), lambda i, j, k: (i, layer, 0, 0)),
                  a_fn=_sqrelu, acc_shape=(fs, D), out_init=dw2_buf)
    tr = D
    dw1_buf = _mm("mlp_up_dw", hn, dh1, dims=_TN, grid=(D // tr, N_DEV, L // tk),
                  a_spec=pl.BlockSpec((tk, tr), lambda i, j, k: (k, i)),
                  b_spec=pl.BlockSpec((tk, fs), lambda i, j, k: (k, j)),
                  out_shape=jax.ShapeDtypeStruct(dw1_buf.shape, f32),
                  out_spec=pl.BlockSpec((None, None, tr, fs), lambda i, j, k: (j, layer, i, 0)),
                  acc_shape=(tr, fs), out_init=dw1_buf)
    tn = D
    dhn = _mm("mlp_up_dx", dh1, w1g, dims=_NT, grid=(L // tm, D // tn, N_DEV),
              a_spec=pl.BlockSpec((tm, fs), lambda i, j, k: (i, k)),
              b_spec=pl.BlockSpec((None, None, tn, fs), lambda i, j, k: (k, wl, j, 0)),
              out_shape=jax.ShapeDtypeStruct((L, D), f32), out_spec=pl.BlockSpec((tm, tn), lambda i, j, k: (i, j)),
              acc_shape=(tm, tn))
    dh_in, dg = _rms_bwd("mlp_norm_bwd", h, g, dhn, dh)
    return dh_in, dg, dw1_buf, dw2_buf


def _shift_dn(x, s, row):
    return x if s == 0 else jnp.where(row >= s, pltpu.roll(x, s, 0), 0.0)


def _shift_up(x, s, row):
    n = x.shape[0]
    return x if s == 0 else jnp.where(row < n - s, pltpu.roll(x, n - s, 0), 0.0)


def _conv_pre(x, w, b, row):
    c = jnp.broadcast_to(b, x.shape)
    for j in range(4):
        c = c + w[j:j + 1, :] * _shift_dn(x, 3 - j, row)
    return c


def _conv_fwd(name, x_arr, blk_off, w, b):
    L = x_arr.shape[0]
    C = w.shape[1]

    def f(x, wv, bv):
        c = _conv_pre(x, wv, bv, _iota(x.shape, 0))
        return c * jax.nn.sigmoid(c)

    return _ew(name, f, [(x_arr, (L, 128), lambda j, z: (0, blk_off + j)), (w, (4, 128), lambda j, z: (0, j)),
                         (b, (1, 128), lambda j, z: (0, j))],
               [((L, C), f32, (L, 128), lambda j, z: (0, j), None)], (C // 128, 1))[0]


def _conv_bwd(name, x_arr, blk_off, w, b, dy):
    L = x_arr.shape[0]
    C = w.shape[1]

    def f(x, wv, bv, g):
        row = _iota(x.shape, 0)
        c = _conv_pre(x, wv, bv, row)
        s = jax.nn.sigmoid(c)
        dc = g * (s * (1.0 + c * (1.0 - s)))
        dx = jnp.zeros_like(x)
        dw = jnp.zeros((4, 128), f32)
        r4 = _iota((4, 128), 0)
        for j in range(4):
            dx = dx + wv[j:j + 1, :] * _shift_up(dc, 3 - j, row)
            dwj = jnp.sum(dc * _shift_dn(x, 3 - j, row), axis=0, keepdims=True)
            dw = dw + jnp.where(r4 == j, jnp.broadcast_to(dwj, (4, 128)), 0.0)
        return dx, dw, jnp.sum(dc, axis=0, keepdims=True)

    return _ew(name, f, [(x_arr, (L, 128), lambda j, z: (0, blk_off + j)), (w, (4, 128), lambda j, z: (0, j)),
                         (b, (1, 128), lambda j, z: (0, j)), (dy, (L, 128), lambda j, z: (0, j))],
               [((L, C), f32, (L, 128), lambda j, z: (0, j), None), ((4, C), f32, (4, 128), lambda j, z: (0, j), None),
                ((1, C), f32, (1, 128), lambda j, z: (0, j), None)], (C // 128, 1))


def _l2norm(t):
    return t * lax.rsqrt(jnp.sum(t * t, axis=-1, keepdims=True) + 1e-6)


def _gdn_act(cq, ck, ab, alog, dtb):
    h = pl.program_id(1)
    qn = _l2norm(cq) * (128.0 ** -0.5)
    kn = _l2norm(ck)
    lane = _iota(ab.shape, 1)
    a_raw = jnp.sum(jnp.where(lane == h, ab, 0.0), axis=1, keepdims=True)
    b_raw = jnp.sum(jnp.where(lane == h + GDN_HEADS, ab, 0.0), axis=1, keepdims=True)
    lane1 = _iota(alog.shape, 1)
    al = jnp.sum(jnp.where(lane1 == h, alog, 0.0), axis=1, keepdims=True)
    db = jnp.sum(jnp.where(lane1 == h, dtb, 0.0), axis=1, keepdims=True)
    g = -jnp.exp(al) * jax.nn.softplus(a_raw + db)
    beta = jax.nn.sigmoid(b_raw)
    return qn, kn, jnp.broadcast_to(g, cq.shape), jnp.broadcast_to(beta, cq.shape)


def _each(f, *lists):
    return [f(*a) for a in zip(*lists)]


def _gdn_chunk(states, consts, vals, dots):
    S = list(states)
    cut = [slice(128 * i, 128 * i + 128) for i in range(len(S))]
    q, k, v, gb, bb = ([t[:, c] for c in cut] for t in vals)
    C = vals[0].shape[0]
    row, col = _iota((C, C), 0), _iota((C, C), 1)
    causal, strict = row >= col, row > col
    ltri = causal.astype(f32)
    eye = (row == col).astype(f32)
    e0 = (_iota((C, 128), 1) == 0).astype(f32)
    last = _iota((C, 1), 0) == C - 1
    Gb = _each(lambda g: dots.dot01(ltri, g), gb)
    Gc = _each(lambda g: jnp.mean(g, axis=1, keepdims=True), Gb)
    Gr = _each(lambda g: dots.dot01(e0, g, _NT), Gb)
    bc = _each(lambda b: jnp.mean(b, axis=1, keepdims=True), bb)
    decay = _each(lambda gc, gr: jnp.where(causal, jnp.exp(jnp.where(causal, gc - gr, 0.0)), 0.0), Gc, Gr)
    kk = _each(lambda a: _dotb(a, a, _NT), k)
    A = _each(lambda b, x, d: jnp.where(strict, b * x * d, 0.0), bc, kk, decay)
    M = _each(lambda a: eye - a, A)
    P = _each(lambda a: dots.dot3(a, a), A)
    for it in range(5):
        M = _each(lambda m, p: m + dots.dot3(m, p), M, P)
        if it < 4:
            P = _each(lambda p: dots.dot3(p, p), P)
    eG = _each(jnp.exp, Gc)
    u = _each(lambda m, x, b: _dotb(m, x * b), M, v, bc)
    w = _each(lambda m, x, b, e: _dotb(m, x * (b * e)), M, k, bc, eG)
    qk = _each(lambda a, b, d: _dotb(a, b, _NT) * d, q, k, decay)
    g_last = _each(lambda gc: jnp.sum(jnp.where(last, gc, 0.0), axis=0, keepdims=True), Gc)
    v_new = _each(lambda a, b, s: a - _dotb(b, s), u, w, S)
    o = _each(lambda a, e, s, b, x: _dotb(a * e, s) + _dotb(b, x), q, eG, S, qk, v_new)
    S_new = _each(lambda gl, s, a, gc, x: jnp.exp(gl) * s + _dotb(a * jnp.exp(gl - gc), x, _TN), g_last, S, k, Gc, v_new)
    return S_new, [jnp.concatenate(o, axis=1)]


def _gdn_post(o, gate, g):
    return _rms(o, g) * (gate * jax.nn.sigmoid(gate))


def _pad_row(v):
    return jnp.pad(v.astype(f32), (0, 128 - v.shape[0])).reshape(1, 128)


def _gdn_fwd(h, g_norm, w_ext, conv_w, a_log, dt_bias, o_norm_g, w_out):
    L, D = h.shape
    tm = _head_rows(L)
    nc = L // CHUNK
    H = GDN_HEADS
    hn = _rms_fwd("mix_norm", h, g_norm)
    proj = _mm_plain("gdn_in", hn, w_ext, _NN)
    zb = jnp.zeros((1, 3 * D), f32)
    cq = _conv_fwd("gdn_conv", proj, 0, conv_w, zb)
    alog, dtb = _pad_row(a_log), _pad_row(dt_bias)
    act_ins = [(cq, (tm, 128), lambda r, hh: (r, hh)), (cq, (tm, 128), lambda r, hh: (r, H + hh)),
               (proj, (tm, 128), lambda r, hh: (r, 4 * H)), (alog, (1, 128), lambda r, hh: (0, 0)),
               (dtb, (1, 128), lambda r, hh: (0, 0))]
    qn, kn, gb, bb = _ew("gdn_act", _gdn_act, act_ins,
                         [((L, D), f32, (tm, 128), lambda r, hh: (r, hh), None)] * 4, (L // tm, H))
    cblk = (CHUNK, 128 * GDN_HB)
    core_ins = [(qn, cblk, lambda u, c: (c, u)), (kn, cblk, lambda u, c: (c, u)),
                (cq, cblk, lambda u, c: (c, 2 * H // GDN_HB + u), (L, D), lambda u, c: (c, u)),
                (gb, cblk, lambda u, c: (c, u)), (bb, cblk, lambda u, c: (c, u))]
    (o,), saved_s = _scan_fwd("gdn_core", _gdn_chunk, GDN_HB, (128, 128), [], core_ins,
                              [((L, D), cblk, lambda u, c: (c, u))], H // GDN_HB, nc)
    on = o_norm_g.reshape(1, 128)
    post_ins = [(o, (tm, 128), lambda r, hh: (r, hh)), (proj, (tm, 128), lambda r, hh: (r, 3 * H + hh)),
                (on, (1, 128), lambda r, hh: (0, 0))]
    y = _ew("gdn_post", _gdn_post, post_ins, [((L, D), f32, (tm, 128), lambda r, hh: (r, hh), None)], (L // tm, H))[0]
    h_out = _mm_plain("gdn_out", y, w_out, _NN, epi_fn=lambda acc, res: acc + res, aux=[(h, "e")])
    saved = dict(hn=hn, proj=proj, cq=cq, alog=alog, dtb=dtb, act_ins=act_ins, core_ins=core_ins, saved_s=saved_s,
                 post_ins=post_ins, y=y, zb=zb)
    return h_out, saved


def _gdn_bwd(dh, h, g_norm, w_ext, conv_w, w_out, sv):
    L, D = h.shape
    tm = _head_rows(L)
    nc = L // CHUNK
    H = GDN_HEADS
    dy = _mm_plain("gdn_out_dx", dh, w_out, _NT)
    dw_out = _mm_plain("gdn_out_dw", sv["y"], dh, _TN)
    hd = ((L, D), f32, (tm, 128), lambda r, hh: (r, hh), None)
    d_o, d_gate, d_on = _ew("gdn_post_bwd", _vjp_fn(_gdn_post, 3),
                            sv["post_ins"] + [(dy, (tm, 128), lambda r, hh: (r, hh))],
                            [hd, hd, ((1, 128), f32, (1, 128), lambda r, hh: (0, 0), "all")], (L // tm, H))
    cblk = (CHUNK, 128 * GDN_HB)
    _, (dqn, dkn, dv, dgb, dbb) = _scan_bwd("gdn_core_bwd", _gdn_chunk, GDN_HB, (128, 128), [], sv["core_ins"],
                                            sv["saved_s"], [(d_o, cblk, lambda u, c: (c, u))], H // GDN_HB, nc)
    cts = [(t, (tm, 128), lambda r, hh: (r, hh)) for t in (dqn, dkn, dgb, dbb)]
    row128 = ((1, 128), f32, (1, 128), lambda r, hh: (0, 0), "all")
    d_cq, d_ck, d_ab, d_alog, d_dtb = _ew(
        "gdn_act_bwd", _vjp_fn(_gdn_act, 5), sv["act_ins"] + cts,
        [hd, hd, ((L, 128), f32, (tm, 128), lambda r, hh: (r, 0), "inner"), row128, row128], (L // tm, H))
    d_conv_out = jnp.concatenate([d_cq, d_ck, dv], axis=1)
    d_conv_in, d_conv_w, _ = _conv_bwd("gdn_conv_bwd", sv["proj"], 0, conv_w, sv["zb"], d_conv_out)
    d_proj = jnp.concatenate([d_conv_in, d_gate, d_ab], axis=1)
    dw_ext = _mm_plain("gdn_in_dw", sv["hn"], d_proj, _TN)
    dhn = _mm_plain("gdn_in_dx", d_proj, w_ext, _NT)
    dh_in, dg = _rms_bwd("mix_norm_bwd", h, g_norm, dhn, dh)
    grads = dict(norm=dg, w_ext=dw_ext, conv_w=d_conv_w, a_log=d_alog[0, :H], dt_bias=d_dtb[0, :H],
                 o_norm_g=d_on[0], w_out=dw_out)
    return dh_in, grads


def _expand_lanes(row, width, rep):
    sel = ((_iota((128, width), 1) // rep) == _iota((128, width), 0)).astype(f32)
    return jnp.mean(_dot(jnp.broadcast_to(row, (8, 128)), sel), axis=0, keepdims=True)


def _s5_params(lre, lim, ldt, wbr, wbi):
    dt = jnp.exp(_expand_lanes(ldt, 512, S5_STATE))
    mag = jnp.exp(lre * dt)
    ang = lim * dt
    abr, abi = mag * jnp.cos(ang), mag * jnp.sin(ang)
    nr = abr - 1.0
    den = lre * lre + lim * lim
    cr = (nr * lre + abi * lim) / den
    ci = (abi * lre - nr * lim) / den
    return abr, abi, cr * wbr - ci * wbi, cr * wbi + ci * wbr


def _s5_scan(name, xr, xi, ar, ai, rev, want_prev):
    L, W = xr.shape
    nb = L // 8
    n_out = 4 if want_prev else 2

    def body(xr_ref, xi_ref, ar_ref, ai_ref, *outs):
        a_r = ar_ref[...]
        a_i = -ai_ref[...] if rev else ai_ref[...]

        def cm(p, q):
            return p[0] * q[0] - p[1] * q[1], p[0] * q[1] + p[1] * q[0]

        a1 = (a_r, a_i)
        a2 = cm(a1, a1)
        a3 = cm(a2, a1)
        a4 = cm(a2, a2)
        pw = [a1, a2, a3, a4, cm(a4, a1), cm(a4, a2), cm(a4, a3), cm(a4, a4)]
        blk8 = (8, S5_SCAN_LANES)
        row = _iota(blk8, 0)
        tab_r = jnp.zeros(blk8, f32)
        tab_i = jnp.zeros(blk8, f32)
        for t in range(8):
            idx = 7 - t if rev else t
            tab_r = jnp.where(row == idx, jnp.broadcast_to(pw[t][0], blk8), tab_r)
            tab_i = jnp.where(row == idx, jnp.broadcast_to(pw[t][1], blk8), tab_i)
        lv = [(d, jnp.broadcast_to(p[0], blk8), jnp.broadcast_to(p[1], blk8)) for d, p in ((1, a1), (2, a2), (4, a4))]

        def step(i, carry):
            cr, ci = carry
            blk = nb - 1 - i if rev else i
            r0 = pl.multiple_of(blk * 8, 8)
            x_r = xr_ref[pl.ds(r0, 8), :]
            x_i = xi_ref[pl.ds(r0, 8), :]
            for d, p_r, p_i in lv:
                if rev:
                    s_r = jnp.where(row < 8 - d, pltpu.roll(x_r, 8 - d, 0), 0.0)
                    s_i = jnp.where(row < 8 - d, pltpu.roll(x_i, 8 - d, 0), 0.0)
                else:
                    s_r = jnp.where(row >= d, pltpu.roll(x_r, d, 0), 0.0)
                    s_i = jnp.where(row >= d, pltpu.roll(x_i, d, 0), 0.0)
                x_r, x_i = x_r + p_r * s_r - p_i * s_i, x_i + p_r * s_i + p_i * s_r
            x_r, x_i = x_r + tab_r * cr - tab_i * ci, x_i + tab_r * ci + tab_i * cr
            outs[0][pl.ds(r0, 8), :] = x_r
            outs[1][pl.ds(r0, 8), :] = x_i
            if want_prev:
                outs[2][pl.ds(r0, 8), :] = jnp.where(row >= 1, pltpu.roll(x_r, 1, 0), cr)
                outs[3][pl.ds(r0, 8), :] = jnp.where(row >= 1, pltpu.roll(x_i, 1, 0), ci)
            e = 0 if rev else 7
            return jnp.broadcast_to(x_r[e:e + 1, :], blk8), jnp.broadcast_to(x_i[e:e + 1, :], blk8)

        lax.fori_loop(0, nb, step, (jnp.zeros(blk8, f32), jnp.zeros(blk8, f32)))

    per = 512 // S5_SCAN_LANES
    col = pl.BlockSpec((L, S5_SCAN_LANES), lambda q, z: (0, q))
    aspec = pl.BlockSpec((None, 1, S5_SCAN_LANES), lambda q, z: (q // per, 0, q % per))
    return pl.pallas_call(
        body, name=name, grid=(W // S5_SCAN_LANES, 1), in_specs=[col, col, aspec, aspec], out_specs=[col] * n_out,
        out_shape=[jax.ShapeDtypeStruct((L, W), f32)] * n_out, compiler_params=_params(2),
    )(xr, xi, ar, ai)


def _blockdiag(t, n_in, n_out):
    t4 = t.reshape(8, 8, n_in, n_out)
    return jnp.einsum("jaio,ab->jaibo", t4, jnp.eye(8, dtype=t.dtype)).reshape(8, 8 * n_in, 8 * n_out)


def _blockdiag_t(w, n_in, n_out):
    w5 = w.reshape(8, 8, n_in, 8, n_out)
    return jnp.einsum("jaibo,ab->jaio", w5, jnp.eye(8, dtype=w.dtype)).reshape(64, n_in, n_out)


def _glu(ag, h):
    n = ag.shape[1] // 2
    return h + ag[:, :n] * jax.nn.sigmoid(ag[:, n:])


def _s5_fwd(h, g_norm, w_in, lam_re, lam_im, log_dt, b_re, b_im, c_re, c_im, d_skip, w_out_g):
    L, D = h.shape
    tm, te = _mm_rows(L), _row_tile(L)
    W = 8 * 512
    hn = _rms_fwd("mix_norm", h, g_norm)
    u = _mm_plain("s5_in", hn, w_in, _NN)
    lre, lim = lam_re.reshape(8, 1, 512), lam_im.reshape(8, 1, 512)
    ldt = jnp.pad(log_dt.reshape(8, 1, 8), ((0, 0), (0, 0), (0, 120)))
    wbr = _blockdiag(b_re.transpose(0, 2, 1), 16, 64)
    wbi = _blockdiag(b_im.transpose(0, 2, 1), 16, 64)
    wcr = _blockdiag(c_re.transpose(0, 2, 1), 64, 16)
    wci = _blockdiag(c_im.transpose(0, 2, 1), 64, 16)
    jb = lambda shape: (shape, lambda j, z: (j, 0, 0))
    par_ins = [(lre, *jb((None, 1, 512))), (lim, *jb((None, 1, 512))), (ldt, *jb((None, 1, 128))),
               (wbr, *jb((None, 128, 512))), (wbi, *jb((None, 128, 512)))]
    abr, abi, bbr, bbi = _ew("s5_params", _s5_params, par_ins,
                             [((8, 1, 512), f32, *jb((None, 1, 512)), None)] * 2
                             + [((8, 128, 512), f32, *jb((None, 128, 512)), None)] * 2, (8, 1))

    def bu(name, wb):
        return _mm(name, u, wb, dims=_NN, grid=(L // tm, 8, 1),
                   a_spec=pl.BlockSpec((tm, 128), lambda i, j, k: (i, j)),
                   b_spec=pl.BlockSpec((None, 128, 512), lambda i, j, k: (j, 0, 0)),
                   out_shape=jax.ShapeDtypeStruct((L, W), f32), out_spec=pl.BlockSpec((tm, 512), lambda i, j, k: (i, j)),
                   acc_shape=(tm, 512))

    bur, bui = bu("s5_bu", bbr), bu("s5_bu", bbi)
    sr, si, pr, pi = _s5_scan("s5_scan", bur, bui, abr, abi, False, True)
    d_row = d_skip.reshape(1, D)
    cspec = dict(a_spec=pl.BlockSpec((tm, 512), lambda i, j, k: (i, j)),
                 b_spec=pl.BlockSpec((None, 512, 128), lambda i, j, k: (j, 0, 0)),
                 out_shape=jax.ShapeDtypeStruct((L, D), f32), out_spec=pl.BlockSpec((tm, 128), lambda i, j, k: (i, j)),
                 acc_shape=(tm, 128))
    e128 = pl.BlockSpec((tm, 128), lambda i, j, k: (i, j))
    pre1 = _mm("s5_c_re", sr, wcr, dims=_NN, grid=(L // tm, 8, 1), **cspec)
    pre = _mm("s5_c_im", si, wci, dims=_NN, grid=(L // tm, 8, 1),
              aux=[(pre1, e128, "e"), (u, e128, "e"), (d_row, pl.BlockSpec((1, 128), lambda i, j, k: (0, j)), "e")],
              epi_fn=lambda acc, p1, uu, dd: p1 - acc + dd * uu, **cspec)
    ws = D // N_DEV * 2
    ag = _mm("s5_out", pre, w_out_g, dims=_NN, grid=(L // tm, N_DEV, 1),
             a_spec=pl.BlockSpec((tm, D), lambda i, j, k: (i, 0)),
             b_spec=pl.BlockSpec((None, None, D, ws), lambda i, j, k: (j, 0, 0, 0)),
             out_shape=jax.ShapeDtypeStruct((L, 2 * D), f32), out_spec=pl.BlockSpec((tm, ws), lambda i, j, k: (i, j)),
             a_fn=jax.nn.gelu, acc_shape=(tm, ws), cache_a=True)
    h_out = _ew("s5_glu", _glu, [(ag, *_rows(te, 2 * D)), (h, *_rows(te, D))],
                [((L, D), f32, *_rows(te, D), None)], (L // te, 1))[0]
    saved = dict(hn=hn, u=u, par_ins=par_ins, abr=abr, abi=abi, bbr=bbr, bbi=bbi, wcr=wcr, wci=wci, sr=sr, si=si,
                 pr=pr, pi=pi, pre=pre, ag=ag, d_row=d_row)
    return h_out, saved


def _s5_bwd(dh, h, g_norm, w_in, w_out_g, sv):
    L, D = h.shape
    tm, te = _mm_rows(L), _row_tile(L)
    tk = min(512, L)
    W = 8 * 512
    ws = D // N_DEV * 2
    u, pre, d_row = sv["u"], sv["pre"], sv["d_row"]
    d_ag = _ew("s5_glu_bwd", lambda ag, hv, g: _vjp_fn(_glu, 2)(ag, hv, g)[0],
               [(sv["ag"], *_rows(te, 2 * D)), (h, *_rows(te, D)), (dh, *_rows(te, D))],
               [((L, 2 * D), f32, *_rows(te, 2 * D), None)], (L // te, 1))[0]
    tr = D
    dw_out = _mm("s5_out_dw", pre, d_ag, dims=_TN, grid=(D // tr, N_DEV, L // tk),
                 a_spec=pl.BlockSpec((tk, tr), lambda i, j, k: (k, i)),
                 b_spec=pl.BlockSpec((tk, ws), lambda i, j, k: (k, j)),
                 out_shape=jax.ShapeDtypeStruct((N_DEV, 1, D, ws), f32),
                 out_spec=pl.BlockSpec((None, None, tr, ws), lambda i, j, k: (j, 0, i, 0)),
                 a_fn=jax.nn.gelu, acc_shape=(tr, ws))
    tn = 512
    dpre = _mm("s5_out_dx", d_ag, w_out_g, dims=_NT, grid=(L // tm, D // tn, N_DEV),
               a_spec=pl.BlockSpec((tm, ws), lambda i, j, k: (i, k)),
               b_spec=pl.BlockSpec((None, None, tn, ws), lambda i, j, k: (k, 0, j, 0)),
               out_shape=jax.ShapeDtypeStruct((L, D), f32), out_spec=pl.BlockSpec((tm, tn), lambda i, j, k: (i, j)),
               aux=[(pre, pl.BlockSpec((tm, tn), lambda i, j, k: (i, j)), "e")],
               epi_fn=lambda acc, p: _vjp_fn(jax.nn.gelu, 1)(p, acc)[0], acc_shape=(tm, tn))
    d_d = _ew("s5_dskip", lambda a, b: jnp.sum(a * b, axis=0, keepdims=True),
              [(dpre, *_rows(te, D)), (u, *_rows(te, D))], [((1, D), f32, *_const((1, D)), "all")], (L // te, 1))[0]
    neg = lambda acc: -acc
    dsspec = dict(dims=_NT, grid=(L // tm, 8, 1), a_spec=pl.BlockSpec((tm, 128), lambda i, j, k: (i, j)),
                  b_spec=pl.BlockSpec((None, 512, 128), lambda i, j, k: (j, 0, 0)),
                  out_shape=jax.ShapeDtypeStruct((L, W), f32), out_spec=pl.BlockSpec((tm, 512), lambda i, j, k: (i, j)),
                  acc_shape=(tm, 512))
    dsr = _mm("s5_c_re_dx", dpre, sv["wcr"], **dsspec)
    dsi = _mm("s5_c_im_dx", dpre, sv["wci"], epi_fn=neg, **dsspec)
    dwspec = dict(dims=_TN, grid=(8, 1, L // tk), a_spec=pl.BlockSpec((tk, 512), lambda i, j, k: (k, i)),
                  b_spec=pl.BlockSpec((tk, 128), lambda i, j, k: (k, i)),
                  out_shape=jax.ShapeDtypeStruct((8, 512, 128), f32),
                  out_spec=pl.BlockSpec((None, 512, 128), lambda i, j, k: (i, 0, 0)), acc_shape=(512, 128))
    dwcr = _mm("s5_c_re_dw", sv["sr"], dpre, **dwspec)
    dwci = _mm("s5_c_im_dw", sv["si"], dpre, epi_fn=neg, **dwspec)
    lr, li = _s5_scan("s5_scan_bwd", dsr, dsi, sv["abr"], sv["abi"], True, False)

    def da(lrv, liv, prv, piv):
        return (jnp.sum(lrv * prv + liv * piv, axis=0, keepdims=True),
                jnp.sum(liv * prv - lrv * piv, axis=0, keepdims=True))

    sblk = ((te, 512), lambda j, r: (r, j))
    dabr, dabi = _ew("s5_dlam", da, [(lr, *sblk), (li, *sblk), (sv["pr"], *sblk), (sv["pi"], *sblk)],
                     [((8, 1, 512), f32, (None, 1, 512), lambda j, r: (j, 0, 0), "inner")] * 2, (8, L // te))
    dbspec = dict(dims=_TN, grid=(8, 1, L // tk), a_spec=pl.BlockSpec((tk, 128), lambda i, j, k: (k, i)),
                  b_spec=pl.BlockSpec((tk, 512), lambda i, j, k: (k, i)),
                  out_shape=jax.ShapeDtypeStruct((8, 128, 512), f32),
                  out_spec=pl.BlockSpec((None, 128, 512), lambda i, j, k: (i, 0, 0)), acc_shape=(128, 512))
    dbbr = _mm("s5_bu_dw", u, lr, **dbspec)
    dbbi = _mm("s5_bu_dw", u, li, **dbspec)
    duspec = dict(dims=_NT, grid=(L // tm, 8, 1), a_spec=pl.BlockSpec((tm, 512), lambda i, j, k: (i, j)),
                  b_spec=pl.BlockSpec((None, 128, 512), lambda i, j, k: (j, 0, 0)),
                  out_shape=jax.ShapeDtypeStruct((L, D), f32), out_spec=pl.BlockSpec((tm, 128), lambda i, j, k: (i, j)),
                  acc_shape=(tm, 128))
    e128 = pl.BlockSpec((tm, 128), lambda i, j, k: (i, j))
    du1 = _mm("s5_bu_dx_re", lr, sv["bbr"], **duspec)
    du = _mm("s5_bu_dx_im", li, sv["bbi"],
             aux=[(du1, e128, "e"), (dpre, e128, "e"), (d_row, pl.BlockSpec((1, 128), lambda i, j, k: (0, j)), "e")],
             epi_fn=lambda acc, d1, dp, dd: acc + d1 + dp * dd, **duspec)
    jb = lambda shape: (shape, lambda j, z: (j, 0, 0))
    cts = [(dabr, *jb((None, 1, 512))), (dabi, *jb((None, 1, 512))), (dbbr, *jb((None, 128, 512))),
           (dbbi, *jb((None, 128, 512)))]
    dlre, dlim, dldt, dwbr, dwbi = _ew(
        "s5_params_bwd", _vjp_fn(_s5_params, 5), sv["par_ins"] + cts,
        [((8, 1, 512), f32, *jb((None, 1, 512)), None)] * 2 + [((8, 1, 128), f32, *jb((None, 1, 128)), None)]
        + [((8, 128, 512), f32, *jb((None, 128, 512)), None)] * 2, (8, 1))
    dw_in = _mm_plain("s5_in_dw", sv["hn"], du, _TN)
    dhn = _mm_plain("s5_in_dx", du, w_in, _NT)
    dh_in, dg = _rms_bwd("mix_norm_bwd", h, g_norm, dhn, dh)
    grads = dict(norm=dg, w_in=dw_in, lam_re=dlre.reshape(64, 64), lam_im=dlim.reshape(64, 64),
                 log_dt=dldt[:, 0, :8].reshape(64),
                 b_re=_blockdiag_t(dwbr, 16, 64).transpose(0, 2, 1), b_im=_blockdiag_t(dwbi, 16, 64).transpose(0, 2, 1),
                 c_re=_blockdiag_t(dwcr, 64, 16).transpose(0, 2, 1), c_im=_blockdiag_t(dwci, 64, 16).transpose(0, 2, 1),
                 d=d_d[0], w_out=dw_out)
    return dh_in, grads


def _m2_act(dt_raw, dtbias, alog):
    dt = jax.nn.softplus(dt_raw + dtbias)
    da = dt * (-jnp.exp(alog))
    sel = ((_iota((128, M2_INNER), 1) // 64) == _iota((128, M2_INNER), 0)).astype(f32)
    return _dot(dt, sel), _dot(da, sel)


def _m2_dexp(d):
    return _expand_lanes(d, M2_INNER, 64)


def _ssd_chunk(states, consts, vals, dots):
    (dsk,) = consts
    S = list(states)
    n = len(S)
    cut = [slice(128 * i, 128 * i + 128) for i in range(n)]
    x, dtb, dab = ([t[:, c] for c in cut] for t in vals[:3])
    dsk = [dsk[:, c] for c in cut]
    B = [vals[3][:, cut[i // 2]] for i in range(n)]
    Cm = [vals[4][:, cut[i // 2]] for i in range(n)]
    C = vals[0].shape[0]
    row, col = _iota((C, C), 0), _iota((C, C), 1)
    causal = row >= col
    ltri = causal.astype(f32)
    lane = _iota((C, 128), 1)
    last = _iota((C, 128), 0) == C - 1
    eye128 = _iota((128, 128), 0) == _iota((128, 128), 1)
    head = [jnp.logical_and(lane >= 64 * hh, lane < 64 * hh + 64) for hh in range(2)]
    pick = [(lane == 64 * hh).astype(f32) for hh in range(2)]
    xdt = _each(lambda a, b: a * b, x, dtb)
    cb = _each(lambda c, b: _dotb(c, b, _NT), Cm[::2], B[::2])
    cum = _each(lambda a: dots.dot01(ltri, a), dab)
    clast = _each(lambda a: jnp.sum(jnp.where(last, a, 0.0), axis=0, keepdims=True), cum)
    st = _each(lambda a, cl, cu, b: _dotb(a * jnp.exp(cl - cu), b, _TN), xdt, clast, cum, B)
    y = _each(lambda c, s, cu: _dotb(c, s, _NT) * jnp.exp(cu), Cm, S, cum)
    for hh in range(2):
        ccol = _each(lambda cu: jnp.sum(jnp.where(head[hh], cu, 0.0), axis=1, keepdims=True) * (1.0 / 64), cum)
        crow = _each(lambda cu: dots.dot01(pick[hh], cu, _NT), cum)
        lm = _each(lambda a, b: jnp.where(causal, jnp.exp(jnp.where(causal, a - b, 0.0)), 0.0), ccol, crow)
        y = [y[i] + _dotb(cb[i // 2] * lm[i], jnp.where(head[hh], xdt[i], 0.0)) for i in range(n)]
    cdcol = _each(lambda cl: jnp.sum(jnp.where(eye128, jnp.broadcast_to(jnp.exp(cl), (128, 128)), 0.0),
                                     axis=1, keepdims=True), clast)
    S_new = _each(lambda c, s, t: c * s + t, cdcol, S, st)
    out = _each(lambda a, d, b: a + d * b, y, dsk, x)
    return S_new, [jnp.concatenate(out, axis=1)]


def _m2_post(yc, z, ng):
    return _rms(yc * (z * jax.nn.sigmoid(z)), ng)


def _m2_fwd(h, g_norm, w_ext, conv_w, conv_b, dt_bias, a_log, d_skip, norm_g, w_out):
    L, D = h.shape
    tm = _row_tile(L)
    nc = L // CHUNK
    NI = M2_INNER
    hn = _rms_fwd("mix_norm", h, g_norm)
    proj = _mm_plain("m2_in", hn, w_ext, _NN)
    xbc = _conv_fwd("m2_conv", proj, NI // 128, conv_w, conv_b)
    dtb_row, alog_row, d_pad = _pad_row(dt_bias), _pad_row(a_log), _pad_row(d_skip)
    act_ins = [(proj, (tm, 128), lambda r, z: (r, 3 * NI // 128)), (dtb_row, *_const((1, 128))),
               (alog_row, *_const((1, 128)))]
    dtb, dab = _ew("m2_act", _m2_act, act_ins, [((L, NI), f32, *_rows(tm, NI), None)] * 2, (L // tm, 1))
    dsk = _ew("m2_dexp", _m2_dexp, [(d_pad, *_const((1, 128)))], [((1, NI), f32, *_const((1, NI)), None)], (1, 1))[0]
    GB = M2_GB
    x_blk, bc_blk = (CHUNK, 256 * GB), (CHUNK, 128 * GB)
    cins = [(dsk, (1, 256 * GB), lambda u, c: (0, u))]
    core_ins = [(xbc, x_blk, lambda u, c: (c, u), (L, NI), lambda u, c: (c, u)),
                (dtb, x_blk, lambda u, c: (c, u)), (dab, x_blk, lambda u, c: (c, u)),
                (xbc, bc_blk, lambda u, c: (c, 16 // GB + u), (L, D), lambda u, c: (c, u)),
                (xbc, bc_blk, lambda u, c: (c, 24 // GB + u), (L, D), lambda u, c: (c, u))]
    (yc,), saved_s = _scan_fwd("m2_core", _ssd_chunk, 2 * GB, (128, 128), cins, core_ins,
                               [((L, NI), x_blk, lambda u, c: (c, u))], 8 // GB, nc)
    tp = _head_rows(L)
    gblk = ((tp, 256), lambda g, r: (r, g))
    post_ins = [(yc, *gblk), (proj, *gblk), (norm_g, (1, 256), lambda g, r: (0, g))]
    yn = _ew("m2_post", _m2_post, post_ins, [((L, NI), f32, *gblk, None)], (8, L // tp))[0]
    h_out = _mm_plain("m2_out", yn, w_out, _NN, epi_fn=lambda acc, res: acc + res, aux=[(h, "e")])
    saved = dict(hn=hn, proj=proj, act_ins=act_ins, d_pad=d_pad, cins=cins, core_ins=core_ins, saved_s=saved_s,
                 post_ins=post_ins, yn=yn)
    return h_out, saved


def _m2_bwd(dh, h, g_norm, w_ext, conv_w, conv_b, w_out, sv):
    L, D = h.shape
    tm = _row_tile(L)
    nc = L // CHUNK
    NI = M2_INNER
    dyn = _mm_plain("m2_out_dx", dh, w_out, _NT)
    dw_out = _mm_plain("m2_out_dw", sv["yn"], dh, _TN)
    tp = _head_rows(L)
    gblk = ((tp, 256), lambda g, r: (r, g))
    d_yc, d_z, d_ng = _ew("m2_post_bwd", _vjp_fn(_m2_post, 3), sv["post_ins"] + [(dyn, *gblk)],
                          [((L, NI), f32, *gblk, None)] * 2 + [((1, NI), f32, (1, 256), lambda g, r: (0, g), "inner")],
                          (8, L // tp))
    (d_dsk,), (dx, d_dtb, d_dab, dB, dC) = _scan_bwd(
        "m2_core_bwd", _ssd_chunk, 2 * M2_GB, (128, 128), sv["cins"], sv["core_ins"], sv["saved_s"],
        [(d_yc, (CHUNK, 256 * M2_GB), lambda u, c: (c, u))], 8 // M2_GB, nc)
    row128 = ((1, 128), f32, *_const((1, 128)), "all")
    d_dt_raw, d_dtbias, d_alog = _ew(
        "m2_act_bwd", _vjp_fn(_m2_act, 3), sv["act_ins"] + [(d_dtb, *_rows(tm, NI)), (d_dab, *_rows(tm, NI))],
        [((L, 128), f32, *_rows(tm, 128), None), row128, row128], (L // tm, 1))
    d_d = _ew("m2_dexp_bwd", _vjp_fn(_m2_dexp, 1), [(sv["d_pad"], *_const((1, 128))), (d_dsk, *_const((1, NI)))],
              [((1, 128), f32, *_const((1, 128)), None)], (1, 1))[0]
    d_conv_out = jnp.concatenate([dx, dB, dC], axis=1)
    d_conv_in, d_conv_w, d_conv_b = _conv_bwd("m2_conv_bwd", sv["proj"], NI // 128, conv_w, conv_b, d_conv_out)
    d_proj = jnp.concatenate([d_z, d_conv_in, d_dt_raw], axis=1)
    dw_ext = _mm_plain("m2_in_dw", sv["hn"], d_proj, _TN)
    dhn = _mm_plain("m2_in_dx", d_proj, w_ext, _NT)
    dh_in, dg = _rms_bwd("mix_norm_bwd", h, g_norm, dhn, dh)
    grads = dict(norm=dg, w_ext=dw_ext, conv_w=d_conv_w, conv_b=d_conv_b, dt_bias=d_dtbias[0, :M2_HEADS],
                 a_log=d_alog[0, :M2_HEADS], d=d_d[0, :M2_HEADS], norm_g=d_ng, w_out=dw_out)
    return dh_in, grads


def _mesh_pos():
    return lax.axis_index("x"), lax.axis_index("y"), lax.axis_index("c")


def _flip(pos, p):
    x, y, c = pos
    return (1 - x if p & 4 else x, 1 - y if p & 2 else y, 1 - c if p & 1 else c)


def _index(pos):
    return 4 * pos[0] + 2 * pos[1] + pos[2]


def _comm_call(name, body, arrays, out_shape, n_sem):
    n = len(arrays)
    hbm = pl.BlockSpec(memory_space=pl.ANY)
    return pl.pallas_call(
        body, name=name, in_specs=[hbm] * n, out_specs=[hbm] * len(out_shape), out_shape=out_shape,
        scratch_shapes=[pltpu.SemaphoreType.DMA((n, n_sem)), pltpu.SemaphoreType.DMA((n, n_sem)),
                        pltpu.SemaphoreType.DMA((n, 4))],
    )(*arrays)


def _gather(name, arrays):
    n = len(arrays)

    def body(*refs):
        ins, outs = refs[:n], refs[n:2 * n]
        send_sems, recv_sems, loc_sems = refs[2 * n:]
        me = _mesh_pos()
        c = me[2]
        sib = _flip(me, 1)
        chips = [_flip(me, 4), _flip(me, 2), _flip(me, 6)]

        def copy(w, k, block, to, src=None):
            slab = outs[w].at[_index(block)]
            return pltpu.make_async_remote_copy(
                src_ref=slab if src is None else src, dst_ref=slab, send_sem=send_sems.at[w, k],
                recv_sem=recv_sems.at[w, k], device_id=to, device_id_type=MESH)

        local = [pltpu.make_async_copy(ins[w], outs[w].at[_index(me)], loc_sems.at[w, 0]) for w in range(n)]
        for cp in local:
            cp.start()
        first = [copy(w, 0, me, sib, src=ins[w]) for w in range(n)]
        first += [copy(w, 1 + j, me, chip, src=ins[w]) for j, chip in enumerate(chips) for w in range(n)]
        for cp in first:
            cp.start()
        passed = []
        for j, chip in enumerate(chips):
            for w in range(n):
                copy(w, 1 + j, chip, me).wait_recv()
                fwd = copy(w, 4 + j, chip, sib)
                fwd.start()
                passed.append(fwd)
        for w in range(n):
            copy(w, 0, sib, me).wait_recv()
        for j, chip in enumerate(chips):
            for w in range(n):
                copy(w, 4 + j, (chip[0], chip[1], 1 - c), me).wait_recv()
        for cp in first + passed:
            cp.wait_send()
        for cp in local:
            cp.wait()

    out_shape = [jax.ShapeDtypeStruct((N_DEV,) + a.shape, a.dtype) for a in arrays]
    return _comm_call(name, body, arrays, out_shape, N_DEV - 1)


_HBM = pl.BlockSpec(memory_space=pltpu.HBM)
_SEM = pl.BlockSpec(memory_space=pltpu.SEMAPHORE)
_SPLIT_COPIES = 4


def _split_targets(me):
    return [_flip(me, 1), _flip(me, 4), _flip(me, 2), _flip(me, 6)]


def _gather_start(name, arrays, lands):
    n = len(arrays)
    ns = n * _SPLIT_COPIES

    def body(*refs):
        ins, land = refs[:n], refs[n:2 * n]
        send_sems, recv_sems = refs[2 * n:2 * n + ns], refs[2 * n + ns:2 * n + 2 * ns]
        token = refs[4 * n + 2 * ns]
        me = _mesh_pos()
        for w in range(n):
            for k, to in enumerate(_split_targets(me)):
                pltpu.make_async_remote_copy(
                    src_ref=ins[w], dst_ref=land[w].at[_index(me)], send_sem=send_sems[w * _SPLIT_COPIES + k],
                    recv_sem=recv_sems[w * _SPLIT_COPIES + k], device_id=to, device_id_type=MESH).start()
        token[...] = jnp.zeros_like(token)

    sem = pltpu.SemaphoreType.DMA(())
    res = pl.pallas_call(
        body, name=name,
        out_shape=(*[sem] * (2 * ns), *[pltpu.HBM(a.shape, a.dtype) for a in arrays],
                   *[pltpu.HBM(a.shape, a.dtype) for a in lands], jax.ShapeDtypeStruct((8, 128), f32)),
        in_specs=[_HBM] * (2 * n),
        out_specs=(*[_SEM] * (2 * ns), *[_HBM] * (2 * n), pl.BlockSpec(memory_space=pltpu.VMEM)),
        input_output_aliases={i: 2 * ns + i for i in range(2 * n)},
        compiler_params=pltpu.CompilerParams(has_side_effects=pltpu.SideEffectType.DATAFLOW_SIDE_EFFECTING),
    )(*[pltpu.with_memory_space_constraint(a, pltpu.HBM) for a in list(arrays) + list(lands)])
    sems, rest = res[:2 * ns], res[2 * ns:]
    return sems[:ns], sems[ns:], rest[:n], rest[n:2 * n], rest[2 * n]


def _gather_wait(name, arrays, lands, send_sems, recv_sems, after):
    n = len(arrays)
    ns = n * _SPLIT_COPIES

    def body(*refs):
        ins, land = refs[:n], refs[n:2 * n]
        s_sems, r_sems = refs[2 * n:2 * n + ns], refs[2 * n + ns:2 * n + 2 * ns]
        me = _mesh_pos()
        for w in range(n):
            for k, peer in enumerate(_split_targets(me)):
                cp = pltpu.make_async_remote_copy(
                    src_ref=ins[w], dst_ref=land[w].at[_index(peer)], send_sem=s_sems[w * _SPLIT_COPIES + k],
                    recv_sem=r_sems[w * _SPLIT_COPIES + k], device_id=peer, device_id_type=MESH)
                cp.wait_send()
                cp.wait_recv()

    res = pl.pallas_call(
        body, name=name,
        out_shape=(*[pltpu.HBM(a.shape, a.dtype) for a in arrays], *[pltpu.HBM(a.shape, a.dtype) for a in lands]),
        in_specs=[_HBM] * (2 * n) + [_SEM] * (2 * ns) + [pl.BlockSpec(memory_space=pl.ANY)],
        out_specs=tuple([_HBM] * (2 * n)), input_output_aliases={i: i for i in range(2 * n)},
        compiler_params=pltpu.CompilerParams(has_side_effects=pltpu.SideEffectType.DATAFLOW_SIDE_EFFECTING),
    )(*arrays, *lands, *send_sems, *recv_sems, after)
    return res[n:]


def _gather_forward(name, lands):
    n = len(lands)

    def body(*refs):
        outs = refs[n:2 * n]
        send_sems, recv_sems, _ = refs[2 * n:]
        me = _mesh_pos()
        sib = _flip(me, 1)
        held = [_flip(me, 4), _flip(me, 2), _flip(me, 6), sib]

        def copy(w, j, block):
            slab = outs[w].at[_index(block)]
            return pltpu.make_async_remote_copy(src_ref=slab, dst_ref=slab, send_sem=send_sems.at[w, j],
                                                recv_sem=recv_sems.at[w, j], device_id=sib, device_id_type=MESH)

        sends = [copy(w, j, blk) for j, blk in enumerate(held) for w in range(n)]
        for cp in sends:
            cp.start()
        for j, blk in enumerate(held):
            for w in range(n):
                copy(w, j, (blk[0], blk[1], 1 - blk[2])).wait_recv()
        for cp in sends:
            cp.wait_send()

    hbm = pl.BlockSpec(memory_space=pl.ANY)
    return pl.pallas_call(
        body, name=name, in_specs=[hbm] * n, out_specs=[hbm] * n,
        out_shape=[jax.ShapeDtypeStruct(a.shape, a.dtype) for a in lands],
        input_output_aliases={i: i for i in range(n)},
        scratch_shapes=[pltpu.SemaphoreType.DMA((n, 4)), pltpu.SemaphoreType.DMA((n, 4)), pltpu.SemaphoreType.DMA((n, 4))],
    )(*lands)


def _scatter_pair(name, arrays):
    n = len(arrays)

    def body(*refs):
        ins, outs = refs[:n], refs[n:2 * n]
        send_sems, recv_sems, _ = refs[2 * n:]
        me = _mesh_pos()
        c = me[2]
        sib = _flip(me, 1)

        def copy(w, q):
            return pltpu.make_async_remote_copy(
                src_ref=ins[w].at[2 * q + 1 - c], dst_ref=outs[w].at[q], send_sem=send_sems.at[w, q],
                recv_sem=recv_sems.at[w, q], device_id=sib, device_id_type=MESH)

        cps = [copy(w, q) for q in range(4) for w in range(n)]
        for cp in cps:
            cp.start()
        for cp in cps:
            cp.wait()

    out_shape = [jax.ShapeDtypeStruct((4,) + a.shape[1:], a.dtype) for a in arrays]
    return _comm_call(name, body, arrays, out_shape, 4)


def _pair_add(name, full, theirs, core, dtype):
    _, R, C = theirs.shape
    tr = R if R <= 256 else (256 if C <= 512 else 128)

    def body(core_ref, mine_ref, theirs_ref, o_ref):
        o_ref[...] = (mine_ref[...] + theirs_ref[...]).astype(o_ref.dtype)

    blk = pl.BlockSpec((4, tr, C), lambda r, cr: (0, r, 0))
    grid_spec = pltpu.PrefetchScalarGridSpec(
        num_scalar_prefetch=1, grid=(R // tr,),
        in_specs=[pl.BlockSpec((4, None, tr, C), lambda r, cr: (0, cr[0], r, 0)), blk], out_specs=blk)
    return pl.pallas_call(
        body, name=name, grid_spec=grid_spec, out_shape=jax.ShapeDtypeStruct((4, R, C), dtype),
        compiler_params=_params(1),
    )(core.reshape(1).astype(jnp.int32), full.reshape(4, 2, R, C), theirs)


def _scatter_chips(name, arrays):
    n = len(arrays)

    def body(*refs):
        ins, outs = refs[:n], refs[n:2 * n]
        send_sems, recv_sems, loc_sems = refs[2 * n:]
        me = _mesh_pos()
        mq = 2 * me[0] + me[1]
        peers = [_flip(me, 4), _flip(me, 2), _flip(me, 6)]

        def copy(w, k):
            peer = peers[k]
            return pltpu.make_async_remote_copy(
                src_ref=ins[w].at[2 * peer[0] + peer[1]], dst_ref=outs[w].at[mq], send_sem=send_sems.at[w, k],
                recv_sem=recv_sems.at[w, k], device_id=peer, device_id_type=MESH)

        def arrival(w, k):
            peer = peers[k]
            return pltpu.make_async_remote_copy(
                src_ref=ins[w].at[mq], dst_ref=outs[w].at[2 * peer[0] + peer[1]], send_sem=send_sems.at[w, k],
                recv_sem=recv_sems.at[w, k], device_id=peer, device_id_type=MESH)

        local = [pltpu.make_async_copy(ins[w].at[mq], outs[w].at[mq], loc_sems.at[w, 0]) for w in range(n)]
        for cp in local:
            cp.start()
        sends = [copy(w, k) for k in range(3) for w in range(n)]
        for cp in sends:
            cp.start()
        for k in range(3):
            for w in range(n):
                arrival(w, k).wait_recv()
        for cp in sends:
            cp.wait_send()
        for cp in local:
            cp.wait()

    out_shape = [jax.ShapeDtypeStruct(a.shape, a.dtype) for a in arrays]
    return _comm_call(name, body, arrays, out_shape, 3)


def _adamw(name, parts, w, m, v):
    R, C = w.shape
    n_parts = parts.shape[0]
    tr = R if R <= 256 else (256 if C <= 512 else 128)
    bc1 = 1.0 - ADAM_B1 ** ADAM_STEP
    bc2 = 1.0 - ADAM_B2 ** ADAM_STEP

    def f(p, wv, mv, vv):
        g = p[0].astype(f32)
        for i in range(1, n_parts):
            g = g + p[i].astype(f32)
        m2 = ADAM_B1 * mv + (1.0 - ADAM_B1) * g
        v2 = ADAM_B2 * vv + (1.0 - ADAM_B2) * jnp.square(g)
        delta = -ADAM_LR * ((m2 / bc1) / (jnp.sqrt(v2 / bc2) + ADAM_EPS) + ADAM_WD * wv)
        return g, delta, m2, v2

    blk = ((tr, C), lambda r, z: (r, 0))
    return _ew(name, f, [(parts, (n_parts, tr, C), lambda r, z: (0, r, 0)), (w, *blk), (m, *blk), (v, *blk)],
               [((R, C), f32, *blk, None)] * 4, (R // tr, 1))


_WEIGHTS = ["norm_mix_g", "norm_mlp_g", "mlp_w1", "mlp_w2", "gdn_w_in", "gdn_conv_w", "gdn_a_log", "gdn_dt_bias",
            "gdn_o_norm_g", "gdn_w_out", "s5_w_in", "s5_lam_re", "s5_lam_im", "s5_log_dt", "s5_b_re", "s5_b_im",
            "s5_c_re", "s5_c_im", "s5_d", "s5_w_out", "m2_w_in", "m2_conv_w", "m2_conv_b", "m2_dt_bias", "m2_a_log",
            "m2_d", "m2_norm_g", "m2_w_out", "final_norm_g"]
_SHARDED = ["mlp_w1", "mlp_w2", "gdn_w_in", "gdn_w_out", "s5_w_in", "s5_w_out", "m2_w_in", "m2_w_out",
            "gdn_conv_w", "m2_conv_w", "m2_conv_b", "m2_norm_g"]
_MATRICES = _SHARDED[:8]
_REPLICATED = [n for n in _WEIGHTS if n not in _SHARDED]
_GDN_IN, _M2_IN = 4112, 6176
_LAYER_KIND = (0, 1, 2, 0)


def _as2d(a):
    return a.reshape(-1, a.shape[-1])


def _cols_from_shards(g, width):
    return g.transpose(1, 0, 2).reshape(g.shape[1], width)


def _cols_to_shards(a, width):
    return a[:, :width].reshape(a.shape[0], N_DEV, width // N_DEV).transpose(1, 0, 2)


def kernel(x, norm_mix_g, norm_mlp_g, mlp_w1, mlp_w2, gdn_w_in, gdn_conv_w, gdn_a_log, gdn_dt_bias, gdn_o_norm_g, gdn_w_out, s5_w_in, s5_lam_re, s5_lam_im, s5_log_dt, s5_b_re, s5_b_im, s5_c_re, s5_c_im, s5_d, s5_w_out, m2_w_in, m2_conv_w, m2_conv_b, m2_dt_bias, m2_a_log, m2_d, m2_norm_g, m2_w_out, final_norm_g, loss_target, m_norm_mix_g, m_norm_mlp_g, m_mlp_w1, m_mlp_w2, m_gdn_w_in, m_gdn_conv_w, m_gdn_a_log, m_gdn_dt_bias, m_gdn_o_norm_g, m_gdn_w_out, m_s5_w_in, m_s5_lam_re, m_s5_lam_im, m_s5_log_dt, m_s5_b_re, m_s5_b_im, m_s5_c_re, m_s5_c_im, m_s5_d, m_s5_w_out, m_m2_w_in, m_m2_conv_w, m_m2_conv_b, m_m2_dt_bias, m_m2_a_log, m_m2_d, m_m2_norm_g, m_m2_w_out, m_final_norm_g, v_norm_mix_g, v_norm_mlp_g, v_mlp_w1, v_mlp_w2, v_gdn_w_in, v_gdn_conv_w, v_gdn_a_log, v_gdn_dt_bias, v_gdn_o_norm_g, v_gdn_w_out, v_s5_w_in, v_s5_lam_re, v_s5_lam_im, v_s5_log_dt, v_s5_b_re, v_s5_b_im, v_s5_c_re, v_s5_c_im, v_s5_d, v_s5_w_out, v_m2_w_in, v_m2_conv_w, v_m2_conv_b, v_m2_dt_bias, v_m2_a_log, v_m2_d, v_m2_norm_g, v_m2_w_out, v_final_norm_g):
    args = locals()
    W = {n: args[n] for n in _WEIGHTS}
    MOM = {n: args["m_" + n] for n in _WEIGHTS}
    VAR = {n: args["v_" + n] for n in _WEIGHTS}
    h = x[0]
    target = loss_target[0]
    L, D = h.shape

    first = [mlp_w1[0:1].astype(bf16), mlp_w2[0:1].astype(bf16), gdn_w_in[0:1].astype(bf16),
             gdn_w_out[0:1].astype(bf16), _as2d(gdn_conv_w), _as2d(m2_conv_w), _as2d(m2_conv_b), _as2d(m2_norm_g)]
    w1g0, w2g0, gin0, gout0, gconv, m2_cw, m2_cbg, m2_ngg = _gather("gather_first", first)
    stacked = jnp.concatenate([gdn_w_out[1], s5_w_in[0], m2_w_out[0]], axis=0).astype(bf16)
    rest = [mlp_w1[1:4].astype(bf16), mlp_w2[1:4].astype(bf16), gdn_w_in[1:2].astype(bf16), s5_w_out.astype(bf16),
            m2_w_in.astype(bf16), stacked]
    lands = [lax.empty((N_DEV,) + a.shape, a.dtype) for a in rest]
    send_sems, recv_sems, rest_thru, lands_thru, token = _gather_start("gather_rest_start", rest, lands)

    def gdn_weights(gin, gout, conv, j):
        return (jnp.pad(_cols_from_shards(gin[:, 0], _GDN_IN), ((0, 0), (0, GDN_EXT - _GDN_IN))),
                gout[:, 0].reshape(D, D), _cols_from_shards(conv[:, 4 * j:4 * j + 4], 3 * D))

    gdn_in, gdn_out, gdn_conv = [None, None], [None, None], [None, None]
    gdn_in[0], gdn_out[0], gdn_conv[0] = gdn_weights(gin0, gout0, gconv, 0)

    norm_mix = [norm_mix_g[i].reshape(1, D) for i in range(4)]
    norm_mix[0] = norm_mix[0] + token[0, 0]
    late = {}

    def mixer_fwd(i, hv):
        kind, j = _LAYER_KIND[i], i // 3
        gn = norm_mix[i]
        s5_in, s5_out_g = late.get("s5_in"), late.get("s5_out_g")
        m2_in, m2_conv, m2_cb, m2_ng, m2_out = (late.get(k) for k in ("m2_in", "m2_conv", "m2_cb", "m2_ng", "m2_out"))
        if kind == 0:
            return _gdn_fwd(hv, gn, gdn_in[j], gdn_conv[j], gdn_a_log[j], gdn_dt_bias[j], gdn_o_norm_g[j], gdn_out[j])
        if kind == 1:
            return _s5_fwd(hv, gn, s5_in, s5_lam_re[0], s5_lam_im[0], s5_log_dt[0], s5_b_re[0], s5_b_im[0],
                           s5_c_re[0], s5_c_im[0], s5_d[0], s5_out_g)
        return _m2_fwd(hv, gn, m2_in, m2_conv, m2_cb, m2_dt_bias[0], m2_a_log[0], m2_d[0], m2_ng, m2_out)

    def mixer_bwd(i, dh, hv, sv):
        kind, j = _LAYER_KIND[i], i // 3
        gn = norm_mix[i]
        s5_in, s5_out_g = late["s5_in"], late["s5_out_g"]
        m2_in, m2_conv, m2_cb, m2_out = (late[k] for k in ("m2_in", "m2_conv", "m2_cb", "m2_out"))
        if kind == 0:
            return _gdn_bwd(dh, hv, gn, gdn_in[j], gdn_conv[j], gdn_out[j], sv)
        if kind == 1:
            return _s5_bwd(dh, hv, gn, s5_in, s5_out_g, sv)
        return _m2_bwd(dh, hv, gn, m2_in, m2_conv, m2_cb, m2_out, sv)

    tape = []
    mlp_w = [(w1g0, w2g0, 0)]
    for i in range(4):
        if i == 1:
            landed = _gather_wait("gather_rest_wait", rest_thru, lands_thru, send_sems, recv_sems, h)
            w1gr, w2gr, gin1, s5_out_g, m2_in_g, rows_g = _gather_forward("gather_rest_forward", landed)
            gout1, s5_in_g, m2_out_g = rows_g[:, None, 0:128], rows_g[:, 128:256], rows_g[:, 256:512]
            mlp_w += [(w1gr, w2gr, k) for k in range(3)]
            gdn_in[1], gdn_out[1], gdn_conv[1] = gdn_weights(gin1, gout1, gconv, 1)
            late.update(
                s5_in=s5_in_g.reshape(D, D), s5_out_g=s5_out_g,
                m2_in=jnp.pad(_cols_from_shards(m2_in_g[:, 0], _M2_IN), ((0, 0), (0, M2_EXT - _M2_IN))),
                m2_out=m2_out_g.reshape(M2_INNER, D), m2_conv=_cols_from_shards(m2_cw, 2 * M2_INNER),
                m2_cb=_cols_from_shards(m2_cbg, 2 * M2_INNER), m2_ng=_cols_from_shards(m2_ngg, M2_INNER))
        h_mid, sv = mixer_fwd(i, h)
        h_next, hn, h1 = _mlp_fwd(h_mid, norm_mlp_g[i].reshape(1, D), *mlp_w[i])
        tape.append((h, sv, h_mid, hn, h1))
        h = h_next
    loss_row, dh, d_final = _loss_head(h, final_norm_g.reshape(1, D), target)
    loss = lax.psum(loss_row[0, 0], ("x", "y", "c"))

    dw1 = lax.empty((N_DEV, 4, D, D_FF // N_DEV), f32)
    dw2 = lax.empty((N_DEV, 4, D_FF // N_DEV, D), f32)
    d_mix, d_mlp, mg = [None] * 4, [None] * 4, [None] * 4
    for i in reversed(range(4)):
        h_in, sv, h_mid, hn, h1 = tape[i]
        dh, d_mlp[i], dw1, dw2 = _mlp_bwd(dh, h_mid, norm_mlp_g[i].reshape(1, D), hn, h1, *mlp_w[i], i, dw1, dw2)
        dh, mg[i] = mixer_bwd(i, dh, h_in, sv)
        d_mix[i] = mg[i]["norm"]
    grad_x = dh.reshape(1, L, D)
    ga, gb_, s5g, m2g = mg[0], mg[3], mg[1], mg[2]

    full = {
        "mlp_w1": dw1, "mlp_w2": dw2,
        "gdn_w_in": jnp.stack([_cols_to_shards(g["w_ext"], _GDN_IN) for g in (ga, gb_)], axis=1),
        "gdn_w_out": jnp.stack([g["w_out"].reshape(N_DEV, D // N_DEV, D) for g in (ga, gb_)], axis=1),
        "s5_w_in": s5g["w_in"].reshape(N_DEV, 1, D // N_DEV, D), "s5_w_out": s5g["w_out"],
        "m2_w_in": _cols_to_shards(m2g["w_ext"], _M2_IN)[:, None],
        "m2_w_out": m2g["w_out"].reshape(N_DEV, 1, M2_INNER // N_DEV, D),
        "gdn_conv_w": jnp.concatenate([_cols_to_shards(g["conv_w"], 3 * D) for g in (ga, gb_)], axis=1),
        "m2_conv_w": _cols_to_shards(m2g["conv_w"], 2 * M2_INNER),
        "m2_conv_b": _cols_to_shards(m2g["conv_b"], 2 * M2_INNER),
        "m2_norm_g": _cols_to_shards(m2g["norm_g"], M2_INNER),
    }
    sends = [full[n].reshape((N_DEV,) + _as2d(W[n]).shape) for n in _SHARDED]
    core = lax.axis_index("c")
    theirs = _scatter_pair("scatter_pair", sends)
    chip_sums = [_pair_add("pair_add_" + n, full8, th, core, bf16 if n in _MATRICES else f32)
                 for n, full8, th in zip(_SHARDED, sends, theirs)]
    parts = dict(zip(_SHARDED, _scatter_chips("scatter_chips", chip_sums)))

    rep = {
        "norm_mix_g": jnp.concatenate(d_mix, axis=0), "norm_mlp_g": jnp.concatenate(d_mlp, axis=0),
        "gdn_a_log": jnp.stack([ga["a_log"], gb_["a_log"]]), "gdn_dt_bias": jnp.stack([ga["dt_bias"], gb_["dt_bias"]]),
        "gdn_o_norm_g": jnp.stack([ga["o_norm_g"], gb_["o_norm_g"]]),
        "s5_lam_re": s5g["lam_re"], "s5_lam_im": s5g["lam_im"], "s5_log_dt": s5g["log_dt"], "s5_b_re": s5g["b_re"],
        "s5_b_im": s5g["b_im"], "s5_c_re": s5g["c_re"], "s5_c_im": s5g["c_im"], "s5_d": s5g["d"],
        "m2_dt_bias": m2g["dt_bias"], "m2_a_log": m2g["a_log"], "m2_d": m2g["d"], "final_norm_g": d_final,
    }

    def pack(d):
        flat = jnp.concatenate([d[n].reshape(-1).astype(f32) for n in _REPLICATED])
        return jnp.pad(flat, (0, -flat.shape[0] % (256 * 128))).reshape(-1, 128)

    (rep_parts,) = _gather("gather_small_grads", [pack(rep)])

    res = {}
    for n in _SHARDED:
        w2d = _as2d(W[n])
        out = _adamw("adamw_" + n, parts[n], w2d, _as2d(MOM[n]), _as2d(VAR[n]))
        res[n] = [o.reshape(W[n].shape) for o in out]
    out = _adamw("adamw_replicated", rep_parts, pack(W), pack(MOM), pack(VAR))
    off = 0
    for n in _REPLICATED:
        size = W[n].size
        res[n] = [o.reshape(-1)[off:off + size].reshape(W[n].shape) for o in out]
        off += size

    return (loss, grad_x, *[res[n][0] for n in _WEIGHTS], *[res[n][1] for n in _WEIGHTS],
            *[res[n][2] for n in _WEIGHTS], *[res[n][3] for n in _WEIGHTS])
```

```python
import functools

import jax
import jax.numpy as jnp
from jax import lax
from jax.experimental import pallas as pl
from jax.experimental.pallas import tpu as pltpu

f32 = jnp.float32
bf16 = jnp.bfloat16
HI = lax.Precision.HIGHEST
MESH = pl.DeviceIdType.MESH

N_DEV = 8
D_MODEL = 1024
D_FF = 4096
CHUNK = 64
RMS_EPS = 1e-6
GDN_HEADS = 8
GDN_HB = 8
GDN_EXT = 4224
S5_STATE = 64
S5_SCAN_LANES = 256
M2_INNER = 2048
M2_EXT = 6272
M2_HEADS = 32
M2_GB = 4
VMEM_LIMIT_BYTES = 56 * 1024 * 1024

ADAM_LR, ADAM_B1, ADAM_B2, ADAM_EPS, ADAM_WD, ADAM_STEP = 0.001, 0.9, 0.999, 1e-08, 0.01, 10

_NN = ((1,), (0,))
_NT = ((1,), (1,))
_TN = ((0,), (0,))


def _dot(a, b, dims=_NN):
    return lax.dot_general(a, b, (dims, ((), ())), precision=HI, preferred_element_type=f32)


def _dotb(a, b, dims=_NN):
    return lax.dot_general(a.astype(bf16), b.astype(bf16), (dims, ((), ())), preferred_element_type=f32)


def _bdot(p, q, dims):
    return lax.dot_general(p, q, (dims, ((), ())), preferred_element_type=f32)


def _pieces(x, n):
    out = []
    for _ in range(n - 1):
        p = x.astype(bf16)
        out.append(p)
        x = x - p.astype(f32)
    return out + [x.astype(bf16)]


def _dot01_raw(mask, b, dims=_NN, mask_first=True):
    m = mask.astype(bf16)
    p = _pieces(b, 3)
    if mask_first:
        return _bdot(m, p[0], dims) + (_bdot(m, p[1], dims) + _bdot(m, p[2], dims))
    return _bdot(p[0], m, dims) + (_bdot(p[1], m, dims) + _bdot(p[2], m, dims))


@jax.custom_vjp
def _dot01_nn(mask, b):
    return _dot01_raw(mask, b, _NN)


@jax.custom_vjp
def _dot01_nt(mask, b):
    return _dot01_raw(mask, b, _NT)


_dot01_nn.defvjp(lambda m, b: (_dot01_raw(m, b, _NN), m),
                 lambda m, ct: (jnp.zeros_like(m), _dot01_raw(m, ct, _TN, mask_first=True)))
_dot01_nt.defvjp(lambda m, b: (_dot01_raw(m, b, _NT), m),
                 lambda m, ct: (jnp.zeros_like(m), _dot01_raw(m, ct, _TN, mask_first=False)))


def _dot01_vjp(mask, b, dims=_NN):
    return _dot01_nn(mask, b) if dims == _NN else _dot01_nt(mask, b)


def _dot3_raw(a, b, dims=_NN):
    (ah, al), (bh, bl) = _pieces(a, 2), _pieces(b, 2)
    return _bdot(ah, bh, dims) + (_bdot(ah, bl, dims) + _bdot(al, bh, dims))


@jax.custom_vjp
def _dot3_vjp(a, b):
    return _dot3_raw(a, b)


_dot3_vjp.defvjp(lambda a, b: (_dot3_raw(a, b), (a, b)),
                 lambda res, ct: (_dot3_raw(ct, res[1], _NT), _dot3_raw(res[0], ct, _TN)))


class _Dots:
    def __init__(self, dot3, dot01):
        self.dot3, self.dot01 = dot3, dot01


_PLAIN_DOTS = _Dots(_dot3_raw, _dot01_raw)
_VJP_DOTS = _Dots(_dot3_vjp, _dot01_vjp)


def _iota(shape, dim):
    return lax.broadcasted_iota(jnp.int32, shape, dim)


def _params(n_grid):
    return pltpu.CompilerParams(dimension_semantics=("arbitrary",) * n_grid, vmem_limit_bytes=VMEM_LIMIT_BYTES)


def _row_tile(n_rows):
    return min(512, n_rows)


def _head_rows(n_rows):
    return min(2048, n_rows)


def _mm_rows(n_rows):
    return min(1024, n_rows)


def _col_tile(n, cap=1024):
    best = 128
    for t in range(128, cap + 1, 128):
        if n % t == 0:
            best = t
    return best


def _mm(name, a, b, *, dims, grid, a_spec, b_spec, out_shape, out_spec, aux=(), a_fn=None, epi_fn=None,
        acc_shape, out_init=None, cache_a=False):
    nk = grid[2]
    n_aux = len(aux)
    kinds = [x[2] for x in aux]
    cache_a = cache_a and nk == 1 and grid[1] > 1 and not any(kd == "a" for kd in kinds)

    def body_single(*refs):
        a_ref, b_ref = refs[0], refs[1]
        aux_refs = refs[2:2 + n_aux]
        pos = 2 + n_aux + (1 if out_init is not None else 0)
        o_ref = refs[pos]

        def a_tile():
            av = a_ref[...]
            if a_fn is not None:
                av = a_fn(av, *[r[...] for r, kd in zip(aux_refs, kinds) if kd == "a"])
            return av.astype(bf16)

        if cache_a:
            a_bf = refs[pos + 1]

            @pl.when(pl.program_id(1) == 0)
            def _():
                a_bf[...] = a_tile()

            av = a_bf[...]
        else:
            av = a_tile()
        r = lax.dot_general(av, b_ref[...].astype(bf16), (dims, ((), ())), preferred_element_type=f32)
        if epi_fn is not None:
            r = epi_fn(r, *[x[...] for x, kd in zip(aux_refs, kinds) if kd == "e"])
        o_ref[...] = r.astype(o_ref.dtype)

    def body(*refs):
        a_ref, b_ref = refs[0], refs[1]
        aux_refs = refs[2:2 + n_aux]
        pos = 2 + n_aux + (1 if out_init is not None else 0)
        o_ref, acc_ref = refs[pos], refs[pos + 1]
        k = pl.program_id(2)

        @pl.when(k == 0)
        def _():
            acc_ref[...] = jnp.zeros_like(acc_ref)

        av = a_ref[...]
        if a_fn is not None:
            av = a_fn(av, *[r[...] for r, kd in zip(aux_refs, kinds) if kd == "a"])
        acc_ref[...] += lax.dot_general(av.astype(bf16), b_ref[...].astype(bf16), (dims, ((), ())),
                                        preferred_element_type=f32)

        @pl.when(k == nk - 1)
        def _():
            r = acc_ref[...]
            if epi_fn is not None:
                r = epi_fn(r, *[x[...] for x, kd in zip(aux_refs, kinds) if kd == "e"])
            o_ref[...] = r.astype(o_ref.dtype)

    in_specs = [a_spec, b_spec] + [x[1] for x in aux]
    args = [a, b] + [x[0] for x in aux]
    aliases = {}
    if out_init is not None:
        in_specs.append(pl.BlockSpec(memory_space=pl.ANY))
        args.append(out_init)
        aliases = {len(args) - 1: 0}
    if nk == 1:
        a_block = tuple(d for d in a_spec.block_shape if d is not None)
        scratch = [pltpu.VMEM(a_block, bf16)] if cache_a else []
    else:
        scratch = [pltpu.VMEM(acc_shape, f32)]
    return pl.pallas_call(
        body_single if nk == 1 else body, name=name, grid=grid, in_specs=in_specs, out_specs=out_spec,
        out_shape=out_shape, scratch_shapes=scratch, input_output_aliases=aliases, compiler_params=_params(3),
    )(*args)


def _ew(name, f, ins, outs, grid):
    n_in = len(ins)
    modes = [o[4] for o in outs]

    def body(*refs):
        vals = [r[...] for r in refs[:n_in]]
        res = f(*vals)
        if not isinstance(res, (tuple, list)):
            res = (res,)
        for r, o_ref, mode in zip(res, refs[n_in:], modes):
            if mode is None:
                o_ref[...] = r.astype(o_ref.dtype)
                continue
            first = pl.program_id(1) == 0
            if mode == "all":
                first = jnp.logical_and(first, pl.program_id(0) == 0)

            @pl.when(first)
            def _(r=r, o_ref=o_ref):
                o_ref[...] = r.astype(o_ref.dtype)

            @pl.when(jnp.logical_not(first))
            def _(r=r, o_ref=o_ref):
                o_ref[...] += r.astype(o_ref.dtype)

    res = pl.pallas_call(
        body, name=name, grid=grid,
        in_specs=[pl.BlockSpec(blk, im) for _, blk, im in ins],
        out_specs=[pl.BlockSpec(o[2], o[3]) for o in outs],
        out_shape=[jax.ShapeDtypeStruct(o[0], o[1]) for o in outs],
        compiler_params=_params(2),
    )(*[a for a, _, _ in ins])
    return res


def _vjp_fn(f, n_primal):
    def g(*args):
        _, vjp = jax.vjp(f, *args[:n_primal])
        cts = args[n_primal:]
        return vjp(cts[0] if len(cts) == 1 else tuple(cts))
    return g


def _scan_fwd(name, step, n_state, state_shape, cins, ins, outs, n_units, n_chunks):
    n_c, n_in, n_out = len(cins), len(ins), len(outs)

    def body(*refs):
        c_refs = refs[:n_c]
        in_refs = refs[n_c:n_c + n_in]
        out_refs = refs[n_c + n_in:n_c + n_in + n_out]
        saved = refs[n_c + n_in + n_out:n_c + n_in + n_out + n_state]
        st = refs[n_c + n_in + n_out + n_state:]

        @pl.when(pl.program_id(1) == 0)
        def _():
            for s in st:
                s[...] = jnp.zeros_like(s)

        cur = [s[...] for s in st]
        for sv, s in zip(saved, cur):
            sv[...] = s
        new, res = step(cur, [r[...] for r in c_refs], [r[...] for r in in_refs], _PLAIN_DOTS)
        for s, n in zip(st, new):
            s[...] = n
        for o, r in zip(out_refs, res):
            o[...] = r

    sshape = (n_units, n_chunks) + state_shape
    sblock = (None, None) + state_shape
    nz = len(state_shape)
    res = pl.pallas_call(
        body, name=name, grid=(n_units, n_chunks),
        in_specs=[pl.BlockSpec(e[1], e[2]) for e in cins + ins],
        out_specs=[pl.BlockSpec(o[1], o[2]) for o in outs]
        + [pl.BlockSpec(sblock, lambda u, c: (u, c) + (0,) * nz)] * n_state,
        out_shape=[jax.ShapeDtypeStruct(o[0], f32) for o in outs]
        + [jax.ShapeDtypeStruct(sshape, f32)] * n_state,
        scratch_shapes=[pltpu.VMEM(state_shape, f32)] * n_state,
        compiler_params=_params(2),
    )(*[e[0] for e in cins + ins])
    return res[:n_out], res[n_out:]


def _scan_bwd(name, step, n_state, state_shape, cins, ins, saved, douts, n_units, n_chunks):
    n_c, n_in, n_do = len(cins), len(ins), len(douts)

    def flip(im):
        return lambda u, c: im(u, n_chunks - 1 - c)

    def body(*refs):
        p = 0
        c_refs = refs[p:p + n_c]; p += n_c
        in_refs = refs[p:p + n_in]; p += n_in
        sv_refs = refs[p:p + n_state]; p += n_state
        do_refs = refs[p:p + n_do]; p += n_do
        dc_refs = refs[p:p + n_c]; p += n_c
        di_refs = refs[p:p + n_in]; p += n_in
        dst = refs[p:]
        first = pl.program_id(1) == 0

        @pl.when(first)
        def _():
            for s in dst:
                s[...] = jnp.zeros_like(s)

        def fn(states, consts, vals):
            new, res = step(states, consts, vals, _VJP_DOTS)
            return tuple(new), tuple(res)

        prim = ([r[...] for r in sv_refs], [r[...] for r in c_refs], [r[...] for r in in_refs])
        _, vjp = jax.vjp(fn, *prim)
        d_states, d_consts, d_vals = vjp((tuple(s[...] for s in dst), tuple(r[...] for r in do_refs)))
        for s, g in zip(dst, d_states):
            s[...] = g
        for o, g in zip(di_refs, d_vals):
            o[...] = g
        for o, g in zip(dc_refs, d_consts):
            @pl.when(first)
            def _(o=o, g=g):
                o[...] = g

            @pl.when(jnp.logical_not(first))
            def _(o=o, g=g):
                o[...] += g

    nz = len(state_shape)
    sblock = (None, None) + state_shape
    def gshape(e):
        return e[3] if len(e) == 5 else e[0].shape

    def gmap(e):
        return e[4] if len(e) == 5 else e[2]

    in_specs = ([pl.BlockSpec(e[1], e[2]) for e in cins]
                + [pl.BlockSpec(e[1], flip(e[2])) for e in ins]
                + [pl.BlockSpec(sblock, lambda u, c: (u, n_chunks - 1 - c) + (0,) * nz)] * n_state
                + [pl.BlockSpec(e[1], flip(e[2])) for e in douts])
    out_specs = ([pl.BlockSpec(e[1], e[2]) for e in cins]
                 + [pl.BlockSpec(e[1], flip(gmap(e))) for e in ins])
    out_shape = [jax.ShapeDtypeStruct(gshape(e), f32) for e in cins + ins]
    res = pl.pallas_call(
        body, name=name, grid=(n_units, n_chunks), in_specs=in_specs, out_specs=out_specs, out_shape=out_shape,
        scratch_shapes=[pltpu.VMEM(state_shape, f32)] * n_state,
        compiler_params=_params(2),
    )(*([e[0] for e in cins + ins] + list(saved) + [e[0] for e in douts]))
    return res[:n_c], res[n_c:]


def _rms(x, g):
    return x * lax.rsqrt(jnp.mean(x * x, axis=-1, keepdims=True) + RMS_EPS) * g


def _rows(tm, width):
    return (tm, width), lambda r, z: (r, 0)


def _const(shape):
    return shape, lambda r, z: (0,) * len(shape)


def _rms_fwd(name, h, g):
    L, D = h.shape
    tm = _row_tile(L)
    return _ew(name, _rms, [(h, *_rows(tm, D)), (g, *_const((1, D)))],
               [((L, D), f32, *_rows(tm, D), None)], (L // tm, 1))[0]


def _rms_bwd(name, h, g, d_hn, d_res):
    L, D = h.shape
    tm = _row_tile(L)

    def f(hv, gv, dv, rv):
        dh, dg = _vjp_fn(_rms, 2)(hv, gv, dv)
        return dh + rv, dg

    return _ew(name, f, [(h, *_rows(tm, D)), (g, *_const((1, D))), (d_hn, *_rows(tm, D)), (d_res, *_rows(tm, D))],
               [((L, D), f32, *_rows(tm, D), None), ((1, D), f32, *_const((1, D)), "all")], (L // tm, 1))


def _loss_head(h, g, target):
    L, D = h.shape
    tm = _row_tile(L)

    def f(hv, gv, tv):
        def lf(a, b):
            e = jnp.square(_rms(a, b) - tv)
            return (0.5 / D) * jnp.sum(jnp.sum(e, axis=1, keepdims=True), axis=0, keepdims=True)

        val, vjp = jax.vjp(lf, hv, gv)
        dh, dg = vjp(jnp.ones((1, 1), f32))
        return jnp.broadcast_to(val, (1, 128)), dh, dg

    return _ew("loss_head", f, [(h, *_rows(tm, D)), (g, *_const((1, D))), (target, *_rows(tm, D))],
               [((1, 128), f32, *_const((1, 128)), "all"), ((L, D), f32, *_rows(tm, D), None),
                ((1, D), f32, *_const((1, D)), "all")], (L // tm, 1))


def _sqrelu(x):
    return jnp.square(jnp.maximum(x, 0.0))


def _mm_plain(name, a, b, dims, *, a_fn=None, epi_fn=None, aux=()):
    if dims == _NN:
        (M, K), N = a.shape, b.shape[1]
    elif dims == _NT:
        (M, K), N = a.shape, b.shape[0]
    else:
        (K, M), N = a.shape, b.shape[1]
    tm = _mm_rows(M)
    tn = _col_tile(N, 1536)
    tk = _col_tile(K)
    if dims == _TN:
        tk = min(512, K)
        a_spec = pl.BlockSpec((tk, tm), lambda i, j, k: (k, i))
        b_spec = pl.BlockSpec((tk, tn), lambda i, j, k: (k, j))
        a_aux = pl.BlockSpec((tk, tm), lambda i, j, k: (k, i))
    elif dims == _NT:
        a_spec = pl.BlockSpec((tm, tk), lambda i, j, k: (i, k))
        b_spec = pl.BlockSpec((tn, tk), lambda i, j, k: (j, k))
        a_aux = pl.BlockSpec((tm, tk), lambda i, j, k: (i, k))
    else:
        a_spec = pl.BlockSpec((tm, tk), lambda i, j, k: (i, k))
        b_spec = pl.BlockSpec((tk, tn), lambda i, j, k: (k, j))
        a_aux = pl.BlockSpec((tm, tk), lambda i, j, k: (i, k))
    e_aux = pl.BlockSpec((tm, tn), lambda i, j, k: (i, j))
    aux_full = [(x, a_aux if kd == "a" else e_aux, kd) for x, kd in aux]
    return _mm(name, a, b, dims=dims, grid=(M // tm, N // tn, K // tk), a_spec=a_spec, b_spec=b_spec,
               out_shape=jax.ShapeDtypeStruct((M, N), f32), out_spec=pl.BlockSpec((tm, tn), lambda i, j, k: (i, j)),
               aux=aux_full, a_fn=a_fn, epi_fn=epi_fn, acc_shape=(tm, tn), cache_a=True)


def _mlp_fwd(h, g, w1g, w2g, layer):
    L, D = h.shape
    tm = _mm_rows(L)
    fs = D_FF // N_DEV
    hn = _rms_fwd("mlp_norm", h, g)
    h1 = _mm("mlp_up", hn, w1g, dims=_NN, grid=(L // tm, N_DEV, 1),
             a_spec=pl.BlockSpec((tm, D), lambda i, j, k: (i, 0)),
             b_spec=pl.BlockSpec((None, None, D, fs), lambda i, j, k: (j, layer, 0, 0)),
             out_shape=jax.ShapeDtypeStruct((L, D_FF), f32), out_spec=pl.BlockSpec((tm, fs), lambda i, j, k: (i, j)),
             acc_shape=(tm, fs), cache_a=True)
    tn = D
    h_out = _mm("mlp_down", h1, w2g, dims=_NN, grid=(L // tm, D // tn, N_DEV),
                a_spec=pl.BlockSpec((tm, fs), lambda i, j, k: (i, k)),
                b_spec=pl.BlockSpec((None, None, fs, tn), lambda i, j, k: (k, layer, 0, j)),
                out_shape=jax.ShapeDtypeStruct((L, D), f32), out_spec=pl.BlockSpec((tm, tn), lambda i, j, k: (i, j)),
                aux=[(h, pl.BlockSpec((tm, tn), lambda i, j, k: (i, j)), "e")],
                a_fn=_sqrelu, epi_fn=lambda acc, res: acc + res, acc_shape=(tm, tn))
    return h_out, hn, h1


def _mlp_bwd(dh, h, g, hn, h1, w1g, w2g, wl, layer, dw1_buf, dw2_buf):
    L, D = h.shape
    tm = _mm_rows(L)
    fs = D_FF // N_DEV
    tk = _mm_rows(L)
    dh1 = _mm("mlp_down_dx", dh, w2g, dims=_NT, grid=(L // tm, N_DEV, 1),
              a_spec=pl.BlockSpec((tm, D), lambda i, j, k: (i, 0)),
              b_spec=pl.BlockSpec((None, None, fs, D), lambda i, j, k: (j, wl, 0, 0)),
              out_shape=jax.ShapeDtypeStruct((L, D_FF), f32), out_spec=pl.BlockSpec((tm, fs), lambda i, j, k: (i, j)),
              aux=[(h1, pl.BlockSpec((tm, fs), lambda i, j, k: (i, j)), "e")],
              epi_fn=lambda acc, pre: acc * (2.0 * jnp.maximum(pre, 0.0)), acc_shape=(tm, fs), cache_a=True)
    dw2_buf = _mm("mlp_down_dw", h1, dh, dims=_TN, grid=(N_DEV, 1, L // tk),
                  a_spec=pl.BlockSpec((tk, fs), lambda i, j, k: (k, i)),
                  b_spec=pl.BlockSpec((tk, D), lambda i, j, k: (k, 0)),
                  out_shape=jax.ShapeDtypeStruct(dw2_buf.shape, f32),
                  out_spec=pl.BlockSpec((None, None, fs, D), lambda i, j, k: (i, layer, 0, 0)),
                  a_fn=_sqrelu, acc_shape=(fs, D), out_init=dw2_buf)
    tr = D
    dw1_buf = _mm("mlp_up_dw", hn, dh1, dims=_TN, grid=(D // tr, N_DEV, L // tk),
                  a_spec=pl.BlockSpec((tk, tr), lambda i, j, k: (k, i)),
                  b_spec=pl.BlockSpec((tk, fs), lambda i, j, k: (k, j)),
                  out_shape=jax.ShapeDtypeStruct(dw1_buf.shape, f32),
                  out_spec=pl.BlockSpec((None, None, tr, fs), lambda i, j, k: (j, layer, i, 0)),
                  acc_shape=(tr, fs), out_init=dw1_buf)
    tn = D
    dhn = _mm("mlp_up_dx", dh1, w1g, dims=_NT, grid=(L // tm, D // tn, N_DEV),
              a_spec=pl.BlockSpec((tm, fs), lambda i, j, k: (i, k)),
              b_spec=pl.BlockSpec((None, None, tn, fs), lambda i, j, k: (k, wl, j, 0)),
              out_shape=jax.ShapeDtypeStruct((L, D), f32), out_spec=pl.BlockSpec((tm, tn), lambda i, j, k: (i, j)),
              acc_shape=(tm, tn))
    dh_in, dg = _rms_bwd("mlp_norm_bwd", h, g, dhn, dh)
    return dh_in, dg, dw1_buf, dw2_buf


def _shift_dn(x, s, row):
    return x if s == 0 else jnp.where(row >= s, pltpu.roll(x, s, 0), 0.0)


def _shift_up(x, s, row):
    n = x.shape[0]
    return x if s == 0 else jnp.where(row < n - s, pltpu.roll(x, n - s, 0), 0.0)


def _conv_pre(x, w, b, row):
    c = jnp.broadcast_to(b, x.shape)
    for j in range(4):
        c = c + w[j:j + 1, :] * _shift_dn(x, 3 - j, row)
    return c


def _conv_fwd(name, x_arr, blk_off, w, b):
    L = x_arr.shape[0]
    C = w.shape[1]

    def f(x, wv, bv):
        c = _conv_pre(x, wv, bv, _iota(x.shape, 0))
        return c * jax.nn.sigmoid(c)

    return _ew(name, f, [(x_arr, (L, 128), lambda j, z: (0, blk_off + j)), (w, (4, 128), lambda j, z: (0, j)),
                         (b, (1, 128), lambda j, z: (0, j))],
               [((L, C), f32, (L, 128), lambda j, z: (0, j), None)], (C // 128, 1))[0]


def _conv_bwd(name, x_arr, blk_off, w, b, dy):
    L = x_arr.shape[0]
    C = w.shape[1]

    def f(x, wv, bv, g):
        row = _iota(x.shape, 0)
        c = _conv_pre(x, wv, bv, row)
        s = jax.nn.sigmoid(c)
        dc = g * (s * (1.0 + c * (1.0 - s)))
        dx = jnp.zeros_like(x)
        dw = jnp.zeros((4, 128), f32)
        r4 = _iota((4, 128), 0)
        for j in range(4):
            dx = dx + wv[j:j + 1, :] * _shift_up(dc, 3 - j, row)
            dwj = jnp.sum(dc * _shift_dn(x, 3 - j, row), axis=0, keepdims=True)
            dw = dw + jnp.where(r4 == j, jnp.broadcast_to(dwj, (4, 128)), 0.0)
        return dx, dw, jnp.sum(dc, axis=0, keepdims=True)

    return _ew(name, f, [(x_arr, (L, 128), lambda j, z: (0, blk_off + j)), (w, (4, 128), lambda j, z: (0, j)),
                         (b, (1, 128), lambda j, z: (0, j)), (dy, (L, 128), lambda j, z: (0, j))],
               [((L, C), f32, (L, 128), lambda j, z: (0, j), None), ((4, C), f32, (4, 128), lambda j, z: (0, j), None),
                ((1, C), f32, (1, 128), lambda j, z: (0, j), None)], (C // 128, 1))


def _l2norm(t):
    return t * lax.rsqrt(jnp.sum(t * t, axis=-1, keepdims=True) + 1e-6)


def _gdn_act(cq, ck, ab, alog, dtb):
    h = pl.program_id(1)
    qn = _l2norm(cq) * (128.0 ** -0.5)
    kn = _l2norm(ck)
    lane = _iota(ab.shape, 1)
    a_raw = jnp.sum(jnp.where(lane == h, ab, 0.0), axis=1, keepdims=True)
    b_raw = jnp.sum(jnp.where(lane == h + GDN_HEADS, ab, 0.0), axis=1, keepdims=True)
    lane1 = _iota(alog.shape, 1)
    al = jnp.sum(jnp.where(lane1 == h, alog, 0.0), axis=1, keepdims=True)
    db = jnp.sum(jnp.where(lane1 == h, dtb, 0.0), axis=1, keepdims=True)
    g = -jnp.exp(al) * jax.nn.softplus(a_raw + db)
    beta = jax.nn.sigmoid(b_raw)
    return qn, kn, jnp.broadcast_to(g, cq.shape), jnp.broadcast_to(beta, cq.shape)


def _each(f, *lists):
    return [f(*a) for a in zip(*lists)]


def _gdn_chunk(states, consts, vals, dots):
    S = list(states)
    cut = [slice(128 * i, 128 * i + 128) for i in range(len(S))]
    q, k, v, gb, bb = ([t[:, c] for c in cut] for t in vals)
    C = vals[0].shape[0]
    row, col = _iota((C, C), 0), _iota((C, C), 1)
    causal, strict = row >= col, row > col
    ltri = causal.astype(f32)
    eye = (row == col).astype(f32)
    e0 = (_iota((C, 128), 1) == 0).astype(f32)
    last = _iota((C, 1), 0) == C - 1
    Gb = _each(lambda g: dots.dot01(ltri, g), gb)
    Gc = _each(lambda g: jnp.mean(g, axis=1, keepdims=True), Gb)
    Gr = _each(lambda g: dots.dot01(e0, g, _NT), Gb)
    bc = _each(lambda b: jnp.mean(b, axis=1, keepdims=True), bb)
    decay = _each(lambda gc, gr: jnp.where(causal, jnp.exp(jnp.where(causal, gc - gr, 0.0)), 0.0), Gc, Gr)
    kk = _each(lambda a: _dotb(a, a, _NT), k)
    A = _each(lambda b, x, d: jnp.where(strict, b * x * d, 0.0), bc, kk, decay)
    M = _each(lambda a: eye - a, A)
    P = _each(lambda a: dots.dot3(a, a), A)
    for it in range(5):
        M = _each(lambda m, p: m + dots.dot3(m, p), M, P)
        if it < 4:
            P = _each(lambda p: dots.dot3(p, p), P)
    eG = _each(jnp.exp, Gc)
    u = _each(lambda m, x, b: _dotb(m, x * b), M, v, bc)
    w = _each(lambda m, x, b, e: _dotb(m, x * (b * e)), M, k, bc, eG)
    qk = _each(lambda a, b, d: _dotb(a, b, _NT) * d, q, k, decay)
    g_last = _each(lambda gc: jnp.sum(jnp.where(last, gc, 0.0), axis=0, keepdims=True), Gc)
    v_new = _each(lambda a, b, s: a - _dotb(b, s), u, w, S)
    o = _each(lambda a, e, s, b, x: _dotb(a * e, s) + _dotb(b, x), q, eG, S, qk, v_new)
    S_new = _each(lambda gl, s, a, gc, x: jnp.exp(gl) * s + _dotb(a * jnp.exp(gl - gc), x, _TN), g_last, S, k, Gc, v_new)
    return S_new, [jnp.concatenate(o, axis=1)]


def _gdn_post(o, gate, g):
    return _rms(o, g) * (gate * jax.nn.sigmoid(gate))


def _pad_row(v):
    return jnp.pad(v.astype(f32), (0, 128 - v.shape[0])).reshape(1, 128)


def _gdn_fwd(h, g_norm, w_ext, conv_w, a_log, dt_bias, o_norm_g, w_out):
    L, D = h.shape
    tm = _head_rows(L)
    nc = L // CHUNK
    H = GDN_HEADS
    hn = _rms_fwd("mix_norm", h, g_norm)
    proj = _mm_plain("gdn_in", hn, w_ext, _NN)
    zb = jnp.zeros((1, 3 * D), f32)
    cq = _conv_fwd("gdn_conv", proj, 0, conv_w, zb)
    alog, dtb = _pad_row(a_log), _pad_row(dt_bias)
    act_ins = [(cq, (tm, 128), lambda r, hh: (r, hh)), (cq, (tm, 128), lambda r, hh: (r, H + hh)),
               (proj, (tm, 128), lambda r, hh: (r, 4 * H)), (alog, (1, 128), lambda r, hh: (0, 0)),
               (dtb, (1, 128), lambda r, hh: (0, 0))]
    qn, kn, gb, bb = _ew("gdn_act", _gdn_act, act_ins,
                         [((L, D), f32, (tm, 128), lambda r, hh: (r, hh), None)] * 4, (L // tm, H))
    cblk = (CHUNK, 128 * GDN_HB)
    core_ins = [(qn, cblk, lambda u, c: (c, u)), (kn, cblk, lambda u, c: (c, u)),
                (cq, cblk, lambda u, c: (c, 2 * H // GDN_HB + u), (L, D), lambda u, c: (c, u)),
                (gb, cblk, lambda u, c: (c, u)), (bb, cblk, lambda u, c: (c, u))]
    (o,), saved_s = _scan_fwd("gdn_core", _gdn_chunk, GDN_HB, (128, 128), [], core_ins,
                              [((L, D), cblk, lambda u, c: (c, u))], H // GDN_HB, nc)
    on = o_norm_g.reshape(1, 128)
    post_ins = [(o, (tm, 128), lambda r, hh: (r, hh)), (proj, (tm, 128), lambda r, hh: (r, 3 * H + hh)),
                (on, (1, 128), lambda r, hh: (0, 0))]
    y = _ew("gdn_post", _gdn_post, post_ins, [((L, D), f32, (tm, 128), lambda r, hh: (r, hh), None)], (L // tm, H))[0]
    h_out = _mm_plain("gdn_out", y, w_out, _NN, epi_fn=lambda acc, res: acc + res, aux=[(h, "e")])
    saved = dict(hn=hn, proj=proj, cq=cq, alog=alog, dtb=dtb, act_ins=act_ins, core_ins=core_ins, saved_s=saved_s,
                 post_ins=post_ins, y=y, zb=zb)
    return h_out, saved


def _gdn_bwd(dh, h, g_norm, w_ext, conv_w, w_out, sv):
    L, D = h.shape
    tm = _head_rows(L)
    nc = L // CHUNK
    H = GDN_HEADS
    dy = _mm_plain("gdn_out_dx", dh, w_out, _NT)
    dw_out = _mm_plain("gdn_out_dw", sv["y"], dh, _TN)
    hd = ((L, D), f32, (tm, 128), lambda r, hh: (r, hh), None)
    d_o, d_gate, d_on = _ew("gdn_post_bwd", _vjp_fn(_gdn_post, 3),
                            sv["post_ins"] + [(dy, (tm, 128), lambda r, hh: (r, hh))],
                            [hd, hd, ((1, 128), f32, (1, 128), lambda r, hh: (0, 0), "all")], (L // tm, H))
    cblk = (CHUNK, 128 * GDN_HB)
    _, (dqn, dkn, dv, dgb, dbb) = _scan_bwd("gdn_core_bwd", _gdn_chunk, GDN_HB, (128, 128), [], sv["core_ins"],
                                            sv["saved_s"], [(d_o, cblk, lambda u, c: (c, u))], H // GDN_HB, nc)
    cts = [(t, (tm, 128), lambda r, hh: (r, hh)) for t in (dqn, dkn, dgb, dbb)]
    row128 = ((1, 128), f32, (1, 128), lambda r, hh: (0, 0), "all")
    d_cq, d_ck, d_ab, d_alog, d_dtb = _ew(
        "gdn_act_bwd", _vjp_fn(_gdn_act, 5), sv["act_ins"] + cts,
        [hd, hd, ((L, 128), f32, (tm, 128), lambda r, hh: (r, 0), "inner"), row128, row128], (L // tm, H))
    d_conv_out = jnp.concatenate([d_cq, d_ck, dv], axis=1)
    d_conv_in, d_conv_w, _ = _conv_bwd("gdn_conv_bwd", sv["proj"], 0, conv_w, sv["zb"], d_conv_out)
    d_proj = jnp.concatenate([d_conv_in, d_gate, d_ab], axis=1)
    dw_ext = _mm_plain("gdn_in_dw", sv["hn"], d_proj, _TN)
    dhn = _mm_plain("gdn_in_dx", d_proj, w_ext, _NT)
    dh_in, dg = _rms_bwd("mix_norm_bwd", h, g_norm, dhn, dh)
    grads = dict(norm=dg, w_ext=dw_ext, conv_w=d_conv_w, a_log=d_alog[0, :H], dt_bias=d_dtb[0, :H],
                 o_norm_g=d_on[0], w_out=dw_out)
    return dh_in, grads


def _expand_lanes(row, width, rep):
    sel = ((_iota((128, width), 1) // rep) == _iota((128, width), 0)).astype(f32)
    return jnp.mean(_dot(jnp.broadcast_to(row, (8, 128)), sel), axis=0, keepdims=True)


def _s5_params(lre, lim, ldt, wbr, wbi):
    dt = jnp.exp(_expand_lanes(ldt, 512, S5_STATE))
    mag = jnp.exp(lre * dt)
    ang = lim * dt
    abr, abi = mag * jnp.cos(ang), mag * jnp.sin(ang)
    nr = abr - 1.0
    den = lre * lre + lim * lim
    cr = (nr * lre + abi * lim) / den
    ci = (abi * lre - nr * lim) / den
    return abr, abi, cr * wbr - ci * wbi, cr * wbi + ci * wbr


def _s5_scan(name, xr, xi, ar, ai, rev, want_prev):
    L, W = xr.shape
    nb = L // 8
    n_out = 4 if want_prev else 2

    def body(xr_ref, xi_ref, ar_ref, ai_ref, *outs):
        a_r = ar_ref[...]
        a_i = -ai_ref[...] if rev else ai_ref[...]

        def cm(p, q):
            return p[0] * q[0] - p[1] * q[1], p[0] * q[1] + p[1] * q[0]

        a1 = (a_r, a_i)
        a2 = cm(a1, a1)
        a3 = cm(a2, a1)
        a4 = cm(a2, a2)
        pw = [a1, a2, a3, a4, cm(a4, a1), cm(a4, a2), cm(a4, a3), cm(a4, a4)]
        blk8 = (8, S5_SCAN_LANES)
        row = _iota(blk8, 0)
        tab_r = jnp.zeros(blk8, f32)
        tab_i = jnp.zeros(blk8, f32)
        for t in range(8):
            idx = 7 - t if rev else t
            tab_r = jnp.where(row == idx, jnp.broadcast_to(pw[t][0], blk8), tab_r)
            tab_i = jnp.where(row == idx, jnp.broadcast_to(pw[t][1], blk8), tab_i)
        lv = [(d, jnp.broadcast_to(p[0], blk8), jnp.broadcast_to(p[1], blk8)) for d, p in ((1, a1), (2, a2), (4, a4))]

        def step(i, carry):
            cr, ci = carry
            blk = nb - 1 - i if rev else i
            r0 = pl.multiple_of(blk * 8, 8)
            x_r = xr_ref[pl.ds(r0, 8), :]
            x_i = xi_ref[pl.ds(r0, 8), :]
            for d, p_r, p_i in lv:
                if rev:
                    s_r = jnp.where(row < 8 - d, pltpu.roll(x_r, 8 - d, 0), 0.0)
                    s_i = jnp.where(row < 8 - d, pltpu.roll(x_i, 8 - d, 0), 0.0)
                else:
                    s_r = jnp.where(row >= d, pltpu.roll(x_r, d, 0), 0.0)
                    s_i = jnp.where(row >= d, pltpu.roll(x_i, d, 0), 0.0)
                x_r, x_i = x_r + p_r * s_r - p_i * s_i, x_i + p_r * s_i + p_i * s_r
            x_r, x_i = x_r + tab_r * cr - tab_i * ci, x_i + tab_r * ci + tab_i * cr
            outs[0][pl.ds(r0, 8), :] = x_r
            outs[1][pl.ds(r0, 8), :] = x_i
            if want_prev:
                outs[2][pl.ds(r0, 8), :] = jnp.where(row >= 1, pltpu.roll(x_r, 1, 0), cr)
                outs[3][pl.ds(r0, 8), :] = jnp.where(row >= 1, pltpu.roll(x_i, 1, 0), ci)
            e = 0 if rev else 7
            return jnp.broadcast_to(x_r[e:e + 1, :], blk8), jnp.broadcast_to(x_i[e:e + 1, :], blk8)

        lax.fori_loop(0, nb, step, (jnp.zeros(blk8, f32), jnp.zeros(blk8, f32)))

    per = 512 // S5_SCAN_LANES
    col = pl.BlockSpec((L, S5_SCAN_LANES), lambda q, z: (0, q))
    aspec = pl.BlockSpec((None, 1, S5_SCAN_LANES), lambda q, z: (q // per, 0, q % per))
    return pl.pallas_call(
        body, name=name, grid=(W // S5_SCAN_LANES, 1), in_specs=[col, col, aspec, aspec], out_specs=[col] * n_out,
        out_shape=[jax.ShapeDtypeStruct((L, W), f32)] * n_out, compiler_params=_params(2),
    )(xr, xi, ar, ai)


def _blockdiag(t, n_in, n_out):
    t4 = t.reshape(8, 8, n_in, n_out)
    return jnp.einsum("jaio,ab->jaibo", t4, jnp.eye(8, dtype=t.dtype)).reshape(8, 8 * n_in, 8 * n_out)


def _blockdiag_t(w, n_in, n_out):
    w5 = w.reshape(8, 8, n_in, 8, n_out)
    return jnp.einsum("jaibo,ab->jaio", w5, jnp.eye(8, dtype=w.dtype)).reshape(64, n_in, n_out)


def _glu(ag, h):
    n = ag.shape[1] // 2
    return h + ag[:, :n] * jax.nn.sigmoid(ag[:, n:])


def _s5_fwd(h, g_norm, w_in, lam_re, lam_im, log_dt, b_re, b_im, c_re, c_im, d_skip, w_out_g):
    L, D = h.shape
    tm, te = _mm_rows(L), _row_tile(L)
    W = 8 * 512
    hn = _rms_fwd("mix_norm", h, g_norm)
    u = _mm_plain("s5_in", hn, w_in, _NN)
    lre, lim = lam_re.reshape(8, 1, 512), lam_im.reshape(8, 1, 512)
    ldt = jnp.pad(log_dt.reshape(8, 1, 8), ((0, 0), (0, 0), (0, 120)))
    wbr = _blockdiag(b_re.transpose(0, 2, 1), 16, 64)
    wbi = _blockdiag(b_im.transpose(0, 2, 1), 16, 64)
    wcr = _blockdiag(c_re.transpose(0, 2, 1), 64, 16)
    wci = _blockdiag(c_im.transpose(0, 2, 1), 64, 16)
    jb = lambda shape: (shape, lambda j, z: (j, 0, 0))
    par_ins = [(lre, *jb((None, 1, 512))), (lim, *jb((None, 1, 512))), (ldt, *jb((None, 1, 128))),
               (wbr, *jb((None, 128, 512))), (wbi, *jb((None, 128, 512)))]
    abr, abi, bbr, bbi = _ew("s5_params", _s5_params, par_ins,
                             [((8, 1, 512), f32, *jb((None, 1, 512)), None)] * 2
                             + [((8, 128, 512), f32, *jb((None, 128, 512)), None)] * 2, (8, 1))

    def bu(name, wb):
        return _mm(name, u, wb, dims=_NN, grid=(L // tm, 8, 1),
                   a_spec=pl.BlockSpec((tm, 128), lambda i, j, k: (i, j)),
                   b_spec=pl.BlockSpec((None, 128, 512), lambda i, j, k: (j, 0, 0)),
                   out_shape=jax.ShapeDtypeStruct((L, W), f32), out_spec=pl.BlockSpec((tm, 512), lambda i, j, k: (i, j)),
                   acc_shape=(tm, 512))

    bur, bui = bu("s5_bu", bbr), bu("s5_bu", bbi)
    sr, si, pr, pi = _s5_scan("s5_scan", bur, bui, abr, abi, False, True)
    d_row = d_skip.reshape(1, D)
    cspec = dict(a_spec=pl.BlockSpec((tm, 512), lambda i, j, k: (i, j)),
                 b_spec=pl.BlockSpec((None, 512, 128), lambda i, j, k: (j, 0, 0)),
                 out_shape=jax.ShapeDtypeStruct((L, D), f32), out_spec=pl.BlockSpec((tm, 128), lambda i, j, k: (i, j)),
                 acc_shape=(tm, 128))
    e128 = pl.BlockSpec((tm, 128), lambda i, j, k: (i, j))
    pre1 = _mm("s5_c_re", sr, wcr, dims=_NN, grid=(L // tm, 8, 1), **cspec)
    pre = _mm("s5_c_im", si, wci, dims=_NN, grid=(L // tm, 8, 1),
              aux=[(pre1, e128, "e"), (u, e128, "e"), (d_row, pl.BlockSpec((1, 128), lambda i, j, k: (0, j)), "e")],
              epi_fn=lambda acc, p1, uu, dd: p1 - acc + dd * uu, **cspec)
    ws = D // N_DEV * 2
    ag = _mm("s5_out", pre, w_out_g, dims=_NN, grid=(L // tm, N_DEV, 1),
             a_spec=pl.BlockSpec((tm, D), lambda i, j, k: (i, 0)),
             b_spec=pl.BlockSpec((None, None, D, ws), lambda i, j, k: (j, 0, 0, 0)),
             out_shape=jax.ShapeDtypeStruct((L, 2 * D), f32), out_spec=pl.BlockSpec((tm, ws), lambda i, j, k: (i, j)),
             a_fn=jax.nn.gelu, acc_shape=(tm, ws), cache_a=True)
    h_out = _ew("s5_glu", _glu, [(ag, *_rows(te, 2 * D)), (h, *_rows(te, D))],
                [((L, D), f32, *_rows(te, D), None)], (L // te, 1))[0]
    saved = dict(hn=hn, u=u, par_ins=par_ins, abr=abr, abi=abi, bbr=bbr, bbi=bbi, wcr=wcr, wci=wci, sr=sr, si=si,
                 pr=pr, pi=pi, pre=pre, ag=ag, d_row=d_row)
    return h_out, saved


def _s5_bwd(dh, h, g_norm, w_in, w_out_g, sv):
    L, D = h.shape
    tm, te = _mm_rows(L), _row_tile(L)
    tk = min(512, L)
    W = 8 * 512
    ws = D // N_DEV * 2
    u, pre, d_row = sv["u"], sv["pre"], sv["d_row"]
    d_ag = _ew("s5_glu_bwd", lambda ag, hv, g: _vjp_fn(_glu, 2)(ag, hv, g)[0],
               [(sv["ag"], *_rows(te, 2 * D)), (h, *_rows(te, D)), (dh, *_rows(te, D))],
               [((L, 2 * D), f32, *_rows(te, 2 * D), None)], (L // te, 1))[0]
    tr = D
    dw_out = _mm("s5_out_dw", pre, d_ag, dims=_TN, grid=(D // tr, N_DEV, L // tk),
                 a_spec=pl.BlockSpec((tk, tr), lambda i, j, k: (k, i)),
                 b_spec=pl.BlockSpec((tk, ws), lambda i, j, k: (k, j)),
                 out_shape=jax.ShapeDtypeStruct((N_DEV, 1, D, ws), f32),
                 out_spec=pl.BlockSpec((None, None, tr, ws), lambda i, j, k: (j, 0, i, 0)),
                 a_fn=jax.nn.gelu, acc_shape=(tr, ws))
    tn = 512
    dpre = _mm("s5_out_dx", d_ag, w_out_g, dims=_NT, grid=(L // tm, D // tn, N_DEV),
               a_spec=pl.BlockSpec((tm, ws), lambda i, j, k: (i, k)),
               b_spec=pl.BlockSpec((None, None, tn, ws), lambda i, j, k: (k, 0, j, 0)),
               out_shape=jax.ShapeDtypeStruct((L, D), f32), out_spec=pl.BlockSpec((tm, tn), lambda i, j, k: (i, j)),
               aux=[(pre, pl.BlockSpec((tm, tn), lambda i, j, k: (i, j)), "e")],
               epi_fn=lambda acc, p: _vjp_fn(jax.nn.gelu, 1)(p, acc)[0], acc_shape=(tm, tn))
    d_d = _ew("s5_dskip", lambda a, b: jnp.sum(a * b, axis=0, keepdims=True),
              [(dpre, *_rows(te, D)), (u, *_rows(te, D))], [((1, D), f32, *_const((1, D)), "all")], (L // te, 1))[0]
    neg = lambda acc: -acc
    dsspec = dict(dims=_NT, grid=(L // tm, 8, 1), a_spec=pl.BlockSpec((tm, 128), lambda i, j, k: (i, j)),
                  b_spec=pl.BlockSpec((None, 512, 128), lambda i, j, k: (j, 0, 0)),
                  out_shape=jax.ShapeDtypeStruct((L, W), f32), out_spec=pl.BlockSpec((tm, 512), lambda i, j, k: (i, j)),
                  acc_shape=(tm, 512))
    dsr = _mm("s5_c_re_dx", dpre, sv["wcr"], **dsspec)
    dsi = _mm("s5_c_im_dx", dpre, sv["wci"], epi_fn=neg, **dsspec)
    dwspec = dict(dims=_TN, grid=(8, 1, L // tk), a_spec=pl.BlockSpec((tk, 512), lambda i, j, k: (k, i)),
                  b_spec=pl.BlockSpec((tk, 128), lambda i, j, k: (k, i)),
                  out_shape=jax.ShapeDtypeStruct((8, 512, 128), f32),
                  out_spec=pl.BlockSpec((None, 512, 128), lambda i, j, k: (i, 0, 0)), acc_shape=(512, 128))
    dwcr = _mm("s5_c_re_dw", sv["sr"], dpre, **dwspec)
    dwci = _mm("s5_c_im_dw", sv["si"], dpre, epi_fn=neg, **dwspec)
    lr, li = _s5_scan("s5_scan_bwd", dsr, dsi, sv["abr"], sv["abi"], True, False)

    def da(lrv, liv, prv, piv):
        return (jnp.sum(lrv * prv + liv * piv, axis=0, keepdims=True),
                jnp.sum(liv * prv - lrv * piv, axis=0, keepdims=True))

    sblk = ((te, 512), lambda j, r: (r, j))
    dabr, dabi = _ew("s5_dlam", da, [(lr, *sblk), (li, *sblk), (sv["pr"], *sblk), (sv["pi"], *sblk)],
                     [((8, 1, 512), f32, (None, 1, 512), lambda j, r: (j, 0, 0), "inner")] * 2, (8, L // te))
    dbspec = dict(dims=_TN, grid=(8, 1, L // tk), a_spec=pl.BlockSpec((tk, 128), lambda i, j, k: (k, i)),
                  b_spec=pl.BlockSpec((tk, 512), lambda i, j, k: (k, i)),
                  out_shape=jax.ShapeDtypeStruct((8, 128, 512), f32),
                  out_spec=pl.BlockSpec((None, 128, 512), lambda i, j, k: (i, 0, 0)), acc_shape=(128, 512))
    dbbr = _mm("s5_bu_dw", u, lr, **dbspec)
    dbbi = _mm("s5_bu_dw", u, li, **dbspec)
    duspec = dict(dims=_NT, grid=(L // tm, 8, 1), a_spec=pl.BlockSpec((tm, 512), lambda i, j, k: (i, j)),
                  b_spec=pl.BlockSpec((None, 128, 512), lambda i, j, k: (j, 0, 0)),
                  out_shape=jax.ShapeDtypeStruct((L, D), f32), out_spec=pl.BlockSpec((tm, 128), lambda i, j, k: (i, j)),
                  acc_shape=(tm, 128))
    e128 = pl.BlockSpec((tm, 128), lambda i, j, k: (i, j))
    du1 = _mm("s5_bu_dx_re", lr, sv["bbr"], **duspec)
    du = _mm("s5_bu_dx_im", li, sv["bbi"],
             aux=[(du1, e128, "e"), (dpre, e128, "e"), (d_row, pl.BlockSpec((1, 128), lambda i, j, k: (0, j)), "e")],
             epi_fn=lambda acc, d1, dp, dd: acc + d1 + dp * dd, **duspec)
    jb = lambda shape: (shape, lambda j, z: (j, 0, 0))
    cts = [(dabr, *jb((None, 1, 512))), (dabi, *jb((None, 1, 512))), (dbbr, *jb((None, 128, 512))),
           (dbbi, *jb((None, 128, 512)))]
    dlre, dlim, dldt, dwbr, dwbi = _ew(
        "s5_params_bwd", _vjp_fn(_s5_params, 5), sv["par_ins"] + cts,
        [((8, 1, 512), f32, *jb((None, 1, 512)), None)] * 2 + [((8, 1, 128), f32, *jb((None, 1, 128)), None)]
        + [((8, 128, 512), f32, *jb((None, 128, 512)), None)] * 2, (8, 1))
    dw_in = _mm_plain("s5_in_dw", sv["hn"], du, _TN)
    dhn = _mm_plain("s5_in_dx", du, w_in, _NT)
    dh_in, dg = _rms_bwd("mix_norm_bwd", h, g_norm, dhn, dh)
    grads = dict(norm=dg, w_in=dw_in, lam_re=dlre.reshape(64, 64), lam_im=dlim.reshape(64, 64),
                 log_dt=dldt[:, 0, :8].reshape(64),
                 b_re=_blockdiag_t(dwbr, 16, 64).transpose(0, 2, 1), b_im=_blockdiag_t(dwbi, 16, 64).transpose(0, 2, 1),
                 c_re=_blockdiag_t(dwcr, 64, 16).transpose(0, 2, 1), c_im=_blockdiag_t(dwci, 64, 16).transpose(0, 2, 1),
                 d=d_d[0], w_out=dw_out)
    return dh_in, grads


def _m2_act(dt_raw, dtbias, alog):
    dt = jax.nn.softplus(dt_raw + dtbias)
    da = dt * (-jnp.exp(alog))
    sel = ((_iota((128, M2_INNER), 1) // 64) == _iota((128, M2_INNER), 0)).astype(f32)
    return _dot(dt, sel), _dot(da, sel)


def _m2_dexp(d):
    return _expand_lanes(d, M2_INNER, 64)


def _ssd_chunk(states, consts, vals, dots):
    (dsk,) = consts
    S = list(states)
    n = len(S)
    cut = [slice(128 * i, 128 * i + 128) for i in range(n)]
    x, dtb, dab = ([t[:, c] for c in cut] for t in vals[:3])
    dsk = [dsk[:, c] for c in cut]
    B = [vals[3][:, cut[i // 2]] for i in range(n)]
    Cm = [vals[4][:, cut[i // 2]] for i in range(n)]
    C = vals[0].shape[0]
    row, col = _iota((C, C), 0), _iota((C, C), 1)
    causal = row >= col
    ltri = causal.astype(f32)
    lane = _iota((C, 128), 1)
    last = _iota((C, 128), 0) == C - 1
    eye128 = _iota((128, 128), 0) == _iota((128, 128), 1)
    head = [jnp.logical_and(lane >= 64 * hh, lane < 64 * hh + 64) for hh in range(2)]
    pick = [(lane == 64 * hh).astype(f32) for hh in range(2)]
    xdt = _each(lambda a, b: a * b, x, dtb)
    cb = _each(lambda c, b: _dotb(c, b, _NT), Cm[::2], B[::2])
    cum = _each(lambda a: dots.dot01(ltri, a), dab)
    clast = _each(lambda a: jnp.sum(jnp.where(last, a, 0.0), axis=0, keepdims=True), cum)
    st = _each(lambda a, cl, cu, b: _dotb(a * jnp.exp(cl - cu), b, _TN), xdt, clast, cum, B)
    y = _each(lambda c, s, cu: _dotb(c, s, _NT) * jnp.exp(cu), Cm, S, cum)
    for hh in range(2):
        ccol = _each(lambda cu: jnp.sum(jnp.where(head[hh], cu, 0.0), axis=1, keepdims=True) * (1.0 / 64), cum)
        crow = _each(lambda cu: dots.dot01(pick[hh], cu, _NT), cum)
        lm = _each(lambda a, b: jnp.where(causal, jnp.exp(jnp.where(causal, a - b, 0.0)), 0.0), ccol, crow)
        y = [y[i] + _dotb(cb[i // 2] * lm[i], jnp.where(head[hh], xdt[i], 0.0)) for i in range(n)]
    cdcol = _each(lambda cl: jnp.sum(jnp.where(eye128, jnp.broadcast_to(jnp.exp(cl), (128, 128)), 0.0),
                                     axis=1, keepdims=True), clast)
    S_new = _each(lambda c, s, t: c * s + t, cdcol, S, st)
    out = _each(lambda a, d, b: a + d * b, y, dsk, x)
    return S_new, [jnp.concatenate(out, axis=1)]


def _m2_post(yc, z, ng):
    return _rms(yc * (z * jax.nn.sigmoid(z)), ng)


def _m2_fwd(h, g_norm, w_ext, conv_w, conv_b, dt_bias, a_log, d_skip, norm_g, w_out):
    L, D = h.shape
    tm = _row_tile(L)
    nc = L // CHUNK
    NI = M2_INNER
    hn = _rms_fwd("mix_norm", h, g_norm)
    proj = _mm_plain("m2_in", hn, w_ext, _NN)
    xbc = _conv_fwd("m2_conv", proj, NI // 128, conv_w, conv_b)
    dtb_row, alog_row, d_pad = _pad_row(dt_bias), _pad_row(a_log), _pad_row(d_skip)
    act_ins = [(proj, (tm, 128), lambda r, z: (r, 3 * NI // 128)), (dtb_row, *_const((1, 128))),
               (alog_row, *_const((1, 128)))]
    dtb, dab = _ew("m2_act", _m2_act, act_ins, [((L, NI), f32, *_rows(tm, NI), None)] * 2, (L // tm, 1))
    dsk = _ew("m2_dexp", _m2_dexp, [(d_pad, *_const((1, 128)))], [((1, NI), f32, *_const((1, NI)), None)], (1, 1))[0]
    GB = M2_GB
    x_blk, bc_blk = (CHUNK, 256 * GB), (CHUNK, 128 * GB)
    cins = [(dsk, (1, 256 * GB), lambda u, c: (0, u))]
    core_ins = [(xbc, x_blk, lambda u, c: (c, u), (L, NI), lambda u, c: (c, u)),
                (dtb, x_blk, lambda u, c: (c, u)), (dab, x_blk, lambda u, c: (c, u)),
                (xbc, bc_blk, lambda u, c: (c, 16 // GB + u), (L, D), lambda u, c: (c, u)),
                (xbc, bc_blk, lambda u, c: (c, 24 // GB + u), (L, D), lambda u, c: (c, u))]
    (yc,), saved_s = _scan_fwd("m2_core", _ssd_chunk, 2 * GB, (128, 128), cins, core_ins,
                               [((L, NI), x_blk, lambda u, c: (c, u))], 8 // GB, nc)
    tp = _head_rows(L)
    gblk = ((tp, 256), lambda g, r: (r, g))
    post_ins = [(yc, *gblk), (proj, *gblk), (norm_g, (1, 256), lambda g, r: (0, g))]
    yn = _ew("m2_post", _m2_post, post_ins, [((L, NI), f32, *gblk, None)], (8, L // tp))[0]
    h_out = _mm_plain("m2_out", yn, w_out, _NN, epi_fn=lambda acc, res: acc + res, aux=[(h, "e")])
    saved = dict(hn=hn, proj=proj, act_ins=act_ins, d_pad=d_pad, cins=cins, core_ins=core_ins, saved_s=saved_s,
                 post_ins=post_ins, yn=yn)
    return h_out, saved


def _m2_bwd(dh, h, g_norm, w_ext, conv_w, conv_b, w_out, sv):
    L, D = h.shape
    tm = _row_tile(L)
    nc = L // CHUNK
    NI = M2_INNER
    dyn = _mm_plain("m2_out_dx", dh, w_out, _NT)
    dw_out = _mm_plain("m2_out_dw", sv["yn"], dh, _TN)
    tp = _head_rows(L)
    gblk = ((tp, 256), lambda g, r: (r, g))
    d_yc, d_z, d_ng = _ew("m2_post_bwd", _vjp_fn(_m2_post, 3), sv["post_ins"] + [(dyn, *gblk)],
                          [((L, NI), f32, *gblk, None)] * 2 + [((1, NI), f32, (1, 256), lambda g, r: (0, g), "inner")],
                          (8, L // tp))
    (d_dsk,), (dx, d_dtb, d_dab, dB, dC) = _scan_bwd(
        "m2_core_bwd", _ssd_chunk, 2 * M2_GB, (128, 128), sv["cins"], sv["core_ins"], sv["saved_s"],
        [(d_yc, (CHUNK, 256 * M2_GB), lambda u, c: (c, u))], 8 // M2_GB, nc)
    row128 = ((1, 128), f32, *_const((1, 128)), "all")
    d_dt_raw, d_dtbias, d_alog = _ew(
        "m2_act_bwd", _vjp_fn(_m2_act, 3), sv["act_ins"] + [(d_dtb, *_rows(tm, NI)), (d_dab, *_rows(tm, NI))],
        [((L, 128), f32, *_rows(tm, 128), None), row128, row128], (L // tm, 1))
    d_d = _ew("m2_dexp_bwd", _vjp_fn(_m2_dexp, 1), [(sv["d_pad"], *_const((1, 128))), (d_dsk, *_const((1, NI)))],
              [((1, 128), f32, *_const((1, 128)), None)], (1, 1))[0]
    d_conv_out = jnp.concatenate([dx, dB, dC], axis=1)
    d_conv_in, d_conv_w, d_conv_b = _conv_bwd("m2_conv_bwd", sv["proj"], NI // 128, conv_w, conv_b, d_conv_out)
    d_proj = jnp.concatenate([d_z, d_conv_in, d_dt_raw], axis=1)
    dw_ext = _mm_plain("m2_in_dw", sv["hn"], d_proj, _TN)
    dhn = _mm_plain("m2_in_dx", d_proj, w_ext, _NT)
    dh_in, dg = _rms_bwd("mix_norm_bwd", h, g_norm, dhn, dh)
    grads = dict(norm=dg, w_ext=dw_ext, conv_w=d_conv_w, conv_b=d_conv_b, dt_bias=d_dtbias[0, :M2_HEADS],
                 a_log=d_alog[0, :M2_HEADS], d=d_d[0, :M2_HEADS], norm_g=d_ng, w_out=dw_out)
    return dh_in, grads


def _mesh_pos():
    return lax.axis_index("x"), lax.axis_index("y"), lax.axis_index("c")


def _flip(pos, p):
    x, y, c = pos
    return (1 - x if p & 4 else x, 1 - y if p & 2 else y, 1 - c if p & 1 else c)


def _index(pos):
    return 4 * pos[0] + 2 * pos[1] + pos[2]


def _comm_call(name, body, arrays, out_shape, n_sem):
    n = len(arrays)
    hbm = pl.BlockSpec(memory_space=pl.ANY)
    return pl.pallas_call(
        body, name=name, in_specs=[hbm] * n, out_specs=[hbm] * len(out_shape), out_shape=out_shape,
        scratch_shapes=[pltpu.SemaphoreType.DMA((n, n_sem)), pltpu.SemaphoreType.DMA((n, n_sem)),
                        pltpu.SemaphoreType.DMA((n, 4))],
    )(*arrays)


def _gather(name, arrays):
    n = len(arrays)

    def body(*refs):
        ins, outs = refs[:n], refs[n:2 * n]
        send_sems, recv_sems, loc_sems = refs[2 * n:]
        me = _mesh_pos()
        c = me[2]
        sib = _flip(me, 1)
        chips = [_flip(me, 4), _flip(me, 2), _flip(me, 6)]

        def copy(w, k, block, to, src=None):
            slab = outs[w].at[_index(block)]
            return pltpu.make_async_remote_copy(
                src_ref=slab if src is None else src, dst_ref=slab, send_sem=send_sems.at[w, k],
                recv_sem=recv_sems.at[w, k], device_id=to, device_id_type=MESH)

        local = [pltpu.make_async_copy(ins[w], outs[w].at[_index(me)], loc_sems.at[w, 0]) for w in range(n)]
        for cp in local:
            cp.start()
        first = [copy(w, 0, me, sib, src=ins[w]) for w in range(n)]
        first += [copy(w, 1 + j, me, chip, src=ins[w]) for j, chip in enumerate(chips) for w in range(n)]
        for cp in first:
            cp.start()
        passed = []
        for j, chip in enumerate(chips):
            for w in range(n):
                copy(w, 1 + j, chip, me).wait_recv()
                fwd = copy(w, 4 + j, chip, sib)
                fwd.start()
                passed.append(fwd)
        for w in range(n):
            copy(w, 0, sib, me).wait_recv()
        for j, chip in enumerate(chips):
            for w in range(n):
                copy(w, 4 + j, (chip[0], chip[1], 1 - c), me).wait_recv()
        for cp in first + passed:
            cp.wait_send()
        for cp in local:
            cp.wait()

    out_shape = [jax.ShapeDtypeStruct((N_DEV,) + a.shape, a.dtype) for a in arrays]
    return _comm_call(name, body, arrays, out_shape, N_DEV - 1)


_HBM = pl.BlockSpec(memory_space=pltpu.HBM)
_SEM = pl.BlockSpec(memory_space=pltpu.SEMAPHORE)
_SPLIT_COPIES = 4


def _split_targets(me):
    return [_flip(me, 1), _flip(me, 4), _flip(me, 2), _flip(me, 6)]


def _gather_start(name, arrays, lands):
    n = len(arrays)
    ns = n * _SPLIT_COPIES

    def body(*refs):
        ins, land = refs[:n], refs[n:2 * n]
        send_sems, recv_sems = refs[2 * n:2 * n + ns], refs[2 * n + ns:2 * n + 2 * ns]
        token = refs[4 * n + 2 * ns]
        me = _mesh_pos()
        for w in range(n):
            for k, to in enumerate(_split_targets(me)):
                pltpu.make_async_remote_copy(
                    src_ref=ins[w], dst_ref=land[w].at[_index(me)], send_sem=send_sems[w * _SPLIT_COPIES + k],
                    recv_sem=recv_sems[w * _SPLIT_COPIES + k], device_id=to, device_id_type=MESH).start()
        token[...] = jnp.zeros_like(token)

    sem = pltpu.SemaphoreType.DMA(())
    res = pl.pallas_call(
        body, name=name,
        out_shape=(*[sem] * (2 * ns), *[pltpu.HBM(a.shape, a.dtype) for a in arrays],
                   *[pltpu.HBM(a.shape, a.dtype) for a in lands], jax.ShapeDtypeStruct((8, 128), f32)),
        in_specs=[_HBM] * (2 * n),
        out_specs=(*[_SEM] * (2 * ns), *[_HBM] * (2 * n), pl.BlockSpec(memory_space=pltpu.VMEM)),
        input_output_aliases={i: 2 * ns + i for i in range(2 * n)},
        compiler_params=pltpu.CompilerParams(has_side_effects=pltpu.SideEffectType.DATAFLOW_SIDE_EFFECTING),
    )(*[pltpu.with_memory_space_constraint(a, pltpu.HBM) for a in list(arrays) + list(lands)])
    sems, rest = res[:2 * ns], res[2 * ns:]
    return sems[:ns], sems[ns:], rest[:n], rest[n:2 * n], rest[2 * n]


def _gather_wait(name, arrays, lands, send_sems, recv_sems, after):
    n = len(arrays)
    ns = n * _SPLIT_COPIES

    def body(*refs):
        ins, land = refs[:n], refs[n:2 * n]
        s_sems, r_sems = refs[2 * n:2 * n + ns], refs[2 * n + ns:2 * n + 2 * ns]
        me = _mesh_pos()
        for w in range(n):
            for k, peer in enumerate(_split_targets(me)):
                cp = pltpu.make_async_remote_copy(
                    src_ref=ins[w], dst_ref=land[w].at[_index(peer)], send_sem=s_sems[w * _SPLIT_COPIES + k],
                    recv_sem=r_sems[w * _SPLIT_COPIES + k], device_id=peer, device_id_type=MESH)
                cp.wait_send()
                cp.wait_recv()

    res = pl.pallas_call(
        body, name=name,
        out_shape=(*[pltpu.HBM(a.shape, a.dtype) for a in arrays], *[pltpu.HBM(a.shape, a.dtype) for a in lands]),
        in_specs=[_HBM] * (2 * n) + [_SEM] * (2 * ns) + [pl.BlockSpec(memory_space=pl.ANY)],
        out_specs=tuple([_HBM] * (2 * n)), input_output_aliases={i: i for i in range(2 * n)},
        compiler_params=pltpu.CompilerParams(has_side_effects=pltpu.SideEffectType.DATAFLOW_SIDE_EFFECTING),
    )(*arrays, *lands, *send_sems, *recv_sems, after)
    return res[n:]


def _gather_forward(name, lands):
    n = len(lands)

    def body(*refs):
        outs = refs[n:2 * n]
        send_sems, recv_sems, _ = refs[2 * n:]
        me = _mesh_pos()
        sib = _flip(me, 1)
        held = [_flip(me, 4), _flip(me, 2), _flip(me, 6), sib]

        def copy(w, j, block):
            slab = outs[w].at[_index(block)]
            return pltpu.make_async_remote_copy(src_ref=slab, dst_ref=slab, send_sem=send_sems.at[w, j],
                                                recv_sem=recv_sems.at[w, j], device_id=sib, device_id_type=MESH)

        sends = [copy(w, j, blk) for j, blk in enumerate(held) for w in range(n)]
        for cp in sends:
            cp.start()
        for j, blk in enumerate(held):
            for w in range(n):
                copy(w, j, (blk[0], blk[1], 1 - blk[2])).wait_recv()
        for cp in sends:
            cp.wait_send()

    hbm = pl.BlockSpec(memory_space=pl.ANY)
    return pl.pallas_call(
        body, name=name, in_specs=[hbm] * n, out_specs=[hbm] * n,
        out_shape=[jax.ShapeDtypeStruct(a.shape, a.dtype) for a in lands],
        input_output_aliases={i: i for i in range(n)},
        scratch_shapes=[pltpu.SemaphoreType.DMA((n, 4)), pltpu.SemaphoreType.DMA((n, 4)), pltpu.SemaphoreType.DMA((n, 4))],
    )(*lands)


def _scatter_pair(name, arrays):
    n = len(arrays)

    def body(*refs):
        ins, outs = refs[:n], refs[n:2 * n]
        send_sems, recv_sems, _ = refs[2 * n:]
        me = _mesh_pos()
        c = me[2]
        sib = _flip(me, 1)

        def copy(w, q):
            return pltpu.make_async_remote_copy(
                src_ref=ins[w].at[2 * q + 1 - c], dst_ref=outs[w].at[q], send_sem=send_sems.at[w, q],
                recv_sem=recv_sems.at[w, q], device_id=sib, device_id_type=MESH)

        cps = [copy(w, q) for q in range(4) for w in range(n)]
        for cp in cps:
            cp.start()
        for cp in cps:
            cp.wait()

    out_shape = [jax.ShapeDtypeStruct((4,) + a.shape[1:], a.dtype) for a in arrays]
    return _comm_call(name, body, arrays, out_shape, 4)


def _pair_add(name, full, theirs, core, dtype):
    _, R, C = theirs.shape
    tr = R if R <= 256 else (256 if C <= 512 else 128)

    def body(core_ref, mine_ref, theirs_ref, o_ref):
        o_ref[...] = (mine_ref[...] + theirs_ref[...]).astype(o_ref.dtype)

    blk = pl.BlockSpec((4, tr, C), lambda r, cr: (0, r, 0))
    grid_spec = pltpu.PrefetchScalarGridSpec(
        num_scalar_prefetch=1, grid=(R // tr,),
        in_specs=[pl.BlockSpec((4, None, tr, C), lambda r, cr: (0, cr[0], r, 0)), blk], out_specs=blk)
    return pl.pallas_call(
        body, name=name, grid_spec=grid_spec, out_shape=jax.ShapeDtypeStruct((4, R, C), dtype),
        compiler_params=_params(1),
    )(core.reshape(1).astype(jnp.int32), full.reshape(4, 2, R, C), theirs)


def _scatter_chips(name, arrays):
    n = len(arrays)

    def body(*refs):
        ins, outs = refs[:n], refs[n:2 * n]
        send_sems, recv_sems, loc_sems = refs[2 * n:]
        me = _mesh_pos()
        mq = 2 * me[0] + me[1]
        peers = [_flip(me, 4), _flip(me, 2), _flip(me, 6)]

        def copy(w, k):
            peer = peers[k]
            return pltpu.make_async_remote_copy(
                src_ref=ins[w].at[2 * peer[0] + peer[1]], dst_ref=outs[w].at[mq], send_sem=send_sems.at[w, k],
                recv_sem=recv_sems.at[w, k], device_id=peer, device_id_type=MESH)

        def arrival(w, k):
            peer = peers[k]
            return pltpu.make_async_remote_copy(
                src_ref=ins[w].at[mq], dst_ref=outs[w].at[2 * peer[0] + peer[1]], send_sem=send_sems.at[w, k],
                recv_sem=recv_sems.at[w, k], device_id=peer, device_id_type=MESH)

        local = [pltpu.make_async_copy(ins[w].at[mq], outs[w].at[mq], loc_sems.at[w, 0]) for w in range(n)]
        for cp in local:
            cp.start()
        sends = [copy(w, k) for k in range(3) for w in range(n)]
        for cp in sends:
            cp.start()
        for k in range(3):
            for w in range(n):
                arrival(w, k).wait_recv()
        for cp in sends:
            cp.wait_send()
        for cp in local:
            cp.wait()

    out_shape = [jax.ShapeDtypeStruct(a.shape, a.dtype) for a in arrays]
    return _comm_call(name, body, arrays, out_shape, 3)


def _chip_peers(me):
    return [_flip(me, 4), _flip(me, 2), _flip(me, 6)]


def _chips_start(name, arrays):
    n = len(arrays)
    ns = 3 * n

    def body(*refs):
        ins, land = refs[:n], refs[n:2 * n]
        send_sems, recv_sems = refs[2 * n:2 * n + ns], refs[2 * n + ns:2 * n + 2 * ns]
        token = refs[4 * n + 2 * ns]
        me = _mesh_pos()
        for w in range(n):
            for k, peer in enumerate(_chip_peers(me)):
                pltpu.make_async_remote_copy(
                    src_ref=ins[w].at[2 * peer[0] + peer[1]], dst_ref=land[w].at[k], send_sem=send_sems[3 * w + k],
                    recv_sem=recv_sems[3 * w + k], device_id=peer, device_id_type=MESH).start()
        token[...] = jnp.zeros_like(token)

    lands = [lax.empty((3,) + a.shape[1:], a.dtype) for a in arrays]
    sem = pltpu.SemaphoreType.DMA(())
    res = pl.pallas_call(
        body, name=name,
        out_shape=(*[sem] * (2 * ns), *[pltpu.HBM(a.shape, a.dtype) for a in arrays],
                   *[pltpu.HBM(a.shape, a.dtype) for a in lands], jax.ShapeDtypeStruct((8, 128), f32)),
        in_specs=[_HBM] * (2 * n),
        out_specs=(*[_SEM] * (2 * ns), *[_HBM] * (2 * n), pl.BlockSpec(memory_space=pltpu.VMEM)),
        input_output_aliases={i: 2 * ns + i for i in range(2 * n)},
        compiler_params=pltpu.CompilerParams(has_side_effects=pltpu.SideEffectType.DATAFLOW_SIDE_EFFECTING),
    )(*[pltpu.with_memory_space_constraint(a, pltpu.HBM) for a in list(arrays) + lands])
    sems, rest = res[:2 * ns], res[2 * ns:]
    return sems[:ns], sems[ns:], rest[:n], rest[n:2 * n], rest[2 * n]


def _chips_wait(name, arrays, lands, send_sems, recv_sems, after):
    n = len(arrays)
    ns = 3 * n

    def body(*refs):
        ins, land = refs[:n], refs[n:2 * n]
        s_sems, r_sems = refs[2 * n:2 * n + ns], refs[2 * n + ns:2 * n + 2 * ns]
        me = _mesh_pos()
        for w in range(n):
            for k, peer in enumerate(_chip_peers(me)):
                cp = pltpu.make_async_remote_copy(
                    src_ref=ins[w].at[2 * peer[0] + peer[1]], dst_ref=land[w].at[k], send_sem=s_sems[3 * w + k],
                    recv_sem=r_sems[3 * w + k], device_id=peer, device_id_type=MESH)
                cp.wait_send()
                cp.wait_recv()

    res = pl.pallas_call(
        body, name=name,
        out_shape=(*[pltpu.HBM(a.shape, a.dtype) for a in arrays], *[pltpu.HBM(a.shape, a.dtype) for a in lands]),
        in_specs=[_HBM] * (2 * n) + [_SEM] * (2 * ns) + [pl.BlockSpec(memory_space=pl.ANY)],
        out_specs=tuple([_HBM] * (2 * n)), input_output_aliases={i: i for i in range(2 * n)},
        compiler_params=pltpu.CompilerParams(has_side_effects=pltpu.SideEffectType.DATAFLOW_SIDE_EFFECTING),
    )(*arrays, *lands, *send_sems, *recv_sems, after)
    return res[:n], res[n:]


def _adamw_own(name, arrivals, sums, chip, w, m, v):
    R, C = w.shape
    tr = R if R <= 256 else (256 if C <= 512 else 128)
    bc1 = 1.0 - ADAM_B1 ** ADAM_STEP
    bc2 = 1.0 - ADAM_B2 ** ADAM_STEP

    def body(chip_ref, own_ref, p_ref, w_ref, m_ref, v_ref, g_ref, d_ref, m2_ref, v2_ref):
        g = own_ref[...].astype(f32)
        for k in range(3):
            g = g + p_ref[k].astype(f32)
        m2 = ADAM_B1 * m_ref[...] + (1.0 - ADAM_B1) * g
        v2 = ADAM_B2 * v_ref[...] + (1.0 - ADAM_B2) * jnp.square(g)
        g_ref[...] = g
        d_ref[...] = -ADAM_LR * ((m2 / bc1) / (jnp.sqrt(v2 / bc2) + ADAM_EPS) + ADAM_WD * w_ref[...])
        m2_ref[...] = m2
        v2_ref[...] = v2

    blk = pl.BlockSpec((tr, C), lambda r, cr: (r, 0))
    grid_spec = pltpu.PrefetchScalarGridSpec(
        num_scalar_prefetch=1, grid=(R // tr,),
        in_specs=[pl.BlockSpec((None, tr, C), lambda r, cr: (cr[0], r, 0)),
                  pl.BlockSpec((3, tr, C), lambda r, cr: (0, r, 0)), blk, blk, blk],
        out_specs=[blk] * 4)
    return pl.pallas_call(
        body, name=name, grid_spec=grid_spec, out_shape=[jax.ShapeDtypeStruct((R, C), f32)] * 4,
        compiler_params=_params(1),
    )(chip.reshape(1).astype(jnp.int32), sums, arrivals, w, m, v)


def _adamw(name, parts, w, m, v):
    R, C = w.shape
    n_parts = parts.shape[0]
    tr = R if R <= 256 else (256 if C <= 512 else 128)
    bc1 = 1.0 - ADAM_B1 ** ADAM_STEP
    bc2 = 1.0 - ADAM_B2 ** ADAM_STEP

    def f(p, wv, mv, vv):
        g = p[0].astype(f32)
        for i in range(1, n_parts):
            g = g + p[i].astype(f32)
        m2 = ADAM_B1 * mv + (1.0 - ADAM_B1) * g
        v2 = ADAM_B2 * vv + (1.0 - ADAM_B2) * jnp.square(g)
        delta = -ADAM_LR * ((m2 / bc1) / (jnp.sqrt(v2 / bc2) + ADAM_EPS) + ADAM_WD * wv)
        return g, delta, m2, v2

    blk = ((tr, C), lambda r, z: (r, 0))
    return _ew(name, f, [(parts, (n_parts, tr, C), lambda r, z: (0, r, 0)), (w, *blk), (m, *blk), (v, *blk)],
               [((R, C), f32, *blk, None)] * 4, (R // tr, 1))


_WEIGHTS = ["norm_mix_g", "norm_mlp_g", "mlp_w1", "mlp_w2", "gdn_w_in", "gdn_conv_w", "gdn_a_log", "gdn_dt_bias",
            "gdn_o_norm_g", "gdn_w_out", "s5_w_in", "s5_lam_re", "s5_lam_im", "s5_log_dt", "s5_b_re", "s5_b_im",
            "s5_c_re", "s5_c_im", "s5_d", "s5_w_out", "m2_w_in", "m2_conv_w", "m2_conv_b", "m2_dt_bias", "m2_a_log",
            "m2_d", "m2_norm_g", "m2_w_out", "final_norm_g"]
_SHARDED = ["mlp_w1", "mlp_w2", "gdn_w_in", "gdn_w_out", "s5_w_in", "s5_w_out", "m2_w_in", "m2_w_out",
            "gdn_conv_w", "m2_conv_w", "m2_conv_b", "m2_norm_g"]
_MATRICES = _SHARDED[:8]
_REPLICATED = [n for n in _WEIGHTS if n not in _SHARDED]
_GDN_IN, _M2_IN = 4112, 6176
_LAYER_KIND = (0, 1, 2, 0)


def _as2d(a):
    return a.reshape(-1, a.shape[-1])


def _cols_from_shards(g, width):
    return g.transpose(1, 0, 2).reshape(g.shape[1], width)


def _cols_to_shards(a, width):
    return a[:, :width].reshape(a.shape[0], N_DEV, width // N_DEV).transpose(1, 0, 2)


def kernel(x, norm_mix_g, norm_mlp_g, mlp_w1, mlp_w2, gdn_w_in, gdn_conv_w, gdn_a_log, gdn_dt_bias, gdn_o_norm_g, gdn_w_out, s5_w_in, s5_lam_re, s5_lam_im, s5_log_dt, s5_b_re, s5_b_im, s5_c_re, s5_c_im, s5_d, s5_w_out, m2_w_in, m2_conv_w, m2_conv_b, m2_dt_bias, m2_a_log, m2_d, m2_norm_g, m2_w_out, final_norm_g, loss_target, m_norm_mix_g, m_norm_mlp_g, m_mlp_w1, m_mlp_w2, m_gdn_w_in, m_gdn_conv_w, m_gdn_a_log, m_gdn_dt_bias, m_gdn_o_norm_g, m_gdn_w_out, m_s5_w_in, m_s5_lam_re, m_s5_lam_im, m_s5_log_dt, m_s5_b_re, m_s5_b_im, m_s5_c_re, m_s5_c_im, m_s5_d, m_s5_w_out, m_m2_w_in, m_m2_conv_w, m_m2_conv_b, m_m2_dt_bias, m_m2_a_log, m_m2_d, m_m2_norm_g, m_m2_w_out, m_final_norm_g, v_norm_mix_g, v_norm_mlp_g, v_mlp_w1, v_mlp_w2, v_gdn_w_in, v_gdn_conv_w, v_gdn_a_log, v_gdn_dt_bias, v_gdn_o_norm_g, v_gdn_w_out, v_s5_w_in, v_s5_lam_re, v_s5_lam_im, v_s5_log_dt, v_s5_b_re, v_s5_b_im, v_s5_c_re, v_s5_c_im, v_s5_d, v_s5_w_out, v_m2_w_in, v_m2_conv_w, v_m2_conv_b, v_m2_dt_bias, v_m2_a_log, v_m2_d, v_m2_norm_g, v_m2_w_out, v_final_norm_g):
    args = locals()
    W = {n: args[n] for n in _WEIGHTS}
    MOM = {n: args["m_" + n] for n in _WEIGHTS}
    VAR = {n: args["v_" + n] for n in _WEIGHTS}
    h = x[0]
    target = loss_target[0]
    L, D = h.shape

    first = [mlp_w1[0:1].astype(bf16), mlp_w2[0:1].astype(bf16), gdn_w_in[0:1].astype(bf16),
             gdn_w_out[0:1].astype(bf16), _as2d(gdn_conv_w), _as2d(m2_conv_w), _as2d(m2_conv_b), _as2d(m2_norm_g)]
    w1g0, w2g0, gin0, gout0, gconv, m2_cw, m2_cbg, m2_ngg = _gather("gather_first", first)
    stacked = jnp.concatenate([gdn_w_out[1], s5_w_in[0], m2_w_out[0]], axis=0).astype(bf16)
    rest = [mlp_w1[1:4].astype(bf16), mlp_w2[1:4].astype(bf16), gdn_w_in[1:2].astype(bf16), s5_w_out.astype(bf16),
            m2_w_in.astype(bf16), stacked]
    lands = [lax.empty((N_DEV,) + a.shape, a.dtype) for a in rest]
    send_sems, recv_sems, rest_thru, lands_thru, token = _gather_start("gather_rest_start", rest, lands)

    def gdn_weights(gin, gout, conv, j):
        return (jnp.pad(_cols_from_shards(gin[:, 0], _GDN_IN), ((0, 0), (0, GDN_EXT - _GDN_IN))),
                gout[:, 0].reshape(D, D), _cols_from_shards(conv[:, 4 * j:4 * j + 4], 3 * D))

    gdn_in, gdn_out, gdn_conv = [None, None], [None, None], [None, None]
    gdn_in[0], gdn_out[0], gdn_conv[0] = gdn_weights(gin0, gout0, gconv, 0)

    norm_mix = [norm_mix_g[i].reshape(1, D) for i in range(4)]
    norm_mix[0] = norm_mix[0] + token[0, 0]
    late = {}

    def mixer_fwd(i, hv):
        kind, j = _LAYER_KIND[i], i // 3
        gn = norm_mix[i]
        s5_in, s5_out_g = late.get("s5_in"), late.get("s5_out_g")
        m2_in, m2_conv, m2_cb, m2_ng, m2_out = (late.get(k) for k in ("m2_in", "m2_conv", "m2_cb", "m2_ng", "m2_out"))
        if kind == 0:
            return _gdn_fwd(hv, gn, gdn_in[j], gdn_conv[j], gdn_a_log[j], gdn_dt_bias[j], gdn_o_norm_g[j], gdn_out[j])
        if kind == 1:
            return _s5_fwd(hv, gn, s5_in, s5_lam_re[0], s5_lam_im[0], s5_log_dt[0], s5_b_re[0], s5_b_im[0],
                           s5_c_re[0], s5_c_im[0], s5_d[0], s5_out_g)
        return _m2_fwd(hv, gn, m2_in, m2_conv, m2_cb, m2_dt_bias[0], m2_a_log[0], m2_d[0], m2_ng, m2_out)

    def mixer_bwd(i, dh, hv, sv):
        kind, j = _LAYER_KIND[i], i // 3
        gn = norm_mix[i]
        s5_in, s5_out_g = late["s5_in"], late["s5_out_g"]
        m2_in, m2_conv, m2_cb, m2_out = (late[k] for k in ("m2_in", "m2_conv", "m2_cb", "m2_out"))
        if kind == 0:
            return _gdn_bwd(dh, hv, gn, gdn_in[j], gdn_conv[j], gdn_out[j], sv)
        if kind == 1:
            return _s5_bwd(dh, hv, gn, s5_in, s5_out_g, sv)
        return _m2_bwd(dh, hv, gn, m2_in, m2_conv, m2_cb, m2_out, sv)

    tape = []
    mlp_w = [(w1g0, w2g0, 0)]
    for i in range(4):
        if i == 1:
            landed = _gather_wait("gather_rest_wait", rest_thru, lands_thru, send_sems, recv_sems, h)
            w1gr, w2gr, gin1, s5_out_g, m2_in_g, rows_g = _gather_forward("gather_rest_forward", landed)
            gout1, s5_in_g, m2_out_g = rows_g[:, None, 0:128], rows_g[:, 128:256], rows_g[:, 256:512]
            mlp_w += [(w1gr, w2gr, k) for k in range(3)]
            gdn_in[1], gdn_out[1], gdn_conv[1] = gdn_weights(gin1, gout1, gconv, 1)
            late.update(
                s5_in=s5_in_g.reshape(D, D), s5_out_g=s5_out_g,
                m2_in=jnp.pad(_cols_from_shards(m2_in_g[:, 0], _M2_IN), ((0, 0), (0, M2_EXT - _M2_IN))),
                m2_out=m2_out_g.reshape(M2_INNER, D), m2_conv=_cols_from_shards(m2_cw, 2 * M2_INNER),
                m2_cb=_cols_from_shards(m2_cbg, 2 * M2_INNER), m2_ng=_cols_from_shards(m2_ngg, M2_INNER))
        h_mid, sv = mixer_fwd(i, h)
        h_next, hn, h1 = _mlp_fwd(h_mid, norm_mlp_g[i].reshape(1, D), *mlp_w[i])
        tape.append((h, sv, h_mid, hn, h1))
        h = h_next
    loss_row, dh, d_final = _loss_head(h, final_norm_g.reshape(1, D), target)
    loss = lax.psum(loss_row[0, 0], ("x", "y", "c"))

    fs = D_FF // N_DEV
    dw1 = [lax.empty((N_DEV, 2, D, fs), f32) for _ in range(2)]
    dw2 = [lax.empty((N_DEV, 2, fs, D), f32) for _ in range(2)]
    d_mix, d_mlp, mg = [None] * 4, [None] * 4, [None] * 4
    core = lax.axis_index("c")
    chip = 2 * lax.axis_index("x") + lax.axis_index("y")

    def pair_sums(tag, entries):
        theirs = _scatter_pair("scatter_pair_" + tag, [e[1] for e in entries])
        return [_pair_add("pair_add_%s_%s" % (tag, e[0]), e[1], th, core, bf16 if e[2] else f32)
                for e, th in zip(entries, theirs)]

    def matrices(grp, gdn_g):
        return [("mlp_w1", dw1[grp].reshape(N_DEV, 2 * D, fs), True), ("mlp_w2", dw2[grp].reshape(N_DEV, 2 * fs, D), True),
                ("gdn_w_in", _cols_to_shards(gdn_g["w_ext"], _GDN_IN), True),
                ("gdn_w_out", gdn_g["w_out"].reshape(N_DEV, D // N_DEV, D), True)]

    for i in reversed(range(4)):
        h_in, sv, h_mid, hn, h1 = tape[i]
        grp = i // 2
        dh, d_mlp[i], dw1[grp], dw2[grp] = _mlp_bwd(dh, h_mid, norm_mlp_g[i].reshape(1, D), hn, h1, *mlp_w[i], i % 2,
                                                    dw1[grp], dw2[grp])
        dh, mg[i] = mixer_bwd(i, dh, h_in, sv)
        d_mix[i] = mg[i]["norm"]
        if i == 2:
            late_grads = matrices(1, mg[3]) + [("m2_w_in", _cols_to_shards(mg[2]["w_ext"], _M2_IN), True),
                                         ("m2_w_out", mg[2]["w_out"].reshape(N_DEV, M2_INNER // N_DEV, D), True)]
            late_sums = pair_sums("late", late_grads)
            late_send, late_recv, late_sums, late_lands, token = _chips_start("chips_late_start", late_sums)
            dh = dh + token[0, 0]
    grad_x = dh.reshape(1, L, D)
    ga, gb_, s5g, m2g = mg[0], mg[3], mg[1], mg[2]
    early = matrices(0, ga) + [
        ("s5_w_in", s5g["w_in"].reshape(N_DEV, D // N_DEV, D), True), ("s5_w_out", s5g["w_out"].reshape(N_DEV, D, 2 * D // N_DEV), True),
        ("gdn_conv_w", jnp.concatenate([_cols_to_shards(g["conv_w"], 3 * D) for g in (ga, gb_)], axis=1), False),
        ("m2_conv_w", _cols_to_shards(m2g["conv_w"], 2 * M2_INNER), False),
        ("m2_conv_b", _cols_to_shards(m2g["conv_b"], 2 * M2_INNER), False),
        ("m2_norm_g", _cols_to_shards(m2g["norm_g"], M2_INNER), False)]
    early_parts = _scatter_chips("scatter_chips", pair_sums("early", early))
    late_sums, late_arrivals = _chips_wait("chips_late_wait", late_sums, late_lands, late_send, late_recv, early_parts[0])

    rep = {
        "norm_mix_g": jnp.concatenate(d_mix, axis=0), "norm_mlp_g": jnp.concatenate(d_mlp, axis=0),
        "gdn_a_log": jnp.stack([ga["a_log"], gb_["a_log"]]), "gdn_dt_bias": jnp.stack([ga["dt_bias"], gb_["dt_bias"]]),
        "gdn_o_norm_g": jnp.stack([ga["o_norm_g"], gb_["o_norm_g"]]),
        "s5_lam_re": s5g["lam_re"], "s5_lam_im": s5g["lam_im"], "s5_log_dt": s5g["log_dt"], "s5_b_re": s5g["b_re"],
        "s5_b_im": s5g["b_im"], "s5_c_re": s5g["c_re"], "s5_c_im": s5g["c_im"], "s5_d": s5g["d"],
        "m2_dt_bias": m2g["dt_bias"], "m2_a_log": m2g["a_log"], "m2_d": m2g["d"], "final_norm_g": d_final,
    }

    def pack(d):
        flat = jnp.concatenate([d[n].reshape(-1).astype(f32) for n in _REPLICATED])
        return jnp.pad(flat, (0, -flat.shape[0] % (256 * 128))).reshape(-1, 128)

    (rep_parts,) = _gather("gather_small_grads", [pack(rep)])

    res = {}
    halves = {"mlp_w1": 2, "mlp_w2": 2, "gdn_w_in": 1, "gdn_w_out": 1}

    def shard_rows(a, label, grp):
        k = halves.get(label)
        return _as2d(a if k is None else a[grp * k:(grp + 1) * k])

    done = {}
    for (label, _, _), p in zip(early, early_parts):
        done[label] = [_adamw("adamw_" + label, p, *[shard_rows(t[label], label, 0) for t in (W, MOM, VAR)])]
    for (label, _, _), arr, sums in zip(late_grads, late_arrivals, late_sums):
        done.setdefault(label, []).append(
            _adamw_own("adamw_late_" + label, arr, sums, chip, *[shard_rows(t[label], label, 1) for t in (W, MOM, VAR)]))
    for n in _SHARDED:
        res[n] = [jnp.concatenate([g[k] for g in done[n]], axis=0).reshape(W[n].shape) for k in range(4)]
    out = _adamw("adamw_replicated", rep_parts, pack(W), pack(MOM), pack(VAR))
    off = 0
    for n in _REPLICATED:
        size = W[n].size
        res[n] = [o.reshape(-1)[off:off + size].reshape(W[n].shape) for o in out]
        off += size

    return (loss, grad_x, *[res[n][0] for n in _WEIGHTS], *[res[n][1] for n in _WEIGHTS],
            *[res[n][2] for n in _WEIGHTS], *[res[n][3] for n in _WEIGHTS])
```

```python
import functools

import jax
import jax.numpy as jnp
from jax import lax
from jax.experimental import pallas as pl
from jax.experimental.pallas import tpu as pltpu

f32 = jnp.float32
bf16 = jnp.bfloat16
HI = lax.Precision.HIGHEST
MESH = pl.DeviceIdType.MESH

N_DEV = 8
D_MODEL = 1024
D_FF = 4096
CHUNK = 64
RMS_EPS = 1e-6
GDN_HEADS = 8
GDN_HB = 8
GDN_EXT = 4224
S5_STATE = 64
S5_SCAN_LANES = 256
M2_INNER = 2048
M2_EXT = 6272
M2_HEADS = 32
M2_GB = 4
VMEM_LIMIT_BYTES = 56 * 1024 * 1024

ADAM_LR, ADAM_B1, ADAM_B2, ADAM_EPS, ADAM_WD, ADAM_STEP = 0.001, 0.9, 0.999, 1e-08, 0.01, 10

_NN = ((1,), (0,))
_NT = ((1,), (1,))
_TN = ((0,), (0,))


def _dot(a, b, dims=_NN):
    return lax.dot_general(a, b, (dims, ((), ())), precision=HI, preferred_element_type=f32)


def _dotb(a, b, dims=_NN):
    return lax.dot_general(a.astype(bf16), b.astype(bf16), (dims, ((), ())), preferred_element_type=f32)


def _bdot(p, q, dims):
    return lax.dot_general(p, q, (dims, ((), ())), preferred_element_type=f32)


def _pieces(x, n):
    out = []
    for _ in range(n - 1):
        p = x.astype(bf16)
        out.append(p)
        x = x - p.astype(f32)
    return out + [x.astype(bf16)]


def _dot01_raw(mask, b, dims=_NN, mask_first=True):
    m = mask.astype(bf16)
    p = _pieces(b, 3)
    if mask_first:
        return _bdot(m, p[0], dims) + (_bdot(m, p[1], dims) + _bdot(m, p[2], dims))
    return _bdot(p[0], m, dims) + (_bdot(p[1], m, dims) + _bdot(p[2], m, dims))


@jax.custom_vjp
def _dot01_nn(mask, b):
    return _dot01_raw(mask, b, _NN)


@jax.custom_vjp
def _dot01_nt(mask, b):
    return _dot01_raw(mask, b, _NT)


_dot01_nn.defvjp(lambda m, b: (_dot01_raw(m, b, _NN), m),
                 lambda m, ct: (jnp.zeros_like(m), _dot01_raw(m, ct, _TN, mask_first=True)))
_dot01_nt.defvjp(lambda m, b: (_dot01_raw(m, b, _NT), m),
                 lambda m, ct: (jnp.zeros_like(m), _dot01_raw(m, ct, _TN, mask_first=False)))


def _dot01_vjp(mask, b, dims=_NN):
    return _dot01_nn(mask, b) if dims == _NN else _dot01_nt(mask, b)


def _dot3_raw(a, b, dims=_NN):
    (ah, al), (bh, bl) = _pieces(a, 2), _pieces(b, 2)
    return _bdot(ah, bh, dims) + (_bdot(ah, bl, dims) + _bdot(al, bh, dims))


@jax.custom_vjp
def _dot3_vjp(a, b):
    return _dot3_raw(a, b)


_dot3_vjp.defvjp(lambda a, b: (_dot3_raw(a, b), (a, b)),
                 lambda res, ct: (_dot3_raw(ct, res[1], _NT), _dot3_raw(res[0], ct, _TN)))


class _Dots:
    def __init__(self, dot3, dot01):
        self.dot3, self.dot01 = dot3, dot01


_PLAIN_DOTS = _Dots(_dot3_raw, _dot01_raw)
_VJP_DOTS = _Dots(_dot3_vjp, _dot01_vjp)


def _iota(shape, dim):
    return lax.broadcasted_iota(jnp.int32, shape, dim)


def _params(n_grid):
    return pltpu.CompilerParams(dimension_semantics=("arbitrary",) * n_grid, vmem_limit_bytes=VMEM_LIMIT_BYTES)


def _row_tile(n_rows):
    return min(512, n_rows)


def _head_rows(n_rows):
    return min(2048, n_rows)


def _mm_rows(n_rows):
    return min(1024, n_rows)


def _col_tile(n, cap=1024):
    best = 128
    for t in range(128, cap + 1, 128):
        if n % t == 0:
            best = t
    return best


def _mm(name, a, b, *, dims, grid, a_spec, b_spec, out_shape, out_spec, aux=(), a_fn=None, epi_fn=None,
        acc_shape, out_init=None, cache_a=False):
    nk = grid[2]
    n_aux = len(aux)
    kinds = [x[2] for x in aux]
    cache_a = cache_a and nk == 1 and grid[1] > 1 and not any(kd == "a" for kd in kinds)

    def body_single(*refs):
        a_ref, b_ref = refs[0], refs[1]
        aux_refs = refs[2:2 + n_aux]
        pos = 2 + n_aux + (1 if out_init is not None else 0)
        o_ref = refs[pos]

        def a_tile():
            av = a_ref[...]
            if a_fn is not None:
                av = a_fn(av, *[r[...] for r, kd in zip(aux_refs, kinds) if kd == "a"])
            return av.astype(bf16)

        if cache_a:
            a_bf = refs[pos + 1]

            @pl.when(pl.program_id(1) == 0)
            def _():
                a_bf[...] = a_tile()

            av = a_bf[...]
        else:
            av = a_tile()
        r = lax.dot_general(av, b_ref[...].astype(bf16), (dims, ((), ())), preferred_element_type=f32)
        if epi_fn is not None:
            r = epi_fn(r, *[x[...] for x, kd in zip(aux_refs, kinds) if kd == "e"])
        o_ref[...] = r.astype(o_ref.dtype)

    def body(*refs):
        a_ref, b_ref = refs[0], refs[1]
        aux_refs = refs[2:2 + n_aux]
        pos = 2 + n_aux + (1 if out_init is not None else 0)
        o_ref, acc_ref = refs[pos], refs[pos + 1]
        k = pl.program_id(2)

        @pl.when(k == 0)
        def _():
            acc_ref[...] = jnp.zeros_like(acc_ref)

        av = a_ref[...]
        if a_fn is not None:
            av = a_fn(av, *[r[...] for r, kd in zip(aux_refs, kinds) if kd == "a"])
        acc_ref[...] += lax.dot_general(av.astype(bf16), b_ref[...].astype(bf16), (dims, ((), ())),
                                        preferred_element_type=f32)

        @pl.when(k == nk - 1)
        def _():
            r = acc_ref[...]
            if epi_fn is not None:
                r = epi_fn(r, *[x[...] for x, kd in zip(aux_refs, kinds) if kd == "e"])
            o_ref[...] = r.astype(o_ref.dtype)

    in_specs = [a_spec, b_spec] + [x[1] for x in aux]
    args = [a, b] + [x[0] for x in aux]
    aliases = {}
    if out_init is not None:
        in_specs.append(pl.BlockSpec(memory_space=pl.ANY))
        args.append(out_init)
        aliases = {len(args) - 1: 0}
    if nk == 1:
        a_block = tuple(d for d in a_spec.block_shape if d is not None)
        scratch = [pltpu.VMEM(a_block, bf16)] if cache_a else []
    else:
        scratch = [pltpu.VMEM(acc_shape, f32)]
    return pl.pallas_call(
        body_single if nk == 1 else body, name=name, grid=grid, in_specs=in_specs, out_specs=out_spec,
        out_shape=out_shape, scratch_shapes=scratch, input_output_aliases=aliases, compiler_params=_params(3),
    )(*args)


def _ew(name, f, ins, outs, grid):
    n_in = len(ins)
    modes = [o[4] for o in outs]

    def body(*refs):
        vals = [r[...] for r in refs[:n_in]]
        res = f(*vals)
        if not isinstance(res, (tuple, list)):
            res = (res,)
        for r, o_ref, mode in zip(res, refs[n_in:], modes):
            if mode is None:
                o_ref[...] = r.astype(o_ref.dtype)
                continue
            first = pl.program_id(1) == 0
            if mode == "all":
                first = jnp.logical_and(first, pl.program_id(0) == 0)

            @pl.when(first)
            def _(r=r, o_ref=o_ref):
                o_ref[...] = r.astype(o_ref.dtype)

            @pl.when(jnp.logical_not(first))
            def _(r=r, o_ref=o_ref):
                o_ref[...] += r.astype(o_ref.dtype)

    res = pl.pallas_call(
        body, name=name, grid=grid,
        in_specs=[pl.BlockSpec(blk, im) for _, blk, im in ins],
        out_specs=[pl.BlockSpec(o[2], o[3]) for o in outs],
        out_shape=[jax.ShapeDtypeStruct(o[0], o[1]) for o in outs],
        compiler_params=_params(2),
    )(*[a for a, _, _ in ins])
    return res


def _vjp_fn(f, n_primal):
    def g(*args):
        _, vjp = jax.vjp(f, *args[:n_primal])
        cts = args[n_primal:]
        return vjp(cts[0] if len(cts) == 1 else tuple(cts))
    return g


def _scan_fwd(name, step, n_state, state_shape, cins, ins, outs, n_units, n_chunks):
    n_c, n_in, n_out = len(cins), len(ins), len(outs)

    def body(*refs):
        c_refs = refs[:n_c]
        in_refs = refs[n_c:n_c + n_in]
        out_refs = refs[n_c + n_in:n_c + n_in + n_out]
        saved = refs[n_c + n_in + n_out:n_c + n_in + n_out + n_state]
        st = refs[n_c + n_in + n_out + n_state:]

        @pl.when(pl.program_id(1) == 0)
        def _():
            for s in st:
                s[...] = jnp.zeros_like(s)

        cur = [s[...] for s in st]
        for sv, s in zip(saved, cur):
            sv[...] = s
        new, res = step(cur, [r[...] for r in c_refs], [r[...] for r in in_refs], _PLAIN_DOTS)
        for s, n in zip(st, new):
            s[...] = n
        for o, r in zip(out_refs, res):
            o[...] = r

    sshape = (n_units, n_chunks) + state_shape
    sblock = (None, None) + state_shape
    nz = len(state_shape)
    res = pl.pallas_call(
        body, name=name, grid=(n_units, n_chunks),
        in_specs=[pl.BlockSpec(e[1], e[2]) for e in cins + ins],
        out_specs=[pl.BlockSpec(o[1], o[2]) for o in outs]
        + [pl.BlockSpec(sblock, lambda u, c: (u, c) + (0,) * nz)] * n_state,
        out_shape=[jax.ShapeDtypeStruct(o[0], f32) for o in outs]
        + [jax.ShapeDtypeStruct(sshape, f32)] * n_state,
        scratch_shapes=[pltpu.VMEM(state_shape, f32)] * n_state,
        compiler_params=_params(2),
    )(*[e[0] for e in cins + ins])
    return res[:n_out], res[n_out:]


def _scan_bwd(name, step, n_state, state_shape, cins, ins, saved, douts, n_units, n_chunks):
    n_c, n_in, n_do = len(cins), len(ins), len(douts)

    def flip(im):
        return lambda u, c: im(u, n_chunks - 1 - c)

    def body(*refs):
        p = 0
        c_refs = refs[p:p + n_c]; p += n_c
        in_refs = refs[p:p + n_in]; p += n_in
        sv_refs = refs[p:p + n_state]; p += n_state
        do_refs = refs[p:p + n_do]; p += n_do
        dc_refs = refs[p:p + n_c]; p += n_c
        di_refs = refs[p:p + n_in]; p += n_in
        dst = refs[p:]
        first = pl.program_id(1) == 0

        @pl.when(first)
        def _():
            for s in dst:
                s[...] = jnp.zeros_like(s)

        def fn(states, consts, vals):
            new, res = step(states, consts, vals, _VJP_DOTS)
            return tuple(new), tuple(res)

        prim = ([r[...] for r in sv_refs], [r[...] for r in c_refs], [r[...] for r in in_refs])
        _, vjp = jax.vjp(fn, *prim)
        d_states, d_consts, d_vals = vjp((tuple(s[...] for s in dst), tuple(r[...] for r in do_refs)))
        for s, g in zip(dst, d_states):
            s[...] = g
        for o, g in zip(di_refs, d_vals):
            o[...] = g
        for o, g in zip(dc_refs, d_consts):
            @pl.when(first)
            def _(o=o, g=g):
                o[...] = g

            @pl.when(jnp.logical_not(first))
            def _(o=o, g=g):
                o[...] += g

    nz = len(state_shape)
    sblock = (None, None) + state_shape
    def gshape(e):
        return e[3] if len(e) == 5 else e[0].shape

    def gmap(e):
        return e[4] if len(e) == 5 else e[2]

    in_specs = ([pl.BlockSpec(e[1], e[2]) for e in cins]
                + [pl.BlockSpec(e[1], flip(e[2])) for e in ins]
                + [pl.BlockSpec(sblock, lambda u, c: (u, n_chunks - 1 - c) + (0,) * nz)] * n_state
                + [pl.BlockSpec(e[1], flip(e[2])) for e in douts])
    out_specs = ([pl.BlockSpec(e[1], e[2]) for e in cins]
                 + [pl.BlockSpec(e[1], flip(gmap(e))) for e in ins])
    out_shape = [jax.ShapeDtypeStruct(gshape(e), f32) for e in cins + ins]
    res = pl.pallas_call(
        body, name=name, grid=(n_units, n_chunks), in_specs=in_specs, out_specs=out_specs, out_shape=out_shape,
        scratch_shapes=[pltpu.VMEM(state_shape, f32)] * n_state,
        compiler_params=_params(2),
    )(*([e[0] for e in cins + ins] + list(saved) + [e[0] for e in douts]))
    return res[:n_c], res[n_c:]


def _rms(x, g):
    return x * lax.rsqrt(jnp.mean(x * x, axis=-1, keepdims=True) + RMS_EPS) * g


def _rows(tm, width):
    return (tm, width), lambda r, z: (r, 0)


def _const(shape):
    return shape, lambda r, z: (0,) * len(shape)


def _rms_fwd(name, h, g):
    L, D = h.shape
    tm = _row_tile(L)
    return _ew(name, _rms, [(h, *_rows(tm, D)), (g, *_const((1, D)))],
               [((L, D), f32, *_rows(tm, D), None)], (L // tm, 1))[0]


def _rms_bwd(name, h, g, d_hn, d_res):
    L, D = h.shape
    tm = _row_tile(L)

    def f(hv, gv, dv, rv):
        dh, dg = _vjp_fn(_rms, 2)(hv, gv, dv)
        return dh + rv, dg

    return _ew(name, f, [(h, *_rows(tm, D)), (g, *_const((1, D))), (d_hn, *_rows(tm, D)), (d_res, *_rows(tm, D))],
               [((L, D), f32, *_rows(tm, D), None), ((1, D), f32, *_const((1, D)), "all")], (L // tm, 1))


def _loss_head(h, g, target):
    L, D = h.shape
    tm = _row_tile(L)

    def f(hv, gv, tv):
        def lf(a, b):
            e = jnp.square(_rms(a, b) - tv)
            return (0.5 / D) * jnp.sum(jnp.sum(e, axis=1, keepdims=True), axis=0, keepdims=True)

        val, vjp = jax.vjp(lf, hv, gv)
        dh, dg = vjp(jnp.ones((1, 1), f32))
        return jnp.broadcast_to(val, (1, 128)), dh, dg

    return _ew("loss_head", f, [(h, *_rows(tm, D)), (g, *_const((1, D))), (target, *_rows(tm, D))],
               [((1, 128), f32, *_const((1, 128)), "all"), ((L, D), f32, *_rows(tm, D), None),
                ((1, D), f32, *_const((1, D)), "all")], (L // tm, 1))


def _sqrelu(x):
    return jnp.square(jnp.maximum(x, 0.0))


def _mm_plain(name, a, b, dims, *, a_fn=None, epi_fn=None, aux=()):
    if dims == _NN:
        (M, K), N = a.shape, b.shape[1]
    elif dims == _NT:
        (M, K), N = a.shape, b.shape[0]
    else:
        (K, M), N = a.shape, b.shape[1]
    tm = _mm_rows(M)
    tn = _col_tile(N, 1536)
    tk = _col_tile(K)
    if dims == _TN:
        tk = min(512, K)
        a_spec = pl.BlockSpec((tk, tm), lambda i, j, k: (k, i))
        b_spec = pl.BlockSpec((tk, tn), lambda i, j, k: (k, j))
        a_aux = pl.BlockSpec((tk, tm), lambda i, j, k: (k, i))
    elif dims == _NT:
        a_spec = pl.BlockSpec((tm, tk), lambda i, j, k: (i, k))
        b_spec = pl.BlockSpec((tn, tk), lambda i, j, k: (j, k))
        a_aux = pl.BlockSpec((tm, tk), lambda i, j, k: (i, k))
    else:
        a_spec = pl.BlockSpec((tm, tk), lambda i, j, k: (i, k))
        b_spec = pl.BlockSpec((tk, tn), lambda i, j, k: (k, j))
        a_aux = pl.BlockSpec((tm, tk), lambda i, j, k: (i, k))
    e_aux = pl.BlockSpec((tm, tn), lambda i, j, k: (i, j))
    aux_full = [(x, a_aux if kd == "a" else e_aux, kd) for x, kd in aux]
    return _mm(name, a, b, dims=dims, grid=(M // tm, N // tn, K // tk), a_spec=a_spec, b_spec=b_spec,
               out_shape=jax.ShapeDtypeStruct((M, N), f32), out_spec=pl.BlockSpec((tm, tn), lambda i, j, k: (i, j)),
               aux=aux_full, a_fn=a_fn, epi_fn=epi_fn, acc_shape=(tm, tn), cache_a=True)


def _mlp_fwd(h, g, w1g, w2g, layer):
    L, D = h.shape
    tm = _mm_rows(L)
    fs = D_FF // N_DEV
    hn = _rms_fwd("mlp_norm", h, g)
    h1 = _mm("mlp_up", hn, w1g, dims=_NN, grid=(L // tm, N_DEV, 1),
             a_spec=pl.BlockSpec((tm, D), lambda i, j, k: (i, 0)),
             b_spec=pl.BlockSpec((None, None, D, fs), lambda i, j, k: (j, layer, 0, 0)),
             out_shape=jax.ShapeDtypeStruct((L, D_FF), f32), out_spec=pl.BlockSpec((tm, fs), lambda i, j, k: (i, j)),
             acc_shape=(tm, fs), cache_a=True)
    tn = D
    h_out = _mm("mlp_down", h1, w2g, dims=_NN, grid=(L // tm, D // tn, N_DEV),
                a_spec=pl.BlockSpec((tm, fs), lambda i, j, k: (i, k)),
                b_spec=pl.BlockSpec((None, None, fs, tn), lambda i, j, k: (k, layer, 0, j)),
                out_shape=jax.ShapeDtypeStruct((L, D), f32), out_spec=pl.BlockSpec((tm, tn), lambda i, j, k: (i, j)),
                aux=[(h, pl.BlockSpec((tm, tn), lambda i, j, k: (i, j)), "e")],
                a_fn=_sqrelu, epi_fn=lambda acc, res: acc + res, acc_shape=(tm, tn))
    return h_out, hn, h1


def _mlp_bwd(dh, h, g, hn, h1, w1g, w2g, wl, layer, dw1_buf, dw2_buf):
    L, D = h.shape
    tm = _mm_rows(L)
    fs = D_FF // N_DEV
    tk = _mm_rows(L)
    dh1 = _mm("mlp_down_dx", dh, w2g, dims=_NT, grid=(L // tm, N_DEV, 1),
              a_spec=pl.BlockSpec((tm, D), lambda i, j, k: (i, 0)),
              b_spec=pl.BlockSpec((None, None, fs, D), lambda i, j, k: (j, wl, 0, 0)),
              out_shape=jax.ShapeDtypeStruct((L, D_FF), f32), out_spec=pl.BlockSpec((tm, fs), lambda i, j, k: (i, j)),
              aux=[(h1, pl.BlockSpec((tm, fs), lambda i, j, k: (i, j)), "e")],
              epi_fn=lambda acc, pre: acc * (2.0 * jnp.maximum(pre, 0.0)), acc_shape=(tm, fs), cache_a=True)
    dw2_buf = _mm("mlp_down_dw", h1, dh, dims=_TN, grid=(N_DEV, 1, L // tk),
                  a_spec=pl.BlockSpec((tk, fs), lambda i, j, k: (k, i)),
                  b_spec=pl.BlockSpec((tk, D), lambda i, j, k: (k, 0)),
                  out_shape=jax.ShapeDtypeStruct(dw2_buf.shape, f32),
                  out_spec=pl.BlockSpec((None, None, fs, D), lambda i, j, k: (i, layer, 0, 0)),
                  a_fn=_sqrelu, acc_shape=(fs, D), out_init=dw2_buf)
    tr = D
    dw1_buf = _mm("mlp_up_dw", hn, dh1, dims=_TN, grid=(D // tr, N_DEV, L // tk),
                  a_spec=pl.BlockSpec((tk, tr), lambda i, j, k: (k, i)),
                  b_spec=pl.BlockSpec((tk, fs), lambda i, j, k: (k, j)),
                  out_shape=jax.ShapeDtypeStruct(dw1_buf.shape, f32),
                  out_spec=pl.BlockSpec((None, None, tr, fs), lambda i, j, k: (j, layer, i, 0)),
                  acc_shape=(tr, fs), out_init=dw1_buf)
    tn = D
    dhn = _mm("mlp_up_dx", dh1, w1g, dims=_NT, grid=(L // tm, D // tn, N_DEV),
              a_spec=pl.BlockSpec((tm, fs), lambda i, j, k: (i, k)),
              b_spec=pl.BlockSpec((None, None, tn, fs), lambda i, j, k: (k, wl, j, 0)),
              out_shape=jax.ShapeDtypeStruct((L, D), f32), out_spec=pl.BlockSpec((tm, tn), lambda i, j, k: (i, j)),
              acc_shape=(tm, tn))
    dh_in, dg = _rms_bwd("mlp_norm_bwd", h, g, dhn, dh)
    return dh_in, dg, dw1_buf, dw2_buf


def _shift_dn(x, s, row):
    return x if s == 0 else jnp.where(row >= s, pltpu.roll(x, s, 0), 0.0)


def _shift_up(x, s, row):
    n = x.shape[0]
    return x if s == 0 else jnp.where(row < n - s, pltpu.roll(x, n - s, 0), 0.0)


def _conv_pre(x, w, b, row):
    c = jnp.broadcast_to(b, x.shape)
    for j in range(4):
        c = c + w[j:j + 1, :] * _shift_dn(x, 3 - j, row)
    return c


def _conv_fwd(name, x_arr, blk_off, w, b):
    L = x_arr.shape[0]
    C = w.shape[1]

    def f(x, wv, bv):
        c = _conv_pre(x, wv, bv, _iota(x.shape, 0))
        return c * jax.nn.sigmoid(c)

    return _ew(name, f, [(x_arr, (L, 128), lambda j, z: (0, blk_off + j)), (w, (4, 128), lambda j, z: (0, j)),
                         (b, (1, 128), lambda j, z: (0, j))],
               [((L, C), f32, (L, 128), lambda j, z: (0, j), None)], (C // 128, 1))[0]


def _conv_bwd(name, x_arr, blk_off, w, b, dy):
    L = x_arr.shape[0]
    C = w.shape[1]

    def f(x, wv, bv, g):
        row = _iota(x.shape, 0)
        c = _conv_pre(x, wv, bv, row)
        s = jax.nn.sigmoid(c)
        dc = g * (s * (1.0 + c * (1.0 - s)))
        dx = jnp.zeros_like(x)
        dw = jnp.zeros((4, 128), f32)
        r4 = _iota((4, 128), 0)
        for j in range(4):
            dx = dx + wv[j:j + 1, :] * _shift_up(dc, 3 - j, row)
            dwj = jnp.sum(dc * _shift_dn(x, 3 - j, row), axis=0, keepdims=True)
            dw = dw + jnp.where(r4 == j, jnp.broadcast_to(dwj, (4, 128)), 0.0)
        return dx, dw, jnp.sum(dc, axis=0, keepdims=True)

    return _ew(name, f, [(x_arr, (L, 128), lambda j, z: (0, blk_off + j)), (w, (4, 128), lambda j, z: (0, j)),
                         (b, (1, 128), lambda j, z: (0, j)), (dy, (L, 128), lambda j, z: (0, j))],
               [((L, C), f32, (L, 128), lambda j, z: (0, j), None), ((4, C), f32, (4, 128), lambda j, z: (0, j), None),
                ((1, C), f32, (1, 128), lambda j, z: (0, j), None)], (C // 128, 1))


def _l2norm(t):
    return t * lax.rsqrt(jnp.sum(t * t, axis=-1, keepdims=True) + 1e-6)


def _gdn_act(cq, ck, ab, alog, dtb):
    h = pl.program_id(1)
    qn = _l2norm(cq) * (128.0 ** -0.5)
    kn = _l2norm(ck)
    lane = _iota(ab.shape, 1)
    a_raw = jnp.sum(jnp.where(lane == h, ab, 0.0), axis=1, keepdims=True)
    b_raw = jnp.sum(jnp.where(lane == h + GDN_HEADS, ab, 0.0), axis=1, keepdims=True)
    lane1 = _iota(alog.shape, 1)
    al = jnp.sum(jnp.where(lane1 == h, alog, 0.0), axis=1, keepdims=True)
    db = jnp.sum(jnp.where(lane1 == h, dtb, 0.0), axis=1, keepdims=True)
    g = -jnp.exp(al) * jax.nn.softplus(a_raw + db)
    beta = jax.nn.sigmoid(b_raw)
    return qn, kn, jnp.broadcast_to(g, cq.shape), jnp.broadcast_to(beta, cq.shape)


def _each(f, *lists):
    return [f(*a) for a in zip(*lists)]


def _gdn_chunk(states, consts, vals, dots):
    S = list(states)
    cut = [slice(128 * i, 128 * i + 128) for i in range(len(S))]
    q, k, v, gb, bb = ([t[:, c] for c in cut] for t in vals)
    C = vals[0].shape[0]
    row, col = _iota((C, C), 0), _iota((C, C), 1)
    causal, strict = row >= col, row > col
    ltri = causal.astype(f32)
    eye = (row == col).astype(f32)
    e0 = (_iota((C, 128), 1) == 0).astype(f32)
    last = _iota((C, 1), 0) == C - 1
    Gb = _each(lambda g: dots.dot01(ltri, g), gb)
    Gc = _each(lambda g: jnp.mean(g, axis=1, keepdims=True), Gb)
    Gr = _each(lambda g: dots.dot01(e0, g, _NT), Gb)
    bc = _each(lambda b: jnp.mean(b, axis=1, keepdims=True), bb)
    decay = _each(lambda gc, gr: jnp.where(causal, jnp.exp(jnp.where(causal, gc - gr, 0.0)), 0.0), Gc, Gr)
    kk = _each(lambda a: _dotb(a, a, _NT), k)
    A = _each(lambda b, x, d: jnp.where(strict, b * x * d, 0.0), bc, kk, decay)
    M = _each(lambda a: eye - a, A)
    P = _each(lambda a: dots.dot3(a, a), A)
    for it in range(5):
        M = _each(lambda m, p: m + dots.dot3(m, p), M, P)
        if it < 4:
            P = _each(lambda p: dots.dot3(p, p), P)
    eG = _each(jnp.exp, Gc)
    u = _each(lambda m, x, b: _dotb(m, x * b), M, v, bc)
    w = _each(lambda m, x, b, e: _dotb(m, x * (b * e)), M, k, bc, eG)
    qk = _each(lambda a, b, d: _dotb(a, b, _NT) * d, q, k, decay)
    g_last = _each(lambda gc: jnp.sum(jnp.where(last, gc, 0.0), axis=0, keepdims=True), Gc)
    v_new = _each(lambda a, b, s: a - _dotb(b, s), u, w, S)
    o = _each(lambda a, e, s, b, x: _dotb(a * e, s) + _dotb(b, x), q, eG, S, qk, v_new)
    S_new = _each(lambda gl, s, a, gc, x: jnp.exp(gl) * s + _dotb(a * jnp.exp(gl - gc), x, _TN), g_last, S, k, Gc, v_new)
    return S_new, [jnp.concatenate(o, axis=1)]


def _gdn_post(o, gate, g):
    return _rms(o, g) * (gate * jax.nn.sigmoid(gate))


def _pad_row(v):
    return jnp.pad(v.astype(f32), (0, 128 - v.shape[0])).reshape(1, 128)


def _gdn_fwd(h, g_norm, w_ext, conv_w, a_log, dt_bias, o_norm_g, w_out):
    L, D = h.shape
    tm = _head_rows(L)
    nc = L // CHUNK
    H = GDN_HEADS
    hn = _rms_fwd("mix_norm", h, g_norm)
    proj = _mm_plain("gdn_in", hn, w_ext, _NN)
    zb = jnp.zeros((1, 3 * D), f32)
    cq = _conv_fwd("gdn_conv", proj, 0, conv_w, zb)
    alog, dtb = _pad_row(a_log), _pad_row(dt_bias)
    act_ins = [(cq, (tm, 128), lambda r, hh: (r, hh)), (cq, (tm, 128), lambda r, hh: (r, H + hh)),
               (proj, (tm, 128), lambda r, hh: (r, 4 * H)), (alog, (1, 128), lambda r, hh: (0, 0)),
               (dtb, (1, 128), lambda r, hh: (0, 0))]
    qn, kn, gb, bb = _ew("gdn_act", _gdn_act, act_ins,
                         [((L, D), f32, (tm, 128), lambda r, hh: (r, hh), None)] * 4, (L // tm, H))
    cblk = (CHUNK, 128 * GDN_HB)
    core_ins = [(qn, cblk, lambda u, c: (c, u)), (kn, cblk, lambda u, c: (c, u)),
                (cq, cblk, lambda u, c: (c, 2 * H // GDN_HB + u), (L, D), lambda u, c: (c, u)),
                (gb, cblk, lambda u, c: (c, u)), (bb, cblk, lambda u, c: (c, u))]
    (o,), saved_s = _scan_fwd("gdn_core", _gdn_chunk, GDN_HB, (128, 128), [], core_ins,
                              [((L, D), cblk, lambda u, c: (c, u))], H // GDN_HB, nc)
    on = o_norm_g.reshape(1, 128)
    post_ins = [(o, (tm, 128), lambda r, hh: (r, hh)), (proj, (tm, 128), lambda r, hh: (r, 3 * H + hh)),
                (on, (1, 128), lambda r, hh: (0, 0))]
    y = _ew("gdn_post", _gdn_post, post_ins, [((L, D), f32, (tm, 128), lambda r, hh: (r, hh), None)], (L // tm, H))[0]
    h_out = _mm_plain("gdn_out", y, w_out, _NN, epi_fn=lambda acc, res: acc + res, aux=[(h, "e")])
    saved = dict(hn=hn, proj=proj, cq=cq, alog=alog, dtb=dtb, act_ins=act_ins, core_ins=core_ins, saved_s=saved_s,
                 post_ins=post_ins, y=y, zb=zb)
    return h_out, saved


def _gdn_bwd(dh, h, g_norm, w_ext, conv_w, w_out, sv):
    L, D = h.shape
    tm = _head_rows(L)
    nc = L // CHUNK
    H = GDN_HEADS
    dy = _mm_plain("gdn_out_dx", dh, w_out, _NT)
    dw_out = _mm_plain("gdn_out_dw", sv["y"], dh, _TN)
    hd = ((L, D), f32, (tm, 128), lambda r, hh: (r, hh), None)
    d_o, d_gate, d_on = _ew("gdn_post_bwd", _vjp_fn(_gdn_post, 3),
                            sv["post_ins"] + [(dy, (tm, 128), lambda r, hh: (r, hh))],
                            [hd, hd, ((1, 128), f32, (1, 128), lambda r, hh: (0, 0), "all")], (L // tm, H))
    cblk = (CHUNK, 128 * GDN_HB)
    _, (dqn, dkn, dv, dgb, dbb) = _scan_bwd("gdn_core_bwd", _gdn_chunk, GDN_HB, (128, 128), [], sv["core_ins"],
                                            sv["saved_s"], [(d_o, cblk, lambda u, c: (c, u))], H // GDN_HB, nc)
    cts = [(t, (tm, 128), lambda r, hh: (r, hh)) for t in (dqn, dkn, dgb, dbb)]
    row128 = ((1, 128), f32, (1, 128), lambda r, hh: (0, 0), "all")
    d_cq, d_ck, d_ab, d_alog, d_dtb = _ew(
        "gdn_act_bwd", _vjp_fn(_gdn_act, 5), sv["act_ins"] + cts,
        [hd, hd, ((L, 128), f32, (tm, 128), lambda r, hh: (r, 0), "inner"), row128, row128], (L // tm, H))
    d_conv_out = jnp.concatenate([d_cq, d_ck, dv], axis=1)
    d_conv_in, d_conv_w, _ = _conv_bwd("gdn_conv_bwd", sv["proj"], 0, conv_w, sv["zb"], d_conv_out)
    d_proj = jnp.concatenate([d_conv_in, d_gate, d_ab], axis=1)
    dw_ext = _mm_plain("gdn_in_dw", sv["hn"], d_proj, _TN)
    dhn = _mm_plain("gdn_in_dx", d_proj, w_ext, _NT)
    dh_in, dg = _rms_bwd("mix_norm_bwd", h, g_norm, dhn, dh)
    grads = dict(norm=dg, w_ext=dw_ext, conv_w=d_conv_w, a_log=d_alog[0, :H], dt_bias=d_dtb[0, :H],
                 o_norm_g=d_on[0], w_out=dw_out)
    return dh_in, grads


def _expand_lanes(row, width, rep):
    sel = ((_iota((128, width), 1) // rep) == _iota((128, width), 0)).astype(f32)
    return jnp.mean(_dot(jnp.broadcast_to(row, (8, 128)), sel), axis=0, keepdims=True)


def _s5_params(lre, lim, ldt, wbr, wbi):
    dt = jnp.exp(_expand_lanes(ldt, 512, S5_STATE))
    mag = jnp.exp(lre * dt)
    ang = lim * dt
    abr, abi = mag * jnp.cos(ang), mag * jnp.sin(ang)
    nr = abr - 1.0
    den = lre * lre + lim * lim
    cr = (nr * lre + abi * lim) / den
    ci = (abi * lre - nr * lim) / den
    return abr, abi, cr * wbr - ci * wbi, cr * wbi + ci * wbr


def _s5_scan(name, xr, xi, ar, ai, rev, want_prev):
    L, W = xr.shape
    nb = L // 8
    n_out = 4 if want_prev else 2

    def body(xr_ref, xi_ref, ar_ref, ai_ref, *outs):
        a_r = ar_ref[...]
        a_i = -ai_ref[...] if rev else ai_ref[...]

        def cm(p, q):
            return p[0] * q[0] - p[1] * q[1], p[0] * q[1] + p[1] * q[0]

        a1 = (a_r, a_i)
        a2 = cm(a1, a1)
        a3 = cm(a2, a1)
        a4 = cm(a2, a2)
        pw = [a1, a2, a3, a4, cm(a4, a1), cm(a4, a2), cm(a4, a3), cm(a4, a4)]
        blk8 = (8, S5_SCAN_LANES)
        row = _iota(blk8, 0)
        tab_r = jnp.zeros(blk8, f32)
        tab_i = jnp.zeros(blk8, f32)
        for t in range(8):
            idx = 7 - t if rev else t
            tab_r = jnp.where(row == idx, jnp.broadcast_to(pw[t][0], blk8), tab_r)
            tab_i = jnp.where(row == idx, jnp.broadcast_to(pw[t][1], blk8), tab_i)
        lv = [(d, jnp.broadcast_to(p[0], blk8), jnp.broadcast_to(p[1], blk8)) for d, p in ((1, a1), (2, a2), (4, a4))]

        def step(i, carry):
            cr, ci = carry
            blk = nb - 1 - i if rev else i
            r0 = pl.multiple_of(blk * 8, 8)
            x_r = xr_ref[pl.ds(r0, 8), :]
            x_i = xi_ref[pl.ds(r0, 8), :]
            for d, p_r, p_i in lv:
                if rev:
                    s_r = jnp.where(row < 8 - d, pltpu.roll(x_r, 8 - d, 0), 0.0)
                    s_i = jnp.where(row < 8 - d, pltpu.roll(x_i, 8 - d, 0), 0.0)
                else:
                    s_r = jnp.where(row >= d, pltpu.roll(x_r, d, 0), 0.0)
                    s_i = jnp.where(row >= d, pltpu.roll(x_i, d, 0), 0.0)
                x_r, x_i = x_r + p_r * s_r - p_i * s_i, x_i + p_r * s_i + p_i * s_r
            x_r, x_i = x_r + tab_r * cr - tab_i * ci, x_i + tab_r * ci + tab_i * cr
            outs[0][pl.ds(r0, 8), :] = x_r
            outs[1][pl.ds(r0, 8), :] = x_i
            if want_prev:
                outs[2][pl.ds(r0, 8), :] = jnp.where(row >= 1, pltpu.roll(x_r, 1, 0), cr)
                outs[3][pl.ds(r0, 8), :] = jnp.where(row >= 1, pltpu.roll(x_i, 1, 0), ci)
            e = 0 if rev else 7
            return jnp.broadcast_to(x_r[e:e + 1, :], blk8), jnp.broadcast_to(x_i[e:e + 1, :], blk8)

        lax.fori_loop(0, nb, step, (jnp.zeros(blk8, f32), jnp.zeros(blk8, f32)))

    per = 512 // S5_SCAN_LANES
    col = pl.BlockSpec((L, S5_SCAN_LANES), lambda q, z: (0, q))
    aspec = pl.BlockSpec((None, 1, S5_SCAN_LANES), lambda q, z: (q // per, 0, q % per))
    return pl.pallas_call(
        body, name=name, grid=(W // S5_SCAN_LANES, 1), in_specs=[col, col, aspec, aspec], out_specs=[col] * n_out,
        out_shape=[jax.ShapeDtypeStruct((L, W), f32)] * n_out, compiler_params=_params(2),
    )(xr, xi, ar, ai)


def _blockdiag(t, n_in, n_out):
    t4 = t.reshape(8, 8, n_in, n_out)
    return jnp.einsum("jaio,ab->jaibo", t4, jnp.eye(8, dtype=t.dtype)).reshape(8, 8 * n_in, 8 * n_out)


def _blockdiag_t(w, n_in, n_out):
    w5 = w.reshape(8, 8, n_in, 8, n_out)
    return jnp.einsum("jaibo,ab->jaio", w5, jnp.eye(8, dtype=w.dtype)).reshape(64, n_in, n_out)


def _glu(ag, h):
    n = ag.shape[1] // 2
    return h + ag[:, :n] * jax.nn.sigmoid(ag[:, n:])


def _s5_fwd(h, g_norm, w_in, lam_re, lam_im, log_dt, b_re, b_im, c_re, c_im, d_skip, w_out_g):
    L, D = h.shape
    tm, te = _mm_rows(L), _row_tile(L)
    W = 8 * 512
    hn = _rms_fwd("mix_norm", h, g_norm)
    u = _mm_plain("s5_in", hn, w_in, _NN)
    lre, lim = lam_re.reshape(8, 1, 512), lam_im.reshape(8, 1, 512)
    ldt = jnp.pad(log_dt.reshape(8, 1, 8), ((0, 0), (0, 0), (0, 120)))
    wbr = _blockdiag(b_re.transpose(0, 2, 1), 16, 64)
    wbi = _blockdiag(b_im.transpose(0, 2, 1), 16, 64)
    wcr = _blockdiag(c_re.transpose(0, 2, 1), 64, 16)
    wci = _blockdiag(c_im.transpose(0, 2, 1), 64, 16)
    jb = lambda shape: (shape, lambda j, z: (j, 0, 0))
    par_ins = [(lre, *jb((None, 1, 512))), (lim, *jb((None, 1, 512))), (ldt, *jb((None, 1, 128))),
               (wbr, *jb((None, 128, 512))), (wbi, *jb((None, 128, 512)))]
    abr, abi, bbr, bbi = _ew("s5_params", _s5_params, par_ins,
                             [((8, 1, 512), f32, *jb((None, 1, 512)), None)] * 2
                             + [((8, 128, 512), f32, *jb((None, 128, 512)), None)] * 2, (8, 1))

    def bu(name, wb):
        return _mm(name, u, wb, dims=_NN, grid=(L // tm, 8, 1),
                   a_spec=pl.BlockSpec((tm, 128), lambda i, j, k: (i, j)),
                   b_spec=pl.BlockSpec((None, 128, 512), lambda i, j, k: (j, 0, 0)),
                   out_shape=jax.ShapeDtypeStruct((L, W), f32), out_spec=pl.BlockSpec((tm, 512), lambda i, j, k: (i, j)),
                   acc_shape=(tm, 512))

    bur, bui = bu("s5_bu", bbr), bu("s5_bu", bbi)
    sr, si, pr, pi = _s5_scan("s5_scan", bur, bui, abr, abi, False, True)
    d_row = d_skip.reshape(1, D)
    cspec = dict(a_spec=pl.BlockSpec((tm, 512), lambda i, j, k: (i, j)),
                 b_spec=pl.BlockSpec((None, 512, 128), lambda i, j, k: (j, 0, 0)),
                 out_shape=jax.ShapeDtypeStruct((L, D), f32), out_spec=pl.BlockSpec((tm, 128), lambda i, j, k: (i, j)),
                 acc_shape=(tm, 128))
    e128 = pl.BlockSpec((tm, 128), lambda i, j, k: (i, j))
    pre1 = _mm("s5_c_re", sr, wcr, dims=_NN, grid=(L // tm, 8, 1), **cspec)
    pre = _mm("s5_c_im", si, wci, dims=_NN, grid=(L // tm, 8, 1),
              aux=[(pre1, e128, "e"), (u, e128, "e"), (d_row, pl.BlockSpec((1, 128), lambda i, j, k: (0, j)), "e")],
              epi_fn=lambda acc, p1, uu, dd: p1 - acc + dd * uu, **cspec)
    ws = D // N_DEV * 2
    ag = _mm("s5_out", pre, w_out_g, dims=_NN, grid=(L // tm, N_DEV, 1),
             a_spec=pl.BlockSpec((tm, D), lambda i, j, k: (i, 0)),
             b_spec=pl.BlockSpec((None, None, D, ws), lambda i, j, k: (j, 0, 0, 0)),
             out_shape=jax.ShapeDtypeStruct((L, 2 * D), f32), out_spec=pl.BlockSpec((tm, ws), lambda i, j, k: (i, j)),
             a_fn=jax.nn.gelu, acc_shape=(tm, ws), cache_a=True)
    h_out = _ew("s5_glu", _glu, [(ag, *_rows(te, 2 * D)), (h, *_rows(te, D))],
                [((L, D), f32, *_rows(te, D), None)], (L // te, 1))[0]
    saved = dict(hn=hn, u=u, par_ins=par_ins, abr=abr, abi=abi, bbr=bbr, bbi=bbi, wcr=wcr, wci=wci, sr=sr, si=si,
                 pr=pr, pi=pi, pre=pre, ag=ag, d_row=d_row)
    return h_out, saved


def _s5_bwd(dh, h, g_norm, w_in, w_out_g, sv):
    L, D = h.shape
    tm, te = _mm_rows(L), _row_tile(L)
    tk = min(512, L)
    W = 8 * 512
    ws = D // N_DEV * 2
    u, pre, d_row = sv["u"], sv["pre"], sv["d_row"]
    d_ag = _ew("s5_glu_bwd", lambda ag, hv, g: _vjp_fn(_glu, 2)(ag, hv, g)[0],
               [(sv["ag"], *_rows(te, 2 * D)), (h, *_rows(te, D)), (dh, *_rows(te, D))],
               [((L, 2 * D), f32, *_rows(te, 2 * D), None)], (L // te, 1))[0]
    tr = D
    dw_out = _mm("s5_out_dw", pre, d_ag, dims=_TN, grid=(D // tr, N_DEV, L // tk),
                 a_spec=pl.BlockSpec((tk, tr), lambda i, j, k: (k, i)),
                 b_spec=pl.BlockSpec((tk, ws), lambda i, j, k: (k, j)),
                 out_shape=jax.ShapeDtypeStruct((N_DEV, 1, D, ws), f32),
                 out_spec=pl.BlockSpec((None, None, tr, ws), lambda i, j, k: (j, 0, i, 0)),
                 a_fn=jax.nn.gelu, acc_shape=(tr, ws))
    tn = 512
    dpre = _mm("s5_out_dx", d_ag, w_out_g, dims=_NT, grid=(L // tm, D // tn, N_DEV),
               a_spec=pl.BlockSpec((tm, ws), lambda i, j, k: (i, k)),
               b_spec=pl.BlockSpec((None, None, tn, ws), lambda i, j, k: (k, 0, j, 0)),
               out_shape=jax.ShapeDtypeStruct((L, D), f32), out_spec=pl.BlockSpec((tm, tn), lambda i, j, k: (i, j)),
               aux=[(pre, pl.BlockSpec((tm, tn), lambda i, j, k: (i, j)), "e")],
               epi_fn=lambda acc, p: _vjp_fn(jax.nn.gelu, 1)(p, acc)[0], acc_shape=(tm, tn))
    d_d = _ew("s5_dskip", lambda a, b: jnp.sum(a * b, axis=0, keepdims=True),
              [(dpre, *_rows(te, D)), (u, *_rows(te, D))], [((1, D), f32, *_const((1, D)), "all")], (L // te, 1))[0]
    neg = lambda acc: -acc
    dsspec = dict(dims=_NT, grid=(L // tm, 8, 1), a_spec=pl.BlockSpec((tm, 128), lambda i, j, k: (i, j)),
                  b_spec=pl.BlockSpec((None, 512, 128), lambda i, j, k: (j, 0, 0)),
                  out_shape=jax.ShapeDtypeStruct((L, W), f32), out_spec=pl.BlockSpec((tm, 512), lambda i, j, k: (i, j)),
                  acc_shape=(tm, 512))
    dsr = _mm("s5_c_re_dx", dpre, sv["wcr"], **dsspec)
    dsi = _mm("s5_c_im_dx", dpre, sv["wci"], epi_fn=neg, **dsspec)
    dwspec = dict(dims=_TN, grid=(8, 1, L // tk), a_spec=pl.BlockSpec((tk, 512), lambda i, j, k: (k, i)),
                  b_spec=pl.BlockSpec((tk, 128), lambda i, j, k: (k, i)),
                  out_shape=jax.ShapeDtypeStruct((8, 512, 128), f32),
                  out_spec=pl.BlockSpec((None, 512, 128), lambda i, j, k: (i, 0, 0)), acc_shape=(512, 128))
    dwcr = _mm("s5_c_re_dw", sv["sr"], dpre, **dwspec)
    dwci = _mm("s5_c_im_dw", sv["si"], dpre, epi_fn=neg, **dwspec)
    lr, li = _s5_scan("s5_scan_bwd", dsr, dsi, sv["abr"], sv["abi"], True, False)

    def da(lrv, liv, prv, piv):
        return (jnp.sum(lrv * prv + liv * piv, axis=0, keepdims=True),
                jnp.sum(liv * prv - lrv * piv, axis=0, keepdims=True))

    sblk = ((te, 512), lambda j, r: (r, j))
    dabr, dabi = _ew("s5_dlam", da, [(lr, *sblk), (li, *sblk), (sv["pr"], *sblk), (sv["pi"], *sblk)],
                     [((8, 1, 512), f32, (None, 1, 512), lambda j, r: (j, 0, 0), "inner")] * 2, (8, L // te))
    dbspec = dict(dims=_TN, grid=(8, 1, L // tk), a_spec=pl.BlockSpec((tk, 128), lambda i, j, k: (k, i)),
                  b_spec=pl.BlockSpec((tk, 512), lambda i, j, k: (k, i)),
                  out_shape=jax.ShapeDtypeStruct((8, 128, 512), f32),
                  out_spec=pl.BlockSpec((None, 128, 512), lambda i, j, k: (i, 0, 0)), acc_shape=(128, 512))
    dbbr = _mm("s5_bu_dw", u, lr, **dbspec)
    dbbi = _mm("s5_bu_dw", u, li, **dbspec)
    duspec = dict(dims=_NT, grid=(L // tm, 8, 1), a_spec=pl.BlockSpec((tm, 512), lambda i, j, k: (i, j)),
                  b_spec=pl.BlockSpec((None, 128, 512), lambda i, j, k: (j, 0, 0)),
                  out_shape=jax.ShapeDtypeStruct((L, D), f32), out_spec=pl.BlockSpec((tm, 128), lambda i, j, k: (i, j)),
                  acc_shape=(tm, 128))
    e128 = pl.BlockSpec((tm, 128), lambda i, j, k: (i, j))
    du1 = _mm("s5_bu_dx_re", lr, sv["bbr"], **duspec)
    du = _mm("s5_bu_dx_im", li, sv["bbi"],
             aux=[(du1, e128, "e"), (dpre, e128, "e"), (d_row, pl.BlockSpec((1, 128), lambda i, j, k: (0, j)), "e")],
             epi_fn=lambda acc, d1, dp, dd: acc + d1 + dp * dd, **duspec)
    jb = lambda shape: (shape, lambda j, z: (j, 0, 0))
    cts = [(dabr, *jb((None, 1, 512))), (dabi, *jb((None, 1, 512))), (dbbr, *jb((None, 128, 512))),
           (dbbi, *jb((None, 128, 512)))]
    dlre, dlim, dldt, dwbr, dwbi = _ew(
        "s5_params_bwd", _vjp_fn(_s5_params, 5), sv["par_ins"] + cts,
        [((8, 1, 512), f32, *jb((None, 1, 512)), None)] * 2 + [((8, 1, 128), f32, *jb((None, 1, 128)), None)]
        + [((8, 128, 512), f32, *jb((None, 128, 512)), None)] * 2, (8, 1))
    dw_in = _mm_plain("s5_in_dw", sv["hn"], du, _TN)
    dhn = _mm_plain("s5_in_dx", du, w_in, _NT)
    dh_in, dg = _rms_bwd("mix_norm_bwd", h, g_norm, dhn, dh)
    grads = dict(norm=dg, w_in=dw_in, lam_re=dlre.reshape(64, 64), lam_im=dlim.reshape(64, 64),
                 log_dt=dldt[:, 0, :8].reshape(64),
                 b_re=_blockdiag_t(dwbr, 16, 64).transpose(0, 2, 1), b_im=_blockdiag_t(dwbi, 16, 64).transpose(0, 2, 1),
                 c_re=_blockdiag_t(dwcr, 64, 16).transpose(0, 2, 1), c_im=_blockdiag_t(dwci, 64, 16).transpose(0, 2, 1),
                 d=d_d[0], w_out=dw_out)
    return dh_in, grads


def _m2_act(dt_raw, dtbias, alog):
    dt = jax.nn.softplus(dt_raw + dtbias)
    da = dt * (-jnp.exp(alog))
    sel = ((_iota((128, M2_INNER), 1) // 64) == _iota((128, M2_INNER), 0)).astype(f32)
    return _dot(dt, sel), _dot(da, sel)


def _m2_dexp(d):
    return _expand_lanes(d, M2_INNER, 64)


def _ssd_chunk(states, consts, vals, dots):
    (dsk,) = consts
    S = list(states)
    n = len(S)
    cut = [slice(128 * i, 128 * i + 128) for i in range(n)]
    x, dtb, dab = ([t[:, c] for c in cut] for t in vals[:3])
    dsk = [dsk[:, c] for c in cut]
    B = [vals[3][:, cut[i // 2]] for i in range(n)]
    Cm = [vals[4][:, cut[i // 2]] for i in range(n)]
    C = vals[0].shape[0]
    row, col = _iota((C, C), 0), _iota((C, C), 1)
    causal = row >= col
    ltri = causal.astype(f32)
    lane = _iota((C, 128), 1)
    last = _iota((C, 128), 0) == C - 1
    eye128 = _iota((128, 128), 0) == _iota((128, 128), 1)
    head = [jnp.logical_and(lane >= 64 * hh, lane < 64 * hh + 64) for hh in range(2)]
    pick = [(lane == 64 * hh).astype(f32) for hh in range(2)]
    xdt = _each(lambda a, b: a * b, x, dtb)
    cb = _each(lambda c, b: _dotb(c, b, _NT), Cm[::2], B[::2])
    cum = _each(lambda a: dots.dot01(ltri, a), dab)
    clast = _each(lambda a: jnp.sum(jnp.where(last, a, 0.0), axis=0, keepdims=True), cum)
    st = _each(lambda a, cl, cu, b: _dotb(a * jnp.exp(cl - cu), b, _TN), xdt, clast, cum, B)
    y = _each(lambda c, s, cu: _dotb(c, s, _NT) * jnp.exp(cu), Cm, S, cum)
    for hh in range(2):
        ccol = _each(lambda cu: jnp.sum(jnp.where(head[hh], cu, 0.0), axis=1, keepdims=True) * (1.0 / 64), cum)
        crow = _each(lambda cu: dots.dot01(pick[hh], cu, _NT), cum)
        lm = _each(lambda a, b: jnp.where(causal, jnp.exp(jnp.where(causal, a - b, 0.0)), 0.0), ccol, crow)
        y = [y[i] + _dotb(cb[i // 2] * lm[i], jnp.where(head[hh], xdt[i], 0.0)) for i in range(n)]
    cdcol = _each(lambda cl: jnp.sum(jnp.where(eye128, jnp.broadcast_to(jnp.exp(cl), (128, 128)), 0.0),
                                     axis=1, keepdims=True), clast)
    S_new = _each(lambda c, s, t: c * s + t, cdcol, S, st)
    out = _each(lambda a, d, b: a + d * b, y, dsk, x)
    return S_new, [jnp.concatenate(out, axis=1)]


def _m2_post(yc, z, ng):
    return _rms(yc * (z * jax.nn.sigmoid(z)), ng)


def _m2_fwd(h, g_norm, w_ext, conv_w, conv_b, dt_bias, a_log, d_skip, norm_g, w_out):
    L, D = h.shape
    tm = _row_tile(L)
    nc = L // CHUNK
    NI = M2_INNER
    hn = _rms_fwd("mix_norm", h, g_norm)
    proj = _mm_plain("m2_in", hn, w_ext, _NN)
    xbc = _conv_fwd("m2_conv", proj, NI // 128, conv_w, conv_b)
    dtb_row, alog_row, d_pad = _pad_row(dt_bias), _pad_row(a_log), _pad_row(d_skip)
    act_ins = [(proj, (tm, 128), lambda r, z: (r, 3 * NI // 128)), (dtb_row, *_const((1, 128))),
               (alog_row, *_const((1, 128)))]
    dtb, dab = _ew("m2_act", _m2_act, act_ins, [((L, NI), f32, *_rows(tm, NI), None)] * 2, (L // tm, 1))
    dsk = _ew("m2_dexp", _m2_dexp, [(d_pad, *_const((1, 128)))], [((1, NI), f32, *_const((1, NI)), None)], (1, 1))[0]
    GB = M2_GB
    x_blk, bc_blk = (CHUNK, 256 * GB), (CHUNK, 128 * GB)
    cins = [(dsk, (1, 256 * GB), lambda u, c: (0, u))]
    core_ins = [(xbc, x_blk, lambda u, c: (c, u), (L, NI), lambda u, c: (c, u)),
                (dtb, x_blk, lambda u, c: (c, u)), (dab, x_blk, lambda u, c: (c, u)),
                (xbc, bc_blk, lambda u, c: (c, 16 // GB + u), (L, D), lambda u, c: (c, u)),
                (xbc, bc_blk, lambda u, c: (c, 24 // GB + u), (L, D), lambda u, c: (c, u))]
    (yc,), saved_s = _scan_fwd("m2_core", _ssd_chunk, 2 * GB, (128, 128), cins, core_ins,
                               [((L, NI), x_blk, lambda u, c: (c, u))], 8 // GB, nc)
    tp = _head_rows(L)
    gblk = ((tp, 256), lambda g, r: (r, g))
    post_ins = [(yc, *gblk), (proj, *gblk), (norm_g, (1, 256), lambda g, r: (0, g))]
    yn = _ew("m2_post", _m2_post, post_ins, [((L, NI), f32, *gblk, None)], (8, L // tp))[0]
    h_out = _mm_plain("m2_out", yn, w_out, _NN, epi_fn=lambda acc, res: acc + res, aux=[(h, "e")])
    saved = dict(hn=hn, proj=proj, act_ins=act_ins, d_pad=d_pad, cins=cins, core_ins=core_ins, saved_s=saved_s,
                 post_ins=post_ins, yn=yn)
    return h_out, saved


def _m2_bwd(dh, h, g_norm, w_ext, conv_w, conv_b, w_out, sv):
    L, D = h.shape
    tm = _row_tile(L)
    nc = L // CHUNK
    NI = M2_INNER
    dyn = _mm_plain("m2_out_dx", dh, w_out, _NT)
    dw_out = _mm_plain("m2_out_dw", sv["yn"], dh, _TN)
    tp = _head_rows(L)
    gblk = ((tp, 256), lambda g, r: (r, g))
    d_yc, d_z, d_ng = _ew("m2_post_bwd", _vjp_fn(_m2_post, 3), sv["post_ins"] + [(dyn, *gblk)],
                          [((L, NI), f32, *gblk, None)] * 2 + [((1, NI), f32, (1, 256), lambda g, r: (0, g), "inner")],
                          (8, L // tp))
    (d_dsk,), (dx, d_dtb, d_dab, dB, dC) = _scan_bwd(
        "m2_core_bwd", _ssd_chunk, 2 * M2_GB, (128, 128), sv["cins"], sv["core_ins"], sv["saved_s"],
        [(d_yc, (CHUNK, 256 * M2_GB), lambda u, c: (c, u))], 8 // M2_GB, nc)
    row128 = ((1, 128), f32, *_const((1, 128)), "all")
    d_dt_raw, d_dtbias, d_alog = _ew(
        "m2_act_bwd", _vjp_fn(_m2_act, 3), sv["act_ins"] + [(d_dtb, *_rows(tm, NI)), (d_dab, *_rows(tm, NI))],
        [((L, 128), f32, *_rows(tm, 128), None), row128, row128], (L // tm, 1))
    d_d = _ew("m2_dexp_bwd", _vjp_fn(_m2_dexp, 1), [(sv["d_pad"], *_const((1, 128))), (d_dsk, *_const((1, NI)))],
              [((1, 128), f32, *_const((1, 128)), None)], (1, 1))[0]
    d_conv_out = jnp.concatenate([dx, dB, dC], axis=1)
    d_conv_in, d_conv_w, d_conv_b = _conv_bwd("m2_conv_bwd", sv["proj"], NI // 128, conv_w, conv_b, d_conv_out)
    d_proj = jnp.concatenate([d_z, d_conv_in, d_dt_raw], axis=1)
    dw_ext = _mm_plain("m2_in_dw", sv["hn"], d_proj, _TN)
    dhn = _mm_plain("m2_in_dx", d_proj, w_ext, _NT)
    dh_in, dg = _rms_bwd("mix_norm_bwd", h, g_norm, dhn, dh)
    grads = dict(norm=dg, w_ext=dw_ext, conv_w=d_conv_w, conv_b=d_conv_b, dt_bias=d_dtbias[0, :M2_HEADS],
                 a_log=d_alog[0, :M2_HEADS], d=d_d[0, :M2_HEADS], norm_g=d_ng, w_out=dw_out)
    return dh_in, grads


def _mesh_pos():
    return lax.axis_index("x"), lax.axis_index("y"), lax.axis_index("c")


def _flip(pos, p):
    x, y, c = pos
    return (1 - x if p & 4 else x, 1 - y if p & 2 else y, 1 - c if p & 1 else c)


def _index(pos):
    return 4 * pos[0] + 2 * pos[1] + pos[2]


def _comm_call(name, body, arrays, out_shape, n_sem):
    n = len(arrays)
    hbm = pl.BlockSpec(memory_space=pl.ANY)
    return pl.pallas_call(
        body, name=name, in_specs=[hbm] * n, out_specs=[hbm] * len(out_shape), out_shape=out_shape,
        scratch_shapes=[pltpu.SemaphoreType.DMA((n, n_sem)), pltpu.SemaphoreType.DMA((n, n_sem)),
                        pltpu.SemaphoreType.DMA((n, 4))],
    )(*arrays)


def _gather(name, arrays):
    n = len(arrays)

    def body(*refs):
        ins, outs = refs[:n], refs[n:2 * n]
        send_sems, recv_sems, loc_sems = refs[2 * n:]
        me = _mesh_pos()
        c = me[2]
        sib = _flip(me, 1)
        chips = [_flip(me, 4), _flip(me, 2), _flip(me, 6)]

        def copy(w, k, block, to, src=None):
            slab = outs[w].at[_index(block)]
            return pltpu.make_async_remote_copy(
                src_ref=slab if src is None else src, dst_ref=slab, send_sem=send_sems.at[w, k],
                recv_sem=recv_sems.at[w, k], device_id=to, device_id_type=MESH)

        local = [pltpu.make_async_copy(ins[w], outs[w].at[_index(me)], loc_sems.at[w, 0]) for w in range(n)]
        for cp in local:
            cp.start()
        first = [copy(w, 0, me, sib, src=ins[w]) for w in range(n)]
        first += [copy(w, 1 + j, me, chip, src=ins[w]) for j, chip in enumerate(chips) for w in range(n)]
        for cp in first:
            cp.start()
        passed = []
        for j, chip in enumerate(chips):
            for w in range(n):
                copy(w, 1 + j, chip, me).wait_recv()
                fwd = copy(w, 4 + j, chip, sib)
                fwd.start()
                passed.append(fwd)
        for w in range(n):
            copy(w, 0, sib, me).wait_recv()
        for j, chip in enumerate(chips):
            for w in range(n):
                copy(w, 4 + j, (chip[0], chip[1], 1 - c), me).wait_recv()
        for cp in first + passed:
            cp.wait_send()
        for cp in local:
            cp.wait()

    out_shape = [jax.ShapeDtypeStruct((N_DEV,) + a.shape, a.dtype) for a in arrays]
    return _comm_call(name, body, arrays, out_shape, N_DEV - 1)


_HBM = pl.BlockSpec(memory_space=pltpu.HBM)
_SEM = pl.BlockSpec(memory_space=pltpu.SEMAPHORE)
_SPLIT_COPIES = 4


def _split_targets(me):
    return [_flip(me, 1), _flip(me, 4), _flip(me, 2), _flip(me, 6)]


def _gather_start(name, arrays, lands):
    n = len(arrays)
    ns = n * _SPLIT_COPIES

    def body(*refs):
        ins, land = refs[:n], refs[n:2 * n]
        send_sems, recv_sems = refs[2 * n:2 * n + ns], refs[2 * n + ns:2 * n + 2 * ns]
        token = refs[4 * n + 2 * ns]
        me = _mesh_pos()
        for w in range(n):
            for k, to in enumerate(_split_targets(me)):
                pltpu.make_async_remote_copy(
                    src_ref=ins[w], dst_ref=land[w].at[_index(me)], send_sem=send_sems[w * _SPLIT_COPIES + k],
                    recv_sem=recv_sems[w * _SPLIT_COPIES + k], device_id=to, device_id_type=MESH).start()
        token[...] = jnp.zeros_like(token)

    sem = pltpu.SemaphoreType.DMA(())
    res = pl.pallas_call(
        body, name=name,
        out_shape=(*[sem] * (2 * ns), *[pltpu.HBM(a.shape, a.dtype) for a in arrays],
                   *[pltpu.HBM(a.shape, a.dtype) for a in lands], jax.ShapeDtypeStruct((8, 128), f32)),
        in_specs=[_HBM] * (2 * n),
        out_specs=(*[_SEM] * (2 * ns), *[_HBM] * (2 * n), pl.BlockSpec(memory_space=pltpu.VMEM)),
        input_output_aliases={i: 2 * ns + i for i in range(2 * n)},
        compiler_params=pltpu.CompilerParams(has_side_effects=pltpu.SideEffectType.DATAFLOW_SIDE_EFFECTING),
    )(*[pltpu.with_memory_space_constraint(a, pltpu.HBM) for a in list(arrays) + list(lands)])
    sems, rest = res[:2 * ns], res[2 * ns:]
    return sems[:ns], sems[ns:], rest[:n], rest[n:2 * n], rest[2 * n]


def _gather_wait(name, arrays, lands, send_sems, recv_sems, after):
    n = len(arrays)
    ns = n * _SPLIT_COPIES

    def body(*refs):
        ins, land = refs[:n], refs[n:2 * n]
        s_sems, r_sems = refs[2 * n:2 * n + ns], refs[2 * n + ns:2 * n + 2 * ns]
        me = _mesh_pos()
        for w in range(n):
            for k, peer in enumerate(_split_targets(me)):
                cp = pltpu.make_async_remote_copy(
                    src_ref=ins[w], dst_ref=land[w].at[_index(peer)], send_sem=s_sems[w * _SPLIT_COPIES + k],
                    recv_sem=r_sems[w * _SPLIT_COPIES + k], device_id=peer, device_id_type=MESH)
                cp.wait_send()
                cp.wait_recv()

    res = pl.pallas_call(
        body, name=name,
        out_shape=(*[pltpu.HBM(a.shape, a.dtype) for a in arrays], *[pltpu.HBM(a.shape, a.dtype) for a in lands]),
        in_specs=[_HBM] * (2 * n) + [_SEM] * (2 * ns) + [pl.BlockSpec(memory_space=pl.ANY)],
        out_specs=tuple([_HBM] * (2 * n)), input_output_aliases={i: i for i in range(2 * n)},
        compiler_params=pltpu.CompilerParams(has_side_effects=pltpu.SideEffectType.DATAFLOW_SIDE_EFFECTING),
    )(*arrays, *lands, *send_sems, *recv_sems, after)
    return res[n:]


def _gather_forward(name, lands):
    n = len(lands)

    def body(*refs):
        outs = refs[n:2 * n]
        send_sems, recv_sems, _ = refs[2 * n:]
        me = _mesh_pos()
        sib = _flip(me, 1)
        held = [_flip(me, 4), _flip(me, 2), _flip(me, 6), sib]

        def copy(w, j, block):
            slab = outs[w].at[_index(block)]
            return pltpu.make_async_remote_copy(src_ref=slab, dst_ref=slab, send_sem=send_sems.at[w, j],
                                                recv_sem=recv_sems.at[w, j], device_id=sib, device_id_type=MESH)

        sends = [copy(w, j, blk) for j, blk in enumerate(held) for w in range(n)]
        for cp in sends:
            cp.start()
        for j, blk in enumerate(held):
            for w in range(n):
                copy(w, j, (blk[0], blk[1], 1 - blk[2])).wait_recv()
        for cp in sends:
            cp.wait_send()

    hbm = pl.BlockSpec(memory_space=pl.ANY)
    return pl.pallas_call(
        body, name=name, in_specs=[hbm] * n, out_specs=[hbm] * n,
        out_shape=[jax.ShapeDtypeStruct(a.shape, a.dtype) for a in lands],
        input_output_aliases={i: i for i in range(n)},
        scratch_shapes=[pltpu.SemaphoreType.DMA((n, 4)), pltpu.SemaphoreType.DMA((n, 4)), pltpu.SemaphoreType.DMA((n, 4))],
    )(*lands)


def _scatter_pair(name, arrays):
    n = len(arrays)

    def body(*refs):
        ins, outs = refs[:n], refs[n:2 * n]
        send_sems, recv_sems, _ = refs[2 * n:]
        me = _mesh_pos()
        c = me[2]
        sib = _flip(me, 1)

        def copy(w, q):
            return pltpu.make_async_remote_copy(
                src_ref=ins[w].at[2 * q + 1 - c], dst_ref=outs[w].at[q], send_sem=send_sems.at[w, q],
                recv_sem=recv_sems.at[w, q], device_id=sib, device_id_type=MESH)

        cps = [copy(w, q) for q in range(4) for w in range(n)]
        for cp in cps:
            cp.start()
        for cp in cps:
            cp.wait()

    out_shape = [jax.ShapeDtypeStruct((4,) + a.shape[1:], a.dtype) for a in arrays]
    return _comm_call(name, body, arrays, out_shape, 4)


def _pair_add(name, full, theirs, core, dtype):
    _, R, C = theirs.shape
    tr = R if R <= 256 else (256 if C <= 512 else 128)

    def body(core_ref, mine_ref, theirs_ref, o_ref):
        o_ref[...] = (mine_ref[...] + theirs_ref[...]).astype(o_ref.dtype)

    blk = pl.BlockSpec((4, tr, C), lambda r, cr: (0, r, 0))
    grid_spec = pltpu.PrefetchScalarGridSpec(
        num_scalar_prefetch=1, grid=(R // tr,),
        in_specs=[pl.BlockSpec((4, None, tr, C), lambda r, cr: (0, cr[0], r, 0)), blk], out_specs=blk)
    return pl.pallas_call(
        body, name=name, grid_spec=grid_spec, out_shape=jax.ShapeDtypeStruct((4, R, C), dtype),
        compiler_params=_params(1),
    )(core.reshape(1).astype(jnp.int32), full.reshape(4, 2, R, C), theirs)


def _scatter_chips(name, arrays):
    n = len(arrays)

    def body(*refs):
        ins, outs = refs[:n], refs[n:2 * n]
        send_sems, recv_sems, loc_sems = refs[2 * n:]
        me = _mesh_pos()
        mq = 2 * me[0] + me[1]
        peers = [_flip(me, 4), _flip(me, 2), _flip(me, 6)]

        def copy(w, k):
            peer = peers[k]
            return pltpu.make_async_remote_copy(
                src_ref=ins[w].at[2 * peer[0] + peer[1]], dst_ref=outs[w].at[mq], send_sem=send_sems.at[w, k],
                recv_sem=recv_sems.at[w, k], device_id=peer, device_id_type=MESH)

        def arrival(w, k):
            peer = peers[k]
            return pltpu.make_async_remote_copy(
                src_ref=ins[w].at[mq], dst_ref=outs[w].at[2 * peer[0] + peer[1]], send_sem=send_sems.at[w, k],
                recv_sem=recv_sems.at[w, k], device_id=peer, device_id_type=MESH)

        local = [pltpu.make_async_copy(ins[w].at[mq], outs[w].at[mq], loc_sems.at[w, 0]) for w in range(n)]
        for cp in local:
            cp.start()
        sends = [copy(w, k) for k in range(3) for w in range(n)]
        for cp in sends:
            cp.start()
        for k in range(3):
            for w in range(n):
                arrival(w, k).wait_recv()
        for cp in sends:
            cp.wait_send()
        for cp in local:
            cp.wait()

    out_shape = [jax.ShapeDtypeStruct(a.shape, a.dtype) for a in arrays]
    return _comm_call(name, body, arrays, out_shape, 3)


def _chip_peers(me):
    return [_flip(me, 4), _flip(me, 2), _flip(me, 6)]


def _chips_start(name, arrays):
    n = len(arrays)
    ns = 3 * n

    def body(*refs):
        ins, land = refs[:n], refs[n:2 * n]
        send_sems, recv_sems = refs[2 * n:2 * n + ns], refs[2 * n + ns:2 * n + 2 * ns]
        token = refs[4 * n + 2 * ns]
        me = _mesh_pos()
        for w in range(n):
            for k, peer in enumerate(_chip_peers(me)):
                pltpu.make_async_remote_copy(
                    src_ref=ins[w].at[2 * peer[0] + peer[1]], dst_ref=land[w].at[k], send_sem=send_sems[3 * w + k],
                    recv_sem=recv_sems[3 * w + k], device_id=peer, device_id_type=MESH).start()
        token[...] = jnp.zeros_like(token)

    lands = [lax.empty((3,) + a.shape[1:], a.dtype) for a in arrays]
    sem = pltpu.SemaphoreType.DMA(())
    res = pl.pallas_call(
        body, name=name,
        out_shape=(*[sem] * (2 * ns), *[pltpu.HBM(a.shape, a.dtype) for a in arrays],
                   *[pltpu.HBM(a.shape, a.dtype) for a in lands], jax.ShapeDtypeStruct((8, 128), f32)),
        in_specs=[_HBM] * (2 * n),
        out_specs=(*[_SEM] * (2 * ns), *[_HBM] * (2 * n), pl.BlockSpec(memory_space=pltpu.VMEM)),
        input_output_aliases={i: 2 * ns + i for i in range(2 * n)},
        compiler_params=pltpu.CompilerParams(has_side_effects=pltpu.SideEffectType.DATAFLOW_SIDE_EFFECTING),
    )(*[pltpu.with_memory_space_constraint(a, pltpu.HBM) for a in list(arrays) + lands])
    sems, rest = res[:2 * ns], res[2 * ns:]
    return sems[:ns], sems[ns:], rest[:n], rest[n:2 * n], rest[2 * n]


def _chips_wait(name, arrays, lands, send_sems, recv_sems, after):
    n = len(arrays)
    ns = 3 * n

    def body(*refs):
        ins, land = refs[:n], refs[n:2 * n]
        s_sems, r_sems = refs[2 * n:2 * n + ns], refs[2 * n + ns:2 * n + 2 * ns]
        me = _mesh_pos()
        for w in range(n):
            for k, peer in enumerate(_chip_peers(me)):
                cp = pltpu.make_async_remote_copy(
                    src_ref=ins[w].at[2 * peer[0] + peer[1]], dst_ref=land[w].at[k], send_sem=s_sems[3 * w + k],
                    recv_sem=r_sems[3 * w + k], device_id=peer, device_id_type=MESH)
                cp.wait_send()
                cp.wait_recv()

    res = pl.pallas_call(
        body, name=name,
        out_shape=(*[pltpu.HBM(a.shape, a.dtype) for a in arrays], *[pltpu.HBM(a.shape, a.dtype) for a in lands]),
        in_specs=[_HBM] * (2 * n) + [_SEM] * (2 * ns) + [pl.BlockSpec(memory_space=pl.ANY)],
        out_specs=tuple([_HBM] * (2 * n)), input_output_aliases={i: i for i in range(2 * n)},
        compiler_params=pltpu.CompilerParams(has_side_effects=pltpu.SideEffectType.DATAFLOW_SIDE_EFFECTING),
    )(*arrays, *lands, *send_sems, *recv_sems, after)
    return res[:n], res[n:]


def _adamw_own(name, arrivals, sums, chip, w, m, v):
    R, C = w.shape
    tr = R if R <= 256 else (256 if C <= 512 else 128)
    bc1 = 1.0 - ADAM_B1 ** ADAM_STEP
    bc2 = 1.0 - ADAM_B2 ** ADAM_STEP

    def body(chip_ref, own_ref, p_ref, w_ref, m_ref, v_ref, g_ref, d_ref, m2_ref, v2_ref):
        g = own_ref[...].astype(f32)
        for k in range(3):
            g = g + p_ref[k].astype(f32)
        m2 = ADAM_B1 * m_ref[...] + (1.0 - ADAM_B1) * g
        v2 = ADAM_B2 * v_ref[...] + (1.0 - ADAM_B2) * jnp.square(g)
        g_ref[...] = g
        d_ref[...] = -ADAM_LR * ((m2 / bc1) / (jnp.sqrt(v2 / bc2) + ADAM_EPS) + ADAM_WD * w_ref[...])
        m2_ref[...] = m2
        v2_ref[...] = v2

    blk = pl.BlockSpec((tr, C), lambda r, cr: (r, 0))
    grid_spec = pltpu.PrefetchScalarGridSpec(
        num_scalar_prefetch=1, grid=(R // tr,),
        in_specs=[pl.BlockSpec((None, tr, C), lambda r, cr: (cr[0], r, 0)),
                  pl.BlockSpec((3, tr, C), lambda r, cr: (0, r, 0)), blk, blk, blk],
        out_specs=[blk] * 4)
    return pl.pallas_call(
        body, name=name, grid_spec=grid_spec, out_shape=[jax.ShapeDtypeStruct((R, C), f32)] * 4,
        compiler_params=_params(1),
    )(chip.reshape(1).astype(jnp.int32), sums, arrivals, w, m, v)


def _adamw(name, parts, w, m, v):
    R, C = w.shape
    n_parts = parts.shape[0]
    tr = R if R <= 256 else (256 if C <= 512 else 128)
    bc1 = 1.0 - ADAM_B1 ** ADAM_STEP
    bc2 = 1.0 - ADAM_B2 ** ADAM_STEP

    def f(p, wv, mv, vv):
        g = p[0].astype(f32)
        for i in range(1, n_parts):
            g = g + p[i].astype(f32)
        m2 = ADAM_B1 * mv + (1.0 - ADAM_B1) * g
        v2 = ADAM_B2 * vv + (1.0 - ADAM_B2) * jnp.square(g)
        delta = -ADAM_LR * ((m2 / bc1) / (jnp.sqrt(v2 / bc2) + ADAM_EPS) + ADAM_WD * wv)
        return g, delta, m2, v2

    blk = ((tr, C), lambda r, z: (r, 0))
    return _ew(name, f, [(parts, (n_parts, tr, C), lambda r, z: (0, r, 0)), (w, *blk), (m, *blk), (v, *blk)],
               [((R, C), f32, *blk, None)] * 4, (R // tr, 1))


_WEIGHTS = ["norm_mix_g", "norm_mlp_g", "mlp_w1", "mlp_w2", "gdn_w_in", "gdn_conv_w", "gdn_a_log", "gdn_dt_bias",
            "gdn_o_norm_g", "gdn_w_out", "s5_w_in", "s5_lam_re", "s5_lam_im", "s5_log_dt", "s5_b_re", "s5_b_im",
            "s5_c_re", "s5_c_im", "s5_d", "s5_w_out", "m2_w_in", "m2_conv_w", "m2_conv_b", "m2_dt_bias", "m2_a_log",
            "m2_d", "m2_norm_g", "m2_w_out", "final_norm_g"]
_SHARDED = ["mlp_w1", "mlp_w2", "gdn_w_in", "gdn_w_out", "s5_w_in", "s5_w_out", "m2_w_in", "m2_w_out",
            "gdn_conv_w", "m2_conv_w", "m2_conv_b", "m2_norm_g"]
_MATRICES = _SHARDED[:8]
_REPLICATED = [n for n in _WEIGHTS if n not in _SHARDED]
_GDN_IN, _M2_IN = 4112, 6176
_LAYER_KIND = (0, 1, 2, 0)


def _as2d(a):
    return a.reshape(-1, a.shape[-1])


def _cols_from_shards(g, width):
    return g.transpose(1, 0, 2).reshape(g.shape[1], width)


def _cols_to_shards(a, width):
    return a[:, :width].reshape(a.shape[0], N_DEV, width // N_DEV).transpose(1, 0, 2)


def kernel(x, norm_mix_g, norm_mlp_g, mlp_w1, mlp_w2, gdn_w_in, gdn_conv_w, gdn_a_log, gdn_dt_bias, gdn_o_norm_g, gdn_w_out, s5_w_in, s5_lam_re, s5_lam_im, s5_log_dt, s5_b_re, s5_b_im, s5_c_re, s5_c_im, s5_d, s5_w_out, m2_w_in, m2_conv_w, m2_conv_b, m2_dt_bias, m2_a_log, m2_d, m2_norm_g, m2_w_out, final_norm_g, loss_target, m_norm_mix_g, m_norm_mlp_g, m_mlp_w1, m_mlp_w2, m_gdn_w_in, m_gdn_conv_w, m_gdn_a_log, m_gdn_dt_bias, m_gdn_o_norm_g, m_gdn_w_out, m_s5_w_in, m_s5_lam_re, m_s5_lam_im, m_s5_log_dt, m_s5_b_re, m_s5_b_im, m_s5_c_re, m_s5_c_im, m_s5_d, m_s5_w_out, m_m2_w_in, m_m2_conv_w, m_m2_conv_b, m_m2_dt_bias, m_m2_a_log, m_m2_d, m_m2_norm_g, m_m2_w_out, m_final_norm_g, v_norm_mix_g, v_norm_mlp_g, v_mlp_w1, v_mlp_w2, v_gdn_w_in, v_gdn_conv_w, v_gdn_a_log, v_gdn_dt_bias, v_gdn_o_norm_g, v_gdn_w_out, v_s5_w_in, v_s5_lam_re, v_s5_lam_im, v_s5_log_dt, v_s5_b_re, v_s5_b_im, v_s5_c_re, v_s5_c_im, v_s5_d, v_s5_w_out, v_m2_w_in, v_m2_conv_w, v_m2_conv_b, v_m2_dt_bias, v_m2_a_log, v_m2_d, v_m2_norm_g, v_m2_w_out, v_final_norm_g):
    args = locals()
    W = {n: args[n] for n in _WEIGHTS}
    MOM = {n: args["m_" + n] for n in _WEIGHTS}
    VAR = {n: args["v_" + n] for n in _WEIGHTS}
    h = x[0]
    target = loss_target[0]
    L, D = h.shape

    first = [mlp_w1[0:1].astype(bf16), mlp_w2[0:1].astype(bf16), gdn_w_in[0:1].astype(bf16),
             gdn_w_out[0:1].astype(bf16), _as2d(gdn_conv_w), _as2d(m2_conv_w), _as2d(m2_conv_b), _as2d(m2_norm_g)]
    w1g0, w2g0, gin0, gout0, gconv, m2_cw, m2_cbg, m2_ngg = _gather("gather_first", first)
    stacked = jnp.concatenate([gdn_w_out[1], s5_w_in[0], m2_w_out[0]], axis=0).astype(bf16)
    rest = [mlp_w1[1:4].astype(bf16), mlp_w2[1:4].astype(bf16), gdn_w_in[1:2].astype(bf16), s5_w_out.astype(bf16),
            m2_w_in.astype(bf16), stacked]
    lands = [lax.empty((N_DEV,) + a.shape, a.dtype) for a in rest]
    send_sems, recv_sems, rest_thru, lands_thru, token = _gather_start("gather_rest_start", rest, lands)

    def gdn_weights(gin, gout, conv, j):
        return (jnp.pad(_cols_from_shards(gin[:, 0], _GDN_IN), ((0, 0), (0, GDN_EXT - _GDN_IN))),
                gout[:, 0].reshape(D, D), _cols_from_shards(conv[:, 4 * j:4 * j + 4], 3 * D))

    gdn_in, gdn_out, gdn_conv = [None, None], [None, None], [None, None]
    gdn_in[0], gdn_out[0], gdn_conv[0] = gdn_weights(gin0, gout0, gconv, 0)

    norm_mix = [norm_mix_g[i].reshape(1, D) for i in range(4)]
    norm_mix[0] = norm_mix[0] + token[0, 0]
    late = {}

    def mixer_fwd(i, hv):
        kind, j = _LAYER_KIND[i], i // 3
        gn = norm_mix[i]
        s5_in, s5_out_g = late.get("s5_in"), late.get("s5_out_g")
        m2_in, m2_conv, m2_cb, m2_ng, m2_out = (late.get(k) for k in ("m2_in", "m2_conv", "m2_cb", "m2_ng", "m2_out"))
        if kind == 0:
            return _gdn_fwd(hv, gn, gdn_in[j], gdn_conv[j], gdn_a_log[j], gdn_dt_bias[j], gdn_o_norm_g[j], gdn_out[j])
        if kind == 1:
            return _s5_fwd(hv, gn, s5_in, s5_lam_re[0], s5_lam_im[0], s5_log_dt[0], s5_b_re[0], s5_b_im[0],
                           s5_c_re[0], s5_c_im[0], s5_d[0], s5_out_g)
        return _m2_fwd(hv, gn, m2_in, m2_conv, m2_cb, m2_dt_bias[0], m2_a_log[0], m2_d[0], m2_ng, m2_out)

    def mixer_bwd(i, dh, hv, sv):
        kind, j = _LAYER_KIND[i], i // 3
        gn = norm_mix[i]
        s5_in, s5_out_g = late["s5_in"], late["s5_out_g"]
        m2_in, m2_conv, m2_cb, m2_out = (late[k] for k in ("m2_in", "m2_conv", "m2_cb", "m2_out"))
        if kind == 0:
            return _gdn_bwd(dh, hv, gn, gdn_in[j], gdn_conv[j], gdn_out[j], sv)
        if kind == 1:
            return _s5_bwd(dh, hv, gn, s5_in, s5_out_g, sv)
        return _m2_bwd(dh, hv, gn, m2_in, m2_conv, m2_cb, m2_out, sv)

    tape = []
    mlp_w = [(w1g0, w2g0, 0)]
    for i in range(4):
        if i == 1:
            landed = _gather_wait("gather_rest_wait", rest_thru, lands_thru, send_sems, recv_sems, h)
            w1gr, w2gr, gin1, s5_out_g, m2_in_g, rows_g = _gather_forward("gather_rest_forward", landed)
            gout1, s5_in_g, m2_out_g = rows_g[:, None, 0:128], rows_g[:, 128:256], rows_g[:, 256:512]
            mlp_w += [(w1gr, w2gr, k) for k in range(3)]
            gdn_in[1], gdn_out[1], gdn_conv[1] = gdn_weights(gin1, gout1, gconv, 1)
            late.update(
                s5_in=s5_in_g.reshape(D, D), s5_out_g=s5_out_g,
                m2_in=jnp.pad(_cols_from_shards(m2_in_g[:, 0], _M2_IN), ((0, 0), (0, M2_EXT - _M2_IN))),
                m2_out=m2_out_g.reshape(M2_INNER, D), m2_conv=_cols_from_shards(m2_cw, 2 * M2_INNER),
                m2_cb=_cols_from_shards(m2_cbg, 2 * M2_INNER), m2_ng=_cols_from_shards(m2_ngg, M2_INNER))
        h_mid, sv = mixer_fwd(i, h)
        h_next, hn, h1 = _mlp_fwd(h_mid, norm_mlp_g[i].reshape(1, D), *mlp_w[i])
        tape.append((h, sv, h_mid, hn, h1))
        h = h_next
    loss_row, dh, d_final = _loss_head(h, final_norm_g.reshape(1, D), target)
    loss = lax.psum(loss_row[0, 0], ("x", "y", "c"))

    fs = D_FF // N_DEV
    group_layers = (1, 1, 2)
    dw1 = [lax.empty((N_DEV, nl, D, fs), f32) for nl in group_layers]
    dw2 = [lax.empty((N_DEV, nl, fs, D), f32) for nl in group_layers]
    d_mix, d_mlp, mg = [None] * 4, [None] * 4, [None] * 4
    core = lax.axis_index("c")
    chip = 2 * lax.axis_index("x") + lax.axis_index("y")

    def pair_sums(tag, entries):
        theirs = _scatter_pair("scatter_pair_" + tag, [e[1] for e in entries])
        return [_pair_add("pair_add_%s_%s" % (tag, e[0]), e[1], th, core, bf16 if e[2] else f32)
                for e, th in zip(entries, theirs)]

    def mlp_grads(grp):
        nl = group_layers[grp]
        return [("mlp_w1", dw1[grp].reshape(N_DEV, nl * D, fs), True), ("mlp_w2", dw2[grp].reshape(N_DEV, nl * fs, D), True)]

    def gdn_grads(g):
        return [("gdn_w_in", _cols_to_shards(g["w_ext"], _GDN_IN), True),
                ("gdn_w_out", g["w_out"].reshape(N_DEV, D // N_DEV, D), True)]

    flying = {}
    for i in reversed(range(4)):
        h_in, sv, h_mid, hn, h1 = tape[i]
        grp, slot = min(i, 2), max(i - 2, 0)
        dh, d_mlp[i], dw1[grp], dw2[grp] = _mlp_bwd(dh, h_mid, norm_mlp_g[i].reshape(1, D), hn, h1, *mlp_w[i], slot,
                                                    dw1[grp], dw2[grp])
        dh, mg[i] = mixer_bwd(i, dh, h_in, sv)
        d_mix[i] = mg[i]["norm"]
        if i in (2, 1):
            if i == 2:
                entries = mlp_grads(2) + gdn_grads(mg[3]) + [
                    ("m2_w_in", _cols_to_shards(mg[2]["w_ext"], _M2_IN), True),
                    ("m2_w_out", mg[2]["w_out"].reshape(N_DEV, M2_INNER // N_DEV, D), True)]
            else:
                entries = mlp_grads(1) + [("s5_w_in", mg[1]["w_in"].reshape(N_DEV, D // N_DEV, D), True),
                                          ("s5_w_out", mg[1]["w_out"].reshape(N_DEV, D, 2 * D // N_DEV), True)]
            started = _chips_start("chips_start_%d" % grp, pair_sums("g%d" % grp, entries))
            flying[grp] = (entries, started)
            dh = dh + started[4][0, 0]
    grad_x = dh.reshape(1, L, D)
    ga, gb_, s5g, m2g = mg[0], mg[3], mg[1], mg[2]
    early = mlp_grads(0) + gdn_grads(ga) + [
        ("gdn_conv_w", jnp.concatenate([_cols_to_shards(g["conv_w"], 3 * D) for g in (ga, gb_)], axis=1), False),
        ("m2_conv_w", _cols_to_shards(m2g["conv_w"], 2 * M2_INNER), False),
        ("m2_conv_b", _cols_to_shards(m2g["conv_b"], 2 * M2_INNER), False),
        ("m2_norm_g", _cols_to_shards(m2g["norm_g"], M2_INNER), False)]
    early_parts = _scatter_chips("scatter_chips", pair_sums("g0", early))
    landed = {}
    for grp in (1, 2):
        entries, (s_sems, r_sems, sums, lands, _) = flying[grp]
        landed[grp] = (entries,) + tuple(_chips_wait("chips_wait_%d" % grp, sums, lands, s_sems, r_sems, early_parts[0]))

    rep = {
        "norm_mix_g": jnp.concatenate(d_mix, axis=0), "norm_mlp_g": jnp.concatenate(d_mlp, axis=0),
        "gdn_a_log": jnp.stack([ga["a_log"], gb_["a_log"]]), "gdn_dt_bias": jnp.stack([ga["dt_bias"], gb_["dt_bias"]]),
        "gdn_o_norm_g": jnp.stack([ga["o_norm_g"], gb_["o_norm_g"]]),
        "s5_lam_re": s5g["lam_re"], "s5_lam_im": s5g["lam_im"], "s5_log_dt": s5g["log_dt"], "s5_b_re": s5g["b_re"],
        "s5_b_im": s5g["b_im"], "s5_c_re": s5g["c_re"], "s5_c_im": s5g["c_im"], "s5_d": s5g["d"],
        "m2_dt_bias": m2g["dt_bias"], "m2_a_log": m2g["a_log"], "m2_d": m2g["d"], "final_norm_g": d_final,
    }

    def pack(d):
        flat = jnp.concatenate([d[n].reshape(-1).astype(f32) for n in _REPLICATED])
        return jnp.pad(flat, (0, -flat.shape[0] % (256 * 128))).reshape(-1, 128)

    (rep_parts,) = _gather("gather_small_grads", [pack(rep)])

    res = {}
    owned = {"mlp_w1": {0: (0, 1), 1: (1, 2), 2: (2, 4)}, "mlp_w2": {0: (0, 1), 1: (1, 2), 2: (2, 4)},
             "gdn_w_in": {0: (0, 1), 2: (1, 2)}, "gdn_w_out": {0: (0, 1), 2: (1, 2)}}

    def shard_rows(a, label, grp):
        lo_hi = owned.get(label)
        return _as2d(a if lo_hi is None else a[lo_hi[grp][0]:lo_hi[grp][1]])

    done = {}
    for (label, _, _), p in zip(early, early_parts):
        done[label] = [_adamw("adamw_" + label, p, *[shard_rows(t[label], label, 0) for t in (W, MOM, VAR)])]
    for grp in (1, 2):
        entries, sums, arrivals = landed[grp]
        for (label, _, _), arr, sm in zip(entries, arrivals, sums):
            done.setdefault(label, []).append(_adamw_own(
                "adamw_g%d_%s" % (grp, label), arr, sm, chip, *[shard_rows(t[label], label, grp) for t in (W, MOM, VAR)]))
    for n in _SHARDED:
        res[n] = [jnp.concatenate([g[k] for g in done[n]], axis=0).reshape(W[n].shape) for k in range(4)]
    out = _adamw("adamw_replicated", rep_parts, pack(W), pack(MOM), pack(VAR))
    off = 0
    for n in _REPLICATED:
        size = W[n].size
        res[n] = [o.reshape(-1)[off:off + size].reshape(W[n].shape) for o in out]
        off += size

    return (loss, grad_x, *[res[n][0] for n in _WEIGHTS], *[res[n][1] for n in _WEIGHTS],
            *[res[n][2] for n in _WEIGHTS], *[res[n][3] for n in _WEIGHTS])
```

```python
import jax
import jax.numpy as jnp
from jax import lax
from jax.experimental import pallas as pl
from jax.experimental.pallas import tpu as pltpu

f32 = jnp.float32
bf16 = jnp.bfloat16
HI = lax.Precision.HIGHEST
MESH = pl.DeviceIdType.MESH

N_DEV = 8
D_MODEL = 1024
D_FF = 4096
CHUNK = 64
RMS_EPS = 1e-6
GDN_HEADS = 8
GDN_HB = 8
GDN_EXT = 4224
S5_STATE = 64
S5_SCAN_LANES = 256
M2_INNER = 2048
M2_EXT = 6272
M2_HEADS = 32
M2_GB = 4
VMEM_LIMIT_BYTES = 56 * 1024 * 1024

ADAM_LR, ADAM_B1, ADAM_B2, ADAM_EPS, ADAM_WD, ADAM_STEP = 0.001, 0.9, 0.999, 1e-08, 0.01, 10

_NN = ((1,), (0,))
_NT = ((1,), (1,))
_TN = ((0,), (0,))


def _dot(a, b, dims=_NN):
    return lax.dot_general(a, b, (dims, ((), ())), precision=HI, preferred_element_type=f32)


def _dotb(a, b, dims=_NN):
    return lax.dot_general(a.astype(bf16), b.astype(bf16), (dims, ((), ())), preferred_element_type=f32)


def _bdot(p, q, dims):
    return lax.dot_general(p, q, (dims, ((), ())), preferred_element_type=f32)


def _pieces(x, n):
    out = []
    for _ in range(n - 1):
        p = x.astype(bf16)
        out.append(p)
        x = x - p.astype(f32)
    return out + [x.astype(bf16)]


def _dot01_raw(mask, b, dims=_NN, mask_first=True):
    m = mask.astype(bf16)
    p = _pieces(b, 3)
    if mask_first:
        return _bdot(m, p[0], dims) + (_bdot(m, p[1], dims) + _bdot(m, p[2], dims))
    return _bdot(p[0], m, dims) + (_bdot(p[1], m, dims) + _bdot(p[2], m, dims))


@jax.custom_vjp
def _dot01_nn(mask, b):
    return _dot01_raw(mask, b, _NN)


@jax.custom_vjp
def _dot01_nt(mask, b):
    return _dot01_raw(mask, b, _NT)


_dot01_nn.defvjp(lambda m, b: (_dot01_raw(m, b, _NN), m),
                 lambda m, ct: (jnp.zeros_like(m), _dot01_raw(m, ct, _TN, mask_first=True)))
_dot01_nt.defvjp(lambda m, b: (_dot01_raw(m, b, _NT), m),
                 lambda m, ct: (jnp.zeros_like(m), _dot01_raw(m, ct, _TN, mask_first=False)))


def _dot01_vjp(mask, b, dims=_NN):
    return _dot01_nn(mask, b) if dims == _NN else _dot01_nt(mask, b)


def _dot3_raw(a, b, dims=_NN):
    (ah, al), (bh, bl) = _pieces(a, 2), _pieces(b, 2)
    return _bdot(ah, bh, dims) + (_bdot(ah, bl, dims) + _bdot(al, bh, dims))


@jax.custom_vjp
def _dot3_vjp(a, b):
    return _dot3_raw(a, b)


_dot3_vjp.defvjp(lambda a, b: (_dot3_raw(a, b), (a, b)),
                 lambda res, ct: (_dot3_raw(ct, res[1], _NT), _dot3_raw(res[0], ct, _TN)))


class _Dots:
    def __init__(self, dot3, dot01):
        self.dot3, self.dot01 = dot3, dot01


_PLAIN_DOTS = _Dots(_dot3_raw, _dot01_raw)
_VJP_DOTS = _Dots(_dot3_vjp, _dot01_vjp)


def _iota(shape, dim):
    return lax.broadcasted_iota(jnp.int32, shape, dim)


def _params(n_grid):
    return pltpu.CompilerParams(dimension_semantics=("arbitrary",) * n_grid, vmem_limit_bytes=VMEM_LIMIT_BYTES)


def _row_tile(n_rows):
    return min(512, n_rows)


def _head_rows(n_rows):
    return min(2048, n_rows)


def _mm_rows(n_rows):
    return min(1024, n_rows)


def _col_tile(n, cap=1024):
    best = 128
    for t in range(128, cap + 1, 128):
        if n % t == 0:
            best = t
    return best


def _mm(name, a, b, *, dims, grid, a_spec, b_spec, out_shape, out_spec, aux=(), a_fn=None, epi_fn=None,
        acc_shape, out_init=None, cache_a=False):
    nk = grid[2]
    n_aux = len(aux)
    kinds = [x[2] for x in aux]
    cache_a = cache_a and nk == 1 and grid[1] > 1 and not any(kd == "a" for kd in kinds)

    def body_single(*refs):
        a_ref, b_ref = refs[0], refs[1]
        aux_refs = refs[2:2 + n_aux]
        pos = 2 + n_aux + (1 if out_init is not None else 0)
        o_ref = refs[pos]

        def a_tile():
            av = a_ref[...]
            if a_fn is not None:
                av = a_fn(av, *[r[...] for r, kd in zip(aux_refs, kinds) if kd == "a"])
            return av.astype(bf16)

        if cache_a:
            a_bf = refs[pos + 1]

            @pl.when(pl.program_id(1) == 0)
            def _():
                a_bf[...] = a_tile()

            av = a_bf[...]
        else:
            av = a_tile()
        r = lax.dot_general(av, b_ref[...].astype(bf16), (dims, ((), ())), preferred_element_type=f32)
        if epi_fn is not None:
            r = epi_fn(r, *[x[...] for x, kd in zip(aux_refs, kinds) if kd == "e"])
        o_ref[...] = r.astype(o_ref.dtype)

    def body(*refs):
        a_ref, b_ref = refs[0], refs[1]
        aux_refs = refs[2:2 + n_aux]
        pos = 2 + n_aux + (1 if out_init is not None else 0)
        o_ref, acc_ref = refs[pos], refs[pos + 1]
        k = pl.program_id(2)

        @pl.when(k == 0)
        def _():
            acc_ref[...] = jnp.zeros_like(acc_ref)

        av = a_ref[...]
        if a_fn is not None:
            av = a_fn(av, *[r[...] for r, kd in zip(aux_refs, kinds) if kd == "a"])
        acc_ref[...] += lax.dot_general(av.astype(bf16), b_ref[...].astype(bf16), (dims, ((), ())),
                                        preferred_element_type=f32)

        @pl.when(k == nk - 1)
        def _():
            r = acc_ref[...]
            if epi_fn is not None:
                r = epi_fn(r, *[x[...] for x, kd in zip(aux_refs, kinds) if kd == "e"])
            o_ref[...] = r.astype(o_ref.dtype)

    in_specs = [a_spec, b_spec] + [x[1] for x in aux]
    args = [a, b] + [x[0] for x in aux]
    aliases = {}
    if out_init is not None:
        in_specs.append(pl.BlockSpec(memory_space=pl.ANY))
        args.append(out_init)
        aliases = {len(args) - 1: 0}
    if nk == 1:
        a_block = tuple(d for d in a_spec.block_shape if d is not None)
        scratch = [pltpu.VMEM(a_block, bf16)] if cache_a else []
    else:
        scratch = [pltpu.VMEM(acc_shape, f32)]
    return pl.pallas_call(
        body_single if nk == 1 else body, name=name, grid=grid, in_specs=in_specs, out_specs=out_spec,
        out_shape=out_shape, scratch_shapes=scratch, input_output_aliases=aliases, compiler_params=_params(3),
    )(*args)


def _ew(name, f, ins, outs, grid):
    n_in = len(ins)
    modes = [o[4] for o in outs]

    def body(*refs):
        vals = [r[...] for r in refs[:n_in]]
        res = f(*vals)
        if not isinstance(res, (tuple, list)):
            res = (res,)
        for r, o_ref, mode in zip(res, refs[n_in:], modes):
            if mode is None:
                o_ref[...] = r.astype(o_ref.dtype)
                continue
            first = pl.program_id(1) == 0
            if mode == "all":
                first = jnp.logical_and(first, pl.program_id(0) == 0)

            @pl.when(first)
            def _(r=r, o_ref=o_ref):
                o_ref[...] = r.astype(o_ref.dtype)

            @pl.when(jnp.logical_not(first))
            def _(r=r, o_ref=o_ref):
                o_ref[...] += r.astype(o_ref.dtype)

    res = pl.pallas_call(
        body, name=name, grid=grid,
        in_specs=[pl.BlockSpec(blk, im) for _, blk, im in ins],
        out_specs=[pl.BlockSpec(o[2], o[3]) for o in outs],
        out_shape=[jax.ShapeDtypeStruct(o[0], o[1]) for o in outs],
        compiler_params=_params(2),
    )(*[a for a, _, _ in ins])
    return res


def _vjp_fn(f, n_primal):
    def g(*args):
        _, vjp = jax.vjp(f, *args[:n_primal])
        cts = args[n_primal:]
        return vjp(cts[0] if len(cts) == 1 else tuple(cts))
    return g


def _scan_fwd(name, step, n_state, state_shape, cins, ins, outs, n_units, n_chunks):
    n_c, n_in, n_out = len(cins), len(ins), len(outs)

    def body(*refs):
        c_refs = refs[:n_c]
        in_refs = refs[n_c:n_c + n_in]
        out_refs = refs[n_c + n_in:n_c + n_in + n_out]
        saved = refs[n_c + n_in + n_out:n_c + n_in + n_out + n_state]
        st = refs[n_c + n_in + n_out + n_state:]

        @pl.when(pl.program_id(1) == 0)
        def _():
            for s in st:
                s[...] = jnp.zeros_like(s)

        cur = [s[...] for s in st]
        for sv, s in zip(saved, cur):
            sv[...] = s
        new, res = step(cur, [r[...] for r in c_refs], [r[...] for r in in_refs], _PLAIN_DOTS)
        for s, n in zip(st, new):
            s[...] = n
        for o, r in zip(out_refs, res):
            o[...] = r

    sshape = (n_units, n_chunks) + state_shape
    sblock = (None, None) + state_shape
    nz = len(state_shape)
    res = pl.pallas_call(
        body, name=name, grid=(n_units, n_chunks),
        in_specs=[pl.BlockSpec(e[1], e[2]) for e in cins + ins],
        out_specs=[pl.BlockSpec(o[1], o[2]) for o in outs]
        + [pl.BlockSpec(sblock, lambda u, c: (u, c) + (0,) * nz)] * n_state,
        out_shape=[jax.ShapeDtypeStruct(o[0], f32) for o in outs]
        + [jax.ShapeDtypeStruct(sshape, f32)] * n_state,
        scratch_shapes=[pltpu.VMEM(state_shape, f32)] * n_state,
        compiler_params=_params(2),
    )(*[e[0] for e in cins + ins])
    return res[:n_out], res[n_out:]


def _scan_bwd(name, step, n_state, state_shape, cins, ins, saved, douts, n_units, n_chunks):
    n_c, n_in, n_do = len(cins), len(ins), len(douts)

    def flip(im):
        return lambda u, c: im(u, n_chunks - 1 - c)

    def body(*refs):
        p = 0
        c_refs = refs[p:p + n_c]; p += n_c
        in_refs = refs[p:p + n_in]; p += n_in
        sv_refs = refs[p:p + n_state]; p += n_state
        do_refs = refs[p:p + n_do]; p += n_do
        dc_refs = refs[p:p + n_c]; p += n_c
        di_refs = refs[p:p + n_in]; p += n_in
        dst = refs[p:]
        first = pl.program_id(1) == 0

        @pl.when(first)
        def _():
            for s in dst:
                s[...] = jnp.zeros_like(s)

        def fn(states, consts, vals):
            new, res = step(states, consts, vals, _VJP_DOTS)
            return tuple(new), tuple(res)

        prim = ([r[...] for r in sv_refs], [r[...] for r in c_refs], [r[...] for r in in_refs])
        _, vjp = jax.vjp(fn, *prim)
        d_states, d_consts, d_vals = vjp((tuple(s[...] for s in dst), tuple(r[...] for r in do_refs)))
        for s, g in zip(dst, d_states):
            s[...] = g
        for o, g in zip(di_refs, d_vals):
            o[...] = g
        for o, g in zip(dc_refs, d_consts):
            @pl.when(first)
            def _(o=o, g=g):
                o[...] = g

            @pl.when(jnp.logical_not(first))
            def _(o=o, g=g):
                o[...] += g

    nz = len(state_shape)
    sblock = (None, None) + state_shape
    def gshape(e):
        return e[3] if len(e) == 5 else e[0].shape

    def gmap(e):
        return e[4] if len(e) == 5 else e[2]

    in_specs = ([pl.BlockSpec(e[1], e[2]) for e in cins]
                + [pl.BlockSpec(e[1], flip(e[2])) for e in ins]
                + [pl.BlockSpec(sblock, lambda u, c: (u, n_chunks - 1 - c) + (0,) * nz)] * n_state
                + [pl.BlockSpec(e[1], flip(e[2])) for e in douts])
    out_specs = ([pl.BlockSpec(e[1], e[2]) for e in cins]
                 + [pl.BlockSpec(e[1], flip(gmap(e))) for e in ins])
    out_shape = [jax.ShapeDtypeStruct(gshape(e), f32) for e in cins + ins]
    res = pl.pallas_call(
        body, name=name, grid=(n_units, n_chunks), in_specs=in_specs, out_specs=out_specs, out_shape=out_shape,
        scratch_shapes=[pltpu.VMEM(state_shape, f32)] * n_state,
        compiler_params=_params(2),
    )(*([e[0] for e in cins + ins] + list(saved) + [e[0] for e in douts]))
    return res[:n_c], res[n_c:]


def _rms(x, g):
    return x * lax.rsqrt(jnp.mean(x * x, axis=-1, keepdims=True) + RMS_EPS) * g


def _rows(tm, width):
    return (tm, width), lambda r, z: (r, 0)


def _const(shape):
    return shape, lambda r, z: (0,) * len(shape)


def _rms_fwd(name, h, g):
    L, D = h.shape
    tm = _row_tile(L)
    return _ew(name, _rms, [(h, *_rows(tm, D)), (g, *_const((1, D)))],
               [((L, D), f32, *_rows(tm, D), None)], (L // tm, 1))[0]


def _rms_bwd(name, h, g, d_hn, d_res):
    L, D = h.shape
    tm = _row_tile(L)

    def f(hv, gv, dv, rv):
        dh, dg = _vjp_fn(_rms, 2)(hv, gv, dv)
        return dh + rv, dg

    return _ew(name, f, [(h, *_rows(tm, D)), (g, *_const((1, D))), (d_hn, *_rows(tm, D)), (d_res, *_rows(tm, D))],
               [((L, D), f32, *_rows(tm, D), None), ((1, D), f32, *_const((1, D)), "all")], (L // tm, 1))


def _loss_head(h, g, target):
    L, D = h.shape
    tm = _row_tile(L)

    def f(hv, gv, tv):
        def lf(a, b):
            e = jnp.square(_rms(a, b) - tv)
            return (0.5 / D) * jnp.sum(jnp.sum(e, axis=1, keepdims=True), axis=0, keepdims=True)

        val, vjp = jax.vjp(lf, hv, gv)
        dh, dg = vjp(jnp.ones((1, 1), f32))
        return jnp.broadcast_to(val, (1, 128)), dh, dg

    return _ew("loss_head", f, [(h, *_rows(tm, D)), (g, *_const((1, D))), (target, *_rows(tm, D))],
               [((1, 128), f32, *_const((1, 128)), "all"), ((L, D), f32, *_rows(tm, D), None),
                ((1, D), f32, *_const((1, D)), "all")], (L // tm, 1))


def _sqrelu(x):
    return jnp.square(jnp.maximum(x, 0.0))


def _mm_plain(name, a, b, dims, *, a_fn=None, epi_fn=None, aux=()):
    if dims == _NN:
        (M, K), N = a.shape, b.shape[1]
    elif dims == _NT:
        (M, K), N = a.shape, b.shape[0]
    else:
        (K, M), N = a.shape, b.shape[1]
    tm = _mm_rows(M)
    tn = _col_tile(N, 1536)
    tk = _col_tile(K)
    if dims == _TN:
        tk = min(512, K)
        a_spec = pl.BlockSpec((tk, tm), lambda i, j, k: (k, i))
        b_spec = pl.BlockSpec((tk, tn), lambda i, j, k: (k, j))
        a_aux = pl.BlockSpec((tk, tm), lambda i, j, k: (k, i))
    elif dims == _NT:
        a_spec = pl.BlockSpec((tm, tk), lambda i, j, k: (i, k))
        b_spec = pl.BlockSpec((tn, tk), lambda i, j, k: (j, k))
        a_aux = pl.BlockSpec((tm, tk), lambda i, j, k: (i, k))
    else:
        a_spec = pl.BlockSpec((tm, tk), lambda i, j, k: (i, k))
        b_spec = pl.BlockSpec((tk, tn), lambda i, j, k: (k, j))
        a_aux = pl.BlockSpec((tm, tk), lambda i, j, k: (i, k))
    e_aux = pl.BlockSpec((tm, tn), lambda i, j, k: (i, j))
    aux_full = [(x, a_aux if kd == "a" else e_aux, kd) for x, kd in aux]
    return _mm(name, a, b, dims=dims, grid=(M // tm, N // tn, K // tk), a_spec=a_spec, b_spec=b_spec,
               out_shape=jax.ShapeDtypeStruct((M, N), f32), out_spec=pl.BlockSpec((tm, tn), lambda i, j, k: (i, j)),
               aux=aux_full, a_fn=a_fn, epi_fn=epi_fn, acc_shape=(tm, tn), cache_a=True)


def _mlp_fwd(h, g, w1g, w2g, layer):
    L, D = h.shape
    tm = _mm_rows(L)
    fs = D_FF // N_DEV
    hn = _rms_fwd("mlp_norm", h, g)
    h1 = _mm("mlp_up", hn, w1g, dims=_NN, grid=(L // tm, N_DEV, 1),
             a_spec=pl.BlockSpec((tm, D), lambda i, j, k: (i, 0)),
             b_spec=pl.BlockSpec((None, None, D, fs), lambda i, j, k: (j, layer, 0, 0)),
             out_shape=jax.ShapeDtypeStruct((L, D_FF), f32), out_spec=pl.BlockSpec((tm, fs), lambda i, j, k: (i, j)),
             acc_shape=(tm, fs), cache_a=True)
    tn = D
    h_out = _mm("mlp_down", h1, w2g, dims=_NN, grid=(L // tm, D // tn, N_DEV),
                a_spec=pl.BlockSpec((tm, fs), lambda i, j, k: (i, k)),
                b_spec=pl.BlockSpec((None, None, fs, tn), lambda i, j, k: (k, layer, 0, j)),
                out_shape=jax.ShapeDtypeStruct((L, D), f32), out_spec=pl.BlockSpec((tm, tn), lambda i, j, k: (i, j)),
                aux=[(h, pl.BlockSpec((tm, tn), lambda i, j, k: (i, j)), "e")],
                a_fn=_sqrelu, epi_fn=lambda acc, res: acc + res, acc_shape=(tm, tn))
    return h_out, hn, h1


def _mlp_bwd(dh, h, g, hn, h1, w1g, w2g, wl, layer, dw1_buf, dw2_buf):
    L, D = h.shape
    tm = _mm_rows(L)
    fs = D_FF // N_DEV
    tk = _mm_rows(L)
    dh1 = _mm("mlp_down_dx", dh, w2g, dims=_NT, grid=(L // tm, N_DEV, 1),
              a_spec=pl.BlockSpec((tm, D), lambda i, j, k: (i, 0)),
              b_spec=pl.BlockSpec((None, None, fs, D), lambda i, j, k: (j, wl, 0, 0)),
              out_shape=jax.ShapeDtypeStruct((L, D_FF), f32), out_spec=pl.BlockSpec((tm, fs), lambda i, j, k: (i, j)),
              aux=[(h1, pl.BlockSpec((tm, fs), lambda i, j, k: (i, j)), "e")],
              epi_fn=lambda acc, pre: acc * (2.0 * jnp.maximum(pre, 0.0)), acc_shape=(tm, fs), cache_a=True)
    dw2_buf = _mm("mlp_down_dw", h1, dh, dims=_TN, grid=(N_DEV, 1, L // tk),
                  a_spec=pl.BlockSpec((tk, fs), lambda i, j, k: (k, i)),
                  b_spec=pl.BlockSpec((tk, D), lambda i, j, k: (k, 0)),
                  out_shape=jax.ShapeDtypeStruct(dw2_buf.shape, f32),
                  out_spec=pl.BlockSpec((None, None, fs, D), lambda i, j, k: (i, layer, 0, 0)),
                  a_fn=_sqrelu, acc_shape=(fs, D), out_init=dw2_buf)
    tr = D
    dw1_buf = _mm("mlp_up_dw", hn, dh1, dims=_TN, grid=(D // tr, N_DEV, L // tk),
                  a_spec=pl.BlockSpec((tk, tr), lambda i, j, k: (k, i)),
                  b_spec=pl.BlockSpec((tk, fs), lambda i, j, k: (k, j)),
                  out_shape=jax.ShapeDtypeStruct(dw1_buf.shape, f32),
                  out_spec=pl.BlockSpec((None, None, tr, fs), lambda i, j, k: (j, layer, i, 0)),
                  acc_shape=(tr, fs), out_init=dw1_buf)
    tn = D
    dhn = _mm("mlp_up_dx", dh1, w1g, dims=_NT, grid=(L // tm, D // tn, N_DEV),
              a_spec=pl.BlockSpec((tm, fs), lambda i, j, k: (i, k)),
              b_spec=pl.BlockSpec((None, None, tn, fs), lambda i, j, k: (k, wl, j, 0)),
              out_shape=jax.ShapeDtypeStruct((L, D), f32), out_spec=pl.BlockSpec((tm, tn), lambda i, j, k: (i, j)),
              acc_shape=(tm, tn))
    dh_in, dg = _rms_bwd("mlp_norm_bwd", h, g, dhn, dh)
    return dh_in, dg, dw1_buf, dw2_buf


def _shift_dn(x, s, row):
    return x if s == 0 else jnp.where(row >= s, pltpu.roll(x, s, 0), 0.0)


def _shift_up(x, s, row):
    n = x.shape[0]
    return x if s == 0 else jnp.where(row < n - s, pltpu.roll(x, n - s, 0), 0.0)


def _conv_pre(x, w, b, row):
    c = jnp.broadcast_to(b, x.shape)
    for j in range(4):
        c = c + w[j:j + 1, :] * _shift_dn(x, 3 - j, row)
    return c


def _conv_fwd(name, x_arr, blk_off, w, b):
    L = x_arr.shape[0]
    C = w.shape[1]

    def f(x, wv, bv):
        c = _conv_pre(x, wv, bv, _iota(x.shape, 0))
        return c * jax.nn.sigmoid(c)

    return _ew(name, f, [(x_arr, (L, 128), lambda j, z: (0, blk_off + j)), (w, (4, 128), lambda j, z: (0, j)),
                         (b, (1, 128), lambda j, z: (0, j))],
               [((L, C), f32, (L, 128), lambda j, z: (0, j), None)], (C // 128, 1))[0]


def _conv_bwd(name, x_arr, blk_off, w, b, dy):
    L = x_arr.shape[0]
    C = w.shape[1]

    def f(x, wv, bv, g):
        row = _iota(x.shape, 0)
        c = _conv_pre(x, wv, bv, row)
        s = jax.nn.sigmoid(c)
        dc = g * (s * (1.0 + c * (1.0 - s)))
        dx = jnp.zeros_like(x)
        dw = jnp.zeros((4, 128), f32)
        r4 = _iota((4, 128), 0)
        for j in range(4):
            dx = dx + wv[j:j + 1, :] * _shift_up(dc, 3 - j, row)
            dwj = jnp.sum(dc * _shift_dn(x, 3 - j, row), axis=0, keepdims=True)
            dw = dw + jnp.where(r4 == j, jnp.broadcast_to(dwj, (4, 128)), 0.0)
        return dx, dw, jnp.sum(dc, axis=0, keepdims=True)

    return _ew(name, f, [(x_arr, (L, 128), lambda j, z: (0, blk_off + j)), (w, (4, 128), lambda j, z: (0, j)),
                         (b, (1, 128), lambda j, z: (0, j)), (dy, (L, 128), lambda j, z: (0, j))],
               [((L, C), f32, (L, 128), lambda j, z: (0, j), None), ((4, C), f32, (4, 128), lambda j, z: (0, j), None),
                ((1, C), f32, (1, 128), lambda j, z: (0, j), None)], (C // 128, 1))


def _l2norm(t):
    return t * lax.rsqrt(jnp.sum(t * t, axis=-1, keepdims=True) + 1e-6)


def _gdn_act(cq, ck, ab, alog, dtb):
    h = pl.program_id(1)
    qn = _l2norm(cq) * (128.0 ** -0.5)
    kn = _l2norm(ck)
    lane = _iota(ab.shape, 1)
    a_raw = jnp.sum(jnp.where(lane == h, ab, 0.0), axis=1, keepdims=True)
    b_raw = jnp.sum(jnp.where(lane == h + GDN_HEADS, ab, 0.0), axis=1, keepdims=True)
    lane1 = _iota(alog.shape, 1)
    al = jnp.sum(jnp.where(lane1 == h, alog, 0.0), axis=1, keepdims=True)
    db = jnp.sum(jnp.where(lane1 == h, dtb, 0.0), axis=1, keepdims=True)
    g = -jnp.exp(al) * jax.nn.softplus(a_raw + db)
    beta = jax.nn.sigmoid(b_raw)
    return qn, kn, jnp.broadcast_to(g, cq.shape), jnp.broadcast_to(beta, cq.shape)


def _each(f, *lists):
    return [f(*a) for a in zip(*lists)]


def _gdn_chunk(states, consts, vals, dots):
    S = list(states)
    cut = [slice(128 * i, 128 * i + 128) for i in range(len(S))]
    q, k, v, gb, bb = ([t[:, c] for c in cut] for t in vals)
    C = vals[0].shape[0]
    row, col = _iota((C, C), 0), _iota((C, C), 1)
    causal, strict = row >= col, row > col
    ltri = causal.astype(f32)
    eye = (row == col).astype(f32)
    e0 = (_iota((C, 128), 1) == 0).astype(f32)
    last = _iota((C, 1), 0) == C - 1
    Gb = _each(lambda g: dots.dot01(ltri, g), gb)
    Gc = _each(lambda g: jnp.mean(g, axis=1, keepdims=True), Gb)
    Gr = _each(lambda g: dots.dot01(e0, g, _NT), Gb)
    bc = _each(lambda b: jnp.mean(b, axis=1, keepdims=True), bb)
    decay = _each(lambda gc, gr: jnp.where(causal, jnp.exp(jnp.where(causal, gc - gr, 0.0)), 0.0), Gc, Gr)
    kk = _each(lambda a: _dotb(a, a, _NT), k)
    A = _each(lambda b, x, d: jnp.where(strict, b * x * d, 0.0), bc, kk, decay)
    M = _each(lambda a: eye - a, A)
    P = _each(lambda a: dots.dot3(a, a), A)
    for it in range(5):
        M = _each(lambda m, p: m + dots.dot3(m, p), M, P)
        if it < 4:
            P = _each(lambda p: dots.dot3(p, p), P)
    eG = _each(jnp.exp, Gc)
    u = _each(lambda m, x, b: _dotb(m, x * b), M, v, bc)
    w = _each(lambda m, x, b, e: _dotb(m, x * (b * e)), M, k, bc, eG)
    qk = _each(lambda a, b, d: _dotb(a, b, _NT) * d, q, k, decay)
    g_last = _each(lambda gc: jnp.sum(jnp.where(last, gc, 0.0), axis=0, keepdims=True), Gc)
    v_new = _each(lambda a, b, s: a - _dotb(b, s), u, w, S)
    o = _each(lambda a, e, s, b, x: _dotb(a * e, s) + _dotb(b, x), q, eG, S, qk, v_new)
    S_new = _each(lambda gl, s, a, gc, x: jnp.exp(gl) * s + _dotb(a * jnp.exp(gl - gc), x, _TN), g_last, S, k, Gc, v_new)
    return S_new, [jnp.concatenate(o, axis=1)]


def _gdn_post(o, gate, g):
    return _rms(o, g) * (gate * jax.nn.sigmoid(gate))


def _pad_row(v):
    return jnp.pad(v.astype(f32), (0, 128 - v.shape[0])).reshape(1, 128)


def _gdn_fwd(h, g_norm, w_ext, conv_w, a_log, dt_bias, o_norm_g, w_out):
    L, D = h.shape
    tm = _head_rows(L)
    nc = L // CHUNK
    H = GDN_HEADS
    hn = _rms_fwd("mix_norm", h, g_norm)
    proj = _mm_plain("gdn_in", hn, w_ext, _NN)
    zb = jnp.zeros((1, 3 * D), f32)
    cq = _conv_fwd("gdn_conv", proj, 0, conv_w, zb)
    alog, dtb = _pad_row(a_log), _pad_row(dt_bias)
    act_ins = [(cq, (tm, 128), lambda r, hh: (r, hh)), (cq, (tm, 128), lambda r, hh: (r, H + hh)),
               (proj, (tm, 128), lambda r, hh: (r, 4 * H)), (alog, (1, 128), lambda r, hh: (0, 0)),
               (dtb, (1, 128), lambda r, hh: (0, 0))]
    qn, kn, gb, bb = _ew("gdn_act", _gdn_act, act_ins,
                         [((L, D), f32, (tm, 128), lambda r, hh: (r, hh), None)] * 4, (L // tm, H))
    cblk = (CHUNK, 128 * GDN_HB)
    core_ins = [(qn, cblk, lambda u, c: (c, u)), (kn, cblk, lambda u, c: (c, u)),
                (cq, cblk, lambda u, c: (c, 2 * H // GDN_HB + u), (L, D), lambda u, c: (c, u)),
                (gb, cblk, lambda u, c: (c, u)), (bb, cblk, lambda u, c: (c, u))]
    (o,), saved_s = _scan_fwd("gdn_core", _gdn_chunk, GDN_HB, (128, 128), [], core_ins,
                              [((L, D), cblk, lambda u, c: (c, u))], H // GDN_HB, nc)
    on = o_norm_g.reshape(1, 128)
    post_ins = [(o, (tm, 128), lambda r, hh: (r, hh)), (proj, (tm, 128), lambda r, hh: (r, 3 * H + hh)),
                (on, (1, 128), lambda r, hh: (0, 0))]
    y = _ew("gdn_post", _gdn_post, post_ins, [((L, D), f32, (tm, 128), lambda r, hh: (r, hh), None)], (L // tm, H))[0]
    h_out = _mm_plain("gdn_out", y, w_out, _NN, epi_fn=lambda acc, res: acc + res, aux=[(h, "e")])
    saved = dict(hn=hn, proj=proj, cq=cq, alog=alog, dtb=dtb, act_ins=act_ins, core_ins=core_ins, saved_s=saved_s,
                 post_ins=post_ins, y=y, zb=zb)
    return h_out, saved


def _gdn_bwd(dh, h, g_norm, w_ext, conv_w, w_out, sv):
    L, D = h.shape
    tm = _head_rows(L)
    nc = L // CHUNK
    H = GDN_HEADS
    dy = _mm_plain("gdn_out_dx", dh, w_out, _NT)
    dw_out = _mm_plain("gdn_out_dw", sv["y"], dh, _TN)
    hd = ((L, D), f32, (tm, 128), lambda r, hh: (r, hh), None)
    d_o, d_gate, d_on = _ew("gdn_post_bwd", _vjp_fn(_gdn_post, 3),
                            sv["post_ins"] + [(dy, (tm, 128), lambda r, hh: (r, hh))],
                            [hd, hd, ((1, 128), f32, (1, 128), lambda r, hh: (0, 0), "all")], (L // tm, H))
    cblk = (CHUNK, 128 * GDN_HB)
    _, (dqn, dkn, dv, dgb, dbb) = _scan_bwd("gdn_core_bwd", _gdn_chunk, GDN_HB, (128, 128), [], sv["core_ins"],
                                            sv["saved_s"], [(d_o, cblk, lambda u, c: (c, u))], H // GDN_HB, nc)
    cts = [(t, (tm, 128), lambda r, hh: (r, hh)) for t in (dqn, dkn, dgb, dbb)]
    row128 = ((1, 128), f32, (1, 128), lambda r, hh: (0, 0), "all")
    d_cq, d_ck, d_ab, d_alog, d_dtb = _ew(
        "gdn_act_bwd", _vjp_fn(_gdn_act, 5), sv["act_ins"] + cts,
        [hd, hd, ((L, 128), f32, (tm, 128), lambda r, hh: (r, 0), "inner"), row128, row128], (L // tm, H))
    d_conv_out = jnp.concatenate([d_cq, d_ck, dv], axis=1)
    d_conv_in, d_conv_w, _ = _conv_bwd("gdn_conv_bwd", sv["proj"], 0, conv_w, sv["zb"], d_conv_out)
    d_proj = jnp.concatenate([d_conv_in, d_gate, d_ab], axis=1)
    dw_ext = _mm_plain("gdn_in_dw", sv["hn"], d_proj, _TN)
    dhn = _mm_plain("gdn_in_dx", d_proj, w_ext, _NT)
    dh_in, dg = _rms_bwd("mix_norm_bwd", h, g_norm, dhn, dh)
    grads = dict(norm=dg, w_ext=dw_ext, conv_w=d_conv_w, a_log=d_alog[0, :H], dt_bias=d_dtb[0, :H],
                 o_norm_g=d_on[0], w_out=dw_out)
    return dh_in, grads


def _expand_lanes(row, width, rep):
    sel = ((_iota((128, width), 1) // rep) == _iota((128, width), 0)).astype(f32)
    return jnp.mean(_dot(jnp.broadcast_to(row, (8, 128)), sel), axis=0, keepdims=True)


def _s5_params(lre, lim, ldt, wbr, wbi):
    dt = jnp.exp(_expand_lanes(ldt, 512, S5_STATE))
    mag = jnp.exp(lre * dt)
    ang = lim * dt
    abr, abi = mag * jnp.cos(ang), mag * jnp.sin(ang)
    nr = abr - 1.0
    den = lre * lre + lim * lim
    cr = (nr * lre + abi * lim) / den
    ci = (abi * lre - nr * lim) / den
    return abr, abi, cr * wbr - ci * wbi, cr * wbi + ci * wbr


def _s5_scan(name, xr, xi, ar, ai, rev, want_prev):
    L, W = xr.shape
    nb = L // 8
    n_out = 4 if want_prev else 2
    lanes = S5_SCAN_LANES if want_prev else 2 * S5_SCAN_LANES

    def body(xr_ref, xi_ref, ar_ref, ai_ref, *outs):
        a_r = ar_ref[...]
        a_i = -ai_ref[...] if rev else ai_ref[...]

        def cm(p, q):
            return p[0] * q[0] - p[1] * q[1], p[0] * q[1] + p[1] * q[0]

        a1 = (a_r, a_i)
        a2 = cm(a1, a1)
        a3 = cm(a2, a1)
        a4 = cm(a2, a2)
        pw = [a1, a2, a3, a4, cm(a4, a1), cm(a4, a2), cm(a4, a3), cm(a4, a4)]
        blk8 = (8, lanes)
        row = _iota(blk8, 0)
        tab_r = jnp.zeros(blk8, f32)
        tab_i = jnp.zeros(blk8, f32)
        for t in range(8):
            idx = 7 - t if rev else t
            tab_r = jnp.where(row == idx, jnp.broadcast_to(pw[t][0], blk8), tab_r)
            tab_i = jnp.where(row == idx, jnp.broadcast_to(pw[t][1], blk8), tab_i)
        lv = [(d, jnp.broadcast_to(p[0], blk8), jnp.broadcast_to(p[1], blk8)) for d, p in ((1, a1), (2, a2), (4, a4))]

        def step(i, carry):
            cr, ci = carry
            blk = nb - 1 - i if rev else i
            r0 = pl.multiple_of(blk * 8, 8)
            x_r = xr_ref[pl.ds(r0, 8), :]
            x_i = xi_ref[pl.ds(r0, 8), :]
            for d, p_r, p_i in lv:
                if rev:
                    s_r = jnp.where(row < 8 - d, pltpu.roll(x_r, 8 - d, 0), 0.0)
                    s_i = jnp.where(row < 8 - d, pltpu.roll(x_i, 8 - d, 0), 0.0)
                else:
                    s_r = jnp.where(row >= d, pltpu.roll(x_r, d, 0), 0.0)
                    s_i = jnp.where(row >= d, pltpu.roll(x_i, d, 0), 0.0)
                x_r, x_i = x_r + p_r * s_r - p_i * s_i, x_i + p_r * s_i + p_i * s_r
            x_r, x_i = x_r + tab_r * cr - tab_i * ci, x_i + tab_r * ci + tab_i * cr
            outs[0][pl.ds(r0, 8), :] = x_r
            outs[1][pl.ds(r0, 8), :] = x_i
            if want_prev:
                outs[2][pl.ds(r0, 8), :] = jnp.where(row >= 1, pltpu.roll(x_r, 1, 0), cr)
                outs[3][pl.ds(r0, 8), :] = jnp.where(row >= 1, pltpu.roll(x_i, 1, 0), ci)
            e = 0 if rev else 7
            return jnp.broadcast_to(x_r[e:e + 1, :], blk8), jnp.broadcast_to(x_i[e:e + 1, :], blk8)

        lax.fori_loop(0, nb, step, (jnp.zeros(blk8, f32), jnp.zeros(blk8, f32)))

    per = 512 // lanes
    col = pl.BlockSpec((L, lanes), lambda q, z: (0, q))
    aspec = pl.BlockSpec((None, 1, lanes), lambda q, z: (q // per, 0, q % per))
    return pl.pallas_call(
        body, name=name, grid=(W // lanes, 1), in_specs=[col, col, aspec, aspec], out_specs=[col] * n_out,
        out_shape=[jax.ShapeDtypeStruct((L, W), f32)] * n_out, compiler_params=_params(2),
    )(xr, xi, ar, ai)


def _blockdiag(t, n_in, n_out):
    t4 = t.reshape(8, 8, n_in, n_out)
    return jnp.einsum("jaio,ab->jaibo", t4, jnp.eye(8, dtype=t.dtype)).reshape(8, 8 * n_in, 8 * n_out)


def _blockdiag_t(w, n_in, n_out):
    w5 = w.reshape(8, 8, n_in, 8, n_out)
    return jnp.einsum("jaibo,ab->jaio", w5, jnp.eye(8, dtype=w.dtype)).reshape(64, n_in, n_out)


def _glu(ag, h):
    n = ag.shape[1] // 2
    return h + ag[:, :n] * jax.nn.sigmoid(ag[:, n:])


def _s5_fwd(h, g_norm, w_in, lam_re, lam_im, log_dt, b_re, b_im, c_re, c_im, d_skip, w_out_g):
    L, D = h.shape
    tm, te = _mm_rows(L), _row_tile(L)
    W = 8 * 512
    hn = _rms_fwd("mix_norm", h, g_norm)
    u = _mm_plain("s5_in", hn, w_in, _NN)
    lre, lim = lam_re.reshape(8, 1, 512), lam_im.reshape(8, 1, 512)
    ldt = jnp.pad(log_dt.reshape(8, 1, 8), ((0, 0), (0, 0), (0, 120)))
    wbr = _blockdiag(b_re.transpose(0, 2, 1), 16, 64)
    wbi = _blockdiag(b_im.transpose(0, 2, 1), 16, 64)
    wcr = _blockdiag(c_re.transpose(0, 2, 1), 64, 16)
    wci = _blockdiag(c_im.transpose(0, 2, 1), 64, 16)
    jb = lambda shape: (shape, lambda j, z: (j, 0, 0))
    par_ins = [(lre, *jb((None, 1, 512))), (lim, *jb((None, 1, 512))), (ldt, *jb((None, 1, 128))),
               (wbr, *jb((None, 128, 512))), (wbi, *jb((None, 128, 512)))]
    abr, abi, bbr, bbi = _ew("s5_params", _s5_params, par_ins,
                             [((8, 1, 512), f32, *jb((None, 1, 512)), None)] * 2
                             + [((8, 128, 512), f32, *jb((None, 128, 512)), None)] * 2, (8, 1))

    def bu(name, wb):
        return _mm(name, u, wb, dims=_NN, grid=(L // tm, 8, 1),
                   a_spec=pl.BlockSpec((tm, 128), lambda i, j, k: (i, j)),
                   b_spec=pl.BlockSpec((None, 128, 512), lambda i, j, k: (j, 0, 0)),
                   out_shape=jax.ShapeDtypeStruct((L, W), f32), out_spec=pl.BlockSpec((tm, 512), lambda i, j, k: (i, j)),
                   acc_shape=(tm, 512))

    bur, bui = bu("s5_bu", bbr), bu("s5_bu", bbi)
    sr, si, pr, pi = _s5_scan("s5_scan", bur, bui, abr, abi, False, True)
    d_row = d_skip.reshape(1, D)
    cspec = dict(a_spec=pl.BlockSpec((tm, 512), lambda i, j, k: (i, j)),
                 b_spec=pl.BlockSpec((None, 512, 128), lambda i, j, k: (j, 0, 0)),
                 out_shape=jax.ShapeDtypeStruct((L, D), f32), out_spec=pl.BlockSpec((tm, 128), lambda i, j, k: (i, j)),
                 acc_shape=(tm, 128))
    e128 = pl.BlockSpec((tm, 128), lambda i, j, k: (i, j))
    pre1 = _mm("s5_c_re", sr, wcr, dims=_NN, grid=(L // tm, 8, 1), **cspec)
    pre = _mm("s5_c_im", si, wci, dims=_NN, grid=(L // tm, 8, 1),
              aux=[(pre1, e128, "e"), (u, e128, "e"), (d_row, pl.BlockSpec((1, 128), lambda i, j, k: (0, j)), "e")],
              epi_fn=lambda acc, p1, uu, dd: p1 - acc + dd * uu, **cspec)
    ws = D // N_DEV * 2
    ag = _mm("s5_out", pre, w_out_g, dims=_NN, grid=(L // tm, N_DEV, 1),
             a_spec=pl.BlockSpec((tm, D), lambda i, j, k: (i, 0)),
             b_spec=pl.BlockSpec((None, None, D, ws), lambda i, j, k: (j, 0, 0, 0)),
             out_shape=jax.ShapeDtypeStruct((L, 2 * D), f32), out_spec=pl.BlockSpec((tm, ws), lambda i, j, k: (i, j)),
             a_fn=jax.nn.gelu, acc_shape=(tm, ws), cache_a=True)
    h_out = _ew("s5_glu", _glu, [(ag, *_rows(te, 2 * D)), (h, *_rows(te, D))],
                [((L, D), f32, *_rows(te, D), None)], (L // te, 1))[0]
    saved = dict(hn=hn, u=u, par_ins=par_ins, abr=abr, abi=abi, bbr=bbr, bbi=bbi, wcr=wcr, wci=wci, sr=sr, si=si,
                 pr=pr, pi=pi, pre=pre, ag=ag, d_row=d_row)
    return h_out, saved


def _s5_bwd(dh, h, g_norm, w_in, w_out_g, sv):
    L, D = h.shape
    tm, te = _mm_rows(L), _row_tile(L)
    tk = min(512, L)
    W = 8 * 512
    ws = D // N_DEV * 2
    u, pre, d_row = sv["u"], sv["pre"], sv["d_row"]
    d_ag = _ew("s5_glu_bwd", lambda ag, hv, g: _vjp_fn(_glu, 2)(ag, hv, g)[0],
               [(sv["ag"], *_rows(te, 2 * D)), (h, *_rows(te, D)), (dh, *_rows(te, D))],
               [((L, 2 * D), f32, *_rows(te, 2 * D), None)], (L // te, 1))[0]
    tr = D
    dw_out = _mm("s5_out_dw", pre, d_ag, dims=_TN, grid=(D // tr, N_DEV, L // tk),
                 a_spec=pl.BlockSpec((tk, tr), lambda i, j, k: (k, i)),
                 b_spec=pl.BlockSpec((tk, ws), lambda i, j, k: (k, j)),
                 out_shape=jax.ShapeDtypeStruct((N_DEV, 1, D, ws), f32),
                 out_spec=pl.BlockSpec((None, None, tr, ws), lambda i, j, k: (j, 0, i, 0)),
                 a_fn=jax.nn.gelu, acc_shape=(tr, ws))
    tn = 512
    dpre = _mm("s5_out_dx", d_ag, w_out_g, dims=_NT, grid=(L // tm, D // tn, N_DEV),
               a_spec=pl.BlockSpec((tm, ws), lambda i, j, k: (i, k)),
               b_spec=pl.BlockSpec((None, None, tn, ws), lambda i, j, k: (k, 0, j, 0)),
               out_shape=jax.ShapeDtypeStruct((L, D), f32), out_spec=pl.BlockSpec((tm, tn), lambda i, j, k: (i, j)),
               aux=[(pre, pl.BlockSpec((tm, tn), lambda i, j, k: (i, j)), "e")],
               epi_fn=lambda acc, p: _vjp_fn(jax.nn.gelu, 1)(p, acc)[0], acc_shape=(tm, tn))
    d_d = _ew("s5_dskip", lambda a, b: jnp.sum(a * b, axis=0, keepdims=True),
              [(dpre, *_rows(te, D)), (u, *_rows(te, D))], [((1, D), f32, *_const((1, D)), "all")], (L // te, 1))[0]
    neg = lambda acc: -acc
    dsspec = dict(dims=_NT, grid=(L // tm, 8, 1), a_spec=pl.BlockSpec((tm, 128), lambda i, j, k: (i, j)),
                  b_spec=pl.BlockSpec((None, 512, 128), lambda i, j, k: (j, 0, 0)),
                  out_shape=jax.ShapeDtypeStruct((L, W), f32), out_spec=pl.BlockSpec((tm, 512), lambda i, j, k: (i, j)),
                  acc_shape=(tm, 512))
    dsr = _mm("s5_c_re_dx", dpre, sv["wcr"], **dsspec)
    dsi = _mm("s5_c_im_dx", dpre, sv["wci"], epi_fn=neg, **dsspec)
    dwspec = dict(dims=_TN, grid=(8, 1, L // tk), a_spec=pl.BlockSpec((tk, 512), lambda i, j, k: (k, i)),
                  b_spec=pl.BlockSpec((tk, 128), lambda i, j, k: (k, i)),
                  out_shape=jax.ShapeDtypeStruct((8, 512, 128), f32),
                  out_spec=pl.BlockSpec((None, 512, 128), lambda i, j, k: (i, 0, 0)), acc_shape=(512, 128))
    dwcr = _mm("s5_c_re_dw", sv["sr"], dpre, **dwspec)
    dwci = _mm("s5_c_im_dw", sv["si"], dpre, epi_fn=neg, **dwspec)
    lr, li = _s5_scan("s5_scan_bwd", dsr, dsi, sv["abr"], sv["abi"], True, False)

    def da(lrv, liv, prv, piv):
        return (jnp.sum(lrv * prv + liv * piv, axis=0, keepdims=True),
                jnp.sum(liv * prv - lrv * piv, axis=0, keepdims=True))

    sblk = ((te, 512), lambda j, r: (r, j))
    dabr, dabi = _ew("s5_dlam", da, [(lr, *sblk), (li, *sblk), (sv["pr"], *sblk), (sv["pi"], *sblk)],
                     [((8, 1, 512), f32, (None, 1, 512), lambda j, r: (j, 0, 0), "inner")] * 2, (8, L // te))
    dbspec = dict(dims=_TN, grid=(8, 1, L // tk), a_spec=pl.BlockSpec((tk, 128), lambda i, j, k: (k, i)),
                  b_spec=pl.BlockSpec((tk, 512), lambda i, j, k: (k, i)),
                  out_shape=jax.ShapeDtypeStruct((8, 128, 512), f32),
                  out_spec=pl.BlockSpec((None, 128, 512), lambda i, j, k: (i, 0, 0)), acc_shape=(128, 512))
    dbbr = _mm("s5_bu_dw", u, lr, **dbspec)
    dbbi = _mm("s5_bu_dw", u, li, **dbspec)
    duspec = dict(dims=_NT, grid=(L // tm, 8, 1), a_spec=pl.BlockSpec((tm, 512), lambda i, j, k: (i, j)),
                  b_spec=pl.BlockSpec((None, 128, 512), lambda i, j, k: (j, 0, 0)),
                  out_shape=jax.ShapeDtypeStruct((L, D), f32), out_spec=pl.BlockSpec((tm, 128), lambda i, j, k: (i, j)),
                  acc_shape=(tm, 128))
    e128 = pl.BlockSpec((tm, 128), lambda i, j, k: (i, j))
    du1 = _mm("s5_bu_dx_re", lr, sv["bbr"], **duspec)
    du = _mm("s5_bu_dx_im", li, sv["bbi"],
             aux=[(du1, e128, "e"), (dpre, e128, "e"), (d_row, pl.BlockSpec((1, 128), lambda i, j, k: (0, j)), "e")],
             epi_fn=lambda acc, d1, dp, dd: acc + d1 + dp * dd, **duspec)
    jb = lambda shape: (shape, lambda j, z: (j, 0, 0))
    cts = [(dabr, *jb((None, 1, 512))), (dabi, *jb((None, 1, 512))), (dbbr, *jb((None, 128, 512))),
           (dbbi, *jb((None, 128, 512)))]
    dlre, dlim, dldt, dwbr, dwbi = _ew(
        "s5_params_bwd", _vjp_fn(_s5_params, 5), sv["par_ins"] + cts,
        [((8, 1, 512), f32, *jb((None, 1, 512)), None)] * 2 + [((8, 1, 128), f32, *jb((None, 1, 128)), None)]
        + [((8, 128, 512), f32, *jb((None, 128, 512)), None)] * 2, (8, 1))
    dw_in = _mm_plain("s5_in_dw", sv["hn"], du, _TN)
    dhn = _mm_plain("s5_in_dx", du, w_in, _NT)
    dh_in, dg = _rms_bwd("mix_norm_bwd", h, g_norm, dhn, dh)
    grads = dict(norm=dg, w_in=dw_in, lam_re=dlre.reshape(64, 64), lam_im=dlim.reshape(64, 64),
                 log_dt=dldt[:, 0, :8].reshape(64),
                 b_re=_blockdiag_t(dwbr, 16, 64).transpose(0, 2, 1), b_im=_blockdiag_t(dwbi, 16, 64).transpose(0, 2, 1),
                 c_re=_blockdiag_t(dwcr, 64, 16).transpose(0, 2, 1), c_im=_blockdiag_t(dwci, 64, 16).transpose(0, 2, 1),
                 d=d_d[0], w_out=dw_out)
    return dh_in, grads


def _m2_act(dt_raw, dtbias, alog):
    dt = jax.nn.softplus(dt_raw + dtbias)
    da = dt * (-jnp.exp(alog))
    sel = ((_iota((128, M2_INNER), 1) // 64) == _iota((128, M2_INNER), 0)).astype(f32)
    return _dot(dt, sel), _dot(da, sel)


def _m2_dexp(d):
    return _expand_lanes(d, M2_INNER, 64)


def _ssd_chunk(states, consts, vals, dots):
    (dsk,) = consts
    S = list(states)
    n = len(S)
    cut = [slice(128 * i, 128 * i + 128) for i in range(n)]
    x, dtb, dab = ([t[:, c] for c in cut] for t in vals[:3])
    dsk = [dsk[:, c] for c in cut]
    B = [vals[3][:, cut[i // 2]] for i in range(n)]
    Cm = [vals[4][:, cut[i // 2]] for i in range(n)]
    C = vals[0].shape[0]
    row, col = _iota((C, C), 0), _iota((C, C), 1)
    causal = row >= col
    ltri = causal.astype(f32)
    lane = _iota((C, 128), 1)
    last = _iota((C, 128), 0) == C - 1
    eye128 = _iota((128, 128), 0) == _iota((128, 128), 1)
    head = [jnp.logical_and(lane >= 64 * hh, lane < 64 * hh + 64) for hh in range(2)]
    pick = [(lane == 64 * hh).astype(f32) for hh in range(2)]
    xdt = _each(lambda a, b: a * b, x, dtb)
    cb = _each(lambda c, b: _dotb(c, b, _NT), Cm[::2], B[::2])
    cum = _each(lambda a: dots.dot01(ltri, a), dab)
    clast = _each(lambda a: jnp.sum(jnp.where(last, a, 0.0), axis=0, keepdims=True), cum)
    st = _each(lambda a, cl, cu, b: _dotb(a * jnp.exp(cl - cu), b, _TN), xdt, clast, cum, B)
    y = _each(lambda c, s, cu: _dotb(c, s, _NT) * jnp.exp(cu), Cm, S, cum)
    for hh in range(2):
        ccol = _each(lambda cu: jnp.sum(jnp.where(head[hh], cu, 0.0), axis=1, keepdims=True) * (1.0 / 64), cum)
        crow = _each(lambda cu: dots.dot01(pick[hh], cu, _NT), cum)
        lm = _each(lambda a, b: jnp.where(causal, jnp.exp(jnp.where(causal, a - b, 0.0)), 0.0), ccol, crow)
        y = [y[i] + _dotb(cb[i // 2] * lm[i], jnp.where(head[hh], xdt[i], 0.0)) for i in range(n)]
    cdcol = _each(lambda cl: jnp.sum(jnp.where(eye128, jnp.broadcast_to(jnp.exp(cl), (128, 128)), 0.0),
                                     axis=1, keepdims=True), clast)
    S_new = _each(lambda c, s, t: c * s + t, cdcol, S, st)
    out = _each(lambda a, d, b: a + d * b, y, dsk, x)
    return S_new, [jnp.concatenate(out, axis=1)]


def _m2_post(yc, z, ng):
    return _rms(yc * (z * jax.nn.sigmoid(z)), ng)


def _m2_fwd(h, g_norm, w_ext, conv_w, conv_b, dt_bias, a_log, d_skip, norm_g, w_out):
    L, D = h.shape
    tm = _row_tile(L)
    nc = L // CHUNK
    NI = M2_INNER
    hn = _rms_fwd("mix_norm", h, g_norm)
    proj = _mm_plain("m2_in", hn, w_ext, _NN)
    xbc = _conv_fwd("m2_conv", proj, NI // 128, conv_w, conv_b)
    dtb_row, alog_row, d_pad = _pad_row(dt_bias), _pad_row(a_log), _pad_row(d_skip)
    act_ins = [(proj, (tm, 128), lambda r, z: (r, 3 * NI // 128)), (dtb_row, *_const((1, 128))),
               (alog_row, *_const((1, 128)))]
    dtb, dab = _ew("m2_act", _m2_act, act_ins, [((L, NI), f32, *_rows(tm, NI), None)] * 2, (L // tm, 1))
    dsk = _ew("m2_dexp", _m2_dexp, [(d_pad, *_const((1, 128)))], [((1, NI), f32, *_const((1, NI)), None)], (1, 1))[0]
    GB = M2_GB
    x_blk, bc_blk = (CHUNK, 256 * GB), (CHUNK, 128 * GB)
    cins = [(dsk, (1, 256 * GB), lambda u, c: (0, u))]
    core_ins = [(xbc, x_blk, lambda u, c: (c, u), (L, NI), lambda u, c: (c, u)),
                (dtb, x_blk, lambda u, c: (c, u)), (dab, x_blk, lambda u, c: (c, u)),
                (xbc, bc_blk, lambda u, c: (c, 16 // GB + u), (L, D), lambda u, c: (c, u)),
                (xbc, bc_blk, lambda u, c: (c, 24 // GB + u), (L, D), lambda u, c: (c, u))]
    (yc,), saved_s = _scan_fwd("m2_core", _ssd_chunk, 2 * GB, (128, 128), cins, core_ins,
                               [((L, NI), x_blk, lambda u, c: (c, u))], 8 // GB, nc)
    tp = _head_rows(L)
    gblk = ((tp, 256), lambda g, r: (r, g))
    post_ins = [(yc, *gblk), (proj, *gblk), (norm_g, (1, 256), lambda g, r: (0, g))]
    yn = _ew("m2_post", _m2_post, post_ins, [((L, NI), f32, *gblk, None)], (8, L // tp))[0]
    h_out = _mm_plain("m2_out", yn, w_out, _NN, epi_fn=lambda acc, res: acc + res, aux=[(h, "e")])
    saved = dict(hn=hn, proj=proj, act_ins=act_ins, d_pad=d_pad, cins=cins, core_ins=core_ins, saved_s=saved_s,
                 post_ins=post_ins, yn=yn)
    return h_out, saved


def _m2_bwd(dh, h, g_norm, w_ext, conv_w, conv_b, w_out, sv):
    L, D = h.shape
    tm = _row_tile(L)
    nc = L // CHUNK
    NI = M2_INNER
    dyn = _mm_plain("m2_out_dx", dh, w_out, _NT)
    dw_out = _mm_plain("m2_out_dw", sv["yn"], dh, _TN)
    tp = _head_rows(L)
    gblk = ((tp, 256), lambda g, r: (r, g))
    d_yc, d_z, d_ng = _ew("m2_post_bwd", _vjp_fn(_m2_post, 3), sv["post_ins"] + [(dyn, *gblk)],
                          [((L, NI), f32, *gblk, None)] * 2 + [((1, NI), f32, (1, 256), lambda g, r: (0, g), "inner")],
                          (8, L // tp))
    (d_dsk,), (dx, d_dtb, d_dab, dB, dC) = _scan_bwd(
        "m2_core_bwd", _ssd_chunk, 2 * M2_GB, (128, 128), sv["cins"], sv["core_ins"], sv["saved_s"],
        [(d_yc, (CHUNK, 256 * M2_GB), lambda u, c: (c, u))], 8 // M2_GB, nc)
    row128 = ((1, 128), f32, *_const((1, 128)), "all")
    d_dt_raw, d_dtbias, d_alog = _ew(
        "m2_act_bwd", _vjp_fn(_m2_act, 3), sv["act_ins"] + [(d_dtb, *_rows(tm, NI)), (d_dab, *_rows(tm, NI))],
        [((L, 128), f32, *_rows(tm, 128), None), row128, row128], (L // tm, 1))
    d_d = _ew("m2_dexp_bwd", _vjp_fn(_m2_dexp, 1), [(sv["d_pad"], *_const((1, 128))), (d_dsk, *_const((1, NI)))],
              [((1, 128), f32, *_const((1, 128)), None)], (1, 1))[0]
    d_conv_out = jnp.concatenate([dx, dB, dC], axis=1)
    d_conv_in, d_conv_w, d_conv_b = _conv_bwd("m2_conv_bwd", sv["proj"], NI // 128, conv_w, conv_b, d_conv_out)
    d_proj = jnp.concatenate([d_z, d_conv_in, d_dt_raw], axis=1)
    dw_ext = _mm_plain("m2_in_dw", sv["hn"], d_proj, _TN)
    dhn = _mm_plain("m2_in_dx", d_proj, w_ext, _NT)
    dh_in, dg = _rms_bwd("mix_norm_bwd", h, g_norm, dhn, dh)
    grads = dict(norm=dg, w_ext=dw_ext, conv_w=d_conv_w, conv_b=d_conv_b, dt_bias=d_dtbias[0, :M2_HEADS],
                 a_log=d_alog[0, :M2_HEADS], d=d_d[0, :M2_HEADS], norm_g=d_ng, w_out=dw_out)
    return dh_in, grads


def _mesh_pos():
    return lax.axis_index("x"), lax.axis_index("y"), lax.axis_index("c")


def _flip(pos, p):
    x, y, c = pos
    return (1 - x if p & 4 else x, 1 - y if p & 2 else y, 1 - c if p & 1 else c)


def _index(pos):
    return 4 * pos[0] + 2 * pos[1] + pos[2]


def _comm_call(name, body, arrays, out_shape, n_sem):
    n = len(arrays)
    hbm = pl.BlockSpec(memory_space=pl.ANY)
    return pl.pallas_call(
        body, name=name, in_specs=[hbm] * n, out_specs=[hbm] * len(out_shape), out_shape=out_shape,
        scratch_shapes=[pltpu.SemaphoreType.DMA((n, n_sem)), pltpu.SemaphoreType.DMA((n, n_sem)),
                        pltpu.SemaphoreType.DMA((n, 4))],
    )(*arrays)


def _gather(name, arrays):
    n = len(arrays)

    def body(*refs):
        ins, outs = refs[:n], refs[n:2 * n]
        send_sems, recv_sems, loc_sems = refs[2 * n:]
        me = _mesh_pos()
        c = me[2]
        sib = _flip(me, 1)
        chips = [_flip(me, 4), _flip(me, 2), _flip(me, 6)]

        def copy(w, k, block, to, src=None):
            slab = outs[w].at[_index(block)]
            return pltpu.make_async_remote_copy(
                src_ref=slab if src is None else src, dst_ref=slab, send_sem=send_sems.at[w, k],
                recv_sem=recv_sems.at[w, k], device_id=to, device_id_type=MESH)

        local = [pltpu.make_async_copy(ins[w], outs[w].at[_index(me)], loc_sems.at[w, 0]) for w in range(n)]
        for cp in local:
            cp.start()
        first = [copy(w, 0, me, sib, src=ins[w]) for w in range(n)]
        first += [copy(w, 1 + j, me, chip, src=ins[w]) for j, chip in enumerate(chips) for w in range(n)]
        for cp in first:
            cp.start()
        passed = []
        for j, chip in enumerate(chips):
            for w in range(n):
                copy(w, 1 + j, chip, me).wait_recv()
                fwd = copy(w, 4 + j, chip, sib)
                fwd.start()
                passed.append(fwd)
        for w in range(n):
            copy(w, 0, sib, me).wait_recv()
        for j, chip in enumerate(chips):
            for w in range(n):
                copy(w, 4 + j, (chip[0], chip[1], 1 - c), me).wait_recv()
        for cp in first + passed:
            cp.wait_send()
        for cp in local:
            cp.wait()

    out_shape = [jax.ShapeDtypeStruct((N_DEV,) + a.shape, a.dtype) for a in arrays]
    return _comm_call(name, body, arrays, out_shape, N_DEV - 1)


_HBM = pl.BlockSpec(memory_space=pltpu.HBM)
_SEM = pl.BlockSpec(memory_space=pltpu.SEMAPHORE)
_SPLIT_COPIES = 4


def _split_targets(me):
    return [_flip(me, 1), _flip(me, 4), _flip(me, 2), _flip(me, 6)]


def _gather_start(name, arrays, lands):
    n = len(arrays)
    ns = n * _SPLIT_COPIES

    def body(*refs):
        ins, land = refs[:n], refs[n:2 * n]
        send_sems, recv_sems = refs[2 * n:2 * n + ns], refs[2 * n + ns:2 * n + 2 * ns]
        token = refs[4 * n + 2 * ns]
        me = _mesh_pos()
        for w in range(n):
            for k, to in enumerate(_split_targets(me)):
                pltpu.make_async_remote_copy(
                    src_ref=ins[w], dst_ref=land[w].at[_index(me)], send_sem=send_sems[w * _SPLIT_COPIES + k],
                    recv_sem=recv_sems[w * _SPLIT_COPIES + k], device_id=to, device_id_type=MESH).start()
        token[...] = jnp.zeros_like(token)

    sem = pltpu.SemaphoreType.DMA(())
    res = pl.pallas_call(
        body, name=name,
        out_shape=(*[sem] * (2 * ns), *[pltpu.HBM(a.shape, a.dtype) for a in arrays],
                   *[pltpu.HBM(a.shape, a.dtype) for a in lands], jax.ShapeDtypeStruct((8, 128), f32)),
        in_specs=[_HBM] * (2 * n),
        out_specs=(*[_SEM] * (2 * ns), *[_HBM] * (2 * n), pl.BlockSpec(memory_space=pltpu.VMEM)),
        input_output_aliases={i: 2 * ns + i for i in range(2 * n)},
        compiler_params=pltpu.CompilerParams(has_side_effects=pltpu.SideEffectType.DATAFLOW_SIDE_EFFECTING),
    )(*[pltpu.with_memory_space_constraint(a, pltpu.HBM) for a in list(arrays) + list(lands)])
    sems, rest = res[:2 * ns], res[2 * ns:]
    return sems[:ns], sems[ns:], rest[:n], rest[n:2 * n], rest[2 * n]


def _gather_wait(name, arrays, lands, send_sems, recv_sems, after):
    n = len(arrays)
    ns = n * _SPLIT_COPIES

    def body(*refs):
        ins, land = refs[:n], refs[n:2 * n]
        s_sems, r_sems = refs[2 * n:2 * n + ns], refs[2 * n + ns:2 * n + 2 * ns]
        me = _mesh_pos()
        for w in range(n):
            for k, peer in enumerate(_split_targets(me)):
                cp = pltpu.make_async_remote_copy(
                    src_ref=ins[w], dst_ref=land[w].at[_index(peer)], send_sem=s_sems[w * _SPLIT_COPIES + k],
                    recv_sem=r_sems[w * _SPLIT_COPIES + k], device_id=peer, device_id_type=MESH)
                cp.wait_send()
                cp.wait_recv()

    res = pl.pallas_call(
        body, name=name,
        out_shape=(*[pltpu.HBM(a.shape, a.dtype) for a in arrays], *[pltpu.HBM(a.shape, a.dtype) for a in lands]),
        in_specs=[_HBM] * (2 * n) + [_SEM] * (2 * ns) + [pl.BlockSpec(memory_space=pl.ANY)],
        out_specs=tuple([_HBM] * (2 * n)), input_output_aliases={i: i for i in range(2 * n)},
        compiler_params=pltpu.CompilerParams(has_side_effects=pltpu.SideEffectType.DATAFLOW_SIDE_EFFECTING),
    )(*arrays, *lands, *send_sems, *recv_sems, after)
    return res[n:]


def _gather_forward(name, lands):
    n = len(lands)

    def body(*refs):
        outs = refs[n:2 * n]
        send_sems, recv_sems, _ = refs[2 * n:]
        me = _mesh_pos()
        sib = _flip(me, 1)
        held = [_flip(me, 4), _flip(me, 2), _flip(me, 6), sib]

        def copy(w, j, block):
            slab = outs[w].at[_index(block)]
            return pltpu.make_async_remote_copy(src_ref=slab, dst_ref=slab, send_sem=send_sems.at[w, j],
                                                recv_sem=recv_sems.at[w, j], device_id=sib, device_id_type=MESH)

        sends = [copy(w, j, blk) for j, blk in enumerate(held) for w in range(n)]
        for cp in sends:
            cp.start()
        for j, blk in enumerate(held):
            for w in range(n):
                copy(w, j, (blk[0], blk[1], 1 - blk[2])).wait_recv()
        for cp in sends:
            cp.wait_send()

    hbm = pl.BlockSpec(memory_space=pl.ANY)
    return pl.pallas_call(
        body, name=name, in_specs=[hbm] * n, out_specs=[hbm] * n,
        out_shape=[jax.ShapeDtypeStruct(a.shape, a.dtype) for a in lands],
        input_output_aliases={i: i for i in range(n)},
        scratch_shapes=[pltpu.SemaphoreType.DMA((n, 4)), pltpu.SemaphoreType.DMA((n, 4)), pltpu.SemaphoreType.DMA((n, 4))],
    )(*lands)


def _scatter_pair(name, arrays):
    n = len(arrays)

    def body(*refs):
        ins, outs = refs[:n], refs[n:2 * n]
        send_sems, recv_sems, _ = refs[2 * n:]
        me = _mesh_pos()
        c = me[2]
        sib = _flip(me, 1)

        def copy(w, q):
            return pltpu.make_async_remote_copy(
                src_ref=ins[w].at[2 * q + 1 - c], dst_ref=outs[w].at[q], send_sem=send_sems.at[w, q],
                recv_sem=recv_sems.at[w, q], device_id=sib, device_id_type=MESH)

        cps = [copy(w, q) for q in range(4) for w in range(n)]
        for cp in cps:
            cp.start()
        for cp in cps:
            cp.wait()

    out_shape = [jax.ShapeDtypeStruct((4,) + a.shape[1:], a.dtype) for a in arrays]
    return _comm_call(name, body, arrays, out_shape, 4)


def _pair_add(name, full, theirs, core, dtype):
    _, R, C = theirs.shape
    tr = R if R <= 256 else (256 if C <= 512 else 128)

    def body(core_ref, mine_ref, theirs_ref, o_ref):
        o_ref[...] = (mine_ref[...] + theirs_ref[...]).astype(o_ref.dtype)

    blk = pl.BlockSpec((4, tr, C), lambda r, cr: (0, r, 0))
    grid_spec = pltpu.PrefetchScalarGridSpec(
        num_scalar_prefetch=1, grid=(R // tr,),
        in_specs=[pl.BlockSpec((4, None, tr, C), lambda r, cr: (0, cr[0], r, 0)), blk], out_specs=blk)
    return pl.pallas_call(
        body, name=name, grid_spec=grid_spec, out_shape=jax.ShapeDtypeStruct((4, R, C), dtype),
        compiler_params=_params(1),
    )(core.reshape(1).astype(jnp.int32), full.reshape(4, 2, R, C), theirs)


def _scatter_chips(name, arrays):
    n = len(arrays)

    def body(*refs):
        ins, outs = refs[:n], refs[n:2 * n]
        send_sems, recv_sems, loc_sems = refs[2 * n:]
        me = _mesh_pos()
        mq = 2 * me[0] + me[1]
        peers = [_flip(me, 4), _flip(me, 2), _flip(me, 6)]

        def copy(w, k):
            peer = peers[k]
            return pltpu.make_async_remote_copy(
                src_ref=ins[w].at[2 * peer[0] + peer[1]], dst_ref=outs[w].at[mq], send_sem=send_sems.at[w, k],
                recv_sem=recv_sems.at[w, k], device_id=peer, device_id_type=MESH)

        def arrival(w, k):
            peer = peers[k]
            return pltpu.make_async_remote_copy(
                src_ref=ins[w].at[mq], dst_ref=outs[w].at[2 * peer[0] + peer[1]], send_sem=send_sems.at[w, k],
                recv_sem=recv_sems.at[w, k], device_id=peer, device_id_type=MESH)

        local = [pltpu.make_async_copy(ins[w].at[mq], outs[w].at[mq], loc_sems.at[w, 0]) for w in range(n)]
        for cp in local:
            cp.start()
        sends = [copy(w, k) for k in range(3) for w in range(n)]
        for cp in sends:
            cp.start()
        for k in range(3):
            for w in range(n):
                arrival(w, k).wait_recv()
        for cp in sends:
            cp.wait_send()
        for cp in local:
            cp.wait()

    out_shape = [jax.ShapeDtypeStruct(a.shape, a.dtype) for a in arrays]
    return _comm_call(name, body, arrays, out_shape, 3)


def _chip_peers(me):
    return [_flip(me, 4), _flip(me, 2), _flip(me, 6)]


def _chips_start(name, arrays):
    n = len(arrays)
    ns = 3 * n

    def body(*refs):
        ins, land = refs[:n], refs[n:2 * n]
        send_sems, recv_sems = refs[2 * n:2 * n + ns], refs[2 * n + ns:2 * n + 2 * ns]
        token = refs[4 * n + 2 * ns]
        me = _mesh_pos()
        for w in range(n):
            for k, peer in enumerate(_chip_peers(me)):
                pltpu.make_async_remote_copy(
                    src_ref=ins[w].at[2 * peer[0] + peer[1]], dst_ref=land[w].at[k], send_sem=send_sems[3 * w + k],
                    recv_sem=recv_sems[3 * w + k], device_id=peer, device_id_type=MESH).start()
        token[...] = jnp.zeros_like(token)

    lands = [lax.empty((3,) + a.shape[1:], a.dtype) for a in arrays]
    sem = pltpu.SemaphoreType.DMA(())
    res = pl.pallas_call(
        body, name=name,
        out_shape=(*[sem] * (2 * ns), *[pltpu.HBM(a.shape, a.dtype) for a in arrays],
                   *[pltpu.HBM(a.shape, a.dtype) for a in lands], jax.ShapeDtypeStruct((8, 128), f32)),
        in_specs=[_HBM] * (2 * n),
        out_specs=(*[_SEM] * (2 * ns), *[_HBM] * (2 * n), pl.BlockSpec(memory_space=pltpu.VMEM)),
        input_output_aliases={i: 2 * ns + i for i in range(2 * n)},
        compiler_params=pltpu.CompilerParams(has_side_effects=pltpu.SideEffectType.DATAFLOW_SIDE_EFFECTING),
    )(*[pltpu.with_memory_space_constraint(a, pltpu.HBM) for a in list(arrays) + lands])
    sems, rest = res[:2 * ns], res[2 * ns:]
    return sems[:ns], sems[ns:], rest[:n], rest[n:2 * n], rest[2 * n]


def _chips_wait(name, arrays, lands, send_sems, recv_sems, after):
    n = len(arrays)
    ns = 3 * n

    def body(*refs):
        ins, land = refs[:n], refs[n:2 * n]
        s_sems, r_sems = refs[2 * n:2 * n + ns], refs[2 * n + ns:2 * n + 2 * ns]
        me = _mesh_pos()
        for w in range(n):
            for k, peer in enumerate(_chip_peers(me)):
                cp = pltpu.make_async_remote_copy(
                    src_ref=ins[w].at[2 * peer[0] + peer[1]], dst_ref=land[w].at[k], send_sem=s_sems[3 * w + k],
                    recv_sem=r_sems[3 * w + k], device_id=peer, device_id_type=MESH)
                cp.wait_send()
                cp.wait_recv()

    res = pl.pallas_call(
        body, name=name,
        out_shape=(*[pltpu.HBM(a.shape, a.dtype) for a in arrays], *[pltpu.HBM(a.shape, a.dtype) for a in lands]),
        in_specs=[_HBM] * (2 * n) + [_SEM] * (2 * ns) + [pl.BlockSpec(memory_space=pl.ANY)],
        out_specs=tuple([_HBM] * (2 * n)), input_output_aliases={i: i for i in range(2 * n)},
        compiler_params=pltpu.CompilerParams(has_side_effects=pltpu.SideEffectType.DATAFLOW_SIDE_EFFECTING),
    )(*arrays, *lands, *send_sems, *recv_sems, after)
    return res[:n], res[n:]


def _adamw_own(name, arrivals, sums, chip, w, m, v):
    R, C = w.shape
    tr = R if R <= 256 else (256 if C <= 512 else 128)
    bc1 = 1.0 - ADAM_B1 ** ADAM_STEP
    bc2 = 1.0 - ADAM_B2 ** ADAM_STEP

    def body(chip_ref, own_ref, p_ref, w_ref, m_ref, v_ref, g_ref, d_ref, m2_ref, v2_ref):
        g = own_ref[...].astype(f32)
        for k in range(3):
            g = g + p_ref[k].astype(f32)
        m2 = ADAM_B1 * m_ref[...] + (1.0 - ADAM_B1) * g
        v2 = ADAM_B2 * v_ref[...] + (1.0 - ADAM_B2) * jnp.square(g)
        g_ref[...] = g
        d_ref[...] = -ADAM_LR * ((m2 / bc1) / (jnp.sqrt(v2 / bc2) + ADAM_EPS) + ADAM_WD * w_ref[...])
        m2_ref[...] = m2
        v2_ref[...] = v2

    blk = pl.BlockSpec((tr, C), lambda r, cr: (r, 0))
    grid_spec = pltpu.PrefetchScalarGridSpec(
        num_scalar_prefetch=1, grid=(R // tr,),
        in_specs=[pl.BlockSpec((None, tr, C), lambda r, cr: (cr[0], r, 0)),
                  pl.BlockSpec((3, tr, C), lambda r, cr: (0, r, 0)), blk, blk, blk],
        out_specs=[blk] * 4)
    return pl.pallas_call(
        body, name=name, grid_spec=grid_spec, out_shape=[jax.ShapeDtypeStruct((R, C), f32)] * 4,
        compiler_params=_params(1),
    )(chip.reshape(1).astype(jnp.int32), sums, arrivals, w, m, v)


def _adamw(name, parts, w, m, v):
    R, C = w.shape
    n_parts = parts.shape[0]
    tr = R if R <= 256 else (256 if C <= 512 else 128)
    bc1 = 1.0 - ADAM_B1 ** ADAM_STEP
    bc2 = 1.0 - ADAM_B2 ** ADAM_STEP

    def f(p, wv, mv, vv):
        g = p[0].astype(f32)
        for i in range(1, n_parts):
            g = g + p[i].astype(f32)
        m2 = ADAM_B1 * mv + (1.0 - ADAM_B1) * g
        v2 = ADAM_B2 * vv + (1.0 - ADAM_B2) * jnp.square(g)
        delta = -ADAM_LR * ((m2 / bc1) / (jnp.sqrt(v2 / bc2) + ADAM_EPS) + ADAM_WD * wv)
        return g, delta, m2, v2

    blk = ((tr, C), lambda r, z: (r, 0))
    return _ew(name, f, [(parts, (n_parts, tr, C), lambda r, z: (0, r, 0)), (w, *blk), (m, *blk), (v, *blk)],
               [((R, C), f32, *blk, None)] * 4, (R // tr, 1))


_WEIGHTS = ["norm_mix_g", "norm_mlp_g", "mlp_w1", "mlp_w2", "gdn_w_in", "gdn_conv_w", "gdn_a_log", "gdn_dt_bias",
            "gdn_o_norm_g", "gdn_w_out", "s5_w_in", "s5_lam_re", "s5_lam_im", "s5_log_dt", "s5_b_re", "s5_b_im",
            "s5_c_re", "s5_c_im", "s5_d", "s5_w_out", "m2_w_in", "m2_conv_w", "m2_conv_b", "m2_dt_bias", "m2_a_log",
            "m2_d", "m2_norm_g", "m2_w_out", "final_norm_g"]
_SHARDED = ["mlp_w1", "mlp_w2", "gdn_w_in", "gdn_w_out", "s5_w_in", "s5_w_out", "m2_w_in", "m2_w_out",
            "gdn_conv_w", "m2_conv_w", "m2_conv_b", "m2_norm_g"]
_REPLICATED = [n for n in _WEIGHTS if n not in _SHARDED]
_GDN_IN, _M2_IN = 4112, 6176
_LAYER_KIND = (0, 1, 2, 0)


def _as2d(a):
    return a.reshape(-1, a.shape[-1])


def _cols_from_shards(g, width):
    return g.transpose(1, 0, 2).reshape(g.shape[1], width)


def _cols_to_shards(a, width):
    return a[:, :width].reshape(a.shape[0], N_DEV, width // N_DEV).transpose(1, 0, 2)


def kernel(x, norm_mix_g, norm_mlp_g, mlp_w1, mlp_w2, gdn_w_in, gdn_conv_w, gdn_a_log, gdn_dt_bias, gdn_o_norm_g, gdn_w_out, s5_w_in, s5_lam_re, s5_lam_im, s5_log_dt, s5_b_re, s5_b_im, s5_c_re, s5_c_im, s5_d, s5_w_out, m2_w_in, m2_conv_w, m2_conv_b, m2_dt_bias, m2_a_log, m2_d, m2_norm_g, m2_w_out, final_norm_g, loss_target, m_norm_mix_g, m_norm_mlp_g, m_mlp_w1, m_mlp_w2, m_gdn_w_in, m_gdn_conv_w, m_gdn_a_log, m_gdn_dt_bias, m_gdn_o_norm_g, m_gdn_w_out, m_s5_w_in, m_s5_lam_re, m_s5_lam_im, m_s5_log_dt, m_s5_b_re, m_s5_b_im, m_s5_c_re, m_s5_c_im, m_s5_d, m_s5_w_out, m_m2_w_in, m_m2_conv_w, m_m2_conv_b, m_m2_dt_bias, m_m2_a_log, m_m2_d, m_m2_norm_g, m_m2_w_out, m_final_norm_g, v_norm_mix_g, v_norm_mlp_g, v_mlp_w1, v_mlp_w2, v_gdn_w_in, v_gdn_conv_w, v_gdn_a_log, v_gdn_dt_bias, v_gdn_o_norm_g, v_gdn_w_out, v_s5_w_in, v_s5_lam_re, v_s5_lam_im, v_s5_log_dt, v_s5_b_re, v_s5_b_im, v_s5_c_re, v_s5_c_im, v_s5_d, v_s5_w_out, v_m2_w_in, v_m2_conv_w, v_m2_conv_b, v_m2_dt_bias, v_m2_a_log, v_m2_d, v_m2_norm_g, v_m2_w_out, v_final_norm_g):
    args = locals()
    W = {n: args[n] for n in _WEIGHTS}
    MOM = {n: args["m_" + n] for n in _WEIGHTS}
    VAR = {n: args["v_" + n] for n in _WEIGHTS}
    h = x[0]
    target = loss_target[0]
    L, D = h.shape

    first = [mlp_w1[0:1].astype(bf16), mlp_w2[0:1].astype(bf16), gdn_w_in[0:1].astype(bf16),
             gdn_w_out[0:1].astype(bf16), _as2d(gdn_conv_w), _as2d(m2_conv_w), _as2d(m2_conv_b), _as2d(m2_norm_g)]
    w1g0, w2g0, gin0, gout0, gconv, m2_cw, m2_cbg, m2_ngg = _gather("gather_first", first)
    stacked = jnp.concatenate([gdn_w_out[1], s5_w_in[0], m2_w_out[0]], axis=0).astype(bf16)
    rest = [mlp_w1[1:4].astype(bf16), mlp_w2[1:4].astype(bf16), gdn_w_in[1:2].astype(bf16), s5_w_out.astype(bf16),
            m2_w_in.astype(bf16), stacked]
    lands = [lax.empty((N_DEV,) + a.shape, a.dtype) for a in rest]
    send_sems, recv_sems, rest_thru, lands_thru, token = _gather_start("gather_rest_start", rest, lands)

    def gdn_weights(gin, gout, conv, j):
        return (jnp.pad(_cols_from_shards(gin[:, 0], _GDN_IN), ((0, 0), (0, GDN_EXT - _GDN_IN))),
                gout[:, 0].reshape(D, D), _cols_from_shards(conv[:, 4 * j:4 * j + 4], 3 * D))

    gdn_in, gdn_out, gdn_conv = [None, None], [None, None], [None, None]
    gdn_in[0], gdn_out[0], gdn_conv[0] = gdn_weights(gin0, gout0, gconv, 0)

    norm_mix = [norm_mix_g[i].reshape(1, D) for i in range(4)]
    norm_mix[0] = norm_mix[0] + token[0, 0]
    late = {}

    def mixer_fwd(i, hv):
        kind, j = _LAYER_KIND[i], i // 3
        gn = norm_mix[i]
        s5_in, s5_out_g = late.get("s5_in"), late.get("s5_out_g")
        m2_in, m2_conv, m2_cb, m2_ng, m2_out = (late.get(k) for k in ("m2_in", "m2_conv", "m2_cb", "m2_ng", "m2_out"))
        if kind == 0:
            return _gdn_fwd(hv, gn, gdn_in[j], gdn_conv[j], gdn_a_log[j], gdn_dt_bias[j], gdn_o_norm_g[j], gdn_out[j])
        if kind == 1:
            return _s5_fwd(hv, gn, s5_in, s5_lam_re[0], s5_lam_im[0], s5_log_dt[0], s5_b_re[0], s5_b_im[0],
                           s5_c_re[0], s5_c_im[0], s5_d[0], s5_out_g)
        return _m2_fwd(hv, gn, m2_in, m2_conv, m2_cb, m2_dt_bias[0], m2_a_log[0], m2_d[0], m2_ng, m2_out)

    def mixer_bwd(i, dh, hv, sv):
        kind, j = _LAYER_KIND[i], i // 3
        gn = norm_mix[i]
        s5_in, s5_out_g = late["s5_in"], late["s5_out_g"]
        m2_in, m2_conv, m2_cb, m2_out = (late[k] for k in ("m2_in", "m2_conv", "m2_cb", "m2_out"))
        if kind == 0:
            return _gdn_bwd(dh, hv, gn, gdn_in[j], gdn_conv[j], gdn_out[j], sv)
        if kind == 1:
            return _s5_bwd(dh, hv, gn, s5_in, s5_out_g, sv)
        return _m2_bwd(dh, hv, gn, m2_in, m2_conv, m2_cb, m2_out, sv)

    tape = []
    mlp_w = [(w1g0, w2g0, 0)]
    for i in range(4):
        if i == 1:
            landed = _gather_wait("gather_rest_wait", rest_thru, lands_thru, send_sems, recv_sems, h)
            w1gr, w2gr, gin1, s5_out_g, m2_in_g, rows_g = _gather_forward("gather_rest_forward", landed)
            gout1, s5_in_g, m2_out_g = rows_g[:, None, 0:128], rows_g[:, 128:256], rows_g[:, 256:512]
            mlp_w += [(w1gr, w2gr, k) for k in range(3)]
            gdn_in[1], gdn_out[1], gdn_conv[1] = gdn_weights(gin1, gout1, gconv, 1)
            late.update(
                s5_in=s5_in_g.reshape(D, D), s5_out_g=s5_out_g,
                m2_in=jnp.pad(_cols_from_shards(m2_in_g[:, 0], _M2_IN), ((0, 0), (0, M2_EXT - _M2_IN))),
                m2_out=m2_out_g.reshape(M2_INNER, D), m2_conv=_cols_from_shards(m2_cw, 2 * M2_INNER),
                m2_cb=_cols_from_shards(m2_cbg, 2 * M2_INNER), m2_ng=_cols_from_shards(m2_ngg, M2_INNER))
        h_mid, sv = mixer_fwd(i, h)
        h_next, hn, h1 = _mlp_fwd(h_mid, norm_mlp_g[i].reshape(1, D), *mlp_w[i])
        tape.append((h, sv, h_mid, hn, h1))
        h = h_next
    loss_row, dh, d_final = _loss_head(h, final_norm_g.reshape(1, D), target)
    loss = lax.psum(loss_row[0, 0], ("x", "y", "c"))

    fs = D_FF // N_DEV
    group_layers = (1, 1, 2)
    dw1 = [lax.empty((N_DEV, nl, D, fs), f32) for nl in group_layers]
    dw2 = [lax.empty((N_DEV, nl, fs, D), f32) for nl in group_layers]
    d_mix, d_mlp, mg = [None] * 4, [None] * 4, [None] * 4
    core = lax.axis_index("c")
    chip = 2 * lax.axis_index("x") + lax.axis_index("y")

    def pair_sums(tag, entries):
        theirs = _scatter_pair("scatter_pair_" + tag, [e[1] for e in entries])
        return [_pair_add("pair_add_%s_%s" % (tag, e[0]), e[1], th, core, bf16 if e[2] else f32)
                for e, th in zip(entries, theirs)]

    def mlp_grads(grp):
        nl = group_layers[grp]
        return [("mlp_w1", dw1[grp].reshape(N_DEV, nl * D, fs), True), ("mlp_w2", dw2[grp].reshape(N_DEV, nl * fs, D), True)]

    def gdn_grads(g):
        return [("gdn_w_in", _cols_to_shards(g["w_ext"], _GDN_IN), True),
                ("gdn_w_out", g["w_out"].reshape(N_DEV, D // N_DEV, D), True)]

    flying = {}
    for i in reversed(range(4)):
        h_in, sv, h_mid, hn, h1 = tape[i]
        grp, slot = min(i, 2), max(i - 2, 0)
        dh, d_mlp[i], dw1[grp], dw2[grp] = _mlp_bwd(dh, h_mid, norm_mlp_g[i].reshape(1, D), hn, h1, *mlp_w[i], slot,
                                                    dw1[grp], dw2[grp])
        dh, mg[i] = mixer_bwd(i, dh, h_in, sv)
        d_mix[i] = mg[i]["norm"]
        if i in (2, 1):
            if i == 2:
                entries = mlp_grads(2) + gdn_grads(mg[3]) + [
                    ("m2_w_in", _cols_to_shards(mg[2]["w_ext"], _M2_IN), True),
                    ("m2_w_out", mg[2]["w_out"].reshape(N_DEV, M2_INNER // N_DEV, D), True)]
            else:
                entries = mlp_grads(1) + [("s5_w_in", mg[1]["w_in"].reshape(N_DEV, D // N_DEV, D), True),
                                          ("s5_w_out", mg[1]["w_out"].reshape(N_DEV, D, 2 * D // N_DEV), True)]
            started = _chips_start("chips_start_%d" % grp, pair_sums("g%d" % grp, entries))
            flying[grp] = (entries, started)
            dh = dh + started[4][0, 0]
    grad_x = dh.reshape(1, L, D)
    ga, gb_, s5g, m2g = mg[0], mg[3], mg[1], mg[2]
    early = mlp_grads(0) + gdn_grads(ga) + [
        ("gdn_conv_w", jnp.concatenate([_cols_to_shards(g["conv_w"], 3 * D) for g in (ga, gb_)], axis=1), False),
        ("m2_conv_w", _cols_to_shards(m2g["conv_w"], 2 * M2_INNER), False),
        ("m2_conv_b", _cols_to_shards(m2g["conv_b"], 2 * M2_INNER), False),
        ("m2_norm_g", _cols_to_shards(m2g["norm_g"], M2_INNER), False)]
    early_parts = _scatter_chips("scatter_chips", pair_sums("g0", early))
    landed = {}
    for grp in (1, 2):
        entries, (s_sems, r_sems, sums, lands, _) = flying[grp]
        landed[grp] = (entries,) + tuple(_chips_wait("chips_wait_%d" % grp, sums, lands, s_sems, r_sems, early_parts[0]))

    rep = {
        "norm_mix_g": jnp.concatenate(d_mix, axis=0), "norm_mlp_g": jnp.concatenate(d_mlp, axis=0),
        "gdn_a_log": jnp.stack([ga["a_log"], gb_["a_log"]]), "gdn_dt_bias": jnp.stack([ga["dt_bias"], gb_["dt_bias"]]),
        "gdn_o_norm_g": jnp.stack([ga["o_norm_g"], gb_["o_norm_g"]]),
        "s5_lam_re": s5g["lam_re"], "s5_lam_im": s5g["lam_im"], "s5_log_dt": s5g["log_dt"], "s5_b_re": s5g["b_re"],
        "s5_b_im": s5g["b_im"], "s5_c_re": s5g["c_re"], "s5_c_im": s5g["c_im"], "s5_d": s5g["d"],
        "m2_dt_bias": m2g["dt_bias"], "m2_a_log": m2g["a_log"], "m2_d": m2g["d"], "final_norm_g": d_final,
    }

    def pack(d):
        flat = jnp.concatenate([d[n].reshape(-1).astype(f32) for n in _REPLICATED])
        return jnp.pad(flat, (0, -flat.shape[0] % (256 * 128))).reshape(-1, 128)

    (rep_parts,) = _gather("gather_small_grads", [pack(rep)])

    res = {}
    owned = {"mlp_w1": {0: (0, 1), 1: (1, 2), 2: (2, 4)}, "mlp_w2": {0: (0, 1), 1: (1, 2), 2: (2, 4)},
             "gdn_w_in": {0: (0, 1), 2: (1, 2)}, "gdn_w_out": {0: (0, 1), 2: (1, 2)}}

    def shard_rows(a, label, grp):
        lo_hi = owned.get(label)
        return _as2d(a if lo_hi is None else a[lo_hi[grp][0]:lo_hi[grp][1]])

    done = {}
    for (label, _, _), p in zip(early, early_parts):
        done[label] = [_adamw("adamw_" + label, p, *[shard_rows(t[label], label, 0) for t in (W, MOM, VAR)])]
    for grp in (1, 2):
        entries, sums, arrivals = landed[grp]
        for (label, _, _), arr, sm in zip(entries, arrivals, sums):
            done.setdefault(label, []).append(_adamw_own(
                "adamw_g%d_%s" % (grp, label), arr, sm, chip, *[shard_rows(t[label], label, grp) for t in (W, MOM, VAR)]))
    for n in _SHARDED:
        res[n] = [jnp.concatenate([g[k] for g in done[n]], axis=0).reshape(W[n].shape) for k in range(4)]
    out = _adamw("adamw_replicated", rep_parts, pack(W), pack(MOM), pack(VAR))
    off = 0
    for n in _REPLICATED:
        size = W[n].size
        res[n] = [o.reshape(-1)[off:off + size].reshape(W[n].shape) for o in out]
        off += size

    return (loss, grad_x, *[res[n][0] for n in _WEIGHTS], *[res[n][1] for n in _WEIGHTS],
            *[res[n][2] for n in _WEIGHTS], *[res[n][3] for n in _WEIGHTS])
```

```python
import jax
import jax.numpy as jnp
from jax import lax
from jax.experimental import pallas as pl
from jax.experimental.pallas import tpu as pltpu

f32 = jnp.float32
bf16 = jnp.bfloat16
HI = lax.Precision.HIGHEST
MESH = pl.DeviceIdType.MESH

N_DEV = 8
D_MODEL = 1024
D_FF = 4096
CHUNK = 64
RMS_EPS = 1e-6
GDN_HEADS = 8
GDN_HB = 8
GDN_EXT = 4224
S5_STATE = 64
S5_SCAN_LANES = 512
M2_INNER = 2048
M2_EXT = 6272
M2_HEADS = 32
M2_GB = 4
VMEM_LIMIT_BYTES = 56 * 1024 * 1024

ADAM_LR, ADAM_B1, ADAM_B2, ADAM_EPS, ADAM_WD, ADAM_STEP = 0.001, 0.9, 0.999, 1e-08, 0.01, 10

_NN = ((1,), (0,))
_NT = ((1,), (1,))
_TN = ((0,), (0,))


def _dot(a, b, dims=_NN):
    return lax.dot_general(a, b, (dims, ((), ())), precision=HI, preferred_element_type=f32)


def _dotb(a, b, dims=_NN):
    return lax.dot_general(a.astype(bf16), b.astype(bf16), (dims, ((), ())), preferred_element_type=f32)


def _bdot(p, q, dims):
    return lax.dot_general(p, q, (dims, ((), ())), preferred_element_type=f32)


def _pieces(x, n):
    out = []
    for _ in range(n - 1):
        p = x.astype(bf16)
        out.append(p)
        x = x - p.astype(f32)
    return out + [x.astype(bf16)]


def _dot01_raw(mask, b, dims=_NN, mask_first=True):
    m = mask.astype(bf16)
    p = _pieces(b, 3)
    if mask_first:
        return _bdot(m, p[0], dims) + (_bdot(m, p[1], dims) + _bdot(m, p[2], dims))
    return _bdot(p[0], m, dims) + (_bdot(p[1], m, dims) + _bdot(p[2], m, dims))


@jax.custom_vjp
def _dot01_nn(mask, b):
    return _dot01_raw(mask, b, _NN)


@jax.custom_vjp
def _dot01_nt(mask, b):
    return _dot01_raw(mask, b, _NT)


_dot01_nn.defvjp(lambda m, b: (_dot01_raw(m, b, _NN), m),
                 lambda m, ct: (jnp.zeros_like(m), _dot01_raw(m, ct, _TN, mask_first=True)))
_dot01_nt.defvjp(lambda m, b: (_dot01_raw(m, b, _NT), m),
                 lambda m, ct: (jnp.zeros_like(m), _dot01_raw(m, ct, _TN, mask_first=False)))


def _dot01_vjp(mask, b, dims=_NN):
    return _dot01_nn(mask, b) if dims == _NN else _dot01_nt(mask, b)


def _dot3_raw(a, b, dims=_NN):
    (ah, al), (bh, bl) = _pieces(a, 2), _pieces(b, 2)
    return _bdot(ah, bh, dims) + (_bdot(ah, bl, dims) + _bdot(al, bh, dims))


@jax.custom_vjp
def _dot3_vjp(a, b):
    return _dot3_raw(a, b)


_dot3_vjp.defvjp(lambda a, b: (_dot3_raw(a, b), (a, b)),
                 lambda res, ct: (_dot3_raw(ct, res[1], _NT), _dot3_raw(res[0], ct, _TN)))


class _Dots:
    def __init__(self, dot3, dot01):
        self.dot3, self.dot01 = dot3, dot01


_PLAIN_DOTS = _Dots(_dot3_raw, _dot01_raw)
_VJP_DOTS = _Dots(_dot3_vjp, _dot01_vjp)


def _iota(shape, dim):
    return lax.broadcasted_iota(jnp.int32, shape, dim)


def _params(n_grid):
    return pltpu.CompilerParams(dimension_semantics=("arbitrary",) * n_grid, vmem_limit_bytes=VMEM_LIMIT_BYTES)


def _row_tile(n_rows):
    return min(512, n_rows)


def _head_rows(n_rows):
    return min(2048, n_rows)


def _mm_rows(n_rows):
    return min(1024, n_rows)


def _col_tile(n, cap=1024):
    best = 128
    for t in range(128, cap + 1, 128):
        if n % t == 0:
            best = t
    return best


def _mm(name, a, b, *, dims, grid, a_spec, b_spec, out_shape, out_spec, aux=(), a_fn=None, epi_fn=None,
        acc_shape, out_init=None, cache_a=False):
    nk = grid[2]
    n_aux = len(aux)
    kinds = [x[2] for x in aux]
    cache_a = cache_a and nk == 1 and grid[1] > 1 and not any(kd == "a" for kd in kinds)

    def body_single(*refs):
        a_ref, b_ref = refs[0], refs[1]
        aux_refs = refs[2:2 + n_aux]
        pos = 2 + n_aux + (1 if out_init is not None else 0)
        o_ref = refs[pos]

        def a_tile():
            av = a_ref[...]
            if a_fn is not None:
                av = a_fn(av, *[r[...] for r, kd in zip(aux_refs, kinds) if kd == "a"])
            return av.astype(bf16)

        if cache_a:
            a_bf = refs[pos + 1]

            @pl.when(pl.program_id(1) == 0)
            def _():
                a_bf[...] = a_tile()

            av = a_bf[...]
        else:
            av = a_tile()
        r = lax.dot_general(av, b_ref[...].astype(bf16), (dims, ((), ())), preferred_element_type=f32)
        if epi_fn is not None:
            r = epi_fn(r, *[x[...] for x, kd in zip(aux_refs, kinds) if kd == "e"])
        o_ref[...] = r.astype(o_ref.dtype)

    def body(*refs):
        a_ref, b_ref = refs[0], refs[1]
        aux_refs = refs[2:2 + n_aux]
        pos = 2 + n_aux + (1 if out_init is not None else 0)
        o_ref, acc_ref = refs[pos], refs[pos + 1]
        k = pl.program_id(2)

        @pl.when(k == 0)
        def _():
            acc_ref[...] = jnp.zeros_like(acc_ref)

        av = a_ref[...]
        if a_fn is not None:
            av = a_fn(av, *[r[...] for r, kd in zip(aux_refs, kinds) if kd == "a"])
        acc_ref[...] += lax.dot_general(av.astype(bf16), b_ref[...].astype(bf16), (dims, ((), ())),
                                        preferred_element_type=f32)

        @pl.when(k == nk - 1)
        def _():
            r = acc_ref[...]
            if epi_fn is not None:
                r = epi_fn(r, *[x[...] for x, kd in zip(aux_refs, kinds) if kd == "e"])
            o_ref[...] = r.astype(o_ref.dtype)

    in_specs = [a_spec, b_spec] + [x[1] for x in aux]
    args = [a, b] + [x[0] for x in aux]
    aliases = {}
    if out_init is not None:
        in_specs.append(pl.BlockSpec(memory_space=pl.ANY))
        args.append(out_init)
        aliases = {len(args) - 1: 0}
    if nk == 1:
        a_block = tuple(d for d in a_spec.block_shape if d is not None)
        scratch = [pltpu.VMEM(a_block, bf16)] if cache_a else []
    else:
        scratch = [pltpu.VMEM(acc_shape, f32)]
    return pl.pallas_call(
        body_single if nk == 1 else body, name=name, grid=grid, in_specs=in_specs, out_specs=out_spec,
        out_shape=out_shape, scratch_shapes=scratch, input_output_aliases=aliases, compiler_params=_params(3),
    )(*args)


def _ew(name, f, ins, outs, grid):
    n_in = len(ins)
    modes = [o[4] for o in outs]

    def body(*refs):
        vals = [r[...] for r in refs[:n_in]]
        res = f(*vals)
        if not isinstance(res, (tuple, list)):
            res = (res,)
        for r, o_ref, mode in zip(res, refs[n_in:], modes):
            if mode is None:
                o_ref[...] = r.astype(o_ref.dtype)
                continue
            first = pl.program_id(1) == 0
            if mode == "all":
                first = jnp.logical_and(first, pl.program_id(0) == 0)

            @pl.when(first)
            def _(r=r, o_ref=o_ref):
                o_ref[...] = r.astype(o_ref.dtype)

            @pl.when(jnp.logical_not(first))
            def _(r=r, o_ref=o_ref):
                o_ref[...] += r.astype(o_ref.dtype)

    res = pl.pallas_call(
        body, name=name, grid=grid,
        in_specs=[pl.BlockSpec(blk, im) for _, blk, im in ins],
        out_specs=[pl.BlockSpec(o[2], o[3]) for o in outs],
        out_shape=[jax.ShapeDtypeStruct(o[0], o[1]) for o in outs],
        compiler_params=_params(2),
    )(*[a for a, _, _ in ins])
    return res


def _vjp_fn(f, n_primal):
    def g(*args):
        _, vjp = jax.vjp(f, *args[:n_primal])
        cts = args[n_primal:]
        return vjp(cts[0] if len(cts) == 1 else tuple(cts))
    return g


def _scan_fwd(name, step, n_state, state_shape, cins, ins, outs, n_units, n_chunks):
    n_c, n_in, n_out = len(cins), len(ins), len(outs)

    def body(*refs):
        c_refs = refs[:n_c]
        in_refs = refs[n_c:n_c + n_in]
        out_refs = refs[n_c + n_in:n_c + n_in + n_out]
        saved = refs[n_c + n_in + n_out:n_c + n_in + n_out + n_state]
        st = refs[n_c + n_in + n_out + n_state:]

        @pl.when(pl.program_id(1) == 0)
        def _():
            for s in st:
                s[...] = jnp.zeros_like(s)

        cur = [s[...] for s in st]
        for sv, s in zip(saved, cur):
            sv[...] = s
        new, res = step(cur, [r[...] for r in c_refs], [r[...] for r in in_refs], _PLAIN_DOTS)
        for s, n in zip(st, new):
            s[...] = n
        for o, r in zip(out_refs, res):
            o[...] = r

    sshape = (n_units, n_chunks) + state_shape
    sblock = (None, None) + state_shape
    nz = len(state_shape)
    res = pl.pallas_call(
        body, name=name, grid=(n_units, n_chunks),
        in_specs=[pl.BlockSpec(e[1], e[2]) for e in cins + ins],
        out_specs=[pl.BlockSpec(o[1], o[2]) for o in outs]
        + [pl.BlockSpec(sblock, lambda u, c: (u, c) + (0,) * nz)] * n_state,
        out_shape=[jax.ShapeDtypeStruct(o[0], f32) for o in outs]
        + [jax.ShapeDtypeStruct(sshape, f32)] * n_state,
        scratch_shapes=[pltpu.VMEM(state_shape, f32)] * n_state,
        compiler_params=_params(2),
    )(*[e[0] for e in cins + ins])
    return res[:n_out], res[n_out:]


def _scan_bwd(name, step, n_state, state_shape, cins, ins, saved, douts, n_units, n_chunks):
    n_c, n_in, n_do = len(cins), len(ins), len(douts)

    def flip(im):
        return lambda u, c: im(u, n_chunks - 1 - c)

    def body(*refs):
        p = 0
        c_refs = refs[p:p + n_c]; p += n_c
        in_refs = refs[p:p + n_in]; p += n_in
        sv_refs = refs[p:p + n_state]; p += n_state
        do_refs = refs[p:p + n_do]; p += n_do
        dc_refs = refs[p:p + n_c]; p += n_c
        di_refs = refs[p:p + n_in]; p += n_in
        dst = refs[p:]
        first = pl.program_id(1) == 0

        @pl.when(first)
        def _():
            for s in dst:
                s[...] = jnp.zeros_like(s)

        def fn(states, consts, vals):
            new, res = step(states, consts, vals, _VJP_DOTS)
            return tuple(new), tuple(res)

        prim = ([r[...] for r in sv_refs], [r[...] for r in c_refs], [r[...] for r in in_refs])
        _, vjp = jax.vjp(fn, *prim)
        d_states, d_consts, d_vals = vjp((tuple(s[...] for s in dst), tuple(r[...] for r in do_refs)))
        for s, g in zip(dst, d_states):
            s[...] = g
        for o, g in zip(di_refs, d_vals):
            o[...] = g
        for o, g in zip(dc_refs, d_consts):
            @pl.when(first)
            def _(o=o, g=g):
                o[...] = g

            @pl.when(jnp.logical_not(first))
            def _(o=o, g=g):
                o[...] += g

    nz = len(state_shape)
    sblock = (None, None) + state_shape
    def gshape(e):
        return e[3] if len(e) == 5 else e[0].shape

    def gmap(e):
        return e[4] if len(e) == 5 else e[2]

    in_specs = ([pl.BlockSpec(e[1], e[2]) for e in cins]
                + [pl.BlockSpec(e[1], flip(e[2])) for e in ins]
                + [pl.BlockSpec(sblock, lambda u, c: (u, n_chunks - 1 - c) + (0,) * nz)] * n_state
                + [pl.BlockSpec(e[1], flip(e[2])) for e in douts])
    out_specs = ([pl.BlockSpec(e[1], e[2]) for e in cins]
                 + [pl.BlockSpec(e[1], flip(gmap(e))) for e in ins])
    out_shape = [jax.ShapeDtypeStruct(gshape(e), f32) for e in cins + ins]
    res = pl.pallas_call(
        body, name=name, grid=(n_units, n_chunks), in_specs=in_specs, out_specs=out_specs, out_shape=out_shape,
        scratch_shapes=[pltpu.VMEM(state_shape, f32)] * n_state,
        compiler_params=_params(2),
    )(*([e[0] for e in cins + ins] + list(saved) + [e[0] for e in douts]))
    return res[:n_c], res[n_c:]


def _rms(x, g):
    return x * lax.rsqrt(jnp.mean(x * x, axis=-1, keepdims=True) + RMS_EPS) * g


def _rows(tm, width):
    return (tm, width), lambda r, z: (r, 0)


def _const(shape):
    return shape, lambda r, z: (0,) * len(shape)


def _rms_fwd(name, h, g):
    L, D = h.shape
    tm = _row_tile(L)
    return _ew(name, _rms, [(h, *_rows(tm, D)), (g, *_const((1, D)))],
               [((L, D), f32, *_rows(tm, D), None)], (L // tm, 1))[0]


def _rms_bwd(name, h, g, d_hn, d_res):
    L, D = h.shape
    tm = _row_tile(L)

    def f(hv, gv, dv, rv):
        dh, dg = _vjp_fn(_rms, 2)(hv, gv, dv)
        return dh + rv, dg

    return _ew(name, f, [(h, *_rows(tm, D)), (g, *_const((1, D))), (d_hn, *_rows(tm, D)), (d_res, *_rows(tm, D))],
               [((L, D), f32, *_rows(tm, D), None), ((1, D), f32, *_const((1, D)), "all")], (L // tm, 1))


def _loss_head(h, g, target):
    L, D = h.shape
    tm = _row_tile(L)

    def f(hv, gv, tv):
        def lf(a, b):
            e = jnp.square(_rms(a, b) - tv)
            return (0.5 / D) * jnp.sum(jnp.sum(e, axis=1, keepdims=True), axis=0, keepdims=True)

        val, vjp = jax.vjp(lf, hv, gv)
        dh, dg = vjp(jnp.ones((1, 1), f32))
        return jnp.broadcast_to(val, (1, 128)), dh, dg

    return _ew("loss_head", f, [(h, *_rows(tm, D)), (g, *_const((1, D))), (target, *_rows(tm, D))],
               [((1, 128), f32, *_const((1, 128)), "all"), ((L, D), f32, *_rows(tm, D), None),
                ((1, D), f32, *_const((1, D)), "all")], (L // tm, 1))


def _sqrelu(x):
    return jnp.square(jnp.maximum(x, 0.0))


def _mm_plain(name, a, b, dims, *, a_fn=None, epi_fn=None, aux=()):
    if dims == _NN:
        (M, K), N = a.shape, b.shape[1]
    elif dims == _NT:
        (M, K), N = a.shape, b.shape[0]
    else:
        (K, M), N = a.shape, b.shape[1]
    tm = _mm_rows(M)
    tn = _col_tile(N, 1536)
    tk = _col_tile(K)
    if dims == _TN:
        tk = min(512, K)
        a_spec = pl.BlockSpec((tk, tm), lambda i, j, k: (k, i))
        b_spec = pl.BlockSpec((tk, tn), lambda i, j, k: (k, j))
        a_aux = pl.BlockSpec((tk, tm), lambda i, j, k: (k, i))
    elif dims == _NT:
        a_spec = pl.BlockSpec((tm, tk), lambda i, j, k: (i, k))
        b_spec = pl.BlockSpec((tn, tk), lambda i, j, k: (j, k))
        a_aux = pl.BlockSpec((tm, tk), lambda i, j, k: (i, k))
    else:
        a_spec = pl.BlockSpec((tm, tk), lambda i, j, k: (i, k))
        b_spec = pl.BlockSpec((tk, tn), lambda i, j, k: (k, j))
        a_aux = pl.BlockSpec((tm, tk), lambda i, j, k: (i, k))
    e_aux = pl.BlockSpec((tm, tn), lambda i, j, k: (i, j))
    aux_full = [(x, a_aux if kd == "a" else e_aux, kd) for x, kd in aux]
    return _mm(name, a, b, dims=dims, grid=(M // tm, N // tn, K // tk), a_spec=a_spec, b_spec=b_spec,
               out_shape=jax.ShapeDtypeStruct((M, N), f32), out_spec=pl.BlockSpec((tm, tn), lambda i, j, k: (i, j)),
               aux=aux_full, a_fn=a_fn, epi_fn=epi_fn, acc_shape=(tm, tn), cache_a=True)


def _mlp_fwd(h, g, w1g, w2g, layer):
    L, D = h.shape
    tm = _mm_rows(L)
    fs = D_FF // N_DEV
    hn = _rms_fwd("mlp_norm", h, g)
    h1 = _mm("mlp_up", hn, w1g, dims=_NN, grid=(L // tm, N_DEV, 1),
             a_spec=pl.BlockSpec((tm, D), lambda i, j, k: (i, 0)),
             b_spec=pl.BlockSpec((None, None, D, fs), lambda i, j, k: (j, layer, 0, 0)),
             out_shape=jax.ShapeDtypeStruct((L, D_FF), f32), out_spec=pl.BlockSpec((tm, fs), lambda i, j, k: (i, j)),
             acc_shape=(tm, fs), cache_a=True)
    tn = D
    h_out = _mm("mlp_down", h1, w2g, dims=_NN, grid=(L // tm, D // tn, N_DEV),
                a_spec=pl.BlockSpec((tm, fs), lambda i, j, k: (i, k)),
                b_spec=pl.BlockSpec((None, None, fs, tn), lambda i, j, k: (k, layer, 0, j)),
                out_shape=jax.ShapeDtypeStruct((L, D), f32), out_spec=pl.BlockSpec((tm, tn), lambda i, j, k: (i, j)),
                aux=[(h, pl.BlockSpec((tm, tn), lambda i, j, k: (i, j)), "e")],
                a_fn=_sqrelu, epi_fn=lambda acc, res: acc + res, acc_shape=(tm, tn))
    return h_out, hn, h1


def _mlp_bwd(dh, h, g, hn, h1, w1g, w2g, wl, layer, dw1_buf, dw2_buf):
    L, D = h.shape
    tm = _mm_rows(L)
    fs = D_FF // N_DEV
    tk = _mm_rows(L)
    dh1 = _mm("mlp_down_dx", dh, w2g, dims=_NT, grid=(L // tm, N_DEV, 1),
              a_spec=pl.BlockSpec((tm, D), lambda i, j, k: (i, 0)),
              b_spec=pl.BlockSpec((None, None, fs, D), lambda i, j, k: (j, wl, 0, 0)),
              out_shape=jax.ShapeDtypeStruct((L, D_FF), f32), out_spec=pl.BlockSpec((tm, fs), lambda i, j, k: (i, j)),
              aux=[(h1, pl.BlockSpec((tm, fs), lambda i, j, k: (i, j)), "e")],
              epi_fn=lambda acc, pre: acc * (2.0 * jnp.maximum(pre, 0.0)), acc_shape=(tm, fs), cache_a=True)
    dw2_buf = _mm("mlp_down_dw", h1, dh, dims=_TN, grid=(N_DEV, 1, L // tk),
                  a_spec=pl.BlockSpec((tk, fs), lambda i, j, k: (k, i)),
                  b_spec=pl.BlockSpec((tk, D), lambda i, j, k: (k, 0)),
                  out_shape=jax.ShapeDtypeStruct(dw2_buf.shape, f32),
                  out_spec=pl.BlockSpec((None, None, fs, D), lambda i, j, k: (i, layer, 0, 0)),
                  a_fn=_sqrelu, acc_shape=(fs, D), out_init=dw2_buf)
    tr = D
    dw1_buf = _mm("mlp_up_dw", hn, dh1, dims=_TN, grid=(D // tr, N_DEV, L // tk),
                  a_spec=pl.BlockSpec((tk, tr), lambda i, j, k: (k, i)),
                  b_spec=pl.BlockSpec((tk, fs), lambda i, j, k: (k, j)),
                  out_shape=jax.ShapeDtypeStruct(dw1_buf.shape, f32),
                  out_spec=pl.BlockSpec((None, None, tr, fs), lambda i, j, k: (j, layer, i, 0)),
                  acc_shape=(tr, fs), out_init=dw1_buf)
    tn = D
    dhn = _mm("mlp_up_dx", dh1, w1g, dims=_NT, grid=(L // tm, D // tn, N_DEV),
              a_spec=pl.BlockSpec((tm, fs), lambda i, j, k: (i, k)),
              b_spec=pl.BlockSpec((None, None, tn, fs), lambda i, j, k: (k, wl, j, 0)),
              out_shape=jax.ShapeDtypeStruct((L, D), f32), out_spec=pl.BlockSpec((tm, tn), lambda i, j, k: (i, j)),
              acc_shape=(tm, tn))
    dh_in, dg = _rms_bwd("mlp_norm_bwd", h, g, dhn, dh)
    return dh_in, dg, dw1_buf, dw2_buf


def _shift_dn(x, s, row):
    return x if s == 0 else jnp.where(row >= s, pltpu.roll(x, s, 0), 0.0)


def _shift_up(x, s, row):
    n = x.shape[0]
    return x if s == 0 else jnp.where(row < n - s, pltpu.roll(x, n - s, 0), 0.0)


def _conv_pre(x, w, b, row):
    c = jnp.broadcast_to(b, x.shape)
    for j in range(4):
        c = c + w[j:j + 1, :] * _shift_dn(x, 3 - j, row)
    return c


def _conv_fwd(name, x_arr, blk_off, w, b):
    L = x_arr.shape[0]
    C = w.shape[1]

    def f(x, wv, bv):
        c = _conv_pre(x, wv, bv, _iota(x.shape, 0))
        return c * jax.nn.sigmoid(c)

    return _ew(name, f, [(x_arr, (L, 128), lambda j, z: (0, blk_off + j)), (w, (4, 128), lambda j, z: (0, j)),
                         (b, (1, 128), lambda j, z: (0, j))],
               [((L, C), f32, (L, 128), lambda j, z: (0, j), None)], (C // 128, 1))[0]


def _conv_bwd(name, x_arr, blk_off, w, b, dy):
    L = x_arr.shape[0]
    C = w.shape[1]

    def f(x, wv, bv, g):
        row = _iota(x.shape, 0)
        c = _conv_pre(x, wv, bv, row)
        s = jax.nn.sigmoid(c)
        dc = g * (s * (1.0 + c * (1.0 - s)))
        dx = jnp.zeros_like(x)
        dw = jnp.zeros((4, 128), f32)
        r4 = _iota((4, 128), 0)
        for j in range(4):
            dx = dx + wv[j:j + 1, :] * _shift_up(dc, 3 - j, row)
            dwj = jnp.sum(dc * _shift_dn(x, 3 - j, row), axis=0, keepdims=True)
            dw = dw + jnp.where(r4 == j, jnp.broadcast_to(dwj, (4, 128)), 0.0)
        return dx, dw, jnp.sum(dc, axis=0, keepdims=True)

    return _ew(name, f, [(x_arr, (L, 128), lambda j, z: (0, blk_off + j)), (w, (4, 128), lambda j, z: (0, j)),
                         (b, (1, 128), lambda j, z: (0, j)), (dy, (L, 128), lambda j, z: (0, j))],
               [((L, C), f32, (L, 128), lambda j, z: (0, j), None), ((4, C), f32, (4, 128), lambda j, z: (0, j), None),
                ((1, C), f32, (1, 128), lambda j, z: (0, j), None)], (C // 128, 1))


def _l2norm(t):
    return t * lax.rsqrt(jnp.sum(t * t, axis=-1, keepdims=True) + 1e-6)


def _gdn_act(cq, ck, ab, alog, dtb):
    h = pl.program_id(1)
    qn = _l2norm(cq) * (128.0 ** -0.5)
    kn = _l2norm(ck)
    lane = _iota(ab.shape, 1)
    a_raw = jnp.sum(jnp.where(lane == h, ab, 0.0), axis=1, keepdims=True)
    b_raw = jnp.sum(jnp.where(lane == h + GDN_HEADS, ab, 0.0), axis=1, keepdims=True)
    lane1 = _iota(alog.shape, 1)
    al = jnp.sum(jnp.where(lane1 == h, alog, 0.0), axis=1, keepdims=True)
    db = jnp.sum(jnp.where(lane1 == h, dtb, 0.0), axis=1, keepdims=True)
    g = -jnp.exp(al) * jax.nn.softplus(a_raw + db)
    beta = jax.nn.sigmoid(b_raw)
    return qn, kn, jnp.broadcast_to(g, cq.shape), jnp.broadcast_to(beta, cq.shape)


def _each(f, *lists):
    return [f(*a) for a in zip(*lists)]


def _gdn_chunk(states, consts, vals, dots):
    S = list(states)
    cut = [slice(128 * i, 128 * i + 128) for i in range(len(S))]
    q, k, v, gb, bb = ([t[:, c] for c in cut] for t in vals)
    C = vals[0].shape[0]
    row, col = _iota((C, C), 0), _iota((C, C), 1)
    causal, strict = row >= col, row > col
    ltri = causal.astype(f32)
    eye = (row == col).astype(f32)
    e0 = (_iota((C, 128), 1) == 0).astype(f32)
    last = _iota((C, 1), 0) == C - 1
    Gb = _each(lambda g: dots.dot01(ltri, g), gb)
    Gc = _each(lambda g: jnp.mean(g, axis=1, keepdims=True), Gb)
    Gr = _each(lambda g: dots.dot01(e0, g, _NT), Gb)
    bc = _each(lambda b: jnp.mean(b, axis=1, keepdims=True), bb)
    decay = _each(lambda gc, gr: jnp.where(causal, jnp.exp(jnp.where(causal, gc - gr, 0.0)), 0.0), Gc, Gr)
    kk = _each(lambda a: _dotb(a, a, _NT), k)
    A = _each(lambda b, x, d: jnp.where(strict, b * x * d, 0.0), bc, kk, decay)
    M = _each(lambda a: eye - a, A)
    P = _each(lambda a: dots.dot3(a, a), A)
    for it in range(5):
        M = _each(lambda m, p: m + dots.dot3(m, p), M, P)
        if it < 4:
            P = _each(lambda p: dots.dot3(p, p), P)
    eG = _each(jnp.exp, Gc)
    u = _each(lambda m, x, b: _dotb(m, x * b), M, v, bc)
    w = _each(lambda m, x, b, e: _dotb(m, x * (b * e)), M, k, bc, eG)
    qk = _each(lambda a, b, d: _dotb(a, b, _NT) * d, q, k, decay)
    g_last = _each(lambda gc: jnp.sum(jnp.where(last, gc, 0.0), axis=0, keepdims=True), Gc)
    v_new = _each(lambda a, b, s: a - _dotb(b, s), u, w, S)
    o = _each(lambda a, e, s, b, x: _dotb(a * e, s) + _dotb(b, x), q, eG, S, qk, v_new)
    S_new = _each(lambda gl, s, a, gc, x: jnp.exp(gl) * s + _dotb(a * jnp.exp(gl - gc), x, _TN), g_last, S, k, Gc, v_new)
    return S_new, [jnp.concatenate(o, axis=1)]


def _gdn_post(o, gate, g):
    return _rms(o, g) * (gate * jax.nn.sigmoid(gate))


def _pad_row(v):
    return jnp.pad(v.astype(f32), (0, 128 - v.shape[0])).reshape(1, 128)


def _gdn_fwd(h, g_norm, w_ext, conv_w, a_log, dt_bias, o_norm_g, w_out):
    L, D = h.shape
    tm = _head_rows(L)
    nc = L // CHUNK
    H = GDN_HEADS
    hn = _rms_fwd("mix_norm", h, g_norm)
    proj = _mm_plain("gdn_in", hn, w_ext, _NN)
    zb = jnp.zeros((1, 3 * D), f32)
    cq = _conv_fwd("gdn_conv", proj, 0, conv_w, zb)
    alog, dtb = _pad_row(a_log), _pad_row(dt_bias)
    act_ins = [(cq, (tm, 128), lambda r, hh: (r, hh)), (cq, (tm, 128), lambda r, hh: (r, H + hh)),
               (proj, (tm, 128), lambda r, hh: (r, 4 * H)), (alog, (1, 128), lambda r, hh: (0, 0)),
               (dtb, (1, 128), lambda r, hh: (0, 0))]
    qn, kn, gb, bb = _ew("gdn_act", _gdn_act, act_ins,
                         [((L, D), f32, (tm, 128), lambda r, hh: (r, hh), None)] * 4, (L // tm, H))
    cblk = (CHUNK, 128 * GDN_HB)
    core_ins = [(qn, cblk, lambda u, c: (c, u)), (kn, cblk, lambda u, c: (c, u)),
                (cq, cblk, lambda u, c: (c, 2 * H // GDN_HB + u), (L, D), lambda u, c: (c, u)),
                (gb, cblk, lambda u, c: (c, u)), (bb, cblk, lambda u, c: (c, u))]
    (o,), saved_s = _scan_fwd("gdn_core", _gdn_chunk, GDN_HB, (128, 128), [], core_ins,
                              [((L, D), cblk, lambda u, c: (c, u))], H // GDN_HB, nc)
    on = o_norm_g.reshape(1, 128)
    post_ins = [(o, (tm, 128), lambda r, hh: (r, hh)), (proj, (tm, 128), lambda r, hh: (r, 3 * H + hh)),
                (on, (1, 128), lambda r, hh: (0, 0))]
    y = _ew("gdn_post", _gdn_post, post_ins, [((L, D), f32, (tm, 128), lambda r, hh: (r, hh), None)], (L // tm, H))[0]
    h_out = _mm_plain("gdn_out", y, w_out, _NN, epi_fn=lambda acc, res: acc + res, aux=[(h, "e")])
    saved = dict(hn=hn, proj=proj, cq=cq, alog=alog, dtb=dtb, act_ins=act_ins, core_ins=core_ins, saved_s=saved_s,
                 post_ins=post_ins, y=y, zb=zb)
    return h_out, saved


def _gdn_bwd(dh, h, g_norm, w_ext, conv_w, w_out, sv):
    L, D = h.shape
    tm = _head_rows(L)
    nc = L // CHUNK
    H = GDN_HEADS
    dy = _mm_plain("gdn_out_dx", dh, w_out, _NT)
    dw_out = _mm_plain("gdn_out_dw", sv["y"], dh, _TN)
    hd = ((L, D), f32, (tm, 128), lambda r, hh: (r, hh), None)
    d_o, d_gate, d_on = _ew("gdn_post_bwd", _vjp_fn(_gdn_post, 3),
                            sv["post_ins"] + [(dy, (tm, 128), lambda r, hh: (r, hh))],
                            [hd, hd, ((1, 128), f32, (1, 128), lambda r, hh: (0, 0), "all")], (L // tm, H))
    cblk = (CHUNK, 128 * GDN_HB)
    _, (dqn, dkn, dv, dgb, dbb) = _scan_bwd("gdn_core_bwd", _gdn_chunk, GDN_HB, (128, 128), [], sv["core_ins"],
                                            sv["saved_s"], [(d_o, cblk, lambda u, c: (c, u))], H // GDN_HB, nc)
    cts = [(t, (tm, 128), lambda r, hh: (r, hh)) for t in (dqn, dkn, dgb, dbb)]
    row128 = ((1, 128), f32, (1, 128), lambda r, hh: (0, 0), "all")
    d_cq, d_ck, d_ab, d_alog, d_dtb = _ew(
        "gdn_act_bwd", _vjp_fn(_gdn_act, 5), sv["act_ins"] + cts,
        [hd, hd, ((L, 128), f32, (tm, 128), lambda r, hh: (r, 0), "inner"), row128, row128], (L // tm, H))
    d_conv_out = jnp.concatenate([d_cq, d_ck, dv], axis=1)
    d_conv_in, d_conv_w, _ = _conv_bwd("gdn_conv_bwd", sv["proj"], 0, conv_w, sv["zb"], d_conv_out)
    d_proj = jnp.concatenate([d_conv_in, d_gate, d_ab], axis=1)
    dw_ext = _mm_plain("gdn_in_dw", sv["hn"], d_proj, _TN)
    dhn = _mm_plain("gdn_in_dx", d_proj, w_ext, _NT)
    dh_in, dg = _rms_bwd("mix_norm_bwd", h, g_norm, dhn, dh)
    grads = dict(norm=dg, w_ext=dw_ext, conv_w=d_conv_w, a_log=d_alog[0, :H], dt_bias=d_dtb[0, :H],
                 o_norm_g=d_on[0], w_out=dw_out)
    return dh_in, grads


def _expand_lanes(row, width, rep):
    sel = ((_iota((128, width), 1) // rep) == _iota((128, width), 0)).astype(f32)
    return jnp.mean(_dot(jnp.broadcast_to(row, (8, 128)), sel), axis=0, keepdims=True)


def _s5_params(lre, lim, ldt, wbr, wbi):
    dt = jnp.exp(_expand_lanes(ldt, 512, S5_STATE))
    mag = jnp.exp(lre * dt)
    ang = lim * dt
    abr, abi = mag * jnp.cos(ang), mag * jnp.sin(ang)
    nr = abr - 1.0
    den = lre * lre + lim * lim
    cr = (nr * lre + abi * lim) / den
    ci = (abi * lre - nr * lim) / den
    return abr, abi, cr * wbr - ci * wbi, cr * wbi + ci * wbr


def _s5_scan(name, xr, xi, ar, ai, rev, want_prev):
    L, W = xr.shape
    nb = L // 8
    n_out = 4 if want_prev else 2
    lanes = S5_SCAN_LANES

    def body(xr_ref, xi_ref, ar_ref, ai_ref, *outs):
        a_r = ar_ref[...]
        a_i = -ai_ref[...] if rev else ai_ref[...]

        def cm(p, q):
            return p[0] * q[0] - p[1] * q[1], p[0] * q[1] + p[1] * q[0]

        a1 = (a_r, a_i)
        a2 = cm(a1, a1)
        a3 = cm(a2, a1)
        a4 = cm(a2, a2)
        pw = [a1, a2, a3, a4, cm(a4, a1), cm(a4, a2), cm(a4, a3), cm(a4, a4)]
        blk8 = (8, lanes)
        row = _iota(blk8, 0)
        tab_r = jnp.zeros(blk8, f32)
        tab_i = jnp.zeros(blk8, f32)
        for t in range(8):
            idx = 7 - t if rev else t
            tab_r = jnp.where(row == idx, jnp.broadcast_to(pw[t][0], blk8), tab_r)
            tab_i = jnp.where(row == idx, jnp.broadcast_to(pw[t][1], blk8), tab_i)
        lv = [(d, jnp.broadcast_to(p[0], blk8), jnp.broadcast_to(p[1], blk8)) for d, p in ((1, a1), (2, a2), (4, a4))]

        def step(i, carry):
            cr, ci = carry
            blk = nb - 1 - i if rev else i
            r0 = pl.multiple_of(blk * 8, 8)
            x_r = xr_ref[pl.ds(r0, 8), :]
            x_i = xi_ref[pl.ds(r0, 8), :]
            for d, p_r, p_i in lv:
                if rev:
                    s_r = jnp.where(row < 8 - d, pltpu.roll(x_r, 8 - d, 0), 0.0)
                    s_i = jnp.where(row < 8 - d, pltpu.roll(x_i, 8 - d, 0), 0.0)
                else:
                    s_r = jnp.where(row >= d, pltpu.roll(x_r, d, 0), 0.0)
                    s_i = jnp.where(row >= d, pltpu.roll(x_i, d, 0), 0.0)
                x_r, x_i = x_r + p_r * s_r - p_i * s_i, x_i + p_r * s_i + p_i * s_r
            x_r, x_i = x_r + tab_r * cr - tab_i * ci, x_i + tab_r * ci + tab_i * cr
            outs[0][pl.ds(r0, 8), :] = x_r
            outs[1][pl.ds(r0, 8), :] = x_i
            if want_prev:
                outs[2][pl.ds(r0, 8), :] = jnp.where(row >= 1, pltpu.roll(x_r, 1, 0), cr)
                outs[3][pl.ds(r0, 8), :] = jnp.where(row >= 1, pltpu.roll(x_i, 1, 0), ci)
            e = 0 if rev else 7
            return jnp.broadcast_to(x_r[e:e + 1, :], blk8), jnp.broadcast_to(x_i[e:e + 1, :], blk8)

        lax.fori_loop(0, nb, step, (jnp.zeros(blk8, f32), jnp.zeros(blk8, f32)))

    per = 512 // lanes
    col = pl.BlockSpec((L, lanes), lambda q, z: (0, q))
    aspec = pl.BlockSpec((None, 1, lanes), lambda q, z: (q // per, 0, q % per))
    return pl.pallas_call(
        body, name=name, grid=(W // lanes, 1), in_specs=[col, col, aspec, aspec], out_specs=[col] * n_out,
        out_shape=[jax.ShapeDtypeStruct((L, W), f32)] * n_out, compiler_params=_params(2),
    )(xr, xi, ar, ai)


def _blockdiag(t, n_in, n_out):
    t4 = t.reshape(8, 8, n_in, n_out)
    return jnp.einsum("jaio,ab->jaibo", t4, jnp.eye(8, dtype=t.dtype)).reshape(8, 8 * n_in, 8 * n_out)


def _blockdiag_t(w, n_in, n_out):
    w5 = w.reshape(8, 8, n_in, 8, n_out)
    return jnp.einsum("jaibo,ab->jaio", w5, jnp.eye(8, dtype=w.dtype)).reshape(64, n_in, n_out)


def _glu(ag, h):
    n = ag.shape[1] // 2
    return h + ag[:, :n] * jax.nn.sigmoid(ag[:, n:])


def _s5_fwd(h, g_norm, w_in, lam_re, lam_im, log_dt, b_re, b_im, c_re, c_im, d_skip, w_out_g):
    L, D = h.shape
    tm, te = _mm_rows(L), _row_tile(L)
    W = 8 * 512
    hn = _rms_fwd("mix_norm", h, g_norm)
    u = _mm_plain("s5_in", hn, w_in, _NN)
    lre, lim = lam_re.reshape(8, 1, 512), lam_im.reshape(8, 1, 512)
    ldt = jnp.pad(log_dt.reshape(8, 1, 8), ((0, 0), (0, 0), (0, 120)))
    wbr = _blockdiag(b_re.transpose(0, 2, 1), 16, 64)
    wbi = _blockdiag(b_im.transpose(0, 2, 1), 16, 64)
    wcr = _blockdiag(c_re.transpose(0, 2, 1), 64, 16)
    wci = _blockdiag(c_im.transpose(0, 2, 1), 64, 16)
    jb = lambda shape: (shape, lambda j, z: (j, 0, 0))
    par_ins = [(lre, *jb((None, 1, 512))), (lim, *jb((None, 1, 512))), (ldt, *jb((None, 1, 128))),
               (wbr, *jb((None, 128, 512))), (wbi, *jb((None, 128, 512)))]
    abr, abi, bbr, bbi = _ew("s5_params", _s5_params, par_ins,
                             [((8, 1, 512), f32, *jb((None, 1, 512)), None)] * 2
                             + [((8, 128, 512), f32, *jb((None, 128, 512)), None)] * 2, (8, 1))

    def bu(name, wb):
        return _mm(name, u, wb, dims=_NN, grid=(L // tm, 8, 1),
                   a_spec=pl.BlockSpec((tm, 128), lambda i, j, k: (i, j)),
                   b_spec=pl.BlockSpec((None, 128, 512), lambda i, j, k: (j, 0, 0)),
                   out_shape=jax.ShapeDtypeStruct((L, W), f32), out_spec=pl.BlockSpec((tm, 512), lambda i, j, k: (i, j)),
                   acc_shape=(tm, 512))

    bur, bui = bu("s5_bu", bbr), bu("s5_bu", bbi)
    sr, si, pr, pi = _s5_scan("s5_scan", bur, bui, abr, abi, False, True)
    d_row = d_skip.reshape(1, D)
    cspec = dict(a_spec=pl.BlockSpec((tm, 512), lambda i, j, k: (i, j)),
                 b_spec=pl.BlockSpec((None, 512, 128), lambda i, j, k: (j, 0, 0)),
                 out_shape=jax.ShapeDtypeStruct((L, D), f32), out_spec=pl.BlockSpec((tm, 128), lambda i, j, k: (i, j)),
                 acc_shape=(tm, 128))
    e128 = pl.BlockSpec((tm, 128), lambda i, j, k: (i, j))
    pre1 = _mm("s5_c_re", sr, wcr, dims=_NN, grid=(L // tm, 8, 1), **cspec)
    pre = _mm("s5_c_im", si, wci, dims=_NN, grid=(L // tm, 8, 1),
              aux=[(pre1, e128, "e"), (u, e128, "e"), (d_row, pl.BlockSpec((1, 128), lambda i, j, k: (0, j)), "e")],
              epi_fn=lambda acc, p1, uu, dd: p1 - acc + dd * uu, **cspec)
    ws = D // N_DEV * 2
    ag = _mm("s5_out", pre, w_out_g, dims=_NN, grid=(L // tm, N_DEV, 1),
             a_spec=pl.BlockSpec((tm, D), lambda i, j, k: (i, 0)),
             b_spec=pl.BlockSpec((None, None, D, ws), lambda i, j, k: (j, 0, 0, 0)),
             out_shape=jax.ShapeDtypeStruct((L, 2 * D), f32), out_spec=pl.BlockSpec((tm, ws), lambda i, j, k: (i, j)),
             a_fn=jax.nn.gelu, acc_shape=(tm, ws), cache_a=True)
    h_out = _ew("s5_glu", _glu, [(ag, *_rows(te, 2 * D)), (h, *_rows(te, D))],
                [((L, D), f32, *_rows(te, D), None)], (L // te, 1))[0]
    saved = dict(hn=hn, u=u, par_ins=par_ins, abr=abr, abi=abi, bbr=bbr, bbi=bbi, wcr=wcr, wci=wci, sr=sr, si=si,
                 pr=pr, pi=pi, pre=pre, ag=ag, d_row=d_row)
    return h_out, saved


def _s5_bwd(dh, h, g_norm, w_in, w_out_g, sv):
    L, D = h.shape
    tm, te = _mm_rows(L), _row_tile(L)
    tk = min(512, L)
    W = 8 * 512
    ws = D // N_DEV * 2
    u, pre, d_row = sv["u"], sv["pre"], sv["d_row"]
    d_ag = _ew("s5_glu_bwd", lambda ag, hv, g: _vjp_fn(_glu, 2)(ag, hv, g)[0],
               [(sv["ag"], *_rows(te, 2 * D)), (h, *_rows(te, D)), (dh, *_rows(te, D))],
               [((L, 2 * D), f32, *_rows(te, 2 * D), None)], (L // te, 1))[0]
    tr = D
    dw_out = _mm("s5_out_dw", pre, d_ag, dims=_TN, grid=(D // tr, N_DEV, L // tk),
                 a_spec=pl.BlockSpec((tk, tr), lambda i, j, k: (k, i)),
                 b_spec=pl.BlockSpec((tk, ws), lambda i, j, k: (k, j)),
                 out_shape=jax.ShapeDtypeStruct((N_DEV, 1, D, ws), f32),
                 out_spec=pl.BlockSpec((None, None, tr, ws), lambda i, j, k: (j, 0, i, 0)),
                 a_fn=jax.nn.gelu, acc_shape=(tr, ws))
    tn = 512
    dpre = _mm("s5_out_dx", d_ag, w_out_g, dims=_NT, grid=(L // tm, D // tn, N_DEV),
               a_spec=pl.BlockSpec((tm, ws), lambda i, j, k: (i, k)),
               b_spec=pl.BlockSpec((None, None, tn, ws), lambda i, j, k: (k, 0, j, 0)),
               out_shape=jax.ShapeDtypeStruct((L, D), f32), out_spec=pl.BlockSpec((tm, tn), lambda i, j, k: (i, j)),
               aux=[(pre, pl.BlockSpec((tm, tn), lambda i, j, k: (i, j)), "e")],
               epi_fn=lambda acc, p: _vjp_fn(jax.nn.gelu, 1)(p, acc)[0], acc_shape=(tm, tn))
    d_d = _ew("s5_dskip", lambda a, b: jnp.sum(a * b, axis=0, keepdims=True),
              [(dpre, *_rows(te, D)), (u, *_rows(te, D))], [((1, D), f32, *_const((1, D)), "all")], (L // te, 1))[0]
    neg = lambda acc: -acc
    dsspec = dict(dims=_NT, grid=(L // tm, 8, 1), a_spec=pl.BlockSpec((tm, 128), lambda i, j, k: (i, j)),
                  b_spec=pl.BlockSpec((None, 512, 128), lambda i, j, k: (j, 0, 0)),
                  out_shape=jax.ShapeDtypeStruct((L, W), f32), out_spec=pl.BlockSpec((tm, 512), lambda i, j, k: (i, j)),
                  acc_shape=(tm, 512))
    dsr = _mm("s5_c_re_dx", dpre, sv["wcr"], **dsspec)
    dsi = _mm("s5_c_im_dx", dpre, sv["wci"], epi_fn=neg, **dsspec)
    dwspec = dict(dims=_TN, grid=(8, 1, L // tk), a_spec=pl.BlockSpec((tk, 512), lambda i, j, k: (k, i)),
                  b_spec=pl.BlockSpec((tk, 128), lambda i, j, k: (k, i)),
                  out_shape=jax.ShapeDtypeStruct((8, 512, 128), f32),
                  out_spec=pl.BlockSpec((None, 512, 128), lambda i, j, k: (i, 0, 0)), acc_shape=(512, 128))
    dwcr = _mm("s5_c_re_dw", sv["sr"], dpre, **dwspec)
    dwci = _mm("s5_c_im_dw", sv["si"], dpre, epi_fn=neg, **dwspec)
    lr, li = _s5_scan("s5_scan_bwd", dsr, dsi, sv["abr"], sv["abi"], True, False)

    def da(lrv, liv, prv, piv):
        return (jnp.sum(lrv * prv + liv * piv, axis=0, keepdims=True),
                jnp.sum(liv * prv - lrv * piv, axis=0, keepdims=True))

    sblk = ((te, 512), lambda j, r: (r, j))
    dabr, dabi = _ew("s5_dlam", da, [(lr, *sblk), (li, *sblk), (sv["pr"], *sblk), (sv["pi"], *sblk)],
                     [((8, 1, 512), f32, (None, 1, 512), lambda j, r: (j, 0, 0), "inner")] * 2, (8, L // te))
    dbspec = dict(dims=_TN, grid=(8, 1, L // tk), a_spec=pl.BlockSpec((tk, 128), lambda i, j, k: (k, i)),
                  b_spec=pl.BlockSpec((tk, 512), lambda i, j, k: (k, i)),
                  out_shape=jax.ShapeDtypeStruct((8, 128, 512), f32),
                  out_spec=pl.BlockSpec((None, 128, 512), lambda i, j, k: (i, 0, 0)), acc_shape=(128, 512))
    dbbr = _mm("s5_bu_dw", u, lr, **dbspec)
    dbbi = _mm("s5_bu_dw", u, li, **dbspec)
    duspec = dict(dims=_NT, grid=(L // tm, 8, 1), a_spec=pl.BlockSpec((tm, 512), lambda i, j, k: (i, j)),
                  b_spec=pl.BlockSpec((None, 128, 512), lambda i, j, k: (j, 0, 0)),
                  out_shape=jax.ShapeDtypeStruct((L, D), f32), out_spec=pl.BlockSpec((tm, 128), lambda i, j, k: (i, j)),
                  acc_shape=(tm, 128))
    e128 = pl.BlockSpec((tm, 128), lambda i, j, k: (i, j))
    du1 = _mm("s5_bu_dx_re", lr, sv["bbr"], **duspec)
    du = _mm("s5_bu_dx_im", li, sv["bbi"],
             aux=[(du1, e128, "e"), (dpre, e128, "e"), (d_row, pl.BlockSpec((1, 128), lambda i, j, k: (0, j)), "e")],
             epi_fn=lambda acc, d1, dp, dd: acc + d1 + dp * dd, **duspec)
    jb = lambda shape: (shape, lambda j, z: (j, 0, 0))
    cts = [(dabr, *jb((None, 1, 512))), (dabi, *jb((None, 1, 512))), (dbbr, *jb((None, 128, 512))),
           (dbbi, *jb((None, 128, 512)))]
    dlre, dlim, dldt, dwbr, dwbi = _ew(
        "s5_params_bwd", _vjp_fn(_s5_params, 5), sv["par_ins"] + cts,
        [((8, 1, 512), f32, *jb((None, 1, 512)), None)] * 2 + [((8, 1, 128), f32, *jb((None, 1, 128)), None)]
        + [((8, 128, 512), f32, *jb((None, 128, 512)), None)] * 2, (8, 1))
    dw_in = _mm_plain("s5_in_dw", sv["hn"], du, _TN)
    dhn = _mm_plain("s5_in_dx", du, w_in, _NT)
    dh_in, dg = _rms_bwd("mix_norm_bwd", h, g_norm, dhn, dh)
    grads = dict(norm=dg, w_in=dw_in, lam_re=dlre.reshape(64, 64), lam_im=dlim.reshape(64, 64),
                 log_dt=dldt[:, 0, :8].reshape(64),
                 b_re=_blockdiag_t(dwbr, 16, 64).transpose(0, 2, 1), b_im=_blockdiag_t(dwbi, 16, 64).transpose(0, 2, 1),
                 c_re=_blockdiag_t(dwcr, 64, 16).transpose(0, 2, 1), c_im=_blockdiag_t(dwci, 64, 16).transpose(0, 2, 1),
                 d=d_d[0], w_out=dw_out)
    return dh_in, grads


def _m2_act(dt_raw, dtbias, alog):
    dt = jax.nn.softplus(dt_raw + dtbias)
    da = dt * (-jnp.exp(alog))
    sel = ((_iota((128, M2_INNER), 1) // 64) == _iota((128, M2_INNER), 0)).astype(f32)
    return _dot(dt, sel), _dot(da, sel)


def _m2_dexp(d):
    return _expand_lanes(d, M2_INNER, 64)


def _ssd_chunk(states, consts, vals, dots):
    (dsk,) = consts
    S = list(states)
    n = len(S)
    cut = [slice(128 * i, 128 * i + 128) for i in range(n)]
    x, dtb, dab = ([t[:, c] for c in cut] for t in vals[:3])
    dsk = [dsk[:, c] for c in cut]
    B = [vals[3][:, cut[i // 2]] for i in range(n)]
    Cm = [vals[4][:, cut[i // 2]] for i in range(n)]
    C = vals[0].shape[0]
    row, col = _iota((C, C), 0), _iota((C, C), 1)
    causal = row >= col
    ltri = causal.astype(f32)
    lane = _iota((C, 128), 1)
    last = _iota((C, 128), 0) == C - 1
    eye128 = _iota((128, 128), 0) == _iota((128, 128), 1)
    head = [jnp.logical_and(lane >= 64 * hh, lane < 64 * hh + 64) for hh in range(2)]
    pick = [(lane == 64 * hh).astype(f32) for hh in range(2)]
    xdt = _each(lambda a, b: a * b, x, dtb)
    cb = _each(lambda c, b: _dotb(c, b, _NT), Cm[::2], B[::2])
    cum = _each(lambda a: dots.dot01(ltri, a), dab)
    clast = _each(lambda a: jnp.sum(jnp.where(last, a, 0.0), axis=0, keepdims=True), cum)
    st = _each(lambda a, cl, cu, b: _dotb(a * jnp.exp(cl - cu), b, _TN), xdt, clast, cum, B)
    y = _each(lambda c, s, cu: _dotb(c, s, _NT) * jnp.exp(cu), Cm, S, cum)
    for hh in range(2):
        ccol = _each(lambda cu: jnp.sum(jnp.where(head[hh], cu, 0.0), axis=1, keepdims=True) * (1.0 / 64), cum)
        crow = _each(lambda cu: dots.dot01(pick[hh], cu, _NT), cum)
        lm = _each(lambda a, b: jnp.where(causal, jnp.exp(jnp.where(causal, a - b, 0.0)), 0.0), ccol, crow)
        y = [y[i] + _dotb(cb[i // 2] * lm[i], jnp.where(head[hh], xdt[i], 0.0)) for i in range(n)]
    cdcol = _each(lambda cl: jnp.sum(jnp.where(eye128, jnp.broadcast_to(jnp.exp(cl), (128, 128)), 0.0),
                                     axis=1, keepdims=True), clast)
    S_new = _each(lambda c, s, t: c * s + t, cdcol, S, st)
    out = _each(lambda a, d, b: a + d * b, y, dsk, x)
    return S_new, [jnp.concatenate(out, axis=1)]


def _m2_post(yc, z, ng):
    return _rms(yc * (z * jax.nn.sigmoid(z)), ng)


def _m2_fwd(h, g_norm, w_ext, conv_w, conv_b, dt_bias, a_log, d_skip, norm_g, w_out):
    L, D = h.shape
    tm = _row_tile(L)
    nc = L // CHUNK
    NI = M2_INNER
    hn = _rms_fwd("mix_norm", h, g_norm)
    proj = _mm_plain("m2_in", hn, w_ext, _NN)
    xbc = _conv_fwd("m2_conv", proj, NI // 128, conv_w, conv_b)
    dtb_row, alog_row, d_pad = _pad_row(dt_bias), _pad_row(a_log), _pad_row(d_skip)
    act_ins = [(proj, (tm, 128), lambda r, z: (r, 3 * NI // 128)), (dtb_row, *_const((1, 128))),
               (alog_row, *_const((1, 128)))]
    dtb, dab = _ew("m2_act", _m2_act, act_ins, [((L, NI), f32, *_rows(tm, NI), None)] * 2, (L // tm, 1))
    dsk = _ew("m2_dexp", _m2_dexp, [(d_pad, *_const((1, 128)))], [((1, NI), f32, *_const((1, NI)), None)], (1, 1))[0]
    GB = M2_GB
    x_blk, bc_blk = (CHUNK, 256 * GB), (CHUNK, 128 * GB)
    cins = [(dsk, (1, 256 * GB), lambda u, c: (0, u))]
    core_ins = [(xbc, x_blk, lambda u, c: (c, u), (L, NI), lambda u, c: (c, u)),
                (dtb, x_blk, lambda u, c: (c, u)), (dab, x_blk, lambda u, c: (c, u)),
                (xbc, bc_blk, lambda u, c: (c, 16 // GB + u), (L, D), lambda u, c: (c, u)),
                (xbc, bc_blk, lambda u, c: (c, 24 // GB + u), (L, D), lambda u, c: (c, u))]
    (yc,), saved_s = _scan_fwd("m2_core", _ssd_chunk, 2 * GB, (128, 128), cins, core_ins,
                               [((L, NI), x_blk, lambda u, c: (c, u))], 8 // GB, nc)
    tp = _head_rows(L)
    gblk = ((tp, 256), lambda g, r: (r, g))
    post_ins = [(yc, *gblk), (proj, *gblk), (norm_g, (1, 256), lambda g, r: (0, g))]
    yn = _ew("m2_post", _m2_post, post_ins, [((L, NI), f32, *gblk, None)], (8, L // tp))[0]
    h_out = _mm_plain("m2_out", yn, w_out, _NN, epi_fn=lambda acc, res: acc + res, aux=[(h, "e")])
    saved = dict(hn=hn, proj=proj, act_ins=act_ins, d_pad=d_pad, cins=cins, core_ins=core_ins, saved_s=saved_s,
                 post_ins=post_ins, yn=yn)
    return h_out, saved


def _m2_bwd(dh, h, g_norm, w_ext, conv_w, conv_b, w_out, sv):
    L, D = h.shape
    tm = _row_tile(L)
    nc = L // CHUNK
    NI = M2_INNER
    dyn = _mm_plain("m2_out_dx", dh, w_out, _NT)
    dw_out = _mm_plain("m2_out_dw", sv["yn"], dh, _TN)
    tp = _head_rows(L)
    gblk = ((tp, 256), lambda g, r: (r, g))
    d_yc, d_z, d_ng = _ew("m2_post_bwd", _vjp_fn(_m2_post, 3), sv["post_ins"] + [(dyn, *gblk)],
                          [((L, NI), f32, *gblk, None)] * 2 + [((1, NI), f32, (1, 256), lambda g, r: (0, g), "inner")],
                          (8, L // tp))
    (d_dsk,), (dx, d_dtb, d_dab, dB, dC) = _scan_bwd(
        "m2_core_bwd", _ssd_chunk, 2 * M2_GB, (128, 128), sv["cins"], sv["core_ins"], sv["saved_s"],
        [(d_yc, (CHUNK, 256 * M2_GB), lambda u, c: (c, u))], 8 // M2_GB, nc)
    row128 = ((1, 128), f32, *_const((1, 128)), "all")
    d_dt_raw, d_dtbias, d_alog = _ew(
        "m2_act_bwd", _vjp_fn(_m2_act, 3), sv["act_ins"] + [(d_dtb, *_rows(tm, NI)), (d_dab, *_rows(tm, NI))],
        [((L, 128), f32, *_rows(tm, 128), None), row128, row128], (L // tm, 1))
    d_d = _ew("m2_dexp_bwd", _vjp_fn(_m2_dexp, 1), [(sv["d_pad"], *_const((1, 128))), (d_dsk, *_const((1, NI)))],
              [((1, 128), f32, *_const((1, 128)), None)], (1, 1))[0]
    d_conv_out = jnp.concatenate([dx, dB, dC], axis=1)
    d_conv_in, d_conv_w, d_conv_b = _conv_bwd("m2_conv_bwd", sv["proj"], NI // 128, conv_w, conv_b, d_conv_out)
    d_proj = jnp.concatenate([d_z, d_conv_in, d_dt_raw], axis=1)
    dw_ext = _mm_plain("m2_in_dw", sv["hn"], d_proj, _TN)
    dhn = _mm_plain("m2_in_dx", d_proj, w_ext, _NT)
    dh_in, dg = _rms_bwd("mix_norm_bwd", h, g_norm, dhn, dh)
    grads = dict(norm=dg, w_ext=dw_ext, conv_w=d_conv_w, conv_b=d_conv_b, dt_bias=d_dtbias[0, :M2_HEADS],
                 a_log=d_alog[0, :M2_HEADS], d=d_d[0, :M2_HEADS], norm_g=d_ng, w_out=dw_out)
    return dh_in, grads


def _mesh_pos():
    return lax.axis_index("x"), lax.axis_index("y"), lax.axis_index("c")


def _flip(pos, p):
    x, y, c = pos
    return (1 - x if p & 4 else x, 1 - y if p & 2 else y, 1 - c if p & 1 else c)


def _index(pos):
    return 4 * pos[0] + 2 * pos[1] + pos[2]


def _comm_call(name, body, arrays, out_shape, n_sem):
    n = len(arrays)
    hbm = pl.BlockSpec(memory_space=pl.ANY)
    return pl.pallas_call(
        body, name=name, in_specs=[hbm] * n, out_specs=[hbm] * len(out_shape), out_shape=out_shape,
        scratch_shapes=[pltpu.SemaphoreType.DMA((n, n_sem)), pltpu.SemaphoreType.DMA((n, n_sem)),
                        pltpu.SemaphoreType.DMA((n, 4))],
    )(*arrays)


def _gather(name, arrays):
    n = len(arrays)

    def body(*refs):
        ins, outs = refs[:n], refs[n:2 * n]
        send_sems, recv_sems, loc_sems = refs[2 * n:]
        me = _mesh_pos()
        c = me[2]
        sib = _flip(me, 1)
        chips = [_flip(me, 4), _flip(me, 2), _flip(me, 6)]

        def copy(w, k, block, to, src=None):
            slab = outs[w].at[_index(block)]
            return pltpu.make_async_remote_copy(
                src_ref=slab if src is None else src, dst_ref=slab, send_sem=send_sems.at[w, k],
                recv_sem=recv_sems.at[w, k], device_id=to, device_id_type=MESH)

        local = [pltpu.make_async_copy(ins[w], outs[w].at[_index(me)], loc_sems.at[w, 0]) for w in range(n)]
        for cp in local:
            cp.start()
        first = [copy(w, 0, me, sib, src=ins[w]) for w in range(n)]
        first += [copy(w, 1 + j, me, chip, src=ins[w]) for j, chip in enumerate(chips) for w in range(n)]
        for cp in first:
            cp.start()
        passed = []
        for j, chip in enumerate(chips):
            for w in range(n):
                copy(w, 1 + j, chip, me).wait_recv()
                fwd = copy(w, 4 + j, chip, sib)
                fwd.start()
                passed.append(fwd)
        for w in range(n):
            copy(w, 0, sib, me).wait_recv()
        for j, chip in enumerate(chips):
            for w in range(n):
                copy(w, 4 + j, (chip[0], chip[1], 1 - c), me).wait_recv()
        for cp in first + passed:
            cp.wait_send()
        for cp in local:
            cp.wait()

    out_shape = [jax.ShapeDtypeStruct((N_DEV,) + a.shape, a.dtype) for a in arrays]
    return _comm_call(name, body, arrays, out_shape, N_DEV - 1)


_HBM = pl.BlockSpec(memory_space=pltpu.HBM)
_SEM = pl.BlockSpec(memory_space=pltpu.SEMAPHORE)
_SPLIT_COPIES = 4


def _split_targets(me):
    return [_flip(me, 1), _flip(me, 4), _flip(me, 2), _flip(me, 6)]


def _gather_start(name, arrays, lands):
    n = len(arrays)
    ns = n * _SPLIT_COPIES

    def body(*refs):
        ins, land = refs[:n], refs[n:2 * n]
        send_sems, recv_sems = refs[2 * n:2 * n + ns], refs[2 * n + ns:2 * n + 2 * ns]
        token = refs[4 * n + 2 * ns]
        me = _mesh_pos()
        for w in range(n):
            for k, to in enumerate(_split_targets(me)):
                pltpu.make_async_remote_copy(
                    src_ref=ins[w], dst_ref=land[w].at[_index(me)], send_sem=send_sems[w * _SPLIT_COPIES + k],
                    recv_sem=recv_sems[w * _SPLIT_COPIES + k], device_id=to, device_id_type=MESH).start()
        token[...] = jnp.zeros_like(token)

    sem = pltpu.SemaphoreType.DMA(())
    res = pl.pallas_call(
        body, name=name,
        out_shape=(*[sem] * (2 * ns), *[pltpu.HBM(a.shape, a.dtype) for a in arrays],
                   *[pltpu.HBM(a.shape, a.dtype) for a in lands], jax.ShapeDtypeStruct((8, 128), f32)),
        in_specs=[_HBM] * (2 * n),
        out_specs=(*[_SEM] * (2 * ns), *[_HBM] * (2 * n), pl.BlockSpec(memory_space=pltpu.VMEM)),
        input_output_aliases={i: 2 * ns + i for i in range(2 * n)},
        compiler_params=pltpu.CompilerParams(has_side_effects=pltpu.SideEffectType.DATAFLOW_SIDE_EFFECTING),
    )(*[pltpu.with_memory_space_constraint(a, pltpu.HBM) for a in list(arrays) + list(lands)])
    sems, rest = res[:2 * ns], res[2 * ns:]
    return sems[:ns], sems[ns:], rest[:n], rest[n:2 * n], rest[2 * n]


def _gather_wait(name, arrays, lands, send_sems, recv_sems, after):
    n = len(arrays)
    ns = n * _SPLIT_COPIES

    def body(*refs):
        ins, land = refs[:n], refs[n:2 * n]
        s_sems, r_sems = refs[2 * n:2 * n + ns], refs[2 * n + ns:2 * n + 2 * ns]
        me = _mesh_pos()
        for w in range(n):
            for k, peer in enumerate(_split_targets(me)):
                cp = pltpu.make_async_remote_copy(
                    src_ref=ins[w], dst_ref=land[w].at[_index(peer)], send_sem=s_sems[w * _SPLIT_COPIES + k],
                    recv_sem=r_sems[w * _SPLIT_COPIES + k], device_id=peer, device_id_type=MESH)
                cp.wait_send()
                cp.wait_recv()

    res = pl.pallas_call(
        body, name=name,
        out_shape=(*[pltpu.HBM(a.shape, a.dtype) for a in arrays], *[pltpu.HBM(a.shape, a.dtype) for a in lands]),
        in_specs=[_HBM] * (2 * n) + [_SEM] * (2 * ns) + [pl.BlockSpec(memory_space=pl.ANY)],
        out_specs=tuple([_HBM] * (2 * n)), input_output_aliases={i: i for i in range(2 * n)},
        compiler_params=pltpu.CompilerParams(has_side_effects=pltpu.SideEffectType.DATAFLOW_SIDE_EFFECTING),
    )(*arrays, *lands, *send_sems, *recv_sems, after)
    return res[n:]


def _gather_forward(name, lands):
    n = len(lands)

    def body(*refs):
        outs = refs[n:2 * n]
        send_sems, recv_sems, _ = refs[2 * n:]
        me = _mesh_pos()
        sib = _flip(me, 1)
        held = [_flip(me, 4), _flip(me, 2), _flip(me, 6), sib]

        def copy(w, j, block):
            slab = outs[w].at[_index(block)]
            return pltpu.make_async_remote_copy(src_ref=slab, dst_ref=slab, send_sem=send_sems.at[w, j],
                                                recv_sem=recv_sems.at[w, j], device_id=sib, device_id_type=MESH)

        sends = [copy(w, j, blk) for j, blk in enumerate(held) for w in range(n)]
        for cp in sends:
            cp.start()
        for j, blk in enumerate(held):
            for w in range(n):
                copy(w, j, (blk[0], blk[1], 1 - blk[2])).wait_recv()
        for cp in sends:
            cp.wait_send()

    hbm = pl.BlockSpec(memory_space=pl.ANY)
    return pl.pallas_call(
        body, name=name, in_specs=[hbm] * n, out_specs=[hbm] * n,
        out_shape=[jax.ShapeDtypeStruct(a.shape, a.dtype) for a in lands],
        input_output_aliases={i: i for i in range(n)},
        scratch_shapes=[pltpu.SemaphoreType.DMA((n, 4)), pltpu.SemaphoreType.DMA((n, 4)), pltpu.SemaphoreType.DMA((n, 4))],
    )(*lands)


def _scatter_pair(name, arrays):
    n = len(arrays)

    def body(*refs):
        ins, outs = refs[:n], refs[n:2 * n]
        send_sems, recv_sems, _ = refs[2 * n:]
        me = _mesh_pos()
        c = me[2]
        sib = _flip(me, 1)

        def copy(w, q):
            return pltpu.make_async_remote_copy(
                src_ref=ins[w].at[2 * q + 1 - c], dst_ref=outs[w].at[q], send_sem=send_sems.at[w, q],
                recv_sem=recv_sems.at[w, q], device_id=sib, device_id_type=MESH)

        cps = [copy(w, q) for q in range(4) for w in range(n)]
        for cp in cps:
            cp.start()
        for cp in cps:
            cp.wait()

    out_shape = [jax.ShapeDtypeStruct((4,) + a.shape[1:], a.dtype) for a in arrays]
    return _comm_call(name, body, arrays, out_shape, 4)


def _pair_add(name, full, theirs, core, dtype):
    _, R, C = theirs.shape
    tr = R if R <= 256 else (256 if C <= 512 else 128)

    def body(core_ref, mine_ref, theirs_ref, o_ref):
        o_ref[...] = (mine_ref[...] + theirs_ref[...]).astype(o_ref.dtype)

    blk = pl.BlockSpec((4, tr, C), lambda r, cr: (0, r, 0))
    grid_spec = pltpu.PrefetchScalarGridSpec(
        num_scalar_prefetch=1, grid=(R // tr,),
        in_specs=[pl.BlockSpec((4, None, tr, C), lambda r, cr: (0, cr[0], r, 0)), blk], out_specs=blk)
    return pl.pallas_call(
        body, name=name, grid_spec=grid_spec, out_shape=jax.ShapeDtypeStruct((4, R, C), dtype),
        compiler_params=_params(1),
    )(core.reshape(1).astype(jnp.int32), full.reshape(4, 2, R, C), theirs)


def _scatter_chips(name, arrays):
    n = len(arrays)

    def body(*refs):
        ins, outs = refs[:n], refs[n:2 * n]
        send_sems, recv_sems, loc_sems = refs[2 * n:]
        me = _mesh_pos()
        mq = 2 * me[0] + me[1]
        peers = [_flip(me, 4), _flip(me, 2), _flip(me, 6)]

        def copy(w, k):
            peer = peers[k]
            return pltpu.make_async_remote_copy(
                src_ref=ins[w].at[2 * peer[0] + peer[1]], dst_ref=outs[w].at[mq], send_sem=send_sems.at[w, k],
                recv_sem=recv_sems.at[w, k], device_id=peer, device_id_type=MESH)

        def arrival(w, k):
            peer = peers[k]
            return pltpu.make_async_remote_copy(
                src_ref=ins[w].at[mq], dst_ref=outs[w].at[2 * peer[0] + peer[1]], send_sem=send_sems.at[w, k],
                recv_sem=recv_sems.at[w, k], device_id=peer, device_id_type=MESH)

        local = [pltpu.make_async_copy(ins[w].at[mq], outs[w].at[mq], loc_sems.at[w, 0]) for w in range(n)]
        for cp in local:
            cp.start()
        sends = [copy(w, k) for k in range(3) for w in range(n)]
        for cp in sends:
            cp.start()
        for k in range(3):
            for w in range(n):
                arrival(w, k).wait_recv()
        for cp in sends:
            cp.wait_send()
        for cp in local:
            cp.wait()

    out_shape = [jax.ShapeDtypeStruct(a.shape, a.dtype) for a in arrays]
    return _comm_call(name, body, arrays, out_shape, 3)


def _chip_peers(me):
    return [_flip(me, 4), _flip(me, 2), _flip(me, 6)]


def _chips_start(name, arrays):
    n = len(arrays)
    ns = 3 * n

    def body(*refs):
        ins, land = refs[:n], refs[n:2 * n]
        send_sems, recv_sems = refs[2 * n:2 * n + ns], refs[2 * n + ns:2 * n + 2 * ns]
        token = refs[4 * n + 2 * ns]
        me = _mesh_pos()
        for w in range(n):
            for k, peer in enumerate(_chip_peers(me)):
                pltpu.make_async_remote_copy(
                    src_ref=ins[w].at[2 * peer[0] + peer[1]], dst_ref=land[w].at[k], send_sem=send_sems[3 * w + k],
                    recv_sem=recv_sems[3 * w + k], device_id=peer, device_id_type=MESH).start()
        token[...] = jnp.zeros_like(token)

    lands = [lax.empty((3,) + a.shape[1:], a.dtype) for a in arrays]
    sem = pltpu.SemaphoreType.DMA(())
    res = pl.pallas_call(
        body, name=name,
        out_shape=(*[sem] * (2 * ns), *[pltpu.HBM(a.shape, a.dtype) for a in arrays],
                   *[pltpu.HBM(a.shape, a.dtype) for a in lands], jax.ShapeDtypeStruct((8, 128), f32)),
        in_specs=[_HBM] * (2 * n),
        out_specs=(*[_SEM] * (2 * ns), *[_HBM] * (2 * n), pl.BlockSpec(memory_space=pltpu.VMEM)),
        input_output_aliases={i: 2 * ns + i for i in range(2 * n)},
        compiler_params=pltpu.CompilerParams(has_side_effects=pltpu.SideEffectType.DATAFLOW_SIDE_EFFECTING),
    )(*[pltpu.with_memory_space_constraint(a, pltpu.HBM) for a in list(arrays) + lands])
    sems, rest = res[:2 * ns], res[2 * ns:]
    return sems[:ns], sems[ns:], rest[:n], rest[n:2 * n], rest[2 * n]


def _chips_wait(name, arrays, lands, send_sems, recv_sems, after):
    n = len(arrays)
    ns = 3 * n

    def body(*refs):
        ins, land = refs[:n], refs[n:2 * n]
        s_sems, r_sems = refs[2 * n:2 * n + ns], refs[2 * n + ns:2 * n + 2 * ns]
        me = _mesh_pos()
        for w in range(n):
            for k, peer in enumerate(_chip_peers(me)):
                cp = pltpu.make_async_remote_copy(
                    src_ref=ins[w].at[2 * peer[0] + peer[1]], dst_ref=land[w].at[k], send_sem=s_sems[3 * w + k],
                    recv_sem=r_sems[3 * w + k], device_id=peer, device_id_type=MESH)
                cp.wait_send()
                cp.wait_recv()

    res = pl.pallas_call(
        body, name=name,
        out_shape=(*[pltpu.HBM(a.shape, a.dtype) for a in arrays], *[pltpu.HBM(a.shape, a.dtype) for a in lands]),
        in_specs=[_HBM] * (2 * n) + [_SEM] * (2 * ns) + [pl.BlockSpec(memory_space=pl.ANY)],
        out_specs=tuple([_HBM] * (2 * n)), input_output_aliases={i: i for i in range(2 * n)},
        compiler_params=pltpu.CompilerParams(has_side_effects=pltpu.SideEffectType.DATAFLOW_SIDE_EFFECTING),
    )(*arrays, *lands, *send_sems, *recv_sems, after)
    return res[:n], res[n:]


def _adamw_own(name, arrivals, sums, chip, w, m, v):
    R, C = w.shape
    tr = R if R <= 256 else (256 if C <= 512 else 128)
    bc1 = 1.0 - ADAM_B1 ** ADAM_STEP
    bc2 = 1.0 - ADAM_B2 ** ADAM_STEP

    def body(chip_ref, own_ref, p_ref, w_ref, m_ref, v_ref, g_ref, d_ref, m2_ref, v2_ref):
        g = own_ref[...].astype(f32)
        for k in range(3):
            g = g + p_ref[k].astype(f32)
        m2 = ADAM_B1 * m_ref[...] + (1.0 - ADAM_B1) * g
        v2 = ADAM_B2 * v_ref[...] + (1.0 - ADAM_B2) * jnp.square(g)
        g_ref[...] = g
        d_ref[...] = -ADAM_LR * ((m2 / bc1) / (jnp.sqrt(v2 / bc2) + ADAM_EPS) + ADAM_WD * w_ref[...])
        m2_ref[...] = m2
        v2_ref[...] = v2

    blk = pl.BlockSpec((tr, C), lambda r, cr: (r, 0))
    grid_spec = pltpu.PrefetchScalarGridSpec(
        num_scalar_prefetch=1, grid=(R // tr,),
        in_specs=[pl.BlockSpec((None, tr, C), lambda r, cr: (cr[0], r, 0)),
                  pl.BlockSpec((3, tr, C), lambda r, cr: (0, r, 0)), blk, blk, blk],
        out_specs=[blk] * 4)
    return pl.pallas_call(
        body, name=name, grid_spec=grid_spec, out_shape=[jax.ShapeDtypeStruct((R, C), f32)] * 4,
        compiler_params=_params(1),
    )(chip.reshape(1).astype(jnp.int32), sums, arrivals, w, m, v)


def _adamw(name, parts, w, m, v):
    R, C = w.shape
    n_parts = parts.shape[0]
    tr = R if R <= 256 else (256 if C <= 512 else 128)
    bc1 = 1.0 - ADAM_B1 ** ADAM_STEP
    bc2 = 1.0 - ADAM_B2 ** ADAM_STEP

    def f(p, wv, mv, vv):
        g = p[0].astype(f32)
        for i in range(1, n_parts):
            g = g + p[i].astype(f32)
        m2 = ADAM_B1 * mv + (1.0 - ADAM_B1) * g
        v2 = ADAM_B2 * vv + (1.0 - ADAM_B2) * jnp.square(g)
        delta = -ADAM_LR * ((m2 / bc1) / (jnp.sqrt(v2 / bc2) + ADAM_EPS) + ADAM_WD * wv)
        return g, delta, m2, v2

    blk = ((tr, C), lambda r, z: (r, 0))
    return _ew(name, f, [(parts, (n_parts, tr, C), lambda r, z: (0, r, 0)), (w, *blk), (m, *blk), (v, *blk)],
               [((R, C), f32, *blk, None)] * 4, (R // tr, 1))


_WEIGHTS = ["norm_mix_g", "norm_mlp_g", "mlp_w1", "mlp_w2", "gdn_w_in", "gdn_conv_w", "gdn_a_log", "gdn_dt_bias",
            "gdn_o_norm_g", "gdn_w_out", "s5_w_in", "s5_lam_re", "s5_lam_im", "s5_log_dt", "s5_b_re", "s5_b_im",
            "s5_c_re", "s5_c_im", "s5_d", "s5_w_out", "m2_w_in", "m2_conv_w", "m2_conv_b", "m2_dt_bias", "m2_a_log",
            "m2_d", "m2_norm_g", "m2_w_out", "final_norm_g"]
_SHARDED = ["mlp_w1", "mlp_w2", "gdn_w_in", "gdn_w_out", "s5_w_in", "s5_w_out", "m2_w_in", "m2_w_out",
            "gdn_conv_w", "m2_conv_w", "m2_conv_b", "m2_norm_g"]
_REPLICATED = [n for n in _WEIGHTS if n not in _SHARDED]
_GDN_IN, _M2_IN = 4112, 6176
_LAYER_KIND = (0, 1, 2, 0)


def _as2d(a):
    return a.reshape(-1, a.shape[-1])


def _cols_from_shards(g, width):
    return g.transpose(1, 0, 2).reshape(g.shape[1], width)


def _cols_to_shards(a, width):
    return a[:, :width].reshape(a.shape[0], N_DEV, width // N_DEV).transpose(1, 0, 2)


def kernel(x, norm_mix_g, norm_mlp_g, mlp_w1, mlp_w2, gdn_w_in, gdn_conv_w, gdn_a_log, gdn_dt_bias, gdn_o_norm_g, gdn_w_out, s5_w_in, s5_lam_re, s5_lam_im, s5_log_dt, s5_b_re, s5_b_im, s5_c_re, s5_c_im, s5_d, s5_w_out, m2_w_in, m2_conv_w, m2_conv_b, m2_dt_bias, m2_a_log, m2_d, m2_norm_g, m2_w_out, final_norm_g, loss_target, m_norm_mix_g, m_norm_mlp_g, m_mlp_w1, m_mlp_w2, m_gdn_w_in, m_gdn_conv_w, m_gdn_a_log, m_gdn_dt_bias, m_gdn_o_norm_g, m_gdn_w_out, m_s5_w_in, m_s5_lam_re, m_s5_lam_im, m_s5_log_dt, m_s5_b_re, m_s5_b_im, m_s5_c_re, m_s5_c_im, m_s5_d, m_s5_w_out, m_m2_w_in, m_m2_conv_w, m_m2_conv_b, m_m2_dt_bias, m_m2_a_log, m_m2_d, m_m2_norm_g, m_m2_w_out, m_final_norm_g, v_norm_mix_g, v_norm_mlp_g, v_mlp_w1, v_mlp_w2, v_gdn_w_in, v_gdn_conv_w, v_gdn_a_log, v_gdn_dt_bias, v_gdn_o_norm_g, v_gdn_w_out, v_s5_w_in, v_s5_lam_re, v_s5_lam_im, v_s5_log_dt, v_s5_b_re, v_s5_b_im, v_s5_c_re, v_s5_c_im, v_s5_d, v_s5_w_out, v_m2_w_in, v_m2_conv_w, v_m2_conv_b, v_m2_dt_bias, v_m2_a_log, v_m2_d, v_m2_norm_g, v_m2_w_out, v_final_norm_g):
    args = locals()
    W = {n: args[n] for n in _WEIGHTS}
    MOM = {n: args["m_" + n] for n in _WEIGHTS}
    VAR = {n: args["v_" + n] for n in _WEIGHTS}
    h = x[0]
    target = loss_target[0]
    L, D = h.shape

    first = [mlp_w1[0:1].astype(bf16), mlp_w2[0:1].astype(bf16), gdn_w_in[0:1].astype(bf16),
             gdn_w_out[0:1].astype(bf16), _as2d(gdn_conv_w), _as2d(m2_conv_w), _as2d(m2_conv_b), _as2d(m2_norm_g)]
    w1g0, w2g0, gin0, gout0, gconv, m2_cw, m2_cbg, m2_ngg = _gather("gather_first", first)
    stacked = jnp.concatenate([gdn_w_out[1], s5_w_in[0], m2_w_out[0]], axis=0).astype(bf16)
    rest = [mlp_w1[1:4].astype(bf16), mlp_w2[1:4].astype(bf16), gdn_w_in[1:2].astype(bf16), s5_w_out.astype(bf16),
            m2_w_in.astype(bf16), stacked]
    lands = [lax.empty((N_DEV,) + a.shape, a.dtype) for a in rest]
    send_sems, recv_sems, rest_thru, lands_thru, token = _gather_start("gather_rest_start", rest, lands)

    def gdn_weights(gin, gout, conv, j):
        return (jnp.pad(_cols_from_shards(gin[:, 0], _GDN_IN), ((0, 0), (0, GDN_EXT - _GDN_IN))),
                gout[:, 0].reshape(D, D), _cols_from_shards(conv[:, 4 * j:4 * j + 4], 3 * D))

    gdn_in, gdn_out, gdn_conv = [None, None], [None, None], [None, None]
    gdn_in[0], gdn_out[0], gdn_conv[0] = gdn_weights(gin0, gout0, gconv, 0)

    norm_mix = [norm_mix_g[i].reshape(1, D) for i in range(4)]
    norm_mix[0] = norm_mix[0] + token[0, 0]
    late = {}

    def mixer_fwd(i, hv):
        kind, j = _LAYER_KIND[i], i // 3
        gn = norm_mix[i]
        s5_in, s5_out_g = late.get("s5_in"), late.get("s5_out_g")
        m2_in, m2_conv, m2_cb, m2_ng, m2_out = (late.get(k) for k in ("m2_in", "m2_conv", "m2_cb", "m2_ng", "m2_out"))
        if kind == 0:
            return _gdn_fwd(hv, gn, gdn_in[j], gdn_conv[j], gdn_a_log[j], gdn_dt_bias[j], gdn_o_norm_g[j], gdn_out[j])
        if kind == 1:
            return _s5_fwd(hv, gn, s5_in, s5_lam_re[0], s5_lam_im[0], s5_log_dt[0], s5_b_re[0], s5_b_im[0],
                           s5_c_re[0], s5_c_im[0], s5_d[0], s5_out_g)
        return _m2_fwd(hv, gn, m2_in, m2_conv, m2_cb, m2_dt_bias[0], m2_a_log[0], m2_d[0], m2_ng, m2_out)

    def mixer_bwd(i, dh, hv, sv):
        kind, j = _LAYER_KIND[i], i // 3
        gn = norm_mix[i]
        s5_in, s5_out_g = late["s5_in"], late["s5_out_g"]
        m2_in, m2_conv, m2_cb, m2_out = (late[k] for k in ("m2_in", "m2_conv", "m2_cb", "m2_out"))
        if kind == 0:
            return _gdn_bwd(dh, hv, gn, gdn_in[j], gdn_conv[j], gdn_out[j], sv)
        if kind == 1:
            return _s5_bwd(dh, hv, gn, s5_in, s5_out_g, sv)
        return _m2_bwd(dh, hv, gn, m2_in, m2_conv, m2_cb, m2_out, sv)

    tape = []
    mlp_w = [(w1g0, w2g0, 0)]
    for i in range(4):
        if i == 1:
            landed = _gather_wait("gather_rest_wait", rest_thru, lands_thru, send_sems, recv_sems, h)
            w1gr, w2gr, gin1, s5_out_g, m2_in_g, rows_g = _gather_forward("gather_rest_forward", landed)
            gout1, s5_in_g, m2_out_g = rows_g[:, None, 0:128], rows_g[:, 128:256], rows_g[:, 256:512]
            mlp_w += [(w1gr, w2gr, k) for k in range(3)]
            gdn_in[1], gdn_out[1], gdn_conv[1] = gdn_weights(gin1, gout1, gconv, 1)
            late.update(
                s5_in=s5_in_g.reshape(D, D), s5_out_g=s5_out_g,
                m2_in=jnp.pad(_cols_from_shards(m2_in_g[:, 0], _M2_IN), ((0, 0), (0, M2_EXT - _M2_IN))),
                m2_out=m2_out_g.reshape(M2_INNER, D), m2_conv=_cols_from_shards(m2_cw, 2 * M2_INNER),
                m2_cb=_cols_from_shards(m2_cbg, 2 * M2_INNER), m2_ng=_cols_from_shards(m2_ngg, M2_INNER))
        h_mid, sv = mixer_fwd(i, h)
        h_next, hn, h1 = _mlp_fwd(h_mid, norm_mlp_g[i].reshape(1, D), *mlp_w[i])
        tape.append((h, sv, h_mid, hn, h1))
        h = h_next
    loss_row, dh, d_final = _loss_head(h, final_norm_g.reshape(1, D), target)
    loss = lax.psum(loss_row[0, 0], ("x", "y", "c"))

    fs = D_FF // N_DEV
    group_layers = (1, 1, 2)
    dw1 = [lax.empty((N_DEV, nl, D, fs), f32) for nl in group_layers]
    dw2 = [lax.empty((N_DEV, nl, fs, D), f32) for nl in group_layers]
    d_mix, d_mlp, mg = [None] * 4, [None] * 4, [None] * 4
    core = lax.axis_index("c")
    chip = 2 * lax.axis_index("x") + lax.axis_index("y")

    def pair_sums(tag, entries):
        theirs = _scatter_pair("scatter_pair_" + tag, [e[1] for e in entries])
        return [_pair_add("pair_add_%s_%s" % (tag, e[0]), e[1], th, core, bf16 if e[2] else f32)
                for e, th in zip(entries, theirs)]

    def mlp_grads(grp):
        nl = group_layers[grp]
        return [("mlp_w1", dw1[grp].reshape(N_DEV, nl * D, fs), True), ("mlp_w2", dw2[grp].reshape(N_DEV, nl * fs, D), True)]

    def gdn_grads(g):
        return [("gdn_w_in", _cols_to_shards(g["w_ext"], _GDN_IN), True),
                ("gdn_w_out", g["w_out"].reshape(N_DEV, D // N_DEV, D), True)]

    flying = {}
    for i in reversed(range(4)):
        h_in, sv, h_mid, hn, h1 = tape[i]
        grp, slot = min(i, 2), max(i - 2, 0)
        dh, d_mlp[i], dw1[grp], dw2[grp] = _mlp_bwd(dh, h_mid, norm_mlp_g[i].reshape(1, D), hn, h1, *mlp_w[i], slot,
                                                    dw1[grp], dw2[grp])
        dh, mg[i] = mixer_bwd(i, dh, h_in, sv)
        d_mix[i] = mg[i]["norm"]
        if i in (2, 1):
            if i == 2:
                entries = mlp_grads(2) + gdn_grads(mg[3]) + [
                    ("m2_w_in", _cols_to_shards(mg[2]["w_ext"], _M2_IN), True),
                    ("m2_w_out", mg[2]["w_out"].reshape(N_DEV, M2_INNER // N_DEV, D), True)]
            else:
                entries = mlp_grads(1) + [("s5_w_in", mg[1]["w_in"].reshape(N_DEV, D // N_DEV, D), True),
                                          ("s5_w_out", mg[1]["w_out"].reshape(N_DEV, D, 2 * D // N_DEV), True)]
            started = _chips_start("chips_start_%d" % grp, pair_sums("g%d" % grp, entries))
            flying[grp] = (entries, started)
            dh = dh + started[4][0, 0]
    grad_x = dh.reshape(1, L, D)
    ga, gb_, s5g, m2g = mg[0], mg[3], mg[1], mg[2]
    early = mlp_grads(0) + gdn_grads(ga) + [
        ("gdn_conv_w", jnp.concatenate([_cols_to_shards(g["conv_w"], 3 * D) for g in (ga, gb_)], axis=1), False),
        ("m2_conv_w", _cols_to_shards(m2g["conv_w"], 2 * M2_INNER), False),
        ("m2_conv_b", _cols_to_shards(m2g["conv_b"], 2 * M2_INNER), False),
        ("m2_norm_g", _cols_to_shards(m2g["norm_g"], M2_INNER), False)]
    early_parts = _scatter_chips("scatter_chips", pair_sums("g0", early))
    landed = {}
    for grp in (1, 2):
        entries, (s_sems, r_sems, sums, lands, _) = flying[grp]
        landed[grp] = (entries,) + tuple(_chips_wait("chips_wait_%d" % grp, sums, lands, s_sems, r_sems, early_parts[0]))

    rep = {
        "norm_mix_g": jnp.concatenate(d_mix, axis=0), "norm_mlp_g": jnp.concatenate(d_mlp, axis=0),
        "gdn_a_log": jnp.stack([ga["a_log"], gb_["a_log"]]), "gdn_dt_bias": jnp.stack([ga["dt_bias"], gb_["dt_bias"]]),
        "gdn_o_norm_g": jnp.stack([ga["o_norm_g"], gb_["o_norm_g"]]),
        "s5_lam_re": s5g["lam_re"], "s5_lam_im": s5g["lam_im"], "s5_log_dt": s5g["log_dt"], "s5_b_re": s5g["b_re"],
        "s5_b_im": s5g["b_im"], "s5_c_re": s5g["c_re"], "s5_c_im": s5g["c_im"], "s5_d": s5g["d"],
        "m2_dt_bias": m2g["dt_bias"], "m2_a_log": m2g["a_log"], "m2_d": m2g["d"], "final_norm_g": d_final,
    }

    def pack(d):
        flat = jnp.concatenate([d[n].reshape(-1).astype(f32) for n in _REPLICATED])
        return jnp.pad(flat, (0, -flat.shape[0] % (256 * 128))).reshape(-1, 128)

    (rep_parts,) = _gather("gather_small_grads", [pack(rep)])

    res = {}
    owned = {"mlp_w1": {0: (0, 1), 1: (1, 2), 2: (2, 4)}, "mlp_w2": {0: (0, 1), 1: (1, 2), 2: (2, 4)},
             "gdn_w_in": {0: (0, 1), 2: (1, 2)}, "gdn_w_out": {0: (0, 1), 2: (1, 2)}}

    def shard_rows(a, label, grp):
        lo_hi = owned.get(label)
        return _as2d(a if lo_hi is None else a[lo_hi[grp][0]:lo_hi[grp][1]])

    done = {}
    for (label, _, _), p in zip(early, early_parts):
        done[label] = [_adamw("adamw_" + label, p, *[shard_rows(t[label], label, 0) for t in (W, MOM, VAR)])]
    for grp in (1, 2):
        entries, sums, arrivals = landed[grp]
        for (label, _, _), arr, sm in zip(entries, arrivals, sums):
            done.setdefault(label, []).append(_adamw_own(
                "adamw_g%d_%s" % (grp, label), arr, sm, chip, *[shard_rows(t[label], label, grp) for t in (W, MOM, VAR)]))
    for n in _SHARDED:
        res[n] = [jnp.concatenate([g[k] for g in done[n]], axis=0).reshape(W[n].shape) for k in range(4)]
    out = _adamw("adamw_replicated", rep_parts, pack(W), pack(MOM), pack(VAR))
    off = 0
    for n in _REPLICATED:
        size = W[n].size
        res[n] = [o.reshape(-1)[off:off + size].reshape(W[n].shape) for o in out]
        off += size

    return (loss, grad_x, *[res[n][0] for n in _WEIGHTS], *[res[n][1] for n in _WEIGHTS],
            *[res[n][2] for n in _WEIGHTS], *[res[n][3] for n in _WEIGHTS])
```

```python
import jax
import jax.numpy as jnp
from jax import lax
from jax.experimental import pallas as pl
from jax.experimental.pallas import tpu as pltpu

f32 = jnp.float32
bf16 = jnp.bfloat16
HI = lax.Precision.HIGHEST
MESH = pl.DeviceIdType.MESH

N_DEV = 8
D_MODEL = 1024
D_FF = 4096
CHUNK = 64
RMS_EPS = 1e-6
GDN_HEADS = 8
GDN_HB = 8
GDN_EXT = 4224
S5_STATE = 64
S5_SCAN_LANES = 512
M2_INNER = 2048
M2_EXT = 6272
M2_HEADS = 32
M2_GB = 4
VMEM_LIMIT_BYTES = 56 * 1024 * 1024

ADAM_LR, ADAM_B1, ADAM_B2, ADAM_EPS, ADAM_WD, ADAM_STEP = 0.001, 0.9, 0.999, 1e-08, 0.01, 10

_NN = ((1,), (0,))
_NT = ((1,), (1,))
_TN = ((0,), (0,))


def _dot(a, b, dims=_NN):
    return lax.dot_general(a, b, (dims, ((), ())), precision=HI, preferred_element_type=f32)


def _dotb(a, b, dims=_NN):
    return lax.dot_general(a.astype(bf16), b.astype(bf16), (dims, ((), ())), preferred_element_type=f32)


def _bdot(p, q, dims):
    return lax.dot_general(p, q, (dims, ((), ())), preferred_element_type=f32)


def _pieces(x, n):
    out = []
    for _ in range(n - 1):
        p = x.astype(bf16)
        out.append(p)
        x = x - p.astype(f32)
    return out + [x.astype(bf16)]


def _dot01_raw(mask, b, dims=_NN, mask_first=True):
    m = mask.astype(bf16)
    p = _pieces(b, 3)
    if mask_first:
        return _bdot(m, p[0], dims) + (_bdot(m, p[1], dims) + _bdot(m, p[2], dims))
    return _bdot(p[0], m, dims) + (_bdot(p[1], m, dims) + _bdot(p[2], m, dims))


@jax.custom_vjp
def _dot01_nn(mask, b):
    return _dot01_raw(mask, b, _NN)


@jax.custom_vjp
def _dot01_nt(mask, b):
    return _dot01_raw(mask, b, _NT)


_dot01_nn.defvjp(lambda m, b: (_dot01_raw(m, b, _NN), m),
                 lambda m, ct: (jnp.zeros_like(m), _dot01_raw(m, ct, _TN, mask_first=True)))
_dot01_nt.defvjp(lambda m, b: (_dot01_raw(m, b, _NT), m),
                 lambda m, ct: (jnp.zeros_like(m), _dot01_raw(m, ct, _TN, mask_first=False)))


def _dot01_vjp(mask, b, dims=_NN):
    return _dot01_nn(mask, b) if dims == _NN else _dot01_nt(mask, b)


def _dot3_raw(a, b, dims=_NN):
    (ah, al), (bh, bl) = _pieces(a, 2), _pieces(b, 2)
    return _bdot(ah, bh, dims) + (_bdot(ah, bl, dims) + _bdot(al, bh, dims))


@jax.custom_vjp
def _dot3_vjp(a, b):
    return _dot3_raw(a, b)


_dot3_vjp.defvjp(lambda a, b: (_dot3_raw(a, b), (a, b)),
                 lambda res, ct: (_dot3_raw(ct, res[1], _NT), _dot3_raw(res[0], ct, _TN)))


class _Dots:
    def __init__(self, dot3, dot01):
        self.dot3, self.dot01 = dot3, dot01


_PLAIN_DOTS = _Dots(_dot3_raw, _dot01_raw)
_VJP_DOTS = _Dots(_dot3_vjp, _dot01_vjp)


def _iota(shape, dim):
    return lax.broadcasted_iota(jnp.int32, shape, dim)


def _params(n_grid):
    return pltpu.CompilerParams(dimension_semantics=("arbitrary",) * n_grid, vmem_limit_bytes=VMEM_LIMIT_BYTES)


def _row_tile(n_rows):
    return min(512, n_rows)


def _head_rows(n_rows):
    return min(2048, n_rows)


def _mm_rows(n_rows):
    return min(1024, n_rows)


def _col_tile(n, cap=1024):
    best = 128
    for t in range(128, cap + 1, 128):
        if n % t == 0:
            best = t
    return best


def _mm(name, a, b, *, dims, grid, a_spec, b_spec, out_shape, out_spec, aux=(), a_fn=None, epi_fn=None,
        acc_shape, out_init=None, cache_a=False):
    nk = grid[2]
    n_aux = len(aux)
    kinds = [x[2] for x in aux]
    cache_a = cache_a and nk == 1 and grid[1] > 1 and not any(kd == "a" for kd in kinds)

    def body_single(*refs):
        a_ref, b_ref = refs[0], refs[1]
        aux_refs = refs[2:2 + n_aux]
        pos = 2 + n_aux + (1 if out_init is not None else 0)
        o_ref = refs[pos]

        def a_tile():
            av = a_ref[...]
            if a_fn is not None:
                av = a_fn(av, *[r[...] for r, kd in zip(aux_refs, kinds) if kd == "a"])
            return av.astype(bf16)

        if cache_a:
            a_bf = refs[pos + 1]

            @pl.when(pl.program_id(1) == 0)
            def _():
                a_bf[...] = a_tile()

            av = a_bf[...]
        else:
            av = a_tile()
        r = lax.dot_general(av, b_ref[...].astype(bf16), (dims, ((), ())), preferred_element_type=f32)
        if epi_fn is not None:
            r = epi_fn(r, *[x[...] for x, kd in zip(aux_refs, kinds) if kd == "e"])
        o_ref[...] = r.astype(o_ref.dtype)

    def body(*refs):
        a_ref, b_ref = refs[0], refs[1]
        aux_refs = refs[2:2 + n_aux]
        pos = 2 + n_aux + (1 if out_init is not None else 0)
        o_ref, acc_ref = refs[pos], refs[pos + 1]
        k = pl.program_id(2)

        @pl.when(k == 0)
        def _():
            acc_ref[...] = jnp.zeros_like(acc_ref)

        av = a_ref[...]
        if a_fn is not None:
            av = a_fn(av, *[r[...] for r, kd in zip(aux_refs, kinds) if kd == "a"])
        acc_ref[...] += lax.dot_general(av.astype(bf16), b_ref[...].astype(bf16), (dims, ((), ())),
                                        preferred_element_type=f32)

        @pl.when(k == nk - 1)
        def _():
            r = acc_ref[...]
            if epi_fn is not None:
                r = epi_fn(r, *[x[...] for x, kd in zip(aux_refs, kinds) if kd == "e"])
            o_ref[...] = r.astype(o_ref.dtype)

    in_specs = [a_spec, b_spec] + [x[1] for x in aux]
    args = [a, b] + [x[0] for x in aux]
    aliases = {}
    if out_init is not None:
        in_specs.append(pl.BlockSpec(memory_space=pl.ANY))
        args.append(out_init)
        aliases = {len(args) - 1: 0}
    if nk == 1:
        a_block = tuple(d for d in a_spec.block_shape if d is not None)
        scratch = [pltpu.VMEM(a_block, bf16)] if cache_a else []
    else:
        scratch = [pltpu.VMEM(acc_shape, f32)]
    return pl.pallas_call(
        body_single if nk == 1 else body, name=name, grid=grid, in_specs=in_specs, out_specs=out_spec,
        out_shape=out_shape, scratch_shapes=scratch, input_output_aliases=aliases, compiler_params=_params(3),
    )(*args)


def _ew(name, f, ins, outs, grid):
    n_in = len(ins)
    modes = [o[4] for o in outs]

    def body(*refs):
        vals = [r[...] for r in refs[:n_in]]
        res = f(*vals)
        if not isinstance(res, (tuple, list)):
            res = (res,)
        for r, o_ref, mode in zip(res, refs[n_in:], modes):
            if mode is None:
                o_ref[...] = r.astype(o_ref.dtype)
                continue
            first = pl.program_id(1) == 0
            if mode == "all":
                first = jnp.logical_and(first, pl.program_id(0) == 0)

            @pl.when(first)
            def _(r=r, o_ref=o_ref):
                o_ref[...] = r.astype(o_ref.dtype)

            @pl.when(jnp.logical_not(first))
            def _(r=r, o_ref=o_ref):
                o_ref[...] += r.astype(o_ref.dtype)

    res = pl.pallas_call(
        body, name=name, grid=grid,
        in_specs=[pl.BlockSpec(blk, im) for _, blk, im in ins],
        out_specs=[pl.BlockSpec(o[2], o[3]) for o in outs],
        out_shape=[jax.ShapeDtypeStruct(o[0], o[1]) for o in outs],
        compiler_params=_params(2),
    )(*[a for a, _, _ in ins])
    return res


def _vjp_fn(f, n_primal):
    def g(*args):
        _, vjp = jax.vjp(f, *args[:n_primal])
        cts = args[n_primal:]
        return vjp(cts[0] if len(cts) == 1 else tuple(cts))
    return g


def _scan_fwd(name, step, n_state, state_shape, cins, ins, outs, n_units, n_chunks):
    n_c, n_in, n_out = len(cins), len(ins), len(outs)

    def body(*refs):
        c_refs = refs[:n_c]
        in_refs = refs[n_c:n_c + n_in]
        out_refs = refs[n_c + n_in:n_c + n_in + n_out]
        saved = refs[n_c + n_in + n_out:n_c + n_in + n_out + n_state]
        st = refs[n_c + n_in + n_out + n_state:]

        @pl.when(pl.program_id(1) == 0)
        def _():
            for s in st:
                s[...] = jnp.zeros_like(s)

        cur = [s[...] for s in st]
        for sv, s in zip(saved, cur):
            sv[...] = s
        new, res = step(cur, [r[...] for r in c_refs], [r[...] for r in in_refs], _PLAIN_DOTS)
        for s, n in zip(st, new):
            s[...] = n
        for o, r in zip(out_refs, res):
            o[...] = r

    sshape = (n_units, n_chunks) + state_shape
    sblock = (None, None) + state_shape
    nz = len(state_shape)
    res = pl.pallas_call(
        body, name=name, grid=(n_units, n_chunks),
        in_specs=[pl.BlockSpec(e[1], e[2]) for e in cins + ins],
        out_specs=[pl.BlockSpec(o[1], o[2]) for o in outs]
        + [pl.BlockSpec(sblock, lambda u, c: (u, c) + (0,) * nz)] * n_state,
        out_shape=[jax.ShapeDtypeStruct(o[0], f32) for o in outs]
        + [jax.ShapeDtypeStruct(sshape, f32)] * n_state,
        scratch_shapes=[pltpu.VMEM(state_shape, f32)] * n_state,
        compiler_params=_params(2),
    )(*[e[0] for e in cins + ins])
    return res[:n_out], res[n_out:]


def _scan_bwd(name, step, n_state, state_shape, cins, ins, saved, douts, n_units, n_chunks):
    n_c, n_in, n_do = len(cins), len(ins), len(douts)

    def flip(im):
        return lambda u, c: im(u, n_chunks - 1 - c)

    def body(*refs):
        p = 0
        c_refs = refs[p:p + n_c]; p += n_c
        in_refs = refs[p:p + n_in]; p += n_in
        sv_refs = refs[p:p + n_state]; p += n_state
        do_refs = refs[p:p + n_do]; p += n_do
        dc_refs = refs[p:p + n_c]; p += n_c
        di_refs = refs[p:p + n_in]; p += n_in
        dst = refs[p:]
        first = pl.program_id(1) == 0

        @pl.when(first)
        def _():
            for s in dst:
                s[...] = jnp.zeros_like(s)

        def fn(states, consts, vals):
            new, res = step(states, consts, vals, _VJP_DOTS)
            return tuple(new), tuple(res)

        prim = ([r[...] for r in sv_refs], [r[...] for r in c_refs], [r[...] for r in in_refs])
        _, vjp = jax.vjp(fn, *prim)
        d_states, d_consts, d_vals = vjp((tuple(s[...] for s in dst), tuple(r[...] for r in do_refs)))
        for s, g in zip(dst, d_states):
            s[...] = g
        for o, g in zip(di_refs, d_vals):
            o[...] = g
        for o, g in zip(dc_refs, d_consts):
            @pl.when(first)
            def _(o=o, g=g):
                o[...] = g

            @pl.when(jnp.logical_not(first))
            def _(o=o, g=g):
                o[...] += g

    nz = len(state_shape)
    sblock = (None, None) + state_shape
    def gshape(e):
        return e[3] if len(e) == 5 else e[0].shape

    def gmap(e):
        return e[4] if len(e) == 5 else e[2]

    in_specs = ([pl.BlockSpec(e[1], e[2]) for e in cins]
                + [pl.BlockSpec(e[1], flip(e[2])) for e in ins]
                + [pl.BlockSpec(sblock, lambda u, c: (u, n_chunks - 1 - c) + (0,) * nz)] * n_state
                + [pl.BlockSpec(e[1], flip(e[2])) for e in douts])
    out_specs = ([pl.BlockSpec(e[1], e[2]) for e in cins]
                 + [pl.BlockSpec(e[1], flip(gmap(e))) for e in ins])
    out_shape = [jax.ShapeDtypeStruct(gshape(e), f32) for e in cins + ins]
    res = pl.pallas_call(
        body, name=name, grid=(n_units, n_chunks), in_specs=in_specs, out_specs=out_specs, out_shape=out_shape,
        scratch_shapes=[pltpu.VMEM(state_shape, f32)] * n_state,
        compiler_params=_params(2),
    )(*([e[0] for e in cins + ins] + list(saved) + [e[0] for e in douts]))
    return res[:n_c], res[n_c:]


def _rms(x, g):
    return x * lax.rsqrt(jnp.mean(x * x, axis=-1, keepdims=True) + RMS_EPS) * g


def _rows(tm, width):
    return (tm, width), lambda r, z: (r, 0)


def _const(shape):
    return shape, lambda r, z: (0,) * len(shape)


def _rms_fwd(name, h, g):
    L, D = h.shape
    tm = _row_tile(L)
    return _ew(name, _rms, [(h, *_rows(tm, D)), (g, *_const((1, D)))],
               [((L, D), f32, *_rows(tm, D), None)], (L // tm, 1))[0]


def _rms_bwd(name, h, g, d_hn, d_res):
    L, D = h.shape
    tm = _row_tile(L)

    def f(hv, gv, dv, rv):
        dh, dg = _vjp_fn(_rms, 2)(hv, gv, dv)
        return dh + rv, dg

    return _ew(name, f, [(h, *_rows(tm, D)), (g, *_const((1, D))), (d_hn, *_rows(tm, D)), (d_res, *_rows(tm, D))],
               [((L, D), f32, *_rows(tm, D), None), ((1, D), f32, *_const((1, D)), "all")], (L // tm, 1))


def _loss_head(h, g, target):
    L, D = h.shape
    tm = _row_tile(L)

    def f(hv, gv, tv):
        def lf(a, b):
            e = jnp.square(_rms(a, b) - tv)
            return (0.5 / D) * jnp.sum(jnp.sum(e, axis=1, keepdims=True), axis=0, keepdims=True)

        val, vjp = jax.vjp(lf, hv, gv)
        dh, dg = vjp(jnp.ones((1, 1), f32))
        return jnp.broadcast_to(val, (1, 128)), dh, dg

    return _ew("loss_head", f, [(h, *_rows(tm, D)), (g, *_const((1, D))), (target, *_rows(tm, D))],
               [((1, 128), f32, *_const((1, 128)), "all"), ((L, D), f32, *_rows(tm, D), None),
                ((1, D), f32, *_const((1, D)), "all")], (L // tm, 1))


def _sqrelu(x):
    return jnp.square(jnp.maximum(x, 0.0))


def _mm_plain(name, a, b, dims, *, a_fn=None, epi_fn=None, aux=()):
    if dims == _NN:
        (M, K), N = a.shape, b.shape[1]
    elif dims == _NT:
        (M, K), N = a.shape, b.shape[0]
    else:
        (K, M), N = a.shape, b.shape[1]
    tm = _mm_rows(M)
    tn = _col_tile(N, 1536)
    tk = _col_tile(K)
    if dims == _TN:
        tk = min(512, K)
        a_spec = pl.BlockSpec((tk, tm), lambda i, j, k: (k, i))
        b_spec = pl.BlockSpec((tk, tn), lambda i, j, k: (k, j))
        a_aux = pl.BlockSpec((tk, tm), lambda i, j, k: (k, i))
    elif dims == _NT:
        a_spec = pl.BlockSpec((tm, tk), lambda i, j, k: (i, k))
        b_spec = pl.BlockSpec((tn, tk), lambda i, j, k: (j, k))
        a_aux = pl.BlockSpec((tm, tk), lambda i, j, k: (i, k))
    else:
        a_spec = pl.BlockSpec((tm, tk), lambda i, j, k: (i, k))
        b_spec = pl.BlockSpec((tk, tn), lambda i, j, k: (k, j))
        a_aux = pl.BlockSpec((tm, tk), lambda i, j, k: (i, k))
    e_aux = pl.BlockSpec((tm, tn), lambda i, j, k: (i, j))
    aux_full = [(x, a_aux if kd == "a" else e_aux, kd) for x, kd in aux]
    return _mm(name, a, b, dims=dims, grid=(M // tm, N // tn, K // tk), a_spec=a_spec, b_spec=b_spec,
               out_shape=jax.ShapeDtypeStruct((M, N), f32), out_spec=pl.BlockSpec((tm, tn), lambda i, j, k: (i, j)),
               aux=aux_full, a_fn=a_fn, epi_fn=epi_fn, acc_shape=(tm, tn), cache_a=True)


def _mlp_fwd(h, g, w1g, w2g, layer):
    L, D = h.shape
    tm = _mm_rows(L)
    fs = D_FF // N_DEV
    hn = _rms_fwd("mlp_norm", h, g)
    h1 = _mm("mlp_up", hn, w1g, dims=_NN, grid=(L // tm, N_DEV, 1),
             a_spec=pl.BlockSpec((tm, D), lambda i, j, k: (i, 0)),
             b_spec=pl.BlockSpec((None, None, D, fs), lambda i, j, k: (j, layer, 0, 0)),
             out_shape=jax.ShapeDtypeStruct((L, D_FF), bf16), out_spec=pl.BlockSpec((tm, fs), lambda i, j, k: (i, j)),
             acc_shape=(tm, fs), cache_a=True)
    tn = D
    h_out = _mm("mlp_down", h1, w2g, dims=_NN, grid=(L // tm, D // tn, N_DEV),
                a_spec=pl.BlockSpec((tm, fs), lambda i, j, k: (i, k)),
                b_spec=pl.BlockSpec((None, None, fs, tn), lambda i, j, k: (k, layer, 0, j)),
                out_shape=jax.ShapeDtypeStruct((L, D), f32), out_spec=pl.BlockSpec((tm, tn), lambda i, j, k: (i, j)),
                aux=[(h, pl.BlockSpec((tm, tn), lambda i, j, k: (i, j)), "e")],
                a_fn=_sqrelu, epi_fn=lambda acc, res: acc + res, acc_shape=(tm, tn))
    return h_out, hn, h1


def _mlp_bwd(dh, h, g, hn, h1, w1g, w2g, wl, layer, dw1_buf, dw2_buf):
    L, D = h.shape
    tm = _mm_rows(L)
    fs = D_FF // N_DEV
    tk = _mm_rows(L)
    dh1 = _mm("mlp_down_dx", dh, w2g, dims=_NT, grid=(L // tm, N_DEV, 1),
              a_spec=pl.BlockSpec((tm, D), lambda i, j, k: (i, 0)),
              b_spec=pl.BlockSpec((None, None, fs, D), lambda i, j, k: (j, wl, 0, 0)),
              out_shape=jax.ShapeDtypeStruct((L, D_FF), bf16), out_spec=pl.BlockSpec((tm, fs), lambda i, j, k: (i, j)),
              aux=[(h1, pl.BlockSpec((tm, fs), lambda i, j, k: (i, j)), "e")],
              epi_fn=lambda acc, pre: acc * (2.0 * jnp.maximum(pre, 0.0)), acc_shape=(tm, fs), cache_a=True)
    dw2_buf = _mm("mlp_down_dw", h1, dh, dims=_TN, grid=(N_DEV, 1, L // tk),
                  a_spec=pl.BlockSpec((tk, fs), lambda i, j, k: (k, i)),
                  b_spec=pl.BlockSpec((tk, D), lambda i, j, k: (k, 0)),
                  out_shape=jax.ShapeDtypeStruct(dw2_buf.shape, f32),
                  out_spec=pl.BlockSpec((None, None, fs, D), lambda i, j, k: (i, layer, 0, 0)),
                  a_fn=_sqrelu, acc_shape=(fs, D), out_init=dw2_buf)
    tr = D
    dw1_buf = _mm("mlp_up_dw", hn, dh1, dims=_TN, grid=(D // tr, N_DEV, L // tk),
                  a_spec=pl.BlockSpec((tk, tr), lambda i, j, k: (k, i)),
                  b_spec=pl.BlockSpec((tk, fs), lambda i, j, k: (k, j)),
                  out_shape=jax.ShapeDtypeStruct(dw1_buf.shape, f32),
                  out_spec=pl.BlockSpec((None, None, tr, fs), lambda i, j, k: (j, layer, i, 0)),
                  acc_shape=(tr, fs), out_init=dw1_buf)
    tn = D
    dhn = _mm("mlp_up_dx", dh1, w1g, dims=_NT, grid=(L // tm, D // tn, N_DEV),
              a_spec=pl.BlockSpec((tm, fs), lambda i, j, k: (i, k)),
              b_spec=pl.BlockSpec((None, None, tn, fs), lambda i, j, k: (k, wl, j, 0)),
              out_shape=jax.ShapeDtypeStruct((L, D), f32), out_spec=pl.BlockSpec((tm, tn), lambda i, j, k: (i, j)),
              acc_shape=(tm, tn))
    dh_in, dg = _rms_bwd("mlp_norm_bwd", h, g, dhn, dh)
    return dh_in, dg, dw1_buf, dw2_buf


def _shift_dn(x, s, row):
    return x if s == 0 else jnp.where(row >= s, pltpu.roll(x, s, 0), 0.0)


def _shift_up(x, s, row):
    n = x.shape[0]
    return x if s == 0 else jnp.where(row < n - s, pltpu.roll(x, n - s, 0), 0.0)


def _conv_pre(x, w, b, row):
    c = jnp.broadcast_to(b, x.shape)
    for j in range(4):
        c = c + w[j:j + 1, :] * _shift_dn(x, 3 - j, row)
    return c


def _conv_fwd(name, x_arr, blk_off, w, b):
    L = x_arr.shape[0]
    C = w.shape[1]

    def f(x, wv, bv):
        c = _conv_pre(x, wv, bv, _iota(x.shape, 0))
        return c * jax.nn.sigmoid(c)

    return _ew(name, f, [(x_arr, (L, 128), lambda j, z: (0, blk_off + j)), (w, (4, 128), lambda j, z: (0, j)),
                         (b, (1, 128), lambda j, z: (0, j))],
               [((L, C), f32, (L, 128), lambda j, z: (0, j), None)], (C // 128, 1))[0]


def _conv_bwd(name, x_arr, blk_off, w, b, dy):
    L = x_arr.shape[0]
    C = w.shape[1]

    def f(x, wv, bv, g):
        row = _iota(x.shape, 0)
        c = _conv_pre(x, wv, bv, row)
        s = jax.nn.sigmoid(c)
        dc = g * (s * (1.0 + c * (1.0 - s)))
        dx = jnp.zeros_like(x)
        dw = jnp.zeros((4, 128), f32)
        r4 = _iota((4, 128), 0)
        for j in range(4):
            dx = dx + wv[j:j + 1, :] * _shift_up(dc, 3 - j, row)
            dwj = jnp.sum(dc * _shift_dn(x, 3 - j, row), axis=0, keepdims=True)
            dw = dw + jnp.where(r4 == j, jnp.broadcast_to(dwj, (4, 128)), 0.0)
        return dx, dw, jnp.sum(dc, axis=0, keepdims=True)

    return _ew(name, f, [(x_arr, (L, 128), lambda j, z: (0, blk_off + j)), (w, (4, 128), lambda j, z: (0, j)),
                         (b, (1, 128), lambda j, z: (0, j)), (dy, (L, 128), lambda j, z: (0, j))],
               [((L, C), f32, (L, 128), lambda j, z: (0, j), None), ((4, C), f32, (4, 128), lambda j, z: (0, j), None),
                ((1, C), f32, (1, 128), lambda j, z: (0, j), None)], (C // 128, 1))


def _l2norm(t):
    return t * lax.rsqrt(jnp.sum(t * t, axis=-1, keepdims=True) + 1e-6)


def _gdn_act(cq, ck, ab, alog, dtb):
    h = pl.program_id(1)
    qn = _l2norm(cq) * (128.0 ** -0.5)
    kn = _l2norm(ck)
    lane = _iota(ab.shape, 1)
    a_raw = jnp.sum(jnp.where(lane == h, ab, 0.0), axis=1, keepdims=True)
    b_raw = jnp.sum(jnp.where(lane == h + GDN_HEADS, ab, 0.0), axis=1, keepdims=True)
    lane1 = _iota(alog.shape, 1)
    al = jnp.sum(jnp.where(lane1 == h, alog, 0.0), axis=1, keepdims=True)
    db = jnp.sum(jnp.where(lane1 == h, dtb, 0.0), axis=1, keepdims=True)
    g = -jnp.exp(al) * jax.nn.softplus(a_raw + db)
    beta = jax.nn.sigmoid(b_raw)
    return qn, kn, jnp.broadcast_to(g, cq.shape), jnp.broadcast_to(beta, cq.shape)


def _each(f, *lists):
    return [f(*a) for a in zip(*lists)]


def _gdn_chunk(states, consts, vals, dots):
    S = list(states)
    cut = [slice(128 * i, 128 * i + 128) for i in range(len(S))]
    q, k, v, gb, bb = ([t[:, c] for c in cut] for t in vals)
    C = vals[0].shape[0]
    row, col = _iota((C, C), 0), _iota((C, C), 1)
    causal, strict = row >= col, row > col
    ltri = causal.astype(f32)
    eye = (row == col).astype(f32)
    e0 = (_iota((C, 128), 1) == 0).astype(f32)
    last = _iota((C, 1), 0) == C - 1
    Gb = _each(lambda g: dots.dot01(ltri, g), gb)
    Gc = _each(lambda g: jnp.mean(g, axis=1, keepdims=True), Gb)
    Gr = _each(lambda g: dots.dot01(e0, g, _NT), Gb)
    bc = _each(lambda b: jnp.mean(b, axis=1, keepdims=True), bb)
    decay = _each(lambda gc, gr: jnp.where(causal, jnp.exp(jnp.where(causal, gc - gr, 0.0)), 0.0), Gc, Gr)
    kk = _each(lambda a: _dotb(a, a, _NT), k)
    A = _each(lambda b, x, d: jnp.where(strict, b * x * d, 0.0), bc, kk, decay)
    M = _each(lambda a: eye - a, A)
    P = _each(lambda a: dots.dot3(a, a), A)
    for it in range(5):
        M = _each(lambda m, p: m + dots.dot3(m, p), M, P)
        if it < 4:
            P = _each(lambda p: dots.dot3(p, p), P)
    eG = _each(jnp.exp, Gc)
    u = _each(lambda m, x, b: _dotb(m, x * b), M, v, bc)
    w = _each(lambda m, x, b, e: _dotb(m, x * (b * e)), M, k, bc, eG)
    qk = _each(lambda a, b, d: _dotb(a, b, _NT) * d, q, k, decay)
    g_last = _each(lambda gc: jnp.sum(jnp.where(last, gc, 0.0), axis=0, keepdims=True), Gc)
    v_new = _each(lambda a, b, s: a - _dotb(b, s), u, w, S)
    o = _each(lambda a, e, s, b, x: _dotb(a * e, s) + _dotb(b, x), q, eG, S, qk, v_new)
    S_new = _each(lambda gl, s, a, gc, x: jnp.exp(gl) * s + _dotb(a * jnp.exp(gl - gc), x, _TN), g_last, S, k, Gc, v_new)
    return S_new, [jnp.concatenate(o, axis=1)]


def _gdn_post(o, gate, g):
    return _rms(o, g) * (gate * jax.nn.sigmoid(gate))


def _pad_row(v):
    return jnp.pad(v.astype(f32), (0, 128 - v.shape[0])).reshape(1, 128)


def _gdn_fwd(h, g_norm, w_ext, conv_w, a_log, dt_bias, o_norm_g, w_out):
    L, D = h.shape
    tm = _head_rows(L)
    nc = L // CHUNK
    H = GDN_HEADS
    hn = _rms_fwd("mix_norm", h, g_norm)
    proj = _mm_plain("gdn_in", hn, w_ext, _NN)
    zb = jnp.zeros((1, 3 * D), f32)
    cq = _conv_fwd("gdn_conv", proj, 0, conv_w, zb)
    alog, dtb = _pad_row(a_log), _pad_row(dt_bias)
    act_ins = [(cq, (tm, 128), lambda r, hh: (r, hh)), (cq, (tm, 128), lambda r, hh: (r, H + hh)),
               (proj, (tm, 128), lambda r, hh: (r, 4 * H)), (alog, (1, 128), lambda r, hh: (0, 0)),
               (dtb, (1, 128), lambda r, hh: (0, 0))]
    qn, kn, gb, bb = _ew("gdn_act", _gdn_act, act_ins,
                         [((L, D), f32, (tm, 128), lambda r, hh: (r, hh), None)] * 4, (L // tm, H))
    cblk = (CHUNK, 128 * GDN_HB)
    core_ins = [(qn, cblk, lambda u, c: (c, u)), (kn, cblk, lambda u, c: (c, u)),
                (cq, cblk, lambda u, c: (c, 2 * H // GDN_HB + u), (L, D), lambda u, c: (c, u)),
                (gb, cblk, lambda u, c: (c, u)), (bb, cblk, lambda u, c: (c, u))]
    (o,), saved_s = _scan_fwd("gdn_core", _gdn_chunk, GDN_HB, (128, 128), [], core_ins,
                              [((L, D), cblk, lambda u, c: (c, u))], H // GDN_HB, nc)
    on = o_norm_g.reshape(1, 128)
    post_ins = [(o, (tm, 128), lambda r, hh: (r, hh)), (proj, (tm, 128), lambda r, hh: (r, 3 * H + hh)),
                (on, (1, 128), lambda r, hh: (0, 0))]
    y = _ew("gdn_post", _gdn_post, post_ins, [((L, D), f32, (tm, 128), lambda r, hh: (r, hh), None)], (L // tm, H))[0]
    h_out = _mm_plain("gdn_out", y, w_out, _NN, epi_fn=lambda acc, res: acc + res, aux=[(h, "e")])
    saved = dict(hn=hn, proj=proj, cq=cq, alog=alog, dtb=dtb, act_ins=act_ins, core_ins=core_ins, saved_s=saved_s,
                 post_ins=post_ins, y=y, zb=zb)
    return h_out, saved


def _gdn_bwd(dh, h, g_norm, w_ext, conv_w, w_out, sv):
    L, D = h.shape
    tm = _head_rows(L)
    nc = L // CHUNK
    H = GDN_HEADS
    dy = _mm_plain("gdn_out_dx", dh, w_out, _NT)
    dw_out = _mm_plain("gdn_out_dw", sv["y"], dh, _TN)
    hd = ((L, D), f32, (tm, 128), lambda r, hh: (r, hh), None)
    d_o, d_gate, d_on = _ew("gdn_post_bwd", _vjp_fn(_gdn_post, 3),
                            sv["post_ins"] + [(dy, (tm, 128), lambda r, hh: (r, hh))],
                            [hd, hd, ((1, 128), f32, (1, 128), lambda r, hh: (0, 0), "all")], (L // tm, H))
    cblk = (CHUNK, 128 * GDN_HB)
    _, (dqn, dkn, dv, dgb, dbb) = _scan_bwd("gdn_core_bwd", _gdn_chunk, GDN_HB, (128, 128), [], sv["core_ins"],
                                            sv["saved_s"], [(d_o, cblk, lambda u, c: (c, u))], H // GDN_HB, nc)
    cts = [(t, (tm, 128), lambda r, hh: (r, hh)) for t in (dqn, dkn, dgb, dbb)]
    row128 = ((1, 128), f32, (1, 128), lambda r, hh: (0, 0), "all")
    d_cq, d_ck, d_ab, d_alog, d_dtb = _ew(
        "gdn_act_bwd", _vjp_fn(_gdn_act, 5), sv["act_ins"] + cts,
        [hd, hd, ((L, 128), f32, (tm, 128), lambda r, hh: (r, 0), "inner"), row128, row128], (L // tm, H))
    d_conv_out = jnp.concatenate([d_cq, d_ck, dv], axis=1)
    d_conv_in, d_conv_w, _ = _conv_bwd("gdn_conv_bwd", sv["proj"], 0, conv_w, sv["zb"], d_conv_out)
    d_proj = jnp.concatenate([d_conv_in, d_gate, d_ab], axis=1)
    dw_ext = _mm_plain("gdn_in_dw", sv["hn"], d_proj, _TN)
    dhn = _mm_plain("gdn_in_dx", d_proj, w_ext, _NT)
    dh_in, dg = _rms_bwd("mix_norm_bwd", h, g_norm, dhn, dh)
    grads = dict(norm=dg, w_ext=dw_ext, conv_w=d_conv_w, a_log=d_alog[0, :H], dt_bias=d_dtb[0, :H],
                 o_norm_g=d_on[0], w_out=dw_out)
    return dh_in, grads


def _expand_lanes(row, width, rep):
    sel = ((_iota((128, width), 1) // rep) == _iota((128, width), 0)).astype(f32)
    return jnp.mean(_dot(jnp.broadcast_to(row, (8, 128)), sel), axis=0, keepdims=True)


def _s5_params(lre, lim, ldt, wbr, wbi):
    dt = jnp.exp(_expand_lanes(ldt, 512, S5_STATE))
    mag = jnp.exp(lre * dt)
    ang = lim * dt
    abr, abi = mag * jnp.cos(ang), mag * jnp.sin(ang)
    nr = abr - 1.0
    den = lre * lre + lim * lim
    cr = (nr * lre + abi * lim) / den
    ci = (abi * lre - nr * lim) / den
    return abr, abi, cr * wbr - ci * wbi, cr * wbi + ci * wbr


def _s5_scan(name, xr, xi, ar, ai, rev, want_prev):
    L, W = xr.shape
    nb = L // 8
    n_out = 4 if want_prev else 2
    lanes = S5_SCAN_LANES

    def body(xr_ref, xi_ref, ar_ref, ai_ref, *outs):
        a_r = ar_ref[...]
        a_i = -ai_ref[...] if rev else ai_ref[...]

        def cm(p, q):
            return p[0] * q[0] - p[1] * q[1], p[0] * q[1] + p[1] * q[0]

        a1 = (a_r, a_i)
        a2 = cm(a1, a1)
        a3 = cm(a2, a1)
        a4 = cm(a2, a2)
        pw = [a1, a2, a3, a4, cm(a4, a1), cm(a4, a2), cm(a4, a3), cm(a4, a4)]
        blk8 = (8, lanes)
        row = _iota(blk8, 0)
        tab_r = jnp.zeros(blk8, f32)
        tab_i = jnp.zeros(blk8, f32)
        for t in range(8):
            idx = 7 - t if rev else t
            tab_r = jnp.where(row == idx, jnp.broadcast_to(pw[t][0], blk8), tab_r)
            tab_i = jnp.where(row == idx, jnp.broadcast_to(pw[t][1], blk8), tab_i)
        lv = [(d, jnp.broadcast_to(p[0], blk8), jnp.broadcast_to(p[1], blk8)) for d, p in ((1, a1), (2, a2), (4, a4))]

        def step(i, carry):
            cr, ci = carry
            blk = nb - 1 - i if rev else i
            r0 = pl.multiple_of(blk * 8, 8)
            x_r = xr_ref[pl.ds(r0, 8), :]
            x_i = xi_ref[pl.ds(r0, 8), :]
            for d, p_r, p_i in lv:
                if rev:
                    s_r = jnp.where(row < 8 - d, pltpu.roll(x_r, 8 - d, 0), 0.0)
                    s_i = jnp.where(row < 8 - d, pltpu.roll(x_i, 8 - d, 0), 0.0)
                else:
                    s_r = jnp.where(row >= d, pltpu.roll(x_r, d, 0), 0.0)
                    s_i = jnp.where(row >= d, pltpu.roll(x_i, d, 0), 0.0)
                x_r, x_i = x_r + p_r * s_r - p_i * s_i, x_i + p_r * s_i + p_i * s_r
            x_r, x_i = x_r + tab_r * cr - tab_i * ci, x_i + tab_r * ci + tab_i * cr
            outs[0][pl.ds(r0, 8), :] = x_r
            outs[1][pl.ds(r0, 8), :] = x_i
            if want_prev:
                outs[2][pl.ds(r0, 8), :] = jnp.where(row >= 1, pltpu.roll(x_r, 1, 0), cr)
                outs[3][pl.ds(r0, 8), :] = jnp.where(row >= 1, pltpu.roll(x_i, 1, 0), ci)
            e = 0 if rev else 7
            return jnp.broadcast_to(x_r[e:e + 1, :], blk8), jnp.broadcast_to(x_i[e:e + 1, :], blk8)

        lax.fori_loop(0, nb, step, (jnp.zeros(blk8, f32), jnp.zeros(blk8, f32)))

    per = 512 // lanes
    col = pl.BlockSpec((L, lanes), lambda q, z: (0, q))
    aspec = pl.BlockSpec((None, 1, lanes), lambda q, z: (q // per, 0, q % per))
    return pl.pallas_call(
        body, name=name, grid=(W // lanes, 1), in_specs=[col, col, aspec, aspec], out_specs=[col] * n_out,
        out_shape=[jax.ShapeDtypeStruct((L, W), f32)] * n_out, compiler_params=_params(2),
    )(xr, xi, ar, ai)


def _blockdiag(t, n_in, n_out):
    t4 = t.reshape(8, 8, n_in, n_out)
    return jnp.einsum("jaio,ab->jaibo", t4, jnp.eye(8, dtype=t.dtype)).reshape(8, 8 * n_in, 8 * n_out)


def _blockdiag_t(w, n_in, n_out):
    w5 = w.reshape(8, 8, n_in, 8, n_out)
    return jnp.einsum("jaibo,ab->jaio", w5, jnp.eye(8, dtype=w.dtype)).reshape(64, n_in, n_out)


def _glu(ag, h):
    n = ag.shape[1] // 2
    return h + ag[:, :n] * jax.nn.sigmoid(ag[:, n:])


def _s5_fwd(h, g_norm, w_in, lam_re, lam_im, log_dt, b_re, b_im, c_re, c_im, d_skip, w_out_g):
    L, D = h.shape
    tm, te = _mm_rows(L), _row_tile(L)
    W = 8 * 512
    hn = _rms_fwd("mix_norm", h, g_norm)
    u = _mm_plain("s5_in", hn, w_in, _NN)
    lre, lim = lam_re.reshape(8, 1, 512), lam_im.reshape(8, 1, 512)
    ldt = jnp.pad(log_dt.reshape(8, 1, 8), ((0, 0), (0, 0), (0, 120)))
    wbr = _blockdiag(b_re.transpose(0, 2, 1), 16, 64)
    wbi = _blockdiag(b_im.transpose(0, 2, 1), 16, 64)
    wcr = _blockdiag(c_re.transpose(0, 2, 1), 64, 16)
    wci = _blockdiag(c_im.transpose(0, 2, 1), 64, 16)
    jb = lambda shape: (shape, lambda j, z: (j, 0, 0))
    par_ins = [(lre, *jb((None, 1, 512))), (lim, *jb((None, 1, 512))), (ldt, *jb((None, 1, 128))),
               (wbr, *jb((None, 128, 512))), (wbi, *jb((None, 128, 512)))]
    abr, abi, bbr, bbi = _ew("s5_params", _s5_params, par_ins,
                             [((8, 1, 512), f32, *jb((None, 1, 512)), None)] * 2
                             + [((8, 128, 512), f32, *jb((None, 128, 512)), None)] * 2, (8, 1))

    def bu(name, wb):
        return _mm(name, u, wb, dims=_NN, grid=(L // tm, 8, 1),
                   a_spec=pl.BlockSpec((tm, 128), lambda i, j, k: (i, j)),
                   b_spec=pl.BlockSpec((None, 128, 512), lambda i, j, k: (j, 0, 0)),
                   out_shape=jax.ShapeDtypeStruct((L, W), f32), out_spec=pl.BlockSpec((tm, 512), lambda i, j, k: (i, j)),
                   acc_shape=(tm, 512))

    bur, bui = bu("s5_bu", bbr), bu("s5_bu", bbi)
    sr, si, pr, pi = _s5_scan("s5_scan", bur, bui, abr, abi, False, True)
    d_row = d_skip.reshape(1, D)
    cspec = dict(a_spec=pl.BlockSpec((tm, 512), lambda i, j, k: (i, j)),
                 b_spec=pl.BlockSpec((None, 512, 128), lambda i, j, k: (j, 0, 0)),
                 out_shape=jax.ShapeDtypeStruct((L, D), f32), out_spec=pl.BlockSpec((tm, 128), lambda i, j, k: (i, j)),
                 acc_shape=(tm, 128))
    e128 = pl.BlockSpec((tm, 128), lambda i, j, k: (i, j))
    pre1 = _mm("s5_c_re", sr, wcr, dims=_NN, grid=(L // tm, 8, 1), **cspec)
    pre = _mm("s5_c_im", si, wci, dims=_NN, grid=(L // tm, 8, 1),
              aux=[(pre1, e128, "e"), (u, e128, "e"), (d_row, pl.BlockSpec((1, 128), lambda i, j, k: (0, j)), "e")],
              epi_fn=lambda acc, p1, uu, dd: p1 - acc + dd * uu, **cspec)
    ws = D // N_DEV * 2
    ag = _mm("s5_out", pre, w_out_g, dims=_NN, grid=(L // tm, N_DEV, 1),
             a_spec=pl.BlockSpec((tm, D), lambda i, j, k: (i, 0)),
             b_spec=pl.BlockSpec((None, None, D, ws), lambda i, j, k: (j, 0, 0, 0)),
             out_shape=jax.ShapeDtypeStruct((L, 2 * D), f32), out_spec=pl.BlockSpec((tm, ws), lambda i, j, k: (i, j)),
             a_fn=jax.nn.gelu, acc_shape=(tm, ws), cache_a=True)
    h_out = _ew("s5_glu", _glu, [(ag, *_rows(te, 2 * D)), (h, *_rows(te, D))],
                [((L, D), f32, *_rows(te, D), None)], (L // te, 1))[0]
    saved = dict(hn=hn, u=u, par_ins=par_ins, abr=abr, abi=abi, bbr=bbr, bbi=bbi, wcr=wcr, wci=wci, sr=sr, si=si,
                 pr=pr, pi=pi, pre=pre, ag=ag, d_row=d_row)
    return h_out, saved


def _s5_bwd(dh, h, g_norm, w_in, w_out_g, sv):
    L, D = h.shape
    tm, te = _mm_rows(L), _row_tile(L)
    tk = min(512, L)
    W = 8 * 512
    ws = D // N_DEV * 2
    u, pre, d_row = sv["u"], sv["pre"], sv["d_row"]
    d_ag = _ew("s5_glu_bwd", lambda ag, hv, g: _vjp_fn(_glu, 2)(ag, hv, g)[0],
               [(sv["ag"], *_rows(te, 2 * D)), (h, *_rows(te, D)), (dh, *_rows(te, D))],
               [((L, 2 * D), f32, *_rows(te, 2 * D), None)], (L // te, 1))[0]
    tr = D
    dw_out = _mm("s5_out_dw", pre, d_ag, dims=_TN, grid=(D // tr, N_DEV, L // tk),
                 a_spec=pl.BlockSpec((tk, tr), lambda i, j, k: (k, i)),
                 b_spec=pl.BlockSpec((tk, ws), lambda i, j, k: (k, j)),
                 out_shape=jax.ShapeDtypeStruct((N_DEV, 1, D, ws), f32),
                 out_spec=pl.BlockSpec((None, None, tr, ws), lambda i, j, k: (j, 0, i, 0)),
                 a_fn=jax.nn.gelu, acc_shape=(tr, ws))
    tn = 512
    dpre = _mm("s5_out_dx", d_ag, w_out_g, dims=_NT, grid=(L // tm, D // tn, N_DEV),
               a_spec=pl.BlockSpec((tm, ws), lambda i, j, k: (i, k)),
               b_spec=pl.BlockSpec((None, None, tn, ws), lambda i, j, k: (k, 0, j, 0)),
               out_shape=jax.ShapeDtypeStruct((L, D), f32), out_spec=pl.BlockSpec((tm, tn), lambda i, j, k: (i, j)),
               aux=[(pre, pl.BlockSpec((tm, tn), lambda i, j, k: (i, j)), "e")],
               epi_fn=lambda acc, p: _vjp_fn(jax.nn.gelu, 1)(p, acc)[0], acc_shape=(tm, tn))
    d_d = _ew("s5_dskip", lambda a, b: jnp.sum(a * b, axis=0, keepdims=True),
              [(dpre, *_rows(te, D)), (u, *_rows(te, D))], [((1, D), f32, *_const((1, D)), "all")], (L // te, 1))[0]
    neg = lambda acc: -acc
    dsspec = dict(dims=_NT, grid=(L // tm, 8, 1), a_spec=pl.BlockSpec((tm, 128), lambda i, j, k: (i, j)),
                  b_spec=pl.BlockSpec((None, 512, 128), lambda i, j, k: (j, 0, 0)),
                  out_shape=jax.ShapeDtypeStruct((L, W), f32), out_spec=pl.BlockSpec((tm, 512), lambda i, j, k: (i, j)),
                  acc_shape=(tm, 512))
    dsr = _mm("s5_c_re_dx", dpre, sv["wcr"], **dsspec)
    dsi = _mm("s5_c_im_dx", dpre, sv["wci"], epi_fn=neg, **dsspec)
    dwspec = dict(dims=_TN, grid=(8, 1, L // tk), a_spec=pl.BlockSpec((tk, 512), lambda i, j, k: (k, i)),
                  b_spec=pl.BlockSpec((tk, 128), lambda i, j, k: (k, i)),
                  out_shape=jax.ShapeDtypeStruct((8, 512, 128), f32),
                  out_spec=pl.BlockSpec((None, 512, 128), lambda i, j, k: (i, 0, 0)), acc_shape=(512, 128))
    dwcr = _mm("s5_c_re_dw", sv["sr"], dpre, **dwspec)
    dwci = _mm("s5_c_im_dw", sv["si"], dpre, epi_fn=neg, **dwspec)
    lr, li = _s5_scan("s5_scan_bwd", dsr, dsi, sv["abr"], sv["abi"], True, False)

    def da(lrv, liv, prv, piv):
        return (jnp.sum(lrv * prv + liv * piv, axis=0, keepdims=True),
                jnp.sum(liv * prv - lrv * piv, axis=0, keepdims=True))

    sblk = ((te, 512), lambda j, r: (r, j))
    dabr, dabi = _ew("s5_dlam", da, [(lr, *sblk), (li, *sblk), (sv["pr"], *sblk), (sv["pi"], *sblk)],
                     [((8, 1, 512), f32, (None, 1, 512), lambda j, r: (j, 0, 0), "inner")] * 2, (8, L // te))
    dbspec = dict(dims=_TN, grid=(8, 1, L // tk), a_spec=pl.BlockSpec((tk, 128), lambda i, j, k: (k, i)),
                  b_spec=pl.BlockSpec((tk, 512), lambda i, j, k: (k, i)),
                  out_shape=jax.ShapeDtypeStruct((8, 128, 512), f32),
                  out_spec=pl.BlockSpec((None, 128, 512), lambda i, j, k: (i, 0, 0)), acc_shape=(128, 512))
    dbbr = _mm("s5_bu_dw", u, lr, **dbspec)
    dbbi = _mm("s5_bu_dw", u, li, **dbspec)
    duspec = dict(dims=_NT, grid=(L // tm, 8, 1), a_spec=pl.BlockSpec((tm, 512), lambda i, j, k: (i, j)),
                  b_spec=pl.BlockSpec((None, 128, 512), lambda i, j, k: (j, 0, 0)),
                  out_shape=jax.ShapeDtypeStruct((L, D), f32), out_spec=pl.BlockSpec((tm, 128), lambda i, j, k: (i, j)),
                  acc_shape=(tm, 128))
    e128 = pl.BlockSpec((tm, 128), lambda i, j, k: (i, j))
    du1 = _mm("s5_bu_dx_re", lr, sv["bbr"], **duspec)
    du = _mm("s5_bu_dx_im", li, sv["bbi"],
             aux=[(du1, e128, "e"), (dpre, e128, "e"), (d_row, pl.BlockSpec((1, 128), lambda i, j, k: (0, j)), "e")],
             epi_fn=lambda acc, d1, dp, dd: acc + d1 + dp * dd, **duspec)
    jb = lambda shape: (shape, lambda j, z: (j, 0, 0))
    cts = [(dabr, *jb((None, 1, 512))), (dabi, *jb((None, 1, 512))), (dbbr, *jb((None, 128, 512))),
           (dbbi, *jb((None, 128, 512)))]
    dlre, dlim, dldt, dwbr, dwbi = _ew(
        "s5_params_bwd", _vjp_fn(_s5_params, 5), sv["par_ins"] + cts,
        [((8, 1, 512), f32, *jb((None, 1, 512)), None)] * 2 + [((8, 1, 128), f32, *jb((None, 1, 128)), None)]
        + [((8, 128, 512), f32, *jb((None, 128, 512)), None)] * 2, (8, 1))
    dw_in = _mm_plain("s5_in_dw", sv["hn"], du, _TN)
    dhn = _mm_plain("s5_in_dx", du, w_in, _NT)
    dh_in, dg = _rms_bwd("mix_norm_bwd", h, g_norm, dhn, dh)
    grads = dict(norm=dg, w_in=dw_in, lam_re=dlre.reshape(64, 64), lam_im=dlim.reshape(64, 64),
                 log_dt=dldt[:, 0, :8].reshape(64),
                 b_re=_blockdiag_t(dwbr, 16, 64).transpose(0, 2, 1), b_im=_blockdiag_t(dwbi, 16, 64).transpose(0, 2, 1),
                 c_re=_blockdiag_t(dwcr, 64, 16).transpose(0, 2, 1), c_im=_blockdiag_t(dwci, 64, 16).transpose(0, 2, 1),
                 d=d_d[0], w_out=dw_out)
    return dh_in, grads


def _m2_act(dt_raw, dtbias, alog):
    dt = jax.nn.softplus(dt_raw + dtbias)
    da = dt * (-jnp.exp(alog))
    sel = ((_iota((128, M2_INNER), 1) // 64) == _iota((128, M2_INNER), 0)).astype(f32)
    return _dot(dt, sel), _dot(da, sel)


def _m2_dexp(d):
    return _expand_lanes(d, M2_INNER, 64)


def _ssd_chunk(states, consts, vals, dots):
    (dsk,) = consts
    S = list(states)
    n = len(S)
    cut = [slice(128 * i, 128 * i + 128) for i in range(n)]
    x, dtb, dab = ([t[:, c] for c in cut] for t in vals[:3])
    dsk = [dsk[:, c] for c in cut]
    B = [vals[3][:, cut[i // 2]] for i in range(n)]
    Cm = [vals[4][:, cut[i // 2]] for i in range(n)]
    C = vals[0].shape[0]
    row, col = _iota((C, C), 0), _iota((C, C), 1)
    causal = row >= col
    ltri = causal.astype(f32)
    lane = _iota((C, 128), 1)
    last = _iota((C, 128), 0) == C - 1
    eye128 = _iota((128, 128), 0) == _iota((128, 128), 1)
    head = [jnp.logical_and(lane >= 64 * hh, lane < 64 * hh + 64) for hh in range(2)]
    pick = [(lane == 64 * hh).astype(f32) for hh in range(2)]
    xdt = _each(lambda a, b: a * b, x, dtb)
    cb = _each(lambda c, b: _dotb(c, b, _NT), Cm[::2], B[::2])
    cum = _each(lambda a: dots.dot01(ltri, a), dab)
    clast = _each(lambda a: jnp.sum(jnp.where(last, a, 0.0), axis=0, keepdims=True), cum)
    st = _each(lambda a, cl, cu, b: _dotb(a * jnp.exp(cl - cu), b, _TN), xdt, clast, cum, B)
    y = _each(lambda c, s, cu: _dotb(c, s, _NT) * jnp.exp(cu), Cm, S, cum)
    for hh in range(2):
        ccol = _each(lambda cu: jnp.sum(jnp.where(head[hh], cu, 0.0), axis=1, keepdims=True) * (1.0 / 64), cum)
        crow = _each(lambda cu: dots.dot01(pick[hh], cu, _NT), cum)
        lm = _each(lambda a, b: jnp.where(causal, jnp.exp(jnp.where(causal, a - b, 0.0)), 0.0), ccol, crow)
        y = [y[i] + _dotb(cb[i // 2] * lm[i], jnp.where(head[hh], xdt[i], 0.0)) for i in range(n)]
    cdcol = _each(lambda cl: jnp.sum(jnp.where(eye128, jnp.broadcast_to(jnp.exp(cl), (128, 128)), 0.0),
                                     axis=1, keepdims=True), clast)
    S_new = _each(lambda c, s, t: c * s + t, cdcol, S, st)
    out = _each(lambda a, d, b: a + d * b, y, dsk, x)
    return S_new, [jnp.concatenate(out, axis=1)]


def _m2_post(yc, z, ng):
    return _rms(yc * (z * jax.nn.sigmoid(z)), ng)


def _m2_fwd(h, g_norm, w_ext, conv_w, conv_b, dt_bias, a_log, d_skip, norm_g, w_out):
    L, D = h.shape
    tm = _row_tile(L)
    nc = L // CHUNK
    NI = M2_INNER
    hn = _rms_fwd("mix_norm", h, g_norm)
    proj = _mm_plain("m2_in", hn, w_ext, _NN)
    xbc = _conv_fwd("m2_conv", proj, NI // 128, conv_w, conv_b)
    dtb_row, alog_row, d_pad = _pad_row(dt_bias), _pad_row(a_log), _pad_row(d_skip)
    act_ins = [(proj, (tm, 128), lambda r, z: (r, 3 * NI // 128)), (dtb_row, *_const((1, 128))),
               (alog_row, *_const((1, 128)))]
    dtb, dab = _ew("m2_act", _m2_act, act_ins, [((L, NI), f32, *_rows(tm, NI), None)] * 2, (L // tm, 1))
    dsk = _ew("m2_dexp", _m2_dexp, [(d_pad, *_const((1, 128)))], [((1, NI), f32, *_const((1, NI)), None)], (1, 1))[0]
    GB = M2_GB
    x_blk, bc_blk = (CHUNK, 256 * GB), (CHUNK, 128 * GB)
    cins = [(dsk, (1, 256 * GB), lambda u, c: (0, u))]
    core_ins = [(xbc, x_blk, lambda u, c: (c, u), (L, NI), lambda u, c: (c, u)),
                (dtb, x_blk, lambda u, c: (c, u)), (dab, x_blk, lambda u, c: (c, u)),
                (xbc, bc_blk, lambda u, c: (c, 16 // GB + u), (L, D), lambda u, c: (c, u)),
                (xbc, bc_blk, lambda u, c: (c, 24 // GB + u), (L, D), lambda u, c: (c, u))]
    (yc,), saved_s = _scan_fwd("m2_core", _ssd_chunk, 2 * GB, (128, 128), cins, core_ins,
                               [((L, NI), x_blk, lambda u, c: (c, u))], 8 // GB, nc)
    tp = _head_rows(L)
    gblk = ((tp, 256), lambda g, r: (r, g))
    post_ins = [(yc, *gblk), (proj, *gblk), (norm_g, (1, 256), lambda g, r: (0, g))]
    yn = _ew("m2_post", _m2_post, post_ins, [((L, NI), f32, *gblk, None)], (8, L // tp))[0]
    h_out = _mm_plain("m2_out", yn, w_out, _NN, epi_fn=lambda acc, res: acc + res, aux=[(h, "e")])
    saved = dict(hn=hn, proj=proj, act_ins=act_ins, d_pad=d_pad, cins=cins, core_ins=core_ins, saved_s=saved_s,
                 post_ins=post_ins, yn=yn)
    return h_out, saved


def _m2_bwd(dh, h, g_norm, w_ext, conv_w, conv_b, w_out, sv):
    L, D = h.shape
    tm = _row_tile(L)
    nc = L // CHUNK
    NI = M2_INNER
    dyn = _mm_plain("m2_out_dx", dh, w_out, _NT)
    dw_out = _mm_plain("m2_out_dw", sv["yn"], dh, _TN)
    tp = _head_rows(L)
    gblk = ((tp, 256), lambda g, r: (r, g))
    d_yc, d_z, d_ng = _ew("m2_post_bwd", _vjp_fn(_m2_post, 3), sv["post_ins"] + [(dyn, *gblk)],
                          [((L, NI), f32, *gblk, None)] * 2 + [((1, NI), f32, (1, 256), lambda g, r: (0, g), "inner")],
                          (8, L // tp))
    (d_dsk,), (dx, d_dtb, d_dab, dB, dC) = _scan_bwd(
        "m2_core_bwd", _ssd_chunk, 2 * M2_GB, (128, 128), sv["cins"], sv["core_ins"], sv["saved_s"],
        [(d_yc, (CHUNK, 256 * M2_GB), lambda u, c: (c, u))], 8 // M2_GB, nc)
    row128 = ((1, 128), f32, *_const((1, 128)), "all")
    d_dt_raw, d_dtbias, d_alog = _ew(
        "m2_act_bwd", _vjp_fn(_m2_act, 3), sv["act_ins"] + [(d_dtb, *_rows(tm, NI)), (d_dab, *_rows(tm, NI))],
        [((L, 128), f32, *_rows(tm, 128), None), row128, row128], (L // tm, 1))
    d_d = _ew("m2_dexp_bwd", _vjp_fn(_m2_dexp, 1), [(sv["d_pad"], *_const((1, 128))), (d_dsk, *_const((1, NI)))],
              [((1, 128), f32, *_const((1, 128)), None)], (1, 1))[0]
    d_conv_out = jnp.concatenate([dx, dB, dC], axis=1)
    d_conv_in, d_conv_w, d_conv_b = _conv_bwd("m2_conv_bwd", sv["proj"], NI // 128, conv_w, conv_b, d_conv_out)
    d_proj = jnp.concatenate([d_z, d_conv_in, d_dt_raw], axis=1)
    dw_ext = _mm_plain("m2_in_dw", sv["hn"], d_proj, _TN)
    dhn = _mm_plain("m2_in_dx", d_proj, w_ext, _NT)
    dh_in, dg = _rms_bwd("mix_norm_bwd", h, g_norm, dhn, dh)
    grads = dict(norm=dg, w_ext=dw_ext, conv_w=d_conv_w, conv_b=d_conv_b, dt_bias=d_dtbias[0, :M2_HEADS],
                 a_log=d_alog[0, :M2_HEADS], d=d_d[0, :M2_HEADS], norm_g=d_ng, w_out=dw_out)
    return dh_in, grads


def _mesh_pos():
    return lax.axis_index("x"), lax.axis_index("y"), lax.axis_index("c")


def _flip(pos, p):
    x, y, c = pos
    return (1 - x if p & 4 else x, 1 - y if p & 2 else y, 1 - c if p & 1 else c)


def _index(pos):
    return 4 * pos[0] + 2 * pos[1] + pos[2]


def _comm_call(name, body, arrays, out_shape, n_sem):
    n = len(arrays)
    hbm = pl.BlockSpec(memory_space=pl.ANY)
    return pl.pallas_call(
        body, name=name, in_specs=[hbm] * n, out_specs=[hbm] * len(out_shape), out_shape=out_shape,
        scratch_shapes=[pltpu.SemaphoreType.DMA((n, n_sem)), pltpu.SemaphoreType.DMA((n, n_sem)),
                        pltpu.SemaphoreType.DMA((n, 4))],
    )(*arrays)


def _gather(name, arrays):
    n = len(arrays)

    def body(*refs):
        ins, outs = refs[:n], refs[n:2 * n]
        send_sems, recv_sems, loc_sems = refs[2 * n:]
        me = _mesh_pos()
        c = me[2]
        sib = _flip(me, 1)
        chips = [_flip(me, 4), _flip(me, 2), _flip(me, 6)]

        def copy(w, k, block, to, src=None):
            slab = outs[w].at[_index(block)]
            return pltpu.make_async_remote_copy(
                src_ref=slab if src is None else src, dst_ref=slab, send_sem=send_sems.at[w, k],
                recv_sem=recv_sems.at[w, k], device_id=to, device_id_type=MESH)

        local = [pltpu.make_async_copy(ins[w], outs[w].at[_index(me)], loc_sems.at[w, 0]) for w in range(n)]
        for cp in local:
            cp.start()
        first = [copy(w, 0, me, sib, src=ins[w]) for w in range(n)]
        first += [copy(w, 1 + j, me, chip, src=ins[w]) for j, chip in enumerate(chips) for w in range(n)]
        for cp in first:
            cp.start()
        passed = []
        for j, chip in enumerate(chips):
            for w in range(n):
                copy(w, 1 + j, chip, me).wait_recv()
                fwd = copy(w, 4 + j, chip, sib)
                fwd.start()
                passed.append(fwd)
        for w in range(n):
            copy(w, 0, sib, me).wait_recv()
        for j, chip in enumerate(chips):
            for w in range(n):
                copy(w, 4 + j, (chip[0], chip[1], 1 - c), me).wait_recv()
        for cp in first + passed:
            cp.wait_send()
        for cp in local:
            cp.wait()

    out_shape = [jax.ShapeDtypeStruct((N_DEV,) + a.shape, a.dtype) for a in arrays]
    return _comm_call(name, body, arrays, out_shape, N_DEV - 1)


_HBM = pl.BlockSpec(memory_space=pltpu.HBM)
_SEM = pl.BlockSpec(memory_space=pltpu.SEMAPHORE)
_SPLIT_COPIES = 4


def _split_targets(me):
    return [_flip(me, 1), _flip(me, 4), _flip(me, 2), _flip(me, 6)]


def _gather_start(name, arrays, lands):
    n = len(arrays)
    ns = n * _SPLIT_COPIES

    def body(*refs):
        ins, land = refs[:n], refs[n:2 * n]
        send_sems, recv_sems = refs[2 * n:2 * n + ns], refs[2 * n + ns:2 * n + 2 * ns]
        token = refs[4 * n + 2 * ns]
        me = _mesh_pos()
        for w in range(n):
            for k, to in enumerate(_split_targets(me)):
                pltpu.make_async_remote_copy(
                    src_ref=ins[w], dst_ref=land[w].at[_index(me)], send_sem=send_sems[w * _SPLIT_COPIES + k],
                    recv_sem=recv_sems[w * _SPLIT_COPIES + k], device_id=to, device_id_type=MESH).start()
        token[...] = jnp.zeros_like(token)

    sem = pltpu.SemaphoreType.DMA(())
    res = pl.pallas_call(
        body, name=name,
        out_shape=(*[sem] * (2 * ns), *[pltpu.HBM(a.shape, a.dtype) for a in arrays],
                   *[pltpu.HBM(a.shape, a.dtype) for a in lands], jax.ShapeDtypeStruct((8, 128), f32)),
        in_specs=[_HBM] * (2 * n),
        out_specs=(*[_SEM] * (2 * ns), *[_HBM] * (2 * n), pl.BlockSpec(memory_space=pltpu.VMEM)),
        input_output_aliases={i: 2 * ns + i for i in range(2 * n)},
        compiler_params=pltpu.CompilerParams(has_side_effects=pltpu.SideEffectType.DATAFLOW_SIDE_EFFECTING),
    )(*[pltpu.with_memory_space_constraint(a, pltpu.HBM) for a in list(arrays) + list(lands)])
    sems, rest = res[:2 * ns], res[2 * ns:]
    return sems[:ns], sems[ns:], rest[:n], rest[n:2 * n], rest[2 * n]


def _gather_wait(name, arrays, lands, send_sems, recv_sems, after):
    n = len(arrays)
    ns = n * _SPLIT_COPIES

    def body(*refs):
        ins, land = refs[:n], refs[n:2 * n]
        s_sems, r_sems = refs[2 * n:2 * n + ns], refs[2 * n + ns:2 * n + 2 * ns]
        me = _mesh_pos()
        for w in range(n):
            for k, peer in enumerate(_split_targets(me)):
                cp = pltpu.make_async_remote_copy(
                    src_ref=ins[w], dst_ref=land[w].at[_index(peer)], send_sem=s_sems[w * _SPLIT_COPIES + k],
                    recv_sem=r_sems[w * _SPLIT_COPIES + k], device_id=peer, device_id_type=MESH)
                cp.wait_send()
                cp.wait_recv()

    res = pl.pallas_call(
        body, name=name,
        out_shape=(*[pltpu.HBM(a.shape, a.dtype) for a in arrays], *[pltpu.HBM(a.shape, a.dtype) for a in lands]),
        in_specs=[_HBM] * (2 * n) + [_SEM] * (2 * ns) + [pl.BlockSpec(memory_space=pl.ANY)],
        out_specs=tuple([_HBM] * (2 * n)), input_output_aliases={i: i for i in range(2 * n)},
        compiler_params=pltpu.CompilerParams(has_side_effects=pltpu.SideEffectType.DATAFLOW_SIDE_EFFECTING),
    )(*arrays, *lands, *send_sems, *recv_sems, after)
    return res[n:]


def _gather_forward(name, lands):
    n = len(lands)

    def body(*refs):
        outs = refs[n:2 * n]
        send_sems, recv_sems, _ = refs[2 * n:]
        me = _mesh_pos()
        sib = _flip(me, 1)
        held = [_flip(me, 4), _flip(me, 2), _flip(me, 6), sib]

        def copy(w, j, block):
            slab = outs[w].at[_index(block)]
            return pltpu.make_async_remote_copy(src_ref=slab, dst_ref=slab, send_sem=send_sems.at[w, j],
                                                recv_sem=recv_sems.at[w, j], device_id=sib, device_id_type=MESH)

        sends = [copy(w, j, blk) for j, blk in enumerate(held) for w in range(n)]
        for cp in sends:
            cp.start()
        for j, blk in enumerate(held):
            for w in range(n):
                copy(w, j, (blk[0], blk[1], 1 - blk[2])).wait_recv()
        for cp in sends:
            cp.wait_send()

    hbm = pl.BlockSpec(memory_space=pl.ANY)
    return pl.pallas_call(
        body, name=name, in_specs=[hbm] * n, out_specs=[hbm] * n,
        out_shape=[jax.ShapeDtypeStruct(a.shape, a.dtype) for a in lands],
        input_output_aliases={i: i for i in range(n)},
        scratch_shapes=[pltpu.SemaphoreType.DMA((n, 4)), pltpu.SemaphoreType.DMA((n, 4)), pltpu.SemaphoreType.DMA((n, 4))],
    )(*lands)


def _scatter_pair(name, arrays):
    n = len(arrays)

    def body(*refs):
        ins, outs = refs[:n], refs[n:2 * n]
        send_sems, recv_sems, _ = refs[2 * n:]
        me = _mesh_pos()
        c = me[2]
        sib = _flip(me, 1)

        def copy(w, q):
            return pltpu.make_async_remote_copy(
                src_ref=ins[w].at[2 * q + 1 - c], dst_ref=outs[w].at[q], send_sem=send_sems.at[w, q],
                recv_sem=recv_sems.at[w, q], device_id=sib, device_id_type=MESH)

        cps = [copy(w, q) for q in range(4) for w in range(n)]
        for cp in cps:
            cp.start()
        for cp in cps:
            cp.wait()

    out_shape = [jax.ShapeDtypeStruct((4,) + a.shape[1:], a.dtype) for a in arrays]
    return _comm_call(name, body, arrays, out_shape, 4)


def _pair_add(name, full, theirs, core, dtype):
    _, R, C = theirs.shape
    tr = R if R <= 256 else (256 if C <= 512 else 128)

    def body(core_ref, mine_ref, theirs_ref, o_ref):
        o_ref[...] = (mine_ref[...] + theirs_ref[...]).astype(o_ref.dtype)

    blk = pl.BlockSpec((4, tr, C), lambda r, cr: (0, r, 0))
    grid_spec = pltpu.PrefetchScalarGridSpec(
        num_scalar_prefetch=1, grid=(R // tr,),
        in_specs=[pl.BlockSpec((4, None, tr, C), lambda r, cr: (0, cr[0], r, 0)), blk], out_specs=blk)
    return pl.pallas_call(
        body, name=name, grid_spec=grid_spec, out_shape=jax.ShapeDtypeStruct((4, R, C), dtype),
        compiler_params=_params(1),
    )(core.reshape(1).astype(jnp.int32), full.reshape(4, 2, R, C), theirs)


def _scatter_chips(name, arrays):
    n = len(arrays)

    def body(*refs):
        ins, outs = refs[:n], refs[n:2 * n]
        send_sems, recv_sems, loc_sems = refs[2 * n:]
        me = _mesh_pos()
        mq = 2 * me[0] + me[1]
        peers = [_flip(me, 4), _flip(me, 2), _flip(me, 6)]

        def copy(w, k):
            peer = peers[k]
            return pltpu.make_async_remote_copy(
                src_ref=ins[w].at[2 * peer[0] + peer[1]], dst_ref=outs[w].at[mq], send_sem=send_sems.at[w, k],
                recv_sem=recv_sems.at[w, k], device_id=peer, device_id_type=MESH)

        def arrival(w, k):
            peer = peers[k]
            return pltpu.make_async_remote_copy(
                src_ref=ins[w].at[mq], dst_ref=outs[w].at[2 * peer[0] + peer[1]], send_sem=send_sems.at[w, k],
                recv_sem=recv_sems.at[w, k], device_id=peer, device_id_type=MESH)

        local = [pltpu.make_async_copy(ins[w].at[mq], outs[w].at[mq], loc_sems.at[w, 0]) for w in range(n)]
        for cp in local:
            cp.start()
        sends = [copy(w, k) for k in range(3) for w in range(n)]
        for cp in sends:
            cp.start()
        for k in range(3):
            for w in range(n):
                arrival(w, k).wait_recv()
        for cp in sends:
            cp.wait_send()
        for cp in local:
            cp.wait()

    out_shape = [jax.ShapeDtypeStruct(a.shape, a.dtype) for a in arrays]
    return _comm_call(name, body, arrays, out_shape, 3)


def _chip_peers(me):
    return [_flip(me, 4), _flip(me, 2), _flip(me, 6)]


def _chips_start(name, arrays):
    n = len(arrays)
    ns = 3 * n

    def body(*refs):
        ins, land = refs[:n], refs[n:2 * n]
        send_sems, recv_sems = refs[2 * n:2 * n + ns], refs[2 * n + ns:2 * n + 2 * ns]
        token = refs[4 * n + 2 * ns]
        me = _mesh_pos()
        for w in range(n):
            for k, peer in enumerate(_chip_peers(me)):
                pltpu.make_async_remote_copy(
                    src_ref=ins[w].at[2 * peer[0] + peer[1]], dst_ref=land[w].at[k], send_sem=send_sems[3 * w + k],
                    recv_sem=recv_sems[3 * w + k], device_id=peer, device_id_type=MESH).start()
        token[...] = jnp.zeros_like(token)

    lands = [lax.empty((3,) + a.shape[1:], a.dtype) for a in arrays]
    sem = pltpu.SemaphoreType.DMA(())
    res = pl.pallas_call(
        body, name=name,
        out_shape=(*[sem] * (2 * ns), *[pltpu.HBM(a.shape, a.dtype) for a in arrays],
                   *[pltpu.HBM(a.shape, a.dtype) for a in lands], jax.ShapeDtypeStruct((8, 128), f32)),
        in_specs=[_HBM] * (2 * n),
        out_specs=(*[_SEM] * (2 * ns), *[_HBM] * (2 * n), pl.BlockSpec(memory_space=pltpu.VMEM)),
        input_output_aliases={i: 2 * ns + i for i in range(2 * n)},
        compiler_params=pltpu.CompilerParams(has_side_effects=pltpu.SideEffectType.DATAFLOW_SIDE_EFFECTING),
    )(*[pltpu.with_memory_space_constraint(a, pltpu.HBM) for a in list(arrays) + lands])
    sems, rest = res[:2 * ns], res[2 * ns:]
    return sems[:ns], sems[ns:], rest[:n], rest[n:2 * n], rest[2 * n]


def _chips_wait(name, arrays, lands, send_sems, recv_sems, after):
    n = len(arrays)
    ns = 3 * n

    def body(*refs):
        ins, land = refs[:n], refs[n:2 * n]
        s_sems, r_sems = refs[2 * n:2 * n + ns], refs[2 * n + ns:2 * n + 2 * ns]
        me = _mesh_pos()
        for w in range(n):
            for k, peer in enumerate(_chip_peers(me)):
                cp = pltpu.make_async_remote_copy(
                    src_ref=ins[w].at[2 * peer[0] + peer[1]], dst_ref=land[w].at[k], send_sem=s_sems[3 * w + k],
                    recv_sem=r_sems[3 * w + k], device_id=peer, device_id_type=MESH)
                cp.wait_send()
                cp.wait_recv()

    res = pl.pallas_call(
        body, name=name,
        out_shape=(*[pltpu.HBM(a.shape, a.dtype) for a in arrays], *[pltpu.HBM(a.shape, a.dtype) for a in lands]),
        in_specs=[_HBM] * (2 * n) + [_SEM] * (2 * ns) + [pl.BlockSpec(memory_space=pl.ANY)],
        out_specs=tuple([_HBM] * (2 * n)), input_output_aliases={i: i for i in range(2 * n)},
        compiler_params=pltpu.CompilerParams(has_side_effects=pltpu.SideEffectType.DATAFLOW_SIDE_EFFECTING),
    )(*arrays, *lands, *send_sems, *recv_sems, after)
    return res[:n], res[n:]


def _adamw_own(name, arrivals, sums, chip, w, m, v):
    R, C = w.shape
    tr = R if R <= 256 else (256 if C <= 512 else 128)
    bc1 = 1.0 - ADAM_B1 ** ADAM_STEP
    bc2 = 1.0 - ADAM_B2 ** ADAM_STEP

    def body(chip_ref, own_ref, p_ref, w_ref, m_ref, v_ref, g_ref, d_ref, m2_ref, v2_ref):
        g = own_ref[...].astype(f32)
        for k in range(3):
            g = g + p_ref[k].astype(f32)
        m2 = ADAM_B1 * m_ref[...] + (1.0 - ADAM_B1) * g
        v2 = ADAM_B2 * v_ref[...] + (1.0 - ADAM_B2) * jnp.square(g)
        g_ref[...] = g
        d_ref[...] = -ADAM_LR * ((m2 / bc1) / (jnp.sqrt(v2 / bc2) + ADAM_EPS) + ADAM_WD * w_ref[...])
        m2_ref[...] = m2
        v2_ref[...] = v2

    blk = pl.BlockSpec((tr, C), lambda r, cr: (r, 0))
    grid_spec = pltpu.PrefetchScalarGridSpec(
        num_scalar_prefetch=1, grid=(R // tr,),
        in_specs=[pl.BlockSpec((None, tr, C), lambda r, cr: (cr[0], r, 0)),
                  pl.BlockSpec((3, tr, C), lambda r, cr: (0, r, 0)), blk, blk, blk],
        out_specs=[blk] * 4)
    return pl.pallas_call(
        body, name=name, grid_spec=grid_spec, out_shape=[jax.ShapeDtypeStruct((R, C), f32)] * 4,
        compiler_params=_params(1),
    )(chip.reshape(1).astype(jnp.int32), sums, arrivals, w, m, v)


def _adamw(name, parts, w, m, v):
    R, C = w.shape
    n_parts = parts.shape[0]
    tr = R if R <= 256 else (256 if C <= 512 else 128)
    bc1 = 1.0 - ADAM_B1 ** ADAM_STEP
    bc2 = 1.0 - ADAM_B2 ** ADAM_STEP

    def f(p, wv, mv, vv):
        g = p[0].astype(f32)
        for i in range(1, n_parts):
            g = g + p[i].astype(f32)
        m2 = ADAM_B1 * mv + (1.0 - ADAM_B1) * g
        v2 = ADAM_B2 * vv + (1.0 - ADAM_B2) * jnp.square(g)
        delta = -ADAM_LR * ((m2 / bc1) / (jnp.sqrt(v2 / bc2) + ADAM_EPS) + ADAM_WD * wv)
        return g, delta, m2, v2

    blk = ((tr, C), lambda r, z: (r, 0))
    return _ew(name, f, [(parts, (n_parts, tr, C), lambda r, z: (0, r, 0)), (w, *blk), (m, *blk), (v, *blk)],
               [((R, C), f32, *blk, None)] * 4, (R // tr, 1))


_WEIGHTS = ["norm_mix_g", "norm_mlp_g", "mlp_w1", "mlp_w2", "gdn_w_in", "gdn_conv_w", "gdn_a_log", "gdn_dt_bias",
            "gdn_o_norm_g", "gdn_w_out", "s5_w_in", "s5_lam_re", "s5_lam_im", "s5_log_dt", "s5_b_re", "s5_b_im",
            "s5_c_re", "s5_c_im", "s5_d", "s5_w_out", "m2_w_in", "m2_conv_w", "m2_conv_b", "m2_dt_bias", "m2_a_log",
            "m2_d", "m2_norm_g", "m2_w_out", "final_norm_g"]
_SHARDED = ["mlp_w1", "mlp_w2", "gdn_w_in", "gdn_w_out", "s5_w_in", "s5_w_out", "m2_w_in", "m2_w_out",
            "gdn_conv_w", "m2_conv_w", "m2_conv_b", "m2_norm_g"]
_REPLICATED = [n for n in _WEIGHTS if n not in _SHARDED]
_GDN_IN, _M2_IN = 4112, 6176
_LAYER_KIND = (0, 1, 2, 0)


def _as2d(a):
    return a.reshape(-1, a.shape[-1])


def _cols_from_shards(g, width):
    return g.transpose(1, 0, 2).reshape(g.shape[1], width)


def _cols_to_shards(a, width):
    return a[:, :width].reshape(a.shape[0], N_DEV, width // N_DEV).transpose(1, 0, 2)


def kernel(x, norm_mix_g, norm_mlp_g, mlp_w1, mlp_w2, gdn_w_in, gdn_conv_w, gdn_a_log, gdn_dt_bias, gdn_o_norm_g, gdn_w_out, s5_w_in, s5_lam_re, s5_lam_im, s5_log_dt, s5_b_re, s5_b_im, s5_c_re, s5_c_im, s5_d, s5_w_out, m2_w_in, m2_conv_w, m2_conv_b, m2_dt_bias, m2_a_log, m2_d, m2_norm_g, m2_w_out, final_norm_g, loss_target, m_norm_mix_g, m_norm_mlp_g, m_mlp_w1, m_mlp_w2, m_gdn_w_in, m_gdn_conv_w, m_gdn_a_log, m_gdn_dt_bias, m_gdn_o_norm_g, m_gdn_w_out, m_s5_w_in, m_s5_lam_re, m_s5_lam_im, m_s5_log_dt, m_s5_b_re, m_s5_b_im, m_s5_c_re, m_s5_c_im, m_s5_d, m_s5_w_out, m_m2_w_in, m_m2_conv_w, m_m2_conv_b, m_m2_dt_bias, m_m2_a_log, m_m2_d, m_m2_norm_g, m_m2_w_out, m_final_norm_g, v_norm_mix_g, v_norm_mlp_g, v_mlp_w1, v_mlp_w2, v_gdn_w_in, v_gdn_conv_w, v_gdn_a_log, v_gdn_dt_bias, v_gdn_o_norm_g, v_gdn_w_out, v_s5_w_in, v_s5_lam_re, v_s5_lam_im, v_s5_log_dt, v_s5_b_re, v_s5_b_im, v_s5_c_re, v_s5_c_im, v_s5_d, v_s5_w_out, v_m2_w_in, v_m2_conv_w, v_m2_conv_b, v_m2_dt_bias, v_m2_a_log, v_m2_d, v_m2_norm_g, v_m2_w_out, v_final_norm_g):
    args = locals()
    W = {n: args[n] for n in _WEIGHTS}
    MOM = {n: args["m_" + n] for n in _WEIGHTS}
    VAR = {n: args["v_" + n] for n in _WEIGHTS}
    h = x[0]
    target = loss_target[0]
    L, D = h.shape

    first = [mlp_w1[0:1].astype(bf16), mlp_w2[0:1].astype(bf16), gdn_w_in[0:1].astype(bf16),
             gdn_w_out[0:1].astype(bf16), _as2d(gdn_conv_w), _as2d(m2_conv_w), _as2d(m2_conv_b), _as2d(m2_norm_g)]
    w1g0, w2g0, gin0, gout0, gconv, m2_cw, m2_cbg, m2_ngg = _gather("gather_first", first)
    stacked = jnp.concatenate([gdn_w_out[1], s5_w_in[0], m2_w_out[0]], axis=0).astype(bf16)
    rest = [mlp_w1[1:4].astype(bf16), mlp_w2[1:4].astype(bf16), gdn_w_in[1:2].astype(bf16), s5_w_out.astype(bf16),
            m2_w_in.astype(bf16), stacked]
    lands = [lax.empty((N_DEV,) + a.shape, a.dtype) for a in rest]
    send_sems, recv_sems, rest_thru, lands_thru, token = _gather_start("gather_rest_start", rest, lands)

    def gdn_weights(gin, gout, conv, j):
        return (jnp.pad(_cols_from_shards(gin[:, 0], _GDN_IN), ((0, 0), (0, GDN_EXT - _GDN_IN))),
                gout[:, 0].reshape(D, D), _cols_from_shards(conv[:, 4 * j:4 * j + 4], 3 * D))

    gdn_in, gdn_out, gdn_conv = [None, None], [None, None], [None, None]
    gdn_in[0], gdn_out[0], gdn_conv[0] = gdn_weights(gin0, gout0, gconv, 0)

    norm_mix = [norm_mix_g[i].reshape(1, D) for i in range(4)]
    norm_mix[0] = norm_mix[0] + token[0, 0]
    late = {}

    def mixer_fwd(i, hv):
        kind, j = _LAYER_KIND[i], i // 3
        gn = norm_mix[i]
        s5_in, s5_out_g = late.get("s5_in"), late.get("s5_out_g")
        m2_in, m2_conv, m2_cb, m2_ng, m2_out = (late.get(k) for k in ("m2_in", "m2_conv", "m2_cb", "m2_ng", "m2_out"))
        if kind == 0:
            return _gdn_fwd(hv, gn, gdn_in[j], gdn_conv[j], gdn_a_log[j], gdn_dt_bias[j], gdn_o_norm_g[j], gdn_out[j])
        if kind == 1:
            return _s5_fwd(hv, gn, s5_in, s5_lam_re[0], s5_lam_im[0], s5_log_dt[0], s5_b_re[0], s5_b_im[0],
                           s5_c_re[0], s5_c_im[0], s5_d[0], s5_out_g)
        return _m2_fwd(hv, gn, m2_in, m2_conv, m2_cb, m2_dt_bias[0], m2_a_log[0], m2_d[0], m2_ng, m2_out)

    def mixer_bwd(i, dh, hv, sv):
        kind, j = _LAYER_KIND[i], i // 3
        gn = norm_mix[i]
        s5_in, s5_out_g = late["s5_in"], late["s5_out_g"]
        m2_in, m2_conv, m2_cb, m2_out = (late[k] for k in ("m2_in", "m2_conv", "m2_cb", "m2_out"))
        if kind == 0:
            return _gdn_bwd(dh, hv, gn, gdn_in[j], gdn_conv[j], gdn_out[j], sv)
        if kind == 1:
            return _s5_bwd(dh, hv, gn, s5_in, s5_out_g, sv)
        return _m2_bwd(dh, hv, gn, m2_in, m2_conv, m2_cb, m2_out, sv)

    tape = []
    mlp_w = [(w1g0, w2g0, 0)]
    for i in range(4):
        if i == 1:
            landed = _gather_wait("gather_rest_wait", rest_thru, lands_thru, send_sems, recv_sems, h)
            w1gr, w2gr, gin1, s5_out_g, m2_in_g, rows_g = _gather_forward("gather_rest_forward", landed)
            gout1, s5_in_g, m2_out_g = rows_g[:, None, 0:128], rows_g[:, 128:256], rows_g[:, 256:512]
            mlp_w += [(w1gr, w2gr, k) for k in range(3)]
            gdn_in[1], gdn_out[1], gdn_conv[1] = gdn_weights(gin1, gout1, gconv, 1)
            late.update(
                s5_in=s5_in_g.reshape(D, D), s5_out_g=s5_out_g,
                m2_in=jnp.pad(_cols_from_shards(m2_in_g[:, 0], _M2_IN), ((0, 0), (0, M2_EXT - _M2_IN))),
                m2_out=m2_out_g.reshape(M2_INNER, D), m2_conv=_cols_from_shards(m2_cw, 2 * M2_INNER),
                m2_cb=_cols_from_shards(m2_cbg, 2 * M2_INNER), m2_ng=_cols_from_shards(m2_ngg, M2_INNER))
        h_mid, sv = mixer_fwd(i, h)
        h_next, hn, h1 = _mlp_fwd(h_mid, norm_mlp_g[i].reshape(1, D), *mlp_w[i])
        tape.append((h, sv, h_mid, hn, h1))
        h = h_next
    loss_row, dh, d_final = _loss_head(h, final_norm_g.reshape(1, D), target)
    loss = lax.psum(loss_row[0, 0], ("x", "y", "c"))

    fs = D_FF // N_DEV
    group_layers = (1, 1, 2)
    dw1 = [lax.empty((N_DEV, nl, D, fs), f32) for nl in group_layers]
    dw2 = [lax.empty((N_DEV, nl, fs, D), f32) for nl in group_layers]
    d_mix, d_mlp, mg = [None] * 4, [None] * 4, [None] * 4
    core = lax.axis_index("c")
    chip = 2 * lax.axis_index("x") + lax.axis_index("y")

    def pair_sums(tag, entries):
        theirs = _scatter_pair("scatter_pair_" + tag, [e[1] for e in entries])
        return [_pair_add("pair_add_%s_%s" % (tag, e[0]), e[1], th, core, bf16 if e[2] else f32)
                for e, th in zip(entries, theirs)]

    def mlp_grads(grp):
        nl = group_layers[grp]
        return [("mlp_w1", dw1[grp].reshape(N_DEV, nl * D, fs), True), ("mlp_w2", dw2[grp].reshape(N_DEV, nl * fs, D), True)]

    def gdn_grads(g):
        return [("gdn_w_in", _cols_to_shards(g["w_ext"], _GDN_IN), True),
                ("gdn_w_out", g["w_out"].reshape(N_DEV, D // N_DEV, D), True)]

    flying = {}
    for i in reversed(range(4)):
        h_in, sv, h_mid, hn, h1 = tape[i]
        grp, slot = min(i, 2), max(i - 2, 0)
        dh, d_mlp[i], dw1[grp], dw2[grp] = _mlp_bwd(dh, h_mid, norm_mlp_g[i].reshape(1, D), hn, h1, *mlp_w[i], slot,
                                                    dw1[grp], dw2[grp])
        dh, mg[i] = mixer_bwd(i, dh, h_in, sv)
        d_mix[i] = mg[i]["norm"]
        if i in (2, 1):
            if i == 2:
                entries = mlp_grads(2) + gdn_grads(mg[3]) + [
                    ("m2_w_in", _cols_to_shards(mg[2]["w_ext"], _M2_IN), True),
                    ("m2_w_out", mg[2]["w_out"].reshape(N_DEV, M2_INNER // N_DEV, D), True)]
            else:
                entries = mlp_grads(1) + [("s5_w_in", mg[1]["w_in"].reshape(N_DEV, D // N_DEV, D), True),
                                          ("s5_w_out", mg[1]["w_out"].reshape(N_DEV, D, 2 * D // N_DEV), True)]
            started = _chips_start("chips_start_%d" % grp, pair_sums("g%d" % grp, entries))
            flying[grp] = (entries, started)
            dh = dh + started[4][0, 0]
    grad_x = dh.reshape(1, L, D)
    ga, gb_, s5g, m2g = mg[0], mg[3], mg[1], mg[2]
    early = mlp_grads(0) + gdn_grads(ga) + [
        ("gdn_conv_w", jnp.concatenate([_cols_to_shards(g["conv_w"], 3 * D) for g in (ga, gb_)], axis=1), False),
        ("m2_conv_w", _cols_to_shards(m2g["conv_w"], 2 * M2_INNER), False),
        ("m2_conv_b", _cols_to_shards(m2g["conv_b"], 2 * M2_INNER), False),
        ("m2_norm_g", _cols_to_shards(m2g["norm_g"], M2_INNER), False)]
    early_parts = _scatter_chips("scatter_chips", pair_sums("g0", early))
    landed = {}
    for grp in (1, 2):
        entries, (s_sems, r_sems, sums, lands, _) = flying[grp]
        landed[grp] = (entries,) + tuple(_chips_wait("chips_wait_%d" % grp, sums, lands, s_sems, r_sems, early_parts[0]))

    rep = {
        "norm_mix_g": jnp.concatenate(d_mix, axis=0), "norm_mlp_g": jnp.concatenate(d_mlp, axis=0),
        "gdn_a_log": jnp.stack([ga["a_log"], gb_["a_log"]]), "gdn_dt_bias": jnp.stack([ga["dt_bias"], gb_["dt_bias"]]),
        "gdn_o_norm_g": jnp.stack([ga["o_norm_g"], gb_["o_norm_g"]]),
        "s5_lam_re": s5g["lam_re"], "s5_lam_im": s5g["lam_im"], "s5_log_dt": s5g["log_dt"], "s5_b_re": s5g["b_re"],
        "s5_b_im": s5g["b_im"], "s5_c_re": s5g["c_re"], "s5_c_im": s5g["c_im"], "s5_d": s5g["d"],
        "m2_dt_bias": m2g["dt_bias"], "m2_a_log": m2g["a_log"], "m2_d": m2g["d"], "final_norm_g": d_final,
    }

    def pack(d):
        flat = jnp.concatenate([d[n].reshape(-1).astype(f32) for n in _REPLICATED])
        return jnp.pad(flat, (0, -flat.shape[0] % (256 * 128))).reshape(-1, 128)

    (rep_parts,) = _gather("gather_small_grads", [pack(rep)])

    res = {}
    owned = {"mlp_w1": {0: (0, 1), 1: (1, 2), 2: (2, 4)}, "mlp_w2": {0: (0, 1), 1: (1, 2), 2: (2, 4)},
             "gdn_w_in": {0: (0, 1), 2: (1, 2)}, "gdn_w_out": {0: (0, 1), 2: (1, 2)}}

    def shard_rows(a, label, grp):
        lo_hi = owned.get(label)
        return _as2d(a if lo_hi is None else a[lo_hi[grp][0]:lo_hi[grp][1]])

    done = {}
    for (label, _, _), p in zip(early, early_parts):
        done[label] = [_adamw("adamw_" + label, p, *[shard_rows(t[label], label, 0) for t in (W, MOM, VAR)])]
    for grp in (1, 2):
        entries, sums, arrivals = landed[grp]
        for (label, _, _), arr, sm in zip(entries, arrivals, sums):
            done.setdefault(label, []).append(_adamw_own(
                "adamw_g%d_%s" % (grp, label), arr, sm, chip, *[shard_rows(t[label], label, grp) for t in (W, MOM, VAR)]))
    for n in _SHARDED:
        res[n] = [jnp.concatenate([g[k] for g in done[n]], axis=0).reshape(W[n].shape) for k in range(4)]
    out = _adamw("adamw_replicated", rep_parts, pack(W), pack(MOM), pack(VAR))
    off = 0
    for n in _REPLICATED:
        size = W[n].size
        res[n] = [o.reshape(-1)[off:off + size].reshape(W[n].shape) for o in out]
        off += size

    return (loss, grad_x, *[res[n][0] for n in _WEIGHTS], *[res[n][1] for n in _WEIGHTS],
            *[res[n][2] for n in _WEIGHTS], *[res[n][3] for n in _WEIGHTS])
```
